```python
import math
import jax, jax.numpy as jnp
from jax import lax
import numpy as np

D_MODEL = 2048
BATCH = 8
SEQ = 4096
DEPTH = 1

D_MIX = D_MODEL
POOL_WIDTH = D_MIX // 2
SSM_WIDTH = D_MIX - POOL_WIDTH
POOL_WINDOWS = (2, 4, 8, 16)
N_POOL_GROUPS = len(POOL_WINDOWS)
POOL_GROUP = POOL_WIDTH // N_POOL_GROUPS
SSM_GROUP = 16
N_SSM_GROUPS = SSM_WIDTH // SSM_GROUP
SSM_STATE = 64
PLE_DIM = 256
EPS = 1e-6
DT_MIN = 1e-3
DT_MAX = 1e-1
A_RE_MAX = -1e-4

kernel_name = "hybrid_pool_s5_parallel_heads"


def rmsnorm(x, gain):
    x32 = x.astype(jnp.float32)
    y = x32 * lax.rsqrt(jnp.mean(x32 * x32, axis=-1, keepdims=True) + EPS)
    return (y * gain.astype(jnp.float32)).astype(x.dtype)


def pool_mixer(u, w_pool, pool_scale):
    B, L, _ = u.shape
    ug = u.astype(jnp.float32).reshape(B, L, N_POOL_GROUPS, POOL_GROUP)
    t = jnp.arange(L)
    outs = []
    for g, w in enumerate(POOL_WINDOWS):
        v = ug[:, :, g]
        cs = jnp.cumsum(v, axis=1)
        lagged = jnp.pad(cs, ((0, 0), (w, 0), (0, 0)))[:, :L]
        count = jnp.minimum(t + 1, w).astype(jnp.float32)[None, :, None]
        outs.append((cs - lagged) / count - v)
    pooled = jnp.stack(outs, axis=2)
    mixed = jnp.einsum('blgc,gcd->blgd', pooled, w_pool.astype(jnp.float32))
    out = mixed.reshape(B, L, POOL_WIDTH) * pool_scale.astype(jnp.float32)
    return out.astype(u.dtype)


def _scan_combine(e1, e2):
    ar1, ai1, br1, bi1 = e1
    ar2, ai2, br2, bi2 = e2
    ar = ar2 * ar1 - ai2 * ai1
    ai = ar2 * ai1 + ai2 * ar1
    br = ar2 * br1 - ai2 * bi1 + br2
    bi = ar2 * bi1 + ai2 * br1 + bi2
    return (ar, ai, br, bi)


def ssm_mixer(u, a_re, a_im, log_dt, b_re, b_im, c_re, c_im, d_skip, w_glu):
    B, L, _ = u.shape
    f32 = jnp.float32
    u32 = u.astype(f32).reshape(B, L, N_SSM_GROUPS, SSM_GROUP)
    lam_re = jnp.minimum(a_re.astype(f32), A_RE_MAX)
    lam_im = a_im.astype(f32)
    dt = jnp.exp(log_dt.astype(f32))[:, None]
    mag = jnp.exp(lam_re * dt)
    ang = lam_im * dt
    ab_re = mag * jnp.cos(ang)
    ab_im = mag * jnp.sin(ang)
    den = lam_re * lam_re + lam_im * lam_im
    n_re = ab_re - 1.0
    n_im = ab_im
    q_re = (n_re * lam_re + n_im * lam_im) / den
    q_im = (n_im * lam_re - n_re * lam_im) / den
    br = b_re.astype(f32)
    bi = b_im.astype(f32)
    bb_re = q_re[..., None] * br - q_im[..., None] * bi
    bb_im = q_re[..., None] * bi + q_im[..., None] * br
    bu_re = jnp.einsum('blgc,gnc->blgn', u32, bb_re)
    bu_im = jnp.einsum('blgc,gnc->blgn', u32, bb_im)
    shp = (1, L, N_SSM_GROUPS, SSM_STATE)
    a_re_t = jnp.broadcast_to(ab_re[None, None], shp)
    a_im_t = jnp.broadcast_to(ab_im[None, None], shp)
    _, _, s_re, s_im = lax.associative_scan(
        _scan_combine, (a_re_t, a_im_t, bu_re, bu_im), axis=1)
    y = (jnp.einsum('blgn,gcn->blgc', s_re, c_re.astype(f32))
         - jnp.einsum('blgn,gcn->blgc', s_im, c_im.astype(f32))
         + d_skip.astype(f32).reshape(N_SSM_GROUPS, SSM_GROUP) * u32)
    y = y.reshape(B, L, SSM_WIDTH)
    g = jax.nn.gelu(y)
    hg = g @ w_glu.astype(f32)
    out = hg[..., :SSM_WIDTH] * jax.nn.sigmoid(hg[..., SSM_WIDTH:])
    return out.astype(u.dtype)


def _fwd_setup_inputs(seed: int = 0) -> dict:
    key = jax.random.key(seed)
    ks = jax.random.split(key, 20)
    f32 = jnp.float32
    n = lambda k, shape, s: jax.random.normal(k, shape, f32) * s
    x = jax.random.normal(ks[0], (BATCH, SEQ, D_MODEL), f32)
    p = jax.random.normal(ks[1], (DEPTH, BATCH, SEQ, PLE_DIM), f32)
    norm_gain = 1.0 + n(ks[2], (DEPTH, D_MODEL), 0.02)
    w_in = n(ks[3], (DEPTH, D_MODEL, 2 * D_MIX), D_MODEL ** -0.5)
    w_pool = n(ks[4], (DEPTH, N_POOL_GROUPS, POOL_GROUP, POOL_GROUP), POOL_GROUP ** -0.5)
    pool_scale = 1.0 + n(ks[5], (DEPTH, POOL_WIDTH), 0.02)
    a_re = -0.5 + n(ks[6], (DEPTH, N_SSM_GROUPS, SSM_STATE), 0.01)
    a_im = (math.pi * jnp.arange(SSM_STATE, dtype=f32))[None, None, :] + n(
        ks[7], (DEPTH, N_SSM_GROUPS, SSM_STATE), 0.01)
    log_dt = jax.random.uniform(ks[8], (DEPTH, N_SSM_GROUPS), f32,
                                math.log(DT_MIN), math.log(DT_MAX))
    b_scale = (2.0 * SSM_GROUP) ** -0.5
    b_re = n(ks[9], (DEPTH, N_SSM_GROUPS, SSM_STATE, SSM_GROUP), b_scale)
    b_im = n(ks[10], (DEPTH, N_SSM_GROUPS, SSM_STATE, SSM_GROUP), b_scale)
    c_scale = SSM_STATE ** -0.5
    c_re = n(ks[11], (DEPTH, N_SSM_GROUPS, SSM_GROUP, SSM_STATE), c_scale)
    c_im = n(ks[12], (DEPTH, N_SSM_GROUPS, SSM_GROUP, SSM_STATE), c_scale)
    d_skip = n(ks[13], (DEPTH, SSM_WIDTH), 1.0)
    w_glu = n(ks[14], (DEPTH, SSM_WIDTH, 2 * SSM_WIDTH), SSM_WIDTH ** -0.5)
    w_out = n(ks[15], (DEPTH, D_MIX, D_MODEL), D_MIX ** -0.5)
    w_ple = n(ks[16], (DEPTH, PLE_DIM, D_MODEL), PLE_DIM ** -0.5)
    w_ple_gate = n(ks[17], (DEPTH, D_MODEL, D_MODEL), D_MODEL ** -0.5)
    final_gain = 1.0 + n(ks[18], (D_MODEL,), 0.02)
    return {"x": x, "p": p, "norm_gain": norm_gain, "w_in": w_in, "w_pool": w_pool,
            "pool_scale": pool_scale, "a_re": a_re, "a_im": a_im, "log_dt": log_dt,
            "b_re": b_re, "b_im": b_im, "c_re": c_re, "c_im": c_im, "d_skip": d_skip,
            "w_glu": w_glu, "w_out": w_out, "w_ple": w_ple, "w_ple_gate": w_ple_gate,
            "final_gain": final_gain}


def _fwd_reference(x, p, norm_gain, w_in, w_pool, pool_scale, a_re, a_im, log_dt, b_re, b_im,
              c_re, c_im, d_skip, w_glu, w_out, w_ple, w_ple_gate, final_gain):
    h = x
    for i in range(DEPTH):
        hn = rmsnorm(h, norm_gain[i])
        proj = hn @ w_in[i]
        pool_in = proj[..., :POOL_WIDTH]
        pool_gate = proj[..., POOL_WIDTH:2 * POOL_WIDTH]
        ssm_in = proj[..., 2 * POOL_WIDTH:2 * POOL_WIDTH + SSM_WIDTH]
        ssm_gate = proj[..., 2 * POOL_WIDTH + SSM_WIDTH:]
        ya = pool_mixer(pool_in, w_pool[i], pool_scale[i]) * jax.nn.silu(pool_gate)
        yb = ssm_mixer(ssm_in, a_re[i], a_im[i], log_dt[i], b_re[i], b_im[i],
                       c_re[i], c_im[i], d_skip[i], w_glu[i]) * jax.nn.silu(ssm_gate)
        h = h + jnp.concatenate([ya, yb], axis=-1) @ w_out[i]
        h = h + (p[i] @ w_ple[i]) * jax.nn.sigmoid(h @ w_ple_gate[i])
    return rmsnorm(h, final_gain)


import jax as _jax
import jax.numpy as _jnp

TWIN_FORMAT = 'train_step'
FWD_PARAMS = ['x', 'p', 'norm_gain', 'w_in', 'w_pool', 'pool_scale', 'a_re', 'a_im', 'log_dt', 'b_re', 'b_im', 'c_re', 'c_im', 'd_skip', 'w_glu', 'w_out', 'w_ple', 'w_ple_gate', 'final_gain']
TWIN_WEIGHTS = ['norm_gain', 'w_in', 'w_pool', 'pool_scale', 'a_re', 'a_im', 'log_dt', 'b_re', 'b_im', 'c_re', 'c_im', 'd_skip', 'w_glu', 'w_out', 'w_ple', 'w_ple_gate', 'final_gain']
TWIN_DIFF_INPUT = 'x'
TWIN_INPUTS = ['x', 'p', 'norm_gain', 'w_in', 'w_pool', 'pool_scale', 'a_re', 'a_im', 'log_dt', 'b_re', 'b_im', 'c_re', 'c_im', 'd_skip', 'w_glu', 'w_out', 'w_ple', 'w_ple_gate', 'final_gain', 'loss_target', 'm_norm_gain', 'm_w_in', 'm_w_pool', 'm_pool_scale', 'm_a_re', 'm_a_im', 'm_log_dt', 'm_b_re', 'm_b_im', 'm_c_re', 'm_c_im', 'm_d_skip', 'm_w_glu', 'm_w_out', 'm_w_ple', 'm_w_ple_gate', 'm_final_gain', 'v_norm_gain', 'v_w_in', 'v_w_pool', 'v_pool_scale', 'v_a_re', 'v_a_im', 'v_log_dt', 'v_b_re', 'v_b_im', 'v_c_re', 'v_c_im', 'v_d_skip', 'v_w_glu', 'v_w_out', 'v_w_ple', 'v_w_ple_gate', 'v_final_gain']
TWIN_OUTPUTS = ['loss', 'grad_x', 'grad_norm_gain', 'grad_w_in', 'grad_w_pool', 'grad_pool_scale', 'grad_a_re', 'grad_a_im', 'grad_log_dt', 'grad_b_re', 'grad_b_im', 'grad_c_re', 'grad_c_im', 'grad_d_skip', 'grad_w_glu', 'grad_w_out', 'grad_w_ple', 'grad_w_ple_gate', 'grad_final_gain', 'delta_norm_gain', 'delta_w_in', 'delta_w_pool', 'delta_pool_scale', 'delta_a_re', 'delta_a_im', 'delta_log_dt', 'delta_b_re', 'delta_b_im', 'delta_c_re', 'delta_c_im', 'delta_d_skip', 'delta_w_glu', 'delta_w_out', 'delta_w_ple', 'delta_w_ple_gate', 'delta_final_gain', 'new_m_norm_gain', 'new_m_w_in', 'new_m_w_pool', 'new_m_pool_scale', 'new_m_a_re', 'new_m_a_im', 'new_m_log_dt', 'new_m_b_re', 'new_m_b_im', 'new_m_c_re', 'new_m_c_im', 'new_m_d_skip', 'new_m_w_glu', 'new_m_w_out', 'new_m_w_ple', 'new_m_w_ple_gate', 'new_m_final_gain', 'new_v_norm_gain', 'new_v_w_in', 'new_v_w_pool', 'new_v_pool_scale', 'new_v_a_re', 'new_v_a_im', 'new_v_log_dt', 'new_v_b_re', 'new_v_b_im', 'new_v_c_re', 'new_v_c_im', 'new_v_d_skip', 'new_v_w_glu', 'new_v_w_out', 'new_v_w_ple', 'new_v_w_ple_gate', 'new_v_final_gain']
TWIN_LEAF_KINDS = {'loss': 'loss', 'grad_x': 'grad_x', 'grad_norm_gain': 'grad_w', 'grad_w_in': 'grad_w', 'grad_w_pool': 'grad_w', 'grad_pool_scale': 'grad_w', 'grad_a_re': 'grad_w', 'grad_a_im': 'grad_w', 'grad_log_dt': 'grad_w', 'grad_b_re': 'grad_w', 'grad_b_im': 'grad_w', 'grad_c_re': 'grad_w', 'grad_c_im': 'grad_w', 'grad_d_skip': 'grad_w', 'grad_w_glu': 'grad_w', 'grad_w_out': 'grad_w', 'grad_w_ple': 'grad_w', 'grad_w_ple_gate': 'grad_w', 'grad_final_gain': 'grad_w', 'delta_norm_gain': 'delta_w', 'delta_w_in': 'delta_w', 'delta_w_pool': 'delta_w', 'delta_pool_scale': 'delta_w', 'delta_a_re': 'delta_w', 'delta_a_im': 'delta_w', 'delta_log_dt': 'delta_w', 'delta_b_re': 'delta_w', 'delta_b_im': 'delta_w', 'delta_c_re': 'delta_w', 'delta_c_im': 'delta_w', 'delta_d_skip': 'delta_w', 'delta_w_glu': 'delta_w', 'delta_w_out': 'delta_w', 'delta_w_ple': 'delta_w', 'delta_w_ple_gate': 'delta_w', 'delta_final_gain': 'delta_w', 'new_m_norm_gain': 'new_m', 'new_m_w_in': 'new_m', 'new_m_w_pool': 'new_m', 'new_m_pool_scale': 'new_m', 'new_m_a_re': 'new_m', 'new_m_a_im': 'new_m', 'new_m_log_dt': 'new_m', 'new_m_b_re': 'new_m', 'new_m_b_im': 'new_m', 'new_m_c_re': 'new_m', 'new_m_c_im': 'new_m', 'new_m_d_skip': 'new_m', 'new_m_w_glu': 'new_m', 'new_m_w_out': 'new_m', 'new_m_w_ple': 'new_m', 'new_m_w_ple_gate': 'new_m', 'new_m_final_gain': 'new_m', 'new_v_norm_gain': 'new_v', 'new_v_w_in': 'new_v', 'new_v_w_pool': 'new_v', 'new_v_pool_scale': 'new_v', 'new_v_a_re': 'new_v', 'new_v_a_im': 'new_v', 'new_v_log_dt': 'new_v', 'new_v_b_re': 'new_v', 'new_v_b_im': 'new_v', 'new_v_c_re': 'new_v', 'new_v_c_im': 'new_v', 'new_v_d_skip': 'new_v', 'new_v_w_glu': 'new_v', 'new_v_w_out': 'new_v', 'new_v_w_ple': 'new_v', 'new_v_w_ple_gate': 'new_v', 'new_v_final_gain': 'new_v'}


def _forward(args):
    return _fwd_reference(*[args[k] for k in FWD_PARAMS])


def _output_shape():
    def fwd():
        inp = _fwd_setup_inputs(0)
        return _fwd_reference(*[inp[k] for k in FWD_PARAMS])
    out = _jax.eval_shape(fwd)
    return out.shape, out.dtype

N_MICROBATCH = 1
ADAM_LR = 0.001
ADAM_B1 = 0.9
ADAM_B2 = 0.999
ADAM_EPS = 1e-08
ADAM_WD = 0.01
ADAM_STEP = 10
PER_EXAMPLE_BATCH_AXIS = {'x': 0, 'p': 1, 'loss_target': 0}
SHARED_INPUTS = []
_WEIGHT_DTYPES = {'norm_gain': _jnp.float32, 'w_in': _jnp.float32, 'w_pool': _jnp.float32, 'pool_scale': _jnp.float32, 'a_re': _jnp.float32, 'a_im': _jnp.float32, 'log_dt': _jnp.float32, 'b_re': _jnp.float32, 'b_im': _jnp.float32, 'c_re': _jnp.float32, 'c_im': _jnp.float32, 'd_skip': _jnp.float32, 'w_glu': _jnp.float32, 'w_out': _jnp.float32, 'w_ple': _jnp.float32, 'w_ple_gate': _jnp.float32, 'final_gain': _jnp.float32}
MOMENT_SCALE = {'norm_gain': 4.261546e-02, 'w_in': 3.059915e-02, 'w_pool': 3.933612e-02, 'pool_scale': 3.943506e-02, 'a_re': 1.155384e-03, 'a_im': 1.107312e-03, 'log_dt': 7.160062e-01, 'b_re': 7.488370e-04, 'b_im': 7.554087e-04, 'c_re': 1.069225e-03, 'c_im': 1.076094e-03, 'd_skip': 1.720805e-02, 'w_glu': 1.188120e-02, 'w_out': 3.001920e-02, 'w_ple': 4.042702e-02, 'w_ple_gate': 1.643543e-02, 'final_gain': 1.600459e+01}


def _to_microbatches(a, axis):
    t = _jnp.moveaxis(a, axis, 0)
    t = t.reshape((N_MICROBATCH, t.shape[0] // N_MICROBATCH) + t.shape[1:])
    return _jnp.moveaxis(t, 1, axis + 1)


def setup_inputs(seed: int = 0) -> dict:
    inp = _fwd_setup_inputs(seed)
    key = _jax.random.fold_in(_jax.random.key(seed), 7919)
    shape, _ = _output_shape()
    out = dict(inp)
    out["loss_target"] = _jax.random.normal(_jax.random.fold_in(key, 0), shape, _jnp.float32)
    for i, name in enumerate(TWIN_WEIGHTS):
        w = inp[name].astype(_jnp.float32)
        if MOMENT_SCALE is None:
            s = _jnp.sqrt(_jnp.mean(_jnp.square(w)) + 1e-30)
        else:
            s = MOMENT_SCALE[name]
        km, kv = _jax.random.split(_jax.random.fold_in(key, i + 1))
        out[name] = w
        out["m_" + name] = s * _jax.random.normal(km, w.shape, _jnp.float32)
        out["v_" + name] = (s * s) * _jax.random.uniform(kv, w.shape, _jnp.float32, 0.5, 1.5)
    if N_MICROBATCH > 1:
        for name, axis in PER_EXAMPLE_BATCH_AXIS.items():
            out[name] = _to_microbatches(out[name], axis)
    return {'x': out['x'], 'p': out['p'], 'norm_gain': out['norm_gain'], 'w_in': out['w_in'], 'w_pool': out['w_pool'], 'pool_scale': out['pool_scale'], 'a_re': out['a_re'], 'a_im': out['a_im'], 'log_dt': out['log_dt'], 'b_re': out['b_re'], 'b_im': out['b_im'], 'c_re': out['c_re'], 'c_im': out['c_im'], 'd_skip': out['d_skip'], 'w_glu': out['w_glu'], 'w_out': out['w_out'], 'w_ple': out['w_ple'], 'w_ple_gate': out['w_ple_gate'], 'final_gain': out['final_gain'], 'loss_target': out['loss_target'], 'm_norm_gain': out['m_norm_gain'], 'm_w_in': out['m_w_in'], 'm_w_pool': out['m_w_pool'], 'm_pool_scale': out['m_pool_scale'], 'm_a_re': out['m_a_re'], 'm_a_im': out['m_a_im'], 'm_log_dt': out['m_log_dt'], 'm_b_re': out['m_b_re'], 'm_b_im': out['m_b_im'], 'm_c_re': out['m_c_re'], 'm_c_im': out['m_c_im'], 'm_d_skip': out['m_d_skip'], 'm_w_glu': out['m_w_glu'], 'm_w_out': out['m_w_out'], 'm_w_ple': out['m_w_ple'], 'm_w_ple_gate': out['m_w_ple_gate'], 'm_final_gain': out['m_final_gain'], 'v_norm_gain': out['v_norm_gain'], 'v_w_in': out['v_w_in'], 'v_w_pool': out['v_w_pool'], 'v_pool_scale': out['v_pool_scale'], 'v_a_re': out['v_a_re'], 'v_a_im': out['v_a_im'], 'v_log_dt': out['v_log_dt'], 'v_b_re': out['v_b_re'], 'v_b_im': out['v_b_im'], 'v_c_re': out['v_c_re'], 'v_c_im': out['v_c_im'], 'v_d_skip': out['v_d_skip'], 'v_w_glu': out['v_w_glu'], 'v_w_out': out['v_w_out'], 'v_w_ple': out['v_w_ple'], 'v_w_ple_gate': out['v_w_ple_gate'], 'v_final_gain': out['v_final_gain']}


def _loss(weights, diff, rest, loss_target):
    with _jax.named_scope("forward"):
        args = {**rest, TWIN_DIFF_INPUT: diff, **{k: w.astype(_WEIGHT_DTYPES[k]) for k, w in weights.items()}}
        y = _forward(args)
    with _jax.named_scope("loss_head"):
        err = _jnp.square(y.astype(_jnp.float32) - loss_target)
        return 0.5 * _jnp.sum(_jnp.mean(err, axis=-1)) if err.ndim else 0.5 * err


def _adamw(w, g, m, v):
    m = ADAM_B1 * m + (1.0 - ADAM_B1) * g
    v = ADAM_B2 * v + (1.0 - ADAM_B2) * _jnp.square(g)
    m_hat = m / (1.0 - ADAM_B1 ** ADAM_STEP)
    v_hat = v / (1.0 - ADAM_B2 ** ADAM_STEP)
    delta = -ADAM_LR * (m_hat / (_jnp.sqrt(v_hat) + ADAM_EPS) + ADAM_WD * w)
    return delta, m, v


def reference(x, p, norm_gain, w_in, w_pool, pool_scale, a_re, a_im, log_dt, b_re, b_im, c_re, c_im, d_skip, w_glu, w_out, w_ple, w_ple_gate, final_gain, loss_target, m_norm_gain, m_w_in, m_w_pool, m_pool_scale, m_a_re, m_a_im, m_log_dt, m_b_re, m_b_im, m_c_re, m_c_im, m_d_skip, m_w_glu, m_w_out, m_w_ple, m_w_ple_gate, m_final_gain, v_norm_gain, v_w_in, v_w_pool, v_pool_scale, v_a_re, v_a_im, v_log_dt, v_b_re, v_b_im, v_c_re, v_c_im, v_d_skip, v_w_glu, v_w_out, v_w_ple, v_w_ple_gate, v_final_gain):
    given = dict(x=x, p=p, norm_gain=norm_gain, w_in=w_in, w_pool=w_pool, pool_scale=pool_scale, a_re=a_re, a_im=a_im, log_dt=log_dt, b_re=b_re, b_im=b_im, c_re=c_re, c_im=c_im, d_skip=d_skip, w_glu=w_glu, w_out=w_out, w_ple=w_ple, w_ple_gate=w_ple_gate, final_gain=final_gain, loss_target=loss_target, m_norm_gain=m_norm_gain, m_w_in=m_w_in, m_w_pool=m_w_pool, m_pool_scale=m_pool_scale, m_a_re=m_a_re, m_a_im=m_a_im, m_log_dt=m_log_dt, m_b_re=m_b_re, m_b_im=m_b_im, m_c_re=m_c_re, m_c_im=m_c_im, m_d_skip=m_d_skip, m_w_glu=m_w_glu, m_w_out=m_w_out, m_w_ple=m_w_ple, m_w_ple_gate=m_w_ple_gate, m_final_gain=m_final_gain, v_norm_gain=v_norm_gain, v_w_in=v_w_in, v_w_pool=v_w_pool, v_pool_scale=v_pool_scale, v_a_re=v_a_re, v_a_im=v_a_im, v_log_dt=v_log_dt, v_b_re=v_b_re, v_b_im=v_b_im, v_c_re=v_c_re, v_c_im=v_c_im, v_d_skip=v_d_skip, v_w_glu=v_w_glu, v_w_out=v_w_out, v_w_ple=v_w_ple, v_w_ple_gate=v_w_ple_gate, v_final_gain=v_final_gain)
    weights = {n: given[n] for n in TWIN_WEIGHTS}
    shared = {n: given[n] for n in SHARED_INPUTS}
    per_example = {n: given[n] for n in ['x', 'p']}
    grad_fn = _jax.value_and_grad(_loss, argnums=(0, 1))

    def one_microbatch(ex, loss_target):
        ex = dict(ex)
        diff = ex.pop(TWIN_DIFF_INPUT)
        return grad_fn(weights, diff, {**shared, **ex}, loss_target)

    if N_MICROBATCH == 1:
        loss, (grad_w, grad_x) = one_microbatch(per_example, given["loss_target"])
    else:
        def body(carry, xs):
            loss_sum, grad_sum = carry
            l_k, (gw_k, gx_k) = one_microbatch(xs[0], xs[1])
            with _jax.named_scope("update"):
                return (loss_sum + l_k, _jax.tree.map(_jnp.add, grad_sum, gw_k)), gx_k

        init = (_jnp.zeros((), _jnp.float32), _jax.tree.map(_jnp.zeros_like, weights))
        (loss, grad_w), grad_x = _jax.lax.scan(body, init, (per_example, given["loss_target"]))
    with _jax.named_scope("update"):
        delta_w, new_m, new_v = {}, {}, {}
        for n in TWIN_WEIGHTS:
            delta_w[n], new_m[n], new_v[n] = _adamw(weights[n], grad_w[n], given["m_" + n], given["v_" + n])
    return (loss, grad_x, *[grad_w[n] for n in TWIN_WEIGHTS], *[delta_w[n] for n in TWIN_WEIGHTS],
            *[new_m[n] for n in TWIN_WEIGHTS], *[new_v[n] for n in TWIN_WEIGHTS])
```

```python
import functools

import jax
import jax.numpy as jnp
from jax import lax
from jax.experimental import pallas as pl
from jax.experimental.pallas import tpu as pltpu

F32, BF16 = jnp.float32, jnp.bfloat16
MESH = pl.DeviceIdType.MESH
ANY = pl.BlockSpec(memory_space=pl.ANY)
VMEM_FULL = pl.BlockSpec(memory_space=pltpu.VMEM)

EPS = 1e-6
A_RE_MAX = -1e-4
SSM_GROUP = 16
SSM_STATE = 64
POOL_WINDOWS = (2, 4, 8, 16)
POOL_HALO = 16
ADAM_LR, ADAM_B1, ADAM_B2, ADAM_EPS, ADAM_WD, ADAM_STEP = 0.001, 0.9, 0.999, 1e-08, 0.01, 10

V7X_VMEM_BYTES = 64 * 1024 * 1024
VMEM_LIMIT = V7X_VMEM_BYTES - 8 * 1024 * 1024
SUBLANES, LANES = 8, 128
SSM_TILE_GROUPS = 16
SCAN_LANES = 512
N_DEV, N_CHIP = 8, 4


def _t(n, pref):
    return pref if n % pref == 0 else n


def _cp(sem=None, vmem=VMEM_LIMIT):
    return pltpu.CompilerParams(dimension_semantics=sem, vmem_limit_bytes=vmem)


def _call(body, **kw):
    return pl.pallas_call(body, **kw)


NN = ((1,), (0,))
NT = ((1,), (1,))
TN = ((0,), (0,))


def _mm(a, b, *, dims, grid, a_spec, b_spec, o_spec, out_shape, name, res=None, r_spec=None):
    nk, kax = grid[-1], len(grid) - 1
    acc_shape = tuple(d for d in o_spec.block_shape if d is not None)

    def body(*refs):
        if res is None:
            a_ref, b_ref, o_ref, acc = refs
            r_ref = None
        else:
            a_ref, b_ref, r_ref, o_ref, acc = refs
        k = pl.program_id(kax)

        @pl.when(k == 0)
        def _():
            acc[...] = jnp.zeros_like(acc)

        acc[...] += lax.dot_general(a_ref[...].astype(BF16), b_ref[...].astype(BF16),
                                    (dims, ((), ())), preferred_element_type=F32)

        @pl.when(k == nk - 1)
        def _():
            r = acc[...]
            if r_ref is not None:
                r = r + r_ref[...]
            o_ref[...] = r.astype(o_ref.dtype)

    ins, specs = [a, b], [a_spec, b_spec]
    if res is not None:
        ins.append(res)
        specs.append(r_spec)
    sem = ("parallel",) * kax + ("arbitrary",)
    return _call(body, grid=grid, in_specs=specs, out_specs=o_spec, out_shape=out_shape,
                 scratch_shapes=[pltpu.VMEM(acc_shape, F32)], compiler_params=_cp(sem), name=name)(*ins)


def _bs(shape, fn):
    return pl.BlockSpec(shape, fn)


def _sigmoid(v):
    return 1.0 / (1.0 + jnp.exp(-v))


def _gelu(v):
    return 0.5 * v * (1.0 + jnp.tanh(0.7978845608028654 * (v + 0.044715 * v * v * v)))


def _gelu_grad(v):
    t = jnp.tanh(0.7978845608028654 * (v + 0.044715 * v * v * v))
    return 0.5 * (1.0 + t) + 0.5 * v * (1.0 - t * t) * 0.7978845608028654 * (1.0 + 3 * 0.044715 * v * v)


def _norm1(x, g1, tb):
    T, D = x.shape

    def body(x_ref, g_ref, o_ref):
        xv = x_ref[...]
        r = lax.rsqrt(jnp.mean(xv * xv, axis=-1, keepdims=True) + EPS)
        o_ref[...] = ((xv * r) * g_ref[...]).astype(BF16)

    return _call(body, grid=(T // tb,),
                 in_specs=[_bs((tb, D), lambda i: (i, 0)), _bs((1, D), lambda i: (0, 0))],
                 out_specs=_bs((tb, D), lambda i: (i, 0)), out_shape=jax.ShapeDtypeStruct((T, D), BF16),
                 compiler_params=_cp(("parallel",)), name="norm1")(x, g1)


def _norm1_bwd(x, dhn, dh1, g1, tb):
    T, D = x.shape

    def body(x_ref, dhn_ref, dh1_ref, g_ref, dx_ref, dg_ref):
        @pl.when(pl.program_id(0) == 0)
        def _():
            dg_ref[...] = jnp.zeros_like(dg_ref)

        xv = x_ref[...]
        r = lax.rsqrt(jnp.mean(xv * xv, axis=-1, keepdims=True) + EPS)
        xh = xv * r
        dhn_v = dhn_ref[...]
        dg_ref[...] += jnp.sum(dhn_v * xh, axis=0, keepdims=True)
        dxh = dhn_v * g_ref[...]
        dx_ref[...] = dh1_ref[...] + r * (dxh - xh * jnp.mean(dxh * xh, axis=-1, keepdims=True))

    row = _bs((tb, D), lambda i: (i, 0))
    vec = _bs((1, D), lambda i: (0, 0))
    return _call(body, grid=(T // tb,), in_specs=[row, row, row, vec], out_specs=[row, vec],
                 out_shape=[jax.ShapeDtypeStruct((T, D), F32), jax.ShapeDtypeStruct((1, D), F32)],
                 compiler_params=_cp(("arbitrary",)), name="norm1_bwd")(x, dhn, dh1, g1)


def _gate_fwd(mixed, proj, hg, ps, tb):
    T, P = mixed.shape

    def body(mx_ref, pg_ref, sg_ref, hg_ref, ps_ref, o_ref):
        pg, sg = pg_ref[...], sg_ref[...]
        ya = (mx_ref[...] * ps_ref[...]) * (pg * _sigmoid(pg))
        hgv = hg_ref[...]
        o = hgv[:, :P] * _sigmoid(hgv[:, P:])
        yb = o * (sg * _sigmoid(sg))
        o_ref[:, :P] = ya.astype(BF16)
        o_ref[:, P:] = yb.astype(BF16)

    return _call(body, grid=(T // tb,),
                 in_specs=[_bs((tb, P), lambda i: (i, 0)), _bs((tb, P), lambda i: (i, 1)),
                           _bs((tb, P), lambda i: (i, 3)), _bs((tb, 2 * P), lambda i: (i, 0)),
                           _bs((1, P), lambda i: (0, 0))],
                 out_specs=_bs((tb, 2 * P), lambda i: (i, 0)),
                 out_shape=jax.ShapeDtypeStruct((T, 2 * P), BF16),
                 compiler_params=_cp(("parallel",)), name="gate_fwd")(mixed, proj, proj, hg, ps)


def _gate_bwd(dcat, mixed, proj, hg, ps, tb):
    T, P = mixed.shape

    def body(dc_ref, mx_ref, pg_ref, sg_ref, hg_ref, ps_ref, dmx_ref, dpg_ref, dsg_ref, dhg_ref, dps_ref):
        @pl.when(pl.program_id(0) == 0)
        def _():
            dps_ref[...] = jnp.zeros_like(dps_ref)

        dc = dc_ref[...]
        dya, dyb = dc[:, :P], dc[:, P:]
        pg, sg, mx, psv = pg_ref[...], sg_ref[...], mx_ref[...], ps_ref[...]
        s_pg = _sigmoid(pg)
        dpa = dya * (pg * s_pg)
        dpg_ref[...] = (dya * (mx * psv) * (s_pg * (1.0 + pg * (1.0 - s_pg)))).astype(BF16)
        dps_ref[...] += jnp.sum(dpa * mx, axis=0, keepdims=True)
        dmx_ref[...] = (dpa * psv).astype(BF16)
        hgv = hg_ref[...]
        h1, s_h2 = hgv[:, :P], _sigmoid(hgv[:, P:])
        s_sg = _sigmoid(sg)
        do = dyb * (sg * s_sg)
        dsg_ref[...] = (dyb * (h1 * s_h2) * (s_sg * (1.0 + sg * (1.0 - s_sg)))).astype(BF16)
        dhg_ref[:, :P] = (do * s_h2).astype(BF16)
        dhg_ref[:, P:] = (do * h1 * s_h2 * (1.0 - s_h2)).astype(BF16)

    rowp = _bs((tb, P), lambda i: (i, 0))
    row2 = _bs((tb, 2 * P), lambda i: (i, 0))
    vec = _bs((1, P), lambda i: (0, 0))
    return _call(body, grid=(T // tb,),
                 in_specs=[row2, rowp, _bs((tb, P), lambda i: (i, 1)), _bs((tb, P), lambda i: (i, 3)), row2, vec],
                 out_specs=[rowp, rowp, rowp, row2, vec],
                 out_shape=[jax.ShapeDtypeStruct((T, P), BF16), jax.ShapeDtypeStruct((T, P), BF16),
                            jax.ShapeDtypeStruct((T, P), BF16), jax.ShapeDtypeStruct((T, 2 * P), BF16),
                            jax.ShapeDtypeStruct((1, P), F32)],
                 compiler_params=_cp(("arbitrary",)), name="gate_bwd")(dcat, mixed, proj, proj, hg, ps)


def _final_fb(h1, e, z, tgt, g2, tb):
    T, D = h1.shape

    def body(h1_ref, e_ref, z_ref, t_ref, g_ref, dh2_ref, de_ref, dz_ref, dg_ref, l_ref):
        @pl.when(pl.program_id(0) == 0)
        def _():
            dg_ref[...] = jnp.zeros_like(dg_ref)
            l_ref[...] = jnp.zeros_like(l_ref)

        ev = e_ref[...]
        s = _sigmoid(z_ref[...])
        h2 = h1_ref[...] + ev * s
        r = lax.rsqrt(jnp.mean(h2 * h2, axis=-1, keepdims=True) + EPS)
        xh = h2 * r
        gv = g_ref[...]
        diff = xh * gv - t_ref[...]
        l_ref[...] += 0.5 * jnp.sum(jnp.mean(diff * diff, axis=-1, keepdims=True))
        dout = diff * (1.0 / D)
        dg_ref[...] += jnp.sum(dout * xh, axis=0, keepdims=True)
        dxh = dout * gv
        dh2 = r * (dxh - xh * jnp.mean(dxh * xh, axis=-1, keepdims=True))
        dh2_ref[...] = dh2
        de_ref[...] = (dh2 * s).astype(BF16)
        dz_ref[...] = (dh2 * ev * s * (1.0 - s)).astype(BF16)

    row = _bs((tb, D), lambda i: (i, 0))
    vec = _bs((1, D), lambda i: (0, 0))
    return _call(body, grid=(T // tb,), in_specs=[row, row, row, row, vec],
                 out_specs=[row, row, row, vec, _bs((1, LANES), lambda i: (0, 0))],
                 out_shape=[jax.ShapeDtypeStruct((T, D), F32), jax.ShapeDtypeStruct((T, D), BF16),
                            jax.ShapeDtypeStruct((T, D), BF16), jax.ShapeDtypeStruct((1, D), F32),
                            jax.ShapeDtypeStruct((1, LANES), F32)],
                 compiler_params=_cp(("arbitrary",)), name="final_fb")(h1, e, z, tgt, g2)


def _pool_inv_count(t0, rows, pg, ngroups):
    t = t0 + lax.broadcasted_iota(jnp.int32, (rows, pg), 0)
    parts = []
    for w in POOL_WINDOWS[:ngroups]:
        parts.append(jnp.where(t + 1 >= w, 1.0 / w, 1.0 / (t + 1).astype(F32)))
    return parts


def _pool_fwd(proj, P, tb):
    T = proj.shape[0]
    ng = len(POOL_WINDOWS)
    pg = P // ng
    hb = tb // POOL_HALO

    def body(v_ref, tail_ref, o_ref, ext):
        i = pl.program_id(0)
        ext[pl.ds(0, POOL_HALO), :] = jnp.where(i > 0, tail_ref[...], 0.0)
        ext[pl.ds(POOL_HALO, tb), :] = v_ref[...]
        inv = _pool_inv_count(i * tb, tb, pg, ng)
        for g, w in enumerate(POOL_WINDOWS):
            cols = pl.ds(g * pg, pg)
            win = ext[pl.ds(POOL_HALO, tb), cols]
            for k in range(1, w):
                win = win + ext[pl.ds(POOL_HALO - k, tb), cols]
            o_ref[:, cols] = (win * inv[g] - ext[pl.ds(POOL_HALO, tb), cols]).astype(BF16)

    return _call(body, grid=(T // tb,),
                 in_specs=[_bs((tb, P), lambda i: (i, 0)),
                           _bs((POOL_HALO, P), lambda i: (jnp.maximum(i * hb - 1, 0), 0))],
                 out_specs=_bs((tb, P), lambda i: (i, 0)), out_shape=jax.ShapeDtypeStruct((T, P), BF16),
                 scratch_shapes=[pltpu.VMEM((tb + POOL_HALO, P), F32)],
                 compiler_params=_cp(("arbitrary",)), name="pool_fwd")(proj, proj)


def _pool_bwd(dpooled, tb):
    T, P = dpooled.shape
    ng = len(POOL_WINDOWS)
    pg = P // ng
    hb = tb // POOL_HALO
    nb = T // tb

    def body(d_ref, head_ref, o_ref, ext):
        i = pl.program_id(0)
        inv = _pool_inv_count(i * tb, tb, pg, ng)
        invh = _pool_inv_count((i + 1) * tb, POOL_HALO, pg, ng)
        for g in range(ng):
            cols = pl.ds(g * pg, pg)
            ext[pl.ds(0, tb), cols] = d_ref[:, cols] * inv[g]
            ext[pl.ds(tb, POOL_HALO), cols] = jnp.where(i < nb - 1, head_ref[:, cols] * invh[g], 0.0)
        for g, w in enumerate(POOL_WINDOWS):
            cols = pl.ds(g * pg, pg)
            acc = ext[pl.ds(0, tb), cols]
            for k in range(1, w):
                acc = acc + ext[pl.ds(k, tb), cols]
            o_ref[:, cols] = (acc - d_ref[:, cols]).astype(BF16)

    return _call(body, grid=(nb,),
                 in_specs=[_bs((tb, P), lambda i: (i, 0)),
                           _bs((POOL_HALO, P), lambda i: (jnp.minimum((i + 1) * hb, T // POOL_HALO - 1), 0))],
                 out_specs=_bs((tb, P), lambda i: (i, 0)), out_shape=jax.ShapeDtypeStruct((T, P), BF16),
                 scratch_shapes=[pltpu.VMEM((tb + POOL_HALO, P), F32)],
                 compiler_params=_cp(("arbitrary",)), name="pool_bwd")(dpooled, dpooled)


def _zoh(a_re, a_im, ldt, b_re, b_im):
    lam_re = jnp.minimum(a_re, A_RE_MAX)
    lam_im = a_im
    dt = jnp.exp(ldt)
    mag = jnp.exp(lam_re * dt)
    ang = lam_im * dt
    ab_re = mag * jnp.cos(ang)
    ab_im = mag * jnp.sin(ang)
    den = lam_re * lam_re + lam_im * lam_im
    n_re = ab_re - 1.0
    n_im = ab_im
    q_re = (n_re * lam_re + n_im * lam_im) / den
    q_im = (n_im * lam_re - n_re * lam_im) / den
    return ab_re, ab_im, q_re * b_re - q_im * b_im, q_re * b_im + q_im * b_re


def _ssm_prep(a_re, a_im, ldt, bt_re, bt_im):
    shp = jax.ShapeDtypeStruct(a_re.shape, F32)

    def body(a, b, c, d, e, o0, o1, o2, o3):
        r = _zoh(a[...], b[...], c[...], d[...], e[...])
        o0[...], o1[...], o2[...], o3[...] = r

    return _call(body, in_specs=[VMEM_FULL] * 5, out_specs=[VMEM_FULL] * 4, out_shape=[shp] * 4,
                 name="ssm_prep")(a_re, a_im, ldt, bt_re, bt_im)


def _ssm_prep_bwd(a_re, a_im, ldt, bt_re, bt_im, dab_re, dab_im, dbb_re, dbb_im, G):
    GC, N = a_re.shape
    C = GC // G

    def body(a, b, c, d, e, g0, g1, g2, g3, da_re, da_im, dldt, db_re, db_im):
        _, vjp = jax.vjp(_zoh, a[...], b[...], c[...], d[...], e[...])
        ga_re, ga_im, gl, gb_re, gb_im = vjp((g0[...], g1[...], g2[...], g3[...]))
        da_re[...] = jnp.sum(ga_re.reshape(G, C, N), axis=1)
        da_im[...] = jnp.sum(ga_im.reshape(G, C, N), axis=1)
        dldt[...] = jnp.sum(jnp.sum(gl.reshape(G, C, N), axis=1), axis=1, keepdims=True)
        db_re[...] = gb_re
        db_im[...] = gb_im

    gn = jax.ShapeDtypeStruct((G, N), F32)
    full = jax.ShapeDtypeStruct((GC, N), F32)
    return _call(body, in_specs=[VMEM_FULL] * 9, out_specs=[VMEM_FULL] * 5,
                 out_shape=[gn, gn, jax.ShapeDtypeStruct((G, 1), F32), full, full],
                 name="ssm_prep_bwd")(a_re, a_im, ldt, bt_re, bt_im, dab_re, dab_im, dbb_re, dbb_im)


def _coef_tiles(abr, abi, reverse):
    ns = abr.shape[1]
    row = lax.broadcasted_iota(jnp.int32, (SUBLANES, ns), 0)
    ar = jnp.broadcast_to(abr, (SUBLANES, ns))
    ai = jnp.broadcast_to(-abi if reverse else abi, (SUBLANES, ns))
    a2r, a2i = ar * ar - ai * ai, 2.0 * ar * ai
    a4r, a4i = a2r * a2r - a2i * a2i, 2.0 * a2r * a2i
    out = []
    for d, (vr, vi) in ((1, (ar, ai)), (2, (a2r, a2i)), (4, (a4r, a4i))):
        keep = (row < SUBLANES - d) if reverse else (row >= d)
        out += [jnp.where(keep, vr, 0.0), jnp.where(keep, vi, 0.0)]
    pr, pi = ar, ai
    for k in range(1, SUBLANES):
        sel = (row <= SUBLANES - 1 - k) if reverse else (row >= k)
        nr, ni = pr * ar - pi * ai, pr * ai + pi * ar
        pr, pi = jnp.where(sel, nr, pr), jnp.where(sel, ni, pi)
    return out + [pr, pi]


def _scan_block(xr_ref, xi_ref, coef_ref, car_ref, cai_ref, *, row0, nrows, ns, reverse):
    ntile = nrows // SUBLANES
    cw = min(SCAN_LANES, ns)
    edge = 0 if reverse else SUBLANES - 1
    for cc in range(ns // cw):
        cols = pl.ds(cc * cw, cw)
        co = [coef_ref[k, :, cols] for k in range(8)]

        def step(r, carry, cols=cols, co=co):
            cr, ci = carry
            rr = (ntile - 1 - r) if reverse else r
            rows = pl.ds(pl.multiple_of(row0 + rr * SUBLANES, SUBLANES), SUBLANES)
            xr, xi = xr_ref[rows, cols], xi_ref[rows, cols]
            for lvl, d in enumerate((1, 2, 4)):
                kr, ki = co[2 * lvl], co[2 * lvl + 1]
                sh = SUBLANES - d if reverse else d
                sr, si = pltpu.roll(xr, sh, 0), pltpu.roll(xi, sh, 0)
                xr, xi = xr + (kr * sr - ki * si), xi + (kr * si + ki * sr)
            xr, xi = xr + (co[6] * cr - co[7] * ci), xi + (co[6] * ci + co[7] * cr)
            xr_ref[rows, cols] = xr
            xi_ref[rows, cols] = xi
            return (jnp.broadcast_to(xr[edge:edge + 1, :], xr.shape),
                    jnp.broadcast_to(xi[edge:edge + 1, :], xi.shape))

        cr, ci = lax.fori_loop(0, ntile, step, (car_ref[:, cols], cai_ref[:, cols]))
        car_ref[:, cols] = cr
        cai_ref[:, cols] = ci


def _ssm_fwd(proj, bdr, bdi, cdr, cdi, abr, abi, dsk, P, tb):
    T = proj.shape[0]
    ntl, ct, st = bdr.shape
    ns = ntl * st
    nb = T // tb

    def body(u_ref, bdr_ref, bdi_ref, cdr_ref, cdi_ref, abr_ref, abi_ref, d_ref,
             y_ref, ge_ref, bsr_ref, bsi_ref, sr, si, coef, car, cai):
        @pl.when(pl.program_id(0) == 0)
        def _():
            for k, tile in enumerate(_coef_tiles(abr_ref[...], abi_ref[...], False)):
                coef[k] = tile
            car[...] = jnp.zeros_like(car)
            cai[...] = jnp.zeros_like(cai)

        bsr_ref[...] = car[...]
        bsi_ref[...] = cai[...]
        u = u_ref[...]
        ub = u.astype(BF16)
        for s in range(ntl):
            us = ub[:, s * ct:(s + 1) * ct]
            sr[:, s * st:(s + 1) * st] = jnp.dot(us, bdr_ref[s], preferred_element_type=F32)
            si[:, s * st:(s + 1) * st] = jnp.dot(us, bdi_ref[s], preferred_element_type=F32)
        _scan_block(sr, si, coef, car, cai, row0=0, nrows=tb, ns=ns, reverse=False)
        for s in range(ntl):
            s_re = sr[:, s * st:(s + 1) * st].astype(BF16)
            s_im = si[:, s * st:(s + 1) * st].astype(BF16)
            y = (jnp.dot(s_re, cdr_ref[s], preferred_element_type=F32)
                 - jnp.dot(s_im, cdi_ref[s], preferred_element_type=F32)
                 + d_ref[:, s * ct:(s + 1) * ct] * u[:, s * ct:(s + 1) * ct])
            y_ref[:, s * ct:(s + 1) * ct] = y
            ge_ref[:, s * ct:(s + 1) * ct] = _gelu(y).astype(BF16)

    full3 = lambda a: _bs(a.shape, lambda i: (0, 0, 0))
    vec = lambda n: _bs((1, n), lambda i: (0, 0))
    row = _bs((tb, P), lambda i: (i, 0))
    st_spec = _bs((None, SUBLANES, ns), lambda i: (i, 0, 0))
    return _call(body, grid=(nb,),
                 in_specs=[_bs((tb, P), lambda i: (i, 2)), full3(bdr), full3(bdi), full3(cdr), full3(cdi),
                           vec(ns), vec(ns), vec(P)],
                 out_specs=[row, row, st_spec, st_spec],
                 out_shape=[jax.ShapeDtypeStruct((T, P), F32), jax.ShapeDtypeStruct((T, P), BF16),
                            jax.ShapeDtypeStruct((nb, SUBLANES, ns), F32),
                            jax.ShapeDtypeStruct((nb, SUBLANES, ns), F32)],
                 scratch_shapes=[pltpu.VMEM((tb, ns), F32), pltpu.VMEM((tb, ns), F32),
                                 pltpu.VMEM((8, SUBLANES, ns), F32),
                                 pltpu.VMEM((SUBLANES, ns), F32), pltpu.VMEM((SUBLANES, ns), F32)],
                 compiler_params=_cp(("arbitrary",)), name="ssm_fwd")(proj, bdr, bdi, cdr, cdi, abr, abi, dsk)


def _ssm_bwd(proj, y, dge, bsr, bsi, bdr, bdi, cdr, cdi, abr, abi, dsk, P, tb):
    T = proj.shape[0]
    ntl, ct, st = bdr.shape
    ns = ntl * st
    nb = T // tb
    pad = SUBLANES

    def body(u_ref, y_ref, dge_ref, bsr_ref, bsi_ref, abr_ref, abi_ref, d_ref, bdr_h, bdi_h, cdr_h, cdi_h,
             du_ref, dabr_ref, dabi_ref, dd_ref, dbdr_h, dbdi_h, dcdr_h, dcdi_h,
             wbdr, wbdi, wcdr, wcdi, abdr, abdi, acdr, acdi, spr, spi, gr, gi, coef_f, coef_r,
             car, cai, gcr, gci):
        i = pl.program_id(0)

        @pl.when(i == 0)
        def _():
            for h, w in ((bdr_h, wbdr), (bdi_h, wbdi), (cdr_h, wcdr), (cdi_h, wcdi)):
                pltpu.sync_copy(h, w)
            for a in (abdr, abdi, acdr, acdi, gcr, gci):
                a[...] = jnp.zeros_like(a)
            for o in (dabr_ref, dabi_ref, dd_ref):
                o[...] = jnp.zeros_like(o)
            for k, tile in enumerate(_coef_tiles(abr_ref[...], abi_ref[...], False)):
                coef_f[k] = tile
            for k, tile in enumerate(_coef_tiles(abr_ref[...], abi_ref[...], True)):
                coef_r[k] = tile

        car[...] = bsr_ref[...]
        cai[...] = bsi_ref[...]
        spr[pl.ds(0, pad), :] = bsr_ref[...]
        spi[pl.ds(0, pad), :] = bsi_ref[...]
        u = u_ref[...]
        ub = u.astype(BF16)
        for s in range(ntl):
            us = ub[:, s * ct:(s + 1) * ct]
            spr[pl.ds(pad, tb), s * st:(s + 1) * st] = jnp.dot(us, wbdr[s], preferred_element_type=F32)
            spi[pl.ds(pad, tb), s * st:(s + 1) * st] = jnp.dot(us, wbdi[s], preferred_element_type=F32)
        _scan_block(spr, spi, coef_f, car, cai, row0=pad, nrows=tb, ns=ns, reverse=False)

        dy = dge_ref[...] * _gelu_grad(y_ref[...])
        dyb = dy.astype(BF16)
        for s in range(ntl):
            dys = dyb[:, s * ct:(s + 1) * ct]
            gr[:, s * st:(s + 1) * st] = lax.dot_general(dys, wcdr[s], (NT, ((), ())), preferred_element_type=F32)
            gi[:, s * st:(s + 1) * st] = -lax.dot_general(dys, wcdi[s], (NT, ((), ())), preferred_element_type=F32)
        _scan_block(gr, gi, coef_r, gcr, gci, row0=0, nrows=tb, ns=ns, reverse=True)

        cw = min(SCAN_LANES, ns)
        for cc in range(ns // cw):
            cols = pl.ds(cc * cw, cw)
            pr, pi = spr[pl.ds(pad - 1, tb), cols], spi[pl.ds(pad - 1, tb), cols]
            g_r, g_i = gr[:, cols], gi[:, cols]
            dabr_ref[:, cols] += jnp.sum(g_r * pr + g_i * pi, axis=0, keepdims=True)
            dabi_ref[:, cols] += jnp.sum(g_i * pr - g_r * pi, axis=0, keepdims=True)

        for s in range(ntl):
            sl_c, sl_s = slice(s * ct, (s + 1) * ct), slice(s * st, (s + 1) * st)
            s_re = spr[pl.ds(pad, tb), sl_s].astype(BF16)
            s_im = spi[pl.ds(pad, tb), sl_s].astype(BF16)
            g_re, g_im = gr[:, sl_s].astype(BF16), gi[:, sl_s].astype(BF16)
            dys, us = dyb[:, sl_c], ub[:, sl_c]
            acdr[s] += lax.dot_general(s_re, dys, (TN, ((), ())), preferred_element_type=F32)
            acdi[s] -= lax.dot_general(s_im, dys, (TN, ((), ())), preferred_element_type=F32)
            abdr[s] += lax.dot_general(us, g_re, (TN, ((), ())), preferred_element_type=F32)
            abdi[s] += lax.dot_general(us, g_im, (TN, ((), ())), preferred_element_type=F32)
            du = (lax.dot_general(g_re, wbdr[s], (NT, ((), ())), preferred_element_type=F32)
                  + lax.dot_general(g_im, wbdi[s], (NT, ((), ())), preferred_element_type=F32)
                  + d_ref[:, sl_c] * dy[:, sl_c])
            du_ref[:, sl_c] = du.astype(BF16)
        dd_ref[...] += jnp.sum(dy * u, axis=0, keepdims=True)

        @pl.when(i == nb - 1)
        def _():
            for a, h in ((abdr, dbdr_h), (abdi, dbdi_h), (acdr, dcdr_h), (acdi, dcdi_h)):
                pltpu.sync_copy(a, h)

    rev = lambda i: nb - 1 - i
    vec = lambda n: _bs((1, n), lambda i: (0, 0))
    row = _bs((tb, P), lambda i: (rev(i), 0))
    st_spec = _bs((None, SUBLANES, ns), lambda i: (rev(i), 0, 0))
    bshape = jax.ShapeDtypeStruct(bdr.shape, F32)
    cshape = jax.ShapeDtypeStruct(cdr.shape, F32)
    return _call(body, grid=(nb,),
                 in_specs=[_bs((tb, P), lambda i: (rev(i), 2)), row, row, st_spec, st_spec,
                           vec(ns), vec(ns), vec(P), ANY, ANY, ANY, ANY],
                 out_specs=[row, vec(ns), vec(ns), vec(P), ANY, ANY, ANY, ANY],
                 out_shape=[jax.ShapeDtypeStruct((T, P), BF16), jax.ShapeDtypeStruct((1, ns), F32),
                            jax.ShapeDtypeStruct((1, ns), F32), jax.ShapeDtypeStruct((1, P), F32),
                            bshape, bshape, cshape, cshape],
                 scratch_shapes=[pltpu.VMEM(bdr.shape, BF16), pltpu.VMEM(bdr.shape, BF16),
                                 pltpu.VMEM(cdr.shape, BF16), pltpu.VMEM(cdr.shape, BF16),
                                 pltpu.VMEM(bdr.shape, F32), pltpu.VMEM(bdr.shape, F32),
                                 pltpu.VMEM(cdr.shape, F32), pltpu.VMEM(cdr.shape, F32),
                                 pltpu.VMEM((tb + pad, ns), F32), pltpu.VMEM((tb + pad, ns), F32),
                                 pltpu.VMEM((tb, ns), F32), pltpu.VMEM((tb, ns), F32),
                                 pltpu.VMEM((8, SUBLANES, ns), F32), pltpu.VMEM((8, SUBLANES, ns), F32),
                                 pltpu.VMEM((SUBLANES, ns), F32), pltpu.VMEM((SUBLANES, ns), F32),
                                 pltpu.VMEM((SUBLANES, ns), F32), pltpu.VMEM((SUBLANES, ns), F32)],
                 compiler_params=_cp(("arbitrary",)), name="ssm_bwd")(
                     proj, y, dge, bsr, bsi, abr, abi, dsk, bdr, bdi, cdr, cdi)


def _adamw(w, g, m, v, name):
    R, C = w.shape
    tr = _t(R, 256)

    def body(w_ref, g_ref, m_ref, v_ref, d_ref, mo_ref, vo_ref):
        gv = g_ref[...]
        mn = ADAM_B1 * m_ref[...] + (1.0 - ADAM_B1) * gv
        vn = ADAM_B2 * v_ref[...] + (1.0 - ADAM_B2) * (gv * gv)
        m_hat = mn / (1.0 - ADAM_B1 ** ADAM_STEP)
        v_hat = vn / (1.0 - ADAM_B2 ** ADAM_STEP)
        d_ref[...] = -ADAM_LR * (m_hat / (jnp.sqrt(v_hat) + ADAM_EPS) + ADAM_WD * w_ref[...])
        mo_ref[...] = mn
        vo_ref[...] = vn

    blk = _bs((tr, C), lambda i: (i, 0))
    shp = jax.ShapeDtypeStruct((R, C), F32)
    return _call(body, grid=(R // tr,), in_specs=[blk] * 4, out_specs=[blk] * 3, out_shape=[shp] * 3,
                 compiler_params=_cp(("parallel",)), name=name)(w, g, m, v)


def _sum_cast(own, got, name):
    J, R, C = own.shape
    tr = _t(R, 256)

    def body(a_ref, b_ref, o_ref):
        o_ref[...] = (a_ref[...] + b_ref[...]).astype(BF16)

    blk = _bs((None, tr, C), lambda j, i: (j, i, 0))
    return _call(body, grid=(J, R // tr), in_specs=[blk, blk], out_specs=blk,
                 out_shape=jax.ShapeDtypeStruct((J, R, C), BF16),
                 compiler_params=_cp(("parallel", "parallel")), name=name)(own, got)


def _sum_chips(parts, name):
    J, R, C = parts.shape
    tr = _t(R, 256)

    def body(p_ref, o_ref):
        acc = p_ref[0].astype(F32)
        for j in range(1, J):
            acc = acc + p_ref[j].astype(F32)
        o_ref[...] = acc

    return _call(body, grid=(R // tr,), in_specs=[_bs((J, tr, C), lambda i: (0, i, 0))],
                 out_specs=_bs((tr, C), lambda i: (i, 0)), out_shape=jax.ShapeDtypeStruct((R, C), F32),
                 compiler_params=_cp(("parallel",)), name=name)(parts)


def _place():
    x, y, c = lax.axis_index("x"), lax.axis_index("y"), lax.axis_index("c")
    chips = [(1 - x, y), (x, 1 - y), (1 - x, 1 - y)]
    return x, y, c, chips


def _ag_weights(packed):
    R, L = packed.shape
    H = R // 2

    def body(src, out, ssem, rsem, lsem):
        x, y, c, chips = _place()
        me = 2 * x + y
        mine = pl.ds(pl.multiple_of(c * H, 16), H)
        other = pl.ds(pl.multiple_of((1 - c) * H, 16), H)
        local = pltpu.make_async_copy(src, out.at[me], lsem)
        local.start()

        def ici(j, chip_idx, to):
            return pltpu.make_async_remote_copy(src_ref=src.at[mine], dst_ref=out.at[chip_idx, mine],
                                                send_sem=ssem.at[j], recv_sem=rsem.at[j],
                                                device_id=to, device_id_type=MESH)

        def d2d(j, chip_idx, half):
            return pltpu.make_async_remote_copy(src_ref=out.at[chip_idx, half], dst_ref=out.at[chip_idx, half],
                                                send_sem=ssem.at[3 + j], recv_sem=rsem.at[3 + j],
                                                device_id=(x, y, 1 - c), device_id_type=MESH)

        first = [ici(j, me, (*chips[j], c)) for j in range(3)]
        for cp in first:
            cp.start()
        passed = []
        for j in range(3):
            idx = 2 * chips[j][0] + chips[j][1]
            ici(j, idx, (*chips[j], c)).wait_recv()
            fw = d2d(j, idx, mine)
            fw.start()
            passed.append(fw)
        for j in range(3):
            idx = 2 * chips[j][0] + chips[j][1]
            d2d(j, idx, other).wait_recv()
        for cp in first + passed:
            cp.wait_send()
        local.wait()

    return _call(body, in_specs=[ANY], out_specs=ANY, out_shape=jax.ShapeDtypeStruct((N_CHIP, R, L), BF16),
                 scratch_shapes=[pltpu.SemaphoreType.DMA((6,)), pltpu.SemaphoreType.DMA((6,)),
                                 pltpu.SemaphoreType.DMA],
                 name="ag_weights")(packed)


def _halves_to_sibling(grads):
    n = len(grads)

    def body(*refs):
        g, own, got = refs[:n], refs[n:2 * n], refs[2 * n:3 * n]
        ssem, rsem, lsem = refs[3 * n:]
        x, y, c, _ = _place()
        cps = []
        for i in range(n):
            H = g[i].shape[1] // 2
            mine = pl.ds(pl.multiple_of(c * H, SUBLANES), H)
            other = pl.ds(pl.multiple_of((1 - c) * H, SUBLANES), H)
            lc = pltpu.make_async_copy(g[i].at[:, mine, :], own[i], lsem.at[i])
            rc = pltpu.make_async_remote_copy(src_ref=g[i].at[:, other, :], dst_ref=got[i],
                                              send_sem=ssem.at[i], recv_sem=rsem.at[i],
                                              device_id=(x, y, 1 - c), device_id_type=MESH)
            lc.start()
            rc.start()
            cps.append((lc, rc))
        for lc, rc in cps:
            rc.wait()
            lc.wait()

    half = [jax.ShapeDtypeStruct((a.shape[0], a.shape[1] // 2, a.shape[2]), a.dtype) for a in grads]
    outs = _call(body, in_specs=[ANY] * n, out_specs=[ANY] * (2 * n), out_shape=half + half,
                 scratch_shapes=[pltpu.SemaphoreType.DMA((n,)), pltpu.SemaphoreType.DMA((n,)),
                                 pltpu.SemaphoreType.DMA((n,))],
                 name="rs_halves")(*grads)
    return outs[:n], outs[n:]


def _scatter_to_chips(parts):
    n = len(parts)

    def body(*refs):
        s, got = refs[:n], refs[n:2 * n]
        ssem, rsem, lsem = refs[2 * n:]
        x, y, c, chips = _place()
        me = 2 * x + y
        cps = []
        for i in range(n):
            lc = pltpu.make_async_copy(s[i].at[me], got[i].at[me], lsem.at[i])
            lc.start()
            cps.append(lc)
            for j in range(3):
                idx = 2 * chips[j][0] + chips[j][1]
                rc = pltpu.make_async_remote_copy(src_ref=s[i].at[idx], dst_ref=got[i].at[me],
                                                  send_sem=ssem.at[i, j], recv_sem=rsem.at[i, j],
                                                  device_id=(*chips[j], c), device_id_type=MESH)
                rc.start()
                cps.append(rc)
        for i in range(n):
            for j in range(3):
                idx = 2 * chips[j][0] + chips[j][1]
                pltpu.make_async_remote_copy(src_ref=s[i].at[idx], dst_ref=got[i].at[idx],
                                             send_sem=ssem.at[i, j], recv_sem=rsem.at[i, j],
                                             device_id=(*chips[j], c), device_id_type=MESH).wait()
            cps[4 * i].wait()

    shp = [jax.ShapeDtypeStruct(a.shape, a.dtype) for a in parts]
    return _call(body, in_specs=[ANY] * n, out_specs=[ANY] * n, out_shape=shp,
                 scratch_shapes=[pltpu.SemaphoreType.DMA((n, 3)), pltpu.SemaphoreType.DMA((n, 3)),
                                 pltpu.SemaphoreType.DMA((n,))],
                 name="rs_chips")(*parts)


def _join_halves(halves):
    n = len(halves)

    def body(*refs):
        h, full = refs[:n], refs[n:2 * n]
        ssem, rsem, lsem = refs[2 * n:]
        x, y, c, _ = _place()
        cps = []
        for i in range(n):
            H = h[i].shape[0]
            mine = pl.ds(pl.multiple_of(c * H, SUBLANES), H)
            lc = pltpu.make_async_copy(h[i], full[i].at[mine], lsem.at[i])
            rc = pltpu.make_async_remote_copy(src_ref=h[i], dst_ref=full[i].at[mine],
                                              send_sem=ssem.at[i], recv_sem=rsem.at[i],
                                              device_id=(x, y, 1 - c), device_id_type=MESH)
            lc.start()
            rc.start()
            cps.append((lc, rc))
        for i, (lc, rc) in enumerate(cps):
            H = h[i].shape[0]
            other = pl.ds(pl.multiple_of((1 - c) * H, SUBLANES), H)
            rc.wait_send()
            pltpu.make_async_remote_copy(src_ref=h[i], dst_ref=full[i].at[other],
                                         send_sem=ssem.at[i], recv_sem=rsem.at[i],
                                         device_id=(x, y, 1 - c), device_id_type=MESH).wait_recv()
            lc.wait()

    shp = [jax.ShapeDtypeStruct((2 * a.shape[0], a.shape[1]), a.dtype) for a in halves]
    return _call(body, in_specs=[ANY] * n, out_specs=[ANY] * n, out_shape=shp,
                 scratch_shapes=[pltpu.SemaphoreType.DMA((n,)), pltpu.SemaphoreType.DMA((n,)),
                                 pltpu.SemaphoreType.DMA((n,))],
                 name="rs_join")(*halves)


def _allreduce_small(buf):
    R, L = buf.shape
    RB = R // N_DEV

    def body(x_ref, o_ref, got, ssem, rsem):
        x, y, c, _ = _place()
        me = 4 * x + 2 * y + c

        def dev(k):
            return (k // 4, (k // 2) % 2, k % 2)

        def slab(k):
            return pl.ds(pl.multiple_of(k * RB, SUBLANES), RB)

        sends = []
        for d in range(1, N_DEV):
            peer = (me + d) % N_DEV
            cp = pltpu.make_async_remote_copy(src_ref=x_ref.at[slab(peer)], dst_ref=got.at[me],
                                              send_sem=ssem.at[0, d], recv_sem=rsem.at[0, d],
                                              device_id=dev(peer), device_id_type=MESH)
            cp.start()
            sends.append(cp)
        got[me] = x_ref[slab(me), :]
        for d in range(1, N_DEV):
            src = (me + N_DEV - d) % N_DEV
            pltpu.make_async_remote_copy(src_ref=x_ref.at[slab(me)], dst_ref=got.at[src],
                                         send_sem=ssem.at[0, d], recv_sem=rsem.at[0, d],
                                         device_id=dev(src), device_id_type=MESH).wait_recv()
        acc = got[0]
        for k in range(1, N_DEV):
            acc = acc + got[k]
        o_ref[slab(me), :] = acc
        for d in range(1, N_DEV):
            peer = (me + d) % N_DEV
            cp = pltpu.make_async_remote_copy(src_ref=o_ref.at[slab(me)], dst_ref=o_ref.at[slab(me)],
                                              send_sem=ssem.at[1, d], recv_sem=rsem.at[1, d],
                                              device_id=dev(peer), device_id_type=MESH)
            cp.start()
            sends.append(cp)
        for d in range(1, N_DEV):
            src = (me + N_DEV - d) % N_DEV
            pltpu.make_async_remote_copy(src_ref=o_ref.at[slab(src)], dst_ref=o_ref.at[slab(src)],
                                         send_sem=ssem.at[1, d], recv_sem=rsem.at[1, d],
                                         device_id=dev(src), device_id_type=MESH).wait_recv()
        for cp in sends:
            cp.wait_send()

    return _call(body, in_specs=[VMEM_FULL], out_specs=VMEM_FULL, out_shape=jax.ShapeDtypeStruct((R, L), F32),
                 scratch_shapes=[pltpu.VMEM((N_DEV, RB, L), F32), pltpu.SemaphoreType.DMA((2, N_DEV)),
                                 pltpu.SemaphoreType.DMA((2, N_DEV))],
                 name="allreduce_small")(buf)


def _block_diag(t, gt):
    G, A, B = t.shape
    t4 = t.reshape(G // gt, gt, A, B)
    eye = jnp.eye(gt, dtype=t.dtype)
    return jnp.einsum('sgab,gh->sgahb', t4, eye).reshape(G // gt, gt * A, gt * B)


def _block_diag_extract(m, gt, A, B):
    S = m.shape[0]
    m5 = m.reshape(S, gt, A, gt, B)
    eye = jnp.eye(gt, dtype=m.dtype)
    return jnp.einsum('sgahb,gh->sgab', m5, eye).reshape(S * gt, A, B)


def _pack_small(arrs, rows):
    flat = jnp.concatenate([a.reshape(-1).astype(F32) for a in arrs])
    return jnp.pad(flat, (0, rows * LANES - flat.shape[0])).reshape(rows, LANES)


def _unpack_small(buf, shapes):
    flat = buf.reshape(-1)
    out, off = [], 0
    for s in shapes:
        n = 1
        for d in s:
            n *= d
        out.append(flat[off:off + n].reshape(s))
        off += n
    return out


def kernel(x, p, norm_gain, w_in, w_pool, pool_scale, a_re, a_im, log_dt, b_re, b_im, c_re, c_im, d_skip, w_glu, w_out, w_ple, w_ple_gate, final_gain, loss_target, m_norm_gain, m_w_in, m_w_pool, m_pool_scale, m_a_re, m_a_im, m_log_dt, m_b_re, m_b_im, m_c_re, m_c_im, m_d_skip, m_w_glu, m_w_out, m_w_ple, m_w_ple_gate, m_final_gain, v_norm_gain, v_w_in, v_w_pool, v_pool_scale, v_a_re, v_a_im, v_log_dt, v_b_re, v_b_im, v_c_re, v_c_im, v_d_skip, v_w_glu, v_w_out, v_w_ple, v_w_ple_gate, v_final_gain):
    xs, pe, tgt = x[0], p[0, 0], loss_target[0]
    T, D = xs.shape
    E = pe.shape[1]
    P = D // 2
    NG = len(POOL_WINDOWS)
    PG = P // NG
    G, N, C = P // SSM_GROUP, SSM_STATE, SSM_GROUP
    GT = min(SSM_TILE_GROUPS, G)
    Q = D // N_CHIP

    big = {"w_in": (w_in, m_w_in, v_w_in), "w_pool": (w_pool, m_w_pool, v_w_pool),
           "w_glu": (w_glu, m_w_glu, v_w_glu), "w_out": (w_out, m_w_out, v_w_out),
           "w_ple": (w_ple, m_w_ple, v_w_ple), "w_ple_gate": (w_ple_gate, m_w_ple_gate, v_w_ple_gate)}
    big_names = list(big)
    shard2d = {n: (big[n][0].size // big[n][0].shape[-1], big[n][0].shape[-1]) for n in big_names}
    packed = jnp.concatenate([big[n][0].astype(BF16).reshape(-1, LANES) for n in big_names])
    gathered = _ag_weights(packed)
    full, off = {}, 0
    for n in big_names:
        r, c_ = shard2d[n]
        rows = r * c_ // LANES
        full[n] = gathered[:, off:off + rows].reshape(N_CHIP, r, c_)
        off += rows
    win_g, wglu_g, wple_g = full["w_in"], full["w_glu"], full["w_ple"]
    wout, wpg = full["w_out"].reshape(D, D), full["w_ple_gate"].reshape(D, D)
    wp = full["w_pool"].reshape(N_CHIP, NG, PG // N_CHIP, PG).transpose(1, 0, 2, 3).reshape(NG, PG, PG)

    rep = lambda a: jnp.repeat(a, C, axis=0)
    a_re_r, a_im_r = rep(a_re[0]), rep(a_im[0])
    ldt_r = rep(jnp.broadcast_to(log_dt[0][:, None], (G, N)))
    bt_re = b_re[0].transpose(0, 2, 1).reshape(G * C, N)
    bt_im = b_im[0].transpose(0, 2, 1).reshape(G * C, N)
    ab_re_r, ab_im_r, bbt_re, bbt_im = _ssm_prep(a_re_r, a_im_r, ldt_r, bt_re, bt_im)
    abr = ab_re_r[::C].reshape(1, G * N)
    abi = ab_im_r[::C].reshape(1, G * N)
    bdr = _block_diag(bbt_re.reshape(G, C, N), GT).astype(BF16)
    bdi = _block_diag(bbt_im.reshape(G, C, N), GT).astype(BF16)
    cdr = _block_diag(c_re[0].transpose(0, 2, 1), GT).astype(BF16)
    cdi = _block_diag(c_im[0].transpose(0, 2, 1), GT).astype(BF16)

    tb = _t(T, 256)
    tbs = _t(T, 256)
    tm = _t(T, 512)
    DH = _t(D, 1024)
    hn = _norm1(xs, norm_gain, tb)
    proj = _mm(hn, win_g, dims=NN, grid=(T // tm, N_CHIP, D // DH),
               a_spec=_bs((tm, DH), lambda i, j, k: (i, k)), b_spec=_bs((None, DH, P), lambda i, j, k: (j, k, 0)),
               o_spec=_bs((tm, P), lambda i, j, k: (i, j)), out_shape=jax.ShapeDtypeStruct((T, 4 * P), F32),
               name="mm_proj")
    pooled = _pool_fwd(proj, P, tb)
    mixed = _mm(pooled, wp, dims=NN, grid=(T // tm, NG, 1),
                a_spec=_bs((tm, PG), lambda i, g, k: (i, g)), b_spec=_bs((None, PG, PG), lambda i, g, k: (g, 0, 0)),
                o_spec=_bs((tm, PG), lambda i, g, k: (i, g)), out_shape=jax.ShapeDtypeStruct((T, P), F32),
                name="mm_pool")
    y, ge, bsr, bsi = _ssm_fwd(proj, bdr, bdi, cdr, cdi, abr, abi, d_skip, P, tbs)
    hg = _mm(ge, wglu_g, dims=NN, grid=(T // tm, N_CHIP, 1),
             a_spec=_bs((tm, P), lambda i, j, k: (i, 0)), b_spec=_bs((None, P, Q), lambda i, j, k: (j, 0, 0)),
             o_spec=_bs((tm, Q), lambda i, j, k: (i, j)), out_shape=jax.ShapeDtypeStruct((T, 2 * P), F32),
             name="mm_glu")
    cat = _gate_fwd(mixed, proj, hg, pool_scale, tb)
    h1 = _mm(cat, wout, dims=NN, grid=(T // tm, D // DH, D // DH), res=xs,
             a_spec=_bs((tm, DH), lambda i, n, k: (i, k)), b_spec=_bs((DH, DH), lambda i, n, k: (k, n)),
             r_spec=_bs((tm, DH), lambda i, n, k: (i, n)), o_spec=_bs((tm, DH), lambda i, n, k: (i, n)),
             out_shape=jax.ShapeDtypeStruct((T, D), F32), name="mm_out")
    e = _mm(pe, wple_g, dims=NN, grid=(T // tm, N_CHIP, 1),
            a_spec=_bs((tm, E), lambda i, j, k: (i, 0)), b_spec=_bs((None, E, Q), lambda i, j, k: (j, 0, 0)),
            o_spec=_bs((tm, Q), lambda i, j, k: (i, j)), out_shape=jax.ShapeDtypeStruct((T, D), F32),
            name="mm_ple")
    z = _mm(h1, wpg, dims=NN, grid=(T // tm, D // DH, D // DH),
            a_spec=_bs((tm, DH), lambda i, n, k: (i, k)), b_spec=_bs((DH, DH), lambda i, n, k: (k, n)),
            o_spec=_bs((tm, DH), lambda i, n, k: (i, n)), out_shape=jax.ShapeDtypeStruct((T, D), F32),
            name="mm_pgate")
    dh2, de, dz, dg2, lpart = _final_fb(h1, e, z, tgt, final_gain.reshape(1, D), tb)

    tk = _t(T, 512)
    dh1 = _mm(dz, wpg, dims=NT, grid=(T // tm, D // DH, D // DH), res=dh2,
              a_spec=_bs((tm, DH), lambda i, n, k: (i, k)), b_spec=_bs((DH, DH), lambda i, n, k: (n, k)),
              r_spec=_bs((tm, DH), lambda i, n, k: (i, n)), o_spec=_bs((tm, DH), lambda i, n, k: (i, n)),
              out_shape=jax.ShapeDtypeStruct((T, D), F32), name="mm_dh1")
    g_wpg = _mm(h1, dz, dims=TN, grid=(D // DH, D // DH, T // tk),
                a_spec=_bs((tk, DH), lambda m, n, k: (k, m)), b_spec=_bs((tk, DH), lambda m, n, k: (k, n)),
                o_spec=_bs((DH, DH), lambda m, n, k: (m, n)), out_shape=jax.ShapeDtypeStruct((D, D), F32),
                name="mm_gwpg")
    g_wple = _mm(pe, de, dims=TN, grid=(1, N_CHIP, T // tk),
                 a_spec=_bs((tk, E), lambda m, j, k: (k, 0)), b_spec=_bs((tk, Q), lambda m, j, k: (k, j)),
                 o_spec=_bs((None, E, Q), lambda m, j, k: (j, 0, 0)),
                 out_shape=jax.ShapeDtypeStruct((N_CHIP, E, Q), F32), name="mm_gwple")
    dcat = _mm(dh1, wout, dims=NT, grid=(T // tm, D // DH, D // DH),
               a_spec=_bs((tm, DH), lambda i, n, k: (i, k)), b_spec=_bs((DH, DH), lambda i, n, k: (n, k)),
               o_spec=_bs((tm, DH), lambda i, n, k: (i, n)), out_shape=jax.ShapeDtypeStruct((T, D), F32),
               name="mm_dcat")
    g_wout = _mm(cat, dh1, dims=TN, grid=(D // DH, D // DH, T // tk),
                 a_spec=_bs((tk, DH), lambda m, n, k: (k, m)), b_spec=_bs((tk, DH), lambda m, n, k: (k, n)),
                 o_spec=_bs((DH, DH), lambda m, n, k: (m, n)), out_shape=jax.ShapeDtypeStruct((D, D), F32),
                 name="mm_gwout")
    dmixed, dpg, dsg, dhg, dps = _gate_bwd(dcat, mixed, proj, hg, pool_scale, tb)
    dge = _mm(dhg, wglu_g, dims=NT, grid=(T // tm, 1, N_CHIP),
              a_spec=_bs((tm, Q), lambda i, n, k: (i, k)), b_spec=_bs((None, P, Q), lambda i, n, k: (k, 0, 0)),
              o_spec=_bs((tm, P), lambda i, n, k: (i, 0)), out_shape=jax.ShapeDtypeStruct((T, P), F32),
              name="mm_dge")
    g_wglu = _mm(ge, dhg, dims=TN, grid=(1, N_CHIP, T // tk),
                 a_spec=_bs((tk, P), lambda m, j, k: (k, 0)), b_spec=_bs((tk, Q), lambda m, j, k: (k, j)),
                 o_spec=_bs((None, P, Q), lambda m, j, k: (j, 0, 0)),
                 out_shape=jax.ShapeDtypeStruct((N_CHIP, P, Q), F32), name="mm_gwglu")
    du, dabr, dabi, dd, dbdr, dbdi, dcdr, dcdi = _ssm_bwd(proj, y, dge, bsr, bsi, bdr, bdi, cdr, cdi,
                                                          abr, abi, d_skip, P, tbs)
    dpooled = _mm(dmixed, wp, dims=NT, grid=(T // tm, NG, 1),
                  a_spec=_bs((tm, PG), lambda i, g, k: (i, g)), b_spec=_bs((None, PG, PG), lambda i, g, k: (g, 0, 0)),
                  o_spec=_bs((tm, PG), lambda i, g, k: (i, g)), out_shape=jax.ShapeDtypeStruct((T, P), F32),
                  name="mm_dpooled")
    g_wp = _mm(pooled, dmixed, dims=TN, grid=(NG, 1, T // tk),
               a_spec=_bs((tk, PG), lambda g, n, k: (k, g)), b_spec=_bs((tk, PG), lambda g, n, k: (k, g)),
               o_spec=_bs((None, PG, PG), lambda g, n, k: (g, 0, 0)),
               out_shape=jax.ShapeDtypeStruct((NG, PG, PG), F32), name="mm_gwp")
    dpi = _pool_bwd(dpooled, tb)
    dproj = jnp.concatenate([dpi, dpg, du, dsg], axis=1)
    dhn = _mm(dproj, win_g, dims=NT, grid=(T // tm, D // DH, N_CHIP),
              a_spec=_bs((tm, P), lambda i, n, k: (i, k)), b_spec=_bs((None, DH, P), lambda i, n, k: (k, n, 0)),
              o_spec=_bs((tm, DH), lambda i, n, k: (i, n)), out_shape=jax.ShapeDtypeStruct((T, D), F32),
              name="mm_dhn")
    g_win = _mm(hn, dproj, dims=TN, grid=(D // DH, N_CHIP, T // tk),
                a_spec=_bs((tk, DH), lambda m, j, k: (k, m)), b_spec=_bs((tk, P), lambda m, j, k: (k, j)),
                o_spec=_bs((None, DH, P), lambda m, j, k: (j, m, 0)),
                out_shape=jax.ShapeDtypeStruct((N_CHIP, D, P), F32), name="mm_gwin")
    grad_x, dg1 = _norm1_bwd(xs, dhn, dh1, norm_gain, tb)

    dbbt_re = _block_diag_extract(dbdr, GT, C, N).reshape(G * C, N)
    dbbt_im = _block_diag_extract(dbdi, GT, C, N).reshape(G * C, N)
    g_c_re = _block_diag_extract(dcdr, GT, N, C).transpose(0, 2, 1)
    g_c_im = _block_diag_extract(dcdi, GT, N, C).transpose(0, 2, 1)
    dab_re_r = rep(dabr.reshape(G, N)) * (1.0 / C)
    dab_im_r = rep(dabi.reshape(G, N)) * (1.0 / C)
    g_a_re, g_a_im, g_ldt, g_bt_re, g_bt_im = _ssm_prep_bwd(a_re_r, a_im_r, ldt_r, bt_re, bt_im,
                                                            dab_re_r, dab_im_r, dbbt_re, dbbt_im, G)
    g_b_re = g_bt_re.reshape(G, C, N).transpose(0, 2, 1)
    g_b_im = g_bt_im.reshape(G, C, N).transpose(0, 2, 1)

    gbig = {"w_in": g_win,
            "w_pool": g_wp.reshape(NG, N_CHIP, PG // N_CHIP, PG).transpose(1, 0, 2, 3).reshape(N_CHIP, NG * PG // N_CHIP, PG),
            "w_glu": g_wglu, "w_out": g_wout.reshape(N_CHIP, Q, D), "w_ple": g_wple,
            "w_ple_gate": g_wpg.reshape(N_CHIP, Q, D)}
    own, got = _halves_to_sibling([gbig[n] for n in big_names])
    chip_sums = [_sum_cast(o, g_, "sum_cast_" + n) for o, g_, n in zip(own, got, big_names)]
    arrived = _scatter_to_chips(chip_sums)
    halves = [_sum_chips(a, "sum_chips_" + n) for a, n in zip(arrived, big_names)]
    gshard = _join_halves(halves)

    small_names = ["norm_gain", "pool_scale", "a_re", "a_im", "log_dt", "b_re", "b_im", "c_re", "c_im",
                   "d_skip", "final_gain"]
    small_w = dict(norm_gain=norm_gain, pool_scale=pool_scale, a_re=a_re, a_im=a_im, log_dt=log_dt, b_re=b_re,
                   b_im=b_im, c_re=c_re, c_im=c_im, d_skip=d_skip, final_gain=final_gain)
    small_m = dict(norm_gain=m_norm_gain, pool_scale=m_pool_scale, a_re=m_a_re, a_im=m_a_im, log_dt=m_log_dt,
                   b_re=m_b_re, b_im=m_b_im, c_re=m_c_re, c_im=m_c_im, d_skip=m_d_skip, final_gain=m_final_gain)
    small_v = dict(norm_gain=v_norm_gain, pool_scale=v_pool_scale, a_re=v_a_re, a_im=v_a_im, log_dt=v_log_dt,
                   b_re=v_b_re, b_im=v_b_im, c_re=v_c_re, c_im=v_c_im, d_skip=v_d_skip, final_gain=v_final_gain)
    small_g = dict(norm_gain=dg1, pool_scale=dps, a_re=g_a_re, a_im=g_a_im, log_dt=g_ldt, b_re=g_b_re,
                   b_im=g_b_im, c_re=g_c_re, c_im=g_c_im, d_skip=dd, final_gain=dg2)
    shapes = [small_w[n].shape for n in small_names]
    total = sum(small_w[n].size for n in small_names) + 1
    unit = N_DEV * SUBLANES
    rows = -(-(-(-total // LANES)) // unit) * unit
    gbuf = _pack_small([small_g[n] for n in small_names] + [lpart[0, :1]], rows)
    gsum = _allreduce_small(gbuf)
    wbuf = _pack_small([small_w[n] for n in small_names], rows)
    mbuf = _pack_small([small_m[n] for n in small_names], rows)
    vbuf = _pack_small([small_v[n] for n in small_names], rows)
    dsm, msm, vsm = _adamw(wbuf, gsum, mbuf, vbuf, "adamw_small")
    g_small = dict(zip(small_names, _unpack_small(gsum, shapes)))
    d_small = dict(zip(small_names, _unpack_small(dsm, shapes)))
    m_small = dict(zip(small_names, _unpack_small(msm, shapes)))
    v_small = dict(zip(small_names, _unpack_small(vsm, shapes)))
    loss = gsum.reshape(-1)[total - 1]

    g_out, d_out, m_out, v_out = dict(g_small), dict(d_small), dict(m_small), dict(v_small)
    for n, gs in zip(big_names, gshard):
        w_, m_, v_ = big[n]
        r2 = shard2d[n]
        d_, mn_, vn_ = _adamw(w_.reshape(r2), gs, m_.reshape(r2), v_.reshape(r2), "adamw_" + n)
        g_out[n], d_out[n], m_out[n], v_out[n] = (a.reshape(w_.shape) for a in (gs, d_, mn_, vn_))

    order = ["norm_gain", "w_in", "w_pool", "pool_scale", "a_re", "a_im", "log_dt", "b_re", "b_im", "c_re",
             "c_im", "d_skip", "w_glu", "w_out", "w_ple", "w_ple_gate", "final_gain"]
    return (loss, grad_x[None], *[g_out[n] for n in order], *[d_out[n] for n in order],
            *[m_out[n] for n in order], *[v_out[n] for n in order])
```

```python
import functools

import jax
import jax.numpy as jnp
from jax import lax
from jax.experimental import pallas as pl
from jax.experimental.pallas import tpu as pltpu

F32, BF16 = jnp.float32, jnp.bfloat16
MESH = pl.DeviceIdType.MESH
ANY = pl.BlockSpec(memory_space=pl.ANY)
VMEM_FULL = pl.BlockSpec(memory_space=pltpu.VMEM)

EPS = 1e-6
A_RE_MAX = -1e-4
SSM_GROUP = 16
SSM_STATE = 64
POOL_WINDOWS = (2, 4, 8, 16)
POOL_HALO = 16
ADAM_LR, ADAM_B1, ADAM_B2, ADAM_EPS, ADAM_WD, ADAM_STEP = 0.001, 0.9, 0.999, 1e-08, 0.01, 10

V7X_VMEM_BYTES = 64 * 1024 * 1024
VMEM_LIMIT = V7X_VMEM_BYTES - 8 * 1024 * 1024
SUBLANES, LANES = 8, 128
SSM_TILE_GROUPS = 16
SCAN_LANES = 512
N_DEV, N_CHIP = 8, 4
DMA_CHUNK_BYTES = 256 * 1024
DMA_MAX_CHUNKS = 32
AG_CHUNKS = 8
RS_CHUNKS = 8


def _t(n, pref):
    return pref if n % pref == 0 else n


def _cp(sem=None, vmem=VMEM_LIMIT):
    return pltpu.CompilerParams(dimension_semantics=sem, vmem_limit_bytes=vmem)


def _call(body, **kw):
    return pl.pallas_call(body, **kw)


NN = ((1,), (0,))
NT = ((1,), (1,))
TN = ((0,), (0,))


def _mm(a, b, *, dims, grid, a_spec, b_spec, o_spec, out_shape, name, res=None, r_spec=None):
    nk, kax = grid[-1], len(grid) - 1
    acc_shape = tuple(d for d in o_spec.block_shape if d is not None)

    def body(*refs):
        if res is None:
            a_ref, b_ref, o_ref, acc = refs
            r_ref = None
        else:
            a_ref, b_ref, r_ref, o_ref, acc = refs
        k = pl.program_id(kax)

        @pl.when(k == 0)
        def _():
            acc[...] = jnp.zeros_like(acc)

        acc[...] += lax.dot_general(a_ref[...].astype(BF16), b_ref[...].astype(BF16),
                                    (dims, ((), ())), preferred_element_type=F32)

        @pl.when(k == nk - 1)
        def _():
            r = acc[...]
            if r_ref is not None:
                r = r + r_ref[...]
            o_ref[...] = r.astype(o_ref.dtype)

    ins, specs = [a, b], [a_spec, b_spec]
    if res is not None:
        ins.append(res)
        specs.append(r_spec)
    sem = ("parallel",) * kax + ("arbitrary",)
    return _call(body, grid=grid, in_specs=specs, out_specs=o_spec, out_shape=out_shape,
                 scratch_shapes=[pltpu.VMEM(acc_shape, F32)], compiler_params=_cp(sem), name=name)(*ins)


def _bs(shape, fn):
    return pl.BlockSpec(shape, fn)


def _sigmoid(v):
    return 1.0 / (1.0 + jnp.exp(-v))


def _gelu(v):
    return 0.5 * v * (1.0 + jnp.tanh(0.7978845608028654 * (v + 0.044715 * v * v * v)))


def _gelu_grad(v):
    t = jnp.tanh(0.7978845608028654 * (v + 0.044715 * v * v * v))
    return 0.5 * (1.0 + t) + 0.5 * v * (1.0 - t * t) * 0.7978845608028654 * (1.0 + 3 * 0.044715 * v * v)


def _norm1(x, g1, tb):
    T, D = x.shape

    def body(x_ref, g_ref, o_ref):
        xv = x_ref[...]
        r = lax.rsqrt(jnp.mean(xv * xv, axis=-1, keepdims=True) + EPS)
        o_ref[...] = ((xv * r) * g_ref[...]).astype(BF16)

    return _call(body, grid=(T // tb,),
                 in_specs=[_bs((tb, D), lambda i: (i, 0)), _bs((1, D), lambda i: (0, 0))],
                 out_specs=_bs((tb, D), lambda i: (i, 0)), out_shape=jax.ShapeDtypeStruct((T, D), BF16),
                 compiler_params=_cp(("parallel",)), name="norm1")(x, g1)


def _norm1_bwd(x, dhn, dh1, g1, tb):
    T, D = x.shape

    def body(x_ref, dhn_ref, dh1_ref, g_ref, dx_ref, dg_ref):
        @pl.when(pl.program_id(0) == 0)
        def _():
            dg_ref[...] = jnp.zeros_like(dg_ref)

        xv = x_ref[...]
        r = lax.rsqrt(jnp.mean(xv * xv, axis=-1, keepdims=True) + EPS)
        xh = xv * r
        dhn_v = dhn_ref[...]
        dg_ref[...] += jnp.sum(dhn_v * xh, axis=0, keepdims=True)
        dxh = dhn_v * g_ref[...]
        dx_ref[...] = dh1_ref[...] + r * (dxh - xh * jnp.mean(dxh * xh, axis=-1, keepdims=True))

    row = _bs((tb, D), lambda i: (i, 0))
    vec = _bs((1, D), lambda i: (0, 0))
    return _call(body, grid=(T // tb,), in_specs=[row, row, row, vec], out_specs=[row, vec],
                 out_shape=[jax.ShapeDtypeStruct((T, D), F32), jax.ShapeDtypeStruct((1, D), F32)],
                 compiler_params=_cp(("arbitrary",)), name="norm1_bwd")(x, dhn, dh1, g1)


def _gate_fwd(mixed, proj, hg, ps, tb):
    T, P = mixed.shape

    def body(mx_ref, pg_ref, sg_ref, hg_ref, ps_ref, o_ref):
        pg, sg = pg_ref[...], sg_ref[...]
        ya = (mx_ref[...] * ps_ref[...]) * (pg * _sigmoid(pg))
        hgv = hg_ref[...]
        o = hgv[:, :P] * _sigmoid(hgv[:, P:])
        yb = o * (sg * _sigmoid(sg))
        o_ref[:, :P] = ya.astype(BF16)
        o_ref[:, P:] = yb.astype(BF16)

    return _call(body, grid=(T // tb,),
                 in_specs=[_bs((tb, P), lambda i: (i, 0)), _bs((tb, P), lambda i: (i, 1)),
                           _bs((tb, P), lambda i: (i, 3)), _bs((tb, 2 * P), lambda i: (i, 0)),
                           _bs((1, P), lambda i: (0, 0))],
                 out_specs=_bs((tb, 2 * P), lambda i: (i, 0)),
                 out_shape=jax.ShapeDtypeStruct((T, 2 * P), BF16),
                 compiler_params=_cp(("parallel",)), name="gate_fwd")(mixed, proj, proj, hg, ps)


def _gate_bwd(dcat, mixed, proj, hg, ps, tb):
    T, P = mixed.shape

    def body(dc_ref, mx_ref, pg_ref, sg_ref, hg_ref, ps_ref, dmx_ref, dpg_ref, dsg_ref, dhg_ref, dps_ref):
        @pl.when(pl.program_id(0) == 0)
        def _():
            dps_ref[...] = jnp.zeros_like(dps_ref)

        dc = dc_ref[...]
        dya, dyb = dc[:, :P], dc[:, P:]
        pg, sg, mx, psv = pg_ref[...], sg_ref[...], mx_ref[...], ps_ref[...]
        s_pg = _sigmoid(pg)
        dpa = dya * (pg * s_pg)
        dpg_ref[...] = (dya * (mx * psv) * (s_pg * (1.0 + pg * (1.0 - s_pg)))).astype(BF16)
        dps_ref[...] += jnp.sum(dpa * mx, axis=0, keepdims=True)
        dmx_ref[...] = (dpa * psv).astype(BF16)
        hgv = hg_ref[...]
        h1, s_h2 = hgv[:, :P], _sigmoid(hgv[:, P:])
        s_sg = _sigmoid(sg)
        do = dyb * (sg * s_sg)
        dsg_ref[...] = (dyb * (h1 * s_h2) * (s_sg * (1.0 + sg * (1.0 - s_sg)))).astype(BF16)
        dhg_ref[:, :P] = (do * s_h2).astype(BF16)
        dhg_ref[:, P:] = (do * h1 * s_h2 * (1.0 - s_h2)).astype(BF16)

    rowp = _bs((tb, P), lambda i: (i, 0))
    row2 = _bs((tb, 2 * P), lambda i: (i, 0))
    vec = _bs((1, P), lambda i: (0, 0))
    return _call(body, grid=(T // tb,),
                 in_specs=[row2, rowp, _bs((tb, P), lambda i: (i, 1)), _bs((tb, P), lambda i: (i, 3)), row2, vec],
                 out_specs=[rowp, rowp, rowp, row2, vec],
                 out_shape=[jax.ShapeDtypeStruct((T, P), BF16), jax.ShapeDtypeStruct((T, P), BF16),
                            jax.ShapeDtypeStruct((T, P), BF16), jax.ShapeDtypeStruct((T, 2 * P), BF16),
                            jax.ShapeDtypeStruct((1, P), F32)],
                 compiler_params=_cp(("arbitrary",)), name="gate_bwd")(dcat, mixed, proj, proj, hg, ps)


def _final_fb(h1, e, z, tgt, g2, tb):
    T, D = h1.shape

    def body(h1_ref, e_ref, z_ref, t_ref, g_ref, dh2_ref, de_ref, dz_ref, dg_ref, l_ref):
        @pl.when(pl.program_id(0) == 0)
        def _():
            dg_ref[...] = jnp.zeros_like(dg_ref)
            l_ref[...] = jnp.zeros_like(l_ref)

        ev = e_ref[...]
        s = _sigmoid(z_ref[...])
        h2 = h1_ref[...] + ev * s
        r = lax.rsqrt(jnp.mean(h2 * h2, axis=-1, keepdims=True) + EPS)
        xh = h2 * r
        gv = g_ref[...]
        diff = xh * gv - t_ref[...]
        l_ref[...] += 0.5 * jnp.sum(jnp.mean(diff * diff, axis=-1, keepdims=True))
        dout = diff * (1.0 / D)
        dg_ref[...] += jnp.sum(dout * xh, axis=0, keepdims=True)
        dxh = dout * gv
        dh2 = r * (dxh - xh * jnp.mean(dxh * xh, axis=-1, keepdims=True))
        dh2_ref[...] = dh2
        de_ref[...] = (dh2 * s).astype(BF16)
        dz_ref[...] = (dh2 * ev * s * (1.0 - s)).astype(BF16)

    row = _bs((tb, D), lambda i: (i, 0))
    vec = _bs((1, D), lambda i: (0, 0))
    return _call(body, grid=(T // tb,), in_specs=[row, row, row, row, vec],
                 out_specs=[row, row, row, vec, _bs((1, LANES), lambda i: (0, 0))],
                 out_shape=[jax.ShapeDtypeStruct((T, D), F32), jax.ShapeDtypeStruct((T, D), BF16),
                            jax.ShapeDtypeStruct((T, D), BF16), jax.ShapeDtypeStruct((1, D), F32),
                            jax.ShapeDtypeStruct((1, LANES), F32)],
                 compiler_params=_cp(("arbitrary",)), name="final_fb")(h1, e, z, tgt, g2)


def _pool_inv_count(t0, rows, pg, ngroups):
    t = t0 + lax.broadcasted_iota(jnp.int32, (rows, pg), 0)
    parts = []
    for w in POOL_WINDOWS[:ngroups]:
        parts.append(jnp.where(t + 1 >= w, 1.0 / w, 1.0 / (t + 1).astype(F32)))
    return parts


def _pool_fwd(proj, P, tb):
    T = proj.shape[0]
    ng = len(POOL_WINDOWS)
    pg = P // ng
    hb = tb // POOL_HALO

    def body(v_ref, tail_ref, o_ref, ext):
        i = pl.program_id(0)
        ext[pl.ds(0, POOL_HALO), :] = jnp.where(i > 0, tail_ref[...], 0.0)
        ext[pl.ds(POOL_HALO, tb), :] = v_ref[...]
        inv = _pool_inv_count(i * tb, tb, pg, ng)
        for g, w in enumerate(POOL_WINDOWS):
            cols = pl.ds(g * pg, pg)
            win = ext[pl.ds(POOL_HALO, tb), cols]
            for k in range(1, w):
                win = win + ext[pl.ds(POOL_HALO - k, tb), cols]
            o_ref[:, cols] = (win * inv[g] - ext[pl.ds(POOL_HALO, tb), cols]).astype(BF16)

    return _call(body, grid=(T // tb,),
                 in_specs=[_bs((tb, P), lambda i: (i, 0)),
                           _bs((POOL_HALO, P), lambda i: (jnp.maximum(i * hb - 1, 0), 0))],
                 out_specs=_bs((tb, P), lambda i: (i, 0)), out_shape=jax.ShapeDtypeStruct((T, P), BF16),
                 scratch_shapes=[pltpu.VMEM((tb + POOL_HALO, P), F32)],
                 compiler_params=_cp(("arbitrary",)), name="pool_fwd")(proj, proj)


def _pool_bwd(dpooled, tb):
    T, P = dpooled.shape
    ng = len(POOL_WINDOWS)
    pg = P // ng
    hb = tb // POOL_HALO
    nb = T // tb

    def body(d_ref, head_ref, o_ref, ext):
        i = pl.program_id(0)
        inv = _pool_inv_count(i * tb, tb, pg, ng)
        invh = _pool_inv_count((i + 1) * tb, POOL_HALO, pg, ng)
        for g in range(ng):
            cols = pl.ds(g * pg, pg)
            ext[pl.ds(0, tb), cols] = d_ref[:, cols] * inv[g]
            ext[pl.ds(tb, POOL_HALO), cols] = jnp.where(i < nb - 1, head_ref[:, cols] * invh[g], 0.0)
        for g, w in enumerate(POOL_WINDOWS):
            cols = pl.ds(g * pg, pg)
            acc = ext[pl.ds(0, tb), cols]
            for k in range(1, w):
                acc = acc + ext[pl.ds(k, tb), cols]
            o_ref[:, cols] = (acc - d_ref[:, cols]).astype(BF16)

    return _call(body, grid=(nb,),
                 in_specs=[_bs((tb, P), lambda i: (i, 0)),
                           _bs((POOL_HALO, P), lambda i: (jnp.minimum((i + 1) * hb, T // POOL_HALO - 1), 0))],
                 out_specs=_bs((tb, P), lambda i: (i, 0)), out_shape=jax.ShapeDtypeStruct((T, P), BF16),
                 scratch_shapes=[pltpu.VMEM((tb + POOL_HALO, P), F32)],
                 compiler_params=_cp(("arbitrary",)), name="pool_bwd")(dpooled, dpooled)


def _zoh(a_re, a_im, ldt, b_re, b_im):
    lam_re = jnp.minimum(a_re, A_RE_MAX)
    lam_im = a_im
    dt = jnp.exp(ldt)
    mag = jnp.exp(lam_re * dt)
    ang = lam_im * dt
    ab_re = mag * jnp.cos(ang)
    ab_im = mag * jnp.sin(ang)
    den = lam_re * lam_re + lam_im * lam_im
    n_re = ab_re - 1.0
    n_im = ab_im
    q_re = (n_re * lam_re + n_im * lam_im) / den
    q_im = (n_im * lam_re - n_re * lam_im) / den
    return ab_re, ab_im, q_re * b_re - q_im * b_im, q_re * b_im + q_im * b_re


def _ssm_prep(a_re, a_im, ldt, bt_re, bt_im):
    shp = jax.ShapeDtypeStruct(a_re.shape, F32)

    def body(a, b, c, d, e, o0, o1, o2, o3):
        r = _zoh(a[...], b[...], c[...], d[...], e[...])
        o0[...], o1[...], o2[...], o3[...] = r

    return _call(body, in_specs=[VMEM_FULL] * 5, out_specs=[VMEM_FULL] * 4, out_shape=[shp] * 4,
                 name="ssm_prep")(a_re, a_im, ldt, bt_re, bt_im)


def _ssm_prep_bwd(a_re, a_im, ldt, bt_re, bt_im, dab_re, dab_im, dbb_re, dbb_im, G):
    GC, N = a_re.shape
    C = GC // G

    def body(a, b, c, d, e, g0, g1, g2, g3, da_re, da_im, dldt, db_re, db_im):
        _, vjp = jax.vjp(_zoh, a[...], b[...], c[...], d[...], e[...])
        ga_re, ga_im, gl, gb_re, gb_im = vjp((g0[...], g1[...], g2[...], g3[...]))
        da_re[...] = jnp.sum(ga_re.reshape(G, C, N), axis=1)
        da_im[...] = jnp.sum(ga_im.reshape(G, C, N), axis=1)
        dldt[...] = jnp.sum(jnp.sum(gl.reshape(G, C, N), axis=1), axis=1, keepdims=True)
        db_re[...] = gb_re
        db_im[...] = gb_im

    gn = jax.ShapeDtypeStruct((G, N), F32)
    full = jax.ShapeDtypeStruct((GC, N), F32)
    return _call(body, in_specs=[VMEM_FULL] * 9, out_specs=[VMEM_FULL] * 5,
                 out_shape=[gn, gn, jax.ShapeDtypeStruct((G, 1), F32), full, full],
                 name="ssm_prep_bwd")(a_re, a_im, ldt, bt_re, bt_im, dab_re, dab_im, dbb_re, dbb_im)


def _coef_tiles(abr, abi, reverse):
    ns = abr.shape[1]
    row = lax.broadcasted_iota(jnp.int32, (SUBLANES, ns), 0)
    ar = jnp.broadcast_to(abr, (SUBLANES, ns))
    ai = jnp.broadcast_to(-abi if reverse else abi, (SUBLANES, ns))
    a2r, a2i = ar * ar - ai * ai, 2.0 * ar * ai
    a4r, a4i = a2r * a2r - a2i * a2i, 2.0 * a2r * a2i
    out = []
    for d, (vr, vi) in ((1, (ar, ai)), (2, (a2r, a2i)), (4, (a4r, a4i))):
        keep = (row < SUBLANES - d) if reverse else (row >= d)
        out += [jnp.where(keep, vr, 0.0), jnp.where(keep, vi, 0.0)]
    pr, pi = ar, ai
    for k in range(1, SUBLANES):
        sel = (row <= SUBLANES - 1 - k) if reverse else (row >= k)
        nr, ni = pr * ar - pi * ai, pr * ai + pi * ar
        pr, pi = jnp.where(sel, nr, pr), jnp.where(sel, ni, pi)
    return out + [pr, pi]


def _scan_block(xr_ref, xi_ref, coef_ref, car_ref, cai_ref, *, row0, nrows, ns, reverse):
    ntile = nrows // SUBLANES
    cw = min(SCAN_LANES, ns)
    edge = 0 if reverse else SUBLANES - 1
    for cc in range(ns // cw):
        cols = pl.ds(cc * cw, cw)
        co = [coef_ref[k, :, cols] for k in range(8)]

        def step(r, carry, cols=cols, co=co):
            cr, ci = carry
            rr = (ntile - 1 - r) if reverse else r
            rows = pl.ds(pl.multiple_of(row0 + rr * SUBLANES, SUBLANES), SUBLANES)
            xr, xi = xr_ref[rows, cols], xi_ref[rows, cols]
            for lvl, d in enumerate((1, 2, 4)):
                kr, ki = co[2 * lvl], co[2 * lvl + 1]
                sh = SUBLANES - d if reverse else d
                sr, si = pltpu.roll(xr, sh, 0), pltpu.roll(xi, sh, 0)
                xr, xi = xr + (kr * sr - ki * si), xi + (kr * si + ki * sr)
            xr, xi = xr + (co[6] * cr - co[7] * ci), xi + (co[6] * ci + co[7] * cr)
            xr_ref[rows, cols] = xr
            xi_ref[rows, cols] = xi
            return (jnp.broadcast_to(xr[edge:edge + 1, :], xr.shape),
                    jnp.broadcast_to(xi[edge:edge + 1, :], xi.shape))

        cr, ci = lax.fori_loop(0, ntile, step, (car_ref[:, cols], cai_ref[:, cols]))
        car_ref[:, cols] = cr
        cai_ref[:, cols] = ci


def _ssm_fwd(proj, bdr, bdi, cdr, cdi, abr, abi, dsk, P, tb):
    T = proj.shape[0]
    ntl, ct, st = bdr.shape
    ns = ntl * st
    nb = T // tb

    def body(u_ref, bdr_ref, bdi_ref, cdr_ref, cdi_ref, abr_ref, abi_ref, d_ref,
             y_ref, ge_ref, bsr_ref, bsi_ref, sr, si, coef, car, cai):
        @pl.when(pl.program_id(0) == 0)
        def _():
            for k, tile in enumerate(_coef_tiles(abr_ref[...], abi_ref[...], False)):
                coef[k] = tile
            car[...] = jnp.zeros_like(car)
            cai[...] = jnp.zeros_like(cai)

        bsr_ref[...] = car[...]
        bsi_ref[...] = cai[...]
        u = u_ref[...]
        ub = u.astype(BF16)
        for s in range(ntl):
            us = ub[:, s * ct:(s + 1) * ct]
            sr[:, s * st:(s + 1) * st] = jnp.dot(us, bdr_ref[s], preferred_element_type=F32)
            si[:, s * st:(s + 1) * st] = jnp.dot(us, bdi_ref[s], preferred_element_type=F32)
        _scan_block(sr, si, coef, car, cai, row0=0, nrows=tb, ns=ns, reverse=False)
        for s in range(ntl):
            s_re = sr[:, s * st:(s + 1) * st].astype(BF16)
            s_im = si[:, s * st:(s + 1) * st].astype(BF16)
            y = (jnp.dot(s_re, cdr_ref[s], preferred_element_type=F32)
                 - jnp.dot(s_im, cdi_ref[s], preferred_element_type=F32)
                 + d_ref[:, s * ct:(s + 1) * ct] * u[:, s * ct:(s + 1) * ct])
            y_ref[:, s * ct:(s + 1) * ct] = y
            ge_ref[:, s * ct:(s + 1) * ct] = _gelu(y).astype(BF16)

    full3 = lambda a: _bs(a.shape, lambda i: (0, 0, 0))
    vec = lambda n: _bs((1, n), lambda i: (0, 0))
    row = _bs((tb, P), lambda i: (i, 0))
    st_spec = _bs((None, SUBLANES, ns), lambda i: (i, 0, 0))
    return _call(body, grid=(nb,),
                 in_specs=[_bs((tb, P), lambda i: (i, 2)), full3(bdr), full3(bdi), full3(cdr), full3(cdi),
                           vec(ns), vec(ns), vec(P)],
                 out_specs=[row, row, st_spec, st_spec],
                 out_shape=[jax.ShapeDtypeStruct((T, P), F32), jax.ShapeDtypeStruct((T, P), BF16),
                            jax.ShapeDtypeStruct((nb, SUBLANES, ns), F32),
                            jax.ShapeDtypeStruct((nb, SUBLANES, ns), F32)],
                 scratch_shapes=[pltpu.VMEM((tb, ns), F32), pltpu.VMEM((tb, ns), F32),
                                 pltpu.VMEM((8, SUBLANES, ns), F32),
                                 pltpu.VMEM((SUBLANES, ns), F32), pltpu.VMEM((SUBLANES, ns), F32)],
                 compiler_params=_cp(("arbitrary",)), name="ssm_fwd")(proj, bdr, bdi, cdr, cdi, abr, abi, dsk)


def _ssm_bwd(proj, y, dge, bsr, bsi, bdr, bdi, cdr, cdi, abr, abi, dsk, P, tb):
    T = proj.shape[0]
    ntl, ct, st = bdr.shape
    ns = ntl * st
    nb = T // tb
    pad = SUBLANES

    def body(u_ref, y_ref, dge_ref, bsr_ref, bsi_ref, abr_ref, abi_ref, d_ref, bdr_h, bdi_h, cdr_h, cdi_h,
             du_ref, dabr_ref, dabi_ref, dd_ref, dbdr_h, dbdi_h, dcdr_h, dcdi_h,
             wbdr, wbdi, wcdr, wcdi, abdr, abdi, acdr, acdi, spr, spi, gr, gi, coef_f, coef_r,
             car, cai, gcr, gci):
        i = pl.program_id(0)

        @pl.when(i == 0)
        def _():
            for h, w in ((bdr_h, wbdr), (bdi_h, wbdi), (cdr_h, wcdr), (cdi_h, wcdi)):
                pltpu.sync_copy(h, w)
            for a in (abdr, abdi, acdr, acdi, gcr, gci):
                a[...] = jnp.zeros_like(a)
            for o in (dabr_ref, dabi_ref, dd_ref):
                o[...] = jnp.zeros_like(o)
            for k, tile in enumerate(_coef_tiles(abr_ref[...], abi_ref[...], False)):
                coef_f[k] = tile
            for k, tile in enumerate(_coef_tiles(abr_ref[...], abi_ref[...], True)):
                coef_r[k] = tile

        car[...] = bsr_ref[...]
        cai[...] = bsi_ref[...]
        spr[pl.ds(0, pad), :] = bsr_ref[...]
        spi[pl.ds(0, pad), :] = bsi_ref[...]
        u = u_ref[...]
        ub = u.astype(BF16)
        for s in range(ntl):
            us = ub[:, s * ct:(s + 1) * ct]
            spr[pl.ds(pad, tb), s * st:(s + 1) * st] = jnp.dot(us, wbdr[s], preferred_element_type=F32)
            spi[pl.ds(pad, tb), s * st:(s + 1) * st] = jnp.dot(us, wbdi[s], preferred_element_type=F32)
        _scan_block(spr, spi, coef_f, car, cai, row0=pad, nrows=tb, ns=ns, reverse=False)

        dy = dge_ref[...] * _gelu_grad(y_ref[...])
        dyb = dy.astype(BF16)
        for s in range(ntl):
            dys = dyb[:, s * ct:(s + 1) * ct]
            gr[:, s * st:(s + 1) * st] = lax.dot_general(dys, wcdr[s], (NT, ((), ())), preferred_element_type=F32)
            gi[:, s * st:(s + 1) * st] = -lax.dot_general(dys, wcdi[s], (NT, ((), ())), preferred_element_type=F32)
        _scan_block(gr, gi, coef_r, gcr, gci, row0=0, nrows=tb, ns=ns, reverse=True)

        cw = min(SCAN_LANES, ns)
        for cc in range(ns // cw):
            cols = pl.ds(cc * cw, cw)
            pr, pi = spr[pl.ds(pad - 1, tb), cols], spi[pl.ds(pad - 1, tb), cols]
            g_r, g_i = gr[:, cols], gi[:, cols]
            dabr_ref[:, cols] += jnp.sum(g_r * pr + g_i * pi, axis=0, keepdims=True)
            dabi_ref[:, cols] += jnp.sum(g_i * pr - g_r * pi, axis=0, keepdims=True)

        for s in range(ntl):
            sl_c, sl_s = slice(s * ct, (s + 1) * ct), slice(s * st, (s + 1) * st)
            s_re = spr[pl.ds(pad, tb), sl_s].astype(BF16)
            s_im = spi[pl.ds(pad, tb), sl_s].astype(BF16)
            g_re, g_im = gr[:, sl_s].astype(BF16), gi[:, sl_s].astype(BF16)
            dys, us = dyb[:, sl_c], ub[:, sl_c]
            acdr[s] += lax.dot_general(s_re, dys, (TN, ((), ())), preferred_element_type=F32)
            acdi[s] -= lax.dot_general(s_im, dys, (TN, ((), ())), preferred_element_type=F32)
            abdr[s] += lax.dot_general(us, g_re, (TN, ((), ())), preferred_element_type=F32)
            abdi[s] += lax.dot_general(us, g_im, (TN, ((), ())), preferred_element_type=F32)
            du = (lax.dot_general(g_re, wbdr[s], (NT, ((), ())), preferred_element_type=F32)
                  + lax.dot_general(g_im, wbdi[s], (NT, ((), ())), preferred_element_type=F32)
                  + d_ref[:, sl_c] * dy[:, sl_c])
            du_ref[:, sl_c] = du.astype(BF16)
        dd_ref[...] += jnp.sum(dy * u, axis=0, keepdims=True)

        @pl.when(i == nb - 1)
        def _():
            for a, h in ((abdr, dbdr_h), (abdi, dbdi_h), (acdr, dcdr_h), (acdi, dcdi_h)):
                pltpu.sync_copy(a, h)

    rev = lambda i: nb - 1 - i
    vec = lambda n: _bs((1, n), lambda i: (0, 0))
    row = _bs((tb, P), lambda i: (rev(i), 0))
    st_spec = _bs((None, SUBLANES, ns), lambda i: (rev(i), 0, 0))
    bshape = jax.ShapeDtypeStruct(bdr.shape, F32)
    cshape = jax.ShapeDtypeStruct(cdr.shape, F32)
    return _call(body, grid=(nb,),
                 in_specs=[_bs((tb, P), lambda i: (rev(i), 2)), row, row, st_spec, st_spec,
                           vec(ns), vec(ns), vec(P), ANY, ANY, ANY, ANY],
                 out_specs=[row, vec(ns), vec(ns), vec(P), ANY, ANY, ANY, ANY],
                 out_shape=[jax.ShapeDtypeStruct((T, P), BF16), jax.ShapeDtypeStruct((1, ns), F32),
                            jax.ShapeDtypeStruct((1, ns), F32), jax.ShapeDtypeStruct((1, P), F32),
                            bshape, bshape, cshape, cshape],
                 scratch_shapes=[pltpu.VMEM(bdr.shape, BF16), pltpu.VMEM(bdr.shape, BF16),
                                 pltpu.VMEM(cdr.shape, BF16), pltpu.VMEM(cdr.shape, BF16),
                                 pltpu.VMEM(bdr.shape, F32), pltpu.VMEM(bdr.shape, F32),
                                 pltpu.VMEM(cdr.shape, F32), pltpu.VMEM(cdr.shape, F32),
                                 pltpu.VMEM((tb + pad, ns), F32), pltpu.VMEM((tb + pad, ns), F32),
                                 pltpu.VMEM((tb, ns), F32), pltpu.VMEM((tb, ns), F32),
                                 pltpu.VMEM((8, SUBLANES, ns), F32), pltpu.VMEM((8, SUBLANES, ns), F32),
                                 pltpu.VMEM((SUBLANES, ns), F32), pltpu.VMEM((SUBLANES, ns), F32),
                                 pltpu.VMEM((SUBLANES, ns), F32), pltpu.VMEM((SUBLANES, ns), F32)],
                 compiler_params=_cp(("arbitrary",)), name="ssm_bwd")(
                     proj, y, dge, bsr, bsi, abr, abi, dsk, bdr, bdi, cdr, cdi)


def _adamw(w, g, m, v, name):
    R, C = w.shape
    tr = _t(R, 256)

    def body(w_ref, g_ref, m_ref, v_ref, d_ref, mo_ref, vo_ref):
        gv = g_ref[...]
        mn = ADAM_B1 * m_ref[...] + (1.0 - ADAM_B1) * gv
        vn = ADAM_B2 * v_ref[...] + (1.0 - ADAM_B2) * (gv * gv)
        m_hat = mn / (1.0 - ADAM_B1 ** ADAM_STEP)
        v_hat = vn / (1.0 - ADAM_B2 ** ADAM_STEP)
        d_ref[...] = -ADAM_LR * (m_hat / (jnp.sqrt(v_hat) + ADAM_EPS) + ADAM_WD * w_ref[...])
        mo_ref[...] = mn
        vo_ref[...] = vn

    blk = _bs((tr, C), lambda i: (i, 0))
    shp = jax.ShapeDtypeStruct((R, C), F32)
    return _call(body, grid=(R // tr,), in_specs=[blk] * 4, out_specs=[blk] * 3, out_shape=[shp] * 3,
                 compiler_params=_cp(("parallel",)), name=name)(w, g, m, v)


def _sum_cast(own, got, name):
    J, R, C = own.shape
    tr = _t(R, 256)

    def body(a_ref, b_ref, o_ref):
        o_ref[...] = (a_ref[...] + b_ref[...]).astype(BF16)

    blk = _bs((None, tr, C), lambda j, i: (j, i, 0))
    return _call(body, grid=(J, R // tr), in_specs=[blk, blk], out_specs=blk,
                 out_shape=jax.ShapeDtypeStruct((J, R, C), BF16),
                 compiler_params=_cp(("parallel", "parallel")), name=name)(own, got)


def _sum_chips(parts, name):
    J, R, C = parts.shape
    tr = _t(R, 256)

    def body(p_ref, o_ref):
        acc = p_ref[0].astype(F32)
        for j in range(1, J):
            acc = acc + p_ref[j].astype(F32)
        o_ref[...] = acc

    return _call(body, grid=(R // tr,), in_specs=[_bs((J, tr, C), lambda i: (0, i, 0))],
                 out_specs=_bs((tr, C), lambda i: (i, 0)), out_shape=jax.ShapeDtypeStruct((R, C), F32),
                 compiler_params=_cp(("parallel",)), name=name)(parts)


def _place():
    x, y, c = lax.axis_index("x"), lax.axis_index("y"), lax.axis_index("c")
    chips = [(1 - x, y), (x, 1 - y), (1 - x, 1 - y)]
    return x, y, c, chips


def _split(nrows, row_bytes, align, cap=None):
    k = max(1, min(cap or DMA_MAX_CHUNKS, (nrows * row_bytes) // DMA_CHUNK_BYTES))
    while k > 1 and nrows % (k * align):
        k -= 1
    return k


def _ag_weights(packed):
    R, L = packed.shape
    H = R // 2
    K = _split(H, L * 2, 16, cap=AG_CHUNKS)
    hr = H // K
    KL = _split(R, L * 2, 16)
    lr = R // KL

    def body(src, out, ssem, rsem, lsem):
        x, y, c, chips = _place()
        me = 2 * x + y
        idx = [2 * cx + cy for cx, cy in chips]

        def rows(half, q):
            return pl.ds(pl.multiple_of(half * H + q * hr, 16), hr)

        for q in range(KL):
            pltpu.make_async_copy(src.at[pl.ds(q * lr, lr)], out.at[me, pl.ds(q * lr, lr)], lsem).start()

        def ici(j, q, chip_idx):
            return pltpu.make_async_remote_copy(src_ref=src.at[rows(c, q)], dst_ref=out.at[chip_idx, rows(c, q)],
                                                send_sem=ssem.at[j, q], recv_sem=rsem.at[j, q],
                                                device_id=(*chips[j], c), device_id_type=MESH)

        def d2d(j, q, half):
            return pltpu.make_async_remote_copy(src_ref=out.at[idx[j], rows(half, q)],
                                                dst_ref=out.at[idx[j], rows(half, q)],
                                                send_sem=ssem.at[3 + j, q], recv_sem=rsem.at[3 + j, q],
                                                device_id=(x, y, 1 - c), device_id_type=MESH)

        for q in range(K):
            for j in range(3):
                ici(j, q, me).start()
        for q in range(K):
            for j in range(3):
                ici(j, q, idx[j]).wait_recv()
                d2d(j, q, c).start()
        for q in range(K):
            for j in range(3):
                d2d(j, q, 1 - c).wait_recv()
        for q in range(K):
            for j in range(3):
                ici(j, q, me).wait_send()
                d2d(j, q, c).wait_send()
        pltpu.make_async_copy(src, out.at[me], lsem).wait()

    return _call(body, in_specs=[ANY], out_specs=ANY, out_shape=jax.ShapeDtypeStruct((N_CHIP, R, L), BF16),
                 scratch_shapes=[pltpu.SemaphoreType.DMA((6, K)), pltpu.SemaphoreType.DMA((6, K)),
                                 pltpu.SemaphoreType.DMA],
                 name="ag_weights")(packed)


def _halves_to_sibling(grads):
    n = len(grads)

    def body(*refs):
        g, own, got = refs[:n], refs[n:2 * n], refs[2 * n:3 * n]
        ssem, rsem, lsem = refs[3 * n:]
        x, y, c, _ = _place()
        sib = (x, y, 1 - c)
        for i in range(n):
            J, R, C = g[i].shape
            H = R // 2
            k = _split(H, C * 4, SUBLANES, cap=DMA_MAX_CHUNKS // J)
            hr = H // k
            for j in range(J):
                for q in range(k):
                    mine = pl.ds(pl.multiple_of(c * H + q * hr, SUBLANES), hr)
                    other = pl.ds(pl.multiple_of((1 - c) * H + q * hr, SUBLANES), hr)
                    to = pl.ds(q * hr, hr)
                    pltpu.make_async_copy(g[i].at[j, mine, :], own[i].at[j, to, :], lsem.at[i]).start()
                    pltpu.make_async_remote_copy(src_ref=g[i].at[j, other, :], dst_ref=got[i].at[j, to, :],
                                                 send_sem=ssem.at[i], recv_sem=rsem.at[i],
                                                 device_id=sib, device_id_type=MESH).start()
        for i in range(n):
            pltpu.make_async_remote_copy(src_ref=got[i], dst_ref=got[i], send_sem=ssem.at[i], recv_sem=rsem.at[i],
                                         device_id=sib, device_id_type=MESH).wait()
            pltpu.make_async_copy(own[i], own[i], lsem.at[i]).wait()

    half =[jax.ShapeDtypeStruct((a.shape[0], a.shape[1] // 2, a.shape[2]), a.dtype) for a in grads]
    outs = _call(body, in_specs=[ANY] * n, out_specs=[ANY] * (2 * n), out_shape=half + half,
                 scratch_shapes=[pltpu.SemaphoreType.DMA((n,)), pltpu.SemaphoreType.DMA((n,)),
                                 pltpu.SemaphoreType.DMA((n,))],
                 name="rs_halves")(*grads)
    return outs[:n], outs[n:]


def _scatter_to_chips(parts):
    n = len(parts)

    def body(*refs):
        s, got = refs[:n], refs[n:2 * n]
        ssem, rsem, lsem = refs[2 * n:]
        x, y, c, chips = _place()
        me = 2 * x + y
        idx = [2 * cx + cy for cx, cy in chips]
        for i in range(n):
            _, H, C = s[i].shape
            k = _split(H, C * 2, 16, cap=RS_CHUNKS)
            hr = H // k
            pltpu.make_async_copy(s[i].at[me], got[i].at[me], lsem.at[i]).start()
            for q in range(k):
                rows = pl.ds(q * hr, hr)
                for j in range(3):
                    pltpu.make_async_remote_copy(src_ref=s[i].at[idx[j], rows, :], dst_ref=got[i].at[me, rows, :],
                                                 send_sem=ssem.at[i, j], recv_sem=rsem.at[i, j],
                                                 device_id=(*chips[j], c), device_id_type=MESH).start()
        for i in range(n):
            for j in range(3):
                pltpu.make_async_remote_copy(src_ref=s[i].at[idx[j]], dst_ref=got[i].at[idx[j]],
                                             send_sem=ssem.at[i, j], recv_sem=rsem.at[i, j],
                                             device_id=(*chips[j], c), device_id_type=MESH).wait()
            pltpu.make_async_copy(s[i].at[me], got[i].at[me], lsem.at[i]).wait()

    shp = [jax.ShapeDtypeStruct(a.shape, a.dtype) for a in parts]
    return _call(body, in_specs=[ANY] * n, out_specs=[ANY] * n, out_shape=shp,
                 scratch_shapes=[pltpu.SemaphoreType.DMA((n, 3)), pltpu.SemaphoreType.DMA((n, 3)),
                                 pltpu.SemaphoreType.DMA((n,))],
                 name="rs_chips")(*parts)


def _join_halves(halves):
    n = len(halves)

    def body(*refs):
        h, full = refs[:n], refs[n:2 * n]
        ssem, rsem, lsem = refs[2 * n:]
        x, y, c, _ = _place()
        sib = (x, y, 1 - c)
        for i in range(n):
            H, C = h[i].shape
            k = _split(H, C * 4, SUBLANES)
            hr = H // k
            pltpu.make_async_copy(h[i], full[i].at[pl.ds(pl.multiple_of(c * H, SUBLANES), H)], lsem.at[i]).start()
            for q in range(k):
                to = pl.ds(pl.multiple_of(c * H + q * hr, SUBLANES), hr)
                pltpu.make_async_remote_copy(src_ref=h[i].at[pl.ds(q * hr, hr)], dst_ref=full[i].at[to],
                                             send_sem=ssem.at[i], recv_sem=rsem.at[i],
                                             device_id=sib, device_id_type=MESH).start()
        for i in range(n):
            pltpu.make_async_remote_copy(src_ref=h[i], dst_ref=h[i], send_sem=ssem.at[i], recv_sem=rsem.at[i],
                                         device_id=sib, device_id_type=MESH).wait()
            pltpu.make_async_copy(h[i], h[i], lsem.at[i]).wait()

    shp = [jax.ShapeDtypeStruct((2 * a.shape[0], a.shape[1]), a.dtype) for a in halves]
    return _call(body, in_specs=[ANY] * n, out_specs=[ANY] * n, out_shape=shp,
                 scratch_shapes=[pltpu.SemaphoreType.DMA((n,)), pltpu.SemaphoreType.DMA((n,)),
                                 pltpu.SemaphoreType.DMA((n,))],
                 name="rs_join")(*halves)


def _allreduce_small(buf):
    R, L = buf.shape
    RB = R // N_DEV

    def body(x_ref, o_ref, got, ssem, rsem):
        x, y, c, _ = _place()
        me = 4 * x + 2 * y + c

        def dev(k):
            return (k // 4, (k // 2) % 2, k % 2)

        def slab(k):
            return pl.ds(pl.multiple_of(k * RB, SUBLANES), RB)

        sends = []
        for d in range(1, N_DEV):
            peer = (me + d) % N_DEV
            cp = pltpu.make_async_remote_copy(src_ref=x_ref.at[slab(peer)], dst_ref=got.at[me],
                                              send_sem=ssem.at[0, d], recv_sem=rsem.at[0, d],
                                              device_id=dev(peer), device_id_type=MESH)
            cp.start()
            sends.append(cp)
        got[me] = x_ref[slab(me), :]
        for d in range(1, N_DEV):
            src = (me + N_DEV - d) % N_DEV
            pltpu.make_async_remote_copy(src_ref=x_ref.at[slab(me)], dst_ref=got.at[src],
                                         send_sem=ssem.at[0, d], recv_sem=rsem.at[0, d],
                                         device_id=dev(src), device_id_type=MESH).wait_recv()
        acc = got[0]
        for k in range(1, N_DEV):
            acc = acc + got[k]
        o_ref[slab(me), :] = acc
        for d in range(1, N_DEV):
            peer = (me + d) % N_DEV
            cp = pltpu.make_async_remote_copy(src_ref=o_ref.at[slab(me)], dst_ref=o_ref.at[slab(me)],
                                              send_sem=ssem.at[1, d], recv_sem=rsem.at[1, d],
                                              device_id=dev(peer), device_id_type=MESH)
            cp.start()
            sends.append(cp)
        for d in range(1, N_DEV):
            src = (me + N_DEV - d) % N_DEV
            pltpu.make_async_remote_copy(src_ref=o_ref.at[slab(src)], dst_ref=o_ref.at[slab(src)],
                                         send_sem=ssem.at[1, d], recv_sem=rsem.at[1, d],
                                         device_id=dev(src), device_id_type=MESH).wait_recv()
        for cp in sends:
            cp.wait_send()

    return _call(body, in_specs=[VMEM_FULL], out_specs=VMEM_FULL, out_shape=jax.ShapeDtypeStruct((R, L), F32),
                 scratch_shapes=[pltpu.VMEM((N_DEV, RB, L), F32), pltpu.SemaphoreType.DMA((2, N_DEV)),
                                 pltpu.SemaphoreType.DMA((2, N_DEV))],
                 name="allreduce_small")(buf)


def _block_diag(t, gt):
    G, A, B = t.shape
    t4 = t.reshape(G // gt, gt, A, B)
    eye = jnp.eye(gt, dtype=t.dtype)
    return jnp.einsum('sgab,gh->sgahb', t4, eye).reshape(G // gt, gt * A, gt * B)


def _block_diag_extract(m, gt, A, B):
    S = m.shape[0]
    m5 = m.reshape(S, gt, A, gt, B)
    eye = jnp.eye(gt, dtype=m.dtype)
    return jnp.einsum('sgahb,gh->sgab', m5, eye).reshape(S * gt, A, B)


def _pack_small(arrs, rows):
    flat = jnp.concatenate([a.reshape(-1).astype(F32) for a in arrs])
    return jnp.pad(flat, (0, rows * LANES - flat.shape[0])).reshape(rows, LANES)


def _unpack_small(buf, shapes):
    flat = buf.reshape(-1)
    out, off = [], 0
    for s in shapes:
        n = 1
        for d in s:
            n *= d
        out.append(flat[off:off + n].reshape(s))
        off += n
    return out


def kernel(x, p, norm_gain, w_in, w_pool, pool_scale, a_re, a_im, log_dt, b_re, b_im, c_re, c_im, d_skip, w_glu, w_out, w_ple, w_ple_gate, final_gain, loss_target, m_norm_gain, m_w_in, m_w_pool, m_pool_scale, m_a_re, m_a_im, m_log_dt, m_b_re, m_b_im, m_c_re, m_c_im, m_d_skip, m_w_glu, m_w_out, m_w_ple, m_w_ple_gate, m_final_gain, v_norm_gain, v_w_in, v_w_pool, v_pool_scale, v_a_re, v_a_im, v_log_dt, v_b_re, v_b_im, v_c_re, v_c_im, v_d_skip, v_w_glu, v_w_out, v_w_ple, v_w_ple_gate, v_final_gain):
    xs, pe, tgt = x[0], p[0, 0], loss_target[0]
    T, D = xs.shape
    E = pe.shape[1]
    P = D // 2
    NG = len(POOL_WINDOWS)
    PG = P // NG
    G, N, C = P // SSM_GROUP, SSM_STATE, SSM_GROUP
    GT = min(SSM_TILE_GROUPS, G)
    Q = D // N_CHIP

    big = {"w_in": (w_in, m_w_in, v_w_in), "w_pool": (w_pool, m_w_pool, v_w_pool),
           "w_glu": (w_glu, m_w_glu, v_w_glu), "w_out": (w_out, m_w_out, v_w_out),
           "w_ple": (w_ple, m_w_ple, v_w_ple), "w_ple_gate": (w_ple_gate, m_w_ple_gate, v_w_ple_gate)}
    big_names = list(big)
    shard2d = {n: (big[n][0].size // big[n][0].shape[-1], big[n][0].shape[-1]) for n in big_names}
    packed = jnp.concatenate([big[n][0].astype(BF16).reshape(-1, LANES) for n in big_names])
    gathered = _ag_weights(packed)
    full, off = {}, 0
    for n in big_names:
        r, c_ = shard2d[n]
        rows = r * c_ // LANES
        full[n] = gathered[:, off:off + rows].reshape(N_CHIP, r, c_)
        off += rows
    win_g, wglu_g, wple_g = full["w_in"], full["w_glu"], full["w_ple"]
    wout, wpg = full["w_out"].reshape(D, D), full["w_ple_gate"].reshape(D, D)
    wp = full["w_pool"].reshape(N_CHIP, NG, PG // N_CHIP, PG).transpose(1, 0, 2, 3).reshape(NG, PG, PG)

    rep = lambda a: jnp.repeat(a, C, axis=0)
    a_re_r, a_im_r = rep(a_re[0]), rep(a_im[0])
    ldt_r = rep(jnp.broadcast_to(log_dt[0][:, None], (G, N)))
    bt_re = b_re[0].transpose(0, 2, 1).reshape(G * C, N)
    bt_im = b_im[0].transpose(0, 2, 1).reshape(G * C, N)
    ab_re_r, ab_im_r, bbt_re, bbt_im = _ssm_prep(a_re_r, a_im_r, ldt_r, bt_re, bt_im)
    abr = ab_re_r[::C].reshape(1, G * N)
    abi = ab_im_r[::C].reshape(1, G * N)
    bdr = _block_diag(bbt_re.reshape(G, C, N), GT).astype(BF16)
    bdi = _block_diag(bbt_im.reshape(G, C, N), GT).astype(BF16)
    cdr = _block_diag(c_re[0].transpose(0, 2, 1), GT).astype(BF16)
    cdi = _block_diag(c_im[0].transpose(0, 2, 1), GT).astype(BF16)

    tb = _t(T, 256)
    tbs = _t(T, 256)
    tm = _t(T, 512)
    DH = _t(D, 1024)
    hn = _norm1(xs, norm_gain, tb)
    proj = _mm(hn, win_g, dims=NN, grid=(T // tm, N_CHIP, D // DH),
               a_spec=_bs((tm, DH), lambda i, j, k: (i, k)), b_spec=_bs((None, DH, P), lambda i, j, k: (j, k, 0)),
               o_spec=_bs((tm, P), lambda i, j, k: (i, j)), out_shape=jax.ShapeDtypeStruct((T, 4 * P), F32),
               name="mm_proj")
    pooled = _pool_fwd(proj, P, tb)
    mixed = _mm(pooled, wp, dims=NN, grid=(T // tm, NG, 1),
                a_spec=_bs((tm, PG), lambda i, g, k: (i, g)), b_spec=_bs((None, PG, PG), lambda i, g, k: (g, 0, 0)),
                o_spec=_bs((tm, PG), lambda i, g, k: (i, g)), out_shape=jax.ShapeDtypeStruct((T, P), F32),
                name="mm_pool")
    y, ge, bsr, bsi = _ssm_fwd(proj, bdr, bdi, cdr, cdi, abr, abi, d_skip, P, tbs)
    hg = _mm(ge, wglu_g, dims=NN, grid=(T // tm, N_CHIP, 1),
             a_spec=_bs((tm, P), lambda i, j, k: (i, 0)), b_spec=_bs((None, P, Q), lambda i, j, k: (j, 0, 0)),
             o_spec=_bs((tm, Q), lambda i, j, k: (i, j)), out_shape=jax.ShapeDtypeStruct((T, 2 * P), F32),
             name="mm_glu")
    cat = _gate_fwd(mixed, proj, hg, pool_scale, tb)
    h1 = _mm(cat, wout, dims=NN, grid=(T // tm, D // DH, D // DH), res=xs,
             a_spec=_bs((tm, DH), lambda i, n, k: (i, k)), b_spec=_bs((DH, DH), lambda i, n, k: (k, n)),
             r_spec=_bs((tm, DH), lambda i, n, k: (i, n)), o_spec=_bs((tm, DH), lambda i, n, k: (i, n)),
             out_shape=jax.ShapeDtypeStruct((T, D), F32), name="mm_out")
    e = _mm(pe, wple_g, dims=NN, grid=(T // tm, N_CHIP, 1),
            a_spec=_bs((tm, E), lambda i, j, k: (i, 0)), b_spec=_bs((None, E, Q), lambda i, j, k: (j, 0, 0)),
            o_spec=_bs((tm, Q), lambda i, j, k: (i, j)), out_shape=jax.ShapeDtypeStruct((T, D), F32),
            name="mm_ple")
    z = _mm(h1, wpg, dims=NN, grid=(T // tm, D // DH, D // DH),
            a_spec=_bs((tm, DH), lambda i, n, k: (i, k)), b_spec=_bs((DH, DH), lambda i, n, k: (k, n)),
            o_spec=_bs((tm, DH), lambda i, n, k: (i, n)), out_shape=jax.ShapeDtypeStruct((T, D), F32),
            name="mm_pgate")
    dh2, de, dz, dg2, lpart = _final_fb(h1, e, z, tgt, final_gain.reshape(1, D), tb)

    tk = _t(T, 512)
    dh1 = _mm(dz, wpg, dims=NT, grid=(T // tm, D // DH, D // DH), res=dh2,
              a_spec=_bs((tm, DH), lambda i, n, k: (i, k)), b_spec=_bs((DH, DH), lambda i, n, k: (n, k)),
              r_spec=_bs((tm, DH), lambda i, n, k: (i, n)), o_spec=_bs((tm, DH), lambda i, n, k: (i, n)),
              out_shape=jax.ShapeDtypeStruct((T, D), F32), name="mm_dh1")
    g_wpg = _mm(h1, dz, dims=TN, grid=(D // DH, D // DH, T // tk),
                a_spec=_bs((tk, DH), lambda m, n, k: (k, m)), b_spec=_bs((tk, DH), lambda m, n, k: (k, n)),
                o_spec=_bs((DH, DH), lambda m, n, k: (m, n)), out_shape=jax.ShapeDtypeStruct((D, D), F32),
                name="mm_gwpg")
    g_wple = _mm(pe, de, dims=TN, grid=(1, N_CHIP, T // tk),
                 a_spec=_bs((tk, E), lambda m, j, k: (k, 0)), b_spec=_bs((tk, Q), lambda m, j, k: (k, j)),
                 o_spec=_bs((None, E, Q), lambda m, j, k: (j, 0, 0)),
                 out_shape=jax.ShapeDtypeStruct((N_CHIP, E, Q), F32), name="mm_gwple")
    dcat = _mm(dh1, wout, dims=NT, grid=(T // tm, D // DH, D // DH),
               a_spec=_bs((tm, DH), lambda i, n, k: (i, k)), b_spec=_bs((DH, DH), lambda i, n, k: (n, k)),
               o_spec=_bs((tm, DH), lambda i, n, k: (i, n)), out_shape=jax.ShapeDtypeStruct((T, D), F32),
               name="mm_dcat")
    g_wout = _mm(cat, dh1, dims=TN, grid=(D // DH, D // DH, T // tk),
                 a_spec=_bs((tk, DH), lambda m, n, k: (k, m)), b_spec=_bs((tk, DH), lambda m, n, k: (k, n)),
                 o_spec=_bs((DH, DH), lambda m, n, k: (m, n)), out_shape=jax.ShapeDtypeStruct((D, D), F32),
                 name="mm_gwout")
    dmixed, dpg, dsg, dhg, dps = _gate_bwd(dcat, mixed, proj, hg, pool_scale, tb)
    dge = _mm(dhg, wglu_g, dims=NT, grid=(T // tm, 1, N_CHIP),
              a_spec=_bs((tm, Q), lambda i, n, k: (i, k)), b_spec=_bs((None, P, Q), lambda i, n, k: (k, 0, 0)),
              o_spec=_bs((tm, P), lambda i, n, k: (i, 0)), out_shape=jax.ShapeDtypeStruct((T, P), F32),
              name="mm_dge")
    g_wglu = _mm(ge, dhg, dims=TN, grid=(1, N_CHIP, T // tk),
                 a_spec=_bs((tk, P), lambda m, j, k: (k, 0)), b_spec=_bs((tk, Q), lambda m, j, k: (k, j)),
                 o_spec=_bs((None, P, Q), lambda m, j, k: (j, 0, 0)),
                 out_shape=jax.ShapeDtypeStruct((N_CHIP, P, Q), F32), name="mm_gwglu")
    du, dabr, dabi, dd, dbdr, dbdi, dcdr, dcdi = _ssm_bwd(proj, y, dge, bsr, bsi, bdr, bdi, cdr, cdi,
                                                          abr, abi, d_skip, P, tbs)
    dpooled = _mm(dmixed, wp, dims=NT, grid=(T // tm, NG, 1),
                  a_spec=_bs((tm, PG), lambda i, g, k: (i, g)), b_spec=_bs((None, PG, PG), lambda i, g, k: (g, 0, 0)),
                  o_spec=_bs((tm, PG), lambda i, g, k: (i, g)), out_shape=jax.ShapeDtypeStruct((T, P), F32),
                  name="mm_dpooled")
    g_wp = _mm(pooled, dmixed, dims=TN, grid=(NG, 1, T // tk),
               a_spec=_bs((tk, PG), lambda g, n, k: (k, g)), b_spec=_bs((tk, PG), lambda g, n, k: (k, g)),
               o_spec=_bs((None, PG, PG), lambda g, n, k: (g, 0, 0)),
               out_shape=jax.ShapeDtypeStruct((NG, PG, PG), F32), name="mm_gwp")
    dpi = _pool_bwd(dpooled, tb)
    dproj = jnp.concatenate([dpi, dpg, du, dsg], axis=1)
    dhn = _mm(dproj, win_g, dims=NT, grid=(T // tm, D // DH, N_CHIP),
              a_spec=_bs((tm, P), lambda i, n, k: (i, k)), b_spec=_bs((None, DH, P), lambda i, n, k: (k, n, 0)),
              o_spec=_bs((tm, DH), lambda i, n, k: (i, n)), out_shape=jax.ShapeDtypeStruct((T, D), F32),
              name="mm_dhn")
    g_win = _mm(hn, dproj, dims=TN, grid=(D // DH, N_CHIP, T // tk),
                a_spec=_bs((tk, DH), lambda m, j, k: (k, m)), b_spec=_bs((tk, P), lambda m, j, k: (k, j)),
                o_spec=_bs((None, DH, P), lambda m, j, k: (j, m, 0)),
                out_shape=jax.ShapeDtypeStruct((N_CHIP, D, P), F32), name="mm_gwin")
    grad_x, dg1 = _norm1_bwd(xs, dhn, dh1, norm_gain, tb)

    dbbt_re = _block_diag_extract(dbdr, GT, C, N).reshape(G * C, N)
    dbbt_im = _block_diag_extract(dbdi, GT, C, N).reshape(G * C, N)
    g_c_re = _block_diag_extract(dcdr, GT, N, C).transpose(0, 2, 1)
    g_c_im = _block_diag_extract(dcdi, GT, N, C).transpose(0, 2, 1)
    dab_re_r = rep(dabr.reshape(G, N)) * (1.0 / C)
    dab_im_r = rep(dabi.reshape(G, N)) * (1.0 / C)
    g_a_re, g_a_im, g_ldt, g_bt_re, g_bt_im = _ssm_prep_bwd(a_re_r, a_im_r, ldt_r, bt_re, bt_im,
                                                            dab_re_r, dab_im_r, dbbt_re, dbbt_im, G)
    g_b_re = g_bt_re.reshape(G, C, N).transpose(0, 2, 1)
    g_b_im = g_bt_im.reshape(G, C, N).transpose(0, 2, 1)

    gbig = {"w_in": g_win,
            "w_pool": g_wp.reshape(NG, N_CHIP, PG // N_CHIP, PG).transpose(1, 0, 2, 3).reshape(N_CHIP, NG * PG // N_CHIP, PG),
            "w_glu": g_wglu, "w_out": g_wout.reshape(N_CHIP, Q, D), "w_ple": g_wple,
            "w_ple_gate": g_wpg.reshape(N_CHIP, Q, D)}
    own, got = _halves_to_sibling([gbig[n] for n in big_names])
    chip_sums = [_sum_cast(o, g_, "sum_cast_" + n) for o, g_, n in zip(own, got, big_names)]
    arrived = _scatter_to_chips(chip_sums)
    halves = [_sum_chips(a, "sum_chips_" + n) for a, n in zip(arrived, big_names)]
    gshard = _join_halves(halves)

    small_names = ["norm_gain", "pool_scale", "a_re", "a_im", "log_dt", "b_re", "b_im", "c_re", "c_im",
                   "d_skip", "final_gain"]
    small_w = dict(norm_gain=norm_gain, pool_scale=pool_scale, a_re=a_re, a_im=a_im, log_dt=log_dt, b_re=b_re,
                   b_im=b_im, c_re=c_re, c_im=c_im, d_skip=d_skip, final_gain=final_gain)
    small_m = dict(norm_gain=m_norm_gain, pool_scale=m_pool_scale, a_re=m_a_re, a_im=m_a_im, log_dt=m_log_dt,
                   b_re=m_b_re, b_im=m_b_im, c_re=m_c_re, c_im=m_c_im, d_skip=m_d_skip, final_gain=m_final_gain)
    small_v = dict(norm_gain=v_norm_gain, pool_scale=v_pool_scale, a_re=v_a_re, a_im=v_a_im, log_dt=v_log_dt,
                   b_re=v_b_re, b_im=v_b_im, c_re=v_c_re, c_im=v_c_im, d_skip=v_d_skip, final_gain=v_final_gain)
    small_g = dict(norm_gain=dg1, pool_scale=dps, a_re=g_a_re, a_im=g_a_im, log_dt=g_ldt, b_re=g_b_re,
                   b_im=g_b_im, c_re=g_c_re, c_im=g_c_im, d_skip=dd, final_gain=dg2)
    shapes = [small_w[n].shape for n in small_names]
    total = sum(small_w[n].size for n in small_names) + 1
    unit = N_DEV * SUBLANES
    rows = -(-(-(-total // LANES)) // unit) * unit
    gbuf = _pack_small([small_g[n] for n in small_names] + [lpart[0, :1]], rows)
    gsum = _allreduce_small(gbuf)
    wbuf = _pack_small([small_w[n] for n in small_names], rows)
    mbuf = _pack_small([small_m[n] for n in small_names], rows)
    vbuf = _pack_small([small_v[n] for n in small_names], rows)
    dsm, msm, vsm = _adamw(wbuf, gsum, mbuf, vbuf, "adamw_small")
    g_small = dict(zip(small_names, _unpack_small(gsum, shapes)))
    d_small = dict(zip(small_names, _unpack_small(dsm, shapes)))
    m_small = dict(zip(small_names, _unpack_small(msm, shapes)))
    v_small = dict(zip(small_names, _unpack_small(vsm, shapes)))
    loss = gsum.reshape(-1)[total - 1]

    g_out, d_out, m_out, v_out = dict(g_small), dict(d_small), dict(m_small), dict(v_small)
    for n, gs in zip(big_names, gshard):
        w_, m_, v_ = big[n]
        r2 = shard2d[n]
        d_, mn_, vn_ = _adamw(w_.reshape(r2), gs, m_.reshape(r2), v_.reshape(r2), "adamw_" + n)
        g_out[n], d_out[n], m_out[n], v_out[n] = (a.reshape(w_.shape) for a in (gs, d_, mn_, vn_))

    order = ["norm_gain", "w_in", "w_pool", "pool_scale", "a_re", "a_im", "log_dt", "b_re", "b_im", "c_re",
             "c_im", "d_skip", "w_glu", "w_out", "w_ple", "w_ple_gate", "final_gain"]
    return (loss, grad_x[None], *[g_out[n] for n in order], *[d_out[n] for n in order],
            *[m_out[n] for n in order], *[v_out[n] for n in order])
```

```python
import functools

import jax
import jax.numpy as jnp
from jax import lax
from jax.experimental import pallas as pl
from jax.experimental.pallas import tpu as pltpu

F32, BF16 = jnp.float32, jnp.bfloat16
MESH = pl.DeviceIdType.MESH
ANY = pl.BlockSpec(memory_space=pl.ANY)
VMEM_FULL = pl.BlockSpec(memory_space=pltpu.VMEM)

EPS = 1e-6
A_RE_MAX = -1e-4
SSM_GROUP = 16
SSM_STATE = 64
POOL_WINDOWS = (2, 4, 8, 16)
POOL_HALO = 16
ADAM_LR, ADAM_B1, ADAM_B2, ADAM_EPS, ADAM_WD, ADAM_STEP = 0.001, 0.9, 0.999, 1e-08, 0.01, 10

V7X_VMEM_BYTES = 64 * 1024 * 1024
VMEM_LIMIT = V7X_VMEM_BYTES - 8 * 1024 * 1024
SUBLANES, LANES = 8, 128
SSM_TILE_GROUPS = 16
SCAN_LANES = 512
N_DEV, N_CHIP = 8, 4
DMA_CHUNK_BYTES = 256 * 1024
DMA_MAX_CHUNKS = 32
AG_CHUNKS = 8
RS_CHUNKS = 8


def _t(n, pref):
    return pref if n % pref == 0 else n


def _cp(sem=None, vmem=VMEM_LIMIT):
    return pltpu.CompilerParams(dimension_semantics=sem, vmem_limit_bytes=vmem)


def _call(body, **kw):
    return pl.pallas_call(body, **kw)


NN = ((1,), (0,))
NT = ((1,), (1,))
TN = ((0,), (0,))


def _mm(a, b, *, dims, grid, a_spec, b_spec, o_spec, out_shape, name, res=None, r_spec=None):
    nk, kax = grid[-1], len(grid) - 1
    acc_shape = tuple(d for d in o_spec.block_shape if d is not None)

    def body(*refs):
        if res is None:
            a_ref, b_ref, o_ref, acc = refs
            r_ref = None
        else:
            a_ref, b_ref, r_ref, o_ref, acc = refs
        k = pl.program_id(kax)

        @pl.when(k == 0)
        def _():
            acc[...] = jnp.zeros_like(acc)

        acc[...] += lax.dot_general(a_ref[...].astype(BF16), b_ref[...].astype(BF16),
                                    (dims, ((), ())), preferred_element_type=F32)

        @pl.when(k == nk - 1)
        def _():
            r = acc[...]
            if r_ref is not None:
                r = r + r_ref[...]
            o_ref[...] = r.astype(o_ref.dtype)

    ins, specs = [a, b], [a_spec, b_spec]
    if res is not None:
        ins.append(res)
        specs.append(r_spec)
    sem = ("parallel",) * kax + ("arbitrary",)
    return _call(body, grid=grid, in_specs=specs, out_specs=o_spec, out_shape=out_shape,
                 scratch_shapes=[pltpu.VMEM(acc_shape, F32)], compiler_params=_cp(sem), name=name)(*ins)


def _bs(shape, fn):
    return pl.BlockSpec(shape, fn)


def _sigmoid(v):
    return 1.0 / (1.0 + jnp.exp(-v))


def _gelu(v):
    return 0.5 * v * (1.0 + jnp.tanh(0.7978845608028654 * (v + 0.044715 * v * v * v)))


def _gelu_grad(v):
    t = jnp.tanh(0.7978845608028654 * (v + 0.044715 * v * v * v))
    return 0.5 * (1.0 + t) + 0.5 * v * (1.0 - t * t) * 0.7978845608028654 * (1.0 + 3 * 0.044715 * v * v)


def _norm1(x, g1, tb):
    T, D = x.shape

    def body(x_ref, g_ref, o_ref):
        xv = x_ref[...]
        r = lax.rsqrt(jnp.mean(xv * xv, axis=-1, keepdims=True) + EPS)
        o_ref[...] = ((xv * r) * g_ref[...]).astype(BF16)

    return _call(body, grid=(T // tb,),
                 in_specs=[_bs((tb, D), lambda i: (i, 0)), _bs((1, D), lambda i: (0, 0))],
                 out_specs=_bs((tb, D), lambda i: (i, 0)), out_shape=jax.ShapeDtypeStruct((T, D), BF16),
                 compiler_params=_cp(("parallel",)), name="norm1")(x, g1)


def _norm1_bwd(x, dhn, dh1, g1, tb):
    T, D = x.shape

    def body(x_ref, dhn_ref, dh1_ref, g_ref, dx_ref, dg_ref):
        @pl.when(pl.program_id(0) == 0)
        def _():
            dg_ref[...] = jnp.zeros_like(dg_ref)

        xv = x_ref[...]
        r = lax.rsqrt(jnp.mean(xv * xv, axis=-1, keepdims=True) + EPS)
        xh = xv * r
        dhn_v = dhn_ref[...]
        dg_ref[...] += jnp.sum(dhn_v * xh, axis=0, keepdims=True)
        dxh = dhn_v * g_ref[...]
        dx_ref[...] = dh1_ref[...] + r * (dxh - xh * jnp.mean(dxh * xh, axis=-1, keepdims=True))

    row = _bs((tb, D), lambda i: (i, 0))
    vec = _bs((1, D), lambda i: (0, 0))
    return _call(body, grid=(T // tb,), in_specs=[row, row, row, vec], out_specs=[row, vec],
                 out_shape=[jax.ShapeDtypeStruct((T, D), F32), jax.ShapeDtypeStruct((1, D), F32)],
                 compiler_params=_cp(("arbitrary",)), name="norm1_bwd")(x, dhn, dh1, g1)


def _gate_fwd(mixed, proj, hg, ps, tb):
    T, P = mixed.shape

    def body(mx_ref, pg_ref, sg_ref, hg_ref, ps_ref, o_ref):
        pg, sg = pg_ref[...], sg_ref[...]
        ya = (mx_ref[...] * ps_ref[...]) * (pg * _sigmoid(pg))
        hgv = hg_ref[...]
        o = hgv[:, :P] * _sigmoid(hgv[:, P:])
        yb = o * (sg * _sigmoid(sg))
        o_ref[:, :P] = ya.astype(BF16)
        o_ref[:, P:] = yb.astype(BF16)

    return _call(body, grid=(T // tb,),
                 in_specs=[_bs((tb, P), lambda i: (i, 0)), _bs((tb, P), lambda i: (i, 1)),
                           _bs((tb, P), lambda i: (i, 3)), _bs((tb, 2 * P), lambda i: (i, 0)),
                           _bs((1, P), lambda i: (0, 0))],
                 out_specs=_bs((tb, 2 * P), lambda i: (i, 0)),
                 out_shape=jax.ShapeDtypeStruct((T, 2 * P), BF16),
                 compiler_params=_cp(("parallel",)), name="gate_fwd")(mixed, proj, proj, hg, ps)


def _gate_bwd(dcat, mixed, proj, hg, ps, tb):
    T, P = mixed.shape

    def body(dc_ref, mx_ref, pg_ref, sg_ref, hg_ref, ps_ref, dmx_ref, dpg_ref, dsg_ref, dhg_ref, dps_ref):
        @pl.when(pl.program_id(0) == 0)
        def _():
            dps_ref[...] = jnp.zeros_like(dps_ref)

        dc = dc_ref[...]
        dya, dyb = dc[:, :P], dc[:, P:]
        pg, sg, mx, psv = pg_ref[...], sg_ref[...], mx_ref[...], ps_ref[...]
        s_pg = _sigmoid(pg)
        dpa = dya * (pg * s_pg)
        dpg_ref[...] = (dya * (mx * psv) * (s_pg * (1.0 + pg * (1.0 - s_pg)))).astype(BF16)
        dps_ref[...] += jnp.sum(dpa * mx, axis=0, keepdims=True)
        dmx_ref[...] = (dpa * psv).astype(BF16)
        hgv = hg_ref[...]
        h1, s_h2 = hgv[:, :P], _sigmoid(hgv[:, P:])
        s_sg = _sigmoid(sg)
        do = dyb * (sg * s_sg)
        dsg_ref[...] = (dyb * (h1 * s_h2) * (s_sg * (1.0 + sg * (1.0 - s_sg)))).astype(BF16)
        dhg_ref[:, :P] = (do * s_h2).astype(BF16)
        dhg_ref[:, P:] = (do * h1 * s_h2 * (1.0 - s_h2)).astype(BF16)

    rowp = _bs((tb, P), lambda i: (i, 0))
    row2 = _bs((tb, 2 * P), lambda i: (i, 0))
    vec = _bs((1, P), lambda i: (0, 0))
    return _call(body, grid=(T // tb,),
                 in_specs=[row2, rowp, _bs((tb, P), lambda i: (i, 1)), _bs((tb, P), lambda i: (i, 3)), row2, vec],
                 out_specs=[rowp, rowp, rowp, row2, vec],
                 out_shape=[jax.ShapeDtypeStruct((T, P), BF16), jax.ShapeDtypeStruct((T, P), BF16),
                            jax.ShapeDtypeStruct((T, P), BF16), jax.ShapeDtypeStruct((T, 2 * P), BF16),
                            jax.ShapeDtypeStruct((1, P), F32)],
                 compiler_params=_cp(("arbitrary",)), name="gate_bwd")(dcat, mixed, proj, proj, hg, ps)


def _final_fb(h1, e, z, tgt, g2, tb):
    T, D = h1.shape

    def body(h1_ref, e_ref, z_ref, t_ref, g_ref, dh2_ref, de_ref, dz_ref, dg_ref, l_ref):
        @pl.when(pl.program_id(0) == 0)
        def _():
            dg_ref[...] = jnp.zeros_like(dg_ref)
            l_ref[...] = jnp.zeros_like(l_ref)

        ev = e_ref[...]
        s = _sigmoid(z_ref[...])
        h2 = h1_ref[...] + ev * s
        r = lax.rsqrt(jnp.mean(h2 * h2, axis=-1, keepdims=True) + EPS)
        xh = h2 * r
        gv = g_ref[...]
        diff = xh * gv - t_ref[...]
        l_ref[...] += 0.5 * jnp.sum(jnp.mean(diff * diff, axis=-1, keepdims=True))
        dout = diff * (1.0 / D)
        dg_ref[...] += jnp.sum(dout * xh, axis=0, keepdims=True)
        dxh = dout * gv
        dh2 = r * (dxh - xh * jnp.mean(dxh * xh, axis=-1, keepdims=True))
        dh2_ref[...] = dh2
        de_ref[...] = (dh2 * s).astype(BF16)
        dz_ref[...] = (dh2 * ev * s * (1.0 - s)).astype(BF16)

    row = _bs((tb, D), lambda i: (i, 0))
    vec = _bs((1, D), lambda i: (0, 0))
    return _call(body, grid=(T // tb,), in_specs=[row, row, row, row, vec],
                 out_specs=[row, row, row, vec, _bs((1, LANES), lambda i: (0, 0))],
                 out_shape=[jax.ShapeDtypeStruct((T, D), F32), jax.ShapeDtypeStruct((T, D), BF16),
                            jax.ShapeDtypeStruct((T, D), BF16), jax.ShapeDtypeStruct((1, D), F32),
                            jax.ShapeDtypeStruct((1, LANES), F32)],
                 compiler_params=_cp(("arbitrary",)), name="final_fb")(h1, e, z, tgt, g2)


def _pool_inv_count(t0, rows, pg, ngroups):
    t = t0 + lax.broadcasted_iota(jnp.int32, (rows, pg), 0)
    parts = []
    for w in POOL_WINDOWS[:ngroups]:
        parts.append(jnp.where(t + 1 >= w, 1.0 / w, 1.0 / (t + 1).astype(F32)))
    return parts


def _pool_fwd(proj, P, tb):
    T = proj.shape[0]
    ng = len(POOL_WINDOWS)
    pg = P // ng
    hb = tb // POOL_HALO

    def body(v_ref, tail_ref, o_ref, ext):
        i = pl.program_id(0)
        ext[pl.ds(0, POOL_HALO), :] = jnp.where(i > 0, tail_ref[...], 0.0)
        ext[pl.ds(POOL_HALO, tb), :] = v_ref[...]
        inv = _pool_inv_count(i * tb, tb, pg, ng)
        for g, w in enumerate(POOL_WINDOWS):
            cols = pl.ds(g * pg, pg)
            win = ext[pl.ds(POOL_HALO, tb), cols]
            for k in range(1, w):
                win = win + ext[pl.ds(POOL_HALO - k, tb), cols]
            o_ref[:, cols] = (win * inv[g] - ext[pl.ds(POOL_HALO, tb), cols]).astype(BF16)

    return _call(body, grid=(T // tb,),
                 in_specs=[_bs((tb, P), lambda i: (i, 0)),
                           _bs((POOL_HALO, P), lambda i: (jnp.maximum(i * hb - 1, 0), 0))],
                 out_specs=_bs((tb, P), lambda i: (i, 0)), out_shape=jax.ShapeDtypeStruct((T, P), BF16),
                 scratch_shapes=[pltpu.VMEM((tb + POOL_HALO, P), F32)],
                 compiler_params=_cp(("arbitrary",)), name="pool_fwd")(proj, proj)


def _pool_bwd(dpooled, tb):
    T, P = dpooled.shape
    ng = len(POOL_WINDOWS)
    pg = P // ng
    hb = tb // POOL_HALO
    nb = T // tb

    def body(d_ref, head_ref, o_ref, ext):
        i = pl.program_id(0)
        inv = _pool_inv_count(i * tb, tb, pg, ng)
        invh = _pool_inv_count((i + 1) * tb, POOL_HALO, pg, ng)
        for g in range(ng):
            cols = pl.ds(g * pg, pg)
            ext[pl.ds(0, tb), cols] = d_ref[:, cols] * inv[g]
            ext[pl.ds(tb, POOL_HALO), cols] = jnp.where(i < nb - 1, head_ref[:, cols] * invh[g], 0.0)
        for g, w in enumerate(POOL_WINDOWS):
            cols = pl.ds(g * pg, pg)
            acc = ext[pl.ds(0, tb), cols]
            for k in range(1, w):
                acc = acc + ext[pl.ds(k, tb), cols]
            o_ref[:, cols] = (acc - d_ref[:, cols]).astype(BF16)

    return _call(body, grid=(nb,),
                 in_specs=[_bs((tb, P), lambda i: (i, 0)),
                           _bs((POOL_HALO, P), lambda i: (jnp.minimum((i + 1) * hb, T // POOL_HALO - 1), 0))],
                 out_specs=_bs((tb, P), lambda i: (i, 0)), out_shape=jax.ShapeDtypeStruct((T, P), BF16),
                 scratch_shapes=[pltpu.VMEM((tb + POOL_HALO, P), F32)],
                 compiler_params=_cp(("arbitrary",)), name="pool_bwd")(dpooled, dpooled)


def _zoh(a_re, a_im, ldt, b_re, b_im):
    lam_re = jnp.minimum(a_re, A_RE_MAX)
    lam_im = a_im
    dt = jnp.exp(ldt)
    mag = jnp.exp(lam_re * dt)
    ang = lam_im * dt
    ab_re = mag * jnp.cos(ang)
    ab_im = mag * jnp.sin(ang)
    den = lam_re * lam_re + lam_im * lam_im
    n_re = ab_re - 1.0
    n_im = ab_im
    q_re = (n_re * lam_re + n_im * lam_im) / den
    q_im = (n_im * lam_re - n_re * lam_im) / den
    return ab_re, ab_im, q_re * b_re - q_im * b_im, q_re * b_im + q_im * b_re


def _ssm_prep(a_re, a_im, ldt, bt_re, bt_im):
    shp = jax.ShapeDtypeStruct(a_re.shape, F32)

    def body(a, b, c, d, e, o0, o1, o2, o3):
        r = _zoh(a[...], b[...], c[...], d[...], e[...])
        o0[...], o1[...], o2[...], o3[...] = r

    return _call(body, in_specs=[VMEM_FULL] * 5, out_specs=[VMEM_FULL] * 4, out_shape=[shp] * 4,
                 name="ssm_prep")(a_re, a_im, ldt, bt_re, bt_im)


def _ssm_prep_bwd(a_re, a_im, ldt, bt_re, bt_im, dab_re, dab_im, dbb_re, dbb_im, G):
    GC, N = a_re.shape
    C = GC // G

    def body(a, b, c, d, e, g0, g1, g2, g3, da_re, da_im, dldt, db_re, db_im):
        _, vjp = jax.vjp(_zoh, a[...], b[...], c[...], d[...], e[...])
        ga_re, ga_im, gl, gb_re, gb_im = vjp((g0[...], g1[...], g2[...], g3[...]))
        da_re[...] = jnp.sum(ga_re.reshape(G, C, N), axis=1)
        da_im[...] = jnp.sum(ga_im.reshape(G, C, N), axis=1)
        dldt[...] = jnp.sum(jnp.sum(gl.reshape(G, C, N), axis=1), axis=1, keepdims=True)
        db_re[...] = gb_re
        db_im[...] = gb_im

    gn = jax.ShapeDtypeStruct((G, N), F32)
    full = jax.ShapeDtypeStruct((GC, N), F32)
    return _call(body, in_specs=[VMEM_FULL] * 9, out_specs=[VMEM_FULL] * 5,
                 out_shape=[gn, gn, jax.ShapeDtypeStruct((G, 1), F32), full, full],
                 name="ssm_prep_bwd")(a_re, a_im, ldt, bt_re, bt_im, dab_re, dab_im, dbb_re, dbb_im)


def _coef_tiles(abr, abi, reverse):
    ns = abr.shape[1]
    row = lax.broadcasted_iota(jnp.int32, (SUBLANES, ns), 0)
    ar = jnp.broadcast_to(abr, (SUBLANES, ns))
    ai = jnp.broadcast_to(-abi if reverse else abi, (SUBLANES, ns))
    a2r, a2i = ar * ar - ai * ai, 2.0 * ar * ai
    a4r, a4i = a2r * a2r - a2i * a2i, 2.0 * a2r * a2i
    out = []
    for d, (vr, vi) in ((1, (ar, ai)), (2, (a2r, a2i)), (4, (a4r, a4i))):
        keep = (row < SUBLANES - d) if reverse else (row >= d)
        out += [jnp.where(keep, vr, 0.0), jnp.where(keep, vi, 0.0)]
    pr, pi = ar, ai
    for k in range(1, SUBLANES):
        sel = (row <= SUBLANES - 1 - k) if reverse else (row >= k)
        nr, ni = pr * ar - pi * ai, pr * ai + pi * ar
        pr, pi = jnp.where(sel, nr, pr), jnp.where(sel, ni, pi)
    return out + [pr, pi]


def _scan_block(xr_ref, xi_ref, coef_ref, car_ref, cai_ref, *, row0, nrows, ns, reverse):
    ntile = nrows // SUBLANES
    cw = min(SCAN_LANES, ns)
    edge = 0 if reverse else SUBLANES - 1
    for cc in range(ns // cw):
        cols = pl.ds(cc * cw, cw)
        co = [coef_ref[k, :, cols] for k in range(8)]

        def step(r, carry, cols=cols, co=co):
            cr, ci = carry
            rr = (ntile - 1 - r) if reverse else r
            rows = pl.ds(pl.multiple_of(row0 + rr * SUBLANES, SUBLANES), SUBLANES)
            xr, xi = xr_ref[rows, cols], xi_ref[rows, cols]
            for lvl, d in enumerate((1, 2, 4)):
                kr, ki = co[2 * lvl], co[2 * lvl + 1]
                sh = SUBLANES - d if reverse else d
                sr, si = pltpu.roll(xr, sh, 0), pltpu.roll(xi, sh, 0)
                xr, xi = xr + (kr * sr - ki * si), xi + (kr * si + ki * sr)
            xr, xi = xr + (co[6] * cr - co[7] * ci), xi + (co[6] * ci + co[7] * cr)
            xr_ref[rows, cols] = xr
            xi_ref[rows, cols] = xi
            return (jnp.broadcast_to(xr[edge:edge + 1, :], xr.shape),
                    jnp.broadcast_to(xi[edge:edge + 1, :], xi.shape))

        cr, ci = lax.fori_loop(0, ntile, step, (car_ref[:, cols], cai_ref[:, cols]))
        car_ref[:, cols] = cr
        cai_ref[:, cols] = ci


def _ssm_fwd(proj, bdr, bdi, cdr, cdi, abr, abi, dsk, P, tb):
    T = proj.shape[0]
    ntl, ct, st = bdr.shape
    ns = ntl * st
    nb = T // tb

    def body(u_ref, bdr_ref, bdi_ref, cdr_ref, cdi_ref, abr_ref, abi_ref, d_ref,
             y_ref, ge_ref, bsr_ref, bsi_ref, sr, si, coef, car, cai):
        @pl.when(pl.program_id(0) == 0)
        def _():
            for k, tile in enumerate(_coef_tiles(abr_ref[...], abi_ref[...], False)):
                coef[k] = tile
            car[...] = jnp.zeros_like(car)
            cai[...] = jnp.zeros_like(cai)

        bsr_ref[...] = car[...]
        bsi_ref[...] = cai[...]
        u = u_ref[...]
        ub = u.astype(BF16)
        for s in range(ntl):
            us = ub[:, s * ct:(s + 1) * ct]
            sr[:, s * st:(s + 1) * st] = jnp.dot(us, bdr_ref[s], preferred_element_type=F32)
            si[:, s * st:(s + 1) * st] = jnp.dot(us, bdi_ref[s], preferred_element_type=F32)
        _scan_block(sr, si, coef, car, cai, row0=0, nrows=tb, ns=ns, reverse=False)
        for s in range(ntl):
            s_re = sr[:, s * st:(s + 1) * st].astype(BF16)
            s_im = si[:, s * st:(s + 1) * st].astype(BF16)
            y = (jnp.dot(s_re, cdr_ref[s], preferred_element_type=F32)
                 - jnp.dot(s_im, cdi_ref[s], preferred_element_type=F32)
                 + d_ref[:, s * ct:(s + 1) * ct] * u[:, s * ct:(s + 1) * ct])
            y_ref[:, s * ct:(s + 1) * ct] = y
            ge_ref[:, s * ct:(s + 1) * ct] = _gelu(y).astype(BF16)

    full3 = lambda a: _bs(a.shape, lambda i: (0, 0, 0))
    vec = lambda n: _bs((1, n), lambda i: (0, 0))
    row = _bs((tb, P), lambda i: (i, 0))
    st_spec = _bs((None, SUBLANES, ns), lambda i: (i, 0, 0))
    return _call(body, grid=(nb,),
                 in_specs=[_bs((tb, P), lambda i: (i, 2)), full3(bdr), full3(bdi), full3(cdr), full3(cdi),
                           vec(ns), vec(ns), vec(P)],
                 out_specs=[row, row, st_spec, st_spec],
                 out_shape=[jax.ShapeDtypeStruct((T, P), F32), jax.ShapeDtypeStruct((T, P), BF16),
                            jax.ShapeDtypeStruct((nb, SUBLANES, ns), F32),
                            jax.ShapeDtypeStruct((nb, SUBLANES, ns), F32)],
                 scratch_shapes=[pltpu.VMEM((tb, ns), F32), pltpu.VMEM((tb, ns), F32),
                                 pltpu.VMEM((8, SUBLANES, ns), F32),
                                 pltpu.VMEM((SUBLANES, ns), F32), pltpu.VMEM((SUBLANES, ns), F32)],
                 compiler_params=_cp(("arbitrary",)), name="ssm_fwd")(proj, bdr, bdi, cdr, cdi, abr, abi, dsk)


def _ssm_bwd(proj, y, dge, bsr, bsi, bdr, bdi, cdr, cdi, abr, abi, dsk, P, tb):
    T = proj.shape[0]
    ntl, ct, st = bdr.shape
    ns = ntl * st
    nb = T // tb
    pad = SUBLANES

    def body(u_ref, y_ref, dge_ref, bsr_ref, bsi_ref, abr_ref, abi_ref, d_ref, bdr_h, bdi_h, cdr_h, cdi_h,
             du_ref, dabr_ref, dabi_ref, dd_ref, dbdr_h, dbdi_h, dcdr_h, dcdi_h,
             wbdr, wbdi, wcdr, wcdi, abdr, abdi, acdr, acdi, spr, spi, gr, gi, coef_f, coef_r,
             car, cai, gcr, gci):
        i = pl.program_id(0)

        @pl.when(i == 0)
        def _():
            for h, w in ((bdr_h, wbdr), (bdi_h, wbdi), (cdr_h, wcdr), (cdi_h, wcdi)):
                pltpu.sync_copy(h, w)
            for a in (abdr, abdi, acdr, acdi, gcr, gci):
                a[...] = jnp.zeros_like(a)
            for o in (dabr_ref, dabi_ref, dd_ref):
                o[...] = jnp.zeros_like(o)
            for k, tile in enumerate(_coef_tiles(abr_ref[...], abi_ref[...], False)):
                coef_f[k] = tile
            for k, tile in enumerate(_coef_tiles(abr_ref[...], abi_ref[...], True)):
                coef_r[k] = tile

        car[...] = bsr_ref[...]
        cai[...] = bsi_ref[...]
        spr[pl.ds(0, pad), :] = bsr_ref[...]
        spi[pl.ds(0, pad), :] = bsi_ref[...]
        u = u_ref[...]
        ub = u.astype(BF16)
        for s in range(ntl):
            us = ub[:, s * ct:(s + 1) * ct]
            spr[pl.ds(pad, tb), s * st:(s + 1) * st] = jnp.dot(us, wbdr[s], preferred_element_type=F32)
            spi[pl.ds(pad, tb), s * st:(s + 1) * st] = jnp.dot(us, wbdi[s], preferred_element_type=F32)
        _scan_block(spr, spi, coef_f, car, cai, row0=pad, nrows=tb, ns=ns, reverse=False)

        dy = dge_ref[...] * _gelu_grad(y_ref[...])
        dyb = dy.astype(BF16)
        for s in range(ntl):
            dys = dyb[:, s * ct:(s + 1) * ct]
            gr[:, s * st:(s + 1) * st] = lax.dot_general(dys, wcdr[s], (NT, ((), ())), preferred_element_type=F32)
            gi[:, s * st:(s + 1) * st] = -lax.dot_general(dys, wcdi[s], (NT, ((), ())), preferred_element_type=F32)
        _scan_block(gr, gi, coef_r, gcr, gci, row0=0, nrows=tb, ns=ns, reverse=True)

        cw = min(SCAN_LANES, ns)
        for cc in range(ns // cw):
            cols = pl.ds(cc * cw, cw)
            pr, pi = spr[pl.ds(pad - 1, tb), cols], spi[pl.ds(pad - 1, tb), cols]
            g_r, g_i = gr[:, cols], gi[:, cols]
            dabr_ref[:, cols] += jnp.sum(g_r * pr + g_i * pi, axis=0, keepdims=True)
            dabi_ref[:, cols] += jnp.sum(g_i * pr - g_r * pi, axis=0, keepdims=True)

        for s in range(ntl):
            sl_c, sl_s = slice(s * ct, (s + 1) * ct), slice(s * st, (s + 1) * st)
            s_re = spr[pl.ds(pad, tb), sl_s].astype(BF16)
            s_im = spi[pl.ds(pad, tb), sl_s].astype(BF16)
            g_re, g_im = gr[:, sl_s].astype(BF16), gi[:, sl_s].astype(BF16)
            dys, us = dyb[:, sl_c], ub[:, sl_c]
            acdr[s] += lax.dot_general(s_re, dys, (TN, ((), ())), preferred_element_type=F32)
            acdi[s] -= lax.dot_general(s_im, dys, (TN, ((), ())), preferred_element_type=F32)
            abdr[s] += lax.dot_general(us, g_re, (TN, ((), ())), preferred_element_type=F32)
            abdi[s] += lax.dot_general(us, g_im, (TN, ((), ())), preferred_element_type=F32)
            du = (lax.dot_general(g_re, wbdr[s], (NT, ((), ())), preferred_element_type=F32)
                  + lax.dot_general(g_im, wbdi[s], (NT, ((), ())), preferred_element_type=F32)
                  + d_ref[:, sl_c] * dy[:, sl_c])
            du_ref[:, sl_c] = du.astype(BF16)
        dd_ref[...] += jnp.sum(dy * u, axis=0, keepdims=True)

        @pl.when(i == nb - 1)
        def _():
            for a, h in ((abdr, dbdr_h), (abdi, dbdi_h), (acdr, dcdr_h), (acdi, dcdi_h)):
                pltpu.sync_copy(a, h)

    rev = lambda i: nb - 1 - i
    vec = lambda n: _bs((1, n), lambda i: (0, 0))
    row = _bs((tb, P), lambda i: (rev(i), 0))
    st_spec = _bs((None, SUBLANES, ns), lambda i: (rev(i), 0, 0))
    bshape = jax.ShapeDtypeStruct(bdr.shape, F32)
    cshape = jax.ShapeDtypeStruct(cdr.shape, F32)
    return _call(body, grid=(nb,),
                 in_specs=[_bs((tb, P), lambda i: (rev(i), 2)), row, row, st_spec, st_spec,
                           vec(ns), vec(ns), vec(P), ANY, ANY, ANY, ANY],
                 out_specs=[row, vec(ns), vec(ns), vec(P), ANY, ANY, ANY, ANY],
                 out_shape=[jax.ShapeDtypeStruct((T, P), BF16), jax.ShapeDtypeStruct((1, ns), F32),
                            jax.ShapeDtypeStruct((1, ns), F32), jax.ShapeDtypeStruct((1, P), F32),
                            bshape, bshape, cshape, cshape],
                 scratch_shapes=[pltpu.VMEM(bdr.shape, BF16), pltpu.VMEM(bdr.shape, BF16),
                                 pltpu.VMEM(cdr.shape, BF16), pltpu.VMEM(cdr.shape, BF16),
                                 pltpu.VMEM(bdr.shape, F32), pltpu.VMEM(bdr.shape, F32),
                                 pltpu.VMEM(cdr.shape, F32), pltpu.VMEM(cdr.shape, F32),
                                 pltpu.VMEM((tb + pad, ns), F32), pltpu.VMEM((tb + pad, ns), F32),
                                 pltpu.VMEM((tb, ns), F32), pltpu.VMEM((tb, ns), F32),
                                 pltpu.VMEM((8, SUBLANES, ns), F32), pltpu.VMEM((8, SUBLANES, ns), F32),
                                 pltpu.VMEM((SUBLANES, ns), F32), pltpu.VMEM((SUBLANES, ns), F32),
                                 pltpu.VMEM((SUBLANES, ns), F32), pltpu.VMEM((SUBLANES, ns), F32)],
                 compiler_params=_cp(("arbitrary",)), name="ssm_bwd")(
                     proj, y, dge, bsr, bsi, abr, abi, dsk, bdr, bdi, cdr, cdi)


def _adamw(w, g, m, v, name):
    R, C = w.shape
    tr = _t(R, 256)

    def body(w_ref, g_ref, m_ref, v_ref, d_ref, mo_ref, vo_ref):
        gv = g_ref[...]
        mn = ADAM_B1 * m_ref[...] + (1.0 - ADAM_B1) * gv
        vn = ADAM_B2 * v_ref[...] + (1.0 - ADAM_B2) * (gv * gv)
        m_hat = mn / (1.0 - ADAM_B1 ** ADAM_STEP)
        v_hat = vn / (1.0 - ADAM_B2 ** ADAM_STEP)
        d_ref[...] = -ADAM_LR * (m_hat / (jnp.sqrt(v_hat) + ADAM_EPS) + ADAM_WD * w_ref[...])
        mo_ref[...] = mn
        vo_ref[...] = vn

    blk = _bs((tr, C), lambda i: (i, 0))
    shp = jax.ShapeDtypeStruct((R, C), F32)
    return _call(body, grid=(R // tr,), in_specs=[blk] * 4, out_specs=[blk] * 3, out_shape=[shp] * 3,
                 compiler_params=_cp(("parallel",)), name=name)(w, g, m, v)


def _sum_cast(grad, got, place, name):
    J, H, C = got.shape
    tr = _t(H, 256)
    nb = H // tr

    def body(pl_ref, a_ref, b_ref, o_ref):
        o_ref[...] = (a_ref[...] + b_ref[...]).astype(BF16)

    blk = _bs((None, tr, C), lambda j, i, pc: (j, i, 0))
    mine = _bs((None, tr, C), lambda j, i, pc: (j, pc[1] * nb + i, 0))
    spec = pltpu.PrefetchScalarGridSpec(num_scalar_prefetch=1, grid=(J, nb), in_specs=[mine, blk], out_specs=blk)
    return _call(body, grid_spec=spec, out_shape=jax.ShapeDtypeStruct((J, H, C), BF16),
                 compiler_params=_cp(("parallel", "parallel")), name=name)(place, grad, got)


def _sum_chips(sent, arrived, place, name):
    J, H, C = arrived.shape
    tr = _t(H, 256)
    nb = H // tr

    def body(pl_ref, own_ref, a0_ref, a1_ref, a2_ref, o_ref):
        acc = own_ref[...].astype(F32)
        for r in (a0_ref, a1_ref, a2_ref):
            acc = acc + r[...].astype(F32)
        o_ref[...] = acc

    def other(k):
        return _bs((None, tr, C), lambda i, pc: (jnp.where(pc[0] <= k, k + 1, k), i, 0))

    spec = pltpu.PrefetchScalarGridSpec(
        num_scalar_prefetch=1, grid=(nb,),
        in_specs=[_bs((None, tr, C), lambda i, pc: (pc[0], i, 0)), other(0), other(1), other(2)],
        out_specs=_bs((tr, C), lambda i, pc: (pc[1] * nb + i, 0)))
    return _call(body, grid_spec=spec, out_shape=jax.ShapeDtypeStruct((2 * H, C), F32),
                 compiler_params=_cp(("parallel",)), name=name)(place, sent, arrived, arrived, arrived)


def _place():
    x, y, c = lax.axis_index("x"), lax.axis_index("y"), lax.axis_index("c")
    chips = [(1 - x, y), (x, 1 - y), (1 - x, 1 - y)]
    return x, y, c, chips


def _split(nrows, row_bytes, align, cap=None):
    k = max(1, min(cap or DMA_MAX_CHUNKS, (nrows * row_bytes) // DMA_CHUNK_BYTES))
    while k > 1 and nrows % (k * align):
        k -= 1
    return k


def _ag_weights(packed):
    R, L = packed.shape
    H = R // 2
    K = _split(H, L * 2, 16, cap=AG_CHUNKS)
    hr = H // K
    KL = _split(R, L * 2, 16)
    lr = R // KL

    def body(src, out, ssem, rsem, lsem):
        x, y, c, chips = _place()
        me = 2 * x + y
        idx = [2 * cx + cy for cx, cy in chips]

        def rows(half, q):
            return pl.ds(pl.multiple_of(half * H + q * hr, 16), hr)

        def own(q):
            part = pl.ds(q * lr, lr)
            return pltpu.make_async_remote_copy(src_ref=src.at[part], dst_ref=out.at[me, part],
                                                send_sem=lsem.at[0], recv_sem=lsem.at[1],
                                                device_id=(x, y, 1 - c), device_id_type=MESH)

        for q in range(KL):
            own(q).start()

        def ici(j, q, chip_idx):
            return pltpu.make_async_remote_copy(src_ref=src.at[rows(c, q)], dst_ref=out.at[chip_idx, rows(c, q)],
                                                send_sem=ssem.at[j, q], recv_sem=rsem.at[j, q],
                                                device_id=(*chips[j], c), device_id_type=MESH)

        def d2d(j, q, half):
            return pltpu.make_async_remote_copy(src_ref=out.at[idx[j], rows(half, q)],
                                                dst_ref=out.at[idx[j], rows(half, q)],
                                                send_sem=ssem.at[3 + j, q], recv_sem=rsem.at[3 + j, q],
                                                device_id=(x, y, 1 - c), device_id_type=MESH)

        for q in range(K):
            for j in range(3):
                ici(j, q, me).start()
        for q in range(K):
            for j in range(3):
                ici(j, q, idx[j]).wait_recv()
                d2d(j, q, c).start()
        for q in range(K):
            for j in range(3):
                d2d(j, q, 1 - c).wait_recv()
        for q in range(K):
            for j in range(3):
                ici(j, q, me).wait_send()
                d2d(j, q, c).wait_send()
        pltpu.make_async_remote_copy(src_ref=src, dst_ref=out.at[me], send_sem=lsem.at[0], recv_sem=lsem.at[1],
                                     device_id=(x, y, 1 - c), device_id_type=MESH).wait()

    return _call(body, in_specs=[ANY], out_specs=ANY, out_shape=jax.ShapeDtypeStruct((N_CHIP, R, L), BF16),
                 scratch_shapes=[pltpu.SemaphoreType.DMA((6, K)), pltpu.SemaphoreType.DMA((6, K)),
                                 pltpu.SemaphoreType.DMA((2,))],
                 name="ag_weights")(packed)


def _halves_to_sibling(grads):
    n = len(grads)

    def body(*refs):
        g, got = refs[:n], refs[n:2 * n]
        ssem, rsem = refs[2 * n:]
        x, y, c, _ = _place()
        sib = (x, y, 1 - c)
        for i in range(n):
            J, R, C = g[i].shape
            H = R // 2
            k = _split(H, C * 4, SUBLANES, cap=DMA_MAX_CHUNKS // J)
            hr = H // k
            for j in range(J):
                for q in range(k):
                    other = pl.ds(pl.multiple_of((1 - c) * H + q * hr, SUBLANES), hr)
                    to = pl.ds(q * hr, hr)
                    pltpu.make_async_remote_copy(src_ref=g[i].at[j, other, :], dst_ref=got[i].at[j, to, :],
                                                 send_sem=ssem.at[i], recv_sem=rsem.at[i],
                                                 device_id=sib, device_id_type=MESH).start()
        for i in range(n):
            pltpu.make_async_remote_copy(src_ref=got[i], dst_ref=got[i], send_sem=ssem.at[i], recv_sem=rsem.at[i],
                                         device_id=sib, device_id_type=MESH).wait()

    half = [jax.ShapeDtypeStruct((a.shape[0], a.shape[1] // 2, a.shape[2]), a.dtype) for a in grads]
    return _call(body, in_specs=[ANY] * n, out_specs=[ANY] * n, out_shape=half,
                 scratch_shapes=[pltpu.SemaphoreType.DMA((n,)), pltpu.SemaphoreType.DMA((n,))],
                 name="rs_halves")(*grads)


def _scatter_to_chips(parts):
    n = len(parts)

    def body(*refs):
        s, got = refs[:n], refs[n:2 * n]
        ssem, rsem = refs[2 * n:]
        x, y, c, chips = _place()
        me = 2 * x + y
        idx = [2 * cx + cy for cx, cy in chips]
        for i in range(n):
            _, H, C = s[i].shape
            k = _split(H, C * 2, 16, cap=RS_CHUNKS)
            hr = H // k
            for q in range(k):
                rows = pl.ds(q * hr, hr)
                for j in range(3):
                    pltpu.make_async_remote_copy(src_ref=s[i].at[idx[j], rows, :], dst_ref=got[i].at[me, rows, :],
                                                 send_sem=ssem.at[i, j], recv_sem=rsem.at[i, j],
                                                 device_id=(*chips[j], c), device_id_type=MESH).start()
        for i in range(n):
            for j in range(3):
                pltpu.make_async_remote_copy(src_ref=s[i].at[idx[j]], dst_ref=got[i].at[idx[j]],
                                             send_sem=ssem.at[i, j], recv_sem=rsem.at[i, j],
                                             device_id=(*chips[j], c), device_id_type=MESH).wait()

    shp = [jax.ShapeDtypeStruct(a.shape, a.dtype) for a in parts]
    return _call(body, in_specs=[ANY] * n, out_specs=[ANY] * n, out_shape=shp,
                 scratch_shapes=[pltpu.SemaphoreType.DMA((n, 3)), pltpu.SemaphoreType.DMA((n, 3))],
                 name="rs_chips")(*parts)


def _join_halves(shards):
    n = len(shards)

    def body(*refs):
        full = refs[n:2 * n]
        ssem, rsem = refs[2 * n:]
        x, y, c, _ = _place()
        sib = (x, y, 1 - c)
        for i in range(n):
            H, C = full[i].shape[0] // 2, full[i].shape[1]
            k = _split(H, C * 4, SUBLANES)
            hr = H // k
            for q in range(k):
                rows = pl.ds(pl.multiple_of(c * H + q * hr, SUBLANES), hr)
                pltpu.make_async_remote_copy(src_ref=full[i].at[rows], dst_ref=full[i].at[rows],
                                             send_sem=ssem.at[i], recv_sem=rsem.at[i],
                                             device_id=sib, device_id_type=MESH).start()
        for i in range(n):
            half = full[i].at[pl.ds(0, full[i].shape[0] // 2)]
            pltpu.make_async_remote_copy(src_ref=half, dst_ref=half, send_sem=ssem.at[i], recv_sem=rsem.at[i],
                                         device_id=sib, device_id_type=MESH).wait()

    shp = [jax.ShapeDtypeStruct(a.shape, a.dtype) for a in shards]
    return _call(body, in_specs=[ANY] * n, out_specs=[ANY] * n, out_shape=shp,
                 input_output_aliases={i: i for i in range(n)},
                 scratch_shapes=[pltpu.SemaphoreType.DMA((n,)), pltpu.SemaphoreType.DMA((n,))],
                 name="rs_join")(*shards)


def _allreduce_small(buf):
    R, L = buf.shape
    RB = R // N_DEV

    def body(x_ref, o_ref, got, ssem, rsem):
        x, y, c, _ = _place()
        me = 4 * x + 2 * y + c

        def dev(k):
            return (k // 4, (k // 2) % 2, k % 2)

        def slab(k):
            return pl.ds(pl.multiple_of(k * RB, SUBLANES), RB)

        sends = []
        for d in range(1, N_DEV):
            peer = (me + d) % N_DEV
            cp = pltpu.make_async_remote_copy(src_ref=x_ref.at[slab(peer)], dst_ref=got.at[me],
                                              send_sem=ssem.at[0, d], recv_sem=rsem.at[0, d],
                                              device_id=dev(peer), device_id_type=MESH)
            cp.start()
            sends.append(cp)
        got[me] = x_ref[slab(me), :]
        for d in range(1, N_DEV):
            src = (me + N_DEV - d) % N_DEV
            pltpu.make_async_remote_copy(src_ref=x_ref.at[slab(me)], dst_ref=got.at[src],
                                         send_sem=ssem.at[0, d], recv_sem=rsem.at[0, d],
                                         device_id=dev(src), device_id_type=MESH).wait_recv()
        acc = got[0]
        for k in range(1, N_DEV):
            acc = acc + got[k]
        o_ref[slab(me), :] = acc
        for d in range(1, N_DEV):
            peer = (me + d) % N_DEV
            cp = pltpu.make_async_remote_copy(src_ref=o_ref.at[slab(me)], dst_ref=o_ref.at[slab(me)],
                                              send_sem=ssem.at[1, d], recv_sem=rsem.at[1, d],
                                              device_id=dev(peer), device_id_type=MESH)
            cp.start()
            sends.append(cp)
        for d in range(1, N_DEV):
            src = (me + N_DEV - d) % N_DEV
            pltpu.make_async_remote_copy(src_ref=o_ref.at[slab(src)], dst_ref=o_ref.at[slab(src)],
                                         send_sem=ssem.at[1, d], recv_sem=rsem.at[1, d],
                                         device_id=dev(src), device_id_type=MESH).wait_recv()
        for cp in sends:
            cp.wait_send()

    return _call(body, in_specs=[VMEM_FULL], out_specs=VMEM_FULL, out_shape=jax.ShapeDtypeStruct((R, L), F32),
                 scratch_shapes=[pltpu.VMEM((N_DEV, RB, L), F32), pltpu.SemaphoreType.DMA((2, N_DEV)),
                                 pltpu.SemaphoreType.DMA((2, N_DEV))],
                 name="allreduce_small")(buf)


def _block_diag(t, gt):
    G, A, B = t.shape
    t4 = t.reshape(G // gt, gt, A, B)
    eye = jnp.eye(gt, dtype=t.dtype)
    return jnp.einsum('sgab,gh->sgahb', t4, eye).reshape(G // gt, gt * A, gt * B)


def _block_diag_extract(m, gt, A, B):
    S = m.shape[0]
    m5 = m.reshape(S, gt, A, gt, B)
    eye = jnp.eye(gt, dtype=m.dtype)
    return jnp.einsum('sgahb,gh->sgab', m5, eye).reshape(S * gt, A, B)


def _pack_small(arrs, rows):
    flat = jnp.concatenate([a.reshape(-1).astype(F32) for a in arrs])
    return jnp.pad(flat, (0, rows * LANES - flat.shape[0])).reshape(rows, LANES)


def _unpack_small(buf, shapes):
    flat = buf.reshape(-1)
    out, off = [], 0
    for s in shapes:
        n = 1
        for d in s:
            n *= d
        out.append(flat[off:off + n].reshape(s))
        off += n
    return out


def kernel(x, p, norm_gain, w_in, w_pool, pool_scale, a_re, a_im, log_dt, b_re, b_im, c_re, c_im, d_skip, w_glu, w_out, w_ple, w_ple_gate, final_gain, loss_target, m_norm_gain, m_w_in, m_w_pool, m_pool_scale, m_a_re, m_a_im, m_log_dt, m_b_re, m_b_im, m_c_re, m_c_im, m_d_skip, m_w_glu, m_w_out, m_w_ple, m_w_ple_gate, m_final_gain, v_norm_gain, v_w_in, v_w_pool, v_pool_scale, v_a_re, v_a_im, v_log_dt, v_b_re, v_b_im, v_c_re, v_c_im, v_d_skip, v_w_glu, v_w_out, v_w_ple, v_w_ple_gate, v_final_gain):
    xs, pe, tgt = x[0], p[0, 0], loss_target[0]
    T, D = xs.shape
    E = pe.shape[1]
    P = D // 2
    NG = len(POOL_WINDOWS)
    PG = P // NG
    G, N, C = P // SSM_GROUP, SSM_STATE, SSM_GROUP
    GT = min(SSM_TILE_GROUPS, G)
    Q = D // N_CHIP

    big = {"w_in": (w_in, m_w_in, v_w_in), "w_pool": (w_pool, m_w_pool, v_w_pool),
           "w_glu": (w_glu, m_w_glu, v_w_glu), "w_out": (w_out, m_w_out, v_w_out),
           "w_ple": (w_ple, m_w_ple, v_w_ple), "w_ple_gate": (w_ple_gate, m_w_ple_gate, v_w_ple_gate)}
    big_names = list(big)
    shard2d = {n: (big[n][0].size // big[n][0].shape[-1], big[n][0].shape[-1]) for n in big_names}
    packed = jnp.concatenate([big[n][0].astype(BF16).reshape(-1, LANES) for n in big_names])
    gathered = _ag_weights(packed)
    full, off = {}, 0
    for n in big_names:
        r, c_ = shard2d[n]
        rows = r * c_ // LANES
        full[n] = gathered[:, off:off + rows].reshape(N_CHIP, r, c_)
        off += rows
    win_g, wglu_g, wple_g = full["w_in"], full["w_glu"], full["w_ple"]
    wout, wpg = full["w_out"].reshape(D, D), full["w_ple_gate"].reshape(D, D)
    wp = full["w_pool"].reshape(N_CHIP, NG, PG // N_CHIP, PG).transpose(1, 0, 2, 3).reshape(NG, PG, PG)

    rep = lambda a: jnp.repeat(a, C, axis=0)
    a_re_r, a_im_r = rep(a_re[0]), rep(a_im[0])
    ldt_r = rep(jnp.broadcast_to(log_dt[0][:, None], (G, N)))
    bt_re = b_re[0].transpose(0, 2, 1).reshape(G * C, N)
    bt_im = b_im[0].transpose(0, 2, 1).reshape(G * C, N)
    ab_re_r, ab_im_r, bbt_re, bbt_im = _ssm_prep(a_re_r, a_im_r, ldt_r, bt_re, bt_im)
    abr = ab_re_r[::C].reshape(1, G * N)
    abi = ab_im_r[::C].reshape(1, G * N)
    bdr = _block_diag(bbt_re.reshape(G, C, N), GT).astype(BF16)
    bdi = _block_diag(bbt_im.reshape(G, C, N), GT).astype(BF16)
    cdr = _block_diag(c_re[0].transpose(0, 2, 1), GT).astype(BF16)
    cdi = _block_diag(c_im[0].transpose(0, 2, 1), GT).astype(BF16)

    tb = _t(T, 256)
    tbs = _t(T, 256)
    tm = _t(T, 512)
    DH = _t(D, 1024)
    hn = _norm1(xs, norm_gain, tb)
    proj = _mm(hn, win_g, dims=NN, grid=(T // tm, N_CHIP, D // DH),
               a_spec=_bs((tm, DH), lambda i, j, k: (i, k)), b_spec=_bs((None, DH, P), lambda i, j, k: (j, k, 0)),
               o_spec=_bs((tm, P), lambda i, j, k: (i, j)), out_shape=jax.ShapeDtypeStruct((T, 4 * P), F32),
               name="mm_proj")
    pooled = _pool_fwd(proj, P, tb)
    mixed = _mm(pooled, wp, dims=NN, grid=(T // tm, NG, 1),
                a_spec=_bs((tm, PG), lambda i, g, k: (i, g)), b_spec=_bs((None, PG, PG), lambda i, g, k: (g, 0, 0)),
                o_spec=_bs((tm, PG), lambda i, g, k: (i, g)), out_shape=jax.ShapeDtypeStruct((T, P), F32),
                name="mm_pool")
    y, ge, bsr, bsi = _ssm_fwd(proj, bdr, bdi, cdr, cdi, abr, abi, d_skip, P, tbs)
    hg = _mm(ge, wglu_g, dims=NN, grid=(T // tm, N_CHIP, 1),
             a_spec=_bs((tm, P), lambda i, j, k: (i, 0)), b_spec=_bs((None, P, Q), lambda i, j, k: (j, 0, 0)),
             o_spec=_bs((tm, Q), lambda i, j, k: (i, j)), out_shape=jax.ShapeDtypeStruct((T, 2 * P), F32),
             name="mm_glu")
    cat = _gate_fwd(mixed, proj, hg, pool_scale, tb)
    h1 = _mm(cat, wout, dims=NN, grid=(T // tm, D // DH, D // DH), res=xs,
             a_spec=_bs((tm, DH), lambda i, n, k: (i, k)), b_spec=_bs((DH, DH), lambda i, n, k: (k, n)),
             r_spec=_bs((tm, DH), lambda i, n, k: (i, n)), o_spec=_bs((tm, DH), lambda i, n, k: (i, n)),
             out_shape=jax.ShapeDtypeStruct((T, D), F32), name="mm_out")
    e = _mm(pe, wple_g, dims=NN, grid=(T // tm, N_CHIP, 1),
            a_spec=_bs((tm, E), lambda i, j, k: (i, 0)), b_spec=_bs((None, E, Q), lambda i, j, k: (j, 0, 0)),
            o_spec=_bs((tm, Q), lambda i, j, k: (i, j)), out_shape=jax.ShapeDtypeStruct((T, D), F32),
            name="mm_ple")
    z = _mm(h1, wpg, dims=NN, grid=(T // tm, D // DH, D // DH),
            a_spec=_bs((tm, DH), lambda i, n, k: (i, k)), b_spec=_bs((DH, DH), lambda i, n, k: (k, n)),
            o_spec=_bs((tm, DH), lambda i, n, k: (i, n)), out_shape=jax.ShapeDtypeStruct((T, D), F32),
            name="mm_pgate")
    dh2, de, dz, dg2, lpart = _final_fb(h1, e, z, tgt, final_gain.reshape(1, D), tb)

    tk = _t(T, 512)
    dh1 = _mm(dz, wpg, dims=NT, grid=(T // tm, D // DH, D // DH), res=dh2,
              a_spec=_bs((tm, DH), lambda i, n, k: (i, k)), b_spec=_bs((DH, DH), lambda i, n, k: (n, k)),
              r_spec=_bs((tm, DH), lambda i, n, k: (i, n)), o_spec=_bs((tm, DH), lambda i, n, k: (i, n)),
              out_shape=jax.ShapeDtypeStruct((T, D), F32), name="mm_dh1")
    g_wpg = _mm(h1, dz, dims=TN, grid=(D // DH, D // DH, T // tk),
                a_spec=_bs((tk, DH), lambda m, n, k: (k, m)), b_spec=_bs((tk, DH), lambda m, n, k: (k, n)),
                o_spec=_bs((DH, DH), lambda m, n, k: (m, n)), out_shape=jax.ShapeDtypeStruct((D, D), F32),
                name="mm_gwpg")
    g_wple = _mm(pe, de, dims=TN, grid=(1, N_CHIP, T // tk),
                 a_spec=_bs((tk, E), lambda m, j, k: (k, 0)), b_spec=_bs((tk, Q), lambda m, j, k: (k, j)),
                 o_spec=_bs((None, E, Q), lambda m, j, k: (j, 0, 0)),
                 out_shape=jax.ShapeDtypeStruct((N_CHIP, E, Q), F32), name="mm_gwple")
    dcat = _mm(dh1, wout, dims=NT, grid=(T // tm, D // DH, D // DH),
               a_spec=_bs((tm, DH), lambda i, n, k: (i, k)), b_spec=_bs((DH, DH), lambda i, n, k: (n, k)),
               o_spec=_bs((tm, DH), lambda i, n, k: (i, n)), out_shape=jax.ShapeDtypeStruct((T, D), F32),
               name="mm_dcat")
    g_wout = _mm(cat, dh1, dims=TN, grid=(D // DH, D // DH, T // tk),
                 a_spec=_bs((tk, DH), lambda m, n, k: (k, m)), b_spec=_bs((tk, DH), lambda m, n, k: (k, n)),
                 o_spec=_bs((DH, DH), lambda m, n, k: (m, n)), out_shape=jax.ShapeDtypeStruct((D, D), F32),
                 name="mm_gwout")
    dmixed, dpg, dsg, dhg, dps = _gate_bwd(dcat, mixed, proj, hg, pool_scale, tb)
    dge = _mm(dhg, wglu_g, dims=NT, grid=(T // tm, 1, N_CHIP),
              a_spec=_bs((tm, Q), lambda i, n, k: (i, k)), b_spec=_bs((None, P, Q), lambda i, n, k: (k, 0, 0)),
              o_spec=_bs((tm, P), lambda i, n, k: (i, 0)), out_shape=jax.ShapeDtypeStruct((T, P), F32),
              name="mm_dge")
    g_wglu = _mm(ge, dhg, dims=TN, grid=(1, N_CHIP, T // tk),
                 a_spec=_bs((tk, P), lambda m, j, k: (k, 0)), b_spec=_bs((tk, Q), lambda m, j, k: (k, j)),
                 o_spec=_bs((None, P, Q), lambda m, j, k: (j, 0, 0)),
                 out_shape=jax.ShapeDtypeStruct((N_CHIP, P, Q), F32), name="mm_gwglu")
    du, dabr, dabi, dd, dbdr, dbdi, dcdr, dcdi = _ssm_bwd(proj, y, dge, bsr, bsi, bdr, bdi, cdr, cdi,
                                                          abr, abi, d_skip, P, tbs)
    dpooled = _mm(dmixed, wp, dims=NT, grid=(T // tm, NG, 1),
                  a_spec=_bs((tm, PG), lambda i, g, k: (i, g)), b_spec=_bs((None, PG, PG), lambda i, g, k: (g, 0, 0)),
                  o_spec=_bs((tm, PG), lambda i, g, k: (i, g)), out_shape=jax.ShapeDtypeStruct((T, P), F32),
                  name="mm_dpooled")
    g_wp = _mm(pooled, dmixed, dims=TN, grid=(NG, 1, T // tk),
               a_spec=_bs((tk, PG), lambda g, n, k: (k, g)), b_spec=_bs((tk, PG), lambda g, n, k: (k, g)),
               o_spec=_bs((None, PG, PG), lambda g, n, k: (g, 0, 0)),
               out_shape=jax.ShapeDtypeStruct((NG, PG, PG), F32), name="mm_gwp")
    dpi = _pool_bwd(dpooled, tb)
    dproj = jnp.concatenate([dpi, dpg, du, dsg], axis=1)
    dhn = _mm(dproj, win_g, dims=NT, grid=(T // tm, D // DH, N_CHIP),
              a_spec=_bs((tm, P), lambda i, n, k: (i, k)), b_spec=_bs((None, DH, P), lambda i, n, k: (k, n, 0)),
              o_spec=_bs((tm, DH), lambda i, n, k: (i, n)), out_shape=jax.ShapeDtypeStruct((T, D), F32),
              name="mm_dhn")
    g_win = _mm(hn, dproj, dims=TN, grid=(D // DH, N_CHIP, T // tk),
                a_spec=_bs((tk, DH), lambda m, j, k: (k, m)), b_spec=_bs((tk, P), lambda m, j, k: (k, j)),
                o_spec=_bs((None, DH, P), lambda m, j, k: (j, m, 0)),
                out_shape=jax.ShapeDtypeStruct((N_CHIP, D, P), F32), name="mm_gwin")
    grad_x, dg1 = _norm1_bwd(xs, dhn, dh1, norm_gain, tb)

    dbbt_re = _block_diag_extract(dbdr, GT, C, N).reshape(G * C, N)
    dbbt_im = _block_diag_extract(dbdi, GT, C, N).reshape(G * C, N)
    g_c_re = _block_diag_extract(dcdr, GT, N, C).transpose(0, 2, 1)
    g_c_im = _block_diag_extract(dcdi, GT, N, C).transpose(0, 2, 1)
    dab_re_r = rep(dabr.reshape(G, N)) * (1.0 / C)
    dab_im_r = rep(dabi.reshape(G, N)) * (1.0 / C)
    g_a_re, g_a_im, g_ldt, g_bt_re, g_bt_im = _ssm_prep_bwd(a_re_r, a_im_r, ldt_r, bt_re, bt_im,
                                                            dab_re_r, dab_im_r, dbbt_re, dbbt_im, G)
    g_b_re = g_bt_re.reshape(G, C, N).transpose(0, 2, 1)
    g_b_im = g_bt_im.reshape(G, C, N).transpose(0, 2, 1)

    gbig = {"w_in": g_win,
            "w_pool": g_wp.reshape(NG, N_CHIP, PG // N_CHIP, PG).transpose(1, 0, 2, 3).reshape(N_CHIP, NG * PG // N_CHIP, PG),
            "w_glu": g_wglu, "w_out": g_wout.reshape(N_CHIP, Q, D), "w_ple": g_wple,
            "w_ple_gate": g_wpg.reshape(N_CHIP, Q, D)}
    place = jnp.stack([2 * lax.axis_index("x") + lax.axis_index("y"), lax.axis_index("c")]).astype(jnp.int32)
    got = _halves_to_sibling([gbig[n] for n in big_names])
    chip_sums = [_sum_cast(gbig[n], g_, place, "sum_cast_" + n) for g_, n in zip(got, big_names)]
    arrived = _scatter_to_chips(chip_sums)
    halves = [_sum_chips(s_, a, place, "sum_chips_" + n) for s_, a, n in zip(chip_sums, arrived, big_names)]
    gshard = _join_halves(halves)

    small_names = ["norm_gain", "pool_scale", "a_re", "a_im", "log_dt", "b_re", "b_im", "c_re", "c_im",
                   "d_skip", "final_gain"]
    small_w = dict(norm_gain=norm_gain, pool_scale=pool_scale, a_re=a_re, a_im=a_im, log_dt=log_dt, b_re=b_re,
                   b_im=b_im, c_re=c_re, c_im=c_im, d_skip=d_skip, final_gain=final_gain)
    small_m = dict(norm_gain=m_norm_gain, pool_scale=m_pool_scale, a_re=m_a_re, a_im=m_a_im, log_dt=m_log_dt,
                   b_re=m_b_re, b_im=m_b_im, c_re=m_c_re, c_im=m_c_im, d_skip=m_d_skip, final_gain=m_final_gain)
    small_v = dict(norm_gain=v_norm_gain, pool_scale=v_pool_scale, a_re=v_a_re, a_im=v_a_im, log_dt=v_log_dt,
                   b_re=v_b_re, b_im=v_b_im, c_re=v_c_re, c_im=v_c_im, d_skip=v_d_skip, final_gain=v_final_gain)
    small_g = dict(norm_gain=dg1, pool_scale=dps, a_re=g_a_re, a_im=g_a_im, log_dt=g_ldt, b_re=g_b_re,
                   b_im=g_b_im, c_re=g_c_re, c_im=g_c_im, d_skip=dd, final_gain=dg2)
    shapes = [small_w[n].shape for n in small_names]
    total = sum(small_w[n].size for n in small_names) + 1
    unit = N_DEV * SUBLANES
    rows = -(-(-(-total // LANES)) // unit) * unit
    gbuf = _pack_small([small_g[n] for n in small_names] + [lpart[0, :1]], rows)
    gsum = _allreduce_small(gbuf)
    wbuf = _pack_small([small_w[n] for n in small_names], rows)
    mbuf = _pack_small([small_m[n] for n in small_names], rows)
    vbuf = _pack_small([small_v[n] for n in small_names], rows)
    dsm, msm, vsm = _adamw(wbuf, gsum, mbuf, vbuf, "adamw_small")
    g_small = dict(zip(small_names, _unpack_small(gsum, shapes)))
    d_small = dict(zip(small_names, _unpack_small(dsm, shapes)))
    m_small = dict(zip(small_names, _unpack_small(msm, shapes)))
    v_small = dict(zip(small_names, _unpack_small(vsm, shapes)))
    loss = gsum.reshape(-1)[total - 1]

    g_out, d_out, m_out, v_out = dict(g_small), dict(d_small), dict(m_small), dict(v_small)
    for n, gs in zip(big_names, gshard):
        w_, m_, v_ = big[n]
        r2 = shard2d[n]
        d_, mn_, vn_ = _adamw(w_.reshape(r2), gs, m_.reshape(r2), v_.reshape(r2), "adamw_" + n)
        g_out[n], d_out[n], m_out[n], v_out[n] = (a.reshape(w_.shape) for a in (gs, d_, mn_, vn_))

    order = ["norm_gain", "w_in", "w_pool", "pool_scale", "a_re", "a_im", "log_dt", "b_re", "b_im", "c_re",
             "c_im", "d_skip", "w_glu", "w_out", "w_ple", "w_ple_gate", "final_gain"]
    return (loss, grad_x[None], *[g_out[n] for n in order], *[d_out[n] for n in order],
            *[m_out[n] for n in order], *[v_out[n] for n in order])
```

```python
import functools

import jax
import jax.numpy as jnp
from jax import lax
from jax.experimental import pallas as pl
from jax.experimental.pallas import tpu as pltpu

F32, BF16 = jnp.float32, jnp.bfloat16
MESH = pl.DeviceIdType.MESH
ANY = pl.BlockSpec(memory_space=pl.ANY)
VMEM_FULL = pl.BlockSpec(memory_space=pltpu.VMEM)

EPS = 1e-6
A_RE_MAX = -1e-4
SSM_GROUP = 16
SSM_STATE = 64
POOL_WINDOWS = (2, 4, 8, 16)
POOL_HALO = 16
ADAM_LR, ADAM_B1, ADAM_B2, ADAM_EPS, ADAM_WD, ADAM_STEP = 0.001, 0.9, 0.999, 1e-08, 0.01, 10

V7X_VMEM_BYTES = 64 * 1024 * 1024
VMEM_LIMIT = V7X_VMEM_BYTES - 8 * 1024 * 1024
SUBLANES, LANES = 8, 128
SSM_TILE_GROUPS = 16
SCAN_LANES = 512
N_DEV, N_CHIP = 8, 4
DMA_CHUNK_BYTES = 256 * 1024
DMA_MAX_CHUNKS = 32
AG_CHUNKS = 8
RS_CHUNKS = 8


def _t(n, pref):
    return pref if n % pref == 0 else n


def _cp(sem=None, vmem=VMEM_LIMIT):
    return pltpu.CompilerParams(dimension_semantics=sem, vmem_limit_bytes=vmem)


def _call(body, **kw):
    return pl.pallas_call(body, **kw)


NN = ((1,), (0,))
NT = ((1,), (1,))
TN = ((0,), (0,))


def _mm(a, b, *, dims, grid, a_spec, b_spec, o_spec, out_shape, name, res=None, r_spec=None, bf16_copy=False):
    nk, kax = grid[-1], len(grid) - 1
    acc_shape = tuple(d for d in o_spec.block_shape if d is not None)

    def body(*refs):
        refs = list(refs)
        a_ref, b_ref = refs[:2]
        r_ref = refs[2] if res is not None else None
        outs = refs[3 if res is not None else 2:]
        o_ref = outs[0]
        o2_ref = outs[1] if bf16_copy else None
        acc = outs[-1] if nk > 1 else None

        def finish(r):
            if r_ref is not None:
                r = r + r_ref[...]
            o_ref[...] = r.astype(o_ref.dtype)
            if o2_ref is not None:
                o2_ref[...] = r.astype(BF16)

        part = lax.dot_general(a_ref[...].astype(BF16), b_ref[...].astype(BF16),
                               (dims, ((), ())), preferred_element_type=F32)
        if nk == 1:
            finish(part)
        else:
            k = pl.program_id(kax)

            @pl.when(k == 0)
            def _():
                acc[...] = part

            @pl.when(k > 0)
            def _():
                acc[...] += part

            @pl.when(k == nk - 1)
            def _():
                finish(acc[...])

    ins, specs = [a, b], [a_spec, b_spec]
    if res is not None:
        ins.append(res)
        specs.append(r_spec)
    o_specs, o_shapes = o_spec, out_shape
    if bf16_copy:
        o_specs = [o_spec, o_spec]
        o_shapes = [out_shape, jax.ShapeDtypeStruct(out_shape.shape, BF16)]
    sem = ("parallel",) * kax + ("arbitrary",)
    return _call(body, grid=grid, in_specs=specs, out_specs=o_specs, out_shape=o_shapes,
                 scratch_shapes=[pltpu.VMEM(acc_shape, F32)] if nk > 1 else [],
                 compiler_params=_cp(sem), name=name)(*ins)


def _bs(shape, fn):
    return pl.BlockSpec(shape, fn)


def _sigmoid(v):
    return 1.0 / (1.0 + jnp.exp(-v))


def _gelu(v):
    return 0.5 * v * (1.0 + jnp.tanh(0.7978845608028654 * (v + 0.044715 * v * v * v)))


def _gelu_grad(v):
    t = jnp.tanh(0.7978845608028654 * (v + 0.044715 * v * v * v))
    return 0.5 * (1.0 + t) + 0.5 * v * (1.0 - t * t) * 0.7978845608028654 * (1.0 + 3 * 0.044715 * v * v)


def _norm1(x, g1, tb):
    T, D = x.shape

    def body(x_ref, g_ref, o_ref):
        xv = x_ref[...]
        r = lax.rsqrt(jnp.mean(xv * xv, axis=-1, keepdims=True) + EPS)
        o_ref[...] = ((xv * r) * g_ref[...]).astype(BF16)

    return _call(body, grid=(T // tb,),
                 in_specs=[_bs((tb, D), lambda i: (i, 0)), _bs((1, D), lambda i: (0, 0))],
                 out_specs=_bs((tb, D), lambda i: (i, 0)), out_shape=jax.ShapeDtypeStruct((T, D), BF16),
                 compiler_params=_cp(("parallel",)), name="norm1")(x, g1)


def _norm1_bwd(x, dhn, dh1, g1, tb):
    T, D = x.shape

    def body(x_ref, dhn_ref, dh1_ref, g_ref, dx_ref, dg_ref):
        @pl.when(pl.program_id(0) == 0)
        def _():
            dg_ref[...] = jnp.zeros_like(dg_ref)

        xv = x_ref[...]
        r = lax.rsqrt(jnp.mean(xv * xv, axis=-1, keepdims=True) + EPS)
        xh = xv * r
        dhn_v = dhn_ref[...]
        dg_ref[...] += jnp.sum(dhn_v * xh, axis=0, keepdims=True)
        dxh = dhn_v * g_ref[...]
        dx_ref[...] = dh1_ref[...] + r * (dxh - xh * jnp.mean(dxh * xh, axis=-1, keepdims=True))

    row = _bs((tb, D), lambda i: (i, 0))
    vec = _bs((1, D), lambda i: (0, 0))
    return _call(body, grid=(T // tb,), in_specs=[row, row, row, vec], out_specs=[row, vec],
                 out_shape=[jax.ShapeDtypeStruct((T, D), F32), jax.ShapeDtypeStruct((1, D), F32)],
                 compiler_params=_cp(("arbitrary",)), name="norm1_bwd")(x, dhn, dh1, g1)


def _gate_fwd(mixed, proj, hg, ps, tb):
    T, P = mixed.shape

    def body(mx_ref, pg_ref, sg_ref, hg_ref, ps_ref, o_ref):
        pg, sg = pg_ref[...], sg_ref[...]
        ya = (mx_ref[...] * ps_ref[...]) * (pg * _sigmoid(pg))
        hgv = hg_ref[...]
        o = hgv[:, :P] * _sigmoid(hgv[:, P:])
        yb = o * (sg * _sigmoid(sg))
        o_ref[:, :P] = ya.astype(BF16)
        o_ref[:, P:] = yb.astype(BF16)

    return _call(body, grid=(T // tb,),
                 in_specs=[_bs((tb, P), lambda i: (i, 0)), _bs((tb, P), lambda i: (i, 1)),
                           _bs((tb, P), lambda i: (i, 3)), _bs((tb, 2 * P), lambda i: (i, 0)),
                           _bs((1, P), lambda i: (0, 0))],
                 out_specs=_bs((tb, 2 * P), lambda i: (i, 0)),
                 out_shape=jax.ShapeDtypeStruct((T, 2 * P), BF16),
                 compiler_params=_cp(("parallel",)), name="gate_fwd")(mixed, proj, proj, hg, ps)


def _gate_bwd(dcat, mixed, proj, hg, ps, tb):
    T, P = mixed.shape

    def body(dc_ref, mx_ref, pg_ref, sg_ref, hg_ref, ps_ref, dmx_ref, dpg_ref, dsg_ref, dhg_ref, dps_ref):
        @pl.when(pl.program_id(0) == 0)
        def _():
            dps_ref[...] = jnp.zeros_like(dps_ref)

        dc = dc_ref[...]
        dya, dyb = dc[:, :P], dc[:, P:]
        pg, sg, mx, psv = pg_ref[...], sg_ref[...], mx_ref[...], ps_ref[...]
        s_pg = _sigmoid(pg)
        dpa = dya * (pg * s_pg)
        dpg_ref[...] = (dya * (mx * psv) * (s_pg * (1.0 + pg * (1.0 - s_pg)))).astype(BF16)
        dps_ref[...] += jnp.sum(dpa * mx, axis=0, keepdims=True)
        dmx_ref[...] = (dpa * psv).astype(BF16)
        hgv = hg_ref[...]
        h1, s_h2 = hgv[:, :P], _sigmoid(hgv[:, P:])
        s_sg = _sigmoid(sg)
        do = dyb * (sg * s_sg)
        dsg_ref[...] = (dyb * (h1 * s_h2) * (s_sg * (1.0 + sg * (1.0 - s_sg)))).astype(BF16)
        dhg_ref[:, :P] = (do * s_h2).astype(BF16)
        dhg_ref[:, P:] = (do * h1 * s_h2 * (1.0 - s_h2)).astype(BF16)

    rowp = _bs((tb, P), lambda i: (i, 0))
    row2 = _bs((tb, 2 * P), lambda i: (i, 0))
    vec = _bs((1, P), lambda i: (0, 0))
    return _call(body, grid=(T // tb,),
                 in_specs=[row2, rowp, _bs((tb, P), lambda i: (i, 1)), _bs((tb, P), lambda i: (i, 3)), row2, vec],
                 out_specs=[rowp, rowp, rowp, row2, vec],
                 out_shape=[jax.ShapeDtypeStruct((T, P), BF16), jax.ShapeDtypeStruct((T, P), BF16),
                            jax.ShapeDtypeStruct((T, P), BF16), jax.ShapeDtypeStruct((T, 2 * P), BF16),
                            jax.ShapeDtypeStruct((1, P), F32)],
                 compiler_params=_cp(("arbitrary",)), name="gate_bwd")(dcat, mixed, proj, proj, hg, ps)


def _final_fb(h1, e, z, tgt, g2, tb):
    T, D = h1.shape

    def body(h1_ref, e_ref, z_ref, t_ref, g_ref, dh2_ref, de_ref, dz_ref, dg_ref, l_ref):
        @pl.when(pl.program_id(0) == 0)
        def _():
            dg_ref[...] = jnp.zeros_like(dg_ref)
            l_ref[...] = jnp.zeros_like(l_ref)

        ev = e_ref[...]
        s = _sigmoid(z_ref[...])
        h2 = h1_ref[...] + ev * s
        r = lax.rsqrt(jnp.mean(h2 * h2, axis=-1, keepdims=True) + EPS)
        xh = h2 * r
        gv = g_ref[...]
        diff = xh * gv - t_ref[...]
        l_ref[...] += 0.5 * jnp.sum(jnp.mean(diff * diff, axis=-1, keepdims=True))
        dout = diff * (1.0 / D)
        dg_ref[...] += jnp.sum(dout * xh, axis=0, keepdims=True)
        dxh = dout * gv
        dh2 = r * (dxh - xh * jnp.mean(dxh * xh, axis=-1, keepdims=True))
        dh2_ref[...] = dh2
        de_ref[...] = (dh2 * s).astype(BF16)
        dz_ref[...] = (dh2 * ev * s * (1.0 - s)).astype(BF16)

    row = _bs((tb, D), lambda i: (i, 0))
    vec = _bs((1, D), lambda i: (0, 0))
    return _call(body, grid=(T // tb,), in_specs=[row, row, row, row, vec],
                 out_specs=[row, row, row, vec, _bs((1, LANES), lambda i: (0, 0))],
                 out_shape=[jax.ShapeDtypeStruct((T, D), F32), jax.ShapeDtypeStruct((T, D), BF16),
                            jax.ShapeDtypeStruct((T, D), BF16), jax.ShapeDtypeStruct((1, D), F32),
                            jax.ShapeDtypeStruct((1, LANES), F32)],
                 compiler_params=_cp(("arbitrary",)), name="final_fb")(h1, e, z, tgt, g2)


def _pool_inv_count(t0, rows, pg, ngroups):
    t = t0 + lax.broadcasted_iota(jnp.int32, (rows, pg), 0)
    parts = []
    for w in POOL_WINDOWS[:ngroups]:
        parts.append(jnp.where(t + 1 >= w, 1.0 / w, 1.0 / (t + 1).astype(F32)))
    return parts


def _pool_fwd(proj, P, tb):
    T = proj.shape[0]
    ng = len(POOL_WINDOWS)
    pg = P // ng
    hb = tb // POOL_HALO

    def body(v_ref, tail_ref, o_ref, ext):
        i = pl.program_id(0)
        ext[pl.ds(0, POOL_HALO), :] = jnp.where(i > 0, tail_ref[...], 0.0)
        ext[pl.ds(POOL_HALO, tb), :] = v_ref[...]
        inv = _pool_inv_count(i * tb, tb, pg, ng)
        for g, w in enumerate(POOL_WINDOWS):
            cols = pl.ds(g * pg, pg)
            win = ext[pl.ds(POOL_HALO, tb), cols]
            for k in range(1, w):
                win = win + ext[pl.ds(POOL_HALO - k, tb), cols]
            o_ref[:, cols] = (win * inv[g] - ext[pl.ds(POOL_HALO, tb), cols]).astype(BF16)

    return _call(body, grid=(T // tb,),
                 in_specs=[_bs((tb, P), lambda i: (i, 0)),
                           _bs((POOL_HALO, P), lambda i: (jnp.maximum(i * hb - 1, 0), 0))],
                 out_specs=_bs((tb, P), lambda i: (i, 0)), out_shape=jax.ShapeDtypeStruct((T, P), BF16),
                 scratch_shapes=[pltpu.VMEM((tb + POOL_HALO, P), F32)],
                 compiler_params=_cp(("arbitrary",)), name="pool_fwd")(proj, proj)


def _pool_bwd(dpooled, tb):
    T, P = dpooled.shape
    ng = len(POOL_WINDOWS)
    pg = P // ng
    hb = tb // POOL_HALO
    nb = T // tb

    def body(d_ref, head_ref, o_ref, ext):
        i = pl.program_id(0)
        inv = _pool_inv_count(i * tb, tb, pg, ng)
        invh = _pool_inv_count((i + 1) * tb, POOL_HALO, pg, ng)
        for g in range(ng):
            cols = pl.ds(g * pg, pg)
            ext[pl.ds(0, tb), cols] = d_ref[:, cols] * inv[g]
            ext[pl.ds(tb, POOL_HALO), cols] = jnp.where(i < nb - 1, head_ref[:, cols] * invh[g], 0.0)
        for g, w in enumerate(POOL_WINDOWS):
            cols = pl.ds(g * pg, pg)
            acc = ext[pl.ds(0, tb), cols]
            for k in range(1, w):
                acc = acc + ext[pl.ds(k, tb), cols]
            o_ref[:, cols] = (acc - d_ref[:, cols]).astype(BF16)

    return _call(body, grid=(nb,),
                 in_specs=[_bs((tb, P), lambda i: (i, 0)),
                           _bs((POOL_HALO, P), lambda i: (jnp.minimum((i + 1) * hb, T // POOL_HALO - 1), 0))],
                 out_specs=_bs((tb, P), lambda i: (i, 0)), out_shape=jax.ShapeDtypeStruct((T, P), BF16),
                 scratch_shapes=[pltpu.VMEM((tb + POOL_HALO, P), F32)],
                 compiler_params=_cp(("arbitrary",)), name="pool_bwd")(dpooled, dpooled)


def _zoh(a_re, a_im, ldt, b_re, b_im):
    lam_re = jnp.minimum(a_re, A_RE_MAX)
    lam_im = a_im
    dt = jnp.exp(ldt)
    mag = jnp.exp(lam_re * dt)
    ang = lam_im * dt
    ab_re = mag * jnp.cos(ang)
    ab_im = mag * jnp.sin(ang)
    den = lam_re * lam_re + lam_im * lam_im
    n_re = ab_re - 1.0
    n_im = ab_im
    q_re = (n_re * lam_re + n_im * lam_im) / den
    q_im = (n_im * lam_re - n_re * lam_im) / den
    return ab_re, ab_im, q_re * b_re - q_im * b_im, q_re * b_im + q_im * b_re


def _ssm_prep(a_re, a_im, ldt, bt_re, bt_im):
    shp = jax.ShapeDtypeStruct(a_re.shape, F32)

    def body(a, b, c, d, e, o0, o1, o2, o3):
        r = _zoh(a[...], b[...], c[...], d[...], e[...])
        o0[...], o1[...], o2[...], o3[...] = r

    return _call(body, in_specs=[VMEM_FULL] * 5, out_specs=[VMEM_FULL] * 4, out_shape=[shp] * 4,
                 name="ssm_prep")(a_re, a_im, ldt, bt_re, bt_im)


def _ssm_prep_bwd(a_re, a_im, ldt, bt_re, bt_im, dab_re, dab_im, dbb_re, dbb_im, G):
    GC, N = a_re.shape
    C = GC // G

    def body(a, b, c, d, e, g0, g1, g2, g3, da_re, da_im, dldt, db_re, db_im):
        _, vjp = jax.vjp(_zoh, a[...], b[...], c[...], d[...], e[...])
        ga_re, ga_im, gl, gb_re, gb_im = vjp((g0[...], g1[...], g2[...], g3[...]))
        da_re[...] = jnp.sum(ga_re.reshape(G, C, N), axis=1)
        da_im[...] = jnp.sum(ga_im.reshape(G, C, N), axis=1)
        dldt[...] = jnp.sum(jnp.sum(gl.reshape(G, C, N), axis=1), axis=1, keepdims=True)
        db_re[...] = gb_re
        db_im[...] = gb_im

    gn = jax.ShapeDtypeStruct((G, N), F32)
    full = jax.ShapeDtypeStruct((GC, N), F32)
    return _call(body, in_specs=[VMEM_FULL] * 9, out_specs=[VMEM_FULL] * 5,
                 out_shape=[gn, gn, jax.ShapeDtypeStruct((G, 1), F32), full, full],
                 name="ssm_prep_bwd")(a_re, a_im, ldt, bt_re, bt_im, dab_re, dab_im, dbb_re, dbb_im)


def _coef_tiles(abr, abi, reverse):
    ns = abr.shape[1]
    row = lax.broadcasted_iota(jnp.int32, (SUBLANES, ns), 0)
    ar = jnp.broadcast_to(abr, (SUBLANES, ns))
    ai = jnp.broadcast_to(-abi if reverse else abi, (SUBLANES, ns))
    a2r, a2i = ar * ar - ai * ai, 2.0 * ar * ai
    a4r, a4i = a2r * a2r - a2i * a2i, 2.0 * a2r * a2i
    out = []
    for d, (vr, vi) in ((1, (ar, ai)), (2, (a2r, a2i)), (4, (a4r, a4i))):
        keep = (row < SUBLANES - d) if reverse else (row >= d)
        out += [jnp.where(keep, vr, 0.0), jnp.where(keep, vi, 0.0)]
    pr, pi = ar, ai
    for k in range(1, SUBLANES):
        sel = (row <= SUBLANES - 1 - k) if reverse else (row >= k)
        nr, ni = pr * ar - pi * ai, pr * ai + pi * ar
        pr, pi = jnp.where(sel, nr, pr), jnp.where(sel, ni, pi)
    return out + [pr, pi]


def _scan_block(xr_ref, xi_ref, coef_ref, car_ref, cai_ref, *, row0, nrows, ns, reverse):
    ntile = nrows // SUBLANES
    cw = min(SCAN_LANES, ns)
    edge = 0 if reverse else SUBLANES - 1
    for cc in range(ns // cw):
        cols = pl.ds(cc * cw, cw)
        co = [coef_ref[k, :, cols] for k in range(8)]

        def step(r, carry, cols=cols, co=co):
            cr, ci = carry
            rr = (ntile - 1 - r) if reverse else r
            rows = pl.ds(pl.multiple_of(row0 + rr * SUBLANES, SUBLANES), SUBLANES)
            xr, xi = xr_ref[rows, cols], xi_ref[rows, cols]
            for lvl, d in enumerate((1, 2, 4)):
                kr, ki = co[2 * lvl], co[2 * lvl + 1]
                sh = SUBLANES - d if reverse else d
                sr, si = pltpu.roll(xr, sh, 0), pltpu.roll(xi, sh, 0)
                xr, xi = xr + (kr * sr - ki * si), xi + (kr * si + ki * sr)
            xr, xi = xr + (co[6] * cr - co[7] * ci), xi + (co[6] * ci + co[7] * cr)
            xr_ref[rows, cols] = xr
            xi_ref[rows, cols] = xi
            return (jnp.broadcast_to(xr[edge:edge + 1, :], xr.shape),
                    jnp.broadcast_to(xi[edge:edge + 1, :], xi.shape))

        cr, ci = lax.fori_loop(0, ntile, step, (car_ref[:, cols], cai_ref[:, cols]))
        car_ref[:, cols] = cr
        cai_ref[:, cols] = ci


def _ssm_fwd(proj, bdr, bdi, cdr, cdi, abr, abi, dsk, P, tb):
    T = proj.shape[0]
    ntl, ct, st = bdr.shape
    ns = ntl * st
    nb = T // tb

    def body(u_ref, bdr_ref, bdi_ref, cdr_ref, cdi_ref, abr_ref, abi_ref, d_ref,
             y_ref, ge_ref, bsr_ref, bsi_ref, sr, si, coef, car, cai):
        @pl.when(pl.program_id(0) == 0)
        def _():
            for k, tile in enumerate(_coef_tiles(abr_ref[...], abi_ref[...], False)):
                coef[k] = tile
            car[...] = jnp.zeros_like(car)
            cai[...] = jnp.zeros_like(cai)

        bsr_ref[...] = car[...]
        bsi_ref[...] = cai[...]
        u = u_ref[...]
        ub = u.astype(BF16)
        for s in range(ntl):
            us = ub[:, s * ct:(s + 1) * ct]
            sr[:, s * st:(s + 1) * st] = jnp.dot(us, bdr_ref[s], preferred_element_type=F32)
            si[:, s * st:(s + 1) * st] = jnp.dot(us, bdi_ref[s], preferred_element_type=F32)
        _scan_block(sr, si, coef, car, cai, row0=0, nrows=tb, ns=ns, reverse=False)
        for s in range(ntl):
            s_re = sr[:, s * st:(s + 1) * st].astype(BF16)
            s_im = si[:, s * st:(s + 1) * st].astype(BF16)
            y = (jnp.dot(s_re, cdr_ref[s], preferred_element_type=F32)
                 - jnp.dot(s_im, cdi_ref[s], preferred_element_type=F32)
                 + d_ref[:, s * ct:(s + 1) * ct] * u[:, s * ct:(s + 1) * ct])
            y_ref[:, s * ct:(s + 1) * ct] = y
            ge_ref[:, s * ct:(s + 1) * ct] = _gelu(y).astype(BF16)

    full3 = lambda a: _bs(a.shape, lambda i: (0, 0, 0))
    vec = lambda n: _bs((1, n), lambda i: (0, 0))
    row = _bs((tb, P), lambda i: (i, 0))
    st_spec = _bs((None, SUBLANES, ns), lambda i: (i, 0, 0))
    return _call(body, grid=(nb,),
                 in_specs=[_bs((tb, P), lambda i: (i, 2)), full3(bdr), full3(bdi), full3(cdr), full3(cdi),
                           vec(ns), vec(ns), vec(P)],
                 out_specs=[row, row, st_spec, st_spec],
                 out_shape=[jax.ShapeDtypeStruct((T, P), F32), jax.ShapeDtypeStruct((T, P), BF16),
                            jax.ShapeDtypeStruct((nb, SUBLANES, ns), F32),
                            jax.ShapeDtypeStruct((nb, SUBLANES, ns), F32)],
                 scratch_shapes=[pltpu.VMEM((tb, ns), F32), pltpu.VMEM((tb, ns), F32),
                                 pltpu.VMEM((8, SUBLANES, ns), F32),
                                 pltpu.VMEM((SUBLANES, ns), F32), pltpu.VMEM((SUBLANES, ns), F32)],
                 compiler_params=_cp(("arbitrary",)), name="ssm_fwd")(proj, bdr, bdi, cdr, cdi, abr, abi, dsk)


def _ssm_bwd(proj, y, dge, bsr, bsi, bdr, bdi, cdr, cdi, abr, abi, dsk, P, tb):
    T = proj.shape[0]
    ntl, ct, st = bdr.shape
    ns = ntl * st
    nb = T // tb
    pad = SUBLANES

    def body(u_ref, y_ref, dge_ref, bsr_ref, bsi_ref, abr_ref, abi_ref, d_ref, bdr_h, bdi_h, cdr_h, cdi_h,
             du_ref, dabr_ref, dabi_ref, dd_ref, dbdr_h, dbdi_h, dcdr_h, dcdi_h,
             wbdr, wbdi, wcdr, wcdi, abdr, abdi, acdr, acdi, spr, spi, gr, gi, coef_f, coef_r,
             car, cai, gcr, gci):
        i = pl.program_id(0)

        @pl.when(i == 0)
        def _():
            for h, w in ((bdr_h, wbdr), (bdi_h, wbdi), (cdr_h, wcdr), (cdi_h, wcdi)):
                pltpu.sync_copy(h, w)
            for a in (abdr, abdi, acdr, acdi, gcr, gci):
                a[...] = jnp.zeros_like(a)
            for o in (dabr_ref, dabi_ref, dd_ref):
                o[...] = jnp.zeros_like(o)
            for k, tile in enumerate(_coef_tiles(abr_ref[...], abi_ref[...], False)):
                coef_f[k] = tile
            for k, tile in enumerate(_coef_tiles(abr_ref[...], abi_ref[...], True)):
                coef_r[k] = tile

        car[...] = bsr_ref[...]
        cai[...] = bsi_ref[...]
        spr[pl.ds(0, pad), :] = bsr_ref[...]
        spi[pl.ds(0, pad), :] = bsi_ref[...]
        u = u_ref[...]
        ub = u.astype(BF16)
        for s in range(ntl):
            us = ub[:, s * ct:(s + 1) * ct]
            spr[pl.ds(pad, tb), s * st:(s + 1) * st] = jnp.dot(us, wbdr[s], preferred_element_type=F32)
            spi[pl.ds(pad, tb), s * st:(s + 1) * st] = jnp.dot(us, wbdi[s], preferred_element_type=F32)
        _scan_block(spr, spi, coef_f, car, cai, row0=pad, nrows=tb, ns=ns, reverse=False)

        dy = dge_ref[...] * _gelu_grad(y_ref[...])
        dyb = dy.astype(BF16)
        for s in range(ntl):
            dys = dyb[:, s * ct:(s + 1) * ct]
            gr[:, s * st:(s + 1) * st] = lax.dot_general(dys, wcdr[s], (NT, ((), ())), preferred_element_type=F32)
            gi[:, s * st:(s + 1) * st] = -lax.dot_general(dys, wcdi[s], (NT, ((), ())), preferred_element_type=F32)
        _scan_block(gr, gi, coef_r, gcr, gci, row0=0, nrows=tb, ns=ns, reverse=True)

        cw = min(SCAN_LANES, ns)
        for cc in range(ns // cw):
            cols = pl.ds(cc * cw, cw)
            pr, pi = spr[pl.ds(pad - 1, tb), cols], spi[pl.ds(pad - 1, tb), cols]
            g_r, g_i = gr[:, cols], gi[:, cols]
            dabr_ref[:, cols] += jnp.sum(g_r * pr + g_i * pi, axis=0, keepdims=True)
            dabi_ref[:, cols] += jnp.sum(g_i * pr - g_r * pi, axis=0, keepdims=True)

        for s in range(ntl):
            sl_c, sl_s = slice(s * ct, (s + 1) * ct), slice(s * st, (s + 1) * st)
            s_re = spr[pl.ds(pad, tb), sl_s].astype(BF16)
            s_im = spi[pl.ds(pad, tb), sl_s].astype(BF16)
            g_re, g_im = gr[:, sl_s].astype(BF16), gi[:, sl_s].astype(BF16)
            dys, us = dyb[:, sl_c], ub[:, sl_c]
            acdr[s] += lax.dot_general(s_re, dys, (TN, ((), ())), preferred_element_type=F32)
            acdi[s] -= lax.dot_general(s_im, dys, (TN, ((), ())), preferred_element_type=F32)
            abdr[s] += lax.dot_general(us, g_re, (TN, ((), ())), preferred_element_type=F32)
            abdi[s] += lax.dot_general(us, g_im, (TN, ((), ())), preferred_element_type=F32)
            du = (lax.dot_general(g_re, wbdr[s], (NT, ((), ())), preferred_element_type=F32)
                  + lax.dot_general(g_im, wbdi[s], (NT, ((), ())), preferred_element_type=F32)
                  + d_ref[:, sl_c] * dy[:, sl_c])
            du_ref[:, sl_c] = du.astype(BF16)
        dd_ref[...] += jnp.sum(dy * u, axis=0, keepdims=True)

        @pl.when(i == nb - 1)
        def _():
            for a, h in ((abdr, dbdr_h), (abdi, dbdi_h), (acdr, dcdr_h), (acdi, dcdi_h)):
                pltpu.sync_copy(a, h)

    rev = lambda i: nb - 1 - i
    vec = lambda n: _bs((1, n), lambda i: (0, 0))
    row = _bs((tb, P), lambda i: (rev(i), 0))
    st_spec = _bs((None, SUBLANES, ns), lambda i: (rev(i), 0, 0))
    bshape = jax.ShapeDtypeStruct(bdr.shape, F32)
    cshape = jax.ShapeDtypeStruct(cdr.shape, F32)
    return _call(body, grid=(nb,),
                 in_specs=[_bs((tb, P), lambda i: (rev(i), 2)), row, row, st_spec, st_spec,
                           vec(ns), vec(ns), vec(P), ANY, ANY, ANY, ANY],
                 out_specs=[row, vec(ns), vec(ns), vec(P), ANY, ANY, ANY, ANY],
                 out_shape=[jax.ShapeDtypeStruct((T, P), BF16), jax.ShapeDtypeStruct((1, ns), F32),
                            jax.ShapeDtypeStruct((1, ns), F32), jax.ShapeDtypeStruct((1, P), F32),
                            bshape, bshape, cshape, cshape],
                 scratch_shapes=[pltpu.VMEM(bdr.shape, BF16), pltpu.VMEM(bdr.shape, BF16),
                                 pltpu.VMEM(cdr.shape, BF16), pltpu.VMEM(cdr.shape, BF16),
                                 pltpu.VMEM(bdr.shape, F32), pltpu.VMEM(bdr.shape, F32),
                                 pltpu.VMEM(cdr.shape, F32), pltpu.VMEM(cdr.shape, F32),
                                 pltpu.VMEM((tb + pad, ns), F32), pltpu.VMEM((tb + pad, ns), F32),
                                 pltpu.VMEM((tb, ns), F32), pltpu.VMEM((tb, ns), F32),
                                 pltpu.VMEM((8, SUBLANES, ns), F32), pltpu.VMEM((8, SUBLANES, ns), F32),
                                 pltpu.VMEM((SUBLANES, ns), F32), pltpu.VMEM((SUBLANES, ns), F32),
                                 pltpu.VMEM((SUBLANES, ns), F32), pltpu.VMEM((SUBLANES, ns), F32)],
                 compiler_params=_cp(("arbitrary",)), name="ssm_bwd")(
                     proj, y, dge, bsr, bsi, abr, abi, dsk, bdr, bdi, cdr, cdi)


def _adamw(w, g, m, v, name):
    R, C = w.shape
    tr = _t(R, 256)

    def body(w_ref, g_ref, m_ref, v_ref, d_ref, mo_ref, vo_ref):
        gv = g_ref[...]
        mn = ADAM_B1 * m_ref[...] + (1.0 - ADAM_B1) * gv
        vn = ADAM_B2 * v_ref[...] + (1.0 - ADAM_B2) * (gv * gv)
        m_hat = mn / (1.0 - ADAM_B1 ** ADAM_STEP)
        v_hat = vn / (1.0 - ADAM_B2 ** ADAM_STEP)
        d_ref[...] = -ADAM_LR * (m_hat / (jnp.sqrt(v_hat) + ADAM_EPS) + ADAM_WD * w_ref[...])
        mo_ref[...] = mn
        vo_ref[...] = vn

    blk = _bs((tr, C), lambda i: (i, 0))
    shp = jax.ShapeDtypeStruct((R, C), F32)
    return _call(body, grid=(R // tr,), in_specs=[blk] * 4, out_specs=[blk] * 3, out_shape=[shp] * 3,
                 compiler_params=_cp(("parallel",)), name=name)(w, g, m, v)


def _sum_cast(grad, got, place, name):
    J, H, C = got.shape
    tr = _t(H, 256)
    nb = H // tr

    def body(pl_ref, a_ref, b_ref, o_ref):
        o_ref[...] = (a_ref[...] + b_ref[...]).astype(BF16)

    blk = _bs((None, tr, C), lambda j, i, pc: (j, i, 0))
    mine = _bs((None, tr, C), lambda j, i, pc: (j, pc[1] * nb + i, 0))
    spec = pltpu.PrefetchScalarGridSpec(num_scalar_prefetch=1, grid=(J, nb), in_specs=[mine, blk], out_specs=blk)
    return _call(body, grid_spec=spec, out_shape=jax.ShapeDtypeStruct((J, H, C), BF16),
                 compiler_params=_cp(("parallel", "parallel")), name=name)(place, grad, got)


def _sum_chips(sent, arrived, place, name):
    J, H, C = arrived.shape
    tr = _t(H, 256)
    nb = H // tr

    def body(pl_ref, own_ref, a0_ref, a1_ref, a2_ref, o_ref):
        acc = own_ref[...].astype(F32)
        for r in (a0_ref, a1_ref, a2_ref):
            acc = acc + r[...].astype(F32)
        o_ref[...] = acc

    def other(k):
        return _bs((None, tr, C), lambda i, pc: (jnp.where(pc[0] <= k, k + 1, k), i, 0))

    spec = pltpu.PrefetchScalarGridSpec(
        num_scalar_prefetch=1, grid=(nb,),
        in_specs=[_bs((None, tr, C), lambda i, pc: (pc[0], i, 0)), other(0), other(1), other(2)],
        out_specs=_bs((tr, C), lambda i, pc: (pc[1] * nb + i, 0)))
    return _call(body, grid_spec=spec, out_shape=jax.ShapeDtypeStruct((2 * H, C), F32),
                 compiler_params=_cp(("parallel",)), name=name)(place, sent, arrived, arrived, arrived)


def _place():
    x, y, c = lax.axis_index("x"), lax.axis_index("y"), lax.axis_index("c")
    chips = [(1 - x, y), (x, 1 - y), (1 - x, 1 - y)]
    return x, y, c, chips


def _split(nrows, row_bytes, align, cap=None):
    k = max(1, min(cap or DMA_MAX_CHUNKS, (nrows * row_bytes) // DMA_CHUNK_BYTES))
    while k > 1 and nrows % (k * align):
        k -= 1
    return k


def _ag_weights(packed):
    R, L = packed.shape
    H = R // 2
    K = _split(H, L * 2, 16, cap=AG_CHUNKS)
    hr = H // K
    KL = _split(R, L * 2, 16)
    lr = R // KL

    def body(src, out, ssem, rsem, lsem):
        x, y, c, chips = _place()
        me = 2 * x + y
        idx = [2 * cx + cy for cx, cy in chips]

        def rows(half, q):
            return pl.ds(pl.multiple_of(half * H + q * hr, 16), hr)

        def own(q):
            part = pl.ds(q * lr, lr)
            return pltpu.make_async_remote_copy(src_ref=src.at[part], dst_ref=out.at[me, part],
                                                send_sem=lsem.at[0], recv_sem=lsem.at[1],
                                                device_id=(x, y, 1 - c), device_id_type=MESH)

        for q in range(KL):
            own(q).start()

        def ici(j, q, chip_idx):
            return pltpu.make_async_remote_copy(src_ref=src.at[rows(c, q)], dst_ref=out.at[chip_idx, rows(c, q)],
                                                send_sem=ssem.at[j, q], recv_sem=rsem.at[j, q],
                                                device_id=(*chips[j], c), device_id_type=MESH)

        def d2d(j, q, half):
            return pltpu.make_async_remote_copy(src_ref=out.at[idx[j], rows(half, q)],
                                                dst_ref=out.at[idx[j], rows(half, q)],
                                                send_sem=ssem.at[3 + j, q], recv_sem=rsem.at[3 + j, q],
                                                device_id=(x, y, 1 - c), device_id_type=MESH)

        for q in range(K):
            for j in range(3):
                ici(j, q, me).start()
        for q in range(K):
            for j in range(3):
                ici(j, q, idx[j]).wait_recv()
                d2d(j, q, c).start()
        for q in range(K):
            for j in range(3):
                d2d(j, q, 1 - c).wait_recv()
        for q in range(K):
            for j in range(3):
                ici(j, q, me).wait_send()
                d2d(j, q, c).wait_send()
        pltpu.make_async_remote_copy(src_ref=src, dst_ref=out.at[me], send_sem=lsem.at[0], recv_sem=lsem.at[1],
                                     device_id=(x, y, 1 - c), device_id_type=MESH).wait()

    return _call(body, in_specs=[ANY], out_specs=ANY, out_shape=jax.ShapeDtypeStruct((N_CHIP, R, L), BF16),
                 scratch_shapes=[pltpu.SemaphoreType.DMA((6, K)), pltpu.SemaphoreType.DMA((6, K)),
                                 pltpu.SemaphoreType.DMA((2,))],
                 name="ag_weights")(packed)


def _halves_to_sibling(grads):
    n = len(grads)

    def body(*refs):
        g, got = refs[:n], refs[n:2 * n]
        ssem, rsem = refs[2 * n:]
        x, y, c, _ = _place()
        sib = (x, y, 1 - c)
        for i in range(n):
            J, R, C = g[i].shape
            H = R // 2
            k = _split(H, C * 4, SUBLANES, cap=DMA_MAX_CHUNKS // J)
            hr = H // k
            for j in range(J):
                for q in range(k):
                    other = pl.ds(pl.multiple_of((1 - c) * H + q * hr, SUBLANES), hr)
                    to = pl.ds(q * hr, hr)
                    pltpu.make_async_remote_copy(src_ref=g[i].at[j, other, :], dst_ref=got[i].at[j, to, :],
                                                 send_sem=ssem.at[i], recv_sem=rsem.at[i],
                                                 device_id=sib, device_id_type=MESH).start()
        for i in range(n):
            pltpu.make_async_remote_copy(src_ref=got[i], dst_ref=got[i], send_sem=ssem.at[i], recv_sem=rsem.at[i],
                                         device_id=sib, device_id_type=MESH).wait()

    half = [jax.ShapeDtypeStruct((a.shape[0], a.shape[1] // 2, a.shape[2]), a.dtype) for a in grads]
    return _call(body, in_specs=[ANY] * n, out_specs=[ANY] * n, out_shape=half,
                 scratch_shapes=[pltpu.SemaphoreType.DMA((n,)), pltpu.SemaphoreType.DMA((n,))],
                 name="rs_halves")(*grads)


def _scatter_to_chips(parts):
    n = len(parts)

    def body(*refs):
        s, got = refs[:n], refs[n:2 * n]
        ssem, rsem = refs[2 * n:]
        x, y, c, chips = _place()
        me = 2 * x + y
        idx = [2 * cx + cy for cx, cy in chips]
        for i in range(n):
            _, H, C = s[i].shape
            k = _split(H, C * 2, 16, cap=RS_CHUNKS)
            hr = H // k
            for q in range(k):
                rows = pl.ds(q * hr, hr)
                for j in range(3):
                    pltpu.make_async_remote_copy(src_ref=s[i].at[idx[j], rows, :], dst_ref=got[i].at[me, rows, :],
                                                 send_sem=ssem.at[i, j], recv_sem=rsem.at[i, j],
                                                 device_id=(*chips[j], c), device_id_type=MESH).start()
        for i in range(n):
            for j in range(3):
                pltpu.make_async_remote_copy(src_ref=s[i].at[idx[j]], dst_ref=got[i].at[idx[j]],
                                             send_sem=ssem.at[i, j], recv_sem=rsem.at[i, j],
                                             device_id=(*chips[j], c), device_id_type=MESH).wait()

    shp = [jax.ShapeDtypeStruct(a.shape, a.dtype) for a in parts]
    return _call(body, in_specs=[ANY] * n, out_specs=[ANY] * n, out_shape=shp,
                 scratch_shapes=[pltpu.SemaphoreType.DMA((n, 3)), pltpu.SemaphoreType.DMA((n, 3))],
                 name="rs_chips")(*parts)


def _join_halves(shards):
    n = len(shards)

    def body(*refs):
        full = refs[n:2 * n]
        ssem, rsem = refs[2 * n:]
        x, y, c, _ = _place()
        sib = (x, y, 1 - c)
        for i in range(n):
            H, C = full[i].shape[0] // 2, full[i].shape[1]
            k = _split(H, C * 4, SUBLANES)
            hr = H // k
            for q in range(k):
                rows = pl.ds(pl.multiple_of(c * H + q * hr, SUBLANES), hr)
                pltpu.make_async_remote_copy(src_ref=full[i].at[rows], dst_ref=full[i].at[rows],
                                             send_sem=ssem.at[i], recv_sem=rsem.at[i],
                                             device_id=sib, device_id_type=MESH).start()
        for i in range(n):
            half = full[i].at[pl.ds(0, full[i].shape[0] // 2)]
            pltpu.make_async_remote_copy(src_ref=half, dst_ref=half, send_sem=ssem.at[i], recv_sem=rsem.at[i],
                                         device_id=sib, device_id_type=MESH).wait()

    shp = [jax.ShapeDtypeStruct(a.shape, a.dtype) for a in shards]
    return _call(body, in_specs=[ANY] * n, out_specs=[ANY] * n, out_shape=shp,
                 input_output_aliases={i: i for i in range(n)},
                 scratch_shapes=[pltpu.SemaphoreType.DMA((n,)), pltpu.SemaphoreType.DMA((n,))],
                 name="rs_join")(*shards)


def _allreduce_small(buf):
    R, L = buf.shape
    RB = R // N_DEV

    def body(x_ref, o_ref, got, ssem, rsem):
        x, y, c, _ = _place()
        me = 4 * x + 2 * y + c

        def dev(k):
            return (k // 4, (k // 2) % 2, k % 2)

        def slab(k):
            return pl.ds(pl.multiple_of(k * RB, SUBLANES), RB)

        sends = []
        for d in range(1, N_DEV):
            peer = (me + d) % N_DEV
            cp = pltpu.make_async_remote_copy(src_ref=x_ref.at[slab(peer)], dst_ref=got.at[me],
                                              send_sem=ssem.at[0, d], recv_sem=rsem.at[0, d],
                                              device_id=dev(peer), device_id_type=MESH)
            cp.start()
            sends.append(cp)
        got[me] = x_ref[slab(me), :]
        for d in range(1, N_DEV):
            src = (me + N_DEV - d) % N_DEV
            pltpu.make_async_remote_copy(src_ref=x_ref.at[slab(me)], dst_ref=got.at[src],
                                         send_sem=ssem.at[0, d], recv_sem=rsem.at[0, d],
                                         device_id=dev(src), device_id_type=MESH).wait_recv()
        acc = got[0]
        for k in range(1, N_DEV):
            acc = acc + got[k]
        o_ref[slab(me), :] = acc
        for d in range(1, N_DEV):
            peer = (me + d) % N_DEV
            cp = pltpu.make_async_remote_copy(src_ref=o_ref.at[slab(me)], dst_ref=o_ref.at[slab(me)],
                                              send_sem=ssem.at[1, d], recv_sem=rsem.at[1, d],
                                              device_id=dev(peer), device_id_type=MESH)
            cp.start()
            sends.append(cp)
        for d in range(1, N_DEV):
            src = (me + N_DEV - d) % N_DEV
            pltpu.make_async_remote_copy(src_ref=o_ref.at[slab(src)], dst_ref=o_ref.at[slab(src)],
                                         send_sem=ssem.at[1, d], recv_sem=rsem.at[1, d],
                                         device_id=dev(src), device_id_type=MESH).wait_recv()
        for cp in sends:
            cp.wait_send()

    return _call(body, in_specs=[VMEM_FULL], out_specs=VMEM_FULL, out_shape=jax.ShapeDtypeStruct((R, L), F32),
                 scratch_shapes=[pltpu.VMEM((N_DEV, RB, L), F32), pltpu.SemaphoreType.DMA((2, N_DEV)),
                                 pltpu.SemaphoreType.DMA((2, N_DEV))],
                 name="allreduce_small")(buf)


def _block_diag(t, gt):
    G, A, B = t.shape
    t4 = t.reshape(G // gt, gt, A, B)
    eye = jnp.eye(gt, dtype=t.dtype)
    return jnp.einsum('sgab,gh->sgahb', t4, eye).reshape(G // gt, gt * A, gt * B)


def _block_diag_extract(m, gt, A, B):
    S = m.shape[0]
    m5 = m.reshape(S, gt, A, gt, B)
    eye = jnp.eye(gt, dtype=m.dtype)
    return jnp.einsum('sgahb,gh->sgab', m5, eye).reshape(S * gt, A, B)


def _pack_small(arrs, rows):
    flat = jnp.concatenate([a.reshape(-1).astype(F32) for a in arrs])
    return jnp.pad(flat, (0, rows * LANES - flat.shape[0])).reshape(rows, LANES)


def _unpack_small(buf, shapes):
    flat = buf.reshape(-1)
    out, off = [], 0
    for s in shapes:
        n = 1
        for d in s:
            n *= d
        out.append(flat[off:off + n].reshape(s))
        off += n
    return out


def kernel(x, p, norm_gain, w_in, w_pool, pool_scale, a_re, a_im, log_dt, b_re, b_im, c_re, c_im, d_skip, w_glu, w_out, w_ple, w_ple_gate, final_gain, loss_target, m_norm_gain, m_w_in, m_w_pool, m_pool_scale, m_a_re, m_a_im, m_log_dt, m_b_re, m_b_im, m_c_re, m_c_im, m_d_skip, m_w_glu, m_w_out, m_w_ple, m_w_ple_gate, m_final_gain, v_norm_gain, v_w_in, v_w_pool, v_pool_scale, v_a_re, v_a_im, v_log_dt, v_b_re, v_b_im, v_c_re, v_c_im, v_d_skip, v_w_glu, v_w_out, v_w_ple, v_w_ple_gate, v_final_gain):
    xs, pe, tgt = x[0], p[0, 0], loss_target[0]
    T, D = xs.shape
    E = pe.shape[1]
    P = D // 2
    NG = len(POOL_WINDOWS)
    PG = P // NG
    G, N, C = P // SSM_GROUP, SSM_STATE, SSM_GROUP
    GT = min(SSM_TILE_GROUPS, G)
    Q = D // N_CHIP

    big = {"w_in": (w_in, m_w_in, v_w_in), "w_pool": (w_pool, m_w_pool, v_w_pool),
           "w_glu": (w_glu, m_w_glu, v_w_glu), "w_out": (w_out, m_w_out, v_w_out),
           "w_ple": (w_ple, m_w_ple, v_w_ple), "w_ple_gate": (w_ple_gate, m_w_ple_gate, v_w_ple_gate)}
    big_names = list(big)
    shard2d = {n: (big[n][0].size // big[n][0].shape[-1], big[n][0].shape[-1]) for n in big_names}
    packed = jnp.concatenate([big[n][0].astype(BF16).reshape(-1, LANES) for n in big_names])
    gathered = _ag_weights(packed)
    full, off = {}, 0
    for n in big_names:
        r, c_ = shard2d[n]
        rows = r * c_ // LANES
        full[n] = gathered[:, off:off + rows].reshape(N_CHIP, r, c_)
        off += rows
    cols = lambda a: a.transpose(1, 0, 2).reshape(a.shape[1], N_CHIP * a.shape[2])
    win, wglu, wple = cols(full["w_in"]), cols(full["w_glu"]), cols(full["w_ple"])
    wout, wpg = full["w_out"].reshape(D, D), full["w_ple_gate"].reshape(D, D)
    wp = full["w_pool"].reshape(N_CHIP, NG, PG // N_CHIP, PG).transpose(1, 0, 2, 3).reshape(NG, PG, PG)

    rep = lambda a: jnp.repeat(a, C, axis=0)
    a_re_r, a_im_r = rep(a_re[0]), rep(a_im[0])
    ldt_r = rep(jnp.broadcast_to(log_dt[0][:, None], (G, N)))
    bt_re = b_re[0].transpose(0, 2, 1).reshape(G * C, N)
    bt_im = b_im[0].transpose(0, 2, 1).reshape(G * C, N)
    ab_re_r, ab_im_r, bbt_re, bbt_im = _ssm_prep(a_re_r, a_im_r, ldt_r, bt_re, bt_im)
    abr = ab_re_r[::C].reshape(1, G * N)
    abi = ab_im_r[::C].reshape(1, G * N)
    bdr = _block_diag(bbt_re.reshape(G, C, N), GT).astype(BF16)
    bdi = _block_diag(bbt_im.reshape(G, C, N), GT).astype(BF16)
    cdr = _block_diag(c_re[0].transpose(0, 2, 1), GT).astype(BF16)
    cdi = _block_diag(c_im[0].transpose(0, 2, 1), GT).astype(BF16)

    tb = _t(T, 256)
    tbs = _t(T, 256)
    tm = _t(T, 1024)
    tk = _t(T, 2048)
    DH = _t(D, 1024)
    row_k = lambda i, n, k: (i, k)
    row_n = lambda i, n, k: (i, n)
    f32 = lambda *shape: jax.ShapeDtypeStruct(shape, F32)
    hn = _norm1(xs, norm_gain, tb)
    proj = _mm(hn, win, dims=NN, grid=(T // tm, N_CHIP, 1),
               a_spec=_bs((tm, D), row_k), b_spec=_bs((D, P), lambda i, n, k: (k, n)),
               o_spec=_bs((tm, P), row_n), out_shape=f32(T, 4 * P), name="mm_proj")
    pooled = _pool_fwd(proj, P, tb)
    mixed = _mm(pooled, wp, dims=NN, grid=(T // tm, NG, 1),
                a_spec=_bs((tm, PG), row_n), b_spec=_bs((None, PG, PG), lambda i, g, k: (g, 0, 0)),
                o_spec=_bs((tm, PG), row_n), out_shape=f32(T, P), name="mm_pool")
    y, ge, bsr, bsi = _ssm_fwd(proj, bdr, bdi, cdr, cdi, abr, abi, d_skip, P, tbs)
    hg = _mm(ge, wglu, dims=NN, grid=(T // tm, 2 * P // DH, 1),
             a_spec=_bs((tm, P), row_k), b_spec=_bs((P, DH), lambda i, n, k: (k, n)),
             o_spec=_bs((tm, DH), row_n), out_shape=f32(T, 2 * P), name="mm_glu")
    cat = _gate_fwd(mixed, proj, hg, pool_scale, tb)
    h1, h1b = _mm(cat, wout, dims=NN, grid=(T // tm, D // DH, 1), res=xs, bf16_copy=True,
                  a_spec=_bs((tm, D), row_k), b_spec=_bs((D, DH), lambda i, n, k: (k, n)),
                  r_spec=_bs((tm, DH), row_n), o_spec=_bs((tm, DH), row_n), out_shape=f32(T, D), name="mm_out")
    e = _mm(pe, wple, dims=NN, grid=(T // tm, D // DH, 1),
            a_spec=_bs((tm, E), row_k), b_spec=_bs((E, DH), lambda i, n, k: (k, n)),
            o_spec=_bs((tm, DH), row_n), out_shape=f32(T, D), name="mm_ple")
    z = _mm(h1b, wpg, dims=NN, grid=(T // tm, D // DH, 1),
            a_spec=_bs((tm, D), row_k), b_spec=_bs((D, DH), lambda i, n, k: (k, n)),
            o_spec=_bs((tm, DH), row_n), out_shape=f32(T, D), name="mm_pgate")
    dh2, de, dz, dg2, lpart = _final_fb(h1, e, z, tgt, final_gain.reshape(1, D), tb)

    col_m = lambda m, n, k: (k, m)
    col_n = lambda m, n, k: (k, n)
    dh1, dh1b = _mm(dz, wpg, dims=NT, grid=(T // tm, D // DH, 1), res=dh2, bf16_copy=True,
                    a_spec=_bs((tm, D), row_k), b_spec=_bs((DH, D), lambda i, n, k: (n, k)),
                    r_spec=_bs((tm, DH), row_n), o_spec=_bs((tm, DH), row_n), out_shape=f32(T, D), name="mm_dh1")
    g_wpg = _mm(h1b, dz, dims=TN, grid=(D // DH, D // DH, T // tk),
                a_spec=_bs((tk, DH), col_m), b_spec=_bs((tk, DH), col_n),
                o_spec=_bs((DH, DH), lambda m, n, k: (m, n)), out_shape=f32(D, D), name="mm_gwpg")
    g_wple = _mm(pe, de, dims=TN, grid=(1, N_CHIP, T // tk),
                 a_spec=_bs((tk, E), col_m), b_spec=_bs((tk, Q), col_n),
                 o_spec=_bs((None, E, Q), lambda m, j, k: (j, 0, 0)), out_shape=f32(N_CHIP, E, Q), name="mm_gwple")
    dcat = _mm(dh1b, wout, dims=NT, grid=(T // tm, D // DH, 1),
               a_spec=_bs((tm, D), row_k), b_spec=_bs((DH, D), lambda i, n, k: (n, k)),
               o_spec=_bs((tm, DH), row_n), out_shape=f32(T, D), name="mm_dcat")
    g_wout = _mm(cat, dh1b, dims=TN, grid=(D // DH, D // DH, T // tk),
                 a_spec=_bs((tk, DH), col_m), b_spec=_bs((tk, DH), col_n),
                 o_spec=_bs((DH, DH), lambda m, n, k: (m, n)), out_shape=f32(D, D), name="mm_gwout")
    dmixed, dpg, dsg, dhg, dps = _gate_bwd(dcat, mixed, proj, hg, pool_scale, tb)
    dge = _mm(dhg, wglu, dims=NT, grid=(T // tm, 1, 1),
              a_spec=_bs((tm, 2 * P), row_k), b_spec=_bs((P, 2 * P), lambda i, n, k: (n, k)),
              o_spec=_bs((tm, P), row_n), out_shape=f32(T, P), name="mm_dge")
    g_wglu = _mm(ge, dhg, dims=TN, grid=(1, N_CHIP, T // tk),
                 a_spec=_bs((tk, P), col_m), b_spec=_bs((tk, Q), col_n),
                 o_spec=_bs((None, P, Q), lambda m, j, k: (j, 0, 0)), out_shape=f32(N_CHIP, P, Q), name="mm_gwglu")
    du, dabr, dabi, dd, dbdr, dbdi, dcdr, dcdi = _ssm_bwd(proj, y, dge, bsr, bsi, bdr, bdi, cdr, cdi,
                                                          abr, abi, d_skip, P, tbs)
    dpooled = _mm(dmixed, wp, dims=NT, grid=(T // tm, NG, 1),
                  a_spec=_bs((tm, PG), row_n), b_spec=_bs((None, PG, PG), lambda i, g, k: (g, 0, 0)),
                  o_spec=_bs((tm, PG), row_n), out_shape=f32(T, P), name="mm_dpooled")
    g_wp = _mm(pooled, dmixed, dims=TN, grid=(NG, 1, T // tk),
               a_spec=_bs((tk, PG), col_m), b_spec=_bs((tk, PG), col_m),
               o_spec=_bs((None, PG, PG), lambda g, n, k: (g, 0, 0)), out_shape=f32(NG, PG, PG), name="mm_gwp")
    dpi = _pool_bwd(dpooled, tb)
    dproj = jnp.concatenate([dpi, dpg, du, dsg], axis=1)
    KH = _t(4 * P, 2048)
    dhn = _mm(dproj, win, dims=NT, grid=(T // tm, D // DH, 4 * P // KH),
              a_spec=_bs((tm, KH), row_k), b_spec=_bs((DH, KH), lambda i, n, k: (n, k)),
              o_spec=_bs((tm, DH), row_n), out_shape=f32(T, D), name="mm_dhn")
    g_win = _mm(hn, dproj, dims=TN, grid=(D // DH, N_CHIP, T // tk),
                a_spec=_bs((tk, DH), col_m), b_spec=_bs((tk, P), col_n),
                o_spec=_bs((None, DH, P), lambda m, j, k: (j, m, 0)), out_shape=f32(N_CHIP, D, P), name="mm_gwin")
    grad_x, dg1 = _norm1_bwd(xs, dhn, dh1, norm_gain, tb)

    dbbt_re = _block_diag_extract(dbdr, GT, C, N).reshape(G * C, N)
    dbbt_im = _block_diag_extract(dbdi, GT, C, N).reshape(G * C, N)
    g_c_re = _block_diag_extract(dcdr, GT, N, C).transpose(0, 2, 1)
    g_c_im = _block_diag_extract(dcdi, GT, N, C).transpose(0, 2, 1)
    dab_re_r = rep(dabr.reshape(G, N)) * (1.0 / C)
    dab_im_r = rep(dabi.reshape(G, N)) * (1.0 / C)
    g_a_re, g_a_im, g_ldt, g_bt_re, g_bt_im = _ssm_prep_bwd(a_re_r, a_im_r, ldt_r, bt_re, bt_im,
                                                            dab_re_r, dab_im_r, dbbt_re, dbbt_im, G)
    g_b_re = g_bt_re.reshape(G, C, N).transpose(0, 2, 1)
    g_b_im = g_bt_im.reshape(G, C, N).transpose(0, 2, 1)

    gbig = {"w_in": g_win,
            "w_pool": g_wp.reshape(NG, N_CHIP, PG // N_CHIP, PG).transpose(1, 0, 2, 3).reshape(N_CHIP, NG * PG // N_CHIP, PG),
            "w_glu": g_wglu, "w_out": g_wout.reshape(N_CHIP, Q, D), "w_ple": g_wple,
            "w_ple_gate": g_wpg.reshape(N_CHIP, Q, D)}
    place = jnp.stack([2 * lax.axis_index("x") + lax.axis_index("y"), lax.axis_index("c")]).astype(jnp.int32)
    got = _halves_to_sibling([gbig[n] for n in big_names])
    chip_sums = [_sum_cast(gbig[n], g_, place, "sum_cast_" + n) for g_, n in zip(got, big_names)]
    arrived = _scatter_to_chips(chip_sums)
    halves = [_sum_chips(s_, a, place, "sum_chips_" + n) for s_, a, n in zip(chip_sums, arrived, big_names)]
    gshard = _join_halves(halves)

    small_names = ["norm_gain", "pool_scale", "a_re", "a_im", "log_dt", "b_re", "b_im", "c_re", "c_im",
                   "d_skip", "final_gain"]
    small_w = dict(norm_gain=norm_gain, pool_scale=pool_scale, a_re=a_re, a_im=a_im, log_dt=log_dt, b_re=b_re,
                   b_im=b_im, c_re=c_re, c_im=c_im, d_skip=d_skip, final_gain=final_gain)
    small_m = dict(norm_gain=m_norm_gain, pool_scale=m_pool_scale, a_re=m_a_re, a_im=m_a_im, log_dt=m_log_dt,
                   b_re=m_b_re, b_im=m_b_im, c_re=m_c_re, c_im=m_c_im, d_skip=m_d_skip, final_gain=m_final_gain)
    small_v = dict(norm_gain=v_norm_gain, pool_scale=v_pool_scale, a_re=v_a_re, a_im=v_a_im, log_dt=v_log_dt,
                   b_re=v_b_re, b_im=v_b_im, c_re=v_c_re, c_im=v_c_im, d_skip=v_d_skip, final_gain=v_final_gain)
    small_g = dict(norm_gain=dg1, pool_scale=dps, a_re=g_a_re, a_im=g_a_im, log_dt=g_ldt, b_re=g_b_re,
                   b_im=g_b_im, c_re=g_c_re, c_im=g_c_im, d_skip=dd, final_gain=dg2)
    shapes = [small_w[n].shape for n in small_names]
    total = sum(small_w[n].size for n in small_names) + 1
    unit = N_DEV * SUBLANES
    rows = -(-(-(-total // LANES)) // unit) * unit
    gbuf = _pack_small([small_g[n] for n in small_names] + [lpart[0, :1]], rows)
    gsum = _allreduce_small(gbuf)
    wbuf = _pack_small([small_w[n] for n in small_names], rows)
    mbuf = _pack_small([small_m[n] for n in small_names], rows)
    vbuf = _pack_small([small_v[n] for n in small_names], rows)
    dsm, msm, vsm = _adamw(wbuf, gsum, mbuf, vbuf, "adamw_small")
    g_small = dict(zip(small_names, _unpack_small(gsum, shapes)))
    d_small = dict(zip(small_names, _unpack_small(dsm, shapes)))
    m_small = dict(zip(small_names, _unpack_small(msm, shapes)))
    v_small = dict(zip(small_names, _unpack_small(vsm, shapes)))
    loss = gsum.reshape(-1)[total - 1]

    g_out, d_out, m_out, v_out = dict(g_small), dict(d_small), dict(m_small), dict(v_small)
    for n, gs in zip(big_names, gshard):
        w_, m_, v_ = big[n]
        r2 = shard2d[n]
        d_, mn_, vn_ = _adamw(w_.reshape(r2), gs, m_.reshape(r2), v_.reshape(r2), "adamw_" + n)
        g_out[n], d_out[n], m_out[n], v_out[n] = (a.reshape(w_.shape) for a in (gs, d_, mn_, vn_))

    order = ["norm_gain", "w_in", "w_pool", "pool_scale", "a_re", "a_im", "log_dt", "b_re", "b_im", "c_re",
             "c_im", "d_skip", "w_glu", "w_out", "w_ple", "w_ple_gate", "final_gain"]
    return (loss, grad_x[None], *[g_out[n] for n in order], *[d_out[n] for n in order],
            *[m_out[n] for n in order], *[v_out[n] for n in order])
```

```python
import functools

import jax
import jax.numpy as jnp
from jax import lax
from jax.experimental import pallas as pl
from jax.experimental.pallas import tpu as pltpu

F32, BF16 = jnp.float32, jnp.bfloat16
MESH = pl.DeviceIdType.MESH
ANY = pl.BlockSpec(memory_space=pl.ANY)
VMEM_FULL = pl.BlockSpec(memory_space=pltpu.VMEM)

EPS = 1e-6
A_RE_MAX = -1e-4
SSM_GROUP = 16
SSM_STATE = 64
POOL_WINDOWS = (2, 4, 8, 16)
POOL_HALO = 16
ADAM_LR, ADAM_B1, ADAM_B2, ADAM_EPS, ADAM_WD, ADAM_STEP = 0.001, 0.9, 0.999, 1e-08, 0.01, 10

V7X_VMEM_BYTES = 64 * 1024 * 1024
VMEM_LIMIT = V7X_VMEM_BYTES - 8 * 1024 * 1024
SUBLANES, LANES = 8, 128
SSM_TILE_GROUPS = 8
SCAN_LANES = 512
N_DEV, N_CHIP = 8, 4
DMA_CHUNK_BYTES = 256 * 1024
DMA_MAX_CHUNKS = 32
AG_CHUNKS = 8
RS_CHUNKS = 8


def _t(n, pref):
    return pref if n % pref == 0 else n


def _cp(sem=None, vmem=VMEM_LIMIT):
    return pltpu.CompilerParams(dimension_semantics=sem, vmem_limit_bytes=vmem)


def _call(body, **kw):
    return pl.pallas_call(body, **kw)


NN = ((1,), (0,))
NT = ((1,), (1,))
TN = ((0,), (0,))


def _mm(a, b, *, dims, grid, a_spec, b_spec, o_spec, out_shape, name, res=None, r_spec=None, bf16_copy=False):
    nk, kax = grid[-1], len(grid) - 1
    acc_shape = tuple(d for d in o_spec.block_shape if d is not None)

    def body(*refs):
        refs = list(refs)
        a_ref, b_ref = refs[:2]
        r_ref = refs[2] if res is not None else None
        outs = refs[3 if res is not None else 2:]
        o_ref = outs[0]
        o2_ref = outs[1] if bf16_copy else None
        acc = outs[-1] if nk > 1 else None

        def finish(r):
            if r_ref is not None:
                r = r + r_ref[...]
            o_ref[...] = r.astype(o_ref.dtype)
            if o2_ref is not None:
                o2_ref[...] = r.astype(BF16)

        part = lax.dot_general(a_ref[...].astype(BF16), b_ref[...].astype(BF16),
                               (dims, ((), ())), preferred_element_type=F32)
        if nk == 1:
            finish(part)
        else:
            k = pl.program_id(kax)

            @pl.when(k == 0)
            def _():
                acc[...] = part

            @pl.when(k > 0)
            def _():
                acc[...] += part

            @pl.when(k == nk - 1)
            def _():
                finish(acc[...])

    ins, specs = [a, b], [a_spec, b_spec]
    if res is not None:
        ins.append(res)
        specs.append(r_spec)
    o_specs, o_shapes = o_spec, out_shape
    if bf16_copy:
        o_specs = [o_spec, o_spec]
        o_shapes = [out_shape, jax.ShapeDtypeStruct(out_shape.shape, BF16)]
    sem = ("parallel",) * kax + ("arbitrary",)
    return _call(body, grid=grid, in_specs=specs, out_specs=o_specs, out_shape=o_shapes,
                 scratch_shapes=[pltpu.VMEM(acc_shape, F32)] if nk > 1 else [],
                 compiler_params=_cp(sem), name=name)(*ins)


def _bs(shape, fn):
    return pl.BlockSpec(shape, fn)


def _sigmoid(v):
    return 1.0 / (1.0 + jnp.exp(-v))


def _gelu(v):
    return 0.5 * v * (1.0 + jnp.tanh(0.7978845608028654 * (v + 0.044715 * v * v * v)))


def _gelu_grad(v):
    t = jnp.tanh(0.7978845608028654 * (v + 0.044715 * v * v * v))
    return 0.5 * (1.0 + t) + 0.5 * v * (1.0 - t * t) * 0.7978845608028654 * (1.0 + 3 * 0.044715 * v * v)


def _norm1(x, g1, tb):
    T, D = x.shape

    def body(x_ref, g_ref, o_ref):
        xv = x_ref[...]
        r = lax.rsqrt(jnp.mean(xv * xv, axis=-1, keepdims=True) + EPS)
        o_ref[...] = ((xv * r) * g_ref[...]).astype(BF16)

    return _call(body, grid=(T // tb,),
                 in_specs=[_bs((tb, D), lambda i: (i, 0)), _bs((1, D), lambda i: (0, 0))],
                 out_specs=_bs((tb, D), lambda i: (i, 0)), out_shape=jax.ShapeDtypeStruct((T, D), BF16),
                 compiler_params=_cp(("parallel",)), name="norm1")(x, g1)


def _norm1_bwd(x, dhn, dh1, g1, tb):
    T, D = x.shape

    def body(x_ref, dhn_ref, dh1_ref, g_ref, dx_ref, dg_ref):
        @pl.when(pl.program_id(0) == 0)
        def _():
            dg_ref[...] = jnp.zeros_like(dg_ref)

        xv = x_ref[...]
        r = lax.rsqrt(jnp.mean(xv * xv, axis=-1, keepdims=True) + EPS)
        xh = xv * r
        dhn_v = dhn_ref[...]
        dg_ref[...] += jnp.sum(dhn_v * xh, axis=0, keepdims=True)
        dxh = dhn_v * g_ref[...]
        dx_ref[...] = dh1_ref[...] + r * (dxh - xh * jnp.mean(dxh * xh, axis=-1, keepdims=True))

    row = _bs((tb, D), lambda i: (i, 0))
    vec = _bs((1, D), lambda i: (0, 0))
    return _call(body, grid=(T // tb,), in_specs=[row, row, row, vec], out_specs=[row, vec],
                 out_shape=[jax.ShapeDtypeStruct((T, D), F32), jax.ShapeDtypeStruct((1, D), F32)],
                 compiler_params=_cp(("arbitrary",)), name="norm1_bwd")(x, dhn, dh1, g1)


def _gate_fwd(mixed, proj, hg, ps, tb):
    T, P = mixed.shape

    def body(mx_ref, pg_ref, sg_ref, hg_ref, ps_ref, o_ref):
        pg, sg = pg_ref[...], sg_ref[...]
        ya = (mx_ref[...] * ps_ref[...]) * (pg * _sigmoid(pg))
        hgv = hg_ref[...]
        o = hgv[:, :P] * _sigmoid(hgv[:, P:])
        yb = o * (sg * _sigmoid(sg))
        o_ref[:, :P] = ya.astype(BF16)
        o_ref[:, P:] = yb.astype(BF16)

    return _call(body, grid=(T // tb,),
                 in_specs=[_bs((tb, P), lambda i: (i, 0)), _bs((tb, P), lambda i: (i, 1)),
                           _bs((tb, P), lambda i: (i, 3)), _bs((tb, 2 * P), lambda i: (i, 0)),
                           _bs((1, P), lambda i: (0, 0))],
                 out_specs=_bs((tb, 2 * P), lambda i: (i, 0)),
                 out_shape=jax.ShapeDtypeStruct((T, 2 * P), BF16),
                 compiler_params=_cp(("parallel",)), name="gate_fwd")(mixed, proj, proj, hg, ps)


def _gate_bwd(dcat, mixed, proj, hg, ps, tb):
    T, P = mixed.shape

    def body(dc_ref, mx_ref, pg_ref, sg_ref, hg_ref, ps_ref, dmx_ref, dpg_ref, dsg_ref, dhg_ref, dps_ref):
        @pl.when(pl.program_id(0) == 0)
        def _():
            dps_ref[...] = jnp.zeros_like(dps_ref)

        dc = dc_ref[...]
        dya, dyb = dc[:, :P], dc[:, P:]
        pg, sg, mx, psv = pg_ref[...], sg_ref[...], mx_ref[...], ps_ref[...]
        s_pg = _sigmoid(pg)
        dpa = dya * (pg * s_pg)
        dpg_ref[...] = (dya * (mx * psv) * (s_pg * (1.0 + pg * (1.0 - s_pg)))).astype(BF16)
        dps_ref[...] += jnp.sum(dpa * mx, axis=0, keepdims=True)
        dmx_ref[...] = (dpa * psv).astype(BF16)
        hgv = hg_ref[...]
        h1, s_h2 = hgv[:, :P], _sigmoid(hgv[:, P:])
        s_sg = _sigmoid(sg)
        do = dyb * (sg * s_sg)
        dsg_ref[...] = (dyb * (h1 * s_h2) * (s_sg * (1.0 + sg * (1.0 - s_sg)))).astype(BF16)
        dhg_ref[:, :P] = (do * s_h2).astype(BF16)
        dhg_ref[:, P:] = (do * h1 * s_h2 * (1.0 - s_h2)).astype(BF16)

    rowp = _bs((tb, P), lambda i: (i, 0))
    row2 = _bs((tb, 2 * P), lambda i: (i, 0))
    vec = _bs((1, P), lambda i: (0, 0))
    return _call(body, grid=(T // tb,),
                 in_specs=[row2, rowp, _bs((tb, P), lambda i: (i, 1)), _bs((tb, P), lambda i: (i, 3)), row2, vec],
                 out_specs=[rowp, rowp, rowp, row2, vec],
                 out_shape=[jax.ShapeDtypeStruct((T, P), BF16), jax.ShapeDtypeStruct((T, P), BF16),
                            jax.ShapeDtypeStruct((T, P), BF16), jax.ShapeDtypeStruct((T, 2 * P), BF16),
                            jax.ShapeDtypeStruct((1, P), F32)],
                 compiler_params=_cp(("arbitrary",)), name="gate_bwd")(dcat, mixed, proj, proj, hg, ps)


def _final_fb(h1, e, z, tgt, g2, tb):
    T, D = h1.shape

    def body(h1_ref, e_ref, z_ref, t_ref, g_ref, dh2_ref, de_ref, dz_ref, dg_ref, l_ref):
        @pl.when(pl.program_id(0) == 0)
        def _():
            dg_ref[...] = jnp.zeros_like(dg_ref)
            l_ref[...] = jnp.zeros_like(l_ref)

        ev = e_ref[...]
        s = _sigmoid(z_ref[...])
        h2 = h1_ref[...] + ev * s
        r = lax.rsqrt(jnp.mean(h2 * h2, axis=-1, keepdims=True) + EPS)
        xh = h2 * r
        gv = g_ref[...]
        diff = xh * gv - t_ref[...]
        l_ref[...] += 0.5 * jnp.sum(jnp.mean(diff * diff, axis=-1, keepdims=True))
        dout = diff * (1.0 / D)
        dg_ref[...] += jnp.sum(dout * xh, axis=0, keepdims=True)
        dxh = dout * gv
        dh2 = r * (dxh - xh * jnp.mean(dxh * xh, axis=-1, keepdims=True))
        dh2_ref[...] = dh2
        de_ref[...] = (dh2 * s).astype(BF16)
        dz_ref[...] = (dh2 * ev * s * (1.0 - s)).astype(BF16)

    row = _bs((tb, D), lambda i: (i, 0))
    vec = _bs((1, D), lambda i: (0, 0))
    return _call(body, grid=(T // tb,), in_specs=[row, row, row, row, vec],
                 out_specs=[row, row, row, vec, _bs((1, LANES), lambda i: (0, 0))],
                 out_shape=[jax.ShapeDtypeStruct((T, D), F32), jax.ShapeDtypeStruct((T, D), BF16),
                            jax.ShapeDtypeStruct((T, D), BF16), jax.ShapeDtypeStruct((1, D), F32),
                            jax.ShapeDtypeStruct((1, LANES), F32)],
                 compiler_params=_cp(("arbitrary",)), name="final_fb")(h1, e, z, tgt, g2)


def _pool_inv_count(t0, rows, pg, ngroups):
    t = t0 + lax.broadcasted_iota(jnp.int32, (rows, pg), 0)
    parts = []
    for w in POOL_WINDOWS[:ngroups]:
        parts.append(jnp.where(t + 1 >= w, 1.0 / w, 1.0 / (t + 1).astype(F32)))
    return parts


def _pool_fwd(proj, P, tb):
    T = proj.shape[0]
    ng = len(POOL_WINDOWS)
    pg = P // ng
    hb = tb // POOL_HALO

    def body(v_ref, tail_ref, o_ref, ext):
        i = pl.program_id(0)
        ext[pl.ds(0, POOL_HALO), :] = jnp.where(i > 0, tail_ref[...], 0.0)
        ext[pl.ds(POOL_HALO, tb), :] = v_ref[...]
        inv = _pool_inv_count(i * tb, tb, pg, ng)
        for g, w in enumerate(POOL_WINDOWS):
            cols = pl.ds(g * pg, pg)
            win = ext[pl.ds(POOL_HALO, tb), cols]
            for k in range(1, w):
                win = win + ext[pl.ds(POOL_HALO - k, tb), cols]
            o_ref[:, cols] = (win * inv[g] - ext[pl.ds(POOL_HALO, tb), cols]).astype(BF16)

    return _call(body, grid=(T // tb,),
                 in_specs=[_bs((tb, P), lambda i: (i, 0)),
                           _bs((POOL_HALO, P), lambda i: (jnp.maximum(i * hb - 1, 0), 0))],
                 out_specs=_bs((tb, P), lambda i: (i, 0)), out_shape=jax.ShapeDtypeStruct((T, P), BF16),
                 scratch_shapes=[pltpu.VMEM((tb + POOL_HALO, P), F32)],
                 compiler_params=_cp(("arbitrary",)), name="pool_fwd")(proj, proj)


def _pool_bwd(dpooled, tb):
    T, P = dpooled.shape
    ng = len(POOL_WINDOWS)
    pg = P // ng
    hb = tb // POOL_HALO
    nb = T // tb

    def body(d_ref, head_ref, o_ref, ext):
        i = pl.program_id(0)
        inv = _pool_inv_count(i * tb, tb, pg, ng)
        invh = _pool_inv_count((i + 1) * tb, POOL_HALO, pg, ng)
        for g in range(ng):
            cols = pl.ds(g * pg, pg)
            ext[pl.ds(0, tb), cols] = d_ref[:, cols] * inv[g]
            ext[pl.ds(tb, POOL_HALO), cols] = jnp.where(i < nb - 1, head_ref[:, cols] * invh[g], 0.0)
        for g, w in enumerate(POOL_WINDOWS):
            cols = pl.ds(g * pg, pg)
            acc = ext[pl.ds(0, tb), cols]
            for k in range(1, w):
                acc = acc + ext[pl.ds(k, tb), cols]
            o_ref[:, cols] = (acc - d_ref[:, cols]).astype(BF16)

    return _call(body, grid=(nb,),
                 in_specs=[_bs((tb, P), lambda i: (i, 0)),
                           _bs((POOL_HALO, P), lambda i: (jnp.minimum((i + 1) * hb, T // POOL_HALO - 1), 0))],
                 out_specs=_bs((tb, P), lambda i: (i, 0)), out_shape=jax.ShapeDtypeStruct((T, P), BF16),
                 scratch_shapes=[pltpu.VMEM((tb + POOL_HALO, P), F32)],
                 compiler_params=_cp(("arbitrary",)), name="pool_bwd")(dpooled, dpooled)


def _zoh(a_re, a_im, ldt, b_re, b_im):
    lam_re = jnp.minimum(a_re, A_RE_MAX)
    lam_im = a_im
    dt = jnp.exp(ldt)
    mag = jnp.exp(lam_re * dt)
    ang = lam_im * dt
    ab_re = mag * jnp.cos(ang)
    ab_im = mag * jnp.sin(ang)
    den = lam_re * lam_re + lam_im * lam_im
    n_re = ab_re - 1.0
    n_im = ab_im
    q_re = (n_re * lam_re + n_im * lam_im) / den
    q_im = (n_im * lam_re - n_re * lam_im) / den
    return ab_re, ab_im, q_re * b_re - q_im * b_im, q_re * b_im + q_im * b_re


def _ssm_prep(a_re, a_im, ldt, bt_re, bt_im):
    shp = jax.ShapeDtypeStruct(a_re.shape, F32)

    def body(a, b, c, d, e, o0, o1, o2, o3):
        r = _zoh(a[...], b[...], c[...], d[...], e[...])
        o0[...], o1[...], o2[...], o3[...] = r

    return _call(body, in_specs=[VMEM_FULL] * 5, out_specs=[VMEM_FULL] * 4, out_shape=[shp] * 4,
                 name="ssm_prep")(a_re, a_im, ldt, bt_re, bt_im)


def _ssm_prep_bwd(a_re, a_im, ldt, bt_re, bt_im, dab_re, dab_im, dbb_re, dbb_im, G):
    GC, N = a_re.shape
    C = GC // G

    def body(a, b, c, d, e, g0, g1, g2, g3, da_re, da_im, dldt, db_re, db_im):
        _, vjp = jax.vjp(_zoh, a[...], b[...], c[...], d[...], e[...])
        ga_re, ga_im, gl, gb_re, gb_im = vjp((g0[...], g1[...], g2[...], g3[...]))
        da_re[...] = jnp.sum(ga_re.reshape(G, C, N), axis=1)
        da_im[...] = jnp.sum(ga_im.reshape(G, C, N), axis=1)
        dldt[...] = jnp.sum(jnp.sum(gl.reshape(G, C, N), axis=1), axis=1, keepdims=True)
        db_re[...] = gb_re
        db_im[...] = gb_im

    gn = jax.ShapeDtypeStruct((G, N), F32)
    full = jax.ShapeDtypeStruct((GC, N), F32)
    return _call(body, in_specs=[VMEM_FULL] * 9, out_specs=[VMEM_FULL] * 5,
                 out_shape=[gn, gn, jax.ShapeDtypeStruct((G, 1), F32), full, full],
                 name="ssm_prep_bwd")(a_re, a_im, ldt, bt_re, bt_im, dab_re, dab_im, dbb_re, dbb_im)


def _coef_tiles(abr, abi, reverse):
    ns = abr.shape[1]
    row = lax.broadcasted_iota(jnp.int32, (SUBLANES, ns), 0)
    ar = jnp.broadcast_to(abr, (SUBLANES, ns))
    ai = jnp.broadcast_to(-abi if reverse else abi, (SUBLANES, ns))
    a2r, a2i = ar * ar - ai * ai, 2.0 * ar * ai
    a4r, a4i = a2r * a2r - a2i * a2i, 2.0 * a2r * a2i
    out = []
    for d, (vr, vi) in ((1, (ar, ai)), (2, (a2r, a2i)), (4, (a4r, a4i))):
        keep = (row < SUBLANES - d) if reverse else (row >= d)
        out += [jnp.where(keep, vr, 0.0), jnp.where(keep, vi, 0.0)]
    pr, pi = ar, ai
    for k in range(1, SUBLANES):
        sel = (row <= SUBLANES - 1 - k) if reverse else (row >= k)
        nr, ni = pr * ar - pi * ai, pr * ai + pi * ar
        pr, pi = jnp.where(sel, nr, pr), jnp.where(sel, ni, pi)
    return out + [pr, pi]


def _cpow(ar, ai, n):
    out, br, bi = None, ar, ai
    while n:
        if n & 1:
            out = (br, bi) if out is None else (out[0] * br - out[1] * bi, out[0] * bi + out[1] * br)
        br, bi = br * br - bi * bi, 2.0 * br * bi
        n >>= 1
    return out


def _seg_order_rows(dst_ref, src, stage, nrows):
    seg = nrows // SUBLANES
    nl = stage.shape[0]
    for j in range(nl):
        stage[j] = src[:, j * LANES:(j + 1) * LANES]

    def step(i, _):
        rows = pl.ds(pl.multiple_of(i * SUBLANES, SUBLANES), SUBLANES)
        for j in range(nl):
            dst_ref[rows, j * LANES:(j + 1) * LANES] = stage[j, pl.ds(i, SUBLANES, stride=seg), :]
        return 0

    lax.fori_loop(0, seg, step, 0)


def _time_order_rows(src_ref, stage, nrows):
    seg = nrows // SUBLANES
    nl = stage.shape[0]

    def step(i, _):
        rows = pl.ds(pl.multiple_of(i * SUBLANES, SUBLANES), SUBLANES)
        for j in range(nl):
            stage[j, pl.ds(i, SUBLANES, stride=seg), :] = src_ref[rows, j * LANES:(j + 1) * LANES]
        return 0

    lax.fori_loop(0, seg, step, 0)
    return jnp.concatenate([stage[j] for j in range(nl)], axis=1)


def _seg_scan(xr_ref, xi_ref, abr_ref, abi_ref, coef_ref, car_ref, cai_ref, *, nrows, ns, reverse,
              cmat=None, dab=None):
    seg = nrows // SUBLANES
    cw = min(SCAN_LANES, ns)
    row = lax.broadcasted_iota(jnp.int32, (SUBLANES, cw), 0)
    first, last = (SUBLANES - 1, 0) if reverse else (0, SUBLANES - 1)

    def tile(i):
        return pl.ds(pl.multiple_of(((seg - 1 - i) if reverse else i) * SUBLANES, SUBLANES), SUBLANES)

    for cc in range(ns // cw):
        cols = pl.ds(cc * cw, cw)
        ar = jnp.broadcast_to(abr_ref[:, cols], (SUBLANES, cw))
        ai = jnp.broadcast_to(abi_ref[:, cols], (SUBLANES, cw))
        if reverse:
            ai = -ai

        def local(i, x, cols=cols, ar=ar, ai=ai):
            rows = tile(i)
            nr = ar * x[0] - ai * x[1] + xr_ref[rows, cols]
            ni = ar * x[1] + ai * x[0] + xi_ref[rows, cols]
            xr_ref[rows, cols] = nr
            xi_ref[rows, cols] = ni
            return nr, ni

        zero = jnp.zeros((SUBLANES, cw), F32)
        er, ei = lax.fori_loop(0, seg, local, (zero, zero))

        co = [coef_ref[k, :, cols] for k in range(8)]
        for lvl, d in enumerate((1, 2, 4)):
            kr, ki = co[2 * lvl], co[2 * lvl + 1]
            sh = SUBLANES - d if reverse else d
            sr, si = pltpu.roll(er, sh, 0), pltpu.roll(ei, sh, 0)
            er, ei = er + (kr * sr - ki * si), ei + (kr * si + ki * sr)
        c0r, c0i = car_ref[:, cols], cai_ref[:, cols]
        er, ei = er + (co[6] * c0r - co[7] * c0i), ei + (co[6] * c0i + co[7] * c0r)
        nb_shift = SUBLANES - 1 if reverse else 1
        cmr = jnp.where(row == first, c0r, pltpu.roll(er, nb_shift, 0))
        cmi = jnp.where(row == first, c0i, pltpu.roll(ei, nb_shift, 0))
        car_ref[:, cols] = jnp.broadcast_to(er[last:last + 1, :], er.shape)
        cai_ref[:, cols] = jnp.broadcast_to(ei[last:last + 1, :], ei.shape)
        if cmat is not None:
            cmat[0][:, cols] = cmr
            cmat[1][:, cols] = cmi

        w0 = (ar * cmr - ai * cmi, ar * cmi + ai * cmr)
        if dab is None:
            def fix(i, w, cols=cols, ar=ar, ai=ai):
                rows = tile(i)
                xr_ref[rows, cols] = xr_ref[rows, cols] + w[0]
                xi_ref[rows, cols] = xi_ref[rows, cols] + w[1]
                return ar * w[0] - ai * w[1], ar * w[1] + ai * w[0]

            lax.fori_loop(0, seg, fix, w0)
        else:
            s_re, s_im, e_re, e_im, o_re, o_im = dab

            def add(rows, w, pr, pi, acc):
                gr = xr_ref[rows, cols] + w[0]
                gi = xi_ref[rows, cols] + w[1]
                xr_ref[rows, cols] = gr
                xi_ref[rows, cols] = gi
                return acc[0] + (gr * pr + gi * pi), acc[1] + (gi * pr - gr * pi)

            def fix(i, st, cols=cols, ar=ar, ai=ai):
                w, acc = st[:2], st[2:]
                rows = tile(i)
                before = pl.ds(pl.multiple_of((seg - 2 - i) * SUBLANES, SUBLANES), SUBLANES)
                acc = add(rows, w, s_re[before, cols], s_im[before, cols], acc)
                return (ar * w[0] - ai * w[1], ar * w[1] + ai * w[0]) + acc

            st = lax.fori_loop(0, seg - 1, fix, w0 + (zero, zero))
            acc = add(pl.ds(0, SUBLANES), st[:2], e_re[:, cols], e_im[:, cols], st[2:])
            o_re[:, cols] += jnp.sum(acc[0], axis=0, keepdims=True)
            o_im[:, cols] += jnp.sum(acc[1], axis=0, keepdims=True)


def _ssm_fwd(proj, bdr, bdi, cdr, cdi, abr, abi, dsk, P, tb):
    T = proj.shape[0]
    ntl, ct, st = bdr.shape
    ns = ntl * st
    nb = T // tb

    def body(u_ref, bdr_ref, bdi_ref, cdr_ref, cdi_ref, abr_ref, abi_ref, d_ref,
             y_ref, ge_ref, bsr_ref, bsi_ref, sr, si, coef, car, cai, up, stage):
        @pl.when(pl.program_id(0) == 0)
        def _():
            seg_pow = _cpow(abr_ref[...], abi_ref[...], tb // SUBLANES)
            for k, tile in enumerate(_coef_tiles(seg_pow[0], seg_pow[1], False)):
                coef[k] = tile
            car[...] = jnp.zeros_like(car)
            cai[...] = jnp.zeros_like(cai)

        bsr_ref[...] = car[...]
        bsi_ref[...] = cai[...]
        _seg_order_rows(up, u_ref[...], stage, tb)
        u = up[...]
        ub = u.astype(BF16)
        for s in range(ntl):
            us = ub[:, s * ct:(s + 1) * ct]
            sr[:, s * st:(s + 1) * st] = jnp.dot(us, bdr_ref[s], preferred_element_type=F32)
            si[:, s * st:(s + 1) * st] = jnp.dot(us, bdi_ref[s], preferred_element_type=F32)
        _seg_scan(sr, si, abr_ref, abi_ref, coef, car, cai, nrows=tb, ns=ns, reverse=False)
        for s in range(ntl):
            s_re = sr[:, s * st:(s + 1) * st].astype(BF16)
            s_im = si[:, s * st:(s + 1) * st].astype(BF16)
            up[:, s * ct:(s + 1) * ct] = (jnp.dot(s_re, cdr_ref[s], preferred_element_type=F32)
                                          - jnp.dot(s_im, cdi_ref[s], preferred_element_type=F32)
                                          + d_ref[:, s * ct:(s + 1) * ct] * u[:, s * ct:(s + 1) * ct])
        y = _time_order_rows(up, stage, tb)
        y_ref[...] = y
        ge_ref[...] = _gelu(y).astype(BF16)

    full3 = lambda a: _bs(a.shape, lambda i: (0, 0, 0))
    vec = lambda n: _bs((1, n), lambda i: (0, 0))
    row = _bs((tb, P), lambda i: (i, 0))
    st_spec = _bs((None, SUBLANES, ns), lambda i: (i, 0, 0))
    return _call(body, grid=(nb,),
                 in_specs=[_bs((tb, P), lambda i: (i, 2)), full3(bdr), full3(bdi), full3(cdr), full3(cdi),
                           vec(ns), vec(ns), vec(P)],
                 out_specs=[row, row, st_spec, st_spec],
                 out_shape=[jax.ShapeDtypeStruct((T, P), F32), jax.ShapeDtypeStruct((T, P), BF16),
                            jax.ShapeDtypeStruct((nb, SUBLANES, ns), F32),
                            jax.ShapeDtypeStruct((nb, SUBLANES, ns), F32)],
                 scratch_shapes=[pltpu.VMEM((tb, ns), F32), pltpu.VMEM((tb, ns), F32),
                                 pltpu.VMEM((8, SUBLANES, ns), F32),
                                 pltpu.VMEM((SUBLANES, ns), F32), pltpu.VMEM((SUBLANES, ns), F32),
                                 pltpu.VMEM((tb, P), F32), pltpu.VMEM((P // LANES, tb, LANES), F32)],
                 compiler_params=_cp(("arbitrary",)), name="ssm_fwd")(proj, bdr, bdi, cdr, cdi, abr, abi, dsk)


def _ssm_bwd(proj, y, dge, bsr, bsi, bdr, bdi, cdr, cdi, abr, abi, dsk, P, tb):
    T = proj.shape[0]
    ntl, ct, st = bdr.shape
    ns = ntl * st
    nb = T // tb

    def body(u_ref, y_ref, dge_ref, bsr_ref, bsi_ref, abr_ref, abi_ref, d_ref, bdr_h, bdi_h, cdr_h, cdi_h,
             du_ref, dabr_ref, dabi_ref, dd_ref, dbdr_h, dbdi_h, dcdr_h, dcdi_h,
             wbdr, wbdi, wcdr, wcdi, abdr, abdi, acdr, acdi, spr, spi, gr, gi, coef_f, coef_r,
             car, cai, gcr, gci, ser, sei, up, dyp, dup, stage):
        i = pl.program_id(0)

        @pl.when(i == 0)
        def _():
            for h, w in ((bdr_h, wbdr), (bdi_h, wbdi), (cdr_h, wcdr), (cdi_h, wcdi)):
                pltpu.sync_copy(h, w)
            for a in (abdr, abdi, acdr, acdi, gcr, gci):
                a[...] = jnp.zeros_like(a)
            for o in (dabr_ref, dabi_ref, dd_ref):
                o[...] = jnp.zeros_like(o)
            seg_pow = _cpow(abr_ref[...], abi_ref[...], tb // SUBLANES)
            for k, tile in enumerate(_coef_tiles(seg_pow[0], seg_pow[1], False)):
                coef_f[k] = tile
            for k, tile in enumerate(_coef_tiles(seg_pow[0], seg_pow[1], True)):
                coef_r[k] = tile

        car[...] = bsr_ref[...]
        cai[...] = bsi_ref[...]
        _seg_order_rows(up, u_ref[...], stage, tb)
        _seg_order_rows(dyp, dge_ref[...] * _gelu_grad(y_ref[...]), stage, tb)
        u = up[...]
        ub = u.astype(BF16)
        for s in range(ntl):
            us = ub[:, s * ct:(s + 1) * ct]
            spr[:, s * st:(s + 1) * st] = jnp.dot(us, wbdr[s], preferred_element_type=F32)
            spi[:, s * st:(s + 1) * st] = jnp.dot(us, wbdi[s], preferred_element_type=F32)
        _seg_scan(spr, spi, abr_ref, abi_ref, coef_f, car, cai, nrows=tb, ns=ns, reverse=False, cmat=(ser, sei))

        dy = dyp[...]
        dyb = dy.astype(BF16)
        for s in range(ntl):
            dys = dyb[:, s * ct:(s + 1) * ct]
            gr[:, s * st:(s + 1) * st] = lax.dot_general(dys, wcdr[s], (NT, ((), ())), preferred_element_type=F32)
            gi[:, s * st:(s + 1) * st] = -lax.dot_general(dys, wcdi[s], (NT, ((), ())), preferred_element_type=F32)
        _seg_scan(gr, gi, abr_ref, abi_ref, coef_r, gcr, gci, nrows=tb, ns=ns, reverse=True,
                  dab=(spr, spi, ser, sei, dabr_ref, dabi_ref))

        for s in range(ntl):
            sl_c, sl_s = slice(s * ct, (s + 1) * ct), slice(s * st, (s + 1) * st)
            s_re = spr[:, sl_s].astype(BF16)
            s_im = spi[:, sl_s].astype(BF16)
            g_re, g_im = gr[:, sl_s].astype(BF16), gi[:, sl_s].astype(BF16)
            dys, us = dyb[:, sl_c], ub[:, sl_c]
            acdr[s] += lax.dot_general(s_re, dys, (TN, ((), ())), preferred_element_type=F32)
            acdi[s] -= lax.dot_general(s_im, dys, (TN, ((), ())), preferred_element_type=F32)
            abdr[s] += lax.dot_general(us, g_re, (TN, ((), ())), preferred_element_type=F32)
            abdi[s] += lax.dot_general(us, g_im, (TN, ((), ())), preferred_element_type=F32)
            dup[:, sl_c] = (lax.dot_general(g_re, wbdr[s], (NT, ((), ())), preferred_element_type=F32)
                            + lax.dot_general(g_im, wbdi[s], (NT, ((), ())), preferred_element_type=F32)
                            + d_ref[:, sl_c] * dy[:, sl_c])
        dd_ref[...] += jnp.sum(dy * u, axis=0, keepdims=True)
        du_ref[...] = _time_order_rows(dup, stage, tb).astype(BF16)

        @pl.when(i == nb - 1)
        def _():
            for a, h in ((abdr, dbdr_h), (abdi, dbdi_h), (acdr, dcdr_h), (acdi, dcdi_h)):
                pltpu.sync_copy(a, h)

    rev = lambda i: nb - 1 - i
    vec = lambda n: _bs((1, n), lambda i: (0, 0))
    row = _bs((tb, P), lambda i: (rev(i), 0))
    st_spec = _bs((None, SUBLANES, ns), lambda i: (rev(i), 0, 0))
    bshape = jax.ShapeDtypeStruct(bdr.shape, F32)
    cshape = jax.ShapeDtypeStruct(cdr.shape, F32)
    return _call(body, grid=(nb,),
                 in_specs=[_bs((tb, P), lambda i: (rev(i), 2)), row, row, st_spec, st_spec,
                           vec(ns), vec(ns), vec(P), ANY, ANY, ANY, ANY],
                 out_specs=[row, vec(ns), vec(ns), vec(P), ANY, ANY, ANY, ANY],
                 out_shape=[jax.ShapeDtypeStruct((T, P), BF16), jax.ShapeDtypeStruct((1, ns), F32),
                            jax.ShapeDtypeStruct((1, ns), F32), jax.ShapeDtypeStruct((1, P), F32),
                            bshape, bshape, cshape, cshape],
                 scratch_shapes=[pltpu.VMEM(bdr.shape, BF16), pltpu.VMEM(bdr.shape, BF16),
                                 pltpu.VMEM(cdr.shape, BF16), pltpu.VMEM(cdr.shape, BF16),
                                 pltpu.VMEM(bdr.shape, F32), pltpu.VMEM(bdr.shape, F32),
                                 pltpu.VMEM(cdr.shape, F32), pltpu.VMEM(cdr.shape, F32),
                                 pltpu.VMEM((tb, ns), F32), pltpu.VMEM((tb, ns), F32),
                                 pltpu.VMEM((tb, ns), F32), pltpu.VMEM((tb, ns), F32),
                                 pltpu.VMEM((8, SUBLANES, ns), F32), pltpu.VMEM((8, SUBLANES, ns), F32)]
                 + [pltpu.VMEM((SUBLANES, ns), F32)] * 6 + [pltpu.VMEM((tb, P), F32)] * 3
                 + [pltpu.VMEM((P // LANES, tb, LANES), F32)],
                 compiler_params=_cp(("arbitrary",)), name="ssm_bwd")(
                     proj, y, dge, bsr, bsi, abr, abi, dsk, bdr, bdi, cdr, cdi)


def _adamw(w, g, m, v, name):
    R, C = w.shape
    tr = _t(R, 256)

    def body(w_ref, g_ref, m_ref, v_ref, d_ref, mo_ref, vo_ref):
        gv = g_ref[...]
        mn = ADAM_B1 * m_ref[...] + (1.0 - ADAM_B1) * gv
        vn = ADAM_B2 * v_ref[...] + (1.0 - ADAM_B2) * (gv * gv)
        m_hat = mn / (1.0 - ADAM_B1 ** ADAM_STEP)
        v_hat = vn / (1.0 - ADAM_B2 ** ADAM_STEP)
        d_ref[...] = -ADAM_LR * (m_hat / (jnp.sqrt(v_hat) + ADAM_EPS) + ADAM_WD * w_ref[...])
        mo_ref[...] = mn
        vo_ref[...] = vn

    blk = _bs((tr, C), lambda i: (i, 0))
    shp = jax.ShapeDtypeStruct((R, C), F32)
    return _call(body, grid=(R // tr,), in_specs=[blk] * 4, out_specs=[blk] * 3, out_shape=[shp] * 3,
                 compiler_params=_cp(("parallel",)), name=name)(w, g, m, v)


def _sum_cast(grad, got, place, name):
    J, H, C = got.shape
    tr = _t(H, 256)
    nb = H // tr

    def body(pl_ref, a_ref, b_ref, o_ref):
        o_ref[...] = (a_ref[...] + b_ref[...]).astype(BF16)

    blk = _bs((None, tr, C), lambda j, i, pc: (j, i, 0))
    mine = _bs((None, tr, C), lambda j, i, pc: (j, pc[1] * nb + i, 0))
    spec = pltpu.PrefetchScalarGridSpec(num_scalar_prefetch=1, grid=(J, nb), in_specs=[mine, blk], out_specs=blk)
    return _call(body, grid_spec=spec, out_shape=jax.ShapeDtypeStruct((J, H, C), BF16),
                 compiler_params=_cp(("parallel", "parallel")), name=name)(place, grad, got)


def _sum_chips(sent, arrived, place, name):
    J, H, C = arrived.shape
    tr = _t(H, 256)
    nb = H // tr

    def body(pl_ref, own_ref, a0_ref, a1_ref, a2_ref, o_ref):
        acc = own_ref[...].astype(F32)
        for r in (a0_ref, a1_ref, a2_ref):
            acc = acc + r[...].astype(F32)
        o_ref[...] = acc

    def other(k):
        return _bs((None, tr, C), lambda i, pc: (jnp.where(pc[0] <= k, k + 1, k), i, 0))

    spec = pltpu.PrefetchScalarGridSpec(
        num_scalar_prefetch=1, grid=(nb,),
        in_specs=[_bs((None, tr, C), lambda i, pc: (pc[0], i, 0)), other(0), other(1), other(2)],
        out_specs=_bs((tr, C), lambda i, pc: (pc[1] * nb + i, 0)))
    return _call(body, grid_spec=spec, out_shape=jax.ShapeDtypeStruct((2 * H, C), F32),
                 compiler_params=_cp(("parallel",)), name=name)(place, sent, arrived, arrived, arrived)


def _place():
    x, y, c = lax.axis_index("x"), lax.axis_index("y"), lax.axis_index("c")
    chips = [(1 - x, y), (x, 1 - y), (1 - x, 1 - y)]
    return x, y, c, chips


def _split(nrows, row_bytes, align, cap=None):
    k = max(1, min(cap or DMA_MAX_CHUNKS, (nrows * row_bytes) // DMA_CHUNK_BYTES))
    while k > 1 and nrows % (k * align):
        k -= 1
    return k


def _ag_weights(shards, axes):
    n = len(shards)
    shapes = [a.shape for a in shards]

    def window(ref, i, chip, half=None):
        S, ax = shapes[i], axes[i]
        idx = []
        for d in range(len(S)):
            off, size = 0, S[d]
            if d == 0 and half is not None:
                off, size = half * (S[0] // 2), S[0] // 2
            if d == ax:
                off = off + chip * S[ax]
            idx.append(pl.ds(off, size))
        return ref.at[tuple(idx)]

    def body(*refs):
        src, full = refs[:n], refs[n:2 * n]
        ssem, rsem = refs[2 * n:]
        x, y, c, chips = _place()
        me = 2 * x + y
        sib = (x, y, 1 - c)
        idx = [2 * cx + cy for cx, cy in chips]

        def rcopy(i, k, s_ref, d_ref, to):
            return pltpu.make_async_remote_copy(src_ref=s_ref, dst_ref=d_ref, send_sem=ssem.at[i, k],
                                                recv_sem=rsem.at[i, k], device_id=to, device_id_type=MESH)

        def ici(i, j, chip):
            half_src = src[i].at[pl.ds(c * (shapes[i][0] // 2), shapes[i][0] // 2)]
            return rcopy(i, j, half_src, window(full[i], i, chip, c), (*chips[j], c))

        def fwd(i, j, half):
            w = window(full[i], i, idx[j], half)
            return rcopy(i, 3 + j, w, w, sib)

        def own(i):
            return rcopy(i, 6, src[i], window(full[i], i, me), sib)

        for i in range(n):
            for j in range(3):
                ici(i, j, me).start()
        for i in range(n):
            own(i).start()
        for i in range(n):
            for j in range(3):
                ici(i, j, idx[j]).wait_recv()
                fwd(i, j, c).start()
        for i in range(n):
            for j in range(3):
                fwd(i, j, 1 - c).wait_recv()
            own(i).wait()
        for i in range(n):
            for j in range(3):
                ici(i, j, me).wait_send()
                fwd(i, j, c).wait_send()

    out_shape = [jax.ShapeDtypeStruct(tuple(N_CHIP * d if k == ax else d for k, d in enumerate(S)), BF16)
                 for S, ax in zip(shapes, axes)]
    return _call(body, in_specs=[ANY] * n, out_specs=[ANY] * n, out_shape=out_shape,
                 scratch_shapes=[pltpu.SemaphoreType.DMA((n, 7)), pltpu.SemaphoreType.DMA((n, 7))],
                 name="ag_weights")(*shards)


def _halves_to_sibling(grads):
    n = len(grads)

    def body(*refs):
        g, got = refs[:n], refs[n:2 * n]
        ssem, rsem = refs[2 * n:]
        x, y, c, _ = _place()
        sib = (x, y, 1 - c)
        for i in range(n):
            J, R, C = g[i].shape
            H = R // 2
            k = _split(H, C * 4, SUBLANES, cap=DMA_MAX_CHUNKS // J)
            hr = H // k
            for j in range(J):
                for q in range(k):
                    other = pl.ds(pl.multiple_of((1 - c) * H + q * hr, SUBLANES), hr)
                    to = pl.ds(q * hr, hr)
                    pltpu.make_async_remote_copy(src_ref=g[i].at[j, other, :], dst_ref=got[i].at[j, to, :],
                                                 send_sem=ssem.at[i], recv_sem=rsem.at[i],
                                                 device_id=sib, device_id_type=MESH).start()
        for i in range(n):
            pltpu.make_async_remote_copy(src_ref=got[i], dst_ref=got[i], send_sem=ssem.at[i], recv_sem=rsem.at[i],
                                         device_id=sib, device_id_type=MESH).wait()

    half = [jax.ShapeDtypeStruct((a.shape[0], a.shape[1] // 2, a.shape[2]), a.dtype) for a in grads]
    return _call(body, in_specs=[ANY] * n, out_specs=[ANY] * n, out_shape=half,
                 scratch_shapes=[pltpu.SemaphoreType.DMA((n,)), pltpu.SemaphoreType.DMA((n,))],
                 name="rs_halves")(*grads)


def _scatter_to_chips(parts):
    n = len(parts)

    def body(*refs):
        s, got = refs[:n], refs[n:2 * n]
        ssem, rsem = refs[2 * n:]
        x, y, c, chips = _place()
        me = 2 * x + y
        idx = [2 * cx + cy for cx, cy in chips]
        for i in range(n):
            _, H, C = s[i].shape
            k = _split(H, C * 2, 16, cap=RS_CHUNKS)
            hr = H // k
            for q in range(k):
                rows = pl.ds(q * hr, hr)
                for j in range(3):
                    pltpu.make_async_remote_copy(src_ref=s[i].at[idx[j], rows, :], dst_ref=got[i].at[me, rows, :],
                                                 send_sem=ssem.at[i, j], recv_sem=rsem.at[i, j],
                                                 device_id=(*chips[j], c), device_id_type=MESH).start()
        for i in range(n):
            for j in range(3):
                pltpu.make_async_remote_copy(src_ref=s[i].at[idx[j]], dst_ref=got[i].at[idx[j]],
                                             send_sem=ssem.at[i, j], recv_sem=rsem.at[i, j],
                                             device_id=(*chips[j], c), device_id_type=MESH).wait()

    shp = [jax.ShapeDtypeStruct(a.shape, a.dtype) for a in parts]
    return _call(body, in_specs=[ANY] * n, out_specs=[ANY] * n, out_shape=shp,
                 scratch_shapes=[pltpu.SemaphoreType.DMA((n, 3)), pltpu.SemaphoreType.DMA((n, 3))],
                 name="rs_chips")(*parts)


def _join_halves(shards):
    n = len(shards)

    def body(*refs):
        full = refs[n:2 * n]
        ssem, rsem = refs[2 * n:]
        x, y, c, _ = _place()
        sib = (x, y, 1 - c)
        for i in range(n):
            H, C = full[i].shape[0] // 2, full[i].shape[1]
            k = _split(H, C * 4, SUBLANES)
            hr = H // k
            for q in range(k):
                rows = pl.ds(pl.multiple_of(c * H + q * hr, SUBLANES), hr)
                pltpu.make_async_remote_copy(src_ref=full[i].at[rows], dst_ref=full[i].at[rows],
                                             send_sem=ssem.at[i], recv_sem=rsem.at[i],
                                             device_id=sib, device_id_type=MESH).start()
        for i in range(n):
            half = full[i].at[pl.ds(0, full[i].shape[0] // 2)]
            pltpu.make_async_remote_copy(src_ref=half, dst_ref=half, send_sem=ssem.at[i], recv_sem=rsem.at[i],
                                         device_id=sib, device_id_type=MESH).wait()

    shp = [jax.ShapeDtypeStruct(a.shape, a.dtype) for a in shards]
    return _call(body, in_specs=[ANY] * n, out_specs=[ANY] * n, out_shape=shp,
                 input_output_aliases={i: i for i in range(n)},
                 scratch_shapes=[pltpu.SemaphoreType.DMA((n,)), pltpu.SemaphoreType.DMA((n,))],
                 name="rs_join")(*shards)


def _allreduce_small(buf):
    R, L = buf.shape
    RB = R // N_DEV

    def body(x_ref, o_ref, got, ssem, rsem):
        x, y, c, _ = _place()
        me = 4 * x + 2 * y + c

        def dev(k):
            return (k // 4, (k // 2) % 2, k % 2)

        def slab(k):
            return pl.ds(pl.multiple_of(k * RB, SUBLANES), RB)

        sends = []
        for d in range(1, N_DEV):
            peer = (me + d) % N_DEV
            cp = pltpu.make_async_remote_copy(src_ref=x_ref.at[slab(peer)], dst_ref=got.at[me],
                                              send_sem=ssem.at[0, d], recv_sem=rsem.at[0, d],
                                              device_id=dev(peer), device_id_type=MESH)
            cp.start()
            sends.append(cp)
        got[me] = x_ref[slab(me), :]
        for d in range(1, N_DEV):
            src = (me + N_DEV - d) % N_DEV
            pltpu.make_async_remote_copy(src_ref=x_ref.at[slab(me)], dst_ref=got.at[src],
                                         send_sem=ssem.at[0, d], recv_sem=rsem.at[0, d],
                                         device_id=dev(src), device_id_type=MESH).wait_recv()
        acc = got[0]
        for k in range(1, N_DEV):
            acc = acc + got[k]
        o_ref[slab(me), :] = acc
        for d in range(1, N_DEV):
            peer = (me + d) % N_DEV
            cp = pltpu.make_async_remote_copy(src_ref=o_ref.at[slab(me)], dst_ref=o_ref.at[slab(me)],
                                              send_sem=ssem.at[1, d], recv_sem=rsem.at[1, d],
                                              device_id=dev(peer), device_id_type=MESH)
            cp.start()
            sends.append(cp)
        for d in range(1, N_DEV):
            src = (me + N_DEV - d) % N_DEV
            pltpu.make_async_remote_copy(src_ref=o_ref.at[slab(src)], dst_ref=o_ref.at[slab(src)],
                                         send_sem=ssem.at[1, d], recv_sem=rsem.at[1, d],
                                         device_id=dev(src), device_id_type=MESH).wait_recv()
        for cp in sends:
            cp.wait_send()

    return _call(body, in_specs=[VMEM_FULL], out_specs=VMEM_FULL, out_shape=jax.ShapeDtypeStruct((R, L), F32),
                 scratch_shapes=[pltpu.VMEM((N_DEV, RB, L), F32), pltpu.SemaphoreType.DMA((2, N_DEV)),
                                 pltpu.SemaphoreType.DMA((2, N_DEV))],
                 name="allreduce_small")(buf)


def _block_diag(t, gt):
    G, A, B = t.shape
    t4 = t.reshape(G // gt, gt, A, B)
    eye = jnp.eye(gt, dtype=t.dtype)
    return jnp.einsum('sgab,gh->sgahb', t4, eye).reshape(G // gt, gt * A, gt * B)


def _block_diag_extract(m, gt, A, B):
    S = m.shape[0]
    m5 = m.reshape(S, gt, A, gt, B)
    eye = jnp.eye(gt, dtype=m.dtype)
    return jnp.einsum('sgahb,gh->sgab', m5, eye).reshape(S * gt, A, B)


def _pack_small(arrs, rows):
    flat = jnp.concatenate([a.reshape(-1).astype(F32) for a in arrs])
    return jnp.pad(flat, (0, rows * LANES - flat.shape[0])).reshape(rows, LANES)


def _unpack_small(buf, shapes):
    flat = buf.reshape(-1)
    out, off = [], 0
    for s in shapes:
        n = 1
        for d in s:
            n *= d
        out.append(flat[off:off + n].reshape(s))
        off += n
    return out


def kernel(x, p, norm_gain, w_in, w_pool, pool_scale, a_re, a_im, log_dt, b_re, b_im, c_re, c_im, d_skip, w_glu, w_out, w_ple, w_ple_gate, final_gain, loss_target, m_norm_gain, m_w_in, m_w_pool, m_pool_scale, m_a_re, m_a_im, m_log_dt, m_b_re, m_b_im, m_c_re, m_c_im, m_d_skip, m_w_glu, m_w_out, m_w_ple, m_w_ple_gate, m_final_gain, v_norm_gain, v_w_in, v_w_pool, v_pool_scale, v_a_re, v_a_im, v_log_dt, v_b_re, v_b_im, v_c_re, v_c_im, v_d_skip, v_w_glu, v_w_out, v_w_ple, v_w_ple_gate, v_final_gain):
    xs, pe, tgt = x[0], p[0, 0], loss_target[0]
    T, D = xs.shape
    E = pe.shape[1]
    P = D // 2
    NG = len(POOL_WINDOWS)
    PG = P // NG
    G, N, C = P // SSM_GROUP, SSM_STATE, SSM_GROUP
    GT = min(SSM_TILE_GROUPS, G)
    Q = D // N_CHIP

    big = {"w_in": (w_in, m_w_in, v_w_in), "w_pool": (w_pool, m_w_pool, v_w_pool),
           "w_glu": (w_glu, m_w_glu, v_w_glu), "w_out": (w_out, m_w_out, v_w_out),
           "w_ple": (w_ple, m_w_ple, v_w_ple), "w_ple_gate": (w_ple_gate, m_w_ple_gate, v_w_ple_gate)}
    big_names = list(big)
    shard2d = {n: (big[n][0].size // big[n][0].shape[-1], big[n][0].shape[-1]) for n in big_names}
    shard_axis = {"w_in": 1, "w_pool": 1, "w_glu": 1, "w_out": 0, "w_ple": 1, "w_ple_gate": 0}
    win, wp, wglu, wout, wple, wpg = _ag_weights([big[n][0][0].astype(BF16) for n in big_names],
                                                 [shard_axis[n] for n in big_names])

    rep = lambda a: jnp.repeat(a, C, axis=0)
    a_re_r, a_im_r = rep(a_re[0]), rep(a_im[0])
    ldt_r = rep(jnp.broadcast_to(log_dt[0][:, None], (G, N)))
    bt_re = b_re[0].transpose(0, 2, 1).reshape(G * C, N)
    bt_im = b_im[0].transpose(0, 2, 1).reshape(G * C, N)
    ab_re_r, ab_im_r, bbt_re, bbt_im = _ssm_prep(a_re_r, a_im_r, ldt_r, bt_re, bt_im)
    abr = ab_re_r[::C].reshape(1, G * N)
    abi = ab_im_r[::C].reshape(1, G * N)
    bdr = _block_diag(bbt_re.reshape(G, C, N), GT).astype(BF16)
    bdi = _block_diag(bbt_im.reshape(G, C, N), GT).astype(BF16)
    cdr = _block_diag(c_re[0].transpose(0, 2, 1), GT).astype(BF16)
    cdi = _block_diag(c_im[0].transpose(0, 2, 1), GT).astype(BF16)

    tb = _t(T, 256)
    tbs = _t(T, 256)
    tm = _t(T, 1024)
    tk = _t(T, 2048)
    DH = _t(D, 1024)
    row_k = lambda i, n, k: (i, k)
    row_n = lambda i, n, k: (i, n)
    f32 = lambda *shape: jax.ShapeDtypeStruct(shape, F32)
    hn = _norm1(xs, norm_gain, tb)
    proj = _mm(hn, win, dims=NN, grid=(T // tm, N_CHIP, 1),
               a_spec=_bs((tm, D), row_k), b_spec=_bs((D, P), lambda i, n, k: (k, n)),
               o_spec=_bs((tm, P), row_n), out_shape=f32(T, 4 * P), name="mm_proj")
    pooled = _pool_fwd(proj, P, tb)
    mixed = _mm(pooled, wp, dims=NN, grid=(T // tm, NG, 1),
                a_spec=_bs((tm, PG), row_n), b_spec=_bs((None, PG, PG), lambda i, g, k: (g, 0, 0)),
                o_spec=_bs((tm, PG), row_n), out_shape=f32(T, P), name="mm_pool")
    y, ge, bsr, bsi = _ssm_fwd(proj, bdr, bdi, cdr, cdi, abr, abi, d_skip, P, tbs)
    hg = _mm(ge, wglu, dims=NN, grid=(T // tm, 2 * P // DH, 1),
             a_spec=_bs((tm, P), row_k), b_spec=_bs((P, DH), lambda i, n, k: (k, n)),
             o_spec=_bs((tm, DH), row_n), out_shape=f32(T, 2 * P), name="mm_glu")
    cat = _gate_fwd(mixed, proj, hg, pool_scale, tb)
    h1, h1b = _mm(cat, wout, dims=NN, grid=(T // tm, D // DH, 1), res=xs, bf16_copy=True,
                  a_spec=_bs((tm, D), row_k), b_spec=_bs((D, DH), lambda i, n, k: (k, n)),
                  r_spec=_bs((tm, DH), row_n), o_spec=_bs((tm, DH), row_n), out_shape=f32(T, D), name="mm_out")
    e = _mm(pe, wple, dims=NN, grid=(T // tm, D // DH, 1),
            a_spec=_bs((tm, E), row_k), b_spec=_bs((E, DH), lambda i, n, k: (k, n)),
            o_spec=_bs((tm, DH), row_n), out_shape=f32(T, D), name="mm_ple")
    z = _mm(h1b, wpg, dims=NN, grid=(T // tm, D // DH, 1),
            a_spec=_bs((tm, D), row_k), b_spec=_bs((D, DH), lambda i, n, k: (k, n)),
            o_spec=_bs((tm, DH), row_n), out_shape=f32(T, D), name="mm_pgate")
    dh2, de, dz, dg2, lpart = _final_fb(h1, e, z, tgt, final_gain.reshape(1, D), tb)

    col_m = lambda m, n, k: (k, m)
    col_n = lambda m, n, k: (k, n)
    dh1, dh1b = _mm(dz, wpg, dims=NT, grid=(T // tm, D // DH, 1), res=dh2, bf16_copy=True,
                    a_spec=_bs((tm, D), row_k), b_spec=_bs((DH, D), lambda i, n, k: (n, k)),
                    r_spec=_bs((tm, DH), row_n), o_spec=_bs((tm, DH), row_n), out_shape=f32(T, D), name="mm_dh1")
    g_wpg = _mm(h1b, dz, dims=TN, grid=(D // DH, D // DH, T // tk),
                a_spec=_bs((tk, DH), col_m), b_spec=_bs((tk, DH), col_n),
                o_spec=_bs((DH, DH), lambda m, n, k: (m, n)), out_shape=f32(D, D), name="mm_gwpg")
    g_wple = _mm(pe, de, dims=TN, grid=(1, N_CHIP, T // tk),
                 a_spec=_bs((tk, E), col_m), b_spec=_bs((tk, Q), col_n),
                 o_spec=_bs((None, E, Q), lambda m, j, k: (j, 0, 0)), out_shape=f32(N_CHIP, E, Q), name="mm_gwple")
    dcat = _mm(dh1b, wout, dims=NT, grid=(T // tm, D // DH, 1),
               a_spec=_bs((tm, D), row_k), b_spec=_bs((DH, D), lambda i, n, k: (n, k)),
               o_spec=_bs((tm, DH), row_n), out_shape=f32(T, D), name="mm_dcat")
    g_wout = _mm(cat, dh1b, dims=TN, grid=(D // DH, D // DH, T // tk),
                 a_spec=_bs((tk, DH), col_m), b_spec=_bs((tk, DH), col_n),
                 o_spec=_bs((DH, DH), lambda m, n, k: (m, n)), out_shape=f32(D, D), name="mm_gwout")
    dmixed, dpg, dsg, dhg, dps = _gate_bwd(dcat, mixed, proj, hg, pool_scale, tb)
    dge = _mm(dhg, wglu, dims=NT, grid=(T // tm, 1, 1),
              a_spec=_bs((tm, 2 * P), row_k), b_spec=_bs((P, 2 * P), lambda i, n, k: (n, k)),
              o_spec=_bs((tm, P), row_n), out_shape=f32(T, P), name="mm_dge")
    g_wglu = _mm(ge, dhg, dims=TN, grid=(1, N_CHIP, T // tk),
                 a_spec=_bs((tk, P), col_m), b_spec=_bs((tk, Q), col_n),
                 o_spec=_bs((None, P, Q), lambda m, j, k: (j, 0, 0)), out_shape=f32(N_CHIP, P, Q), name="mm_gwglu")
    du, dabr, dabi, dd, dbdr, dbdi, dcdr, dcdi = _ssm_bwd(proj, y, dge, bsr, bsi, bdr, bdi, cdr, cdi,
                                                          abr, abi, d_skip, P, tbs)
    dpooled = _mm(dmixed, wp, dims=NT, grid=(T // tm, NG, 1),
                  a_spec=_bs((tm, PG), row_n), b_spec=_bs((None, PG, PG), lambda i, g, k: (g, 0, 0)),
                  o_spec=_bs((tm, PG), row_n), out_shape=f32(T, P), name="mm_dpooled")
    g_wp = _mm(pooled, dmixed, dims=TN, grid=(NG, 1, T // tk),
               a_spec=_bs((tk, PG), col_m), b_spec=_bs((tk, PG), col_m),
               o_spec=_bs((None, PG, PG), lambda g, n, k: (g, 0, 0)), out_shape=f32(NG, PG, PG), name="mm_gwp")
    dpi = _pool_bwd(dpooled, tb)
    dproj = jnp.concatenate([dpi, dpg, du, dsg], axis=1)
    KH = _t(4 * P, 2048)
    dhn = _mm(dproj, win, dims=NT, grid=(T // tm, D // DH, 4 * P // KH),
              a_spec=_bs((tm, KH), row_k), b_spec=_bs((DH, KH), lambda i, n, k: (n, k)),
              o_spec=_bs((tm, DH), row_n), out_shape=f32(T, D), name="mm_dhn")
    g_win = _mm(hn, dproj, dims=TN, grid=(D // DH, N_CHIP, T // tk),
                a_spec=_bs((tk, DH), col_m), b_spec=_bs((tk, P), col_n),
                o_spec=_bs((None, DH, P), lambda m, j, k: (j, m, 0)), out_shape=f32(N_CHIP, D, P), name="mm_gwin")
    grad_x, dg1 = _norm1_bwd(xs, dhn, dh1, norm_gain, tb)

    dbbt_re = _block_diag_extract(dbdr, GT, C, N).reshape(G * C, N)
    dbbt_im = _block_diag_extract(dbdi, GT, C, N).reshape(G * C, N)
    g_c_re = _block_diag_extract(dcdr, GT, N, C).transpose(0, 2, 1)
    g_c_im = _block_diag_extract(dcdi, GT, N, C).transpose(0, 2, 1)
    dab_re_r = rep(dabr.reshape(G, N)) * (1.0 / C)
    dab_im_r = rep(dabi.reshape(G, N)) * (1.0 / C)
    g_a_re, g_a_im, g_ldt, g_bt_re, g_bt_im = _ssm_prep_bwd(a_re_r, a_im_r, ldt_r, bt_re, bt_im,
                                                            dab_re_r, dab_im_r, dbbt_re, dbbt_im, G)
    g_b_re = g_bt_re.reshape(G, C, N).transpose(0, 2, 1)
    g_b_im = g_bt_im.reshape(G, C, N).transpose(0, 2, 1)

    gbig = {"w_in": g_win,
            "w_pool": g_wp.reshape(NG, N_CHIP, PG // N_CHIP, PG).transpose(1, 0, 2, 3).reshape(N_CHIP, NG * PG // N_CHIP, PG),
            "w_glu": g_wglu, "w_out": g_wout.reshape(N_CHIP, Q, D), "w_ple": g_wple,
            "w_ple_gate": g_wpg.reshape(N_CHIP, Q, D)}
    place = jnp.stack([2 * lax.axis_index("x") + lax.axis_index("y"), lax.axis_index("c")]).astype(jnp.int32)
    got = _halves_to_sibling([gbig[n] for n in big_names])
    chip_sums = [_sum_cast(gbig[n], g_, place, "sum_cast_" + n) for g_, n in zip(got, big_names)]
    arrived = _scatter_to_chips(chip_sums)
    halves = [_sum_chips(s_, a, place, "sum_chips_" + n) for s_, a, n in zip(chip_sums, arrived, big_names)]
    gshard = _join_halves(halves)

    small_names = ["norm_gain", "pool_scale", "a_re", "a_im", "log_dt", "b_re", "b_im", "c_re", "c_im",
                   "d_skip", "final_gain"]
    small_w = dict(norm_gain=norm_gain, pool_scale=pool_scale, a_re=a_re, a_im=a_im, log_dt=log_dt, b_re=b_re,
                   b_im=b_im, c_re=c_re, c_im=c_im, d_skip=d_skip, final_gain=final_gain)
    small_m = dict(norm_gain=m_norm_gain, pool_scale=m_pool_scale, a_re=m_a_re, a_im=m_a_im, log_dt=m_log_dt,
                   b_re=m_b_re, b_im=m_b_im, c_re=m_c_re, c_im=m_c_im, d_skip=m_d_skip, final_gain=m_final_gain)
    small_v = dict(norm_gain=v_norm_gain, pool_scale=v_pool_scale, a_re=v_a_re, a_im=v_a_im, log_dt=v_log_dt,
                   b_re=v_b_re, b_im=v_b_im, c_re=v_c_re, c_im=v_c_im, d_skip=v_d_skip, final_gain=v_final_gain)
    small_g = dict(norm_gain=dg1, pool_scale=dps, a_re=g_a_re, a_im=g_a_im, log_dt=g_ldt, b_re=g_b_re,
                   b_im=g_b_im, c_re=g_c_re, c_im=g_c_im, d_skip=dd, final_gain=dg2)
    shapes = [small_w[n].shape for n in small_names]
    total = sum(small_w[n].size for n in small_names) + 1
    unit = N_DEV * SUBLANES
    rows = -(-(-(-total // LANES)) // unit) * unit
    gbuf = _pack_small([small_g[n] for n in small_names] + [lpart[0, :1]], rows)
    gsum = _allreduce_small(gbuf)
    wbuf = _pack_small([small_w[n] for n in small_names], rows)
    mbuf = _pack_small([small_m[n] for n in small_names], rows)
    vbuf = _pack_small([small_v[n] for n in small_names], rows)
    dsm, msm, vsm = _adamw(wbuf, gsum, mbuf, vbuf, "adamw_small")
    g_small = dict(zip(small_names, _unpack_small(gsum, shapes)))
    d_small = dict(zip(small_names, _unpack_small(dsm, shapes)))
    m_small = dict(zip(small_names, _unpack_small(msm, shapes)))
    v_small = dict(zip(small_names, _unpack_small(vsm, shapes)))
    loss = gsum.reshape(-1)[total - 1]

    g_out, d_out, m_out, v_out = dict(g_small), dict(d_small), dict(m_small), dict(v_small)
    for n, gs in zip(big_names, gshard):
        w_, m_, v_ = big[n]
        r2 = shard2d[n]
        d_, mn_, vn_ = _adamw(w_.reshape(r2), gs, m_.reshape(r2), v_.reshape(r2), "adamw_" + n)
        g_out[n], d_out[n], m_out[n], v_out[n] = (a.reshape(w_.shape) for a in (gs, d_, mn_, vn_))

    order = ["norm_gain", "w_in", "w_pool", "pool_scale", "a_re", "a_im", "log_dt", "b_re", "b_im", "c_re",
             "c_im", "d_skip", "w_glu", "w_out", "w_ple", "w_ple_gate", "final_gain"]
    return (loss, grad_x[None], *[g_out[n] for n in order], *[d_out[n] for n in order],
            *[m_out[n] for n in order], *[v_out[n] for n in order])
```

```python
import functools

import jax
import jax.numpy as jnp
from jax import lax
from jax.experimental import pallas as pl
from jax.experimental.pallas import tpu as pltpu

F32, BF16 = jnp.float32, jnp.bfloat16
MESH = pl.DeviceIdType.MESH
ANY = pl.BlockSpec(memory_space=pl.ANY)
VMEM_FULL = pl.BlockSpec(memory_space=pltpu.VMEM)

EPS = 1e-6
A_RE_MAX = -1e-4
SSM_GROUP = 16
SSM_STATE = 64
POOL_WINDOWS = (2, 4, 8, 16)
POOL_HALO = 16
ADAM_LR, ADAM_B1, ADAM_B2, ADAM_EPS, ADAM_WD, ADAM_STEP = 0.001, 0.9, 0.999, 1e-08, 0.01, 10

V7X_VMEM_BYTES = 64 * 1024 * 1024
VMEM_LIMIT = V7X_VMEM_BYTES - 8 * 1024 * 1024
SUBLANES, LANES = 8, 128
SSM_TILE_GROUPS = 8
SCAN_LANES = 512
N_DEV, N_CHIP = 8, 4
DMA_CHUNK_BYTES = 256 * 1024
DMA_MAX_CHUNKS = 32
AG_CHUNKS = 8
RS_CHUNKS = 8


def _t(n, pref):
    return pref if n % pref == 0 else n


def _cp(sem=None, vmem=VMEM_LIMIT):
    return pltpu.CompilerParams(dimension_semantics=sem, vmem_limit_bytes=vmem)


def _call(body, **kw):
    return pl.pallas_call(body, **kw)


NN = ((1,), (0,))
NT = ((1,), (1,))
TN = ((0,), (0,))


def _mm(a, b, *, dims, grid, a_spec, b_spec, o_spec, out_shape, name, res=None, r_spec=None, bf16_copy=False):
    nk, kax = grid[-1], len(grid) - 1
    acc_shape = tuple(d for d in o_spec.block_shape if d is not None)

    def body(*refs):
        refs = list(refs)
        a_ref, b_ref = refs[:2]
        r_ref = refs[2] if res is not None else None
        outs = refs[3 if res is not None else 2:]
        o_ref = outs[0]
        o2_ref = outs[1] if bf16_copy else None
        acc = outs[-1] if nk > 1 else None

        def finish(r):
            if r_ref is not None:
                r = r + r_ref[...]
            o_ref[...] = r.astype(o_ref.dtype)
            if o2_ref is not None:
                o2_ref[...] = r.astype(BF16)

        part = lax.dot_general(a_ref[...].astype(BF16), b_ref[...].astype(BF16),
                               (dims, ((), ())), preferred_element_type=F32)
        if nk == 1:
            finish(part)
        else:
            k = pl.program_id(kax)

            @pl.when(k == 0)
            def _():
                acc[...] = part

            @pl.when(k > 0)
            def _():
                acc[...] += part

            @pl.when(k == nk - 1)
            def _():
                finish(acc[...])

    ins, specs = [a, b], [a_spec, b_spec]
    if res is not None:
        ins.append(res)
        specs.append(r_spec)
    o_specs, o_shapes = o_spec, out_shape
    if bf16_copy:
        o_specs = [o_spec, o_spec]
        o_shapes = [out_shape, jax.ShapeDtypeStruct(out_shape.shape, BF16)]
    sem = ("parallel",) * kax + ("arbitrary",)
    return _call(body, grid=grid, in_specs=specs, out_specs=o_specs, out_shape=o_shapes,
                 scratch_shapes=[pltpu.VMEM(acc_shape, F32)] if nk > 1 else [],
                 compiler_params=_cp(sem), name=name)(*ins)


def _bs(shape, fn):
    return pl.BlockSpec(shape, fn)


def _sigmoid(v):
    return 1.0 / (1.0 + jnp.exp(-v))


def _gelu(v):
    return 0.5 * v * (1.0 + jnp.tanh(0.7978845608028654 * (v + 0.044715 * v * v * v)))


def _gelu_grad(v):
    t = jnp.tanh(0.7978845608028654 * (v + 0.044715 * v * v * v))
    return 0.5 * (1.0 + t) + 0.5 * v * (1.0 - t * t) * 0.7978845608028654 * (1.0 + 3 * 0.044715 * v * v)


def _norm1(x, g1, tb):
    T, D = x.shape

    def body(x_ref, g_ref, o_ref):
        xv = x_ref[...]
        r = lax.rsqrt(jnp.mean(xv * xv, axis=-1, keepdims=True) + EPS)
        o_ref[...] = ((xv * r) * g_ref[...]).astype(BF16)

    return _call(body, grid=(T // tb,),
                 in_specs=[_bs((tb, D), lambda i: (i, 0)), _bs((1, D), lambda i: (0, 0))],
                 out_specs=_bs((tb, D), lambda i: (i, 0)), out_shape=jax.ShapeDtypeStruct((T, D), BF16),
                 compiler_params=_cp(("parallel",)), name="norm1")(x, g1)


def _norm1_bwd(x, dhn, dh1, g1, tb):
    T, D = x.shape

    def body(x_ref, dhn_ref, dh1_ref, g_ref, dx_ref, dg_ref):
        @pl.when(pl.program_id(0) == 0)
        def _():
            dg_ref[...] = jnp.zeros_like(dg_ref)

        xv = x_ref[...]
        r = lax.rsqrt(jnp.mean(xv * xv, axis=-1, keepdims=True) + EPS)
        xh = xv * r
        dhn_v = dhn_ref[...]
        dg_ref[...] += jnp.sum(dhn_v * xh, axis=0, keepdims=True)
        dxh = dhn_v * g_ref[...]
        dx_ref[...] = dh1_ref[...] + r * (dxh - xh * jnp.mean(dxh * xh, axis=-1, keepdims=True))

    row = _bs((tb, D), lambda i: (i, 0))
    vec = _bs((1, D), lambda i: (0, 0))
    return _call(body, grid=(T // tb,), in_specs=[row, row, row, vec], out_specs=[row, vec],
                 out_shape=[jax.ShapeDtypeStruct((T, D), F32), jax.ShapeDtypeStruct((1, D), F32)],
                 compiler_params=_cp(("arbitrary",)), name="norm1_bwd")(x, dhn, dh1, g1)


def _gate_fwd(mixed, proj, hg, ps, tb):
    T, P = mixed.shape

    def body(mx_ref, pg_ref, sg_ref, hg_ref, ps_ref, o_ref):
        pg, sg = pg_ref[...], sg_ref[...]
        ya = (mx_ref[...] * ps_ref[...]) * (pg * _sigmoid(pg))
        hgv = hg_ref[...]
        o = hgv[:, :P] * _sigmoid(hgv[:, P:])
        yb = o * (sg * _sigmoid(sg))
        o_ref[:, :P] = ya.astype(BF16)
        o_ref[:, P:] = yb.astype(BF16)

    return _call(body, grid=(T // tb,),
                 in_specs=[_bs((tb, P), lambda i: (i, 0)), _bs((tb, P), lambda i: (i, 1)),
                           _bs((tb, P), lambda i: (i, 3)), _bs((tb, 2 * P), lambda i: (i, 0)),
                           _bs((1, P), lambda i: (0, 0))],
                 out_specs=_bs((tb, 2 * P), lambda i: (i, 0)),
                 out_shape=jax.ShapeDtypeStruct((T, 2 * P), BF16),
                 compiler_params=_cp(("parallel",)), name="gate_fwd")(mixed, proj, proj, hg, ps)


def _gate_bwd(dcat, mixed, proj, hg, ps, tb):
    T, P = mixed.shape

    def body(dc_ref, mx_ref, pg_ref, sg_ref, hg_ref, ps_ref, dmx_ref, dpg_ref, dsg_ref, dhg_ref, dps_ref):
        @pl.when(pl.program_id(0) == 0)
        def _():
            dps_ref[...] = jnp.zeros_like(dps_ref)

        dc = dc_ref[...]
        dya, dyb = dc[:, :P], dc[:, P:]
        pg, sg, mx, psv = pg_ref[...], sg_ref[...], mx_ref[...], ps_ref[...]
        s_pg = _sigmoid(pg)
        dpa = dya * (pg * s_pg)
        dpg_ref[...] = (dya * (mx * psv) * (s_pg * (1.0 + pg * (1.0 - s_pg)))).astype(BF16)
        dps_ref[...] += jnp.sum(dpa * mx, axis=0, keepdims=True)
        dmx_ref[...] = (dpa * psv).astype(BF16)
        hgv = hg_ref[...]
        h1, s_h2 = hgv[:, :P], _sigmoid(hgv[:, P:])
        s_sg = _sigmoid(sg)
        do = dyb * (sg * s_sg)
        dsg_ref[...] = (dyb * (h1 * s_h2) * (s_sg * (1.0 + sg * (1.0 - s_sg)))).astype(BF16)
        dhg_ref[:, :P] = (do * s_h2).astype(BF16)
        dhg_ref[:, P:] = (do * h1 * s_h2 * (1.0 - s_h2)).astype(BF16)

    rowp = _bs((tb, P), lambda i: (i, 0))
    row2 = _bs((tb, 2 * P), lambda i: (i, 0))
    vec = _bs((1, P), lambda i: (0, 0))
    return _call(body, grid=(T // tb,),
                 in_specs=[row2, rowp, _bs((tb, P), lambda i: (i, 1)), _bs((tb, P), lambda i: (i, 3)), row2, vec],
                 out_specs=[rowp, rowp, rowp, row2, vec],
                 out_shape=[jax.ShapeDtypeStruct((T, P), BF16), jax.ShapeDtypeStruct((T, P), BF16),
                            jax.ShapeDtypeStruct((T, P), BF16), jax.ShapeDtypeStruct((T, 2 * P), BF16),
                            jax.ShapeDtypeStruct((1, P), F32)],
                 compiler_params=_cp(("arbitrary",)), name="gate_bwd")(dcat, mixed, proj, proj, hg, ps)


def _final_fb(h1, e, z, tgt, g2, tb):
    T, D = h1.shape

    def body(h1_ref, e_ref, z_ref, t_ref, g_ref, dh2_ref, de_ref, dz_ref, dg_ref, l_ref):
        @pl.when(pl.program_id(0) == 0)
        def _():
            dg_ref[...] = jnp.zeros_like(dg_ref)
            l_ref[...] = jnp.zeros_like(l_ref)

        ev = e_ref[...]
        s = _sigmoid(z_ref[...])
        h2 = h1_ref[...] + ev * s
        r = lax.rsqrt(jnp.mean(h2 * h2, axis=-1, keepdims=True) + EPS)
        xh = h2 * r
        gv = g_ref[...]
        diff = xh * gv - t_ref[...]
        l_ref[...] += 0.5 * jnp.sum(jnp.mean(diff * diff, axis=-1, keepdims=True))
        dout = diff * (1.0 / D)
        dg_ref[...] += jnp.sum(dout * xh, axis=0, keepdims=True)
        dxh = dout * gv
        dh2 = r * (dxh - xh * jnp.mean(dxh * xh, axis=-1, keepdims=True))
        dh2_ref[...] = dh2
        de_ref[...] = (dh2 * s).astype(BF16)
        dz_ref[...] = (dh2 * ev * s * (1.0 - s)).astype(BF16)

    row = _bs((tb, D), lambda i: (i, 0))
    vec = _bs((1, D), lambda i: (0, 0))
    return _call(body, grid=(T // tb,), in_specs=[row, row, row, row, vec],
                 out_specs=[row, row, row, vec, _bs((1, LANES), lambda i: (0, 0))],
                 out_shape=[jax.ShapeDtypeStruct((T, D), F32), jax.ShapeDtypeStruct((T, D), BF16),
                            jax.ShapeDtypeStruct((T, D), BF16), jax.ShapeDtypeStruct((1, D), F32),
                            jax.ShapeDtypeStruct((1, LANES), F32)],
                 compiler_params=_cp(("arbitrary",)), name="final_fb")(h1, e, z, tgt, g2)


def _pool_inv_count(t0, rows, pg, ngroups):
    t = t0 + lax.broadcasted_iota(jnp.int32, (rows, pg), 0)
    parts = []
    for w in POOL_WINDOWS[:ngroups]:
        parts.append(jnp.where(t + 1 >= w, 1.0 / w, 1.0 / (t + 1).astype(F32)))
    return parts


def _pool_fwd(proj, P, tb):
    T = proj.shape[0]
    ng = len(POOL_WINDOWS)
    pg = P // ng
    hb = tb // POOL_HALO

    def body(v_ref, tail_ref, o_ref, ext):
        i = pl.program_id(0)
        ext[pl.ds(0, POOL_HALO), :] = jnp.where(i > 0, tail_ref[...], 0.0)
        ext[pl.ds(POOL_HALO, tb), :] = v_ref[...]
        inv = _pool_inv_count(i * tb, tb, pg, ng)
        for g, w in enumerate(POOL_WINDOWS):
            cols = pl.ds(g * pg, pg)
            win = ext[pl.ds(POOL_HALO, tb), cols]
            for k in range(1, w):
                win = win + ext[pl.ds(POOL_HALO - k, tb), cols]
            o_ref[:, cols] = (win * inv[g] - ext[pl.ds(POOL_HALO, tb), cols]).astype(BF16)

    return _call(body, grid=(T // tb,),
                 in_specs=[_bs((tb, P), lambda i: (i, 0)),
                           _bs((POOL_HALO, P), lambda i: (jnp.maximum(i * hb - 1, 0), 0))],
                 out_specs=_bs((tb, P), lambda i: (i, 0)), out_shape=jax.ShapeDtypeStruct((T, P), BF16),
                 scratch_shapes=[pltpu.VMEM((tb + POOL_HALO, P), F32)],
                 compiler_params=_cp(("arbitrary",)), name="pool_fwd")(proj, proj)


def _pool_bwd(dpooled, tb):
    T, P = dpooled.shape
    ng = len(POOL_WINDOWS)
    pg = P // ng
    hb = tb // POOL_HALO
    nb = T // tb

    def body(d_ref, head_ref, o_ref, ext):
        i = pl.program_id(0)
        inv = _pool_inv_count(i * tb, tb, pg, ng)
        invh = _pool_inv_count((i + 1) * tb, POOL_HALO, pg, ng)
        for g in range(ng):
            cols = pl.ds(g * pg, pg)
            ext[pl.ds(0, tb), cols] = d_ref[:, cols] * inv[g]
            ext[pl.ds(tb, POOL_HALO), cols] = jnp.where(i < nb - 1, head_ref[:, cols] * invh[g], 0.0)
        for g, w in enumerate(POOL_WINDOWS):
            cols = pl.ds(g * pg, pg)
            acc = ext[pl.ds(0, tb), cols]
            for k in range(1, w):
                acc = acc + ext[pl.ds(k, tb), cols]
            o_ref[:, cols] = (acc - d_ref[:, cols]).astype(BF16)

    return _call(body, grid=(nb,),
                 in_specs=[_bs((tb, P), lambda i: (i, 0)),
                           _bs((POOL_HALO, P), lambda i: (jnp.minimum((i + 1) * hb, T // POOL_HALO - 1), 0))],
                 out_specs=_bs((tb, P), lambda i: (i, 0)), out_shape=jax.ShapeDtypeStruct((T, P), BF16),
                 scratch_shapes=[pltpu.VMEM((tb + POOL_HALO, P), F32)],
                 compiler_params=_cp(("arbitrary",)), name="pool_bwd")(dpooled, dpooled)


def _zoh(a_re, a_im, ldt, b_re, b_im):
    lam_re = jnp.minimum(a_re, A_RE_MAX)
    lam_im = a_im
    dt = jnp.exp(ldt)
    mag = jnp.exp(lam_re * dt)
    ang = lam_im * dt
    ab_re = mag * jnp.cos(ang)
    ab_im = mag * jnp.sin(ang)
    den = lam_re * lam_re + lam_im * lam_im
    n_re = ab_re - 1.0
    n_im = ab_im
    q_re = (n_re * lam_re + n_im * lam_im) / den
    q_im = (n_im * lam_re - n_re * lam_im) / den
    return ab_re, ab_im, q_re * b_re - q_im * b_im, q_re * b_im + q_im * b_re


def _ssm_prep(a_re, a_im, ldt, bt_re, bt_im):
    shp = jax.ShapeDtypeStruct(a_re.shape, F32)

    def body(a, b, c, d, e, o0, o1, o2, o3):
        r = _zoh(a[...], b[...], c[...], d[...], e[...])
        o0[...], o1[...], o2[...], o3[...] = r

    return _call(body, in_specs=[VMEM_FULL] * 5, out_specs=[VMEM_FULL] * 4, out_shape=[shp] * 4,
                 name="ssm_prep")(a_re, a_im, ldt, bt_re, bt_im)


def _ssm_prep_bwd(a_re, a_im, ldt, bt_re, bt_im, dab_re, dab_im, dbb_re, dbb_im, G):
    GC, N = a_re.shape
    C = GC // G

    def body(a, b, c, d, e, g0, g1, g2, g3, da_re, da_im, dldt, db_re, db_im):
        _, vjp = jax.vjp(_zoh, a[...], b[...], c[...], d[...], e[...])
        ga_re, ga_im, gl, gb_re, gb_im = vjp((g0[...], g1[...], g2[...], g3[...]))
        da_re[...] = jnp.sum(ga_re.reshape(G, C, N), axis=1)
        da_im[...] = jnp.sum(ga_im.reshape(G, C, N), axis=1)
        dldt[...] = jnp.sum(jnp.sum(gl.reshape(G, C, N), axis=1), axis=1, keepdims=True)
        db_re[...] = gb_re
        db_im[...] = gb_im

    gn = jax.ShapeDtypeStruct((G, N), F32)
    full = jax.ShapeDtypeStruct((GC, N), F32)
    return _call(body, in_specs=[VMEM_FULL] * 9, out_specs=[VMEM_FULL] * 5,
                 out_shape=[gn, gn, jax.ShapeDtypeStruct((G, 1), F32), full, full],
                 name="ssm_prep_bwd")(a_re, a_im, ldt, bt_re, bt_im, dab_re, dab_im, dbb_re, dbb_im)


def _coef_tiles(abr, abi, reverse):
    ns = abr.shape[1]
    row = lax.broadcasted_iota(jnp.int32, (SUBLANES, ns), 0)
    ar = jnp.broadcast_to(abr, (SUBLANES, ns))
    ai = jnp.broadcast_to(-abi if reverse else abi, (SUBLANES, ns))
    a2r, a2i = ar * ar - ai * ai, 2.0 * ar * ai
    a4r, a4i = a2r * a2r - a2i * a2i, 2.0 * a2r * a2i
    out = []
    for d, (vr, vi) in ((1, (ar, ai)), (2, (a2r, a2i)), (4, (a4r, a4i))):
        keep = (row < SUBLANES - d) if reverse else (row >= d)
        out += [jnp.where(keep, vr, 0.0), jnp.where(keep, vi, 0.0)]
    pr, pi = ar, ai
    for k in range(1, SUBLANES):
        sel = (row <= SUBLANES - 1 - k) if reverse else (row >= k)
        nr, ni = pr * ar - pi * ai, pr * ai + pi * ar
        pr, pi = jnp.where(sel, nr, pr), jnp.where(sel, ni, pi)
    return out + [pr, pi]


def _cpow(ar, ai, n):
    out, br, bi = None, ar, ai
    while n:
        if n & 1:
            out = (br, bi) if out is None else (out[0] * br - out[1] * bi, out[0] * bi + out[1] * br)
        br, bi = br * br - bi * bi, 2.0 * br * bi
        n >>= 1
    return out


def _seg_order_rows(dst_ref, src, stage, nrows):
    seg = nrows // SUBLANES
    nl = stage.shape[0]
    for j in range(nl):
        stage[j] = src[:, j * LANES:(j + 1) * LANES]

    def step(i, _):
        rows = pl.ds(pl.multiple_of(i * SUBLANES, SUBLANES), SUBLANES)
        for j in range(nl):
            dst_ref[rows, j * LANES:(j + 1) * LANES] = stage[j, pl.ds(i, SUBLANES, stride=seg), :]
        return 0

    lax.fori_loop(0, seg, step, 0)


def _time_order_rows(src_ref, stage, nrows):
    seg = nrows // SUBLANES
    nl = stage.shape[0]

    def step(i, _):
        rows = pl.ds(pl.multiple_of(i * SUBLANES, SUBLANES), SUBLANES)
        for j in range(nl):
            stage[j, pl.ds(i, SUBLANES, stride=seg), :] = src_ref[rows, j * LANES:(j + 1) * LANES]
        return 0

    lax.fori_loop(0, seg, step, 0)
    return jnp.concatenate([stage[j] for j in range(nl)], axis=1)


def _seg_scan(xr_ref, xi_ref, abr_ref, abi_ref, coef_ref, car_ref, cai_ref, *, nrows, ns, reverse,
              cmat=None, dab=None):
    seg = nrows // SUBLANES
    cw = min(SCAN_LANES, ns)
    row = lax.broadcasted_iota(jnp.int32, (SUBLANES, cw), 0)
    first, last = (SUBLANES - 1, 0) if reverse else (0, SUBLANES - 1)

    def tile(i):
        return pl.ds(pl.multiple_of(((seg - 1 - i) if reverse else i) * SUBLANES, SUBLANES), SUBLANES)

    for cc in range(ns // cw):
        cols = pl.ds(cc * cw, cw)
        ar = jnp.broadcast_to(abr_ref[:, cols], (SUBLANES, cw))
        ai = jnp.broadcast_to(abi_ref[:, cols], (SUBLANES, cw))
        if reverse:
            ai = -ai

        def local(i, x, cols=cols, ar=ar, ai=ai):
            rows = tile(i)
            nr = ar * x[0] - ai * x[1] + xr_ref[rows, cols]
            ni = ar * x[1] + ai * x[0] + xi_ref[rows, cols]
            xr_ref[rows, cols] = nr
            xi_ref[rows, cols] = ni
            return nr, ni

        zero = jnp.zeros((SUBLANES, cw), F32)
        er, ei = lax.fori_loop(0, seg, local, (zero, zero))

        co = [coef_ref[k, :, cols] for k in range(8)]
        for lvl, d in enumerate((1, 2, 4)):
            kr, ki = co[2 * lvl], co[2 * lvl + 1]
            sh = SUBLANES - d if reverse else d
            sr, si = pltpu.roll(er, sh, 0), pltpu.roll(ei, sh, 0)
            er, ei = er + (kr * sr - ki * si), ei + (kr * si + ki * sr)
        c0r, c0i = car_ref[:, cols], cai_ref[:, cols]
        er, ei = er + (co[6] * c0r - co[7] * c0i), ei + (co[6] * c0i + co[7] * c0r)
        nb_shift = SUBLANES - 1 if reverse else 1
        cmr = jnp.where(row == first, c0r, pltpu.roll(er, nb_shift, 0))
        cmi = jnp.where(row == first, c0i, pltpu.roll(ei, nb_shift, 0))
        car_ref[:, cols] = jnp.broadcast_to(er[last:last + 1, :], er.shape)
        cai_ref[:, cols] = jnp.broadcast_to(ei[last:last + 1, :], ei.shape)
        if cmat is not None:
            cmat[0][:, cols] = cmr
            cmat[1][:, cols] = cmi

        w0 = (ar * cmr - ai * cmi, ar * cmi + ai * cmr)
        if dab is None:
            def fix(i, w, cols=cols, ar=ar, ai=ai):
                rows = tile(i)
                xr_ref[rows, cols] = xr_ref[rows, cols] + w[0]
                xi_ref[rows, cols] = xi_ref[rows, cols] + w[1]
                return ar * w[0] - ai * w[1], ar * w[1] + ai * w[0]

            lax.fori_loop(0, seg, fix, w0)
        else:
            s_re, s_im, e_re, e_im, o_re, o_im = dab

            def add(rows, w, pr, pi, acc):
                gr = xr_ref[rows, cols] + w[0]
                gi = xi_ref[rows, cols] + w[1]
                xr_ref[rows, cols] = gr
                xi_ref[rows, cols] = gi
                return acc[0] + (gr * pr + gi * pi), acc[1] + (gi * pr - gr * pi)

            def fix(i, st, cols=cols, ar=ar, ai=ai):
                w, acc = st[:2], st[2:]
                rows = tile(i)
                before = pl.ds(pl.multiple_of((seg - 2 - i) * SUBLANES, SUBLANES), SUBLANES)
                acc = add(rows, w, s_re[before, cols], s_im[before, cols], acc)
                return (ar * w[0] - ai * w[1], ar * w[1] + ai * w[0]) + acc

            st = lax.fori_loop(0, seg - 1, fix, w0 + (zero, zero))
            acc = add(pl.ds(0, SUBLANES), st[:2], e_re[:, cols], e_im[:, cols], st[2:])
            o_re[:, cols] += jnp.sum(acc[0], axis=0, keepdims=True)
            o_im[:, cols] += jnp.sum(acc[1], axis=0, keepdims=True)


def _hosted(core, comm, nsteps, n_in, n_out, n_scratch):
    ci = len(comm["ins"]) if comm else 0
    co = len(comm["out_shape"]) if comm else 0

    def body(*refs):
        ins, rest = refs[:n_in + ci], refs[n_in + ci:]
        outs, scr = rest[:n_out + co], rest[n_out + co:]
        hooks = functools.partial(_comm_hooks, comm, nsteps, ins[n_in:], outs[n_out:], scr[n_scratch:])
        hooks(before=True)
        core(*ins[:n_in], *outs[:n_out], *scr[:n_scratch])
        hooks(before=False)

    extra = dict(ins=list(comm["ins"]) if comm else [], in_specs=[ANY] * ci, out_specs=[ANY] * co,
                 out_shape=list(comm["out_shape"]) if comm else [], scratch=list(comm["scratch"]) if comm else [])
    return body, extra


def _ssm_fwd(proj, bdr, bdi, cdr, cdi, abr, abi, dsk, P, tb, comm=None):
    T = proj.shape[0]
    ntl, ct, st = bdr.shape
    ns = ntl * st
    nb = T // tb

    def core(u_ref, bdr_ref, bdi_ref, cdr_ref, cdi_ref, abr_ref, abi_ref, d_ref,
             y_ref, ge_ref, bsr_ref, bsi_ref, sr, si, coef, car, cai, up, stage):
        @pl.when(pl.program_id(0) == 0)
        def _():
            seg_pow = _cpow(abr_ref[...], abi_ref[...], tb // SUBLANES)
            for k, tile in enumerate(_coef_tiles(seg_pow[0], seg_pow[1], False)):
                coef[k] = tile
            car[...] = jnp.zeros_like(car)
            cai[...] = jnp.zeros_like(cai)

        bsr_ref[...] = car[...]
        bsi_ref[...] = cai[...]
        _seg_order_rows(up, u_ref[...], stage, tb)
        u = up[...]
        ub = u.astype(BF16)
        for s in range(ntl):
            us = ub[:, s * ct:(s + 1) * ct]
            sr[:, s * st:(s + 1) * st] = jnp.dot(us, bdr_ref[s], preferred_element_type=F32)
            si[:, s * st:(s + 1) * st] = jnp.dot(us, bdi_ref[s], preferred_element_type=F32)
        _seg_scan(sr, si, abr_ref, abi_ref, coef, car, cai, nrows=tb, ns=ns, reverse=False)
        for s in range(ntl):
            s_re = sr[:, s * st:(s + 1) * st].astype(BF16)
            s_im = si[:, s * st:(s + 1) * st].astype(BF16)
            up[:, s * ct:(s + 1) * ct] = (jnp.dot(s_re, cdr_ref[s], preferred_element_type=F32)
                                          - jnp.dot(s_im, cdi_ref[s], preferred_element_type=F32)
                                          + d_ref[:, s * ct:(s + 1) * ct] * u[:, s * ct:(s + 1) * ct])
        y = _time_order_rows(up, stage, tb)
        y_ref[...] = y
        ge_ref[...] = _gelu(y).astype(BF16)

    full3 = lambda a: _bs(a.shape, lambda i: (0, 0, 0))
    vec = lambda n: _bs((1, n), lambda i: (0, 0))
    row = _bs((tb, P), lambda i: (i, 0))
    st_spec = _bs((None, SUBLANES, ns), lambda i: (i, 0, 0))
    body, extra = _hosted(core, comm, nb, 8, 4, 7)
    return _call(body, grid=(nb,),
                 in_specs=[_bs((tb, P), lambda i: (i, 2)), full3(bdr), full3(bdi), full3(cdr), full3(cdi),
                           vec(ns), vec(ns), vec(P)] + extra["in_specs"],
                 out_specs=[row, row, st_spec, st_spec] + extra["out_specs"],
                 out_shape=[jax.ShapeDtypeStruct((T, P), F32), jax.ShapeDtypeStruct((T, P), BF16),
                            jax.ShapeDtypeStruct((nb, SUBLANES, ns), F32),
                            jax.ShapeDtypeStruct((nb, SUBLANES, ns), F32)] + extra["out_shape"],
                 scratch_shapes=[pltpu.VMEM((tb, ns), F32), pltpu.VMEM((tb, ns), F32),
                                 pltpu.VMEM((8, SUBLANES, ns), F32),
                                 pltpu.VMEM((SUBLANES, ns), F32), pltpu.VMEM((SUBLANES, ns), F32),
                                 pltpu.VMEM((tb, P), F32), pltpu.VMEM((P // LANES, tb, LANES), F32)]
                 + extra["scratch"],
                 compiler_params=_cp(("arbitrary",)), name="ssm_fwd")(
                     proj, bdr, bdi, cdr, cdi, abr, abi, dsk, *extra["ins"])


def _ssm_bwd(proj, y, dge, bsr, bsi, bdr, bdi, cdr, cdi, abr, abi, dsk, P, tb, comm=None):
    T = proj.shape[0]
    ntl, ct, st = bdr.shape
    ns = ntl * st
    nb = T // tb

    def core(u_ref, y_ref, dge_ref, bsr_ref, bsi_ref, abr_ref, abi_ref, d_ref, bdr_h, bdi_h, cdr_h, cdi_h,
             du_ref, dabr_ref, dabi_ref, dd_ref, dbdr_h, dbdi_h, dcdr_h, dcdi_h,
             wbdr, wbdi, wcdr, wcdi, abdr, abdi, acdr, acdi, spr, spi, gr, gi, coef_f, coef_r,
             car, cai, gcr, gci, ser, sei, up, dyp, dup, stage):
        i = pl.program_id(0)

        @pl.when(i == 0)
        def _():
            for h, w in ((bdr_h, wbdr), (bdi_h, wbdi), (cdr_h, wcdr), (cdi_h, wcdi)):
                pltpu.sync_copy(h, w)
            for a in (abdr, abdi, acdr, acdi, gcr, gci):
                a[...] = jnp.zeros_like(a)
            for o in (dabr_ref, dabi_ref, dd_ref):
                o[...] = jnp.zeros_like(o)
            seg_pow = _cpow(abr_ref[...], abi_ref[...], tb // SUBLANES)
            for k, tile in enumerate(_coef_tiles(seg_pow[0], seg_pow[1], False)):
                coef_f[k] = tile
            for k, tile in enumerate(_coef_tiles(seg_pow[0], seg_pow[1], True)):
                coef_r[k] = tile

        car[...] = bsr_ref[...]
        cai[...] = bsi_ref[...]
        _seg_order_rows(up, u_ref[...], stage, tb)
        _seg_order_rows(dyp, dge_ref[...] * _gelu_grad(y_ref[...]), stage, tb)
        u = up[...]
        ub = u.astype(BF16)
        for s in range(ntl):
            us = ub[:, s * ct:(s + 1) * ct]
            spr[:, s * st:(s + 1) * st] = jnp.dot(us, wbdr[s], preferred_element_type=F32)
            spi[:, s * st:(s + 1) * st] = jnp.dot(us, wbdi[s], preferred_element_type=F32)
        _seg_scan(spr, spi, abr_ref, abi_ref, coef_f, car, cai, nrows=tb, ns=ns, reverse=False, cmat=(ser, sei))

        dy = dyp[...]
        dyb = dy.astype(BF16)
        for s in range(ntl):
            dys = dyb[:, s * ct:(s + 1) * ct]
            gr[:, s * st:(s + 1) * st] = lax.dot_general(dys, wcdr[s], (NT, ((), ())), preferred_element_type=F32)
            gi[:, s * st:(s + 1) * st] = -lax.dot_general(dys, wcdi[s], (NT, ((), ())), preferred_element_type=F32)
        _seg_scan(gr, gi, abr_ref, abi_ref, coef_r, gcr, gci, nrows=tb, ns=ns, reverse=True,
                  dab=(spr, spi, ser, sei, dabr_ref, dabi_ref))

        for s in range(ntl):
            sl_c, sl_s = slice(s * ct, (s + 1) * ct), slice(s * st, (s + 1) * st)
            s_re = spr[:, sl_s].astype(BF16)
            s_im = spi[:, sl_s].astype(BF16)
            g_re, g_im = gr[:, sl_s].astype(BF16), gi[:, sl_s].astype(BF16)
            dys, us = dyb[:, sl_c], ub[:, sl_c]
            acdr[s] += lax.dot_general(s_re, dys, (TN, ((), ())), preferred_element_type=F32)
            acdi[s] -= lax.dot_general(s_im, dys, (TN, ((), ())), preferred_element_type=F32)
            abdr[s] += lax.dot_general(us, g_re, (TN, ((), ())), preferred_element_type=F32)
            abdi[s] += lax.dot_general(us, g_im, (TN, ((), ())), preferred_element_type=F32)
            dup[:, sl_c] = (lax.dot_general(g_re, wbdr[s], (NT, ((), ())), preferred_element_type=F32)
                            + lax.dot_general(g_im, wbdi[s], (NT, ((), ())), preferred_element_type=F32)
                            + d_ref[:, sl_c] * dy[:, sl_c])
        dd_ref[...] += jnp.sum(dy * u, axis=0, keepdims=True)
        du_ref[...] = _time_order_rows(dup, stage, tb).astype(BF16)

        @pl.when(i == nb - 1)
        def _():
            for a, h in ((abdr, dbdr_h), (abdi, dbdi_h), (acdr, dcdr_h), (acdi, dcdi_h)):
                pltpu.sync_copy(a, h)

    rev = lambda i: nb - 1 - i
    vec = lambda n: _bs((1, n), lambda i: (0, 0))
    row = _bs((tb, P), lambda i: (rev(i), 0))
    st_spec = _bs((None, SUBLANES, ns), lambda i: (rev(i), 0, 0))
    bshape = jax.ShapeDtypeStruct(bdr.shape, F32)
    cshape = jax.ShapeDtypeStruct(cdr.shape, F32)
    body, extra = _hosted(core, comm, nb, 12, 8, 24)
    return _call(body, grid=(nb,),
                 in_specs=[_bs((tb, P), lambda i: (rev(i), 2)), row, row, st_spec, st_spec,
                           vec(ns), vec(ns), vec(P), ANY, ANY, ANY, ANY] + extra["in_specs"],
                 out_specs=[row, vec(ns), vec(ns), vec(P), ANY, ANY, ANY, ANY] + extra["out_specs"],
                 out_shape=[jax.ShapeDtypeStruct((T, P), BF16), jax.ShapeDtypeStruct((1, ns), F32),
                            jax.ShapeDtypeStruct((1, ns), F32), jax.ShapeDtypeStruct((1, P), F32),
                            bshape, bshape, cshape, cshape] + extra["out_shape"],
                 scratch_shapes=[pltpu.VMEM(bdr.shape, BF16), pltpu.VMEM(bdr.shape, BF16),
                                 pltpu.VMEM(cdr.shape, BF16), pltpu.VMEM(cdr.shape, BF16),
                                 pltpu.VMEM(bdr.shape, F32), pltpu.VMEM(bdr.shape, F32),
                                 pltpu.VMEM(cdr.shape, F32), pltpu.VMEM(cdr.shape, F32),
                                 pltpu.VMEM((tb, ns), F32), pltpu.VMEM((tb, ns), F32),
                                 pltpu.VMEM((tb, ns), F32), pltpu.VMEM((tb, ns), F32),
                                 pltpu.VMEM((8, SUBLANES, ns), F32), pltpu.VMEM((8, SUBLANES, ns), F32)]
                 + [pltpu.VMEM((SUBLANES, ns), F32)] * 6 + [pltpu.VMEM((tb, P), F32)] * 3
                 + [pltpu.VMEM((P // LANES, tb, LANES), F32)] + extra["scratch"],
                 compiler_params=_cp(("arbitrary",)), name="ssm_bwd")(
                     proj, y, dge, bsr, bsi, abr, abi, dsk, bdr, bdi, cdr, cdi, *extra["ins"])


def _adamw(w, g, m, v, name):
    R, C = w.shape
    tr = _t(R, 256)

    def body(w_ref, g_ref, m_ref, v_ref, d_ref, mo_ref, vo_ref):
        gv = g_ref[...]
        mn = ADAM_B1 * m_ref[...] + (1.0 - ADAM_B1) * gv
        vn = ADAM_B2 * v_ref[...] + (1.0 - ADAM_B2) * (gv * gv)
        m_hat = mn / (1.0 - ADAM_B1 ** ADAM_STEP)
        v_hat = vn / (1.0 - ADAM_B2 ** ADAM_STEP)
        d_ref[...] = -ADAM_LR * (m_hat / (jnp.sqrt(v_hat) + ADAM_EPS) + ADAM_WD * w_ref[...])
        mo_ref[...] = mn
        vo_ref[...] = vn

    blk = _bs((tr, C), lambda i: (i, 0))
    shp = jax.ShapeDtypeStruct((R, C), F32)
    return _call(body, grid=(R // tr,), in_specs=[blk] * 4, out_specs=[blk] * 3, out_shape=[shp] * 3,
                 compiler_params=_cp(("parallel",)), name=name)(w, g, m, v)


def _sum_cast(grad, got, place, name):
    J, H, C = got.shape
    tr = _t(H, 256)
    nb = H // tr

    def body(pl_ref, a_ref, b_ref, o_ref):
        o_ref[...] = (a_ref[...] + b_ref[...]).astype(BF16)

    blk = _bs((None, tr, C), lambda j, i, pc: (j, i, 0))
    mine = _bs((None, tr, C), lambda j, i, pc: (j, pc[1] * nb + i, 0))
    spec = pltpu.PrefetchScalarGridSpec(num_scalar_prefetch=1, grid=(J, nb), in_specs=[mine, blk], out_specs=blk)
    return _call(body, grid_spec=spec, out_shape=jax.ShapeDtypeStruct((J, H, C), BF16),
                 compiler_params=_cp(("parallel", "parallel")), name=name)(place, grad, got)


def _sum_chips(sent, arrived, place, name):
    J, H, C = arrived.shape
    tr = _t(H, 256)
    nb = H // tr

    def body(pl_ref, own_ref, a0_ref, a1_ref, a2_ref, o_ref):
        acc = own_ref[...].astype(F32)
        for r in (a0_ref, a1_ref, a2_ref):
            acc = acc + r[...].astype(F32)
        o_ref[...] = acc

    def other(k):
        return _bs((None, tr, C), lambda i, pc: (jnp.where(pc[0] <= k, k + 1, k), i, 0))

    spec = pltpu.PrefetchScalarGridSpec(
        num_scalar_prefetch=1, grid=(nb,),
        in_specs=[_bs((None, tr, C), lambda i, pc: (pc[0], i, 0)), other(0), other(1), other(2)],
        out_specs=_bs((tr, C), lambda i, pc: (pc[1] * nb + i, 0)))
    return _call(body, grid_spec=spec, out_shape=jax.ShapeDtypeStruct((2 * H, C), F32),
                 compiler_params=_cp(("parallel",)), name=name)(place, sent, arrived, arrived, arrived)


def _place():
    x, y, c = lax.axis_index("x"), lax.axis_index("y"), lax.axis_index("c")
    chips = [(1 - x, y), (x, 1 - y), (1 - x, 1 - y)]
    return x, y, c, chips


def _split(nrows, row_bytes, align, cap=None):
    k = max(1, min(cap or DMA_MAX_CHUNKS, (nrows * row_bytes) // DMA_CHUNK_BYTES))
    while k > 1 and nrows % (k * align):
        k -= 1
    return k


def _comm_call(plan, name):
    n_in, n_out = len(plan["ins"]), len(plan["out_shape"])

    def body(*refs):
        for phase in plan["phases"]:
            phase(refs[:n_in], refs[n_in:n_in + n_out], refs[n_in + n_out:])

    return _call(body, in_specs=[ANY] * n_in, out_specs=[ANY] * n_out, out_shape=plan["out_shape"],
                 scratch_shapes=plan["scratch"], name=name)(*plan["ins"])


def _comm_hooks(plan, nsteps, ins, outs, sems, *, before):
    if plan is None:
        return
    m = len(plan["phases"])
    step = pl.program_id(0)
    for p, phase in enumerate(plan["phases"]):
        if (p == 0) == before:
            at = 0 if p == 0 else (p * (nsteps - 1)) // (m - 1)
            pl.when(step == at)(functools.partial(phase, ins, outs, sems))


def _ag_plan(shards, axes):
    n = len(shards)
    shapes = [a.shape for a in shards]

    def window(ref, i, chip, half=None):
        S, ax = shapes[i], axes[i]
        idx = []
        for d in range(len(S)):
            off, size = 0, S[d]
            if d == 0 and half is not None:
                off, size = half * (S[0] // 2), S[0] // 2
            if d == ax:
                off = off + chip * S[ax]
            idx.append(pl.ds(off, size))
        return ref.at[tuple(idx)]

    def copies(src, full, sems):
        ssem, rsem = sems
        x, y, c, chips = _place()
        me = 2 * x + y
        sib = (x, y, 1 - c)
        idx = [2 * cx + cy for cx, cy in chips]

        def rcopy(i, k, s_ref, d_ref, to):
            return pltpu.make_async_remote_copy(src_ref=s_ref, dst_ref=d_ref, send_sem=ssem.at[i, k],
                                                recv_sem=rsem.at[i, k], device_id=to, device_id_type=MESH)

        def ici(i, j, incoming):
            half_src = src[i].at[pl.ds(c * (shapes[i][0] // 2), shapes[i][0] // 2)]
            return rcopy(i, j, half_src, window(full[i], i, idx[j] if incoming else me, c), (*chips[j], c))

        def fwd(i, j, half):
            w = window(full[i], i, idx[j], half)
            return rcopy(i, 3 + j, w, w, sib)

        def own(i):
            return rcopy(i, 6, src[i], window(full[i], i, me), sib)

        return c, ici, fwd, own

    def send(src, full, sems):
        c, ici, fwd, own = copies(src, full, sems)
        for i in range(n):
            for j in range(3):
                ici(i, j, False).start()
        for i in range(n):
            own(i).start()

    def forward(src, full, sems):
        c, ici, fwd, own = copies(src, full, sems)
        for i in range(n):
            for j in range(3):
                ici(i, j, True).wait_recv()
                fwd(i, j, c).start()

    def finish(src, full, sems):
        c, ici, fwd, own = copies(src, full, sems)
        for i in range(n):
            for j in range(3):
                fwd(i, j, 1 - c).wait_recv()
            own(i).wait()
        for i in range(n):
            for j in range(3):
                ici(i, j, False).wait_send()
                fwd(i, j, c).wait_send()

    out_shape = [jax.ShapeDtypeStruct(tuple(N_CHIP * d if k == ax else d for k, d in enumerate(S)), BF16)
                 for S, ax in zip(shapes, axes)]
    return dict(ins=list(shards), out_shape=out_shape, phases=[send, forward, finish],
                scratch=[pltpu.SemaphoreType.DMA((n, 7)), pltpu.SemaphoreType.DMA((n, 7))])


def _halves_to_sibling(grads, name):
    n = len(grads)

    def body(*refs):
        g, got = refs[:n], refs[n:2 * n]
        ssem, rsem = refs[2 * n:]
        x, y, c, _ = _place()
        sib = (x, y, 1 - c)
        for i in range(n):
            J, R, C = g[i].shape
            H = R // 2
            k = _split(H, C * 4, SUBLANES, cap=DMA_MAX_CHUNKS // J)
            hr = H // k
            for j in range(J):
                for q in range(k):
                    other = pl.ds(pl.multiple_of((1 - c) * H + q * hr, SUBLANES), hr)
                    to = pl.ds(q * hr, hr)
                    pltpu.make_async_remote_copy(src_ref=g[i].at[j, other, :], dst_ref=got[i].at[j, to, :],
                                                 send_sem=ssem.at[i], recv_sem=rsem.at[i],
                                                 device_id=sib, device_id_type=MESH).start()
        for i in range(n):
            pltpu.make_async_remote_copy(src_ref=got[i], dst_ref=got[i], send_sem=ssem.at[i], recv_sem=rsem.at[i],
                                         device_id=sib, device_id_type=MESH).wait()

    half = [jax.ShapeDtypeStruct((a.shape[0], a.shape[1] // 2, a.shape[2]), a.dtype) for a in grads]
    return _call(body, in_specs=[ANY] * n, out_specs=[ANY] * n, out_shape=half,
                 scratch_shapes=[pltpu.SemaphoreType.DMA((n,)), pltpu.SemaphoreType.DMA((n,))],
                 name=name)(*grads)


def _scatter_plan(parts):
    n = len(parts)

    def peers():
        x, y, c, chips = _place()
        return 2 * x + y, c, chips, [2 * cx + cy for cx, cy in chips]

    def send(s, got, sems):
        ssem, rsem = sems
        me, c, chips, idx = peers()
        for i in range(n):
            _, H, C = s[i].shape
            k = _split(H, C * 2, 16, cap=RS_CHUNKS)
            hr = H // k
            for q in range(k):
                rows = pl.ds(q * hr, hr)
                for j in range(3):
                    pltpu.make_async_remote_copy(src_ref=s[i].at[idx[j], rows, :], dst_ref=got[i].at[me, rows, :],
                                                 send_sem=ssem.at[i, j], recv_sem=rsem.at[i, j],
                                                 device_id=(*chips[j], c), device_id_type=MESH).start()

    def finish(s, got, sems):
        ssem, rsem = sems
        me, c, chips, idx = peers()
        for i in range(n):
            for j in range(3):
                pltpu.make_async_remote_copy(src_ref=s[i].at[idx[j]], dst_ref=got[i].at[idx[j]],
                                             send_sem=ssem.at[i, j], recv_sem=rsem.at[i, j],
                                             device_id=(*chips[j], c), device_id_type=MESH).wait()

    return dict(ins=list(parts), out_shape=[jax.ShapeDtypeStruct(a.shape, a.dtype) for a in parts],
                phases=[send, finish],
                scratch=[pltpu.SemaphoreType.DMA((n, 3)), pltpu.SemaphoreType.DMA((n, 3))])


def _join_halves(shards):
    n = len(shards)

    def body(*refs):
        full = refs[n:2 * n]
        ssem, rsem = refs[2 * n:]
        x, y, c, _ = _place()
        sib = (x, y, 1 - c)
        for i in range(n):
            H, C = full[i].shape[0] // 2, full[i].shape[1]
            k = _split(H, C * 4, SUBLANES)
            hr = H // k
            for q in range(k):
                rows = pl.ds(pl.multiple_of(c * H + q * hr, SUBLANES), hr)
                pltpu.make_async_remote_copy(src_ref=full[i].at[rows], dst_ref=full[i].at[rows],
                                             send_sem=ssem.at[i], recv_sem=rsem.at[i],
                                             device_id=sib, device_id_type=MESH).start()
        for i in range(n):
            half = full[i].at[pl.ds(0, full[i].shape[0] // 2)]
            pltpu.make_async_remote_copy(src_ref=half, dst_ref=half, send_sem=ssem.at[i], recv_sem=rsem.at[i],
                                         device_id=sib, device_id_type=MESH).wait()

    shp = [jax.ShapeDtypeStruct(a.shape, a.dtype) for a in shards]
    return _call(body, in_specs=[ANY] * n, out_specs=[ANY] * n, out_shape=shp,
                 input_output_aliases={i: i for i in range(n)},
                 scratch_shapes=[pltpu.SemaphoreType.DMA((n,)), pltpu.SemaphoreType.DMA((n,))],
                 name="rs_join")(*shards)


def _allreduce_small(buf):
    R, L = buf.shape
    RB = R // N_DEV

    def body(x_ref, o_ref, got, ssem, rsem):
        x, y, c, _ = _place()
        me = 4 * x + 2 * y + c

        def dev(k):
            return (k // 4, (k // 2) % 2, k % 2)

        def slab(k):
            return pl.ds(pl.multiple_of(k * RB, SUBLANES), RB)

        sends = []
        for d in range(1, N_DEV):
            peer = (me + d) % N_DEV
            cp = pltpu.make_async_remote_copy(src_ref=x_ref.at[slab(peer)], dst_ref=got.at[me],
                                              send_sem=ssem.at[0, d], recv_sem=rsem.at[0, d],
                                              device_id=dev(peer), device_id_type=MESH)
            cp.start()
            sends.append(cp)
        got[me] = x_ref[slab(me), :]
        for d in range(1, N_DEV):
            src = (me + N_DEV - d) % N_DEV
            pltpu.make_async_remote_copy(src_ref=x_ref.at[slab(me)], dst_ref=got.at[src],
                                         send_sem=ssem.at[0, d], recv_sem=rsem.at[0, d],
                                         device_id=dev(src), device_id_type=MESH).wait_recv()
        acc = got[0]
        for k in range(1, N_DEV):
            acc = acc + got[k]
        o_ref[slab(me), :] = acc
        for d in range(1, N_DEV):
            peer = (me + d) % N_DEV
            cp = pltpu.make_async_remote_copy(src_ref=o_ref.at[slab(me)], dst_ref=o_ref.at[slab(me)],
                                              send_sem=ssem.at[1, d], recv_sem=rsem.at[1, d],
                                              device_id=dev(peer), device_id_type=MESH)
            cp.start()
            sends.append(cp)
        for d in range(1, N_DEV):
            src = (me + N_DEV - d) % N_DEV
            pltpu.make_async_remote_copy(src_ref=o_ref.at[slab(src)], dst_ref=o_ref.at[slab(src)],
                                         send_sem=ssem.at[1, d], recv_sem=rsem.at[1, d],
                                         device_id=dev(src), device_id_type=MESH).wait_recv()
        for cp in sends:
            cp.wait_send()

    return _call(body, in_specs=[VMEM_FULL], out_specs=VMEM_FULL, out_shape=jax.ShapeDtypeStruct((R, L), F32),
                 scratch_shapes=[pltpu.VMEM((N_DEV, RB, L), F32), pltpu.SemaphoreType.DMA((2, N_DEV)),
                                 pltpu.SemaphoreType.DMA((2, N_DEV))],
                 name="allreduce_small")(buf)


def _block_diag(t, gt):
    G, A, B = t.shape
    t4 = t.reshape(G // gt, gt, A, B)
    eye = jnp.eye(gt, dtype=t.dtype)
    return jnp.einsum('sgab,gh->sgahb', t4, eye).reshape(G // gt, gt * A, gt * B)


def _block_diag_extract(m, gt, A, B):
    S = m.shape[0]
    m5 = m.reshape(S, gt, A, gt, B)
    eye = jnp.eye(gt, dtype=m.dtype)
    return jnp.einsum('sgahb,gh->sgab', m5, eye).reshape(S * gt, A, B)


def _pack_small(arrs, rows):
    flat = jnp.concatenate([a.reshape(-1).astype(F32) for a in arrs])
    return jnp.pad(flat, (0, rows * LANES - flat.shape[0])).reshape(rows, LANES)


def _unpack_small(buf, shapes):
    flat = buf.reshape(-1)
    out, off = [], 0
    for s in shapes:
        n = 1
        for d in s:
            n *= d
        out.append(flat[off:off + n].reshape(s))
        off += n
    return out


def kernel(x, p, norm_gain, w_in, w_pool, pool_scale, a_re, a_im, log_dt, b_re, b_im, c_re, c_im, d_skip, w_glu, w_out, w_ple, w_ple_gate, final_gain, loss_target, m_norm_gain, m_w_in, m_w_pool, m_pool_scale, m_a_re, m_a_im, m_log_dt, m_b_re, m_b_im, m_c_re, m_c_im, m_d_skip, m_w_glu, m_w_out, m_w_ple, m_w_ple_gate, m_final_gain, v_norm_gain, v_w_in, v_w_pool, v_pool_scale, v_a_re, v_a_im, v_log_dt, v_b_re, v_b_im, v_c_re, v_c_im, v_d_skip, v_w_glu, v_w_out, v_w_ple, v_w_ple_gate, v_final_gain):
    xs, pe, tgt = x[0], p[0, 0], loss_target[0]
    T, D = xs.shape
    E = pe.shape[1]
    P = D // 2
    NG = len(POOL_WINDOWS)
    PG = P // NG
    G, N, C = P // SSM_GROUP, SSM_STATE, SSM_GROUP
    GT = min(SSM_TILE_GROUPS, G)
    Q = D // N_CHIP

    big = {"w_in": (w_in, m_w_in, v_w_in), "w_pool": (w_pool, m_w_pool, v_w_pool),
           "w_glu": (w_glu, m_w_glu, v_w_glu), "w_out": (w_out, m_w_out, v_w_out),
           "w_ple": (w_ple, m_w_ple, v_w_ple), "w_ple_gate": (w_ple_gate, m_w_ple_gate, v_w_ple_gate)}
    big_names = list(big)
    shard2d = {n: (big[n][0].size // big[n][0].shape[-1], big[n][0].shape[-1]) for n in big_names}
    shard_axis = {"w_in": 1, "w_pool": 1, "w_glu": 1, "w_out": 0, "w_ple": 1, "w_ple_gate": 0}
    shard16 = {n: big[n][0][0].astype(BF16) for n in big_names}
    place = jnp.stack([2 * lax.axis_index("x") + lax.axis_index("y"), lax.axis_index("c")]).astype(jnp.int32)
    win, = _comm_call(_ag_plan([shard16["w_in"]], [shard_axis["w_in"]]), "ag_w_in")
    later = [n for n in big_names if n != "w_in"]
    ag_later = _ag_plan([shard16[n] for n in later], [shard_axis[n] for n in later])

    rep = lambda a: jnp.repeat(a, C, axis=0)
    a_re_r, a_im_r = rep(a_re[0]), rep(a_im[0])
    ldt_r = rep(jnp.broadcast_to(log_dt[0][:, None], (G, N)))
    bt_re = b_re[0].transpose(0, 2, 1).reshape(G * C, N)
    bt_im = b_im[0].transpose(0, 2, 1).reshape(G * C, N)
    ab_re_r, ab_im_r, bbt_re, bbt_im = _ssm_prep(a_re_r, a_im_r, ldt_r, bt_re, bt_im)
    abr = ab_re_r[::C].reshape(1, G * N)
    abi = ab_im_r[::C].reshape(1, G * N)
    bdr = _block_diag(bbt_re.reshape(G, C, N), GT).astype(BF16)
    bdi = _block_diag(bbt_im.reshape(G, C, N), GT).astype(BF16)
    cdr = _block_diag(c_re[0].transpose(0, 2, 1), GT).astype(BF16)
    cdi = _block_diag(c_im[0].transpose(0, 2, 1), GT).astype(BF16)

    tb = _t(T, 256)
    tbs = _t(T, 256)
    tm = _t(T, 1024)
    tk = _t(T, 2048)
    DH = _t(D, 1024)
    row_k = lambda i, n, k: (i, k)
    row_n = lambda i, n, k: (i, n)
    f32 = lambda *shape: jax.ShapeDtypeStruct(shape, F32)
    hn = _norm1(xs, norm_gain, tb)
    proj = _mm(hn, win, dims=NN, grid=(T // tm, N_CHIP, 1),
               a_spec=_bs((tm, D), row_k), b_spec=_bs((D, P), lambda i, n, k: (k, n)),
               o_spec=_bs((tm, P), row_n), out_shape=f32(T, 4 * P), name="mm_proj")
    y, ge, bsr, bsi, wp, wglu, wout, wple, wpg = _ssm_fwd(proj, bdr, bdi, cdr, cdi, abr, abi, d_skip, P, tbs,
                                                          comm=ag_later)
    pooled = _pool_fwd(proj, P, tb)
    mixed = _mm(pooled, wp, dims=NN, grid=(T // tm, NG, 1),
                a_spec=_bs((tm, PG), row_n), b_spec=_bs((None, PG, PG), lambda i, g, k: (g, 0, 0)),
                o_spec=_bs((tm, PG), row_n), out_shape=f32(T, P), name="mm_pool")
    hg = _mm(ge, wglu, dims=NN, grid=(T // tm, 2 * P // DH, 1),
             a_spec=_bs((tm, P), row_k), b_spec=_bs((P, DH), lambda i, n, k: (k, n)),
             o_spec=_bs((tm, DH), row_n), out_shape=f32(T, 2 * P), name="mm_glu")
    cat = _gate_fwd(mixed, proj, hg, pool_scale, tb)
    h1, h1b = _mm(cat, wout, dims=NN, grid=(T // tm, D // DH, 1), res=xs, bf16_copy=True,
                  a_spec=_bs((tm, D), row_k), b_spec=_bs((D, DH), lambda i, n, k: (k, n)),
                  r_spec=_bs((tm, DH), row_n), o_spec=_bs((tm, DH), row_n), out_shape=f32(T, D), name="mm_out")
    e = _mm(pe, wple, dims=NN, grid=(T // tm, D // DH, 1),
            a_spec=_bs((tm, E), row_k), b_spec=_bs((E, DH), lambda i, n, k: (k, n)),
            o_spec=_bs((tm, DH), row_n), out_shape=f32(T, D), name="mm_ple")
    z = _mm(h1b, wpg, dims=NN, grid=(T // tm, D // DH, 1),
            a_spec=_bs((tm, D), row_k), b_spec=_bs((D, DH), lambda i, n, k: (k, n)),
            o_spec=_bs((tm, DH), row_n), out_shape=f32(T, D), name="mm_pgate")
    dh2, de, dz, dg2, lpart = _final_fb(h1, e, z, tgt, final_gain.reshape(1, D), tb)

    col_m = lambda m, n, k: (k, m)
    col_n = lambda m, n, k: (k, n)
    dh1, dh1b = _mm(dz, wpg, dims=NT, grid=(T // tm, D // DH, 1), res=dh2, bf16_copy=True,
                    a_spec=_bs((tm, D), row_k), b_spec=_bs((DH, D), lambda i, n, k: (n, k)),
                    r_spec=_bs((tm, DH), row_n), o_spec=_bs((tm, DH), row_n), out_shape=f32(T, D), name="mm_dh1")
    g_wpg = _mm(h1b, dz, dims=TN, grid=(D // DH, D // DH, T // tk),
                a_spec=_bs((tk, DH), col_m), b_spec=_bs((tk, DH), col_n),
                o_spec=_bs((DH, DH), lambda m, n, k: (m, n)), out_shape=f32(D, D), name="mm_gwpg")
    g_wple = _mm(pe, de, dims=TN, grid=(1, N_CHIP, T // tk),
                 a_spec=_bs((tk, E), col_m), b_spec=_bs((tk, Q), col_n),
                 o_spec=_bs((None, E, Q), lambda m, j, k: (j, 0, 0)), out_shape=f32(N_CHIP, E, Q), name="mm_gwple")
    dcat = _mm(dh1b, wout, dims=NT, grid=(T // tm, D // DH, 1),
               a_spec=_bs((tm, D), row_k), b_spec=_bs((DH, D), lambda i, n, k: (n, k)),
               o_spec=_bs((tm, DH), row_n), out_shape=f32(T, D), name="mm_dcat")
    g_wout = _mm(cat, dh1b, dims=TN, grid=(D // DH, D // DH, T // tk),
                 a_spec=_bs((tk, DH), col_m), b_spec=_bs((tk, DH), col_n),
                 o_spec=_bs((DH, DH), lambda m, n, k: (m, n)), out_shape=f32(D, D), name="mm_gwout")
    dmixed, dpg, dsg, dhg, dps = _gate_bwd(dcat, mixed, proj, hg, pool_scale, tb)
    dge = _mm(dhg, wglu, dims=NT, grid=(T // tm, 1, 1),
              a_spec=_bs((tm, 2 * P), row_k), b_spec=_bs((P, 2 * P), lambda i, n, k: (n, k)),
              o_spec=_bs((tm, P), row_n), out_shape=f32(T, P), name="mm_dge")
    g_wglu = _mm(ge, dhg, dims=TN, grid=(1, N_CHIP, T // tk),
                 a_spec=_bs((tk, P), col_m), b_spec=_bs((tk, Q), col_n),
                 o_spec=_bs((None, P, Q), lambda m, j, k: (j, 0, 0)), out_shape=f32(N_CHIP, P, Q), name="mm_gwglu")
    gbig = {"w_glu": g_wglu, "w_out": g_wout.reshape(N_CHIP, Q, D), "w_ple": g_wple,
            "w_ple_gate": g_wpg.reshape(N_CHIP, Q, D)}
    early = list(gbig)
    got = _halves_to_sibling([gbig[n] for n in early], "rs_halves_early")
    chip_sums = {n: _sum_cast(gbig[n], g_, place, "sum_cast_" + n) for g_, n in zip(got, early)}
    res = _ssm_bwd(proj, y, dge, bsr, bsi, bdr, bdi, cdr, cdi, abr, abi, d_skip, P, tbs,
                   comm=_scatter_plan([chip_sums[n] for n in early]))
    du, dabr, dabi, dd, dbdr, dbdi, dcdr, dcdi = res[:8]
    arrived = dict(zip(early, res[8:]))
    dpooled = _mm(dmixed, wp, dims=NT, grid=(T // tm, NG, 1),
                  a_spec=_bs((tm, PG), row_n), b_spec=_bs((None, PG, PG), lambda i, g, k: (g, 0, 0)),
                  o_spec=_bs((tm, PG), row_n), out_shape=f32(T, P), name="mm_dpooled")
    g_wp = _mm(pooled, dmixed, dims=TN, grid=(NG, 1, T // tk),
               a_spec=_bs((tk, PG), col_m), b_spec=_bs((tk, PG), col_m),
               o_spec=_bs((None, PG, PG), lambda g, n, k: (g, 0, 0)), out_shape=f32(NG, PG, PG), name="mm_gwp")
    dpi = _pool_bwd(dpooled, tb)
    dproj = jnp.concatenate([dpi, dpg, du, dsg], axis=1)
    KH = _t(4 * P, 2048)
    dhn = _mm(dproj, win, dims=NT, grid=(T // tm, D // DH, 4 * P // KH),
              a_spec=_bs((tm, KH), row_k), b_spec=_bs((DH, KH), lambda i, n, k: (n, k)),
              o_spec=_bs((tm, DH), row_n), out_shape=f32(T, D), name="mm_dhn")
    g_win = _mm(hn, dproj, dims=TN, grid=(D // DH, N_CHIP, T // tk),
                a_spec=_bs((tk, DH), col_m), b_spec=_bs((tk, P), col_n),
                o_spec=_bs((None, DH, P), lambda m, j, k: (j, m, 0)), out_shape=f32(N_CHIP, D, P), name="mm_gwin")
    grad_x, dg1 = _norm1_bwd(xs, dhn, dh1, norm_gain, tb)

    dbbt_re = _block_diag_extract(dbdr, GT, C, N).reshape(G * C, N)
    dbbt_im = _block_diag_extract(dbdi, GT, C, N).reshape(G * C, N)
    g_c_re = _block_diag_extract(dcdr, GT, N, C).transpose(0, 2, 1)
    g_c_im = _block_diag_extract(dcdi, GT, N, C).transpose(0, 2, 1)
    dab_re_r = rep(dabr.reshape(G, N)) * (1.0 / C)
    dab_im_r = rep(dabi.reshape(G, N)) * (1.0 / C)
    g_a_re, g_a_im, g_ldt, g_bt_re, g_bt_im = _ssm_prep_bwd(a_re_r, a_im_r, ldt_r, bt_re, bt_im,
                                                            dab_re_r, dab_im_r, dbbt_re, dbbt_im, G)
    g_b_re = g_bt_re.reshape(G, C, N).transpose(0, 2, 1)
    g_b_im = g_bt_im.reshape(G, C, N).transpose(0, 2, 1)

    gbig["w_in"] = g_win
    gbig["w_pool"] = g_wp.reshape(NG, N_CHIP, PG // N_CHIP, PG).transpose(1, 0, 2, 3).reshape(
        N_CHIP, NG * PG // N_CHIP, PG)
    late = ["w_in", "w_pool"]
    got = _halves_to_sibling([gbig[n] for n in late], "rs_halves_late")
    chip_sums.update({n: _sum_cast(gbig[n], g_, place, "sum_cast_" + n) for g_, n in zip(got, late)})
    arrived.update(zip(late, _comm_call(_scatter_plan([chip_sums[n] for n in late]), "rs_chips_late")))
    halves = [_sum_chips(chip_sums[n], arrived[n], place, "sum_chips_" + n) for n in big_names]
    gshard = _join_halves(halves)

    small_names = ["norm_gain", "pool_scale", "a_re", "a_im", "log_dt", "b_re", "b_im", "c_re", "c_im",
                   "d_skip", "final_gain"]
    small_w = dict(norm_gain=norm_gain, pool_scale=pool_scale, a_re=a_re, a_im=a_im, log_dt=log_dt, b_re=b_re,
                   b_im=b_im, c_re=c_re, c_im=c_im, d_skip=d_skip, final_gain=final_gain)
    small_m = dict(norm_gain=m_norm_gain, pool_scale=m_pool_scale, a_re=m_a_re, a_im=m_a_im, log_dt=m_log_dt,
                   b_re=m_b_re, b_im=m_b_im, c_re=m_c_re, c_im=m_c_im, d_skip=m_d_skip, final_gain=m_final_gain)
    small_v = dict(norm_gain=v_norm_gain, pool_scale=v_pool_scale, a_re=v_a_re, a_im=v_a_im, log_dt=v_log_dt,
                   b_re=v_b_re, b_im=v_b_im, c_re=v_c_re, c_im=v_c_im, d_skip=v_d_skip, final_gain=v_final_gain)
    small_g = dict(norm_gain=dg1, pool_scale=dps, a_re=g_a_re, a_im=g_a_im, log_dt=g_ldt, b_re=g_b_re,
                   b_im=g_b_im, c_re=g_c_re, c_im=g_c_im, d_skip=dd, final_gain=dg2)
    shapes = [small_w[n].shape for n in small_names]
    total = sum(small_w[n].size for n in small_names) + 1
    unit = N_DEV * SUBLANES
    rows = -(-(-(-total // LANES)) // unit) * unit
    gbuf = _pack_small([small_g[n] for n in small_names] + [lpart[0, :1]], rows)
    gsum = _allreduce_small(gbuf)
    wbuf = _pack_small([small_w[n] for n in small_names], rows)
    mbuf = _pack_small([small_m[n] for n in small_names], rows)
    vbuf = _pack_small([small_v[n] for n in small_names], rows)
    dsm, msm, vsm = _adamw(wbuf, gsum, mbuf, vbuf, "adamw_small")
    g_small = dict(zip(small_names, _unpack_small(gsum, shapes)))
    d_small = dict(zip(small_names, _unpack_small(dsm, shapes)))
    m_small = dict(zip(small_names, _unpack_small(msm, shapes)))
    v_small = dict(zip(small_names, _unpack_small(vsm, shapes)))
    loss = gsum.reshape(-1)[total - 1]

    g_out, d_out, m_out, v_out = dict(g_small), dict(d_small), dict(m_small), dict(v_small)
    for n, gs in zip(big_names, gshard):
        w_, m_, v_ = big[n]
        r2 = shard2d[n]
        d_, mn_, vn_ = _adamw(w_.reshape(r2), gs, m_.reshape(r2), v_.reshape(r2), "adamw_" + n)
        g_out[n], d_out[n], m_out[n], v_out[n] = (a.reshape(w_.shape) for a in (gs, d_, mn_, vn_))

    order = ["norm_gain", "w_in", "w_pool", "pool_scale", "a_re", "a_im", "log_dt", "b_re", "b_im", "c_re",
             "c_im", "d_skip", "w_glu", "w_out", "w_ple", "w_ple_gate", "final_gain"]
    return (loss, grad_x[None], *[g_out[n] for n in order], *[d_out[n] for n in order],
            *[m_out[n] for n in order], *[v_out[n] for n in order])
```

```python
import functools

import jax
import jax.numpy as jnp
from jax import lax
from jax.experimental import pallas as pl
from jax.experimental.pallas import tpu as pltpu

F32, BF16 = jnp.float32, jnp.bfloat16
MESH = pl.DeviceIdType.MESH
ANY = pl.BlockSpec(memory_space=pl.ANY)
VMEM_FULL = pl.BlockSpec(memory_space=pltpu.VMEM)

EPS = 1e-6
A_RE_MAX = -1e-4
SSM_GROUP = 16
SSM_STATE = 64
POOL_WINDOWS = (2, 4, 8, 16)
POOL_HALO = 16
ADAM_LR, ADAM_B1, ADAM_B2, ADAM_EPS, ADAM_WD, ADAM_STEP = 0.001, 0.9, 0.999, 1e-08, 0.01, 10

V7X_VMEM_BYTES = 64 * 1024 * 1024
VMEM_LIMIT = V7X_VMEM_BYTES - 8 * 1024 * 1024
SUBLANES, LANES = 8, 128
SSM_TILE_GROUPS = 8
SCAN_LANES = 512
N_DEV, N_CHIP = 8, 4
DMA_CHUNK_BYTES = 256 * 1024
DMA_MAX_CHUNKS = 32
AG_CHUNKS = 8
RS_CHUNKS = 8


def _t(n, pref):
    return pref if n % pref == 0 else n


def _cp(sem=None, vmem=VMEM_LIMIT):
    return pltpu.CompilerParams(dimension_semantics=sem, vmem_limit_bytes=vmem)


def _call(body, **kw):
    return pl.pallas_call(body, **kw)


NN = ((1,), (0,))
NT = ((1,), (1,))
TN = ((0,), (0,))


def _mm(a, b, *, dims, grid, a_spec, b_spec, o_spec, out_shape, name, res=None, r_spec=None, bf16_copy=False,
        comm=None):
    nk, kax = grid[-1], len(grid) - 1
    acc_shape = tuple(d for d in o_spec.block_shape if d is not None)

    def core(*refs):
        refs = list(refs)
        a_ref, b_ref = refs[:2]
        r_ref = refs[2] if res is not None else None
        outs = refs[3 if res is not None else 2:]
        o_ref = outs[0]
        o2_ref = outs[1] if bf16_copy else None
        acc = outs[-1] if nk > 1 else None

        def finish(r):
            if r_ref is not None:
                r = r + r_ref[...]
            o_ref[...] = r.astype(o_ref.dtype)
            if o2_ref is not None:
                o2_ref[...] = r.astype(BF16)

        part = lax.dot_general(a_ref[...].astype(BF16), b_ref[...].astype(BF16),
                               (dims, ((), ())), preferred_element_type=F32)
        if nk == 1:
            finish(part)
        else:
            k = pl.program_id(kax)

            @pl.when(k == 0)
            def _():
                acc[...] = part

            @pl.when(k > 0)
            def _():
                acc[...] += part

            @pl.when(k == nk - 1)
            def _():
                finish(acc[...])

    ins, specs = [a, b], [a_spec, b_spec]
    if res is not None:
        ins.append(res)
        specs.append(r_spec)
    o_specs, o_shapes = [o_spec], [out_shape]
    if bf16_copy:
        o_specs = [o_spec, o_spec]
        o_shapes = [out_shape, jax.ShapeDtypeStruct(out_shape.shape, BF16)]
    scratch = [pltpu.VMEM(acc_shape, F32)] if nk > 1 else []
    body, extra = _hosted(core, comm, grid, len(ins), len(o_specs), len(scratch))
    sem = ("arbitrary",) * len(grid) if comm else ("parallel",) * kax + ("arbitrary",)
    outs = _call(body, grid=grid, in_specs=specs + extra["in_specs"], out_specs=o_specs + extra["out_specs"],
                 out_shape=o_shapes + extra["out_shape"], scratch_shapes=scratch + extra["scratch"],
                 compiler_params=_cp(sem), name=name)(*ins, *extra["ins"])
    return outs[0] if len(outs) == 1 else outs


def _bs(shape, fn):
    return pl.BlockSpec(shape, fn)


def _sigmoid(v):
    return 1.0 / (1.0 + jnp.exp(-v))


def _gelu(v):
    return 0.5 * v * (1.0 + jnp.tanh(0.7978845608028654 * (v + 0.044715 * v * v * v)))


def _gelu_grad(v):
    t = jnp.tanh(0.7978845608028654 * (v + 0.044715 * v * v * v))
    return 0.5 * (1.0 + t) + 0.5 * v * (1.0 - t * t) * 0.7978845608028654 * (1.0 + 3 * 0.044715 * v * v)


def _norm1(x, g1, tb):
    T, D = x.shape

    def body(x_ref, g_ref, o_ref):
        xv = x_ref[...]
        r = lax.rsqrt(jnp.mean(xv * xv, axis=-1, keepdims=True) + EPS)
        o_ref[...] = ((xv * r) * g_ref[...]).astype(BF16)

    return _call(body, grid=(T // tb,),
                 in_specs=[_bs((tb, D), lambda i: (i, 0)), _bs((1, D), lambda i: (0, 0))],
                 out_specs=_bs((tb, D), lambda i: (i, 0)), out_shape=jax.ShapeDtypeStruct((T, D), BF16),
                 compiler_params=_cp(("parallel",)), name="norm1")(x, g1)


def _norm1_bwd(x, dhn, dh1, g1, tb):
    T, D = x.shape

    def body(x_ref, dhn_ref, dh1_ref, g_ref, dx_ref, dg_ref):
        @pl.when(pl.program_id(0) == 0)
        def _():
            dg_ref[...] = jnp.zeros_like(dg_ref)

        xv = x_ref[...]
        r = lax.rsqrt(jnp.mean(xv * xv, axis=-1, keepdims=True) + EPS)
        xh = xv * r
        dhn_v = dhn_ref[...]
        dg_ref[...] += jnp.sum(dhn_v * xh, axis=0, keepdims=True)
        dxh = dhn_v * g_ref[...]
        dx_ref[...] = dh1_ref[...] + r * (dxh - xh * jnp.mean(dxh * xh, axis=-1, keepdims=True))

    row = _bs((tb, D), lambda i: (i, 0))
    vec = _bs((1, D), lambda i: (0, 0))
    return _call(body, grid=(T // tb,), in_specs=[row, row, row, vec], out_specs=[row, vec],
                 out_shape=[jax.ShapeDtypeStruct((T, D), F32), jax.ShapeDtypeStruct((1, D), F32)],
                 compiler_params=_cp(("arbitrary",)), name="norm1_bwd")(x, dhn, dh1, g1)


def _gate_fwd(mixed, proj, hg, ps, tb):
    T, P = mixed.shape

    def body(mx_ref, pg_ref, sg_ref, hg_ref, ps_ref, o_ref):
        pg, sg = pg_ref[...], sg_ref[...]
        ya = (mx_ref[...] * ps_ref[...]) * (pg * _sigmoid(pg))
        hgv = hg_ref[...]
        o = hgv[:, :P] * _sigmoid(hgv[:, P:])
        yb = o * (sg * _sigmoid(sg))
        o_ref[:, :P] = ya.astype(BF16)
        o_ref[:, P:] = yb.astype(BF16)

    return _call(body, grid=(T // tb,),
                 in_specs=[_bs((tb, P), lambda i: (i, 0)), _bs((tb, P), lambda i: (i, 1)),
                           _bs((tb, P), lambda i: (i, 3)), _bs((tb, 2 * P), lambda i: (i, 0)),
                           _bs((1, P), lambda i: (0, 0))],
                 out_specs=_bs((tb, 2 * P), lambda i: (i, 0)),
                 out_shape=jax.ShapeDtypeStruct((T, 2 * P), BF16),
                 compiler_params=_cp(("parallel",)), name="gate_fwd")(mixed, proj, proj, hg, ps)


def _gate_bwd(dcat, mixed, proj, hg, ps, tb):
    T, P = mixed.shape

    def body(dc_ref, mx_ref, pg_ref, sg_ref, hg_ref, ps_ref, dmx_ref, dpg_ref, dsg_ref, dhg_ref, dps_ref):
        @pl.when(pl.program_id(0) == 0)
        def _():
            dps_ref[...] = jnp.zeros_like(dps_ref)

        dc = dc_ref[...]
        dya, dyb = dc[:, :P], dc[:, P:]
        pg, sg, mx, psv = pg_ref[...], sg_ref[...], mx_ref[...], ps_ref[...]
        s_pg = _sigmoid(pg)
        dpa = dya * (pg * s_pg)
        dpg_ref[...] = (dya * (mx * psv) * (s_pg * (1.0 + pg * (1.0 - s_pg)))).astype(BF16)
        dps_ref[...] += jnp.sum(dpa * mx, axis=0, keepdims=True)
        dmx_ref[...] = (dpa * psv).astype(BF16)
        hgv = hg_ref[...]
        h1, s_h2 = hgv[:, :P], _sigmoid(hgv[:, P:])
        s_sg = _sigmoid(sg)
        do = dyb * (sg * s_sg)
        dsg_ref[...] = (dyb * (h1 * s_h2) * (s_sg * (1.0 + sg * (1.0 - s_sg)))).astype(BF16)
        dhg_ref[:, :P] = (do * s_h2).astype(BF16)
        dhg_ref[:, P:] = (do * h1 * s_h2 * (1.0 - s_h2)).astype(BF16)

    rowp = _bs((tb, P), lambda i: (i, 0))
    row2 = _bs((tb, 2 * P), lambda i: (i, 0))
    vec = _bs((1, P), lambda i: (0, 0))
    return _call(body, grid=(T // tb,),
                 in_specs=[row2, rowp, _bs((tb, P), lambda i: (i, 1)), _bs((tb, P), lambda i: (i, 3)), row2, vec],
                 out_specs=[rowp, rowp, rowp, row2, vec],
                 out_shape=[jax.ShapeDtypeStruct((T, P), BF16), jax.ShapeDtypeStruct((T, P), BF16),
                            jax.ShapeDtypeStruct((T, P), BF16), jax.ShapeDtypeStruct((T, 2 * P), BF16),
                            jax.ShapeDtypeStruct((1, P), F32)],
                 compiler_params=_cp(("arbitrary",)), name="gate_bwd")(dcat, mixed, proj, proj, hg, ps)


def _final_fb(h1, e, z, tgt, g2, tb):
    T, D = h1.shape

    def body(h1_ref, e_ref, z_ref, t_ref, g_ref, dh2_ref, de_ref, dz_ref, dg_ref, l_ref):
        @pl.when(pl.program_id(0) == 0)
        def _():
            dg_ref[...] = jnp.zeros_like(dg_ref)
            l_ref[...] = jnp.zeros_like(l_ref)

        ev = e_ref[...]
        s = _sigmoid(z_ref[...])
        h2 = h1_ref[...] + ev * s
        r = lax.rsqrt(jnp.mean(h2 * h2, axis=-1, keepdims=True) + EPS)
        xh = h2 * r
        gv = g_ref[...]
        diff = xh * gv - t_ref[...]
        l_ref[...] += 0.5 * jnp.sum(jnp.mean(diff * diff, axis=-1, keepdims=True))
        dout = diff * (1.0 / D)
        dg_ref[...] += jnp.sum(dout * xh, axis=0, keepdims=True)
        dxh = dout * gv
        dh2 = r * (dxh - xh * jnp.mean(dxh * xh, axis=-1, keepdims=True))
        dh2_ref[...] = dh2
        de_ref[...] = (dh2 * s).astype(BF16)
        dz_ref[...] = (dh2 * ev * s * (1.0 - s)).astype(BF16)

    row = _bs((tb, D), lambda i: (i, 0))
    vec = _bs((1, D), lambda i: (0, 0))
    return _call(body, grid=(T // tb,), in_specs=[row, row, row, row, vec],
                 out_specs=[row, row, row, vec, _bs((1, LANES), lambda i: (0, 0))],
                 out_shape=[jax.ShapeDtypeStruct((T, D), F32), jax.ShapeDtypeStruct((T, D), BF16),
                            jax.ShapeDtypeStruct((T, D), BF16), jax.ShapeDtypeStruct((1, D), F32),
                            jax.ShapeDtypeStruct((1, LANES), F32)],
                 compiler_params=_cp(("arbitrary",)), name="final_fb")(h1, e, z, tgt, g2)


def _pool_inv_count(t0, rows, pg, ngroups):
    t = t0 + lax.broadcasted_iota(jnp.int32, (rows, pg), 0)
    parts = []
    for w in POOL_WINDOWS[:ngroups]:
        parts.append(jnp.where(t + 1 >= w, 1.0 / w, 1.0 / (t + 1).astype(F32)))
    return parts


def _pool_fwd(proj, P, tb):
    T = proj.shape[0]
    ng = len(POOL_WINDOWS)
    pg = P // ng
    hb = tb // POOL_HALO

    def body(v_ref, tail_ref, o_ref, ext):
        i = pl.program_id(0)
        ext[pl.ds(0, POOL_HALO), :] = jnp.where(i > 0, tail_ref[...], 0.0)
        ext[pl.ds(POOL_HALO, tb), :] = v_ref[...]
        inv = _pool_inv_count(i * tb, tb, pg, ng)
        for g, w in enumerate(POOL_WINDOWS):
            cols = pl.ds(g * pg, pg)
            win = ext[pl.ds(POOL_HALO, tb), cols]
            for k in range(1, w):
                win = win + ext[pl.ds(POOL_HALO - k, tb), cols]
            o_ref[:, cols] = (win * inv[g] - ext[pl.ds(POOL_HALO, tb), cols]).astype(BF16)

    return _call(body, grid=(T // tb,),
                 in_specs=[_bs((tb, P), lambda i: (i, 0)),
                           _bs((POOL_HALO, P), lambda i: (jnp.maximum(i * hb - 1, 0), 0))],
                 out_specs=_bs((tb, P), lambda i: (i, 0)), out_shape=jax.ShapeDtypeStruct((T, P), BF16),
                 scratch_shapes=[pltpu.VMEM((tb + POOL_HALO, P), F32)],
                 compiler_params=_cp(("arbitrary",)), name="pool_fwd")(proj, proj)


def _pool_bwd(dpooled, tb):
    T, P = dpooled.shape
    ng = len(POOL_WINDOWS)
    pg = P // ng
    hb = tb // POOL_HALO
    nb = T // tb

    def body(d_ref, head_ref, o_ref, ext):
        i = pl.program_id(0)
        inv = _pool_inv_count(i * tb, tb, pg, ng)
        invh = _pool_inv_count((i + 1) * tb, POOL_HALO, pg, ng)
        for g in range(ng):
            cols = pl.ds(g * pg, pg)
            ext[pl.ds(0, tb), cols] = d_ref[:, cols] * inv[g]
            ext[pl.ds(tb, POOL_HALO), cols] = jnp.where(i < nb - 1, head_ref[:, cols] * invh[g], 0.0)
        for g, w in enumerate(POOL_WINDOWS):
            cols = pl.ds(g * pg, pg)
            acc = ext[pl.ds(0, tb), cols]
            for k in range(1, w):
                acc = acc + ext[pl.ds(k, tb), cols]
            o_ref[:, cols] = (acc - d_ref[:, cols]).astype(BF16)

    return _call(body, grid=(nb,),
                 in_specs=[_bs((tb, P), lambda i: (i, 0)),
                           _bs((POOL_HALO, P), lambda i: (jnp.minimum((i + 1) * hb, T // POOL_HALO - 1), 0))],
                 out_specs=_bs((tb, P), lambda i: (i, 0)), out_shape=jax.ShapeDtypeStruct((T, P), BF16),
                 scratch_shapes=[pltpu.VMEM((tb + POOL_HALO, P), F32)],
                 compiler_params=_cp(("arbitrary",)), name="pool_bwd")(dpooled, dpooled)


def _zoh(a_re, a_im, ldt, b_re, b_im):
    lam_re = jnp.minimum(a_re, A_RE_MAX)
    lam_im = a_im
    dt = jnp.exp(ldt)
    mag = jnp.exp(lam_re * dt)
    ang = lam_im * dt
    ab_re = mag * jnp.cos(ang)
    ab_im = mag * jnp.sin(ang)
    den = lam_re * lam_re + lam_im * lam_im
    n_re = ab_re - 1.0
    n_im = ab_im
    q_re = (n_re * lam_re + n_im * lam_im) / den
    q_im = (n_im * lam_re - n_re * lam_im) / den
    return ab_re, ab_im, q_re * b_re - q_im * b_im, q_re * b_im + q_im * b_re


def _ssm_prep(a_re, a_im, ldt, bt_re, bt_im):
    shp = jax.ShapeDtypeStruct(a_re.shape, F32)

    def body(a, b, c, d, e, o0, o1, o2, o3):
        r = _zoh(a[...], b[...], c[...], d[...], e[...])
        o0[...], o1[...], o2[...], o3[...] = r

    return _call(body, in_specs=[VMEM_FULL] * 5, out_specs=[VMEM_FULL] * 4, out_shape=[shp] * 4,
                 name="ssm_prep")(a_re, a_im, ldt, bt_re, bt_im)


def _ssm_prep_bwd(a_re, a_im, ldt, bt_re, bt_im, dab_re, dab_im, dbb_re, dbb_im, G):
    GC, N = a_re.shape
    C = GC // G

    def body(a, b, c, d, e, g0, g1, g2, g3, da_re, da_im, dldt, db_re, db_im):
        _, vjp = jax.vjp(_zoh, a[...], b[...], c[...], d[...], e[...])
        ga_re, ga_im, gl, gb_re, gb_im = vjp((g0[...], g1[...], g2[...], g3[...]))
        da_re[...] = jnp.sum(ga_re.reshape(G, C, N), axis=1)
        da_im[...] = jnp.sum(ga_im.reshape(G, C, N), axis=1)
        dldt[...] = jnp.sum(jnp.sum(gl.reshape(G, C, N), axis=1), axis=1, keepdims=True)
        db_re[...] = gb_re
        db_im[...] = gb_im

    gn = jax.ShapeDtypeStruct((G, N), F32)
    full = jax.ShapeDtypeStruct((GC, N), F32)
    return _call(body, in_specs=[VMEM_FULL] * 9, out_specs=[VMEM_FULL] * 5,
                 out_shape=[gn, gn, jax.ShapeDtypeStruct((G, 1), F32), full, full],
                 name="ssm_prep_bwd")(a_re, a_im, ldt, bt_re, bt_im, dab_re, dab_im, dbb_re, dbb_im)


def _coef_tiles(abr, abi, reverse):
    ns = abr.shape[1]
    row = lax.broadcasted_iota(jnp.int32, (SUBLANES, ns), 0)
    ar = jnp.broadcast_to(abr, (SUBLANES, ns))
    ai = jnp.broadcast_to(-abi if reverse else abi, (SUBLANES, ns))
    a2r, a2i = ar * ar - ai * ai, 2.0 * ar * ai
    a4r, a4i = a2r * a2r - a2i * a2i, 2.0 * a2r * a2i
    out = []
    for d, (vr, vi) in ((1, (ar, ai)), (2, (a2r, a2i)), (4, (a4r, a4i))):
        keep = (row < SUBLANES - d) if reverse else (row >= d)
        out += [jnp.where(keep, vr, 0.0), jnp.where(keep, vi, 0.0)]
    pr, pi = ar, ai
    for k in range(1, SUBLANES):
        sel = (row <= SUBLANES - 1 - k) if reverse else (row >= k)
        nr, ni = pr * ar - pi * ai, pr * ai + pi * ar
        pr, pi = jnp.where(sel, nr, pr), jnp.where(sel, ni, pi)
    return out + [pr, pi]


def _cpow(ar, ai, n):
    out, br, bi = None, ar, ai
    while n:
        if n & 1:
            out = (br, bi) if out is None else (out[0] * br - out[1] * bi, out[0] * bi + out[1] * br)
        br, bi = br * br - bi * bi, 2.0 * br * bi
        n >>= 1
    return out


def _seg_order_rows(dst_ref, src, stage, nrows):
    seg = nrows // SUBLANES
    nl = stage.shape[0]
    for j in range(nl):
        stage[j] = src[:, j * LANES:(j + 1) * LANES]

    def step(i, _):
        rows = pl.ds(pl.multiple_of(i * SUBLANES, SUBLANES), SUBLANES)
        for j in range(nl):
            dst_ref[rows, j * LANES:(j + 1) * LANES] = stage[j, pl.ds(i, SUBLANES, stride=seg), :]
        return 0

    lax.fori_loop(0, seg, step, 0)


def _time_order_rows(src_ref, stage, nrows):
    seg = nrows // SUBLANES
    nl = stage.shape[0]

    def step(i, _):
        rows = pl.ds(pl.multiple_of(i * SUBLANES, SUBLANES), SUBLANES)
        for j in range(nl):
            stage[j, pl.ds(i, SUBLANES, stride=seg), :] = src_ref[rows, j * LANES:(j + 1) * LANES]
        return 0

    lax.fori_loop(0, seg, step, 0)
    return jnp.concatenate([stage[j] for j in range(nl)], axis=1)


def _seg_scan(xr_ref, xi_ref, abr_ref, abi_ref, coef_ref, car_ref, cai_ref, *, nrows, ns, reverse,
              cmat=None, dab=None):
    seg = nrows // SUBLANES
    cw = min(SCAN_LANES, ns)
    row = lax.broadcasted_iota(jnp.int32, (SUBLANES, cw), 0)
    first, last = (SUBLANES - 1, 0) if reverse else (0, SUBLANES - 1)

    def tile(i):
        return pl.ds(pl.multiple_of(((seg - 1 - i) if reverse else i) * SUBLANES, SUBLANES), SUBLANES)

    for cc in range(ns // cw):
        cols = pl.ds(cc * cw, cw)
        ar = jnp.broadcast_to(abr_ref[:, cols], (SUBLANES, cw))
        ai = jnp.broadcast_to(abi_ref[:, cols], (SUBLANES, cw))
        if reverse:
            ai = -ai

        def local(i, x, cols=cols, ar=ar, ai=ai):
            rows = tile(i)
            nr = ar * x[0] - ai * x[1] + xr_ref[rows, cols]
            ni = ar * x[1] + ai * x[0] + xi_ref[rows, cols]
            xr_ref[rows, cols] = nr
            xi_ref[rows, cols] = ni
            return nr, ni

        zero = jnp.zeros((SUBLANES, cw), F32)
        er, ei = lax.fori_loop(0, seg, local, (zero, zero))

        co = [coef_ref[k, :, cols] for k in range(8)]
        for lvl, d in enumerate((1, 2, 4)):
            kr, ki = co[2 * lvl], co[2 * lvl + 1]
            sh = SUBLANES - d if reverse else d
            sr, si = pltpu.roll(er, sh, 0), pltpu.roll(ei, sh, 0)
            er, ei = er + (kr * sr - ki * si), ei + (kr * si + ki * sr)
        c0r, c0i = car_ref[:, cols], cai_ref[:, cols]
        er, ei = er + (co[6] * c0r - co[7] * c0i), ei + (co[6] * c0i + co[7] * c0r)
        nb_shift = SUBLANES - 1 if reverse else 1
        cmr = jnp.where(row == first, c0r, pltpu.roll(er, nb_shift, 0))
        cmi = jnp.where(row == first, c0i, pltpu.roll(ei, nb_shift, 0))
        car_ref[:, cols] = jnp.broadcast_to(er[last:last + 1, :], er.shape)
        cai_ref[:, cols] = jnp.broadcast_to(ei[last:last + 1, :], ei.shape)
        if cmat is not None:
            cmat[0][:, cols] = cmr
            cmat[1][:, cols] = cmi

        w0 = (ar * cmr - ai * cmi, ar * cmi + ai * cmr)
        if dab is None:
            def fix(i, w, cols=cols, ar=ar, ai=ai):
                rows = tile(i)
                xr_ref[rows, cols] = xr_ref[rows, cols] + w[0]
                xi_ref[rows, cols] = xi_ref[rows, cols] + w[1]
                return ar * w[0] - ai * w[1], ar * w[1] + ai * w[0]

            lax.fori_loop(0, seg, fix, w0)
        else:
            s_re, s_im, e_re, e_im, o_re, o_im = dab

            def add(rows, w, pr, pi, acc):
                gr = xr_ref[rows, cols] + w[0]
                gi = xi_ref[rows, cols] + w[1]
                xr_ref[rows, cols] = gr
                xi_ref[rows, cols] = gi
                return acc[0] + (gr * pr + gi * pi), acc[1] + (gi * pr - gr * pi)

            def fix(i, st, cols=cols, ar=ar, ai=ai):
                w, acc = st[:2], st[2:]
                rows = tile(i)
                before = pl.ds(pl.multiple_of((seg - 2 - i) * SUBLANES, SUBLANES), SUBLANES)
                acc = add(rows, w, s_re[before, cols], s_im[before, cols], acc)
                return (ar * w[0] - ai * w[1], ar * w[1] + ai * w[0]) + acc

            st = lax.fori_loop(0, seg - 1, fix, w0 + (zero, zero))
            acc = add(pl.ds(0, SUBLANES), st[:2], e_re[:, cols], e_im[:, cols], st[2:])
            o_re[:, cols] += jnp.sum(acc[0], axis=0, keepdims=True)
            o_im[:, cols] += jnp.sum(acc[1], axis=0, keepdims=True)


def _hosted(core, comm, grid, n_in, n_out, n_scratch):
    ci = len(comm["ins"]) if comm else 0
    co = len(comm["out_shape"]) if comm else 0

    def body(*refs):
        ins, rest = refs[:n_in + ci], refs[n_in + ci:]
        outs, scr = rest[:n_out + co], rest[n_out + co:]
        hooks = functools.partial(_comm_hooks, comm, grid, ins[n_in:], outs[n_out:], scr[n_scratch:])
        hooks(before=True)
        core(*ins[:n_in], *outs[:n_out], *scr[:n_scratch])
        hooks(before=False)

    extra = dict(ins=list(comm["ins"]) if comm else [], in_specs=[ANY] * ci, out_specs=[ANY] * co,
                 out_shape=list(comm["out_shape"]) if comm else [], scratch=list(comm["scratch"]) if comm else [])
    return body, extra


def _ssm_fwd(proj, bdr, bdi, cdr, cdi, abr, abi, dsk, P, tb, comm=None):
    T = proj.shape[0]
    ntl, ct, st = bdr.shape
    ns = ntl * st
    nb = T // tb

    def core(u_ref, bdr_ref, bdi_ref, cdr_ref, cdi_ref, abr_ref, abi_ref, d_ref,
             y_ref, ge_ref, bsr_ref, bsi_ref, sr, si, coef, car, cai, up, stage):
        @pl.when(pl.program_id(0) == 0)
        def _():
            seg_pow = _cpow(abr_ref[...], abi_ref[...], tb // SUBLANES)
            for k, tile in enumerate(_coef_tiles(seg_pow[0], seg_pow[1], False)):
                coef[k] = tile
            car[...] = jnp.zeros_like(car)
            cai[...] = jnp.zeros_like(cai)

        bsr_ref[...] = car[...]
        bsi_ref[...] = cai[...]
        _seg_order_rows(up, u_ref[...], stage, tb)
        u = up[...]
        ub = u.astype(BF16)
        for s in range(ntl):
            us = ub[:, s * ct:(s + 1) * ct]
            sr[:, s * st:(s + 1) * st] = jnp.dot(us, bdr_ref[s], preferred_element_type=F32)
            si[:, s * st:(s + 1) * st] = jnp.dot(us, bdi_ref[s], preferred_element_type=F32)
        _seg_scan(sr, si, abr_ref, abi_ref, coef, car, cai, nrows=tb, ns=ns, reverse=False)
        for s in range(ntl):
            s_re = sr[:, s * st:(s + 1) * st].astype(BF16)
            s_im = si[:, s * st:(s + 1) * st].astype(BF16)
            up[:, s * ct:(s + 1) * ct] = (jnp.dot(s_re, cdr_ref[s], preferred_element_type=F32)
                                          - jnp.dot(s_im, cdi_ref[s], preferred_element_type=F32)
                                          + d_ref[:, s * ct:(s + 1) * ct] * u[:, s * ct:(s + 1) * ct])
        y = _time_order_rows(up, stage, tb)
        y_ref[...] = y
        ge_ref[...] = _gelu(y).astype(BF16)

    full3 = lambda a: _bs(a.shape, lambda i: (0, 0, 0))
    vec = lambda n: _bs((1, n), lambda i: (0, 0))
    row = _bs((tb, P), lambda i: (i, 0))
    st_spec = _bs((None, SUBLANES, ns), lambda i: (i, 0, 0))
    body, extra = _hosted(core, comm, (nb,), 8, 4, 7)
    return _call(body, grid=(nb,),
                 in_specs=[_bs((tb, P), lambda i: (i, 2)), full3(bdr), full3(bdi), full3(cdr), full3(cdi),
                           vec(ns), vec(ns), vec(P)] + extra["in_specs"],
                 out_specs=[row, row, st_spec, st_spec] + extra["out_specs"],
                 out_shape=[jax.ShapeDtypeStruct((T, P), F32), jax.ShapeDtypeStruct((T, P), BF16),
                            jax.ShapeDtypeStruct((nb, SUBLANES, ns), F32),
                            jax.ShapeDtypeStruct((nb, SUBLANES, ns), F32)] + extra["out_shape"],
                 scratch_shapes=[pltpu.VMEM((tb, ns), F32), pltpu.VMEM((tb, ns), F32),
                                 pltpu.VMEM((8, SUBLANES, ns), F32),
                                 pltpu.VMEM((SUBLANES, ns), F32), pltpu.VMEM((SUBLANES, ns), F32),
                                 pltpu.VMEM((tb, P), F32), pltpu.VMEM((P // LANES, tb, LANES), F32)]
                 + extra["scratch"],
                 compiler_params=_cp(("arbitrary",)), name="ssm_fwd")(
                     proj, bdr, bdi, cdr, cdi, abr, abi, dsk, *extra["ins"])


def _ssm_bwd(proj, y, dge, bsr, bsi, bdr, bdi, cdr, cdi, abr, abi, dsk, P, tb, comm=None):
    T = proj.shape[0]
    ntl, ct, st = bdr.shape
    ns = ntl * st
    nb = T // tb

    def core(u_ref, y_ref, dge_ref, bsr_ref, bsi_ref, abr_ref, abi_ref, d_ref, bdr_h, bdi_h, cdr_h, cdi_h,
             du_ref, dabr_ref, dabi_ref, dd_ref, dbdr_h, dbdi_h, dcdr_h, dcdi_h,
             wbdr, wbdi, wcdr, wcdi, abdr, abdi, acdr, acdi, spr, spi, gr, gi, coef_f, coef_r,
             car, cai, gcr, gci, ser, sei, up, dyp, dup, stage):
        i = pl.program_id(0)

        @pl.when(i == 0)
        def _():
            for h, w in ((bdr_h, wbdr), (bdi_h, wbdi), (cdr_h, wcdr), (cdi_h, wcdi)):
                pltpu.sync_copy(h, w)
            for a in (abdr, abdi, acdr, acdi, gcr, gci):
                a[...] = jnp.zeros_like(a)
            for o in (dabr_ref, dabi_ref, dd_ref):
                o[...] = jnp.zeros_like(o)
            seg_pow = _cpow(abr_ref[...], abi_ref[...], tb // SUBLANES)
            for k, tile in enumerate(_coef_tiles(seg_pow[0], seg_pow[1], False)):
                coef_f[k] = tile
            for k, tile in enumerate(_coef_tiles(seg_pow[0], seg_pow[1], True)):
                coef_r[k] = tile

        car[...] = bsr_ref[...]
        cai[...] = bsi_ref[...]
        _seg_order_rows(up, u_ref[...], stage, tb)
        _seg_order_rows(dyp, dge_ref[...] * _gelu_grad(y_ref[...]), stage, tb)
        u = up[...]
        ub = u.astype(BF16)
        for s in range(ntl):
            us = ub[:, s * ct:(s + 1) * ct]
            spr[:, s * st:(s + 1) * st] = jnp.dot(us, wbdr[s], preferred_element_type=F32)
            spi[:, s * st:(s + 1) * st] = jnp.dot(us, wbdi[s], preferred_element_type=F32)
        _seg_scan(spr, spi, abr_ref, abi_ref, coef_f, car, cai, nrows=tb, ns=ns, reverse=False, cmat=(ser, sei))

        dy = dyp[...]
        dyb = dy.astype(BF16)
        for s in range(ntl):
            dys = dyb[:, s * ct:(s + 1) * ct]
            gr[:, s * st:(s + 1) * st] = lax.dot_general(dys, wcdr[s], (NT, ((), ())), preferred_element_type=F32)
            gi[:, s * st:(s + 1) * st] = -lax.dot_general(dys, wcdi[s], (NT, ((), ())), preferred_element_type=F32)
        _seg_scan(gr, gi, abr_ref, abi_ref, coef_r, gcr, gci, nrows=tb, ns=ns, reverse=True,
                  dab=(spr, spi, ser, sei, dabr_ref, dabi_ref))

        for s in range(ntl):
            sl_c, sl_s = slice(s * ct, (s + 1) * ct), slice(s * st, (s + 1) * st)
            s_re = spr[:, sl_s].astype(BF16)
            s_im = spi[:, sl_s].astype(BF16)
            g_re, g_im = gr[:, sl_s].astype(BF16), gi[:, sl_s].astype(BF16)
            dys, us = dyb[:, sl_c], ub[:, sl_c]
            acdr[s] += lax.dot_general(s_re, dys, (TN, ((), ())), preferred_element_type=F32)
            acdi[s] -= lax.dot_general(s_im, dys, (TN, ((), ())), preferred_element_type=F32)
            abdr[s] += lax.dot_general(us, g_re, (TN, ((), ())), preferred_element_type=F32)
            abdi[s] += lax.dot_general(us, g_im, (TN, ((), ())), preferred_element_type=F32)
            dup[:, sl_c] = (lax.dot_general(g_re, wbdr[s], (NT, ((), ())), preferred_element_type=F32)
                            + lax.dot_general(g_im, wbdi[s], (NT, ((), ())), preferred_element_type=F32)
                            + d_ref[:, sl_c] * dy[:, sl_c])
        dd_ref[...] += jnp.sum(dy * u, axis=0, keepdims=True)
        du_ref[...] = _time_order_rows(dup, stage, tb).astype(BF16)

        @pl.when(i == nb - 1)
        def _():
            for a, h in ((abdr, dbdr_h), (abdi, dbdi_h), (acdr, dcdr_h), (acdi, dcdi_h)):
                pltpu.sync_copy(a, h)

    rev = lambda i: nb - 1 - i
    vec = lambda n: _bs((1, n), lambda i: (0, 0))
    row = _bs((tb, P), lambda i: (rev(i), 0))
    st_spec = _bs((None, SUBLANES, ns), lambda i: (rev(i), 0, 0))
    bshape = jax.ShapeDtypeStruct(bdr.shape, F32)
    cshape = jax.ShapeDtypeStruct(cdr.shape, F32)
    body, extra = _hosted(core, comm, (nb,), 12, 8, 24)
    return _call(body, grid=(nb,),
                 in_specs=[_bs((tb, P), lambda i: (rev(i), 2)), row, row, st_spec, st_spec,
                           vec(ns), vec(ns), vec(P), ANY, ANY, ANY, ANY] + extra["in_specs"],
                 out_specs=[row, vec(ns), vec(ns), vec(P), ANY, ANY, ANY, ANY] + extra["out_specs"],
                 out_shape=[jax.ShapeDtypeStruct((T, P), BF16), jax.ShapeDtypeStruct((1, ns), F32),
                            jax.ShapeDtypeStruct((1, ns), F32), jax.ShapeDtypeStruct((1, P), F32),
                            bshape, bshape, cshape, cshape] + extra["out_shape"],
                 scratch_shapes=[pltpu.VMEM(bdr.shape, BF16), pltpu.VMEM(bdr.shape, BF16),
                                 pltpu.VMEM(cdr.shape, BF16), pltpu.VMEM(cdr.shape, BF16),
                                 pltpu.VMEM(bdr.shape, F32), pltpu.VMEM(bdr.shape, F32),
                                 pltpu.VMEM(cdr.shape, F32), pltpu.VMEM(cdr.shape, F32),
                                 pltpu.VMEM((tb, ns), F32), pltpu.VMEM((tb, ns), F32),
                                 pltpu.VMEM((tb, ns), F32), pltpu.VMEM((tb, ns), F32),
                                 pltpu.VMEM((8, SUBLANES, ns), F32), pltpu.VMEM((8, SUBLANES, ns), F32)]
                 + [pltpu.VMEM((SUBLANES, ns), F32)] * 6 + [pltpu.VMEM((tb, P), F32)] * 3
                 + [pltpu.VMEM((P // LANES, tb, LANES), F32)] + extra["scratch"],
                 compiler_params=_cp(("arbitrary",)), name="ssm_bwd")(
                     proj, y, dge, bsr, bsi, abr, abi, dsk, bdr, bdi, cdr, cdi, *extra["ins"])


def _adamw(w, g, m, v, name):
    R, C = w.shape
    tr = _t(R, 256)

    def body(w_ref, g_ref, m_ref, v_ref, d_ref, mo_ref, vo_ref):
        gv = g_ref[...]
        mn = ADAM_B1 * m_ref[...] + (1.0 - ADAM_B1) * gv
        vn = ADAM_B2 * v_ref[...] + (1.0 - ADAM_B2) * (gv * gv)
        m_hat = mn / (1.0 - ADAM_B1 ** ADAM_STEP)
        v_hat = vn / (1.0 - ADAM_B2 ** ADAM_STEP)
        d_ref[...] = -ADAM_LR * (m_hat / (jnp.sqrt(v_hat) + ADAM_EPS) + ADAM_WD * w_ref[...])
        mo_ref[...] = mn
        vo_ref[...] = vn

    blk = _bs((tr, C), lambda i: (i, 0))
    shp = jax.ShapeDtypeStruct((R, C), F32)
    return _call(body, grid=(R // tr,), in_specs=[blk] * 4, out_specs=[blk] * 3, out_shape=[shp] * 3,
                 compiler_params=_cp(("parallel",)), name=name)(w, g, m, v)


def _sum_cast(grad, got, place, name):
    J, H, C = got.shape
    tr = _t(H, 256)
    nb = H // tr

    def body(pl_ref, a_ref, b_ref, o_ref):
        o_ref[...] = (a_ref[...] + b_ref[...]).astype(BF16)

    blk = _bs((None, tr, C), lambda j, i, pc: (j, i, 0))
    mine = _bs((None, tr, C), lambda j, i, pc: (j, pc[1] * nb + i, 0))
    spec = pltpu.PrefetchScalarGridSpec(num_scalar_prefetch=1, grid=(J, nb), in_specs=[mine, blk], out_specs=blk)
    return _call(body, grid_spec=spec, out_shape=jax.ShapeDtypeStruct((J, H, C), BF16),
                 compiler_params=_cp(("parallel", "parallel")), name=name)(place, grad, got)


def _sum_chips(sent, arrived, place, name):
    J, H, C = arrived.shape
    tr = _t(H, 256)
    nb = H // tr

    def body(pl_ref, own_ref, a0_ref, a1_ref, a2_ref, o_ref):
        acc = own_ref[...].astype(F32)
        for r in (a0_ref, a1_ref, a2_ref):
            acc = acc + r[...].astype(F32)
        o_ref[...] = acc

    def other(k):
        return _bs((None, tr, C), lambda i, pc: (jnp.where(pc[0] <= k, k + 1, k), i, 0))

    spec = pltpu.PrefetchScalarGridSpec(
        num_scalar_prefetch=1, grid=(nb,),
        in_specs=[_bs((None, tr, C), lambda i, pc: (pc[0], i, 0)), other(0), other(1), other(2)],
        out_specs=_bs((tr, C), lambda i, pc: (pc[1] * nb + i, 0)))
    return _call(body, grid_spec=spec, out_shape=jax.ShapeDtypeStruct((2 * H, C), F32),
                 compiler_params=_cp(("parallel",)), name=name)(place, sent, arrived, arrived, arrived)


def _place():
    x, y, c = lax.axis_index("x"), lax.axis_index("y"), lax.axis_index("c")
    chips = [(1 - x, y), (x, 1 - y), (1 - x, 1 - y)]
    return x, y, c, chips


def _split(nrows, row_bytes, align, cap=None):
    k = max(1, min(cap or DMA_MAX_CHUNKS, (nrows * row_bytes) // DMA_CHUNK_BYTES))
    while k > 1 and nrows % (k * align):
        k -= 1
    return k


def _comm_call(plan, name):
    n_in, n_out = len(plan["ins"]), len(plan["out_shape"])

    def body(*refs):
        for phase in plan["phases"]:
            phase(refs[:n_in], refs[n_in:n_in + n_out], refs[n_in + n_out:])

    return _call(body, in_specs=[ANY] * n_in, out_specs=[ANY] * n_out, out_shape=plan["out_shape"],
                 scratch_shapes=plan["scratch"], name=name)(*plan["ins"])


def _comm_hooks(plan, grid, ins, outs, sems, *, before):
    if plan is None:
        return
    nsteps, step = 1, 0
    for d, g in enumerate(grid):
        nsteps, step = nsteps * g, step * g + pl.program_id(d)
    for p, (phase, frac) in enumerate(zip(plan["phases"], plan["at"])):
        if (p == 0) == before:
            pl.when(step == int(frac * (nsteps - 1)))(functools.partial(phase, ins, outs, sems))


def _ag_plan(shards, axes):
    n = len(shards)
    shapes = [a.shape for a in shards]

    def window(ref, i, chip, half=None):
        S, ax = shapes[i], axes[i]
        idx = []
        for d in range(len(S)):
            off, size = 0, S[d]
            if d == 0 and half is not None:
                off, size = half * (S[0] // 2), S[0] // 2
            if d == ax:
                off = off + chip * S[ax]
            idx.append(pl.ds(off, size))
        return ref.at[tuple(idx)]

    def copies(src, full, sems):
        ssem, rsem = sems
        x, y, c, chips = _place()
        me = 2 * x + y
        sib = (x, y, 1 - c)
        idx = [2 * cx + cy for cx, cy in chips]

        def rcopy(i, k, s_ref, d_ref, to):
            return pltpu.make_async_remote_copy(src_ref=s_ref, dst_ref=d_ref, send_sem=ssem.at[i, k],
                                                recv_sem=rsem.at[i, k], device_id=to, device_id_type=MESH)

        def ici(i, j, incoming):
            half_src = src[i].at[pl.ds(c * (shapes[i][0] // 2), shapes[i][0] // 2)]
            return rcopy(i, j, half_src, window(full[i], i, idx[j] if incoming else me, c), (*chips[j], c))

        def fwd(i, j, half):
            w = window(full[i], i, idx[j], half)
            return rcopy(i, 3 + j, w, w, sib)

        def own(i):
            return rcopy(i, 6, src[i], window(full[i], i, me), sib)

        return c, ici, fwd, own

    def send(src, full, sems):
        c, ici, fwd, own = copies(src, full, sems)
        for i in range(n):
            for j in range(3):
                ici(i, j, False).start()
        for i in range(n):
            own(i).start()

    def forward(src, full, sems):
        c, ici, fwd, own = copies(src, full, sems)
        for i in range(n):
            for j in range(3):
                ici(i, j, True).wait_recv()
                fwd(i, j, c).start()

    def finish(src, full, sems):
        c, ici, fwd, own = copies(src, full, sems)
        for i in range(n):
            for j in range(3):
                fwd(i, j, 1 - c).wait_recv()
            own(i).wait()
        for i in range(n):
            for j in range(3):
                ici(i, j, False).wait_send()
                fwd(i, j, c).wait_send()

    out_shape = [jax.ShapeDtypeStruct(tuple(N_CHIP * d if k == ax else d for k, d in enumerate(S)), BF16)
                 for S, ax in zip(shapes, axes)]
    return dict(ins=list(shards), out_shape=out_shape, phases=[send, forward, finish], at=[0.0, 0.8, 1.0],
                scratch=[pltpu.SemaphoreType.DMA((n, 7)), pltpu.SemaphoreType.DMA((n, 7))])


def _halves_to_sibling(grads, name):
    n = len(grads)

    def body(*refs):
        g, got = refs[:n], refs[n:2 * n]
        ssem, rsem = refs[2 * n:]
        x, y, c, _ = _place()
        sib = (x, y, 1 - c)
        for i in range(n):
            J, R, C = g[i].shape
            H = R // 2
            k = _split(H, C * 4, SUBLANES, cap=DMA_MAX_CHUNKS // J)
            hr = H // k
            for j in range(J):
                for q in range(k):
                    other = pl.ds(pl.multiple_of((1 - c) * H + q * hr, SUBLANES), hr)
                    to = pl.ds(q * hr, hr)
                    pltpu.make_async_remote_copy(src_ref=g[i].at[j, other, :], dst_ref=got[i].at[j, to, :],
                                                 send_sem=ssem.at[i], recv_sem=rsem.at[i],
                                                 device_id=sib, device_id_type=MESH).start()
        for i in range(n):
            pltpu.make_async_remote_copy(src_ref=got[i], dst_ref=got[i], send_sem=ssem.at[i], recv_sem=rsem.at[i],
                                         device_id=sib, device_id_type=MESH).wait()

    half = [jax.ShapeDtypeStruct((a.shape[0], a.shape[1] // 2, a.shape[2]), a.dtype) for a in grads]
    return _call(body, in_specs=[ANY] * n, out_specs=[ANY] * n, out_shape=half,
                 scratch_shapes=[pltpu.SemaphoreType.DMA((n,)), pltpu.SemaphoreType.DMA((n,))],
                 name=name)(*grads)


def _scatter_plan(parts):
    n = len(parts)

    def peers():
        x, y, c, chips = _place()
        return 2 * x + y, c, chips, [2 * cx + cy for cx, cy in chips]

    def send(s, got, sems):
        ssem, rsem = sems
        me, c, chips, idx = peers()
        for i in range(n):
            _, H, C = s[i].shape
            k = _split(H, C * 2, 16, cap=RS_CHUNKS)
            hr = H // k
            for q in range(k):
                rows = pl.ds(q * hr, hr)
                for j in range(3):
                    pltpu.make_async_remote_copy(src_ref=s[i].at[idx[j], rows, :], dst_ref=got[i].at[me, rows, :],
                                                 send_sem=ssem.at[i, j], recv_sem=rsem.at[i, j],
                                                 device_id=(*chips[j], c), device_id_type=MESH).start()

    def finish(s, got, sems):
        ssem, rsem = sems
        me, c, chips, idx = peers()
        for i in range(n):
            for j in range(3):
                pltpu.make_async_remote_copy(src_ref=s[i].at[idx[j]], dst_ref=got[i].at[idx[j]],
                                             send_sem=ssem.at[i, j], recv_sem=rsem.at[i, j],
                                             device_id=(*chips[j], c), device_id_type=MESH).wait()

    return dict(ins=list(parts), out_shape=[jax.ShapeDtypeStruct(a.shape, a.dtype) for a in parts],
                phases=[send, finish], at=[0.0, 1.0],
                scratch=[pltpu.SemaphoreType.DMA((n, 3)), pltpu.SemaphoreType.DMA((n, 3))])


def _join_halves(shards):
    n = len(shards)

    def body(*refs):
        full = refs[n:2 * n]
        ssem, rsem = refs[2 * n:]
        x, y, c, _ = _place()
        sib = (x, y, 1 - c)
        for i in range(n):
            H, C = full[i].shape[0] // 2, full[i].shape[1]
            k = _split(H, C * 4, SUBLANES)
            hr = H // k
            for q in range(k):
                rows = pl.ds(pl.multiple_of(c * H + q * hr, SUBLANES), hr)
                pltpu.make_async_remote_copy(src_ref=full[i].at[rows], dst_ref=full[i].at[rows],
                                             send_sem=ssem.at[i], recv_sem=rsem.at[i],
                                             device_id=sib, device_id_type=MESH).start()
        for i in range(n):
            half = full[i].at[pl.ds(0, full[i].shape[0] // 2)]
            pltpu.make_async_remote_copy(src_ref=half, dst_ref=half, send_sem=ssem.at[i], recv_sem=rsem.at[i],
                                         device_id=sib, device_id_type=MESH).wait()

    shp = [jax.ShapeDtypeStruct(a.shape, a.dtype) for a in shards]
    return _call(body, in_specs=[ANY] * n, out_specs=[ANY] * n, out_shape=shp,
                 input_output_aliases={i: i for i in range(n)},
                 scratch_shapes=[pltpu.SemaphoreType.DMA((n,)), pltpu.SemaphoreType.DMA((n,))],
                 name="rs_join")(*shards)


def _allreduce_small(buf):
    R, L = buf.shape
    RB = R // N_DEV

    def body(x_ref, o_ref, got, ssem, rsem):
        x, y, c, _ = _place()
        me = 4 * x + 2 * y + c

        def dev(k):
            return (k // 4, (k // 2) % 2, k % 2)

        def slab(k):
            return pl.ds(pl.multiple_of(k * RB, SUBLANES), RB)

        sends = []
        for d in range(1, N_DEV):
            peer = (me + d) % N_DEV
            cp = pltpu.make_async_remote_copy(src_ref=x_ref.at[slab(peer)], dst_ref=got.at[me],
                                              send_sem=ssem.at[0, d], recv_sem=rsem.at[0, d],
                                              device_id=dev(peer), device_id_type=MESH)
            cp.start()
            sends.append(cp)
        got[me] = x_ref[slab(me), :]
        for d in range(1, N_DEV):
            src = (me + N_DEV - d) % N_DEV
            pltpu.make_async_remote_copy(src_ref=x_ref.at[slab(me)], dst_ref=got.at[src],
                                         send_sem=ssem.at[0, d], recv_sem=rsem.at[0, d],
                                         device_id=dev(src), device_id_type=MESH).wait_recv()
        acc = got[0]
        for k in range(1, N_DEV):
            acc = acc + got[k]
        o_ref[slab(me), :] = acc
        for d in range(1, N_DEV):
            peer = (me + d) % N_DEV
            cp = pltpu.make_async_remote_copy(src_ref=o_ref.at[slab(me)], dst_ref=o_ref.at[slab(me)],
                                              send_sem=ssem.at[1, d], recv_sem=rsem.at[1, d],
                                              device_id=dev(peer), device_id_type=MESH)
            cp.start()
            sends.append(cp)
        for d in range(1, N_DEV):
            src = (me + N_DEV - d) % N_DEV
            pltpu.make_async_remote_copy(src_ref=o_ref.at[slab(src)], dst_ref=o_ref.at[slab(src)],
                                         send_sem=ssem.at[1, d], recv_sem=rsem.at[1, d],
                                         device_id=dev(src), device_id_type=MESH).wait_recv()
        for cp in sends:
            cp.wait_send()

    return _call(body, in_specs=[VMEM_FULL], out_specs=VMEM_FULL, out_shape=jax.ShapeDtypeStruct((R, L), F32),
                 scratch_shapes=[pltpu.VMEM((N_DEV, RB, L), F32), pltpu.SemaphoreType.DMA((2, N_DEV)),
                                 pltpu.SemaphoreType.DMA((2, N_DEV))],
                 name="allreduce_small")(buf)


def _block_diag(t, gt):
    G, A, B = t.shape
    t4 = t.reshape(G // gt, gt, A, B)
    eye = jnp.eye(gt, dtype=t.dtype)
    return jnp.einsum('sgab,gh->sgahb', t4, eye).reshape(G // gt, gt * A, gt * B)


def _block_diag_extract(m, gt, A, B):
    S = m.shape[0]
    m5 = m.reshape(S, gt, A, gt, B)
    eye = jnp.eye(gt, dtype=m.dtype)
    return jnp.einsum('sgahb,gh->sgab', m5, eye).reshape(S * gt, A, B)


def _pack_small(arrs, rows):
    flat = jnp.concatenate([a.reshape(-1).astype(F32) for a in arrs])
    return jnp.pad(flat, (0, rows * LANES - flat.shape[0])).reshape(rows, LANES)


def _unpack_small(buf, shapes):
    flat = buf.reshape(-1)
    out, off = [], 0
    for s in shapes:
        n = 1
        for d in s:
            n *= d
        out.append(flat[off:off + n].reshape(s))
        off += n
    return out


def kernel(x, p, norm_gain, w_in, w_pool, pool_scale, a_re, a_im, log_dt, b_re, b_im, c_re, c_im, d_skip, w_glu, w_out, w_ple, w_ple_gate, final_gain, loss_target, m_norm_gain, m_w_in, m_w_pool, m_pool_scale, m_a_re, m_a_im, m_log_dt, m_b_re, m_b_im, m_c_re, m_c_im, m_d_skip, m_w_glu, m_w_out, m_w_ple, m_w_ple_gate, m_final_gain, v_norm_gain, v_w_in, v_w_pool, v_pool_scale, v_a_re, v_a_im, v_log_dt, v_b_re, v_b_im, v_c_re, v_c_im, v_d_skip, v_w_glu, v_w_out, v_w_ple, v_w_ple_gate, v_final_gain):
    xs, pe, tgt = x[0], p[0, 0], loss_target[0]
    T, D = xs.shape
    E = pe.shape[1]
    P = D // 2
    NG = len(POOL_WINDOWS)
    PG = P // NG
    G, N, C = P // SSM_GROUP, SSM_STATE, SSM_GROUP
    GT = min(SSM_TILE_GROUPS, G)
    Q = D // N_CHIP

    big = {"w_in": (w_in, m_w_in, v_w_in), "w_pool": (w_pool, m_w_pool, v_w_pool),
           "w_glu": (w_glu, m_w_glu, v_w_glu), "w_out": (w_out, m_w_out, v_w_out),
           "w_ple": (w_ple, m_w_ple, v_w_ple), "w_ple_gate": (w_ple_gate, m_w_ple_gate, v_w_ple_gate)}
    big_names = list(big)
    shard2d = {n: (big[n][0].size // big[n][0].shape[-1], big[n][0].shape[-1]) for n in big_names}
    shard_axis = {"w_in": 1, "w_pool": 1, "w_glu": 1, "w_out": 0, "w_ple": 1, "w_ple_gate": 0}
    shard16 = {n: big[n][0][0].astype(BF16) for n in big_names}
    place = jnp.stack([2 * lax.axis_index("x") + lax.axis_index("y"), lax.axis_index("c")]).astype(jnp.int32)
    win, = _comm_call(_ag_plan([shard16["w_in"]], [shard_axis["w_in"]]), "ag_w_in")
    later = [n for n in big_names if n != "w_in"]
    ag_later = _ag_plan([shard16[n] for n in later], [shard_axis[n] for n in later])

    rep = lambda a: jnp.repeat(a, C, axis=0)
    a_re_r, a_im_r = rep(a_re[0]), rep(a_im[0])
    ldt_r = rep(jnp.broadcast_to(log_dt[0][:, None], (G, N)))
    bt_re = b_re[0].transpose(0, 2, 1).reshape(G * C, N)
    bt_im = b_im[0].transpose(0, 2, 1).reshape(G * C, N)
    ab_re_r, ab_im_r, bbt_re, bbt_im = _ssm_prep(a_re_r, a_im_r, ldt_r, bt_re, bt_im)
    abr = ab_re_r[::C].reshape(1, G * N)
    abi = ab_im_r[::C].reshape(1, G * N)
    bdr = _block_diag(bbt_re.reshape(G, C, N), GT).astype(BF16)
    bdi = _block_diag(bbt_im.reshape(G, C, N), GT).astype(BF16)
    cdr = _block_diag(c_re[0].transpose(0, 2, 1), GT).astype(BF16)
    cdi = _block_diag(c_im[0].transpose(0, 2, 1), GT).astype(BF16)

    tb = _t(T, 256)
    tbs = _t(T, 256)
    tm = _t(T, 1024)
    tk = _t(T, 2048)
    DH = _t(D, 1024)
    row_k = lambda i, n, k: (i, k)
    row_n = lambda i, n, k: (i, n)
    f32 = lambda *shape: jax.ShapeDtypeStruct(shape, F32)
    hn = _norm1(xs, norm_gain, tb)
    proj = _mm(hn, win, dims=NN, grid=(T // tm, N_CHIP, 1),
               a_spec=_bs((tm, D), row_k), b_spec=_bs((D, P), lambda i, n, k: (k, n)),
               o_spec=_bs((tm, P), row_n), out_shape=f32(T, 4 * P), name="mm_proj")
    y, ge, bsr, bsi, wp, wglu, wout, wple, wpg = _ssm_fwd(proj, bdr, bdi, cdr, cdi, abr, abi, d_skip, P, tbs,
                                                          comm=ag_later)
    pooled = _pool_fwd(proj, P, tb)
    mixed = _mm(pooled, wp, dims=NN, grid=(T // tm, NG, 1),
                a_spec=_bs((tm, PG), row_n), b_spec=_bs((None, PG, PG), lambda i, g, k: (g, 0, 0)),
                o_spec=_bs((tm, PG), row_n), out_shape=f32(T, P), name="mm_pool")
    hg = _mm(ge, wglu, dims=NN, grid=(T // tm, 2 * P // DH, 1),
             a_spec=_bs((tm, P), row_k), b_spec=_bs((P, DH), lambda i, n, k: (k, n)),
             o_spec=_bs((tm, DH), row_n), out_shape=f32(T, 2 * P), name="mm_glu")
    cat = _gate_fwd(mixed, proj, hg, pool_scale, tb)
    h1, h1b = _mm(cat, wout, dims=NN, grid=(T // tm, D // DH, 1), res=xs, bf16_copy=True,
                  a_spec=_bs((tm, D), row_k), b_spec=_bs((D, DH), lambda i, n, k: (k, n)),
                  r_spec=_bs((tm, DH), row_n), o_spec=_bs((tm, DH), row_n), out_shape=f32(T, D), name="mm_out")
    e = _mm(pe, wple, dims=NN, grid=(T // tm, D // DH, 1),
            a_spec=_bs((tm, E), row_k), b_spec=_bs((E, DH), lambda i, n, k: (k, n)),
            o_spec=_bs((tm, DH), row_n), out_shape=f32(T, D), name="mm_ple")
    z = _mm(h1b, wpg, dims=NN, grid=(T // tm, D // DH, 1),
            a_spec=_bs((tm, D), row_k), b_spec=_bs((D, DH), lambda i, n, k: (k, n)),
            o_spec=_bs((tm, DH), row_n), out_shape=f32(T, D), name="mm_pgate")
    dh2, de, dz, dg2, lpart = _final_fb(h1, e, z, tgt, final_gain.reshape(1, D), tb)

    col_m = lambda m, n, k: (k, m)
    col_n = lambda m, n, k: (k, n)
    dh1, dh1b = _mm(dz, wpg, dims=NT, grid=(T // tm, D // DH, 1), res=dh2, bf16_copy=True,
                    a_spec=_bs((tm, D), row_k), b_spec=_bs((DH, D), lambda i, n, k: (n, k)),
                    r_spec=_bs((tm, DH), row_n), o_spec=_bs((tm, DH), row_n), out_shape=f32(T, D), name="mm_dh1")
    g_wpg = _mm(h1b, dz, dims=TN, grid=(D // DH, D // DH, T // tk),
                a_spec=_bs((tk, DH), col_m), b_spec=_bs((tk, DH), col_n),
                o_spec=_bs((DH, DH), lambda m, n, k: (m, n)), out_shape=f32(D, D), name="mm_gwpg")
    g_wple = _mm(pe, de, dims=TN, grid=(1, N_CHIP, T // tk),
                 a_spec=_bs((tk, E), col_m), b_spec=_bs((tk, Q), col_n),
                 o_spec=_bs((None, E, Q), lambda m, j, k: (j, 0, 0)), out_shape=f32(N_CHIP, E, Q), name="mm_gwple")
    dcat = _mm(dh1b, wout, dims=NT, grid=(T // tm, D // DH, 1),
               a_spec=_bs((tm, D), row_k), b_spec=_bs((DH, D), lambda i, n, k: (n, k)),
               o_spec=_bs((tm, DH), row_n), out_shape=f32(T, D), name="mm_dcat")
    g_wout = _mm(cat, dh1b, dims=TN, grid=(D // DH, D // DH, T // tk),
                 a_spec=_bs((tk, DH), col_m), b_spec=_bs((tk, DH), col_n),
                 o_spec=_bs((DH, DH), lambda m, n, k: (m, n)), out_shape=f32(D, D), name="mm_gwout")
    dmixed, dpg, dsg, dhg, dps = _gate_bwd(dcat, mixed, proj, hg, pool_scale, tb)
    dge = _mm(dhg, wglu, dims=NT, grid=(T // tm, 1, 1),
              a_spec=_bs((tm, 2 * P), row_k), b_spec=_bs((P, 2 * P), lambda i, n, k: (n, k)),
              o_spec=_bs((tm, P), row_n), out_shape=f32(T, P), name="mm_dge")
    g_wglu = _mm(ge, dhg, dims=TN, grid=(1, N_CHIP, T // tk),
                 a_spec=_bs((tk, P), col_m), b_spec=_bs((tk, Q), col_n),
                 o_spec=_bs((None, P, Q), lambda m, j, k: (j, 0, 0)), out_shape=f32(N_CHIP, P, Q), name="mm_gwglu")
    dpooled = _mm(dmixed, wp, dims=NT, grid=(T // tm, NG, 1),
                  a_spec=_bs((tm, PG), row_n), b_spec=_bs((None, PG, PG), lambda i, g, k: (g, 0, 0)),
                  o_spec=_bs((tm, PG), row_n), out_shape=f32(T, P), name="mm_dpooled")
    g_wp = _mm(pooled, dmixed, dims=TN, grid=(NG, 1, T // tk),
               a_spec=_bs((tk, PG), col_m), b_spec=_bs((tk, PG), col_m),
               o_spec=_bs((None, PG, PG), lambda g, n, k: (g, 0, 0)), out_shape=f32(NG, PG, PG), name="mm_gwp")
    dpi = _pool_bwd(dpooled, tb)
    gbig = {"w_pool": g_wp.reshape(NG, N_CHIP, PG // N_CHIP, PG).transpose(1, 0, 2, 3).reshape(
                N_CHIP, NG * PG // N_CHIP, PG),
            "w_glu": g_wglu, "w_out": g_wout.reshape(N_CHIP, Q, D), "w_ple": g_wple,
            "w_ple_gate": g_wpg.reshape(N_CHIP, Q, D)}
    early = list(gbig)
    got = _halves_to_sibling([gbig[n] for n in early], "rs_halves_early")
    chip_sums = {n: _sum_cast(gbig[n], g_, place, "sum_cast_" + n) for g_, n in zip(got, early)}
    res = _ssm_bwd(proj, y, dge, bsr, bsi, bdr, bdi, cdr, cdi, abr, abi, d_skip, P, tbs,
                   comm=_scatter_plan([chip_sums[n] for n in early]))
    du, dabr, dabi, dd, dbdr, dbdi, dcdr, dcdi = res[:8]
    arrived = dict(zip(early, res[8:]))
    dproj = jnp.concatenate([dpi, dpg, du, dsg], axis=1)
    gbig["w_in"] = _mm(hn, dproj, dims=TN, grid=(D // DH, N_CHIP, T // tk),
                       a_spec=_bs((tk, DH), col_m), b_spec=_bs((tk, P), col_n),
                       o_spec=_bs((None, DH, P), lambda m, j, k: (j, m, 0)), out_shape=f32(N_CHIP, D, P),
                       name="mm_gwin")
    got, = _halves_to_sibling([gbig["w_in"]], "rs_halves_late")
    chip_sums["w_in"] = _sum_cast(gbig["w_in"], got, place, "sum_cast_w_in")
    KH = _t(4 * P, 2048)
    dhn, arrived["w_in"] = _mm(dproj, win, dims=NT, grid=(T // tm, D // DH, 4 * P // KH),
                               a_spec=_bs((tm, KH), row_k), b_spec=_bs((DH, KH), lambda i, n, k: (n, k)),
                               o_spec=_bs((tm, DH), row_n), out_shape=f32(T, D), name="mm_dhn",
                               comm=_scatter_plan([chip_sums["w_in"]]))
    grad_x, dg1 = _norm1_bwd(xs, dhn, dh1, norm_gain, tb)

    dbbt_re = _block_diag_extract(dbdr, GT, C, N).reshape(G * C, N)
    dbbt_im = _block_diag_extract(dbdi, GT, C, N).reshape(G * C, N)
    g_c_re = _block_diag_extract(dcdr, GT, N, C).transpose(0, 2, 1)
    g_c_im = _block_diag_extract(dcdi, GT, N, C).transpose(0, 2, 1)
    dab_re_r = rep(dabr.reshape(G, N)) * (1.0 / C)
    dab_im_r = rep(dabi.reshape(G, N)) * (1.0 / C)
    g_a_re, g_a_im, g_ldt, g_bt_re, g_bt_im = _ssm_prep_bwd(a_re_r, a_im_r, ldt_r, bt_re, bt_im,
                                                            dab_re_r, dab_im_r, dbbt_re, dbbt_im, G)
    g_b_re = g_bt_re.reshape(G, C, N).transpose(0, 2, 1)
    g_b_im = g_bt_im.reshape(G, C, N).transpose(0, 2, 1)

    halves = [_sum_chips(chip_sums[n], arrived[n], place, "sum_chips_" + n) for n in big_names]
    gshard = _join_halves(halves)

    small_names = ["norm_gain", "pool_scale", "a_re", "a_im", "log_dt", "b_re", "b_im", "c_re", "c_im",
                   "d_skip", "final_gain"]
    small_w = dict(norm_gain=norm_gain, pool_scale=pool_scale, a_re=a_re, a_im=a_im, log_dt=log_dt, b_re=b_re,
                   b_im=b_im, c_re=c_re, c_im=c_im, d_skip=d_skip, final_gain=final_gain)
    small_m = dict(norm_gain=m_norm_gain, pool_scale=m_pool_scale, a_re=m_a_re, a_im=m_a_im, log_dt=m_log_dt,
                   b_re=m_b_re, b_im=m_b_im, c_re=m_c_re, c_im=m_c_im, d_skip=m_d_skip, final_gain=m_final_gain)
    small_v = dict(norm_gain=v_norm_gain, pool_scale=v_pool_scale, a_re=v_a_re, a_im=v_a_im, log_dt=v_log_dt,
                   b_re=v_b_re, b_im=v_b_im, c_re=v_c_re, c_im=v_c_im, d_skip=v_d_skip, final_gain=v_final_gain)
    small_g = dict(norm_gain=dg1, pool_scale=dps, a_re=g_a_re, a_im=g_a_im, log_dt=g_ldt, b_re=g_b_re,
                   b_im=g_b_im, c_re=g_c_re, c_im=g_c_im, d_skip=dd, final_gain=dg2)
    shapes = [small_w[n].shape for n in small_names]
    total = sum(small_w[n].size for n in small_names) + 1
    unit = N_DEV * SUBLANES
    rows = -(-(-(-total // LANES)) // unit) * unit
    gbuf = _pack_small([small_g[n] for n in small_names] + [lpart[0, :1]], rows)
    gsum = _allreduce_small(gbuf)
    wbuf = _pack_small([small_w[n] for n in small_names], rows)
    mbuf = _pack_small([small_m[n] for n in small_names], rows)
    vbuf = _pack_small([small_v[n] for n in small_names], rows)
    dsm, msm, vsm = _adamw(wbuf, gsum, mbuf, vbuf, "adamw_small")
    g_small = dict(zip(small_names, _unpack_small(gsum, shapes)))
    d_small = dict(zip(small_names, _unpack_small(dsm, shapes)))
    m_small = dict(zip(small_names, _unpack_small(msm, shapes)))
    v_small = dict(zip(small_names, _unpack_small(vsm, shapes)))
    loss = gsum.reshape(-1)[total - 1]

    g_out, d_out, m_out, v_out = dict(g_small), dict(d_small), dict(m_small), dict(v_small)
    for n, gs in zip(big_names, gshard):
        w_, m_, v_ = big[n]
        r2 = shard2d[n]
        d_, mn_, vn_ = _adamw(w_.reshape(r2), gs, m_.reshape(r2), v_.reshape(r2), "adamw_" + n)
        g_out[n], d_out[n], m_out[n], v_out[n] = (a.reshape(w_.shape) for a in (gs, d_, mn_, vn_))

    order = ["norm_gain", "w_in", "w_pool", "pool_scale", "a_re", "a_im", "log_dt", "b_re", "b_im", "c_re",
             "c_im", "d_skip", "w_glu", "w_out", "w_ple", "w_ple_gate", "final_gain"]
    return (loss, grad_x[None], *[g_out[n] for n in order], *[d_out[n] for n in order],
            *[m_out[n] for n in order], *[v_out[n] for n in order])
```

```python
import functools

import jax
import jax.numpy as jnp
from jax import lax
from jax.experimental import pallas as pl
from jax.experimental.pallas import tpu as pltpu

F32, BF16 = jnp.float32, jnp.bfloat16
MESH = pl.DeviceIdType.MESH
ANY = pl.BlockSpec(memory_space=pl.ANY)
VMEM_FULL = pl.BlockSpec(memory_space=pltpu.VMEM)

EPS = 1e-6
A_RE_MAX = -1e-4
SSM_GROUP = 16
SSM_STATE = 64
POOL_WINDOWS = (2, 4, 8, 16)
POOL_HALO = 16
ADAM_LR, ADAM_B1, ADAM_B2, ADAM_EPS, ADAM_WD, ADAM_STEP = 0.001, 0.9, 0.999, 1e-08, 0.01, 10

V7X_VMEM_BYTES = 64 * 1024 * 1024
VMEM_LIMIT = V7X_VMEM_BYTES - 8 * 1024 * 1024
SUBLANES, LANES = 8, 128
SSM_TILE_GROUPS = 8
SCAN_LANES = 512
N_DEV, N_CHIP = 8, 4
DMA_CHUNK_BYTES = 256 * 1024
DMA_MAX_CHUNKS = 32
AG_CHUNKS = 8
RS_CHUNKS = 8


def _t(n, pref):
    return pref if n % pref == 0 else n


def _cp(sem=None, vmem=VMEM_LIMIT):
    return pltpu.CompilerParams(dimension_semantics=sem, vmem_limit_bytes=vmem)


def _call(body, **kw):
    return pl.pallas_call(body, **kw)


NN = ((1,), (0,))
NT = ((1,), (1,))
TN = ((0,), (0,))


def _mm(a, b, *, dims, grid, a_spec, b_spec, o_spec, out_shape, name, res=None, r_spec=None, bf16_copy=False,
        comm=None):
    nk, kax = grid[-1], len(grid) - 1
    acc_shape = tuple(d for d in o_spec.block_shape if d is not None)

    def core(*refs):
        refs = list(refs)
        a_ref, b_ref = refs[:2]
        r_ref = refs[2] if res is not None else None
        outs = refs[3 if res is not None else 2:]
        o_ref = outs[0]
        o2_ref = outs[1] if bf16_copy else None
        acc = outs[-1] if nk > 1 else None

        def finish(r):
            if r_ref is not None:
                r = r + r_ref[...]
            o_ref[...] = r.astype(o_ref.dtype)
            if o2_ref is not None:
                o2_ref[...] = r.astype(BF16)

        part = lax.dot_general(a_ref[...].astype(BF16), b_ref[...].astype(BF16),
                               (dims, ((), ())), preferred_element_type=F32)
        if nk == 1:
            finish(part)
        else:
            k = pl.program_id(kax)

            @pl.when(k == 0)
            def _():
                acc[...] = part

            @pl.when(k > 0)
            def _():
                acc[...] += part

            @pl.when(k == nk - 1)
            def _():
                finish(acc[...])

    ins, specs = [a, b], [a_spec, b_spec]
    if res is not None:
        ins.append(res)
        specs.append(r_spec)
    o_specs, o_shapes = [o_spec], [out_shape]
    if bf16_copy:
        o_specs = [o_spec, o_spec]
        o_shapes = [out_shape, jax.ShapeDtypeStruct(out_shape.shape, BF16)]
    scratch = [pltpu.VMEM(acc_shape, F32)] if nk > 1 else []
    body, extra = _hosted(core, comm, grid, len(ins), len(o_specs), len(scratch))
    sem = ("arbitrary",) * len(grid) if comm else ("parallel",) * kax + ("arbitrary",)
    outs = _call(body, grid=grid, in_specs=specs + extra["in_specs"], out_specs=o_specs + extra["out_specs"],
                 out_shape=o_shapes + extra["out_shape"], scratch_shapes=scratch + extra["scratch"],
                 compiler_params=_cp(sem), name=name)(*ins, *extra["ins"])
    return outs[0] if len(outs) == 1 else outs


def _bs(shape, fn):
    return pl.BlockSpec(shape, fn)


def _sigmoid(v):
    return 1.0 / (1.0 + jnp.exp(-v))


def _gelu(v):
    return 0.5 * v * (1.0 + jnp.tanh(0.7978845608028654 * (v + 0.044715 * v * v * v)))


def _gelu_grad(v):
    t = jnp.tanh(0.7978845608028654 * (v + 0.044715 * v * v * v))
    return 0.5 * (1.0 + t) + 0.5 * v * (1.0 - t * t) * 0.7978845608028654 * (1.0 + 3 * 0.044715 * v * v)


def _norm1(x, g1, tb):
    T, D = x.shape

    def body(x_ref, g_ref, o_ref):
        xv = x_ref[...]
        r = lax.rsqrt(jnp.mean(xv * xv, axis=-1, keepdims=True) + EPS)
        o_ref[...] = ((xv * r) * g_ref[...]).astype(BF16)

    return _call(body, grid=(T // tb,),
                 in_specs=[_bs((tb, D), lambda i: (i, 0)), _bs((1, D), lambda i: (0, 0))],
                 out_specs=_bs((tb, D), lambda i: (i, 0)), out_shape=jax.ShapeDtypeStruct((T, D), BF16),
                 compiler_params=_cp(("parallel",)), name="norm1")(x, g1)


def _norm1_bwd(x, dhn, dh1, g1, tb):
    T, D = x.shape

    def body(x_ref, dhn_ref, dh1_ref, g_ref, dx_ref, dg_ref):
        @pl.when(pl.program_id(0) == 0)
        def _():
            dg_ref[...] = jnp.zeros_like(dg_ref)

        xv = x_ref[...]
        r = lax.rsqrt(jnp.mean(xv * xv, axis=-1, keepdims=True) + EPS)
        xh = xv * r
        dhn_v = dhn_ref[...]
        dg_ref[...] += jnp.sum(dhn_v * xh, axis=0, keepdims=True)
        dxh = dhn_v * g_ref[...]
        dx_ref[...] = dh1_ref[...] + r * (dxh - xh * jnp.mean(dxh * xh, axis=-1, keepdims=True))

    row = _bs((tb, D), lambda i: (i, 0))
    vec = _bs((1, D), lambda i: (0, 0))
    return _call(body, grid=(T // tb,), in_specs=[row, row, row, vec], out_specs=[row, vec],
                 out_shape=[jax.ShapeDtypeStruct((T, D), F32), jax.ShapeDtypeStruct((1, D), F32)],
                 compiler_params=_cp(("arbitrary",)), name="norm1_bwd")(x, dhn, dh1, g1)


def _gate_fwd(mixed, proj, hg, ps, tb):
    T, P = mixed.shape

    def body(mx_ref, pg_ref, sg_ref, hg_ref, ps_ref, o_ref):
        pg, sg = pg_ref[...], sg_ref[...]
        ya = (mx_ref[...] * ps_ref[...]) * (pg * _sigmoid(pg))
        hgv = hg_ref[...]
        o = hgv[:, :P] * _sigmoid(hgv[:, P:])
        yb = o * (sg * _sigmoid(sg))
        o_ref[:, :P] = ya.astype(BF16)
        o_ref[:, P:] = yb.astype(BF16)

    return _call(body, grid=(T // tb,),
                 in_specs=[_bs((tb, P), lambda i: (i, 0)), _bs((tb, P), lambda i: (i, 1)),
                           _bs((tb, P), lambda i: (i, 3)), _bs((tb, 2 * P), lambda i: (i, 0)),
                           _bs((1, P), lambda i: (0, 0))],
                 out_specs=_bs((tb, 2 * P), lambda i: (i, 0)),
                 out_shape=jax.ShapeDtypeStruct((T, 2 * P), BF16),
                 compiler_params=_cp(("parallel",)), name="gate_fwd")(mixed, proj, proj, hg, ps)


def _gate_bwd(dcat, mixed, proj, hg, ps, tb, comm=None):
    T, P = mixed.shape

    def core(dc_ref, mx_ref, pg_ref, sg_ref, hg_ref, ps_ref, dmx_ref, dpg_ref, dsg_ref, dhg_ref, dps_ref):
        @pl.when(pl.program_id(0) == 0)
        def _():
            dps_ref[...] = jnp.zeros_like(dps_ref)

        dc = dc_ref[...]
        dya, dyb = dc[:, :P], dc[:, P:]
        pg, sg, mx, psv = pg_ref[...], sg_ref[...], mx_ref[...], ps_ref[...]
        s_pg = _sigmoid(pg)
        dpa = dya * (pg * s_pg)
        dpg_ref[...] = (dya * (mx * psv) * (s_pg * (1.0 + pg * (1.0 - s_pg)))).astype(BF16)
        dps_ref[...] += jnp.sum(dpa * mx, axis=0, keepdims=True)
        dmx_ref[...] = (dpa * psv).astype(BF16)
        hgv = hg_ref[...]
        h1, s_h2 = hgv[:, :P], _sigmoid(hgv[:, P:])
        s_sg = _sigmoid(sg)
        do = dyb * (sg * s_sg)
        dsg_ref[...] = (dyb * (h1 * s_h2) * (s_sg * (1.0 + sg * (1.0 - s_sg)))).astype(BF16)
        dhg_ref[:, :P] = (do * s_h2).astype(BF16)
        dhg_ref[:, P:] = (do * h1 * s_h2 * (1.0 - s_h2)).astype(BF16)

    rowp = _bs((tb, P), lambda i: (i, 0))
    row2 = _bs((tb, 2 * P), lambda i: (i, 0))
    vec = _bs((1, P), lambda i: (0, 0))
    body, extra = _hosted(core, comm, (T // tb,), 6, 5, 0)
    return _call(body, grid=(T // tb,),
                 in_specs=[row2, rowp, _bs((tb, P), lambda i: (i, 1)), _bs((tb, P), lambda i: (i, 3)), row2, vec]
                 + extra["in_specs"],
                 out_specs=[rowp, rowp, rowp, row2, vec] + extra["out_specs"],
                 out_shape=[jax.ShapeDtypeStruct((T, P), BF16), jax.ShapeDtypeStruct((T, P), BF16),
                            jax.ShapeDtypeStruct((T, P), BF16), jax.ShapeDtypeStruct((T, 2 * P), BF16),
                            jax.ShapeDtypeStruct((1, P), F32)] + extra["out_shape"],
                 scratch_shapes=extra["scratch"],
                 compiler_params=_cp(("arbitrary",)), name="gate_bwd")(dcat, mixed, proj, proj, hg, ps, *extra["ins"])


def _final_fb(h1, e, z, tgt, g2, tb):
    T, D = h1.shape

    def body(h1_ref, e_ref, z_ref, t_ref, g_ref, dh2_ref, de_ref, dz_ref, dg_ref, l_ref):
        @pl.when(pl.program_id(0) == 0)
        def _():
            dg_ref[...] = jnp.zeros_like(dg_ref)
            l_ref[...] = jnp.zeros_like(l_ref)

        ev = e_ref[...]
        s = _sigmoid(z_ref[...])
        h2 = h1_ref[...] + ev * s
        r = lax.rsqrt(jnp.mean(h2 * h2, axis=-1, keepdims=True) + EPS)
        xh = h2 * r
        gv = g_ref[...]
        diff = xh * gv - t_ref[...]
        l_ref[...] += 0.5 * jnp.sum(jnp.mean(diff * diff, axis=-1, keepdims=True))
        dout = diff * (1.0 / D)
        dg_ref[...] += jnp.sum(dout * xh, axis=0, keepdims=True)
        dxh = dout * gv
        dh2 = r * (dxh - xh * jnp.mean(dxh * xh, axis=-1, keepdims=True))
        dh2_ref[...] = dh2
        de_ref[...] = (dh2 * s).astype(BF16)
        dz_ref[...] = (dh2 * ev * s * (1.0 - s)).astype(BF16)

    row = _bs((tb, D), lambda i: (i, 0))
    vec = _bs((1, D), lambda i: (0, 0))
    return _call(body, grid=(T // tb,), in_specs=[row, row, row, row, vec],
                 out_specs=[row, row, row, vec, _bs((1, LANES), lambda i: (0, 0))],
                 out_shape=[jax.ShapeDtypeStruct((T, D), F32), jax.ShapeDtypeStruct((T, D), BF16),
                            jax.ShapeDtypeStruct((T, D), BF16), jax.ShapeDtypeStruct((1, D), F32),
                            jax.ShapeDtypeStruct((1, LANES), F32)],
                 compiler_params=_cp(("arbitrary",)), name="final_fb")(h1, e, z, tgt, g2)


def _pool_inv_count(t0, rows, pg, ngroups):
    t = t0 + lax.broadcasted_iota(jnp.int32, (rows, pg), 0)
    parts = []
    for w in POOL_WINDOWS[:ngroups]:
        parts.append(jnp.where(t + 1 >= w, 1.0 / w, 1.0 / (t + 1).astype(F32)))
    return parts


def _pool_fwd(proj, P, tb):
    T = proj.shape[0]
    ng = len(POOL_WINDOWS)
    pg = P // ng
    hb = tb // POOL_HALO

    def body(v_ref, tail_ref, o_ref, ext):
        i = pl.program_id(0)
        ext[pl.ds(0, POOL_HALO), :] = jnp.where(i > 0, tail_ref[...], 0.0)
        ext[pl.ds(POOL_HALO, tb), :] = v_ref[...]
        inv = _pool_inv_count(i * tb, tb, pg, ng)
        for g, w in enumerate(POOL_WINDOWS):
            cols = pl.ds(g * pg, pg)
            win = ext[pl.ds(POOL_HALO, tb), cols]
            for k in range(1, w):
                win = win + ext[pl.ds(POOL_HALO - k, tb), cols]
            o_ref[:, cols] = (win * inv[g] - ext[pl.ds(POOL_HALO, tb), cols]).astype(BF16)

    return _call(body, grid=(T // tb,),
                 in_specs=[_bs((tb, P), lambda i: (i, 0)),
                           _bs((POOL_HALO, P), lambda i: (jnp.maximum(i * hb - 1, 0), 0))],
                 out_specs=_bs((tb, P), lambda i: (i, 0)), out_shape=jax.ShapeDtypeStruct((T, P), BF16),
                 scratch_shapes=[pltpu.VMEM((tb + POOL_HALO, P), F32)],
                 compiler_params=_cp(("arbitrary",)), name="pool_fwd")(proj, proj)


def _pool_bwd(dpooled, tb, comm=None):
    T, P = dpooled.shape
    ng = len(POOL_WINDOWS)
    pg = P // ng
    hb = tb // POOL_HALO
    nb = T // tb

    def core(d_ref, head_ref, o_ref, ext):
        i = pl.program_id(0)
        inv = _pool_inv_count(i * tb, tb, pg, ng)
        invh = _pool_inv_count((i + 1) * tb, POOL_HALO, pg, ng)
        for g in range(ng):
            cols = pl.ds(g * pg, pg)
            ext[pl.ds(0, tb), cols] = d_ref[:, cols] * inv[g]
            ext[pl.ds(tb, POOL_HALO), cols] = jnp.where(i < nb - 1, head_ref[:, cols] * invh[g], 0.0)
        for g, w in enumerate(POOL_WINDOWS):
            cols = pl.ds(g * pg, pg)
            acc = ext[pl.ds(0, tb), cols]
            for k in range(1, w):
                acc = acc + ext[pl.ds(k, tb), cols]
            o_ref[:, cols] = (acc - d_ref[:, cols]).astype(BF16)

    body, extra = _hosted(core, comm, (nb,), 2, 1, 1)
    return _call(body, grid=(nb,),
                 in_specs=[_bs((tb, P), lambda i: (i, 0)),
                           _bs((POOL_HALO, P), lambda i: (jnp.minimum((i + 1) * hb, T // POOL_HALO - 1), 0))]
                 + extra["in_specs"],
                 out_specs=[_bs((tb, P), lambda i: (i, 0))] + extra["out_specs"],
                 out_shape=[jax.ShapeDtypeStruct((T, P), BF16)] + extra["out_shape"],
                 scratch_shapes=[pltpu.VMEM((tb + POOL_HALO, P), F32)] + extra["scratch"],
                 compiler_params=_cp(("arbitrary",)), name="pool_bwd")(dpooled, dpooled, *extra["ins"])


def _zoh(a_re, a_im, ldt, b_re, b_im):
    lam_re = jnp.minimum(a_re, A_RE_MAX)
    lam_im = a_im
    dt = jnp.exp(ldt)
    mag = jnp.exp(lam_re * dt)
    ang = lam_im * dt
    ab_re = mag * jnp.cos(ang)
    ab_im = mag * jnp.sin(ang)
    den = lam_re * lam_re + lam_im * lam_im
    n_re = ab_re - 1.0
    n_im = ab_im
    q_re = (n_re * lam_re + n_im * lam_im) / den
    q_im = (n_im * lam_re - n_re * lam_im) / den
    return ab_re, ab_im, q_re * b_re - q_im * b_im, q_re * b_im + q_im * b_re


def _ssm_prep(a_re, a_im, ldt, bt_re, bt_im):
    shp = jax.ShapeDtypeStruct(a_re.shape, F32)

    def body(a, b, c, d, e, o0, o1, o2, o3):
        r = _zoh(a[...], b[...], c[...], d[...], e[...])
        o0[...], o1[...], o2[...], o3[...] = r

    return _call(body, in_specs=[VMEM_FULL] * 5, out_specs=[VMEM_FULL] * 4, out_shape=[shp] * 4,
                 name="ssm_prep")(a_re, a_im, ldt, bt_re, bt_im)


def _ssm_prep_bwd(a_re, a_im, ldt, bt_re, bt_im, dab_re, dab_im, dbb_re, dbb_im, G):
    GC, N = a_re.shape
    C = GC // G

    def body(a, b, c, d, e, g0, g1, g2, g3, da_re, da_im, dldt, db_re, db_im):
        _, vjp = jax.vjp(_zoh, a[...], b[...], c[...], d[...], e[...])
        ga_re, ga_im, gl, gb_re, gb_im = vjp((g0[...], g1[...], g2[...], g3[...]))
        da_re[...] = jnp.sum(ga_re.reshape(G, C, N), axis=1)
        da_im[...] = jnp.sum(ga_im.reshape(G, C, N), axis=1)
        dldt[...] = jnp.sum(jnp.sum(gl.reshape(G, C, N), axis=1), axis=1, keepdims=True)
        db_re[...] = gb_re
        db_im[...] = gb_im

    gn = jax.ShapeDtypeStruct((G, N), F32)
    full = jax.ShapeDtypeStruct((GC, N), F32)
    return _call(body, in_specs=[VMEM_FULL] * 9, out_specs=[VMEM_FULL] * 5,
                 out_shape=[gn, gn, jax.ShapeDtypeStruct((G, 1), F32), full, full],
                 name="ssm_prep_bwd")(a_re, a_im, ldt, bt_re, bt_im, dab_re, dab_im, dbb_re, dbb_im)


def _coef_tiles(abr, abi, reverse):
    ns = abr.shape[1]
    row = lax.broadcasted_iota(jnp.int32, (SUBLANES, ns), 0)
    ar = jnp.broadcast_to(abr, (SUBLANES, ns))
    ai = jnp.broadcast_to(-abi if reverse else abi, (SUBLANES, ns))
    a2r, a2i = ar * ar - ai * ai, 2.0 * ar * ai
    a4r, a4i = a2r * a2r - a2i * a2i, 2.0 * a2r * a2i
    out = []
    for d, (vr, vi) in ((1, (ar, ai)), (2, (a2r, a2i)), (4, (a4r, a4i))):
        keep = (row < SUBLANES - d) if reverse else (row >= d)
        out += [jnp.where(keep, vr, 0.0), jnp.where(keep, vi, 0.0)]
    pr, pi = ar, ai
    for k in range(1, SUBLANES):
        sel = (row <= SUBLANES - 1 - k) if reverse else (row >= k)
        nr, ni = pr * ar - pi * ai, pr * ai + pi * ar
        pr, pi = jnp.where(sel, nr, pr), jnp.where(sel, ni, pi)
    return out + [pr, pi]


def _cpow(ar, ai, n):
    out, br, bi = None, ar, ai
    while n:
        if n & 1:
            out = (br, bi) if out is None else (out[0] * br - out[1] * bi, out[0] * bi + out[1] * br)
        br, bi = br * br - bi * bi, 2.0 * br * bi
        n >>= 1
    return out


def _seg_perm_matrix(nrows):
    r = jnp.arange(nrows)
    src = (nrows // SUBLANES) * (r % SUBLANES) + r // SUBLANES
    return (src[:, None] == jnp.arange(nrows)[None, :]).astype(BF16)


def _seg_order_rows(pm, xb):
    return jnp.dot(pm, xb, preferred_element_type=F32).astype(BF16)


def _time_order_rows(pm, x, terms):
    out, rest = None, x
    for t in range(terms):
        piece = rest.astype(BF16)
        part = lax.dot_general(pm, piece, (TN, ((), ())), preferred_element_type=F32)
        out = part if out is None else out + part
        if t + 1 < terms:
            rest = rest - piece.astype(F32)
    return out


def _seg_scan(xr_ref, xi_ref, abr_ref, abi_ref, coef_ref, car_ref, cai_ref, *, nrows, ns, reverse,
              cmat=None, dab=None):
    seg = nrows // SUBLANES
    cw = min(SCAN_LANES, ns)
    row = lax.broadcasted_iota(jnp.int32, (SUBLANES, cw), 0)
    first, last = (SUBLANES - 1, 0) if reverse else (0, SUBLANES - 1)

    def tile(i):
        return pl.ds(pl.multiple_of(((seg - 1 - i) if reverse else i) * SUBLANES, SUBLANES), SUBLANES)

    for cc in range(ns // cw):
        cols = pl.ds(cc * cw, cw)
        ar = jnp.broadcast_to(abr_ref[:, cols], (SUBLANES, cw))
        ai = jnp.broadcast_to(abi_ref[:, cols], (SUBLANES, cw))
        if reverse:
            ai = -ai

        def local(i, x, cols=cols, ar=ar, ai=ai):
            rows = tile(i)
            nr = ar * x[0] - ai * x[1] + xr_ref[rows, cols]
            ni = ar * x[1] + ai * x[0] + xi_ref[rows, cols]
            xr_ref[rows, cols] = nr
            xi_ref[rows, cols] = ni
            return nr, ni

        zero = jnp.zeros((SUBLANES, cw), F32)
        er, ei = lax.fori_loop(0, seg, local, (zero, zero))

        co = [coef_ref[k, :, cols] for k in range(8)]
        for lvl, d in enumerate((1, 2, 4)):
            kr, ki = co[2 * lvl], co[2 * lvl + 1]
            sh = SUBLANES - d if reverse else d
            sr, si = pltpu.roll(er, sh, 0), pltpu.roll(ei, sh, 0)
            er, ei = er + (kr * sr - ki * si), ei + (kr * si + ki * sr)
        c0r, c0i = car_ref[:, cols], cai_ref[:, cols]
        er, ei = er + (co[6] * c0r - co[7] * c0i), ei + (co[6] * c0i + co[7] * c0r)
        nb_shift = SUBLANES - 1 if reverse else 1
        cmr = jnp.where(row == first, c0r, pltpu.roll(er, nb_shift, 0))
        cmi = jnp.where(row == first, c0i, pltpu.roll(ei, nb_shift, 0))
        car_ref[:, cols] = jnp.broadcast_to(er[last:last + 1, :], er.shape)
        cai_ref[:, cols] = jnp.broadcast_to(ei[last:last + 1, :], ei.shape)
        if cmat is not None:
            cmat[0][:, cols] = cmr
            cmat[1][:, cols] = cmi

        w0 = (ar * cmr - ai * cmi, ar * cmi + ai * cmr)
        if dab is None:
            def fix(i, w, cols=cols, ar=ar, ai=ai):
                rows = tile(i)
                xr_ref[rows, cols] = xr_ref[rows, cols] + w[0]
                xi_ref[rows, cols] = xi_ref[rows, cols] + w[1]
                return ar * w[0] - ai * w[1], ar * w[1] + ai * w[0]

            lax.fori_loop(0, seg, fix, w0)
        else:
            s_re, s_im, e_re, e_im, o_re, o_im = dab

            def add(rows, w, pr, pi, acc):
                gr = xr_ref[rows, cols] + w[0]
                gi = xi_ref[rows, cols] + w[1]
                xr_ref[rows, cols] = gr
                xi_ref[rows, cols] = gi
                return acc[0] + (gr * pr + gi * pi), acc[1] + (gi * pr - gr * pi)

            def fix(i, st, cols=cols, ar=ar, ai=ai):
                w, acc = st[:2], st[2:]
                rows = tile(i)
                before = pl.ds(pl.multiple_of((seg - 2 - i) * SUBLANES, SUBLANES), SUBLANES)
                acc = add(rows, w, s_re[before, cols], s_im[before, cols], acc)
                return (ar * w[0] - ai * w[1], ar * w[1] + ai * w[0]) + acc

            st = lax.fori_loop(0, seg - 1, fix, w0 + (zero, zero))
            acc = add(pl.ds(0, SUBLANES), st[:2], e_re[:, cols], e_im[:, cols], st[2:])
            o_re[:, cols] += jnp.sum(acc[0], axis=0, keepdims=True)
            o_im[:, cols] += jnp.sum(acc[1], axis=0, keepdims=True)


def _hosted(core, comm, grid, n_in, n_out, n_scratch):
    ci = len(comm["ins"]) if comm else 0
    co = len(comm["out_shape"]) if comm else 0

    def body(*refs):
        ins, rest = refs[:n_in + ci], refs[n_in + ci:]
        outs, scr = rest[:n_out + co], rest[n_out + co:]
        hooks = functools.partial(_comm_hooks, comm, grid, ins[n_in:], outs[n_out:], scr[n_scratch:])
        hooks(before=True)
        core(*ins[:n_in], *outs[:n_out], *scr[:n_scratch])
        hooks(before=False)

    extra = dict(ins=list(comm["ins"]) if comm else [], in_specs=[ANY] * ci, out_specs=[ANY] * co,
                 out_shape=list(comm["out_shape"]) if comm else [], scratch=list(comm["scratch"]) if comm else [])
    return body, extra


def _ssm_fwd(proj, bdr, bdi, cdr, cdi, abr, abi, dsk, P, tb, comm=None):
    T = proj.shape[0]
    ntl, ct, st = bdr.shape
    ns = ntl * st
    nb = T // tb

    def core(u_ref, bdr_ref, bdi_ref, cdr_ref, cdi_ref, abr_ref, abi_ref, d_ref, pm_ref,
             y_ref, ge_ref, bsr_ref, bsi_ref, sr, si, coef, car, cai, up):
        @pl.when(pl.program_id(0) == 0)
        def _():
            seg_pow = _cpow(abr_ref[...], abi_ref[...], tb // SUBLANES)
            for k, tile in enumerate(_coef_tiles(seg_pow[0], seg_pow[1], False)):
                coef[k] = tile
            car[...] = jnp.zeros_like(car)
            cai[...] = jnp.zeros_like(cai)

        bsr_ref[...] = car[...]
        bsi_ref[...] = cai[...]
        u = u_ref[...]
        ub = _seg_order_rows(pm_ref[...], u.astype(BF16))
        for s in range(ntl):
            us = ub[:, s * ct:(s + 1) * ct]
            sr[:, s * st:(s + 1) * st] = jnp.dot(us, bdr_ref[s], preferred_element_type=F32)
            si[:, s * st:(s + 1) * st] = jnp.dot(us, bdi_ref[s], preferred_element_type=F32)
        _seg_scan(sr, si, abr_ref, abi_ref, coef, car, cai, nrows=tb, ns=ns, reverse=False)
        for s in range(ntl):
            s_re = sr[:, s * st:(s + 1) * st].astype(BF16)
            s_im = si[:, s * st:(s + 1) * st].astype(BF16)
            up[:, s * ct:(s + 1) * ct] = (jnp.dot(s_re, cdr_ref[s], preferred_element_type=F32)
                                          - jnp.dot(s_im, cdi_ref[s], preferred_element_type=F32))
        y = _time_order_rows(pm_ref[...], up[...], 3) + d_ref[...] * u
        y_ref[...] = y
        ge_ref[...] = _gelu(y).astype(BF16)

    full3 = lambda a: _bs(a.shape, lambda i: (0, 0, 0))
    vec = lambda n: _bs((1, n), lambda i: (0, 0))
    row = _bs((tb, P), lambda i: (i, 0))
    st_spec = _bs((None, SUBLANES, ns), lambda i: (i, 0, 0))
    body, extra = _hosted(core, comm, (nb,), 9, 4, 6)
    return _call(body, grid=(nb,),
                 in_specs=[_bs((tb, P), lambda i: (i, 2)), full3(bdr), full3(bdi), full3(cdr), full3(cdi),
                           vec(ns), vec(ns), vec(P), _bs((tb, tb), lambda i: (0, 0))] + extra["in_specs"],
                 out_specs=[row, row, st_spec, st_spec] + extra["out_specs"],
                 out_shape=[jax.ShapeDtypeStruct((T, P), F32), jax.ShapeDtypeStruct((T, P), BF16),
                            jax.ShapeDtypeStruct((nb, SUBLANES, ns), F32),
                            jax.ShapeDtypeStruct((nb, SUBLANES, ns), F32)] + extra["out_shape"],
                 scratch_shapes=[pltpu.VMEM((tb, ns), F32), pltpu.VMEM((tb, ns), F32),
                                 pltpu.VMEM((8, SUBLANES, ns), F32),
                                 pltpu.VMEM((SUBLANES, ns), F32), pltpu.VMEM((SUBLANES, ns), F32),
                                 pltpu.VMEM((tb, P), F32)] + extra["scratch"],
                 compiler_params=_cp(("arbitrary",)), name="ssm_fwd")(
                     proj, bdr, bdi, cdr, cdi, abr, abi, dsk, _seg_perm_matrix(tb), *extra["ins"])


def _ssm_bwd(proj, y, dge, bsr, bsi, bdr, bdi, cdr, cdi, abr, abi, dsk, dpi, dpg, dsg, P, tb, comm=None):
    T = proj.shape[0]
    ntl, ct, st = bdr.shape
    ns = ntl * st
    nb = T // tb

    def core(u_ref, y_ref, dge_ref, bsr_ref, bsi_ref, abr_ref, abi_ref, d_ref, pm_ref, dpi_ref, dpg_ref, dsg_ref,
             bdr_h, bdi_h, cdr_h, cdi_h,
             dproj_ref, dabr_ref, dabi_ref, dd_ref, dbdr_h, dbdi_h, dcdr_h, dcdi_h,
             wbdr, wbdi, wcdr, wcdi, abdr, abdi, acdr, acdi, spr, spi, gr, gi, coef_f, coef_r,
             car, cai, gcr, gci, ser, sei, dup):
        i = pl.program_id(0)

        @pl.when(i == 0)
        def _():
            for h, w in ((bdr_h, wbdr), (bdi_h, wbdi), (cdr_h, wcdr), (cdi_h, wcdi)):
                pltpu.sync_copy(h, w)
            for a in (abdr, abdi, acdr, acdi, gcr, gci):
                a[...] = jnp.zeros_like(a)
            for o in (dabr_ref, dabi_ref, dd_ref):
                o[...] = jnp.zeros_like(o)
            seg_pow = _cpow(abr_ref[...], abi_ref[...], tb // SUBLANES)
            for k, tile in enumerate(_coef_tiles(seg_pow[0], seg_pow[1], False)):
                coef_f[k] = tile
            for k, tile in enumerate(_coef_tiles(seg_pow[0], seg_pow[1], True)):
                coef_r[k] = tile

        car[...] = bsr_ref[...]
        cai[...] = bsi_ref[...]
        u = u_ref[...]
        dy = dge_ref[...] * _gelu_grad(y_ref[...])
        ub = _seg_order_rows(pm_ref[...], u.astype(BF16))
        dyb = _seg_order_rows(pm_ref[...], dy.astype(BF16))
        for s in range(ntl):
            us = ub[:, s * ct:(s + 1) * ct]
            spr[:, s * st:(s + 1) * st] = jnp.dot(us, wbdr[s], preferred_element_type=F32)
            spi[:, s * st:(s + 1) * st] = jnp.dot(us, wbdi[s], preferred_element_type=F32)
        _seg_scan(spr, spi, abr_ref, abi_ref, coef_f, car, cai, nrows=tb, ns=ns, reverse=False, cmat=(ser, sei))

        for s in range(ntl):
            dys = dyb[:, s * ct:(s + 1) * ct]
            gr[:, s * st:(s + 1) * st] = lax.dot_general(dys, wcdr[s], (NT, ((), ())), preferred_element_type=F32)
            gi[:, s * st:(s + 1) * st] = -lax.dot_general(dys, wcdi[s], (NT, ((), ())), preferred_element_type=F32)
        _seg_scan(gr, gi, abr_ref, abi_ref, coef_r, gcr, gci, nrows=tb, ns=ns, reverse=True,
                  dab=(spr, spi, ser, sei, dabr_ref, dabi_ref))

        for s in range(ntl):
            sl_c, sl_s = slice(s * ct, (s + 1) * ct), slice(s * st, (s + 1) * st)
            s_re = spr[:, sl_s].astype(BF16)
            s_im = spi[:, sl_s].astype(BF16)
            g_re, g_im = gr[:, sl_s].astype(BF16), gi[:, sl_s].astype(BF16)
            dys, us = dyb[:, sl_c], ub[:, sl_c]
            acdr[s] += lax.dot_general(s_re, dys, (TN, ((), ())), preferred_element_type=F32)
            acdi[s] -= lax.dot_general(s_im, dys, (TN, ((), ())), preferred_element_type=F32)
            abdr[s] += lax.dot_general(us, g_re, (TN, ((), ())), preferred_element_type=F32)
            abdi[s] += lax.dot_general(us, g_im, (TN, ((), ())), preferred_element_type=F32)
            dup[:, sl_c] = (lax.dot_general(g_re, wbdr[s], (NT, ((), ())), preferred_element_type=F32)
                            + lax.dot_general(g_im, wbdi[s], (NT, ((), ())), preferred_element_type=F32))
        dd_ref[...] += jnp.sum(dy * u, axis=0, keepdims=True)
        du = _time_order_rows(pm_ref[...], dup[...], 2) + d_ref[...] * dy
        dproj_ref[:, 0:P] = dpi_ref[...]
        dproj_ref[:, P:2 * P] = dpg_ref[...]
        dproj_ref[:, 2 * P:3 * P] = du.astype(BF16)
        dproj_ref[:, 3 * P:4 * P] = dsg_ref[...]

        @pl.when(i == nb - 1)
        def _():
            for a, h in ((abdr, dbdr_h), (abdi, dbdi_h), (acdr, dcdr_h), (acdi, dcdi_h)):
                pltpu.sync_copy(a, h)

    rev = lambda i: nb - 1 - i
    vec = lambda n: _bs((1, n), lambda i: (0, 0))
    row = _bs((tb, P), lambda i: (rev(i), 0))
    st_spec = _bs((None, SUBLANES, ns), lambda i: (rev(i), 0, 0))
    bshape = jax.ShapeDtypeStruct(bdr.shape, F32)
    cshape = jax.ShapeDtypeStruct(cdr.shape, F32)
    body, extra = _hosted(core, comm, (nb,), 16, 8, 21)
    return _call(body, grid=(nb,),
                 in_specs=[_bs((tb, P), lambda i: (rev(i), 2)), row, row, st_spec, st_spec,
                           vec(ns), vec(ns), vec(P), _bs((tb, tb), lambda i: (0, 0)), row, row, row,
                           ANY, ANY, ANY, ANY] + extra["in_specs"],
                 out_specs=[_bs((tb, 4 * P), lambda i: (rev(i), 0)), vec(ns), vec(ns), vec(P), ANY, ANY, ANY, ANY]
                 + extra["out_specs"],
                 out_shape=[jax.ShapeDtypeStruct((T, 4 * P), BF16), jax.ShapeDtypeStruct((1, ns), F32),
                            jax.ShapeDtypeStruct((1, ns), F32), jax.ShapeDtypeStruct((1, P), F32),
                            bshape, bshape, cshape, cshape] + extra["out_shape"],
                 scratch_shapes=[pltpu.VMEM(bdr.shape, BF16), pltpu.VMEM(bdr.shape, BF16),
                                 pltpu.VMEM(cdr.shape, BF16), pltpu.VMEM(cdr.shape, BF16),
                                 pltpu.VMEM(bdr.shape, F32), pltpu.VMEM(bdr.shape, F32),
                                 pltpu.VMEM(cdr.shape, F32), pltpu.VMEM(cdr.shape, F32),
                                 pltpu.VMEM((tb, ns), F32), pltpu.VMEM((tb, ns), F32),
                                 pltpu.VMEM((tb, ns), F32), pltpu.VMEM((tb, ns), F32),
                                 pltpu.VMEM((8, SUBLANES, ns), F32), pltpu.VMEM((8, SUBLANES, ns), F32)]
                 + [pltpu.VMEM((SUBLANES, ns), F32)] * 6 + [pltpu.VMEM((tb, P), F32)] + extra["scratch"],
                 compiler_params=_cp(("arbitrary",)), name="ssm_bwd")(
                     proj, y, dge, bsr, bsi, abr, abi, dsk, _seg_perm_matrix(tb), dpi, dpg, dsg,
                     bdr, bdi, cdr, cdi, *extra["ins"])


def _adamw(w, g, m, v, name):
    R, C = w.shape
    tr = _t(R, 256)

    def body(w_ref, g_ref, m_ref, v_ref, d_ref, mo_ref, vo_ref):
        gv = g_ref[...]
        mn = ADAM_B1 * m_ref[...] + (1.0 - ADAM_B1) * gv
        vn = ADAM_B2 * v_ref[...] + (1.0 - ADAM_B2) * (gv * gv)
        m_hat = mn / (1.0 - ADAM_B1 ** ADAM_STEP)
        v_hat = vn / (1.0 - ADAM_B2 ** ADAM_STEP)
        d_ref[...] = -ADAM_LR * (m_hat / (jnp.sqrt(v_hat) + ADAM_EPS) + ADAM_WD * w_ref[...])
        mo_ref[...] = mn
        vo_ref[...] = vn

    blk = _bs((tr, C), lambda i: (i, 0))
    shp = jax.ShapeDtypeStruct((R, C), F32)
    return _call(body, grid=(R // tr,), in_specs=[blk] * 4, out_specs=[blk] * 3, out_shape=[shp] * 3,
                 compiler_params=_cp(("parallel",)), name=name)(w, g, m, v)


def _sum_cast(grad, got, place, name):
    J, H, C = got.shape
    tr = _t(H, 256)
    nb = H // tr

    def body(pl_ref, a_ref, b_ref, o_ref):
        o_ref[...] = (a_ref[...] + b_ref[...]).astype(BF16)

    blk = _bs((None, tr, C), lambda j, i, pc: (j, i, 0))
    mine = _bs((None, tr, C), lambda j, i, pc: (j, pc[1] * nb + i, 0))
    spec = pltpu.PrefetchScalarGridSpec(num_scalar_prefetch=1, grid=(J, nb), in_specs=[mine, blk], out_specs=blk)
    return _call(body, grid_spec=spec, out_shape=jax.ShapeDtypeStruct((J, H, C), BF16),
                 compiler_params=_cp(("parallel", "parallel")), name=name)(place, grad, got)


def _sum_chips(sent, arrived, place, name):
    J, H, C = arrived.shape
    tr = _t(H, 256)
    nb = H // tr

    def body(pl_ref, own_ref, a0_ref, a1_ref, a2_ref, o_ref):
        acc = own_ref[...].astype(F32)
        for r in (a0_ref, a1_ref, a2_ref):
            acc = acc + r[...].astype(F32)
        o_ref[...] = acc

    def other(k):
        return _bs((None, tr, C), lambda i, pc: (jnp.where(pc[0] <= k, k + 1, k), i, 0))

    spec = pltpu.PrefetchScalarGridSpec(
        num_scalar_prefetch=1, grid=(nb,),
        in_specs=[_bs((None, tr, C), lambda i, pc: (pc[0], i, 0)), other(0), other(1), other(2)],
        out_specs=_bs((tr, C), lambda i, pc: (pc[1] * nb + i, 0)))
    return _call(body, grid_spec=spec, out_shape=jax.ShapeDtypeStruct((2 * H, C), F32),
                 compiler_params=_cp(("parallel",)), name=name)(place, sent, arrived, arrived, arrived)


def _place():
    x, y, c = lax.axis_index("x"), lax.axis_index("y"), lax.axis_index("c")
    chips = [(1 - x, y), (x, 1 - y), (1 - x, 1 - y)]
    return x, y, c, chips


def _split(nrows, row_bytes, align, cap=None):
    k = max(1, min(cap or DMA_MAX_CHUNKS, (nrows * row_bytes) // DMA_CHUNK_BYTES))
    while k > 1 and nrows % (k * align):
        k -= 1
    return k


def _comm_call(plan, name):
    n_in, n_out = len(plan["ins"]), len(plan["out_shape"])

    def body(*refs):
        for phase in plan["phases"]:
            phase(refs[:n_in], refs[n_in:n_in + n_out], refs[n_in + n_out:])

    return _call(body, in_specs=[ANY] * n_in, out_specs=[ANY] * n_out, out_shape=plan["out_shape"],
                 scratch_shapes=plan["scratch"], name=name)(*plan["ins"])


def _comm_hooks(plan, grid, ins, outs, sems, *, before):
    if plan is None:
        return
    nsteps, step = 1, 0
    for d, g in enumerate(grid):
        nsteps, step = nsteps * g, step * g + pl.program_id(d)
    for p, (phase, frac) in enumerate(zip(plan["phases"], plan["at"])):
        if (p == 0) == before:
            pl.when(step == int(frac * (nsteps - 1)))(functools.partial(phase, ins, outs, sems))


def _ag_plan(shards, axes):
    n = len(shards)
    shapes = [a.shape for a in shards]

    def window(ref, i, chip, half=None):
        S, ax = shapes[i], axes[i]
        idx = []
        for d in range(len(S)):
            off, size = 0, S[d]
            if d == 0 and half is not None:
                off, size = half * (S[0] // 2), S[0] // 2
            if d == ax:
                off = off + chip * S[ax]
            idx.append(pl.ds(off, size))
        return ref.at[tuple(idx)]

    def copies(src, full, sems):
        ssem, rsem = sems
        x, y, c, chips = _place()
        me = 2 * x + y
        sib = (x, y, 1 - c)
        idx = [2 * cx + cy for cx, cy in chips]

        def rcopy(i, k, s_ref, d_ref, to):
            return pltpu.make_async_remote_copy(src_ref=s_ref, dst_ref=d_ref, send_sem=ssem.at[i, k],
                                                recv_sem=rsem.at[i, k], device_id=to, device_id_type=MESH)

        def ici(i, j, incoming):
            half_src = src[i].at[pl.ds(c * (shapes[i][0] // 2), shapes[i][0] // 2)]
            return rcopy(i, j, half_src, window(full[i], i, idx[j] if incoming else me, c), (*chips[j], c))

        def fwd(i, j, half):
            w = window(full[i], i, idx[j], half)
            return rcopy(i, 3 + j, w, w, sib)

        def own(i):
            return rcopy(i, 6, src[i], window(full[i], i, me), sib)

        return c, ici, fwd, own

    def send(src, full, sems):
        c, ici, fwd, own = copies(src, full, sems)
        for i in range(n):
            for j in range(3):
                ici(i, j, False).start()
        for i in range(n):
            own(i).start()

    def forward(src, full, sems):
        c, ici, fwd, own = copies(src, full, sems)
        for i in range(n):
            for j in range(3):
                ici(i, j, True).wait_recv()
                fwd(i, j, c).start()

    def finish(src, full, sems):
        c, ici, fwd, own = copies(src, full, sems)
        for i in range(n):
            for j in range(3):
                fwd(i, j, 1 - c).wait_recv()
            own(i).wait()
        for i in range(n):
            for j in range(3):
                ici(i, j, False).wait_send()
                fwd(i, j, c).wait_send()

    out_shape = [jax.ShapeDtypeStruct(tuple(N_CHIP * d if k == ax else d for k, d in enumerate(S)), BF16)
                 for S, ax in zip(shapes, axes)]
    return dict(ins=list(shards), out_shape=out_shape, phases=[send, forward, finish], at=[0.0, 0.8, 1.0],
                scratch=[pltpu.SemaphoreType.DMA((n, 7)), pltpu.SemaphoreType.DMA((n, 7))])


def _halves_plan(grads):
    n = len(grads)

    def send(g, got, sems):
        ssem, rsem = sems
        x, y, c, _ = _place()
        sib = (x, y, 1 - c)
        for i in range(n):
            J, R, C = g[i].shape
            H = R // 2
            k = _split(H, C * 4, SUBLANES, cap=DMA_MAX_CHUNKS // J)
            hr = H // k
            for j in range(J):
                for q in range(k):
                    other = pl.ds(pl.multiple_of((1 - c) * H + q * hr, SUBLANES), hr)
                    to = pl.ds(q * hr, hr)
                    pltpu.make_async_remote_copy(src_ref=g[i].at[j, other, :], dst_ref=got[i].at[j, to, :],
                                                 send_sem=ssem.at[i], recv_sem=rsem.at[i],
                                                 device_id=sib, device_id_type=MESH).start()

    def finish(g, got, sems):
        ssem, rsem = sems
        x, y, c, _ = _place()
        for i in range(n):
            pltpu.make_async_remote_copy(src_ref=got[i], dst_ref=got[i], send_sem=ssem.at[i], recv_sem=rsem.at[i],
                                         device_id=(x, y, 1 - c), device_id_type=MESH).wait()

    half = [jax.ShapeDtypeStruct((a.shape[0], a.shape[1] // 2, a.shape[2]), a.dtype) for a in grads]
    return dict(ins=list(grads), out_shape=half, phases=[send, finish], at=[0.0, 1.0],
                scratch=[pltpu.SemaphoreType.DMA((n,)), pltpu.SemaphoreType.DMA((n,))])


def _scatter_plan(parts):
    n = len(parts)

    def peers():
        x, y, c, chips = _place()
        return 2 * x + y, c, chips, [2 * cx + cy for cx, cy in chips]

    def send(s, got, sems):
        ssem, rsem = sems
        me, c, chips, idx = peers()
        for i in range(n):
            _, H, C = s[i].shape
            k = _split(H, C * 2, 16, cap=RS_CHUNKS)
            hr = H // k
            for q in range(k):
                rows = pl.ds(q * hr, hr)
                for j in range(3):
                    pltpu.make_async_remote_copy(src_ref=s[i].at[idx[j], rows, :], dst_ref=got[i].at[me, rows, :],
                                                 send_sem=ssem.at[i, j], recv_sem=rsem.at[i, j],
                                                 device_id=(*chips[j], c), device_id_type=MESH).start()

    def finish(s, got, sems):
        ssem, rsem = sems
        me, c, chips, idx = peers()
        for i in range(n):
            for j in range(3):
                pltpu.make_async_remote_copy(src_ref=s[i].at[idx[j]], dst_ref=got[i].at[idx[j]],
                                             send_sem=ssem.at[i, j], recv_sem=rsem.at[i, j],
                                             device_id=(*chips[j], c), device_id_type=MESH).wait()

    return dict(ins=list(parts), out_shape=[jax.ShapeDtypeStruct(a.shape, a.dtype) for a in parts],
                phases=[send, finish], at=[0.0, 1.0],
                scratch=[pltpu.SemaphoreType.DMA((n, 3)), pltpu.SemaphoreType.DMA((n, 3))])


def _join_halves(shards):
    n = len(shards)

    def body(*refs):
        full = refs[n:2 * n]
        ssem, rsem = refs[2 * n:]
        x, y, c, _ = _place()
        sib = (x, y, 1 - c)
        for i in range(n):
            H, C = full[i].shape[0] // 2, full[i].shape[1]
            k = _split(H, C * 4, SUBLANES)
            hr = H // k
            for q in range(k):
                rows = pl.ds(pl.multiple_of(c * H + q * hr, SUBLANES), hr)
                pltpu.make_async_remote_copy(src_ref=full[i].at[rows], dst_ref=full[i].at[rows],
                                             send_sem=ssem.at[i], recv_sem=rsem.at[i],
                                             device_id=sib, device_id_type=MESH).start()
        for i in range(n):
            half = full[i].at[pl.ds(0, full[i].shape[0] // 2)]
            pltpu.make_async_remote_copy(src_ref=half, dst_ref=half, send_sem=ssem.at[i], recv_sem=rsem.at[i],
                                         device_id=sib, device_id_type=MESH).wait()

    shp = [jax.ShapeDtypeStruct(a.shape, a.dtype) for a in shards]
    return _call(body, in_specs=[ANY] * n, out_specs=[ANY] * n, out_shape=shp,
                 input_output_aliases={i: i for i in range(n)},
                 scratch_shapes=[pltpu.SemaphoreType.DMA((n,)), pltpu.SemaphoreType.DMA((n,))],
                 name="rs_join")(*shards)


def _allreduce_small(buf):
    R, L = buf.shape
    RB = R // N_DEV

    def body(x_ref, o_ref, got, ssem, rsem):
        x, y, c, _ = _place()
        me = 4 * x + 2 * y + c

        def dev(k):
            return (k // 4, (k // 2) % 2, k % 2)

        def slab(k):
            return pl.ds(pl.multiple_of(k * RB, SUBLANES), RB)

        sends = []
        for d in range(1, N_DEV):
            peer = (me + d) % N_DEV
            cp = pltpu.make_async_remote_copy(src_ref=x_ref.at[slab(peer)], dst_ref=got.at[me],
                                              send_sem=ssem.at[0, d], recv_sem=rsem.at[0, d],
                                              device_id=dev(peer), device_id_type=MESH)
            cp.start()
            sends.append(cp)
        got[me] = x_ref[slab(me), :]
        for d in range(1, N_DEV):
            src = (me + N_DEV - d) % N_DEV
            pltpu.make_async_remote_copy(src_ref=x_ref.at[slab(me)], dst_ref=got.at[src],
                                         send_sem=ssem.at[0, d], recv_sem=rsem.at[0, d],
                                         device_id=dev(src), device_id_type=MESH).wait_recv()
        acc = got[0]
        for k in range(1, N_DEV):
            acc = acc + got[k]
        o_ref[slab(me), :] = acc
        for d in range(1, N_DEV):
            peer = (me + d) % N_DEV
            cp = pltpu.make_async_remote_copy(src_ref=o_ref.at[slab(me)], dst_ref=o_ref.at[slab(me)],
                                              send_sem=ssem.at[1, d], recv_sem=rsem.at[1, d],
                                              device_id=dev(peer), device_id_type=MESH)
            cp.start()
            sends.append(cp)
        for d in range(1, N_DEV):
            src = (me + N_DEV - d) % N_DEV
            pltpu.make_async_remote_copy(src_ref=o_ref.at[slab(src)], dst_ref=o_ref.at[slab(src)],
                                         send_sem=ssem.at[1, d], recv_sem=rsem.at[1, d],
                                         device_id=dev(src), device_id_type=MESH).wait_recv()
        for cp in sends:
            cp.wait_send()

    return _call(body, in_specs=[VMEM_FULL], out_specs=VMEM_FULL, out_shape=jax.ShapeDtypeStruct((R, L), F32),
                 scratch_shapes=[pltpu.VMEM((N_DEV, RB, L), F32), pltpu.SemaphoreType.DMA((2, N_DEV)),
                                 pltpu.SemaphoreType.DMA((2, N_DEV))],
                 name="allreduce_small")(buf)


def _block_diag(t, gt):
    G, A, B = t.shape
    t4 = t.reshape(G // gt, gt, A, B)
    eye = jnp.eye(gt, dtype=t.dtype)
    return jnp.einsum('sgab,gh->sgahb', t4, eye).reshape(G // gt, gt * A, gt * B)


def _block_diag_extract(m, gt, A, B):
    S = m.shape[0]
    m5 = m.reshape(S, gt, A, gt, B)
    eye = jnp.eye(gt, dtype=m.dtype)
    return jnp.einsum('sgahb,gh->sgab', m5, eye).reshape(S * gt, A, B)


def _pack_small(arrs, rows):
    flat = jnp.concatenate([a.reshape(-1).astype(F32) for a in arrs])
    return jnp.pad(flat, (0, rows * LANES - flat.shape[0])).reshape(rows, LANES)


def _unpack_small(buf, shapes):
    flat = buf.reshape(-1)
    out, off = [], 0
    for s in shapes:
        n = 1
        for d in s:
            n *= d
        out.append(flat[off:off + n].reshape(s))
        off += n
    return out


def kernel(x, p, norm_gain, w_in, w_pool, pool_scale, a_re, a_im, log_dt, b_re, b_im, c_re, c_im, d_skip, w_glu, w_out, w_ple, w_ple_gate, final_gain, loss_target, m_norm_gain, m_w_in, m_w_pool, m_pool_scale, m_a_re, m_a_im, m_log_dt, m_b_re, m_b_im, m_c_re, m_c_im, m_d_skip, m_w_glu, m_w_out, m_w_ple, m_w_ple_gate, m_final_gain, v_norm_gain, v_w_in, v_w_pool, v_pool_scale, v_a_re, v_a_im, v_log_dt, v_b_re, v_b_im, v_c_re, v_c_im, v_d_skip, v_w_glu, v_w_out, v_w_ple, v_w_ple_gate, v_final_gain):
    xs, pe, tgt = x[0], p[0, 0], loss_target[0]
    T, D = xs.shape
    E = pe.shape[1]
    P = D // 2
    NG = len(POOL_WINDOWS)
    PG = P // NG
    G, N, C = P // SSM_GROUP, SSM_STATE, SSM_GROUP
    GT = min(SSM_TILE_GROUPS, G)
    Q = D // N_CHIP

    big = {"w_in": (w_in, m_w_in, v_w_in), "w_pool": (w_pool, m_w_pool, v_w_pool),
           "w_glu": (w_glu, m_w_glu, v_w_glu), "w_out": (w_out, m_w_out, v_w_out),
           "w_ple": (w_ple, m_w_ple, v_w_ple), "w_ple_gate": (w_ple_gate, m_w_ple_gate, v_w_ple_gate)}
    big_names = list(big)
    shard2d = {n: (big[n][0].size // big[n][0].shape[-1], big[n][0].shape[-1]) for n in big_names}
    shard_axis = {"w_in": 1, "w_pool": 1, "w_glu": 1, "w_out": 0, "w_ple": 1, "w_ple_gate": 0}
    shard16 = {n: big[n][0][0].astype(BF16) for n in big_names}
    place = jnp.stack([2 * lax.axis_index("x") + lax.axis_index("y"), lax.axis_index("c")]).astype(jnp.int32)
    win, = _comm_call(_ag_plan([shard16["w_in"]], [shard_axis["w_in"]]), "ag_w_in")
    later = [n for n in big_names if n != "w_in"]
    ag_later = _ag_plan([shard16[n] for n in later], [shard_axis[n] for n in later])

    rep = lambda a: jnp.repeat(a, C, axis=0)
    a_re_r, a_im_r = rep(a_re[0]), rep(a_im[0])
    ldt_r = rep(jnp.broadcast_to(log_dt[0][:, None], (G, N)))
    bt_re = b_re[0].transpose(0, 2, 1).reshape(G * C, N)
    bt_im = b_im[0].transpose(0, 2, 1).reshape(G * C, N)
    ab_re_r, ab_im_r, bbt_re, bbt_im = _ssm_prep(a_re_r, a_im_r, ldt_r, bt_re, bt_im)
    abr = ab_re_r[::C].reshape(1, G * N)
    abi = ab_im_r[::C].reshape(1, G * N)
    bdr = _block_diag(bbt_re.reshape(G, C, N), GT).astype(BF16)
    bdi = _block_diag(bbt_im.reshape(G, C, N), GT).astype(BF16)
    cdr = _block_diag(c_re[0].transpose(0, 2, 1), GT).astype(BF16)
    cdi = _block_diag(c_im[0].transpose(0, 2, 1), GT).astype(BF16)

    tb = _t(T, 256)
    tbs = _t(T, 256)
    tm = _t(T, 1024)
    tk = _t(T, 2048)
    DH = _t(D, 1024)
    row_k = lambda i, n, k: (i, k)
    row_n = lambda i, n, k: (i, n)
    f32 = lambda *shape: jax.ShapeDtypeStruct(shape, F32)
    hn = _norm1(xs, norm_gain, tb)
    proj = _mm(hn, win, dims=NN, grid=(T // tm, N_CHIP, 1),
               a_spec=_bs((tm, D), row_k), b_spec=_bs((D, P), lambda i, n, k: (k, n)),
               o_spec=_bs((tm, P), row_n), out_shape=f32(T, 4 * P), name="mm_proj")
    y, ge, bsr, bsi, wp, wglu, wout, wple, wpg = _ssm_fwd(proj, bdr, bdi, cdr, cdi, abr, abi, d_skip, P, tbs,
                                                          comm=ag_later)
    pooled = _pool_fwd(proj, P, tb)
    mixed = _mm(pooled, wp, dims=NN, grid=(T // tm, NG, 1),
                a_spec=_bs((tm, PG), row_n), b_spec=_bs((None, PG, PG), lambda i, g, k: (g, 0, 0)),
                o_spec=_bs((tm, PG), row_n), out_shape=f32(T, P), name="mm_pool")
    hg = _mm(ge, wglu, dims=NN, grid=(T // tm, 2 * P // DH, 1),
             a_spec=_bs((tm, P), row_k), b_spec=_bs((P, DH), lambda i, n, k: (k, n)),
             o_spec=_bs((tm, DH), row_n), out_shape=f32(T, 2 * P), name="mm_glu")
    cat = _gate_fwd(mixed, proj, hg, pool_scale, tb)
    h1, h1b = _mm(cat, wout, dims=NN, grid=(T // tm, D // DH, 1), res=xs, bf16_copy=True,
                  a_spec=_bs((tm, D), row_k), b_spec=_bs((D, DH), lambda i, n, k: (k, n)),
                  r_spec=_bs((tm, DH), row_n), o_spec=_bs((tm, DH), row_n), out_shape=f32(T, D), name="mm_out")
    e = _mm(pe, wple, dims=NN, grid=(T // tm, D // DH, 1),
            a_spec=_bs((tm, E), row_k), b_spec=_bs((E, DH), lambda i, n, k: (k, n)),
            o_spec=_bs((tm, DH), row_n), out_shape=f32(T, D), name="mm_ple")
    z = _mm(h1b, wpg, dims=NN, grid=(T // tm, D // DH, 1),
            a_spec=_bs((tm, D), row_k), b_spec=_bs((D, DH), lambda i, n, k: (k, n)),
            o_spec=_bs((tm, DH), row_n), out_shape=f32(T, D), name="mm_pgate")
    dh2, de, dz, dg2, lpart = _final_fb(h1, e, z, tgt, final_gain.reshape(1, D), tb)

    col_m = lambda m, n, k: (k, m)
    col_n = lambda m, n, k: (k, n)
    dh1, dh1b = _mm(dz, wpg, dims=NT, grid=(T // tm, D // DH, 1), res=dh2, bf16_copy=True,
                    a_spec=_bs((tm, D), row_k), b_spec=_bs((DH, D), lambda i, n, k: (n, k)),
                    r_spec=_bs((tm, DH), row_n), o_spec=_bs((tm, DH), row_n), out_shape=f32(T, D), name="mm_dh1")
    g_wpg = _mm(h1b, dz, dims=TN, grid=(D // DH, D // DH, T // tk),
                a_spec=_bs((tk, DH), col_m), b_spec=_bs((tk, DH), col_n),
                o_spec=_bs((DH, DH), lambda m, n, k: (m, n)), out_shape=f32(D, D), name="mm_gwpg")
    g_wple = _mm(pe, de, dims=TN, grid=(1, N_CHIP, T // tk),
                 a_spec=_bs((tk, E), col_m), b_spec=_bs((tk, Q), col_n),
                 o_spec=_bs((None, E, Q), lambda m, j, k: (j, 0, 0)), out_shape=f32(N_CHIP, E, Q), name="mm_gwple")
    dcat = _mm(dh1b, wout, dims=NT, grid=(T // tm, D // DH, 1),
               a_spec=_bs((tm, D), row_k), b_spec=_bs((DH, D), lambda i, n, k: (n, k)),
               o_spec=_bs((tm, DH), row_n), out_shape=f32(T, D), name="mm_dcat")
    g_wout = _mm(cat, dh1b, dims=TN, grid=(D // DH, D // DH, T // tk),
                 a_spec=_bs((tk, DH), col_m), b_spec=_bs((tk, DH), col_n),
                 o_spec=_bs((DH, DH), lambda m, n, k: (m, n)), out_shape=f32(D, D), name="mm_gwout")
    gbig = {"w_out": g_wout.reshape(N_CHIP, Q, D), "w_ple": g_wple, "w_ple_gate": g_wpg.reshape(N_CHIP, Q, D)}
    first = list(gbig)
    res = _gate_bwd(dcat, mixed, proj, hg, pool_scale, tb, comm=_halves_plan([gbig[n] for n in first]))
    dmixed, dpg, dsg, dhg, dps = res[:5]
    got = dict(zip(first, res[5:]))
    dge = _mm(dhg, wglu, dims=NT, grid=(T // tm, 1, 1),
              a_spec=_bs((tm, 2 * P), row_k), b_spec=_bs((P, 2 * P), lambda i, n, k: (n, k)),
              o_spec=_bs((tm, P), row_n), out_shape=f32(T, P), name="mm_dge")
    g_wglu = _mm(ge, dhg, dims=TN, grid=(1, N_CHIP, T // tk),
                 a_spec=_bs((tk, P), col_m), b_spec=_bs((tk, Q), col_n),
                 o_spec=_bs((None, P, Q), lambda m, j, k: (j, 0, 0)), out_shape=f32(N_CHIP, P, Q), name="mm_gwglu")
    dpooled = _mm(dmixed, wp, dims=NT, grid=(T // tm, NG, 1),
                  a_spec=_bs((tm, PG), row_n), b_spec=_bs((None, PG, PG), lambda i, g, k: (g, 0, 0)),
                  o_spec=_bs((tm, PG), row_n), out_shape=f32(T, P), name="mm_dpooled")
    g_wp = _mm(pooled, dmixed, dims=TN, grid=(NG, 1, T // tk),
               a_spec=_bs((tk, PG), col_m), b_spec=_bs((tk, PG), col_m),
               o_spec=_bs((None, PG, PG), lambda g, n, k: (g, 0, 0)), out_shape=f32(NG, PG, PG), name="mm_gwp")
    gbig["w_pool"] = g_wp.reshape(NG, N_CHIP, PG // N_CHIP, PG).transpose(1, 0, 2, 3).reshape(
        N_CHIP, NG * PG // N_CHIP, PG)
    gbig["w_glu"] = g_wglu
    res = _pool_bwd(dpooled, tb, comm=_halves_plan([gbig["w_pool"], gbig["w_glu"]]))
    dpi, got["w_pool"], got["w_glu"] = res
    early = list(gbig)
    chip_sums = {n: _sum_cast(gbig[n], got[n], place, "sum_cast_" + n) for n in early}
    res = _ssm_bwd(proj, y, dge, bsr, bsi, bdr, bdi, cdr, cdi, abr, abi, d_skip, dpi, dpg, dsg, P, tbs,
                   comm=_scatter_plan([chip_sums[n] for n in early]))
    dproj, dabr, dabi, dd, dbdr, dbdi, dcdr, dcdi = res[:8]
    arrived = dict(zip(early, res[8:]))
    gbig["w_in"] = _mm(hn, dproj, dims=TN, grid=(D // DH, N_CHIP, T // tk),
                       a_spec=_bs((tk, DH), col_m), b_spec=_bs((tk, P), col_n),
                       o_spec=_bs((None, DH, P), lambda m, j, k: (j, m, 0)), out_shape=f32(N_CHIP, D, P),
                       name="mm_gwin")
    got["w_in"], = _comm_call(_halves_plan([gbig["w_in"]]), "rs_halves_late")
    chip_sums["w_in"] = _sum_cast(gbig["w_in"], got["w_in"], place, "sum_cast_w_in")
    KH = _t(4 * P, 2048)
    dhn, arrived["w_in"] = _mm(dproj, win, dims=NT, grid=(T // tm, D // DH, 4 * P // KH),
                               a_spec=_bs((tm, KH), row_k), b_spec=_bs((DH, KH), lambda i, n, k: (n, k)),
                               o_spec=_bs((tm, DH), row_n), out_shape=f32(T, D), name="mm_dhn",
                               comm=_scatter_plan([chip_sums["w_in"]]))
    grad_x, dg1 = _norm1_bwd(xs, dhn, dh1, norm_gain, tb)

    dbbt_re = _block_diag_extract(dbdr, GT, C, N).reshape(G * C, N)
    dbbt_im = _block_diag_extract(dbdi, GT, C, N).reshape(G * C, N)
    g_c_re = _block_diag_extract(dcdr, GT, N, C).transpose(0, 2, 1)
    g_c_im = _block_diag_extract(dcdi, GT, N, C).transpose(0, 2, 1)
    dab_re_r = rep(dabr.reshape(G, N)) * (1.0 / C)
    dab_im_r = rep(dabi.reshape(G, N)) * (1.0 / C)
    g_a_re, g_a_im, g_ldt, g_bt_re, g_bt_im = _ssm_prep_bwd(a_re_r, a_im_r, ldt_r, bt_re, bt_im,
                                                            dab_re_r, dab_im_r, dbbt_re, dbbt_im, G)
    g_b_re = g_bt_re.reshape(G, C, N).transpose(0, 2, 1)
    g_b_im = g_bt_im.reshape(G, C, N).transpose(0, 2, 1)

    halves = [_sum_chips(chip_sums[n], arrived[n], place, "sum_chips_" + n) for n in big_names]
    gshard = _join_halves(halves)

    small_names = ["norm_gain", "pool_scale", "a_re", "a_im", "log_dt", "b_re", "b_im", "c_re", "c_im",
                   "d_skip", "final_gain"]
    small_w = dict(norm_gain=norm_gain, pool_scale=pool_scale, a_re=a_re, a_im=a_im, log_dt=log_dt, b_re=b_re,
                   b_im=b_im, c_re=c_re, c_im=c_im, d_skip=d_skip, final_gain=final_gain)
    small_m = dict(norm_gain=m_norm_gain, pool_scale=m_pool_scale, a_re=m_a_re, a_im=m_a_im, log_dt=m_log_dt,
                   b_re=m_b_re, b_im=m_b_im, c_re=m_c_re, c_im=m_c_im, d_skip=m_d_skip, final_gain=m_final_gain)
    small_v = dict(norm_gain=v_norm_gain, pool_scale=v_pool_scale, a_re=v_a_re, a_im=v_a_im, log_dt=v_log_dt,
                   b_re=v_b_re, b_im=v_b_im, c_re=v_c_re, c_im=v_c_im, d_skip=v_d_skip, final_gain=v_final_gain)
    small_g = dict(norm_gain=dg1, pool_scale=dps, a_re=g_a_re, a_im=g_a_im, log_dt=g_ldt, b_re=g_b_re,
                   b_im=g_b_im, c_re=g_c_re, c_im=g_c_im, d_skip=dd, final_gain=dg2)
    shapes = [small_w[n].shape for n in small_names]
    total = sum(small_w[n].size for n in small_names) + 1
    unit = N_DEV * SUBLANES
    rows = -(-(-(-total // LANES)) // unit) * unit
    gbuf = _pack_small([small_g[n] for n in small_names] + [lpart[0, :1]], rows)
    gsum = _allreduce_small(gbuf)
    wbuf = _pack_small([small_w[n] for n in small_names], rows)
    mbuf = _pack_small([small_m[n] for n in small_names], rows)
    vbuf = _pack_small([small_v[n] for n in small_names], rows)
    dsm, msm, vsm = _adamw(wbuf, gsum, mbuf, vbuf, "adamw_small")
    g_small = dict(zip(small_names, _unpack_small(gsum, shapes)))
    d_small = dict(zip(small_names, _unpack_small(dsm, shapes)))
    m_small = dict(zip(small_names, _unpack_small(msm, shapes)))
    v_small = dict(zip(small_names, _unpack_small(vsm, shapes)))
    loss = gsum.reshape(-1)[total - 1]

    g_out, d_out, m_out, v_out = dict(g_small), dict(d_small), dict(m_small), dict(v_small)
    for n, gs in zip(big_names, gshard):
        w_, m_, v_ = big[n]
        r2 = shard2d[n]
        d_, mn_, vn_ = _adamw(w_.reshape(r2), gs, m_.reshape(r2), v_.reshape(r2), "adamw_" + n)
        g_out[n], d_out[n], m_out[n], v_out[n] = (a.reshape(w_.shape) for a in (gs, d_, mn_, vn_))

    order = ["norm_gain", "w_in", "w_pool", "pool_scale", "a_re", "a_im", "log_dt", "b_re", "b_im", "c_re",
             "c_im", "d_skip", "w_glu", "w_out", "w_ple", "w_ple_gate", "final_gain"]
    return (loss, grad_x[None], *[g_out[n] for n in order], *[d_out[n] for n in order],
            *[m_out[n] for n in order], *[v_out[n] for n in order])
```

```python
import functools

import jax
import jax.numpy as jnp
from jax import lax
from jax.experimental import pallas as pl
from jax.experimental.pallas import tpu as pltpu

F32, BF16 = jnp.float32, jnp.bfloat16
MESH = pl.DeviceIdType.MESH
ANY = pl.BlockSpec(memory_space=pl.ANY)
VMEM_FULL = pl.BlockSpec(memory_space=pltpu.VMEM)

EPS = 1e-6
A_RE_MAX = -1e-4
SSM_GROUP = 16
SSM_STATE = 64
POOL_WINDOWS = (2, 4, 8, 16)
POOL_HALO = 16
ADAM_LR, ADAM_B1, ADAM_B2, ADAM_EPS, ADAM_WD, ADAM_STEP = 0.001, 0.9, 0.999, 1e-08, 0.01, 10

V7X_VMEM_BYTES = 64 * 1024 * 1024
VMEM_LIMIT = V7X_VMEM_BYTES - 8 * 1024 * 1024
SUBLANES, LANES = 8, 128
SSM_TILE_GROUPS = 8
SCAN_LANES = 512
N_DEV, N_CHIP = 8, 4
DMA_CHUNK_BYTES = 256 * 1024
DMA_MAX_CHUNKS = 32
AG_CHUNKS = 8
RS_CHUNKS = 8


def _t(n, pref):
    return pref if n % pref == 0 else n


def _cp(sem=None, vmem=VMEM_LIMIT):
    return pltpu.CompilerParams(dimension_semantics=sem, vmem_limit_bytes=vmem)


def _call(body, **kw):
    return pl.pallas_call(body, **kw)


NN = ((1,), (0,))
NT = ((1,), (1,))
TN = ((0,), (0,))


def _mm(a, b, *, dims, grid, a_spec, b_spec, o_spec, out_shape, name, res=None, r_spec=None, bf16_copy=False,
        comm=None):
    nk, kax = grid[-1], len(grid) - 1
    acc_shape = tuple(d for d in o_spec.block_shape if d is not None)

    def core(*refs):
        refs = list(refs)
        a_ref, b_ref = refs[:2]
        r_ref = refs[2] if res is not None else None
        outs = refs[3 if res is not None else 2:]
        o_ref = outs[0]
        o2_ref = outs[1] if bf16_copy else None
        acc = outs[-1] if nk > 1 else None

        def finish(r):
            if r_ref is not None:
                r = r + r_ref[...]
            o_ref[...] = r.astype(o_ref.dtype)
            if o2_ref is not None:
                o2_ref[...] = r.astype(BF16)

        part = lax.dot_general(a_ref[...].astype(BF16), b_ref[...].astype(BF16),
                               (dims, ((), ())), preferred_element_type=F32)
        if nk == 1:
            finish(part)
        else:
            k = pl.program_id(kax)

            @pl.when(k == 0)
            def _():
                acc[...] = part

            @pl.when(k > 0)
            def _():
                acc[...] += part

            @pl.when(k == nk - 1)
            def _():
                finish(acc[...])

    ins, specs = [a, b], [a_spec, b_spec]
    if res is not None:
        ins.append(res)
        specs.append(r_spec)
    o_specs, o_shapes = [o_spec], [out_shape]
    if bf16_copy:
        o_specs = [o_spec, o_spec]
        o_shapes = [out_shape, jax.ShapeDtypeStruct(out_shape.shape, BF16)]
    scratch = [pltpu.VMEM(acc_shape, F32)] if nk > 1 else []
    body, extra = _hosted(core, comm, grid, len(ins), len(o_specs), len(scratch))
    sem = ("arbitrary",) * len(grid) if comm else ("parallel",) * kax + ("arbitrary",)
    outs = _call(body, grid=grid, in_specs=specs + extra["in_specs"], out_specs=o_specs + extra["out_specs"],
                 out_shape=o_shapes + extra["out_shape"], scratch_shapes=scratch + extra["scratch"],
                 compiler_params=_cp(sem), name=name)(*ins, *extra["ins"])
    return outs[0] if len(outs) == 1 else outs


def _bs(shape, fn):
    return pl.BlockSpec(shape, fn)


def _sigmoid(v):
    return 1.0 / (1.0 + jnp.exp(-v))


def _gelu(v):
    return 0.5 * v * (1.0 + jnp.tanh(0.7978845608028654 * (v + 0.044715 * v * v * v)))


def _gelu_grad(v):
    t = jnp.tanh(0.7978845608028654 * (v + 0.044715 * v * v * v))
    return 0.5 * (1.0 + t) + 0.5 * v * (1.0 - t * t) * 0.7978845608028654 * (1.0 + 3 * 0.044715 * v * v)


def _norm1(x, g1, tb):
    T, D = x.shape

    def body(x_ref, g_ref, o_ref):
        xv = x_ref[...]
        r = lax.rsqrt(jnp.mean(xv * xv, axis=-1, keepdims=True) + EPS)
        o_ref[...] = ((xv * r) * g_ref[...]).astype(BF16)

    return _call(body, grid=(T // tb,),
                 in_specs=[_bs((tb, D), lambda i: (i, 0)), _bs((1, D), lambda i: (0, 0))],
                 out_specs=_bs((tb, D), lambda i: (i, 0)), out_shape=jax.ShapeDtypeStruct((T, D), BF16),
                 compiler_params=_cp(("parallel",)), name="norm1")(x, g1)


def _norm1_bwd(x, dhn, dh1, g1, tb):
    T, D = x.shape

    def body(x_ref, dhn_ref, dh1_ref, g_ref, dx_ref, dg_ref):
        @pl.when(pl.program_id(0) == 0)
        def _():
            dg_ref[...] = jnp.zeros_like(dg_ref)

        xv = x_ref[...]
        r = lax.rsqrt(jnp.mean(xv * xv, axis=-1, keepdims=True) + EPS)
        xh = xv * r
        dhn_v = dhn_ref[...]
        dg_ref[...] += jnp.sum(dhn_v * xh, axis=0, keepdims=True)
        dxh = dhn_v * g_ref[...]
        dx_ref[...] = dh1_ref[...] + r * (dxh - xh * jnp.mean(dxh * xh, axis=-1, keepdims=True))

    row = _bs((tb, D), lambda i: (i, 0))
    vec = _bs((1, D), lambda i: (0, 0))
    return _call(body, grid=(T // tb,), in_specs=[row, row, row, vec], out_specs=[row, vec],
                 out_shape=[jax.ShapeDtypeStruct((T, D), F32), jax.ShapeDtypeStruct((1, D), F32)],
                 compiler_params=_cp(("arbitrary",)), name="norm1_bwd")(x, dhn, dh1, g1)


def _gate_fwd(mixed, proj, hg, ps, tb):
    T, P = mixed.shape

    def body(mx_ref, pg_ref, sg_ref, hg_ref, ps_ref, o_ref):
        pg, sg = pg_ref[...], sg_ref[...]
        ya = (mx_ref[...] * ps_ref[...]) * (pg * _sigmoid(pg))
        hgv = hg_ref[...]
        o = hgv[:, :P] * _sigmoid(hgv[:, P:])
        yb = o * (sg * _sigmoid(sg))
        o_ref[:, :P] = ya.astype(BF16)
        o_ref[:, P:] = yb.astype(BF16)

    return _call(body, grid=(T // tb,),
                 in_specs=[_bs((tb, P), lambda i: (i, 0)), _bs((tb, P), lambda i: (i, 1)),
                           _bs((tb, P), lambda i: (i, 3)), _bs((tb, 2 * P), lambda i: (i, 0)),
                           _bs((1, P), lambda i: (0, 0))],
                 out_specs=_bs((tb, 2 * P), lambda i: (i, 0)),
                 out_shape=jax.ShapeDtypeStruct((T, 2 * P), BF16),
                 compiler_params=_cp(("parallel",)), name="gate_fwd")(mixed, proj, proj, hg, ps)


def _gate_bwd(dcat, mixed, proj, hg, ps, tb, comm=None):
    T, P = mixed.shape

    def core(dc_ref, mx_ref, pg_ref, sg_ref, hg_ref, ps_ref, dmx_ref, dpg_ref, dsg_ref, dhg_ref, dps_ref):
        @pl.when(pl.program_id(0) == 0)
        def _():
            dps_ref[...] = jnp.zeros_like(dps_ref)

        dc = dc_ref[...]
        dya, dyb = dc[:, :P], dc[:, P:]
        pg, sg, mx, psv = pg_ref[...], sg_ref[...], mx_ref[...], ps_ref[...]
        s_pg = _sigmoid(pg)
        dpa = dya * (pg * s_pg)
        dpg_ref[...] = (dya * (mx * psv) * (s_pg * (1.0 + pg * (1.0 - s_pg)))).astype(BF16)
        dps_ref[...] += jnp.sum(dpa * mx, axis=0, keepdims=True)
        dmx_ref[...] = (dpa * psv).astype(BF16)
        hgv = hg_ref[...]
        h1, s_h2 = hgv[:, :P], _sigmoid(hgv[:, P:])
        s_sg = _sigmoid(sg)
        do = dyb * (sg * s_sg)
        dsg_ref[...] = (dyb * (h1 * s_h2) * (s_sg * (1.0 + sg * (1.0 - s_sg)))).astype(BF16)
        dhg_ref[:, :P] = (do * s_h2).astype(BF16)
        dhg_ref[:, P:] = (do * h1 * s_h2 * (1.0 - s_h2)).astype(BF16)

    rowp = _bs((tb, P), lambda i: (i, 0))
    row2 = _bs((tb, 2 * P), lambda i: (i, 0))
    vec = _bs((1, P), lambda i: (0, 0))
    body, extra = _hosted(core, comm, (T // tb,), 6, 5, 0)
    return _call(body, grid=(T // tb,),
                 in_specs=[row2, rowp, _bs((tb, P), lambda i: (i, 1)), _bs((tb, P), lambda i: (i, 3)), row2, vec]
                 + extra["in_specs"],
                 out_specs=[rowp, rowp, rowp, row2, vec] + extra["out_specs"],
                 out_shape=[jax.ShapeDtypeStruct((T, P), BF16), jax.ShapeDtypeStruct((T, P), BF16),
                            jax.ShapeDtypeStruct((T, P), BF16), jax.ShapeDtypeStruct((T, 2 * P), BF16),
                            jax.ShapeDtypeStruct((1, P), F32)] + extra["out_shape"],
                 scratch_shapes=extra["scratch"],
                 compiler_params=_cp(("arbitrary",)), name="gate_bwd")(dcat, mixed, proj, proj, hg, ps, *extra["ins"])


def _final_fb(h1, e, z, tgt, g2, tb):
    T, D = h1.shape

    def body(h1_ref, e_ref, z_ref, t_ref, g_ref, dh2_ref, de_ref, dz_ref, dg_ref, l_ref):
        @pl.when(pl.program_id(0) == 0)
        def _():
            dg_ref[...] = jnp.zeros_like(dg_ref)
            l_ref[...] = jnp.zeros_like(l_ref)

        ev = e_ref[...]
        s = _sigmoid(z_ref[...])
        h2 = h1_ref[...] + ev * s
        r = lax.rsqrt(jnp.mean(h2 * h2, axis=-1, keepdims=True) + EPS)
        xh = h2 * r
        gv = g_ref[...]
        diff = xh * gv - t_ref[...]
        l_ref[...] += 0.5 * jnp.sum(jnp.mean(diff * diff, axis=-1, keepdims=True))
        dout = diff * (1.0 / D)
        dg_ref[...] += jnp.sum(dout * xh, axis=0, keepdims=True)
        dxh = dout * gv
        dh2 = r * (dxh - xh * jnp.mean(dxh * xh, axis=-1, keepdims=True))
        dh2_ref[...] = dh2
        de_ref[...] = (dh2 * s).astype(BF16)
        dz_ref[...] = (dh2 * ev * s * (1.0 - s)).astype(BF16)

    row = _bs((tb, D), lambda i: (i, 0))
    vec = _bs((1, D), lambda i: (0, 0))
    return _call(body, grid=(T // tb,), in_specs=[row, row, row, row, vec],
                 out_specs=[row, row, row, vec, _bs((1, LANES), lambda i: (0, 0))],
                 out_shape=[jax.ShapeDtypeStruct((T, D), F32), jax.ShapeDtypeStruct((T, D), BF16),
                            jax.ShapeDtypeStruct((T, D), BF16), jax.ShapeDtypeStruct((1, D), F32),
                            jax.ShapeDtypeStruct((1, LANES), F32)],
                 compiler_params=_cp(("arbitrary",)), name="final_fb")(h1, e, z, tgt, g2)


def _pool_inv_count(t0, rows, pg, ngroups):
    t = t0 + lax.broadcasted_iota(jnp.int32, (rows, pg), 0)
    parts = []
    for w in POOL_WINDOWS[:ngroups]:
        parts.append(jnp.where(t + 1 >= w, 1.0 / w, 1.0 / (t + 1).astype(F32)))
    return parts


def _pool_fwd(proj, P, tb):
    T = proj.shape[0]
    ng = len(POOL_WINDOWS)
    pg = P // ng
    hb = tb // POOL_HALO

    def body(v_ref, tail_ref, o_ref, ext):
        i = pl.program_id(0)
        ext[pl.ds(0, POOL_HALO), :] = jnp.where(i > 0, tail_ref[...], 0.0)
        ext[pl.ds(POOL_HALO, tb), :] = v_ref[...]
        inv = _pool_inv_count(i * tb, tb, pg, ng)
        for g, w in enumerate(POOL_WINDOWS):
            cols = pl.ds(g * pg, pg)
            win = ext[pl.ds(POOL_HALO, tb), cols]
            for k in range(1, w):
                win = win + ext[pl.ds(POOL_HALO - k, tb), cols]
            o_ref[:, cols] = (win * inv[g] - ext[pl.ds(POOL_HALO, tb), cols]).astype(BF16)

    return _call(body, grid=(T // tb,),
                 in_specs=[_bs((tb, P), lambda i: (i, 0)),
                           _bs((POOL_HALO, P), lambda i: (jnp.maximum(i * hb - 1, 0), 0))],
                 out_specs=_bs((tb, P), lambda i: (i, 0)), out_shape=jax.ShapeDtypeStruct((T, P), BF16),
                 scratch_shapes=[pltpu.VMEM((tb + POOL_HALO, P), F32)],
                 compiler_params=_cp(("arbitrary",)), name="pool_fwd")(proj, proj)


def _pool_bwd(dpooled, tb, comm=None):
    T, P = dpooled.shape
    ng = len(POOL_WINDOWS)
    pg = P // ng
    hb = tb // POOL_HALO
    nb = T // tb

    def core(d_ref, head_ref, o_ref, ext):
        i = pl.program_id(0)
        inv = _pool_inv_count(i * tb, tb, pg, ng)
        invh = _pool_inv_count((i + 1) * tb, POOL_HALO, pg, ng)
        for g in range(ng):
            cols = pl.ds(g * pg, pg)
            ext[pl.ds(0, tb), cols] = d_ref[:, cols] * inv[g]
            ext[pl.ds(tb, POOL_HALO), cols] = jnp.where(i < nb - 1, head_ref[:, cols] * invh[g], 0.0)
        for g, w in enumerate(POOL_WINDOWS):
            cols = pl.ds(g * pg, pg)
            acc = ext[pl.ds(0, tb), cols]
            for k in range(1, w):
                acc = acc + ext[pl.ds(k, tb), cols]
            o_ref[:, cols] = (acc - d_ref[:, cols]).astype(BF16)

    body, extra = _hosted(core, comm, (nb,), 2, 1, 1)
    return _call(body, grid=(nb,),
                 in_specs=[_bs((tb, P), lambda i: (i, 0)),
                           _bs((POOL_HALO, P), lambda i: (jnp.minimum((i + 1) * hb, T // POOL_HALO - 1), 0))]
                 + extra["in_specs"],
                 out_specs=[_bs((tb, P), lambda i: (i, 0))] + extra["out_specs"],
                 out_shape=[jax.ShapeDtypeStruct((T, P), BF16)] + extra["out_shape"],
                 scratch_shapes=[pltpu.VMEM((tb + POOL_HALO, P), F32)] + extra["scratch"],
                 compiler_params=_cp(("arbitrary",)), name="pool_bwd")(dpooled, dpooled, *extra["ins"])


def _zoh(a_re, a_im, ldt, b_re, b_im):
    lam_re = jnp.minimum(a_re, A_RE_MAX)
    lam_im = a_im
    dt = jnp.exp(ldt)
    mag = jnp.exp(lam_re * dt)
    ang = lam_im * dt
    ab_re = mag * jnp.cos(ang)
    ab_im = mag * jnp.sin(ang)
    den = lam_re * lam_re + lam_im * lam_im
    n_re = ab_re - 1.0
    n_im = ab_im
    q_re = (n_re * lam_re + n_im * lam_im) / den
    q_im = (n_im * lam_re - n_re * lam_im) / den
    return ab_re, ab_im, q_re * b_re - q_im * b_im, q_re * b_im + q_im * b_re


def _ssm_prep(a_re, a_im, ldt, bt_re, bt_im):
    shp = jax.ShapeDtypeStruct(a_re.shape, F32)

    def body(a, b, c, d, e, o0, o1, o2, o3):
        r = _zoh(a[...], b[...], c[...], d[...], e[...])
        o0[...], o1[...], o2[...], o3[...] = r

    return _call(body, in_specs=[VMEM_FULL] * 5, out_specs=[VMEM_FULL] * 4, out_shape=[shp] * 4,
                 name="ssm_prep")(a_re, a_im, ldt, bt_re, bt_im)


def _ssm_prep_bwd(a_re, a_im, ldt, bt_re, bt_im, dab_re, dab_im, dbb_re, dbb_im, G):
    GC, N = a_re.shape
    C = GC // G

    def body(a, b, c, d, e, g0, g1, g2, g3, da_re, da_im, dldt, db_re, db_im):
        _, vjp = jax.vjp(_zoh, a[...], b[...], c[...], d[...], e[...])
        ga_re, ga_im, gl, gb_re, gb_im = vjp((g0[...], g1[...], g2[...], g3[...]))
        da_re[...] = jnp.sum(ga_re.reshape(G, C, N), axis=1)
        da_im[...] = jnp.sum(ga_im.reshape(G, C, N), axis=1)
        dldt[...] = jnp.sum(jnp.sum(gl.reshape(G, C, N), axis=1), axis=1, keepdims=True)
        db_re[...] = gb_re
        db_im[...] = gb_im

    gn = jax.ShapeDtypeStruct((G, N), F32)
    full = jax.ShapeDtypeStruct((GC, N), F32)
    return _call(body, in_specs=[VMEM_FULL] * 9, out_specs=[VMEM_FULL] * 5,
                 out_shape=[gn, gn, jax.ShapeDtypeStruct((G, 1), F32), full, full],
                 name="ssm_prep_bwd")(a_re, a_im, ldt, bt_re, bt_im, dab_re, dab_im, dbb_re, dbb_im)


def _coef_tiles(abr, abi, reverse):
    ns = abr.shape[1]
    row = lax.broadcasted_iota(jnp.int32, (SUBLANES, ns), 0)
    ar = jnp.broadcast_to(abr, (SUBLANES, ns))
    ai = jnp.broadcast_to(-abi if reverse else abi, (SUBLANES, ns))
    a2r, a2i = ar * ar - ai * ai, 2.0 * ar * ai
    a4r, a4i = a2r * a2r - a2i * a2i, 2.0 * a2r * a2i
    out = []
    for d, (vr, vi) in ((1, (ar, ai)), (2, (a2r, a2i)), (4, (a4r, a4i))):
        keep = (row < SUBLANES - d) if reverse else (row >= d)
        out += [jnp.where(keep, vr, 0.0), jnp.where(keep, vi, 0.0)]
    pr, pi = ar, ai
    for k in range(1, SUBLANES):
        sel = (row <= SUBLANES - 1 - k) if reverse else (row >= k)
        nr, ni = pr * ar - pi * ai, pr * ai + pi * ar
        pr, pi = jnp.where(sel, nr, pr), jnp.where(sel, ni, pi)
    return out + [pr, pi]


def _cpow(ar, ai, n):
    out, br, bi = None, ar, ai
    while n:
        if n & 1:
            out = (br, bi) if out is None else (out[0] * br - out[1] * bi, out[0] * bi + out[1] * br)
        br, bi = br * br - bi * bi, 2.0 * br * bi
        n >>= 1
    return out


def _seg_perm_matrix(nrows):
    r = jnp.arange(nrows)
    src = (nrows // SUBLANES) * (r % SUBLANES) + r // SUBLANES
    return (src[:, None] == jnp.arange(nrows)[None, :]).astype(BF16)


def _seg_order_rows(pm, xb):
    return jnp.dot(pm, xb, preferred_element_type=F32).astype(BF16)


def _time_order_rows(pm, x, terms):
    out, rest = None, x
    for t in range(terms):
        piece = rest.astype(BF16)
        part = lax.dot_general(pm, piece, (TN, ((), ())), preferred_element_type=F32)
        out = part if out is None else out + part
        if t + 1 < terms:
            rest = rest - piece.astype(F32)
    return out


def _seg_scan(xr_ref, xi_ref, abr_ref, abi_ref, coef_ref, car_ref, cai_ref, *, nrows, ns, reverse,
              cmat=None, dab=None):
    seg = nrows // SUBLANES
    cw = min(SCAN_LANES, ns)
    row = lax.broadcasted_iota(jnp.int32, (SUBLANES, cw), 0)
    first, last = (SUBLANES - 1, 0) if reverse else (0, SUBLANES - 1)

    def tile(i):
        return pl.ds(pl.multiple_of(((seg - 1 - i) if reverse else i) * SUBLANES, SUBLANES), SUBLANES)

    for cc in range(ns // cw):
        cols = pl.ds(cc * cw, cw)
        ar = jnp.broadcast_to(abr_ref[:, cols], (SUBLANES, cw))
        ai = jnp.broadcast_to(abi_ref[:, cols], (SUBLANES, cw))
        if reverse:
            ai = -ai

        def local(i, x, cols=cols, ar=ar, ai=ai):
            rows = tile(i)
            nr = ar * x[0] - ai * x[1] + xr_ref[rows, cols]
            ni = ar * x[1] + ai * x[0] + xi_ref[rows, cols]
            xr_ref[rows, cols] = nr
            xi_ref[rows, cols] = ni
            return nr, ni

        zero = jnp.zeros((SUBLANES, cw), F32)
        er, ei = lax.fori_loop(0, seg, local, (zero, zero))

        co = [coef_ref[k, :, cols] for k in range(8)]
        for lvl, d in enumerate((1, 2, 4)):
            kr, ki = co[2 * lvl], co[2 * lvl + 1]
            sh = SUBLANES - d if reverse else d
            sr, si = pltpu.roll(er, sh, 0), pltpu.roll(ei, sh, 0)
            er, ei = er + (kr * sr - ki * si), ei + (kr * si + ki * sr)
        c0r, c0i = car_ref[:, cols], cai_ref[:, cols]
        er, ei = er + (co[6] * c0r - co[7] * c0i), ei + (co[6] * c0i + co[7] * c0r)
        nb_shift = SUBLANES - 1 if reverse else 1
        cmr = jnp.where(row == first, c0r, pltpu.roll(er, nb_shift, 0))
        cmi = jnp.where(row == first, c0i, pltpu.roll(ei, nb_shift, 0))
        car_ref[:, cols] = jnp.broadcast_to(er[last:last + 1, :], er.shape)
        cai_ref[:, cols] = jnp.broadcast_to(ei[last:last + 1, :], ei.shape)
        if cmat is not None:
            cmat[0][:, cols] = cmr
            cmat[1][:, cols] = cmi

        w0 = (ar * cmr - ai * cmi, ar * cmi + ai * cmr)
        if dab is None:
            def fix(i, w, cols=cols, ar=ar, ai=ai):
                rows = tile(i)
                xr_ref[rows, cols] = xr_ref[rows, cols] + w[0]
                xi_ref[rows, cols] = xi_ref[rows, cols] + w[1]
                return ar * w[0] - ai * w[1], ar * w[1] + ai * w[0]

            lax.fori_loop(0, seg, fix, w0)
        else:
            s_re, s_im, e_re, e_im, o_re, o_im = dab

            def add(rows, w, pr, pi, acc):
                gr = xr_ref[rows, cols] + w[0]
                gi = xi_ref[rows, cols] + w[1]
                xr_ref[rows, cols] = gr
                xi_ref[rows, cols] = gi
                return acc[0] + (gr * pr + gi * pi), acc[1] + (gi * pr - gr * pi)

            def fix(i, st, cols=cols, ar=ar, ai=ai):
                w, acc = st[:2], st[2:]
                rows = tile(i)
                before = pl.ds(pl.multiple_of((seg - 2 - i) * SUBLANES, SUBLANES), SUBLANES)
                acc = add(rows, w, s_re[before, cols], s_im[before, cols], acc)
                return (ar * w[0] - ai * w[1], ar * w[1] + ai * w[0]) + acc

            st = lax.fori_loop(0, seg - 1, fix, w0 + (zero, zero))
            acc = add(pl.ds(0, SUBLANES), st[:2], e_re[:, cols], e_im[:, cols], st[2:])
            o_re[:, cols] += jnp.sum(acc[0], axis=0, keepdims=True)
            o_im[:, cols] += jnp.sum(acc[1], axis=0, keepdims=True)


def _hosted(core, comm, grid, n_in, n_out, n_scratch):
    ci = len(comm["ins"]) if comm else 0
    co = len(comm["out_shape"]) if comm else 0

    def body(*refs):
        ins, rest = refs[:n_in + ci], refs[n_in + ci:]
        outs, scr = rest[:n_out + co], rest[n_out + co:]
        hooks = functools.partial(_comm_hooks, comm, grid, ins[n_in:], outs[n_out:], scr[n_scratch:])
        hooks(before=True)
        core(*ins[:n_in], *outs[:n_out], *scr[:n_scratch])
        hooks(before=False)

    extra = dict(ins=list(comm["ins"]) if comm else [], in_specs=[ANY] * ci, out_specs=[ANY] * co,
                 out_shape=list(comm["out_shape"]) if comm else [], scratch=list(comm["scratch"]) if comm else [])
    return body, extra


def _ssm_fwd(proj, bdr, bdi, cdr, cdi, abr, abi, dsk, P, tb, comm=None):
    T = proj.shape[0]
    ntl, ct, st = bdr.shape
    ns = ntl * st
    nb = T // tb

    def core(u_ref, bdr_ref, bdi_ref, cdr_ref, cdi_ref, abr_ref, abi_ref, d_ref, pm_ref,
             y_ref, ge_ref, bsr_ref, bsi_ref, sr, si, coef, car, cai, up):
        @pl.when(pl.program_id(0) == 0)
        def _():
            seg_pow = _cpow(abr_ref[...], abi_ref[...], tb // SUBLANES)
            for k, tile in enumerate(_coef_tiles(seg_pow[0], seg_pow[1], False)):
                coef[k] = tile
            car[...] = jnp.zeros_like(car)
            cai[...] = jnp.zeros_like(cai)

        bsr_ref[...] = car[...]
        bsi_ref[...] = cai[...]
        u = u_ref[...]
        ub = _seg_order_rows(pm_ref[...], u.astype(BF16))
        for s in range(ntl):
            us = ub[:, s * ct:(s + 1) * ct]
            sr[:, s * st:(s + 1) * st] = jnp.dot(us, bdr_ref[s], preferred_element_type=F32)
            si[:, s * st:(s + 1) * st] = jnp.dot(us, bdi_ref[s], preferred_element_type=F32)
        _seg_scan(sr, si, abr_ref, abi_ref, coef, car, cai, nrows=tb, ns=ns, reverse=False)
        for s in range(ntl):
            s_re = sr[:, s * st:(s + 1) * st].astype(BF16)
            s_im = si[:, s * st:(s + 1) * st].astype(BF16)
            up[:, s * ct:(s + 1) * ct] = (jnp.dot(s_re, cdr_ref[s], preferred_element_type=F32)
                                          - jnp.dot(s_im, cdi_ref[s], preferred_element_type=F32))
        y = _time_order_rows(pm_ref[...], up[...], 3) + d_ref[...] * u
        y_ref[...] = y
        ge_ref[...] = _gelu(y).astype(BF16)

    full3 = lambda a: _bs(a.shape, lambda i: (0, 0, 0))
    vec = lambda n: _bs((1, n), lambda i: (0, 0))
    row = _bs((tb, P), lambda i: (i, 0))
    st_spec = _bs((None, SUBLANES, ns), lambda i: (i, 0, 0))
    body, extra = _hosted(core, comm, (nb,), 9, 4, 6)
    return _call(body, grid=(nb,),
                 in_specs=[_bs((tb, P), lambda i: (i, 2)), full3(bdr), full3(bdi), full3(cdr), full3(cdi),
                           vec(ns), vec(ns), vec(P), _bs((tb, tb), lambda i: (0, 0))] + extra["in_specs"],
                 out_specs=[row, row, st_spec, st_spec] + extra["out_specs"],
                 out_shape=[jax.ShapeDtypeStruct((T, P), F32), jax.ShapeDtypeStruct((T, P), BF16),
                            jax.ShapeDtypeStruct((nb, SUBLANES, ns), F32),
                            jax.ShapeDtypeStruct((nb, SUBLANES, ns), F32)] + extra["out_shape"],
                 scratch_shapes=[pltpu.VMEM((tb, ns), F32), pltpu.VMEM((tb, ns), F32),
                                 pltpu.VMEM((8, SUBLANES, ns), F32),
                                 pltpu.VMEM((SUBLANES, ns), F32), pltpu.VMEM((SUBLANES, ns), F32),
                                 pltpu.VMEM((tb, P), F32)] + extra["scratch"],
                 compiler_params=_cp(("arbitrary",)), name="ssm_fwd")(
                     proj, bdr, bdi, cdr, cdi, abr, abi, dsk, _seg_perm_matrix(tb), *extra["ins"])


def _ssm_bwd(proj, y, dge, bsr, bsi, bdr, bdi, cdr, cdi, abr, abi, dsk, dpi, dpg, dsg, P, tb, comm=None):
    T = proj.shape[0]
    ntl, ct, st = bdr.shape
    ns = ntl * st
    nb = T // tb

    def core(u_ref, y_ref, dge_ref, bsr_ref, bsi_ref, abr_ref, abi_ref, d_ref, pm_ref, dpi_ref, dpg_ref, dsg_ref,
             bdr_h, bdi_h, cdr_h, cdi_h,
             dproj_ref, dabr_ref, dabi_ref, dd_ref, dbdr_h, dbdi_h, dcdr_h, dcdi_h,
             wbdr, wbdi, wcdr, wcdi, abdr, abdi, acdr, acdi, spr, spi, gr, gi, coef_f, coef_r,
             car, cai, gcr, gci, ser, sei, dup):
        i = pl.program_id(0)

        @pl.when(i == 0)
        def _():
            for h, w in ((bdr_h, wbdr), (bdi_h, wbdi), (cdr_h, wcdr), (cdi_h, wcdi)):
                pltpu.sync_copy(h, w)
            for a in (abdr, abdi, acdr, acdi, gcr, gci):
                a[...] = jnp.zeros_like(a)
            for o in (dabr_ref, dabi_ref, dd_ref):
                o[...] = jnp.zeros_like(o)
            seg_pow = _cpow(abr_ref[...], abi_ref[...], tb // SUBLANES)
            for k, tile in enumerate(_coef_tiles(seg_pow[0], seg_pow[1], False)):
                coef_f[k] = tile
            for k, tile in enumerate(_coef_tiles(seg_pow[0], seg_pow[1], True)):
                coef_r[k] = tile

        car[...] = bsr_ref[...]
        cai[...] = bsi_ref[...]
        u = u_ref[...]
        dy = dge_ref[...] * _gelu_grad(y_ref[...])
        ub = _seg_order_rows(pm_ref[...], u.astype(BF16))
        dyb = _seg_order_rows(pm_ref[...], dy.astype(BF16))
        for s in range(ntl):
            us = ub[:, s * ct:(s + 1) * ct]
            spr[:, s * st:(s + 1) * st] = jnp.dot(us, wbdr[s], preferred_element_type=F32)
            spi[:, s * st:(s + 1) * st] = jnp.dot(us, wbdi[s], preferred_element_type=F32)
        _seg_scan(spr, spi, abr_ref, abi_ref, coef_f, car, cai, nrows=tb, ns=ns, reverse=False, cmat=(ser, sei))

        for s in range(ntl):
            dys = dyb[:, s * ct:(s + 1) * ct]
            gr[:, s * st:(s + 1) * st] = lax.dot_general(dys, wcdr[s], (NT, ((), ())), preferred_element_type=F32)
            gi[:, s * st:(s + 1) * st] = -lax.dot_general(dys, wcdi[s], (NT, ((), ())), preferred_element_type=F32)
        _seg_scan(gr, gi, abr_ref, abi_ref, coef_r, gcr, gci, nrows=tb, ns=ns, reverse=True,
                  dab=(spr, spi, ser, sei, dabr_ref, dabi_ref))

        for s in range(ntl):
            sl_c, sl_s = slice(s * ct, (s + 1) * ct), slice(s * st, (s + 1) * st)
            s_re = spr[:, sl_s].astype(BF16)
            s_im = spi[:, sl_s].astype(BF16)
            g_re, g_im = gr[:, sl_s].astype(BF16), gi[:, sl_s].astype(BF16)
            dys, us = dyb[:, sl_c], ub[:, sl_c]
            acdr[s] += lax.dot_general(s_re, dys, (TN, ((), ())), preferred_element_type=F32)
            acdi[s] -= lax.dot_general(s_im, dys, (TN, ((), ())), preferred_element_type=F32)
            abdr[s] += lax.dot_general(us, g_re, (TN, ((), ())), preferred_element_type=F32)
            abdi[s] += lax.dot_general(us, g_im, (TN, ((), ())), preferred_element_type=F32)
            dup[:, sl_c] = (lax.dot_general(g_re, wbdr[s], (NT, ((), ())), preferred_element_type=F32)
                            + lax.dot_general(g_im, wbdi[s], (NT, ((), ())), preferred_element_type=F32))
        dd_ref[...] += jnp.sum(dy * u, axis=0, keepdims=True)
        du = _time_order_rows(pm_ref[...], dup[...], 2) + d_ref[...] * dy
        dproj_ref[:, 0:P] = dpi_ref[...]
        dproj_ref[:, P:2 * P] = dpg_ref[...]
        dproj_ref[:, 2 * P:3 * P] = du.astype(BF16)
        dproj_ref[:, 3 * P:4 * P] = dsg_ref[...]

        @pl.when(i == nb - 1)
        def _():
            for a, h in ((abdr, dbdr_h), (abdi, dbdi_h), (acdr, dcdr_h), (acdi, dcdi_h)):
                pltpu.sync_copy(a, h)

    rev = lambda i: nb - 1 - i
    vec = lambda n: _bs((1, n), lambda i: (0, 0))
    row = _bs((tb, P), lambda i: (rev(i), 0))
    st_spec = _bs((None, SUBLANES, ns), lambda i: (rev(i), 0, 0))
    bshape = jax.ShapeDtypeStruct(bdr.shape, F32)
    cshape = jax.ShapeDtypeStruct(cdr.shape, F32)
    body, extra = _hosted(core, comm, (nb,), 16, 8, 21)
    return _call(body, grid=(nb,),
                 in_specs=[_bs((tb, P), lambda i: (rev(i), 2)), row, row, st_spec, st_spec,
                           vec(ns), vec(ns), vec(P), _bs((tb, tb), lambda i: (0, 0)), row, row, row,
                           ANY, ANY, ANY, ANY] + extra["in_specs"],
                 out_specs=[_bs((tb, 4 * P), lambda i: (rev(i), 0)), vec(ns), vec(ns), vec(P), ANY, ANY, ANY, ANY]
                 + extra["out_specs"],
                 out_shape=[jax.ShapeDtypeStruct((T, 4 * P), BF16), jax.ShapeDtypeStruct((1, ns), F32),
                            jax.ShapeDtypeStruct((1, ns), F32), jax.ShapeDtypeStruct((1, P), F32),
                            bshape, bshape, cshape, cshape] + extra["out_shape"],
                 scratch_shapes=[pltpu.VMEM(bdr.shape, BF16), pltpu.VMEM(bdr.shape, BF16),
                                 pltpu.VMEM(cdr.shape, BF16), pltpu.VMEM(cdr.shape, BF16),
                                 pltpu.VMEM(bdr.shape, F32), pltpu.VMEM(bdr.shape, F32),
                                 pltpu.VMEM(cdr.shape, F32), pltpu.VMEM(cdr.shape, F32),
                                 pltpu.VMEM((tb, ns), F32), pltpu.VMEM((tb, ns), F32),
                                 pltpu.VMEM((tb, ns), F32), pltpu.VMEM((tb, ns), F32),
                                 pltpu.VMEM((8, SUBLANES, ns), F32), pltpu.VMEM((8, SUBLANES, ns), F32)]
                 + [pltpu.VMEM((SUBLANES, ns), F32)] * 6 + [pltpu.VMEM((tb, P), F32)] + extra["scratch"],
                 compiler_params=_cp(("arbitrary",)), name="ssm_bwd")(
                     proj, y, dge, bsr, bsi, abr, abi, dsk, _seg_perm_matrix(tb), dpi, dpg, dsg,
                     bdr, bdi, cdr, cdi, *extra["ins"])


def _adamw(w, g, m, v, name):
    R, C = w.shape
    tr = _t(R, 256)

    def body(w_ref, g_ref, m_ref, v_ref, d_ref, mo_ref, vo_ref):
        gv = g_ref[...]
        mn = ADAM_B1 * m_ref[...] + (1.0 - ADAM_B1) * gv
        vn = ADAM_B2 * v_ref[...] + (1.0 - ADAM_B2) * (gv * gv)
        m_hat = mn / (1.0 - ADAM_B1 ** ADAM_STEP)
        v_hat = vn / (1.0 - ADAM_B2 ** ADAM_STEP)
        d_ref[...] = -ADAM_LR * (m_hat / (jnp.sqrt(v_hat) + ADAM_EPS) + ADAM_WD * w_ref[...])
        mo_ref[...] = mn
        vo_ref[...] = vn

    blk = _bs((tr, C), lambda i: (i, 0))
    shp = jax.ShapeDtypeStruct((R, C), F32)
    return _call(body, grid=(R // tr,), in_specs=[blk] * 4, out_specs=[blk] * 3, out_shape=[shp] * 3,
                 compiler_params=_cp(("parallel",)), name=name)(w, g, m, v)


def _sum_cast(grad, got, place, name):
    J, H, C = got.shape
    tr = _t(H, 256)
    nb = H // tr

    def body(pl_ref, a_ref, b_ref, o_ref):
        o_ref[...] = (a_ref[...] + b_ref[...]).astype(BF16)

    blk = _bs((None, tr, C), lambda j, i, pc: (j, i, 0))
    mine = _bs((None, tr, C), lambda j, i, pc: (j, pc[1] * nb + i, 0))
    spec = pltpu.PrefetchScalarGridSpec(num_scalar_prefetch=1, grid=(J, nb), in_specs=[mine, blk], out_specs=blk)
    return _call(body, grid_spec=spec, out_shape=jax.ShapeDtypeStruct((J, H, C), BF16),
                 compiler_params=_cp(("parallel", "parallel")), name=name)(place, grad, got)


def _sum_chips(sent, arrived, place, name):
    J, H, C = arrived.shape
    tr = _t(H, 256)
    nb = H // tr

    def body(pl_ref, own_ref, a0_ref, a1_ref, a2_ref, o_ref):
        acc = own_ref[...].astype(F32)
        for r in (a0_ref, a1_ref, a2_ref):
            acc = acc + r[...].astype(F32)
        o_ref[...] = acc

    def other(k):
        return _bs((None, tr, C), lambda i, pc: (jnp.where(pc[0] <= k, k + 1, k), i, 0))

    spec = pltpu.PrefetchScalarGridSpec(
        num_scalar_prefetch=1, grid=(nb,),
        in_specs=[_bs((None, tr, C), lambda i, pc: (pc[0], i, 0)), other(0), other(1), other(2)],
        out_specs=_bs((tr, C), lambda i, pc: (pc[1] * nb + i, 0)))
    return _call(body, grid_spec=spec, out_shape=jax.ShapeDtypeStruct((2 * H, C), F32),
                 compiler_params=_cp(("parallel",)), name=name)(place, sent, arrived, arrived, arrived)


def _place():
    x, y, c = lax.axis_index("x"), lax.axis_index("y"), lax.axis_index("c")
    chips = [(1 - x, y), (x, 1 - y), (1 - x, 1 - y)]
    return x, y, c, chips


def _split(nrows, row_bytes, align, cap=None):
    k = max(1, min(cap or DMA_MAX_CHUNKS, (nrows * row_bytes) // DMA_CHUNK_BYTES))
    while k > 1 and nrows % (k * align):
        k -= 1
    return k


def _comm_call(plan, name):
    n_in, n_out = len(plan["ins"]), len(plan["out_shape"])

    def body(*refs):
        for phase in plan["phases"]:
            phase(refs[:n_in], refs[n_in:n_in + n_out], refs[n_in + n_out:])

    return _call(body, in_specs=[ANY] * n_in, out_specs=[ANY] * n_out, out_shape=plan["out_shape"],
                 scratch_shapes=plan["scratch"], name=name)(*plan["ins"])


def _comm_hooks(plan, grid, ins, outs, sems, *, before):
    if plan is None:
        return
    nsteps, step = 1, 0
    for d, g in enumerate(grid):
        nsteps, step = nsteps * g, step * g + pl.program_id(d)
    for p, (phase, frac) in enumerate(zip(plan["phases"], plan["at"])):
        if (p == 0) == before:
            pl.when(step == int(frac * (nsteps - 1)))(functools.partial(phase, ins, outs, sems))


def _ag_plan(shards, axes):
    n = len(shards)
    shapes = [a.shape for a in shards]

    def window(ref, i, chip, half=None):
        S, ax = shapes[i], axes[i]
        idx = []
        for d in range(len(S)):
            off, size = 0, S[d]
            if d == 0 and half is not None:
                off, size = half * (S[0] // 2), S[0] // 2
            if d == ax:
                off = off + chip * S[ax]
            idx.append(pl.ds(off, size))
        return ref.at[tuple(idx)]

    def copies(src, full, sems):
        ssem, rsem = sems
        x, y, c, chips = _place()
        me = 2 * x + y
        sib = (x, y, 1 - c)
        idx = [2 * cx + cy for cx, cy in chips]

        def rcopy(i, k, s_ref, d_ref, to):
            return pltpu.make_async_remote_copy(src_ref=s_ref, dst_ref=d_ref, send_sem=ssem.at[i, k],
                                                recv_sem=rsem.at[i, k], device_id=to, device_id_type=MESH)

        def ici(i, j, incoming):
            half_src = src[i].at[pl.ds(c * (shapes[i][0] // 2), shapes[i][0] // 2)]
            return rcopy(i, j, half_src, window(full[i], i, idx[j] if incoming else me, c), (*chips[j], c))

        def fwd(i, j, half):
            w = window(full[i], i, idx[j], half)
            return rcopy(i, 3 + j, w, w, sib)

        def own(i):
            return rcopy(i, 6, src[i], window(full[i], i, me), sib)

        return c, ici, fwd, own

    def send(src, full, sems):
        c, ici, fwd, own = copies(src, full, sems)
        for i in range(n):
            for j in range(3):
                ici(i, j, False).start()
        for i in range(n):
            own(i).start()

    def forward(src, full, sems):
        c, ici, fwd, own = copies(src, full, sems)
        for i in range(n):
            for j in range(3):
                ici(i, j, True).wait_recv()
                fwd(i, j, c).start()

    def finish(src, full, sems):
        c, ici, fwd, own = copies(src, full, sems)
        for i in range(n):
            for j in range(3):
                fwd(i, j, 1 - c).wait_recv()
            own(i).wait()
        for i in range(n):
            for j in range(3):
                ici(i, j, False).wait_send()
                fwd(i, j, c).wait_send()

    out_shape = [jax.ShapeDtypeStruct(tuple(N_CHIP * d if k == ax else d for k, d in enumerate(S)), BF16)
                 for S, ax in zip(shapes, axes)]
    return dict(ins=list(shards), out_shape=out_shape, phases=[send, forward, finish], at=[0.0, 0.8, 1.0],
                scratch=[pltpu.SemaphoreType.DMA((n, 7)), pltpu.SemaphoreType.DMA((n, 7))])


def _proj_ag(hn, wsh, order, tm):
    T, D = hn.shape
    P = wsh.shape[1]
    H = D // 2
    nt = T // tm

    def body(order_ref, hn_ref, wsh_ref, proj_ref, win_ref, wbuf, lsem, ssem, rsem):
        n, i = pl.program_id(0), pl.program_id(1)
        x, y, c, chips = _place()
        me = 2 * x + y
        sib = (x, y, 1 - c)
        idx = [2 * cx + cy for cx, cy in chips]

        def rcopy(k, s_ref, d_ref, to):
            return pltpu.make_async_remote_copy(src_ref=s_ref, dst_ref=d_ref, send_sem=ssem.at[k],
                                                recv_sem=rsem.at[k], device_id=to, device_id_type=MESH)

        def cols(chip):
            return pl.ds(pl.multiple_of(chip * P, LANES), P)

        def rows(half):
            return pl.ds(pl.multiple_of(half * H, 16), H)

        def ici(j, incoming):
            return rcopy(j, wsh_ref.at[rows(c)], win_ref.at[rows(c), cols(idx[j] if incoming else me)],
                         (*chips[j], c))

        def fwd(j, half):
            w = win_ref.at[rows(half), cols(idx[j])]
            return rcopy(3 + j, w, w, sib)

        def own():
            return rcopy(6, wsh_ref, win_ref.at[:, cols(me)], sib)

        def load(src):
            cp = pltpu.make_async_copy(src, wbuf, lsem)
            cp.start()
            cp.wait()

        @pl.when((n == 0) & (i == 0))
        def _():
            ici(0, False).start()
            ici(1, False).start()
            own().start()
            load(wsh_ref)

        for j in range(3):
            @pl.when((n == j + 1) & (i == 0))
            def _(j=j):
                if j == 0:
                    ici(2, False).start()
                ici(j, True).wait_recv()
                fwd(j, c).start()
                fwd(j, 1 - c).wait_recv()
                load(win_ref.at[:, cols(idx[j])])

        proj_ref[...] = jnp.dot(hn_ref[...], wbuf[...], preferred_element_type=F32)

        @pl.when((n == 3) & (i == nt - 1))
        def _():
            own().wait()
            for j in range(3):
                ici(j, False).wait_send()
                fwd(j, c).wait_send()

    spec = pltpu.PrefetchScalarGridSpec(
        num_scalar_prefetch=1, grid=(N_CHIP, nt),
        in_specs=[_bs((tm, D), lambda n, i, o: (i, 0)), ANY],
        out_specs=[_bs((tm, P), lambda n, i, o: (i, o[n])), ANY],
        scratch_shapes=[pltpu.VMEM((D, P), BF16), pltpu.SemaphoreType.DMA,
                        pltpu.SemaphoreType.DMA((7,)), pltpu.SemaphoreType.DMA((7,))])
    return _call(body, grid_spec=spec,
                 out_shape=[jax.ShapeDtypeStruct((T, N_CHIP * P), F32), jax.ShapeDtypeStruct((D, N_CHIP * P), BF16)],
                 compiler_params=_cp(("arbitrary", "arbitrary")), name="proj_ag")(order, hn, wsh)


def _halves_plan(grads):
    n = len(grads)

    def send(g, got, sems):
        ssem, rsem = sems
        x, y, c, _ = _place()
        sib = (x, y, 1 - c)
        for i in range(n):
            J, R, C = g[i].shape
            H = R // 2
            k = _split(H, C * 4, SUBLANES, cap=DMA_MAX_CHUNKS // J)
            hr = H // k
            for j in range(J):
                for q in range(k):
                    other = pl.ds(pl.multiple_of((1 - c) * H + q * hr, SUBLANES), hr)
                    to = pl.ds(q * hr, hr)
                    pltpu.make_async_remote_copy(src_ref=g[i].at[j, other, :], dst_ref=got[i].at[j, to, :],
                                                 send_sem=ssem.at[i], recv_sem=rsem.at[i],
                                                 device_id=sib, device_id_type=MESH).start()

    def finish(g, got, sems):
        ssem, rsem = sems
        x, y, c, _ = _place()
        for i in range(n):
            pltpu.make_async_remote_copy(src_ref=got[i], dst_ref=got[i], send_sem=ssem.at[i], recv_sem=rsem.at[i],
                                         device_id=(x, y, 1 - c), device_id_type=MESH).wait()

    half = [jax.ShapeDtypeStruct((a.shape[0], a.shape[1] // 2, a.shape[2]), a.dtype) for a in grads]
    return dict(ins=list(grads), out_shape=half, phases=[send, finish], at=[0.0, 1.0],
                scratch=[pltpu.SemaphoreType.DMA((n,)), pltpu.SemaphoreType.DMA((n,))])


def _scatter_plan(parts):
    n = len(parts)

    def peers():
        x, y, c, chips = _place()
        return 2 * x + y, c, chips, [2 * cx + cy for cx, cy in chips]

    def send(s, got, sems):
        ssem, rsem = sems
        me, c, chips, idx = peers()
        for i in range(n):
            _, H, C = s[i].shape
            k = _split(H, C * 2, 16, cap=RS_CHUNKS)
            hr = H // k
            for q in range(k):
                rows = pl.ds(q * hr, hr)
                for j in range(3):
                    pltpu.make_async_remote_copy(src_ref=s[i].at[idx[j], rows, :], dst_ref=got[i].at[me, rows, :],
                                                 send_sem=ssem.at[i, j], recv_sem=rsem.at[i, j],
                                                 device_id=(*chips[j], c), device_id_type=MESH).start()

    def finish(s, got, sems):
        ssem, rsem = sems
        me, c, chips, idx = peers()
        for i in range(n):
            for j in range(3):
                pltpu.make_async_remote_copy(src_ref=s[i].at[idx[j]], dst_ref=got[i].at[idx[j]],
                                             send_sem=ssem.at[i, j], recv_sem=rsem.at[i, j],
                                             device_id=(*chips[j], c), device_id_type=MESH).wait()

    return dict(ins=list(parts), out_shape=[jax.ShapeDtypeStruct(a.shape, a.dtype) for a in parts],
                phases=[send, finish], at=[0.0, 1.0],
                scratch=[pltpu.SemaphoreType.DMA((n, 3)), pltpu.SemaphoreType.DMA((n, 3))])


def _join_halves(shards):
    n = len(shards)

    def body(*refs):
        full = refs[n:2 * n]
        ssem, rsem = refs[2 * n:]
        x, y, c, _ = _place()
        sib = (x, y, 1 - c)
        for i in range(n):
            H, C = full[i].shape[0] // 2, full[i].shape[1]
            k = _split(H, C * 4, SUBLANES)
            hr = H // k
            for q in range(k):
                rows = pl.ds(pl.multiple_of(c * H + q * hr, SUBLANES), hr)
                pltpu.make_async_remote_copy(src_ref=full[i].at[rows], dst_ref=full[i].at[rows],
                                             send_sem=ssem.at[i], recv_sem=rsem.at[i],
                                             device_id=sib, device_id_type=MESH).start()
        for i in range(n):
            half = full[i].at[pl.ds(0, full[i].shape[0] // 2)]
            pltpu.make_async_remote_copy(src_ref=half, dst_ref=half, send_sem=ssem.at[i], recv_sem=rsem.at[i],
                                         device_id=sib, device_id_type=MESH).wait()

    shp = [jax.ShapeDtypeStruct(a.shape, a.dtype) for a in shards]
    return _call(body, in_specs=[ANY] * n, out_specs=[ANY] * n, out_shape=shp,
                 input_output_aliases={i: i for i in range(n)},
                 scratch_shapes=[pltpu.SemaphoreType.DMA((n,)), pltpu.SemaphoreType.DMA((n,))],
                 name="rs_join")(*shards)


def _allreduce_small(buf):
    R, L = buf.shape
    RB = R // N_DEV

    def body(x_ref, o_ref, got, ssem, rsem):
        x, y, c, _ = _place()
        me = 4 * x + 2 * y + c

        def dev(k):
            return (k // 4, (k // 2) % 2, k % 2)

        def slab(k):
            return pl.ds(pl.multiple_of(k * RB, SUBLANES), RB)

        sends = []
        for d in range(1, N_DEV):
            peer = (me + d) % N_DEV
            cp = pltpu.make_async_remote_copy(src_ref=x_ref.at[slab(peer)], dst_ref=got.at[me],
                                              send_sem=ssem.at[0, d], recv_sem=rsem.at[0, d],
                                              device_id=dev(peer), device_id_type=MESH)
            cp.start()
            sends.append(cp)
        got[me] = x_ref[slab(me), :]
        for d in range(1, N_DEV):
            src = (me + N_DEV - d) % N_DEV
            pltpu.make_async_remote_copy(src_ref=x_ref.at[slab(me)], dst_ref=got.at[src],
                                         send_sem=ssem.at[0, d], recv_sem=rsem.at[0, d],
                                         device_id=dev(src), device_id_type=MESH).wait_recv()
        acc = got[0]
        for k in range(1, N_DEV):
            acc = acc + got[k]
        o_ref[slab(me), :] = acc
        for d in range(1, N_DEV):
            peer = (me + d) % N_DEV
            cp = pltpu.make_async_remote_copy(src_ref=o_ref.at[slab(me)], dst_ref=o_ref.at[slab(me)],
                                              send_sem=ssem.at[1, d], recv_sem=rsem.at[1, d],
                                              device_id=dev(peer), device_id_type=MESH)
            cp.start()
            sends.append(cp)
        for d in range(1, N_DEV):
            src = (me + N_DEV - d) % N_DEV
            pltpu.make_async_remote_copy(src_ref=o_ref.at[slab(src)], dst_ref=o_ref.at[slab(src)],
                                         send_sem=ssem.at[1, d], recv_sem=rsem.at[1, d],
                                         device_id=dev(src), device_id_type=MESH).wait_recv()
        for cp in sends:
            cp.wait_send()

    return _call(body, in_specs=[VMEM_FULL], out_specs=VMEM_FULL, out_shape=jax.ShapeDtypeStruct((R, L), F32),
                 scratch_shapes=[pltpu.VMEM((N_DEV, RB, L), F32), pltpu.SemaphoreType.DMA((2, N_DEV)),
                                 pltpu.SemaphoreType.DMA((2, N_DEV))],
                 name="allreduce_small")(buf)


def _block_diag(t, gt):
    G, A, B = t.shape
    t4 = t.reshape(G // gt, gt, A, B)
    eye = jnp.eye(gt, dtype=t.dtype)
    return jnp.einsum('sgab,gh->sgahb', t4, eye).reshape(G // gt, gt * A, gt * B)


def _block_diag_extract(m, gt, A, B):
    S = m.shape[0]
    m5 = m.reshape(S, gt, A, gt, B)
    eye = jnp.eye(gt, dtype=m.dtype)
    return jnp.einsum('sgahb,gh->sgab', m5, eye).reshape(S * gt, A, B)


def _pack_small(arrs, rows):
    flat = jnp.concatenate([a.reshape(-1).astype(F32) for a in arrs])
    return jnp.pad(flat, (0, rows * LANES - flat.shape[0])).reshape(rows, LANES)


def _unpack_small(buf, shapes):
    flat = buf.reshape(-1)
    out, off = [], 0
    for s in shapes:
        n = 1
        for d in s:
            n *= d
        out.append(flat[off:off + n].reshape(s))
        off += n
    return out


def kernel(x, p, norm_gain, w_in, w_pool, pool_scale, a_re, a_im, log_dt, b_re, b_im, c_re, c_im, d_skip, w_glu, w_out, w_ple, w_ple_gate, final_gain, loss_target, m_norm_gain, m_w_in, m_w_pool, m_pool_scale, m_a_re, m_a_im, m_log_dt, m_b_re, m_b_im, m_c_re, m_c_im, m_d_skip, m_w_glu, m_w_out, m_w_ple, m_w_ple_gate, m_final_gain, v_norm_gain, v_w_in, v_w_pool, v_pool_scale, v_a_re, v_a_im, v_log_dt, v_b_re, v_b_im, v_c_re, v_c_im, v_d_skip, v_w_glu, v_w_out, v_w_ple, v_w_ple_gate, v_final_gain):
    xs, pe, tgt = x[0], p[0, 0], loss_target[0]
    T, D = xs.shape
    E = pe.shape[1]
    P = D // 2
    NG = len(POOL_WINDOWS)
    PG = P // NG
    G, N, C = P // SSM_GROUP, SSM_STATE, SSM_GROUP
    GT = min(SSM_TILE_GROUPS, G)
    Q = D // N_CHIP

    big = {"w_in": (w_in, m_w_in, v_w_in), "w_pool": (w_pool, m_w_pool, v_w_pool),
           "w_glu": (w_glu, m_w_glu, v_w_glu), "w_out": (w_out, m_w_out, v_w_out),
           "w_ple": (w_ple, m_w_ple, v_w_ple), "w_ple_gate": (w_ple_gate, m_w_ple_gate, v_w_ple_gate)}
    big_names = list(big)
    shard2d = {n: (big[n][0].size // big[n][0].shape[-1], big[n][0].shape[-1]) for n in big_names}
    shard_axis = {"w_in": 1, "w_pool": 1, "w_glu": 1, "w_out": 0, "w_ple": 1, "w_ple_gate": 0}
    shard16 = {n: big[n][0][0].astype(BF16) for n in big_names}
    place = jnp.stack([2 * lax.axis_index("x") + lax.axis_index("y"), lax.axis_index("c")]).astype(jnp.int32)
    mx, my = lax.axis_index("x"), lax.axis_index("y")
    block_order = jnp.stack([2 * mx + my, 2 * (1 - mx) + my, 2 * mx + (1 - my),
                             2 * (1 - mx) + (1 - my)]).astype(jnp.int32)
    later = [n for n in big_names if n != "w_in"]
    ag_later = _ag_plan([shard16[n] for n in later], [shard_axis[n] for n in later])

    rep = lambda a: jnp.repeat(a, C, axis=0)
    a_re_r, a_im_r = rep(a_re[0]), rep(a_im[0])
    ldt_r = rep(jnp.broadcast_to(log_dt[0][:, None], (G, N)))
    bt_re = b_re[0].transpose(0, 2, 1).reshape(G * C, N)
    bt_im = b_im[0].transpose(0, 2, 1).reshape(G * C, N)
    ab_re_r, ab_im_r, bbt_re, bbt_im = _ssm_prep(a_re_r, a_im_r, ldt_r, bt_re, bt_im)
    abr = ab_re_r[::C].reshape(1, G * N)
    abi = ab_im_r[::C].reshape(1, G * N)
    bdr = _block_diag(bbt_re.reshape(G, C, N), GT).astype(BF16)
    bdi = _block_diag(bbt_im.reshape(G, C, N), GT).astype(BF16)
    cdr = _block_diag(c_re[0].transpose(0, 2, 1), GT).astype(BF16)
    cdi = _block_diag(c_im[0].transpose(0, 2, 1), GT).astype(BF16)

    tb = _t(T, 256)
    tbs = _t(T, 256)
    tm = _t(T, 1024)
    tk = _t(T, 2048)
    DH = _t(D, 1024)
    row_k = lambda i, n, k: (i, k)
    row_n = lambda i, n, k: (i, n)
    f32 = lambda *shape: jax.ShapeDtypeStruct(shape, F32)
    hn = _norm1(xs, norm_gain, tb)
    proj, win = _proj_ag(hn, shard16["w_in"], block_order, tm)
    y, ge, bsr, bsi, wp, wglu, wout, wple, wpg = _ssm_fwd(proj, bdr, bdi, cdr, cdi, abr, abi, d_skip, P, tbs,
                                                          comm=ag_later)
    pooled = _pool_fwd(proj, P, tb)
    mixed = _mm(pooled, wp, dims=NN, grid=(T // tm, NG, 1),
                a_spec=_bs((tm, PG), row_n), b_spec=_bs((None, PG, PG), lambda i, g, k: (g, 0, 0)),
                o_spec=_bs((tm, PG), row_n), out_shape=f32(T, P), name="mm_pool")
    hg = _mm(ge, wglu, dims=NN, grid=(T // tm, 2 * P // DH, 1),
             a_spec=_bs((tm, P), row_k), b_spec=_bs((P, DH), lambda i, n, k: (k, n)),
             o_spec=_bs((tm, DH), row_n), out_shape=f32(T, 2 * P), name="mm_glu")
    cat = _gate_fwd(mixed, proj, hg, pool_scale, tb)
    h1, h1b = _mm(cat, wout, dims=NN, grid=(T // tm, D // DH, 1), res=xs, bf16_copy=True,
                  a_spec=_bs((tm, D), row_k), b_spec=_bs((D, DH), lambda i, n, k: (k, n)),
                  r_spec=_bs((tm, DH), row_n), o_spec=_bs((tm, DH), row_n), out_shape=f32(T, D), name="mm_out")
    e = _mm(pe, wple, dims=NN, grid=(T // tm, D // DH, 1),
            a_spec=_bs((tm, E), row_k), b_spec=_bs((E, DH), lambda i, n, k: (k, n)),
            o_spec=_bs((tm, DH), row_n), out_shape=f32(T, D), name="mm_ple")
    z = _mm(h1b, wpg, dims=NN, grid=(T // tm, D // DH, 1),
            a_spec=_bs((tm, D), row_k), b_spec=_bs((D, DH), lambda i, n, k: (k, n)),
            o_spec=_bs((tm, DH), row_n), out_shape=f32(T, D), name="mm_pgate")
    dh2, de, dz, dg2, lpart = _final_fb(h1, e, z, tgt, final_gain.reshape(1, D), tb)

    col_m = lambda m, n, k: (k, m)
    col_n = lambda m, n, k: (k, n)
    dh1, dh1b = _mm(dz, wpg, dims=NT, grid=(T // tm, D // DH, 1), res=dh2, bf16_copy=True,
                    a_spec=_bs((tm, D), row_k), b_spec=_bs((DH, D), lambda i, n, k: (n, k)),
                    r_spec=_bs((tm, DH), row_n), o_spec=_bs((tm, DH), row_n), out_shape=f32(T, D), name="mm_dh1")
    g_wpg = _mm(h1b, dz, dims=TN, grid=(D // DH, D // DH, T // tk),
                a_spec=_bs((tk, DH), col_m), b_spec=_bs((tk, DH), col_n),
                o_spec=_bs((DH, DH), lambda m, n, k: (m, n)), out_shape=f32(D, D), name="mm_gwpg")
    g_wple = _mm(pe, de, dims=TN, grid=(1, N_CHIP, T // tk),
                 a_spec=_bs((tk, E), col_m), b_spec=_bs((tk, Q), col_n),
                 o_spec=_bs((None, E, Q), lambda m, j, k: (j, 0, 0)), out_shape=f32(N_CHIP, E, Q), name="mm_gwple")
    dcat = _mm(dh1b, wout, dims=NT, grid=(T // tm, D // DH, 1),
               a_spec=_bs((tm, D), row_k), b_spec=_bs((DH, D), lambda i, n, k: (n, k)),
               o_spec=_bs((tm, DH), row_n), out_shape=f32(T, D), name="mm_dcat")
    g_wout = _mm(cat, dh1b, dims=TN, grid=(D // DH, D // DH, T // tk),
                 a_spec=_bs((tk, DH), col_m), b_spec=_bs((tk, DH), col_n),
                 o_spec=_bs((DH, DH), lambda m, n, k: (m, n)), out_shape=f32(D, D), name="mm_gwout")
    gbig = {"w_out": g_wout.reshape(N_CHIP, Q, D), "w_ple": g_wple, "w_ple_gate": g_wpg.reshape(N_CHIP, Q, D)}
    first = list(gbig)
    res = _gate_bwd(dcat, mixed, proj, hg, pool_scale, tb, comm=_halves_plan([gbig[n] for n in first]))
    dmixed, dpg, dsg, dhg, dps = res[:5]
    got = dict(zip(first, res[5:]))
    dge = _mm(dhg, wglu, dims=NT, grid=(T // tm, 1, 1),
              a_spec=_bs((tm, 2 * P), row_k), b_spec=_bs((P, 2 * P), lambda i, n, k: (n, k)),
              o_spec=_bs((tm, P), row_n), out_shape=f32(T, P), name="mm_dge")
    g_wglu = _mm(ge, dhg, dims=TN, grid=(1, N_CHIP, T // tk),
                 a_spec=_bs((tk, P), col_m), b_spec=_bs((tk, Q), col_n),
                 o_spec=_bs((None, P, Q), lambda m, j, k: (j, 0, 0)), out_shape=f32(N_CHIP, P, Q), name="mm_gwglu")
    dpooled = _mm(dmixed, wp, dims=NT, grid=(T // tm, NG, 1),
                  a_spec=_bs((tm, PG), row_n), b_spec=_bs((None, PG, PG), lambda i, g, k: (g, 0, 0)),
                  o_spec=_bs((tm, PG), row_n), out_shape=f32(T, P), name="mm_dpooled")
    g_wp = _mm(pooled, dmixed, dims=TN, grid=(NG, 1, T // tk),
               a_spec=_bs((tk, PG), col_m), b_spec=_bs((tk, PG), col_m),
               o_spec=_bs((None, PG, PG), lambda g, n, k: (g, 0, 0)), out_shape=f32(NG, PG, PG), name="mm_gwp")
    gbig["w_pool"] = g_wp.reshape(NG, N_CHIP, PG // N_CHIP, PG).transpose(1, 0, 2, 3).reshape(
        N_CHIP, NG * PG // N_CHIP, PG)
    gbig["w_glu"] = g_wglu
    res = _pool_bwd(dpooled, tb, comm=_halves_plan([gbig["w_pool"], gbig["w_glu"]]))
    dpi, got["w_pool"], got["w_glu"] = res
    early = list(gbig)
    chip_sums = {n: _sum_cast(gbig[n], got[n], place, "sum_cast_" + n) for n in early}
    res = _ssm_bwd(proj, y, dge, bsr, bsi, bdr, bdi, cdr, cdi, abr, abi, d_skip, dpi, dpg, dsg, P, tbs,
                   comm=_scatter_plan([chip_sums[n] for n in early]))
    dproj, dabr, dabi, dd, dbdr, dbdi, dcdr, dcdi = res[:8]
    arrived = dict(zip(early, res[8:]))
    gbig["w_in"] = _mm(hn, dproj, dims=TN, grid=(D // DH, N_CHIP, T // tk),
                       a_spec=_bs((tk, DH), col_m), b_spec=_bs((tk, P), col_n),
                       o_spec=_bs((None, DH, P), lambda m, j, k: (j, m, 0)), out_shape=f32(N_CHIP, D, P),
                       name="mm_gwin")
    got["w_in"], = _comm_call(_halves_plan([gbig["w_in"]]), "rs_halves_late")
    chip_sums["w_in"] = _sum_cast(gbig["w_in"], got["w_in"], place, "sum_cast_w_in")
    KH = _t(4 * P, 2048)
    dhn, arrived["w_in"] = _mm(dproj, win, dims=NT, grid=(T // tm, D // DH, 4 * P // KH),
                               a_spec=_bs((tm, KH), row_k), b_spec=_bs((DH, KH), lambda i, n, k: (n, k)),
                               o_spec=_bs((tm, DH), row_n), out_shape=f32(T, D), name="mm_dhn",
                               comm=_scatter_plan([chip_sums["w_in"]]))
    grad_x, dg1 = _norm1_bwd(xs, dhn, dh1, norm_gain, tb)

    dbbt_re = _block_diag_extract(dbdr, GT, C, N).reshape(G * C, N)
    dbbt_im = _block_diag_extract(dbdi, GT, C, N).reshape(G * C, N)
    g_c_re = _block_diag_extract(dcdr, GT, N, C).transpose(0, 2, 1)
    g_c_im = _block_diag_extract(dcdi, GT, N, C).transpose(0, 2, 1)
    dab_re_r = rep(dabr.reshape(G, N)) * (1.0 / C)
    dab_im_r = rep(dabi.reshape(G, N)) * (1.0 / C)
    g_a_re, g_a_im, g_ldt, g_bt_re, g_bt_im = _ssm_prep_bwd(a_re_r, a_im_r, ldt_r, bt_re, bt_im,
                                                            dab_re_r, dab_im_r, dbbt_re, dbbt_im, G)
    g_b_re = g_bt_re.reshape(G, C, N).transpose(0, 2, 1)
    g_b_im = g_bt_im.reshape(G, C, N).transpose(0, 2, 1)

    halves = [_sum_chips(chip_sums[n], arrived[n], place, "sum_chips_" + n) for n in big_names]
    gshard = _join_halves(halves)

    small_names = ["norm_gain", "pool_scale", "a_re", "a_im", "log_dt", "b_re", "b_im", "c_re", "c_im",
                   "d_skip", "final_gain"]
    small_w = dict(norm_gain=norm_gain, pool_scale=pool_scale, a_re=a_re, a_im=a_im, log_dt=log_dt, b_re=b_re,
                   b_im=b_im, c_re=c_re, c_im=c_im, d_skip=d_skip, final_gain=final_gain)
    small_m = dict(norm_gain=m_norm_gain, pool_scale=m_pool_scale, a_re=m_a_re, a_im=m_a_im, log_dt=m_log_dt,
                   b_re=m_b_re, b_im=m_b_im, c_re=m_c_re, c_im=m_c_im, d_skip=m_d_skip, final_gain=m_final_gain)
    small_v = dict(norm_gain=v_norm_gain, pool_scale=v_pool_scale, a_re=v_a_re, a_im=v_a_im, log_dt=v_log_dt,
                   b_re=v_b_re, b_im=v_b_im, c_re=v_c_re, c_im=v_c_im, d_skip=v_d_skip, final_gain=v_final_gain)
    small_g = dict(norm_gain=dg1, pool_scale=dps, a_re=g_a_re, a_im=g_a_im, log_dt=g_ldt, b_re=g_b_re,
                   b_im=g_b_im, c_re=g_c_re, c_im=g_c_im, d_skip=dd, final_gain=dg2)
    shapes = [small_w[n].shape for n in small_names]
    total = sum(small_w[n].size for n in small_names) + 1
    unit = N_DEV * SUBLANES
    rows = -(-(-(-total // LANES)) // unit) * unit
    gbuf = _pack_small([small_g[n] for n in small_names] + [lpart[0, :1]], rows)
    gsum = _allreduce_small(gbuf)
    wbuf = _pack_small([small_w[n] for n in small_names], rows)
    mbuf = _pack_small([small_m[n] for n in small_names], rows)
    vbuf = _pack_small([small_v[n] for n in small_names], rows)
    dsm, msm, vsm = _adamw(wbuf, gsum, mbuf, vbuf, "adamw_small")
    g_small = dict(zip(small_names, _unpack_small(gsum, shapes)))
    d_small = dict(zip(small_names, _unpack_small(dsm, shapes)))
    m_small = dict(zip(small_names, _unpack_small(msm, shapes)))
    v_small = dict(zip(small_names, _unpack_small(vsm, shapes)))
    loss = gsum.reshape(-1)[total - 1]

    g_out, d_out, m_out, v_out = dict(g_small), dict(d_small), dict(m_small), dict(v_small)
    for n, gs in zip(big_names, gshard):
        w_, m_, v_ = big[n]
        r2 = shard2d[n]
        d_, mn_, vn_ = _adamw(w_.reshape(r2), gs, m_.reshape(r2), v_.reshape(r2), "adamw_" + n)
        g_out[n], d_out[n], m_out[n], v_out[n] = (a.reshape(w_.shape) for a in (gs, d_, mn_, vn_))

    order = ["norm_gain", "w_in", "w_pool", "pool_scale", "a_re", "a_im", "log_dt", "b_re", "b_im", "c_re",
             "c_im", "d_skip", "w_glu", "w_out", "w_ple", "w_ple_gate", "final_gain"]
    return (loss, grad_x[None], *[g_out[n] for n in order], *[d_out[n] for n in order],
            *[m_out[n] for n in order], *[v_out[n] for n in order])
```

```python
import functools

import jax
import jax.numpy as jnp
from jax import lax
from jax.experimental import pallas as pl
from jax.experimental.pallas import tpu as pltpu

F32, BF16 = jnp.float32, jnp.bfloat16
MESH = pl.DeviceIdType.MESH
ANY = pl.BlockSpec(memory_space=pl.ANY)
VMEM_FULL = pl.BlockSpec(memory_space=pltpu.VMEM)

EPS = 1e-6
A_RE_MAX = -1e-4
SSM_GROUP = 16
SSM_STATE = 64
POOL_WINDOWS = (2, 4, 8, 16)
POOL_HALO = 16
ADAM_LR, ADAM_B1, ADAM_B2, ADAM_EPS, ADAM_WD, ADAM_STEP = 0.001, 0.9, 0.999, 1e-08, 0.01, 10

V7X_VMEM_BYTES = 64 * 1024 * 1024
VMEM_LIMIT = V7X_VMEM_BYTES - 8 * 1024 * 1024
SUBLANES, LANES = 8, 128
SSM_TILE_GROUPS = 8
SCAN_LANES = 512
N_DEV, N_CHIP = 8, 4
DMA_CHUNK_BYTES = 256 * 1024
DMA_MAX_CHUNKS = 32
AG_CHUNKS = 8
RS_CHUNKS = 8


def _t(n, pref):
    return pref if n % pref == 0 else n


def _cp(sem=None, vmem=VMEM_LIMIT):
    return pltpu.CompilerParams(dimension_semantics=sem, vmem_limit_bytes=vmem)


def _call(body, **kw):
    return pl.pallas_call(body, **kw)


NN = ((1,), (0,))
NT = ((1,), (1,))
TN = ((0,), (0,))


def _mm(a, b, *, dims, grid, a_spec, b_spec, o_spec, out_shape, name, res=None, r_spec=None, bf16_copy=False,
        comm=None):
    nk, kax = grid[-1], len(grid) - 1
    acc_shape = tuple(d for d in o_spec.block_shape if d is not None)

    def core(*refs):
        refs = list(refs)
        a_ref, b_ref = refs[:2]
        r_ref = refs[2] if res is not None else None
        outs = refs[3 if res is not None else 2:]
        o_ref = outs[0]
        o2_ref = outs[1] if bf16_copy else None
        acc = outs[-1] if nk > 1 else None

        def finish(r):
            if r_ref is not None:
                r = r + r_ref[...]
            o_ref[...] = r.astype(o_ref.dtype)
            if o2_ref is not None:
                o2_ref[...] = r.astype(BF16)

        part = lax.dot_general(a_ref[...].astype(BF16), b_ref[...].astype(BF16),
                               (dims, ((), ())), preferred_element_type=F32)
        if nk == 1:
            finish(part)
        else:
            k = pl.program_id(kax)

            @pl.when(k == 0)
            def _():
                acc[...] = part

            @pl.when(k > 0)
            def _():
                acc[...] += part

            @pl.when(k == nk - 1)
            def _():
                finish(acc[...])

    ins, specs = [a, b], [a_spec, b_spec]
    if res is not None:
        ins.append(res)
        specs.append(r_spec)
    o_specs, o_shapes = [o_spec], [out_shape]
    if bf16_copy:
        o_specs = [o_spec, o_spec]
        o_shapes = [out_shape, jax.ShapeDtypeStruct(out_shape.shape, BF16)]
    scratch = [pltpu.VMEM(acc_shape, F32)] if nk > 1 else []
    body, extra = _hosted(core, comm, grid, len(ins), len(o_specs), len(scratch))
    sem = ("arbitrary",) * len(grid) if comm else ("parallel",) * kax + ("arbitrary",)
    outs = _call(body, grid=grid, in_specs=specs + extra["in_specs"], out_specs=o_specs + extra["out_specs"],
                 out_shape=o_shapes + extra["out_shape"], scratch_shapes=scratch + extra["scratch"],
                 compiler_params=_cp(sem), name=name)(*ins, *extra["ins"])
    return outs[0] if len(outs) == 1 else outs


def _bs(shape, fn):
    return pl.BlockSpec(shape, fn)


def _sigmoid(v):
    return 1.0 / (1.0 + jnp.exp(-v))


def _gelu(v):
    return 0.5 * v * (1.0 + jnp.tanh(0.7978845608028654 * (v + 0.044715 * v * v * v)))


def _gelu_grad(v):
    t = jnp.tanh(0.7978845608028654 * (v + 0.044715 * v * v * v))
    return 0.5 * (1.0 + t) + 0.5 * v * (1.0 - t * t) * 0.7978845608028654 * (1.0 + 3 * 0.044715 * v * v)


def _norm1(x, g1, tb):
    T, D = x.shape

    def body(x_ref, g_ref, o_ref):
        xv = x_ref[...]
        r = lax.rsqrt(jnp.mean(xv * xv, axis=-1, keepdims=True) + EPS)
        o_ref[...] = ((xv * r) * g_ref[...]).astype(BF16)

    return _call(body, grid=(T // tb,),
                 in_specs=[_bs((tb, D), lambda i: (i, 0)), _bs((1, D), lambda i: (0, 0))],
                 out_specs=_bs((tb, D), lambda i: (i, 0)), out_shape=jax.ShapeDtypeStruct((T, D), BF16),
                 compiler_params=_cp(("parallel",)), name="norm1")(x, g1)


def _norm1_bwd(x, dhn, dh1, g1, tb):
    T, D = x.shape

    def body(x_ref, dhn_ref, dh1_ref, g_ref, dx_ref, dg_ref):
        @pl.when(pl.program_id(0) == 0)
        def _():
            dg_ref[...] = jnp.zeros_like(dg_ref)

        xv = x_ref[...]
        r = lax.rsqrt(jnp.mean(xv * xv, axis=-1, keepdims=True) + EPS)
        xh = xv * r
        dhn_v = dhn_ref[...]
        dg_ref[...] += jnp.sum(dhn_v * xh, axis=0, keepdims=True)
        dxh = dhn_v * g_ref[...]
        dx_ref[...] = dh1_ref[...] + r * (dxh - xh * jnp.mean(dxh * xh, axis=-1, keepdims=True))

    row = _bs((tb, D), lambda i: (i, 0))
    vec = _bs((1, D), lambda i: (0, 0))
    return _call(body, grid=(T // tb,), in_specs=[row, row, row, vec], out_specs=[row, vec],
                 out_shape=[jax.ShapeDtypeStruct((T, D), F32), jax.ShapeDtypeStruct((1, D), F32)],
                 compiler_params=_cp(("arbitrary",)), name="norm1_bwd")(x, dhn, dh1, g1)


def _gate_fwd(mixed, proj, hg, ps, tb):
    T, P = mixed.shape

    def body(mx_ref, pg_ref, sg_ref, hg_ref, ps_ref, o_ref):
        pg, sg = pg_ref[...], sg_ref[...]
        ya = (mx_ref[...] * ps_ref[...]) * (pg * _sigmoid(pg))
        hgv = hg_ref[...]
        o = hgv[:, :P] * _sigmoid(hgv[:, P:])
        yb = o * (sg * _sigmoid(sg))
        o_ref[:, :P] = ya.astype(BF16)
        o_ref[:, P:] = yb.astype(BF16)

    return _call(body, grid=(T // tb,),
                 in_specs=[_bs((tb, P), lambda i: (i, 0)), _bs((tb, P), lambda i: (i, 1)),
                           _bs((tb, P), lambda i: (i, 3)), _bs((tb, 2 * P), lambda i: (i, 0)),
                           _bs((1, P), lambda i: (0, 0))],
                 out_specs=_bs((tb, 2 * P), lambda i: (i, 0)),
                 out_shape=jax.ShapeDtypeStruct((T, 2 * P), BF16),
                 compiler_params=_cp(("parallel",)), name="gate_fwd")(mixed, proj, proj, hg, ps)


def _gate_bwd(dcat, mixed, proj, hg, ps, tb, comm=None):
    T, P = mixed.shape

    def core(dc_ref, mx_ref, pg_ref, sg_ref, hg_ref, ps_ref, dmx_ref, dpg_ref, dsg_ref, dhg_ref, dps_ref):
        @pl.when(pl.program_id(0) == 0)
        def _():
            dps_ref[...] = jnp.zeros_like(dps_ref)

        dc = dc_ref[...]
        dya, dyb = dc[:, :P], dc[:, P:]
        pg, sg, mx, psv = pg_ref[...], sg_ref[...], mx_ref[...], ps_ref[...]
        s_pg = _sigmoid(pg)
        dpa = dya * (pg * s_pg)
        dpg_ref[...] = (dya * (mx * psv) * (s_pg * (1.0 + pg * (1.0 - s_pg)))).astype(BF16)
        dps_ref[...] += jnp.sum(dpa * mx, axis=0, keepdims=True)
        dmx_ref[...] = (dpa * psv).astype(BF16)
        hgv = hg_ref[...]
        h1, s_h2 = hgv[:, :P], _sigmoid(hgv[:, P:])
        s_sg = _sigmoid(sg)
        do = dyb * (sg * s_sg)
        dsg_ref[...] = (dyb * (h1 * s_h2) * (s_sg * (1.0 + sg * (1.0 - s_sg)))).astype(BF16)
        dhg_ref[:, :P] = (do * s_h2).astype(BF16)
        dhg_ref[:, P:] = (do * h1 * s_h2 * (1.0 - s_h2)).astype(BF16)

    rowp = _bs((tb, P), lambda i: (i, 0))
    row2 = _bs((tb, 2 * P), lambda i: (i, 0))
    vec = _bs((1, P), lambda i: (0, 0))
    body, extra = _hosted(core, comm, (T // tb,), 6, 5, 0)
    return _call(body, grid=(T // tb,),
                 in_specs=[row2, rowp, _bs((tb, P), lambda i: (i, 1)), _bs((tb, P), lambda i: (i, 3)), row2, vec]
                 + extra["in_specs"],
                 out_specs=[rowp, rowp, rowp, row2, vec] + extra["out_specs"],
                 out_shape=[jax.ShapeDtypeStruct((T, P), BF16), jax.ShapeDtypeStruct((T, P), BF16),
                            jax.ShapeDtypeStruct((T, P), BF16), jax.ShapeDtypeStruct((T, 2 * P), BF16),
                            jax.ShapeDtypeStruct((1, P), F32)] + extra["out_shape"],
                 scratch_shapes=extra["scratch"],
                 compiler_params=_cp(("arbitrary",)), name="gate_bwd")(dcat, mixed, proj, proj, hg, ps, *extra["ins"])


def _final_fb(h1, pe, wple, z, tgt, g2, tb):
    T, D = h1.shape
    E = pe.shape[1]

    def body(h1_ref, p_ref, w_ref, z_ref, t_ref, g_ref, dh2_ref, de_ref, dz_ref, dg_ref, l_ref):
        @pl.when(pl.program_id(0) == 0)
        def _():
            dg_ref[...] = jnp.zeros_like(dg_ref)
            l_ref[...] = jnp.zeros_like(l_ref)

        ev = jnp.dot(p_ref[...].astype(BF16), w_ref[...], preferred_element_type=F32)
        s = _sigmoid(z_ref[...])
        h2 = h1_ref[...] + ev * s
        r = lax.rsqrt(jnp.mean(h2 * h2, axis=-1, keepdims=True) + EPS)
        xh = h2 * r
        gv = g_ref[...]
        diff = xh * gv - t_ref[...]
        l_ref[...] += 0.5 * jnp.sum(jnp.mean(diff * diff, axis=-1, keepdims=True))
        dout = diff * (1.0 / D)
        dg_ref[...] += jnp.sum(dout * xh, axis=0, keepdims=True)
        dxh = dout * gv
        dh2 = r * (dxh - xh * jnp.mean(dxh * xh, axis=-1, keepdims=True))
        dh2_ref[...] = dh2
        de_ref[...] = (dh2 * s).astype(BF16)
        dz_ref[...] = (dh2 * ev * s * (1.0 - s)).astype(BF16)

    row = _bs((tb, D), lambda i: (i, 0))
    vec = _bs((1, D), lambda i: (0, 0))
    return _call(body, grid=(T // tb,),
                 in_specs=[row, _bs((tb, E), lambda i: (i, 0)), _bs((E, D), lambda i: (0, 0)), row, row, vec],
                 out_specs=[row, row, row, vec, _bs((1, LANES), lambda i: (0, 0))],
                 out_shape=[jax.ShapeDtypeStruct((T, D), F32), jax.ShapeDtypeStruct((T, D), BF16),
                            jax.ShapeDtypeStruct((T, D), BF16), jax.ShapeDtypeStruct((1, D), F32),
                            jax.ShapeDtypeStruct((1, LANES), F32)],
                 compiler_params=_cp(("arbitrary",)), name="final_fb")(h1, pe, wple, z, tgt, g2)


def _pool_inv_count(t0, rows, pg, ngroups):
    t = t0 + lax.broadcasted_iota(jnp.int32, (rows, pg), 0)
    parts = []
    for w in POOL_WINDOWS[:ngroups]:
        parts.append(jnp.where(t + 1 >= w, 1.0 / w, 1.0 / (t + 1).astype(F32)))
    return parts


def _pool_fwd(proj, wp, P, tb):
    T = proj.shape[0]
    ng = len(POOL_WINDOWS)
    pg = P // ng
    hb = tb // POOL_HALO

    def body(v_ref, tail_ref, w_ref, o_ref, mx_ref, ext):
        i = pl.program_id(0)
        ext[pl.ds(0, POOL_HALO), :] = jnp.where(i > 0, tail_ref[...], 0.0)
        ext[pl.ds(POOL_HALO, tb), :] = v_ref[...]
        inv = _pool_inv_count(i * tb, tb, pg, ng)
        for g, w in enumerate(POOL_WINDOWS):
            cols = pl.ds(g * pg, pg)
            win = ext[pl.ds(POOL_HALO, tb), cols]
            for k in range(1, w):
                win = win + ext[pl.ds(POOL_HALO - k, tb), cols]
            pooled = (win * inv[g] - ext[pl.ds(POOL_HALO, tb), cols]).astype(BF16)
            o_ref[:, cols] = pooled
            mx_ref[:, cols] = jnp.dot(pooled, w_ref[g], preferred_element_type=F32)

    row = _bs((tb, P), lambda i: (i, 0))
    return _call(body, grid=(T // tb,),
                 in_specs=[row, _bs((POOL_HALO, P), lambda i: (jnp.maximum(i * hb - 1, 0), 0)),
                           _bs(wp.shape, lambda i: (0, 0, 0))],
                 out_specs=[row, row],
                 out_shape=[jax.ShapeDtypeStruct((T, P), BF16), jax.ShapeDtypeStruct((T, P), F32)],
                 scratch_shapes=[pltpu.VMEM((tb + POOL_HALO, P), F32)],
                 compiler_params=_cp(("arbitrary",)), name="pool_fwd")(proj, proj, wp)


def _pool_bwd(dmixed, wp, tb, comm=None):
    T, P = dmixed.shape
    ng = len(POOL_WINDOWS)
    pg = P // ng
    hb = tb // POOL_HALO
    nb = T // tb

    def core(d_ref, head_ref, w_ref, o_ref, ext, dpl):
        i = pl.program_id(0)
        inv = _pool_inv_count(i * tb, tb, pg, ng)
        invh = _pool_inv_count((i + 1) * tb, POOL_HALO, pg, ng)
        for g in range(ng):
            cols = pl.ds(g * pg, pg)
            dp = lax.dot_general(d_ref[:, cols], w_ref[g], (NT, ((), ())), preferred_element_type=F32)
            dph = lax.dot_general(head_ref[:, cols], w_ref[g], (NT, ((), ())), preferred_element_type=F32)
            dpl[:, cols] = dp
            ext[pl.ds(0, tb), cols] = dp * inv[g]
            ext[pl.ds(tb, POOL_HALO), cols] = jnp.where(i < nb - 1, dph * invh[g], 0.0)
        for g, w in enumerate(POOL_WINDOWS):
            cols = pl.ds(g * pg, pg)
            acc = ext[pl.ds(0, tb), cols]
            for k in range(1, w):
                acc = acc + ext[pl.ds(k, tb), cols]
            o_ref[:, cols] = (acc - dpl[:, cols]).astype(BF16)

    body, extra = _hosted(core, comm, (nb,), 3, 1, 2)
    return _call(body, grid=(nb,),
                 in_specs=[_bs((tb, P), lambda i: (i, 0)),
                           _bs((POOL_HALO, P), lambda i: (jnp.minimum((i + 1) * hb, T // POOL_HALO - 1), 0)),
                           _bs(wp.shape, lambda i: (0, 0, 0))] + extra["in_specs"],
                 out_specs=[_bs((tb, P), lambda i: (i, 0))] + extra["out_specs"],
                 out_shape=[jax.ShapeDtypeStruct((T, P), BF16)] + extra["out_shape"],
                 scratch_shapes=[pltpu.VMEM((tb + POOL_HALO, P), F32), pltpu.VMEM((tb, P), F32)] + extra["scratch"],
                 compiler_params=_cp(("arbitrary",)), name="pool_bwd")(dmixed, dmixed, wp, *extra["ins"])


def _zoh(a_re, a_im, ldt, b_re, b_im):
    lam_re = jnp.minimum(a_re, A_RE_MAX)
    lam_im = a_im
    dt = jnp.exp(ldt)
    mag = jnp.exp(lam_re * dt)
    ang = lam_im * dt
    ab_re = mag * jnp.cos(ang)
    ab_im = mag * jnp.sin(ang)
    den = lam_re * lam_re + lam_im * lam_im
    n_re = ab_re - 1.0
    n_im = ab_im
    q_re = (n_re * lam_re + n_im * lam_im) / den
    q_im = (n_im * lam_re - n_re * lam_im) / den
    return ab_re, ab_im, q_re * b_re - q_im * b_im, q_re * b_im + q_im * b_re


def _ssm_prep(a_re, a_im, ldt, bt_re, bt_im):
    shp = jax.ShapeDtypeStruct(a_re.shape, F32)

    def body(a, b, c, d, e, o0, o1, o2, o3):
        r = _zoh(a[...], b[...], c[...], d[...], e[...])
        o0[...], o1[...], o2[...], o3[...] = r

    return _call(body, in_specs=[VMEM_FULL] * 5, out_specs=[VMEM_FULL] * 4, out_shape=[shp] * 4,
                 name="ssm_prep")(a_re, a_im, ldt, bt_re, bt_im)


def _ssm_prep_bwd(a_re, a_im, ldt, bt_re, bt_im, dab_re, dab_im, dbb_re, dbb_im, G):
    GC, N = a_re.shape
    C = GC // G

    def body(a, b, c, d, e, g0, g1, g2, g3, da_re, da_im, dldt, db_re, db_im):
        _, vjp = jax.vjp(_zoh, a[...], b[...], c[...], d[...], e[...])
        ga_re, ga_im, gl, gb_re, gb_im = vjp((g0[...], g1[...], g2[...], g3[...]))
        da_re[...] = jnp.sum(ga_re.reshape(G, C, N), axis=1)
        da_im[...] = jnp.sum(ga_im.reshape(G, C, N), axis=1)
        dldt[...] = jnp.sum(jnp.sum(gl.reshape(G, C, N), axis=1), axis=1, keepdims=True)
        db_re[...] = gb_re
        db_im[...] = gb_im

    gn = jax.ShapeDtypeStruct((G, N), F32)
    full = jax.ShapeDtypeStruct((GC, N), F32)
    return _call(body, in_specs=[VMEM_FULL] * 9, out_specs=[VMEM_FULL] * 5,
                 out_shape=[gn, gn, jax.ShapeDtypeStruct((G, 1), F32), full, full],
                 name="ssm_prep_bwd")(a_re, a_im, ldt, bt_re, bt_im, dab_re, dab_im, dbb_re, dbb_im)


def _coef_tiles(abr, abi, reverse):
    ns = abr.shape[1]
    row = lax.broadcasted_iota(jnp.int32, (SUBLANES, ns), 0)
    ar = jnp.broadcast_to(abr, (SUBLANES, ns))
    ai = jnp.broadcast_to(-abi if reverse else abi, (SUBLANES, ns))
    a2r, a2i = ar * ar - ai * ai, 2.0 * ar * ai
    a4r, a4i = a2r * a2r - a2i * a2i, 2.0 * a2r * a2i
    out = []
    for d, (vr, vi) in ((1, (ar, ai)), (2, (a2r, a2i)), (4, (a4r, a4i))):
        keep = (row < SUBLANES - d) if reverse else (row >= d)
        out += [jnp.where(keep, vr, 0.0), jnp.where(keep, vi, 0.0)]
    pr, pi = ar, ai
    for k in range(1, SUBLANES):
        sel = (row <= SUBLANES - 1 - k) if reverse else (row >= k)
        nr, ni = pr * ar - pi * ai, pr * ai + pi * ar
        pr, pi = jnp.where(sel, nr, pr), jnp.where(sel, ni, pi)
    return out + [pr, pi]


def _cpow(ar, ai, n):
    out, br, bi = None, ar, ai
    while n:
        if n & 1:
            out = (br, bi) if out is None else (out[0] * br - out[1] * bi, out[0] * bi + out[1] * br)
        br, bi = br * br - bi * bi, 2.0 * br * bi
        n >>= 1
    return out


def _seg_perm_matrix(nrows):
    r = jnp.arange(nrows)
    src = (nrows // SUBLANES) * (r % SUBLANES) + r // SUBLANES
    return (src[:, None] == jnp.arange(nrows)[None, :]).astype(BF16)


def _seg_order_rows(pm, xb):
    return jnp.dot(pm, xb, preferred_element_type=F32).astype(BF16)


def _time_order_rows(pm, x, terms):
    out, rest = None, x
    for t in range(terms):
        piece = rest.astype(BF16)
        part = lax.dot_general(pm, piece, (TN, ((), ())), preferred_element_type=F32)
        out = part if out is None else out + part
        if t + 1 < terms:
            rest = rest - piece.astype(F32)
    return out


def _seg_scan(xr_ref, xi_ref, abr_ref, abi_ref, coef_ref, car_ref, cai_ref, *, nrows, ns, reverse,
              cmat=None, dab=None):
    seg = nrows // SUBLANES
    cw = min(SCAN_LANES, ns)
    row = lax.broadcasted_iota(jnp.int32, (SUBLANES, cw), 0)
    first, last = (SUBLANES - 1, 0) if reverse else (0, SUBLANES - 1)

    def tile(i):
        return pl.ds(pl.multiple_of(((seg - 1 - i) if reverse else i) * SUBLANES, SUBLANES), SUBLANES)

    for cc in range(ns // cw):
        cols = pl.ds(cc * cw, cw)
        ar = jnp.broadcast_to(abr_ref[:, cols], (SUBLANES, cw))
        ai = jnp.broadcast_to(abi_ref[:, cols], (SUBLANES, cw))
        if reverse:
            ai = -ai

        def local(i, x, cols=cols, ar=ar, ai=ai):
            rows = tile(i)
            nr = ar * x[0] - ai * x[1] + xr_ref[rows, cols]
            ni = ar * x[1] + ai * x[0] + xi_ref[rows, cols]
            xr_ref[rows, cols] = nr
            xi_ref[rows, cols] = ni
            return nr, ni

        zero = jnp.zeros((SUBLANES, cw), F32)
        er, ei = lax.fori_loop(0, seg, local, (zero, zero))

        co = [coef_ref[k, :, cols] for k in range(8)]
        for lvl, d in enumerate((1, 2, 4)):
            kr, ki = co[2 * lvl], co[2 * lvl + 1]
            sh = SUBLANES - d if reverse else d
            sr, si = pltpu.roll(er, sh, 0), pltpu.roll(ei, sh, 0)
            er, ei = er + (kr * sr - ki * si), ei + (kr * si + ki * sr)
        c0r, c0i = car_ref[:, cols], cai_ref[:, cols]
        er, ei = er + (co[6] * c0r - co[7] * c0i), ei + (co[6] * c0i + co[7] * c0r)
        nb_shift = SUBLANES - 1 if reverse else 1
        cmr = jnp.where(row == first, c0r, pltpu.roll(er, nb_shift, 0))
        cmi = jnp.where(row == first, c0i, pltpu.roll(ei, nb_shift, 0))
        car_ref[:, cols] = jnp.broadcast_to(er[last:last + 1, :], er.shape)
        cai_ref[:, cols] = jnp.broadcast_to(ei[last:last + 1, :], ei.shape)
        if cmat is not None:
            cmat[0][:, cols] = cmr
            cmat[1][:, cols] = cmi

        w0 = (ar * cmr - ai * cmi, ar * cmi + ai * cmr)
        if dab is None:
            def fix(i, w, cols=cols, ar=ar, ai=ai):
                rows = tile(i)
                xr_ref[rows, cols] = xr_ref[rows, cols] + w[0]
                xi_ref[rows, cols] = xi_ref[rows, cols] + w[1]
                return ar * w[0] - ai * w[1], ar * w[1] + ai * w[0]

            lax.fori_loop(0, seg, fix, w0)
        else:
            s_re, s_im, e_re, e_im, o_re, o_im = dab

            def add(rows, w, pr, pi, acc):
                gr = xr_ref[rows, cols] + w[0]
                gi = xi_ref[rows, cols] + w[1]
                xr_ref[rows, cols] = gr
                xi_ref[rows, cols] = gi
                return acc[0] + (gr * pr + gi * pi), acc[1] + (gi * pr - gr * pi)

            def fix(i, st, cols=cols, ar=ar, ai=ai):
                w, acc = st[:2], st[2:]
                rows = tile(i)
                before = pl.ds(pl.multiple_of((seg - 2 - i) * SUBLANES, SUBLANES), SUBLANES)
                acc = add(rows, w, s_re[before, cols], s_im[before, cols], acc)
                return (ar * w[0] - ai * w[1], ar * w[1] + ai * w[0]) + acc

            st = lax.fori_loop(0, seg - 1, fix, w0 + (zero, zero))
            acc = add(pl.ds(0, SUBLANES), st[:2], e_re[:, cols], e_im[:, cols], st[2:])
            o_re[:, cols] += jnp.sum(acc[0], axis=0, keepdims=True)
            o_im[:, cols] += jnp.sum(acc[1], axis=0, keepdims=True)


def _hosted(core, comm, grid, n_in, n_out, n_scratch):
    ci = len(comm["ins"]) if comm else 0
    co = len(comm["out_shape"]) if comm else 0

    def body(*refs):
        ins, rest = refs[:n_in + ci], refs[n_in + ci:]
        outs, scr = rest[:n_out + co], rest[n_out + co:]
        hooks = functools.partial(_comm_hooks, comm, grid, ins[n_in:], outs[n_out:], scr[n_scratch:])
        hooks(before=True)
        core(*ins[:n_in], *outs[:n_out], *scr[:n_scratch])
        hooks(before=False)

    extra = dict(ins=list(comm["ins"]) if comm else [], in_specs=[ANY] * ci, out_specs=[ANY] * co,
                 out_shape=list(comm["out_shape"]) if comm else [], scratch=list(comm["scratch"]) if comm else [])
    return body, extra


def _ssm_fwd(proj, bdr, bdi, cdr, cdi, abr, abi, dsk, P, tb, comm=None):
    T = proj.shape[0]
    ntl, ct, st = bdr.shape
    ns = ntl * st
    nb = T // tb

    def core(u_ref, bdr_ref, bdi_ref, cdr_ref, cdi_ref, abr_ref, abi_ref, d_ref, pm_ref,
             y_ref, ge_ref, bsr_ref, bsi_ref, sr, si, coef, car, cai, up):
        @pl.when(pl.program_id(0) == 0)
        def _():
            seg_pow = _cpow(abr_ref[...], abi_ref[...], tb // SUBLANES)
            for k, tile in enumerate(_coef_tiles(seg_pow[0], seg_pow[1], False)):
                coef[k] = tile
            car[...] = jnp.zeros_like(car)
            cai[...] = jnp.zeros_like(cai)

        bsr_ref[...] = car[...]
        bsi_ref[...] = cai[...]
        u = u_ref[...]
        ub = _seg_order_rows(pm_ref[...], u.astype(BF16))
        for s in range(ntl):
            us = ub[:, s * ct:(s + 1) * ct]
            sr[:, s * st:(s + 1) * st] = jnp.dot(us, bdr_ref[s], preferred_element_type=F32)
            si[:, s * st:(s + 1) * st] = jnp.dot(us, bdi_ref[s], preferred_element_type=F32)
        _seg_scan(sr, si, abr_ref, abi_ref, coef, car, cai, nrows=tb, ns=ns, reverse=False)
        for s in range(ntl):
            s_re = sr[:, s * st:(s + 1) * st].astype(BF16)
            s_im = si[:, s * st:(s + 1) * st].astype(BF16)
            up[:, s * ct:(s + 1) * ct] = (jnp.dot(s_re, cdr_ref[s], preferred_element_type=F32)
                                          - jnp.dot(s_im, cdi_ref[s], preferred_element_type=F32))
        y = _time_order_rows(pm_ref[...], up[...], 3) + d_ref[...] * u
        y_ref[...] = y
        ge_ref[...] = _gelu(y).astype(BF16)

    full3 = lambda a: _bs(a.shape, lambda i: (0, 0, 0))
    vec = lambda n: _bs((1, n), lambda i: (0, 0))
    row = _bs((tb, P), lambda i: (i, 0))
    st_spec = _bs((None, SUBLANES, ns), lambda i: (i, 0, 0))
    body, extra = _hosted(core, comm, (nb,), 9, 4, 6)
    return _call(body, grid=(nb,),
                 in_specs=[_bs((tb, P), lambda i: (i, 2)), full3(bdr), full3(bdi), full3(cdr), full3(cdi),
                           vec(ns), vec(ns), vec(P), _bs((tb, tb), lambda i: (0, 0))] + extra["in_specs"],
                 out_specs=[row, row, st_spec, st_spec] + extra["out_specs"],
                 out_shape=[jax.ShapeDtypeStruct((T, P), F32), jax.ShapeDtypeStruct((T, P), BF16),
                            jax.ShapeDtypeStruct((nb, SUBLANES, ns), F32),
                            jax.ShapeDtypeStruct((nb, SUBLANES, ns), F32)] + extra["out_shape"],
                 scratch_shapes=[pltpu.VMEM((tb, ns), F32), pltpu.VMEM((tb, ns), F32),
                                 pltpu.VMEM((8, SUBLANES, ns), F32),
                                 pltpu.VMEM((SUBLANES, ns), F32), pltpu.VMEM((SUBLANES, ns), F32),
                                 pltpu.VMEM((tb, P), F32)] + extra["scratch"],
                 compiler_params=_cp(("arbitrary",)), name="ssm_fwd")(
                     proj, bdr, bdi, cdr, cdi, abr, abi, dsk, _seg_perm_matrix(tb), *extra["ins"])


def _ssm_bwd(proj, y, dge, bsr, bsi, bdr, bdi, cdr, cdi, abr, abi, dsk, dpi, dpg, dsg, P, tb, comm=None):
    T = proj.shape[0]
    ntl, ct, st = bdr.shape
    ns = ntl * st
    nb = T // tb

    def core(u_ref, y_ref, dge_ref, bsr_ref, bsi_ref, abr_ref, abi_ref, d_ref, pm_ref, dpi_ref, dpg_ref, dsg_ref,
             bdr_h, bdi_h, cdr_h, cdi_h,
             dproj_ref, dabr_ref, dabi_ref, dd_ref, dbdr_h, dbdi_h, dcdr_h, dcdi_h,
             wbdr, wbdi, wcdr, wcdi, abdr, abdi, acdr, acdi, spr, spi, gr, gi, coef_f, coef_r,
             car, cai, gcr, gci, ser, sei, dup):
        i = pl.program_id(0)

        @pl.when(i == 0)
        def _():
            for h, w in ((bdr_h, wbdr), (bdi_h, wbdi), (cdr_h, wcdr), (cdi_h, wcdi)):
                pltpu.sync_copy(h, w)
            for a in (abdr, abdi, acdr, acdi, gcr, gci):
                a[...] = jnp.zeros_like(a)
            for o in (dabr_ref, dabi_ref, dd_ref):
                o[...] = jnp.zeros_like(o)
            seg_pow = _cpow(abr_ref[...], abi_ref[...], tb // SUBLANES)
            for k, tile in enumerate(_coef_tiles(seg_pow[0], seg_pow[1], False)):
                coef_f[k] = tile
            for k, tile in enumerate(_coef_tiles(seg_pow[0], seg_pow[1], True)):
                coef_r[k] = tile

        car[...] = bsr_ref[...]
        cai[...] = bsi_ref[...]
        u = u_ref[...]
        dy = dge_ref[...] * _gelu_grad(y_ref[...])
        ub = _seg_order_rows(pm_ref[...], u.astype(BF16))
        dyb = _seg_order_rows(pm_ref[...], dy.astype(BF16))
        for s in range(ntl):
            us = ub[:, s * ct:(s + 1) * ct]
            spr[:, s * st:(s + 1) * st] = jnp.dot(us, wbdr[s], preferred_element_type=F32)
            spi[:, s * st:(s + 1) * st] = jnp.dot(us, wbdi[s], preferred_element_type=F32)
        _seg_scan(spr, spi, abr_ref, abi_ref, coef_f, car, cai, nrows=tb, ns=ns, reverse=False, cmat=(ser, sei))

        for s in range(ntl):
            dys = dyb[:, s * ct:(s + 1) * ct]
            gr[:, s * st:(s + 1) * st] = lax.dot_general(dys, wcdr[s], (NT, ((), ())), preferred_element_type=F32)
            gi[:, s * st:(s + 1) * st] = -lax.dot_general(dys, wcdi[s], (NT, ((), ())), preferred_element_type=F32)
        _seg_scan(gr, gi, abr_ref, abi_ref, coef_r, gcr, gci, nrows=tb, ns=ns, reverse=True,
                  dab=(spr, spi, ser, sei, dabr_ref, dabi_ref))

        for s in range(ntl):
            sl_c, sl_s = slice(s * ct, (s + 1) * ct), slice(s * st, (s + 1) * st)
            s_re = spr[:, sl_s].astype(BF16)
            s_im = spi[:, sl_s].astype(BF16)
            g_re, g_im = gr[:, sl_s].astype(BF16), gi[:, sl_s].astype(BF16)
            dys, us = dyb[:, sl_c], ub[:, sl_c]
            acdr[s] += lax.dot_general(s_re, dys, (TN, ((), ())), preferred_element_type=F32)
            acdi[s] -= lax.dot_general(s_im, dys, (TN, ((), ())), preferred_element_type=F32)
            abdr[s] += lax.dot_general(us, g_re, (TN, ((), ())), preferred_element_type=F32)
            abdi[s] += lax.dot_general(us, g_im, (TN, ((), ())), preferred_element_type=F32)
            dup[:, sl_c] = (lax.dot_general(g_re, wbdr[s], (NT, ((), ())), preferred_element_type=F32)
                            + lax.dot_general(g_im, wbdi[s], (NT, ((), ())), preferred_element_type=F32))
        dd_ref[...] += jnp.sum(dy * u, axis=0, keepdims=True)
        du = _time_order_rows(pm_ref[...], dup[...], 2) + d_ref[...] * dy
        dproj_ref[:, 0:P] = dpi_ref[...]
        dproj_ref[:, P:2 * P] = dpg_ref[...]
        dproj_ref[:, 2 * P:3 * P] = du.astype(BF16)
        dproj_ref[:, 3 * P:4 * P] = dsg_ref[...]

        @pl.when(i == nb - 1)
        def _():
            for a, h in ((abdr, dbdr_h), (abdi, dbdi_h), (acdr, dcdr_h), (acdi, dcdi_h)):
                pltpu.sync_copy(a, h)

    rev = lambda i: nb - 1 - i
    vec = lambda n: _bs((1, n), lambda i: (0, 0))
    row = _bs((tb, P), lambda i: (rev(i), 0))
    st_spec = _bs((None, SUBLANES, ns), lambda i: (rev(i), 0, 0))
    bshape = jax.ShapeDtypeStruct(bdr.shape, F32)
    cshape = jax.ShapeDtypeStruct(cdr.shape, F32)
    body, extra = _hosted(core, comm, (nb,), 16, 8, 21)
    return _call(body, grid=(nb,),
                 in_specs=[_bs((tb, P), lambda i: (rev(i), 2)), row, row, st_spec, st_spec,
                           vec(ns), vec(ns), vec(P), _bs((tb, tb), lambda i: (0, 0)), row, row, row,
                           ANY, ANY, ANY, ANY] + extra["in_specs"],
                 out_specs=[_bs((tb, 4 * P), lambda i: (rev(i), 0)), vec(ns), vec(ns), vec(P), ANY, ANY, ANY, ANY]
                 + extra["out_specs"],
                 out_shape=[jax.ShapeDtypeStruct((T, 4 * P), BF16), jax.ShapeDtypeStruct((1, ns), F32),
                            jax.ShapeDtypeStruct((1, ns), F32), jax.ShapeDtypeStruct((1, P), F32),
                            bshape, bshape, cshape, cshape] + extra["out_shape"],
                 scratch_shapes=[pltpu.VMEM(bdr.shape, BF16), pltpu.VMEM(bdr.shape, BF16),
                                 pltpu.VMEM(cdr.shape, BF16), pltpu.VMEM(cdr.shape, BF16),
                                 pltpu.VMEM(bdr.shape, F32), pltpu.VMEM(bdr.shape, F32),
                                 pltpu.VMEM(cdr.shape, F32), pltpu.VMEM(cdr.shape, F32),
                                 pltpu.VMEM((tb, ns), F32), pltpu.VMEM((tb, ns), F32),
                                 pltpu.VMEM((tb, ns), F32), pltpu.VMEM((tb, ns), F32),
                                 pltpu.VMEM((8, SUBLANES, ns), F32), pltpu.VMEM((8, SUBLANES, ns), F32)]
                 + [pltpu.VMEM((SUBLANES, ns), F32)] * 6 + [pltpu.VMEM((tb, P), F32)] + extra["scratch"],
                 compiler_params=_cp(("arbitrary",)), name="ssm_bwd")(
                     proj, y, dge, bsr, bsi, abr, abi, dsk, _seg_perm_matrix(tb), dpi, dpg, dsg,
                     bdr, bdi, cdr, cdi, *extra["ins"])


def _adamw(w, g, m, v, name):
    R, C = w.shape
    tr = _t(R, 256)

    def body(w_ref, g_ref, m_ref, v_ref, d_ref, mo_ref, vo_ref):
        gv = g_ref[...]
        mn = ADAM_B1 * m_ref[...] + (1.0 - ADAM_B1) * gv
        vn = ADAM_B2 * v_ref[...] + (1.0 - ADAM_B2) * (gv * gv)
        m_hat = mn / (1.0 - ADAM_B1 ** ADAM_STEP)
        v_hat = vn / (1.0 - ADAM_B2 ** ADAM_STEP)
        d_ref[...] = -ADAM_LR * (m_hat / (jnp.sqrt(v_hat) + ADAM_EPS) + ADAM_WD * w_ref[...])
        mo_ref[...] = mn
        vo_ref[...] = vn

    blk = _bs((tr, C), lambda i: (i, 0))
    shp = jax.ShapeDtypeStruct((R, C), F32)
    return _call(body, grid=(R // tr,), in_specs=[blk] * 4, out_specs=[blk] * 3, out_shape=[shp] * 3,
                 compiler_params=_cp(("parallel",)), name=name)(w, g, m, v)


def _sum_cast(grad, got, place, name):
    J, H, C = got.shape
    tr = _t(H, 256)
    nb = H // tr

    def body(pl_ref, a_ref, b_ref, o_ref):
        o_ref[...] = (a_ref[...] + b_ref[...]).astype(BF16)

    blk = _bs((None, tr, C), lambda j, i, pc: (j, i, 0))
    mine = _bs((None, tr, C), lambda j, i, pc: (j, pc[1] * nb + i, 0))
    spec = pltpu.PrefetchScalarGridSpec(num_scalar_prefetch=1, grid=(J, nb), in_specs=[mine, blk], out_specs=blk)
    return _call(body, grid_spec=spec, out_shape=jax.ShapeDtypeStruct((J, H, C), BF16),
                 compiler_params=_cp(("parallel", "parallel")), name=name)(place, grad, got)


def _sum_chips(sent, arrived, place, name):
    J, H, C = arrived.shape
    tr = _t(H, 256)
    nb = H // tr

    def body(pl_ref, own_ref, a0_ref, a1_ref, a2_ref, o_ref):
        acc = own_ref[...].astype(F32)
        for r in (a0_ref, a1_ref, a2_ref):
            acc = acc + r[...].astype(F32)
        o_ref[...] = acc

    def other(k):
        return _bs((None, tr, C), lambda i, pc: (jnp.where(pc[0] <= k, k + 1, k), i, 0))

    spec = pltpu.PrefetchScalarGridSpec(
        num_scalar_prefetch=1, grid=(nb,),
        in_specs=[_bs((None, tr, C), lambda i, pc: (pc[0], i, 0)), other(0), other(1), other(2)],
        out_specs=_bs((tr, C), lambda i, pc: (pc[1] * nb + i, 0)))
    return _call(body, grid_spec=spec, out_shape=jax.ShapeDtypeStruct((2 * H, C), F32),
                 compiler_params=_cp(("parallel",)), name=name)(place, sent, arrived, arrived, arrived)


def _place():
    x, y, c = lax.axis_index("x"), lax.axis_index("y"), lax.axis_index("c")
    chips = [(1 - x, y), (x, 1 - y), (1 - x, 1 - y)]
    return x, y, c, chips


def _split(nrows, row_bytes, align, cap=None):
    k = max(1, min(cap or DMA_MAX_CHUNKS, (nrows * row_bytes) // DMA_CHUNK_BYTES))
    while k > 1 and nrows % (k * align):
        k -= 1
    return k


def _comm_call(plan, name):
    n_in, n_out = len(plan["ins"]), len(plan["out_shape"])

    def body(*refs):
        for phase in plan["phases"]:
            phase(refs[:n_in], refs[n_in:n_in + n_out], refs[n_in + n_out:])

    return _call(body, in_specs=[ANY] * n_in, out_specs=[ANY] * n_out, out_shape=plan["out_shape"],
                 scratch_shapes=plan["scratch"], name=name)(*plan["ins"])


def _comm_hooks(plan, grid, ins, outs, sems, *, before):
    if plan is None:
        return
    nsteps, step = 1, 0
    for d, g in enumerate(grid):
        nsteps, step = nsteps * g, step * g + pl.program_id(d)
    for p, (phase, frac) in enumerate(zip(plan["phases"], plan["at"])):
        if (p == 0) == before:
            pl.when(step == int(frac * (nsteps - 1)))(functools.partial(phase, ins, outs, sems))


def _ag_plan(shards, axes):
    n = len(shards)
    shapes = [a.shape for a in shards]

    def window(ref, i, chip, half=None):
        S, ax = shapes[i], axes[i]
        idx = []
        for d in range(len(S)):
            off, size = 0, S[d]
            if d == 0 and half is not None:
                off, size = half * (S[0] // 2), S[0] // 2
            if d == ax:
                off = off + chip * S[ax]
            idx.append(pl.ds(off, size))
        return ref.at[tuple(idx)]

    def copies(src, full, sems):
        ssem, rsem = sems
        x, y, c, chips = _place()
        me = 2 * x + y
        sib = (x, y, 1 - c)
        idx = [2 * cx + cy for cx, cy in chips]

        def rcopy(i, k, s_ref, d_ref, to):
            return pltpu.make_async_remote_copy(src_ref=s_ref, dst_ref=d_ref, send_sem=ssem.at[i, k],
                                                recv_sem=rsem.at[i, k], device_id=to, device_id_type=MESH)

        def ici(i, j, incoming):
            half_src = src[i].at[pl.ds(c * (shapes[i][0] // 2), shapes[i][0] // 2)]
            return rcopy(i, j, half_src, window(full[i], i, idx[j] if incoming else me, c), (*chips[j], c))

        def fwd(i, j, half):
            w = window(full[i], i, idx[j], half)
            return rcopy(i, 3 + j, w, w, sib)

        def own(i):
            return rcopy(i, 6, src[i], window(full[i], i, me), sib)

        return c, ici, fwd, own

    def send(src, full, sems):
        c, ici, fwd, own = copies(src, full, sems)
        for i in range(n):
            for j in range(3):
                ici(i, j, False).start()
        for i in range(n):
            own(i).start()

    def forward(i, src, full, sems):
        c, ici, fwd, own = copies(src, full, sems)
        for j in range(3):
            ici(i, j, True).wait_recv()
            fwd(i, j, c).start()

    def finish(src, full, sems):
        c, ici, fwd, own = copies(src, full, sems)
        for i in range(n):
            for j in range(3):
                fwd(i, j, 1 - c).wait_recv()
            own(i).wait()
        for i in range(n):
            for j in range(3):
                ici(i, j, False).wait_send()
                fwd(i, j, c).wait_send()

    out_shape = [jax.ShapeDtypeStruct(tuple(N_CHIP * d if k == ax else d for k, d in enumerate(S)), BF16)
                 for S, ax in zip(shapes, axes)]
    sizes = [a.size for a in shards]
    behind = [0.85 * sum(sizes[:i + 1]) / sum(sizes) + 0.05 for i in range(n)]
    return dict(ins=list(shards), out_shape=out_shape,
                phases=[send] + [functools.partial(forward, i) for i in range(n)] + [finish],
                at=[0.0] + behind + [1.0],
                scratch=[pltpu.SemaphoreType.DMA((n, 7)), pltpu.SemaphoreType.DMA((n, 7))])


def _proj_ag(hn, wsh, order, tm):
    T, D = hn.shape
    P = wsh.shape[1]
    H = D // 2
    nt = T // tm

    def body(order_ref, hn_ref, wsh_ref, proj_ref, win_ref, wbuf, lsem, ssem, rsem):
        n, i = pl.program_id(0), pl.program_id(1)
        x, y, c, chips = _place()
        me = 2 * x + y
        sib = (x, y, 1 - c)
        idx = [2 * cx + cy for cx, cy in chips]

        def rcopy(k, s_ref, d_ref, to):
            return pltpu.make_async_remote_copy(src_ref=s_ref, dst_ref=d_ref, send_sem=ssem.at[k],
                                                recv_sem=rsem.at[k], device_id=to, device_id_type=MESH)

        def cols(chip):
            return pl.ds(pl.multiple_of(chip * P, LANES), P)

        def rows(half):
            return pl.ds(pl.multiple_of(half * H, 16), H)

        def ici(j, incoming):
            return rcopy(j, wsh_ref.at[rows(c)], win_ref.at[rows(c), cols(idx[j] if incoming else me)],
                         (*chips[j], c))

        def fwd(j, half):
            w = win_ref.at[rows(half), cols(idx[j])]
            return rcopy(3 + j, w, w, sib)

        def own():
            return rcopy(6, wsh_ref, win_ref.at[:, cols(me)], sib)

        def load(src):
            cp = pltpu.make_async_copy(src, wbuf, lsem)
            cp.start()
            cp.wait()

        @pl.when((n == 0) & (i == 0))
        def _():
            ici(0, False).start()
            ici(1, False).start()
            own().start()
            load(wsh_ref)

        for j in range(3):
            @pl.when((n == j + 1) & (i == 0))
            def _(j=j):
                if j == 0:
                    ici(2, False).start()
                ici(j, True).wait_recv()
                fwd(j, c).start()
                fwd(j, 1 - c).wait_recv()
                load(win_ref.at[:, cols(idx[j])])

        proj_ref[...] = jnp.dot(hn_ref[...], wbuf[...], preferred_element_type=F32)

        @pl.when((n == 3) & (i == nt - 1))
        def _():
            own().wait()
            for j in range(3):
                ici(j, False).wait_send()
                fwd(j, c).wait_send()

    spec = pltpu.PrefetchScalarGridSpec(
        num_scalar_prefetch=1, grid=(N_CHIP, nt),
        in_specs=[_bs((tm, D), lambda n, i, o: (i, 0)), ANY],
        out_specs=[_bs((tm, P), lambda n, i, o: (i, o[n])), ANY],
        scratch_shapes=[pltpu.VMEM((D, P), BF16), pltpu.SemaphoreType.DMA,
                        pltpu.SemaphoreType.DMA((7,)), pltpu.SemaphoreType.DMA((7,))])
    return _call(body, grid_spec=spec,
                 out_shape=[jax.ShapeDtypeStruct((T, N_CHIP * P), F32), jax.ShapeDtypeStruct((D, N_CHIP * P), BF16)],
                 compiler_params=_cp(("arbitrary", "arbitrary")), name="proj_ag")(order, hn, wsh)


def _halves_plan(grads):
    n = len(grads)

    def send(g, got, sems):
        ssem, rsem = sems
        x, y, c, _ = _place()
        sib = (x, y, 1 - c)
        for i in range(n):
            J, R, C = g[i].shape
            H = R // 2
            k = _split(H, C * 4, SUBLANES, cap=DMA_MAX_CHUNKS // J)
            hr = H // k
            for j in range(J):
                for q in range(k):
                    other = pl.ds(pl.multiple_of((1 - c) * H + q * hr, SUBLANES), hr)
                    to = pl.ds(q * hr, hr)
                    pltpu.make_async_remote_copy(src_ref=g[i].at[j, other, :], dst_ref=got[i].at[j, to, :],
                                                 send_sem=ssem.at[i], recv_sem=rsem.at[i],
                                                 device_id=sib, device_id_type=MESH).start()

    def finish(g, got, sems):
        ssem, rsem = sems
        x, y, c, _ = _place()
        for i in range(n):
            pltpu.make_async_remote_copy(src_ref=got[i], dst_ref=got[i], send_sem=ssem.at[i], recv_sem=rsem.at[i],
                                         device_id=(x, y, 1 - c), device_id_type=MESH).wait()

    half = [jax.ShapeDtypeStruct((a.shape[0], a.shape[1] // 2, a.shape[2]), a.dtype) for a in grads]
    return dict(ins=list(grads), out_shape=half, phases=[send, finish], at=[0.0, 1.0],
                scratch=[pltpu.SemaphoreType.DMA((n,)), pltpu.SemaphoreType.DMA((n,))])


def _scatter_plan(parts):
    n = len(parts)

    def peers():
        x, y, c, chips = _place()
        return 2 * x + y, c, chips, [2 * cx + cy for cx, cy in chips]

    def send(s, got, sems):
        ssem, rsem = sems
        me, c, chips, idx = peers()
        for i in range(n):
            _, H, C = s[i].shape
            k = _split(H, C * 2, 16, cap=RS_CHUNKS)
            hr = H // k
            for q in range(k):
                rows = pl.ds(q * hr, hr)
                for j in range(3):
                    pltpu.make_async_remote_copy(src_ref=s[i].at[idx[j], rows, :], dst_ref=got[i].at[me, rows, :],
                                                 send_sem=ssem.at[i, j], recv_sem=rsem.at[i, j],
                                                 device_id=(*chips[j], c), device_id_type=MESH).start()

    def finish(s, got, sems):
        ssem, rsem = sems
        me, c, chips, idx = peers()
        for i in range(n):
            for j in range(3):
                pltpu.make_async_remote_copy(src_ref=s[i].at[idx[j]], dst_ref=got[i].at[idx[j]],
                                             send_sem=ssem.at[i, j], recv_sem=rsem.at[i, j],
                                             device_id=(*chips[j], c), device_id_type=MESH).wait()

    return dict(ins=list(parts), out_shape=[jax.ShapeDtypeStruct(a.shape, a.dtype) for a in parts],
                phases=[send, finish], at=[0.0, 1.0],
                scratch=[pltpu.SemaphoreType.DMA((n, 3)), pltpu.SemaphoreType.DMA((n, 3))])


def _join_halves(shards):
    n = len(shards)

    def body(*refs):
        full = refs[n:2 * n]
        ssem, rsem = refs[2 * n:]
        x, y, c, _ = _place()
        sib = (x, y, 1 - c)
        for i in range(n):
            H, C = full[i].shape[0] // 2, full[i].shape[1]
            k = _split(H, C * 4, SUBLANES)
            hr = H // k
            for q in range(k):
                rows = pl.ds(pl.multiple_of(c * H + q * hr, SUBLANES), hr)
                pltpu.make_async_remote_copy(src_ref=full[i].at[rows], dst_ref=full[i].at[rows],
                                             send_sem=ssem.at[i], recv_sem=rsem.at[i],
                                             device_id=sib, device_id_type=MESH).start()
        for i in range(n):
            half = full[i].at[pl.ds(0, full[i].shape[0] // 2)]
            pltpu.make_async_remote_copy(src_ref=half, dst_ref=half, send_sem=ssem.at[i], recv_sem=rsem.at[i],
                                         device_id=sib, device_id_type=MESH).wait()

    shp = [jax.ShapeDtypeStruct(a.shape, a.dtype) for a in shards]
    return _call(body, in_specs=[ANY] * n, out_specs=[ANY] * n, out_shape=shp,
                 input_output_aliases={i: i for i in range(n)},
                 scratch_shapes=[pltpu.SemaphoreType.DMA((n,)), pltpu.SemaphoreType.DMA((n,))],
                 name="rs_join")(*shards)


def _allreduce_small(buf):
    R, L = buf.shape
    RB = R // N_DEV

    def body(x_ref, o_ref, got, ssem, rsem):
        x, y, c, _ = _place()
        me = 4 * x + 2 * y + c

        def dev(k):
            return (k // 4, (k // 2) % 2, k % 2)

        def slab(k):
            return pl.ds(pl.multiple_of(k * RB, SUBLANES), RB)

        sends = []
        for d in range(1, N_DEV):
            peer = (me + d) % N_DEV
            cp = pltpu.make_async_remote_copy(src_ref=x_ref.at[slab(peer)], dst_ref=got.at[me],
                                              send_sem=ssem.at[0, d], recv_sem=rsem.at[0, d],
                                              device_id=dev(peer), device_id_type=MESH)
            cp.start()
            sends.append(cp)
        got[me] = x_ref[slab(me), :]
        for d in range(1, N_DEV):
            src = (me + N_DEV - d) % N_DEV
            pltpu.make_async_remote_copy(src_ref=x_ref.at[slab(me)], dst_ref=got.at[src],
                                         send_sem=ssem.at[0, d], recv_sem=rsem.at[0, d],
                                         device_id=dev(src), device_id_type=MESH).wait_recv()
        acc = got[0]
        for k in range(1, N_DEV):
            acc = acc + got[k]
        o_ref[slab(me), :] = acc
        for d in range(1, N_DEV):
            peer = (me + d) % N_DEV
            cp = pltpu.make_async_remote_copy(src_ref=o_ref.at[slab(me)], dst_ref=o_ref.at[slab(me)],
                                              send_sem=ssem.at[1, d], recv_sem=rsem.at[1, d],
                                              device_id=dev(peer), device_id_type=MESH)
            cp.start()
            sends.append(cp)
        for d in range(1, N_DEV):
            src = (me + N_DEV - d) % N_DEV
            pltpu.make_async_remote_copy(src_ref=o_ref.at[slab(src)], dst_ref=o_ref.at[slab(src)],
                                         send_sem=ssem.at[1, d], recv_sem=rsem.at[1, d],
                                         device_id=dev(src), device_id_type=MESH).wait_recv()
        for cp in sends:
            cp.wait_send()

    return _call(body, in_specs=[VMEM_FULL], out_specs=VMEM_FULL, out_shape=jax.ShapeDtypeStruct((R, L), F32),
                 scratch_shapes=[pltpu.VMEM((N_DEV, RB, L), F32), pltpu.SemaphoreType.DMA((2, N_DEV)),
                                 pltpu.SemaphoreType.DMA((2, N_DEV))],
                 name="allreduce_small")(buf)


def _block_diag(t, gt):
    G, A, B = t.shape
    t4 = t.reshape(G // gt, gt, A, B)
    eye = jnp.eye(gt, dtype=t.dtype)
    return jnp.einsum('sgab,gh->sgahb', t4, eye).reshape(G // gt, gt * A, gt * B)


def _block_diag_extract(m, gt, A, B):
    S = m.shape[0]
    m5 = m.reshape(S, gt, A, gt, B)
    eye = jnp.eye(gt, dtype=m.dtype)
    return jnp.einsum('sgahb,gh->sgab', m5, eye).reshape(S * gt, A, B)


def _pack_small(arrs, rows):
    flat = jnp.concatenate([a.reshape(-1).astype(F32) for a in arrs])
    return jnp.pad(flat, (0, rows * LANES - flat.shape[0])).reshape(rows, LANES)


def _unpack_small(buf, shapes):
    flat = buf.reshape(-1)
    out, off = [], 0
    for s in shapes:
        n = 1
        for d in s:
            n *= d
        out.append(flat[off:off + n].reshape(s))
        off += n
    return out


def kernel(x, p, norm_gain, w_in, w_pool, pool_scale, a_re, a_im, log_dt, b_re, b_im, c_re, c_im, d_skip, w_glu, w_out, w_ple, w_ple_gate, final_gain, loss_target, m_norm_gain, m_w_in, m_w_pool, m_pool_scale, m_a_re, m_a_im, m_log_dt, m_b_re, m_b_im, m_c_re, m_c_im, m_d_skip, m_w_glu, m_w_out, m_w_ple, m_w_ple_gate, m_final_gain, v_norm_gain, v_w_in, v_w_pool, v_pool_scale, v_a_re, v_a_im, v_log_dt, v_b_re, v_b_im, v_c_re, v_c_im, v_d_skip, v_w_glu, v_w_out, v_w_ple, v_w_ple_gate, v_final_gain):
    xs, pe, tgt = x[0], p[0, 0], loss_target[0]
    T, D = xs.shape
    E = pe.shape[1]
    P = D // 2
    NG = len(POOL_WINDOWS)
    PG = P // NG
    G, N, C = P // SSM_GROUP, SSM_STATE, SSM_GROUP
    GT = min(SSM_TILE_GROUPS, G)
    Q = D // N_CHIP

    big = {"w_in": (w_in, m_w_in, v_w_in), "w_pool": (w_pool, m_w_pool, v_w_pool),
           "w_glu": (w_glu, m_w_glu, v_w_glu), "w_out": (w_out, m_w_out, v_w_out),
           "w_ple": (w_ple, m_w_ple, v_w_ple), "w_ple_gate": (w_ple_gate, m_w_ple_gate, v_w_ple_gate)}
    big_names = list(big)
    shard2d = {n: (big[n][0].size // big[n][0].shape[-1], big[n][0].shape[-1]) for n in big_names}
    shard_axis = {"w_in": 1, "w_pool": 1, "w_glu": 1, "w_out": 0, "w_ple": 1, "w_ple_gate": 0}
    shard16 = {n: big[n][0][0].astype(BF16) for n in big_names}
    place = jnp.stack([2 * lax.axis_index("x") + lax.axis_index("y"), lax.axis_index("c")]).astype(jnp.int32)
    mx, my = lax.axis_index("x"), lax.axis_index("y")
    block_order = jnp.stack([2 * mx + my, 2 * (1 - mx) + my, 2 * mx + (1 - my),
                             2 * (1 - mx) + (1 - my)]).astype(jnp.int32)
    later = [n for n in big_names if n != "w_in"]
    ag_later = _ag_plan([shard16[n] for n in later], [shard_axis[n] for n in later])

    rep = lambda a: jnp.repeat(a, C, axis=0)
    a_re_r, a_im_r = rep(a_re[0]), rep(a_im[0])
    ldt_r = rep(jnp.broadcast_to(log_dt[0][:, None], (G, N)))
    bt_re = b_re[0].transpose(0, 2, 1).reshape(G * C, N)
    bt_im = b_im[0].transpose(0, 2, 1).reshape(G * C, N)
    ab_re_r, ab_im_r, bbt_re, bbt_im = _ssm_prep(a_re_r, a_im_r, ldt_r, bt_re, bt_im)
    abr = ab_re_r[::C].reshape(1, G * N)
    abi = ab_im_r[::C].reshape(1, G * N)
    bdr = _block_diag(bbt_re.reshape(G, C, N), GT).astype(BF16)
    bdi = _block_diag(bbt_im.reshape(G, C, N), GT).astype(BF16)
    cdr = _block_diag(c_re[0].transpose(0, 2, 1), GT).astype(BF16)
    cdi = _block_diag(c_im[0].transpose(0, 2, 1), GT).astype(BF16)

    tb = _t(T, 256)
    tbs = _t(T, 256)
    tm = _t(T, 1024)
    tk = _t(T, 2048)
    DH = _t(D, 1024)
    row_k = lambda i, n, k: (i, k)
    row_n = lambda i, n, k: (i, n)
    f32 = lambda *shape: jax.ShapeDtypeStruct(shape, F32)
    hn = _norm1(xs, norm_gain, tb)
    proj, win = _proj_ag(hn, shard16["w_in"], block_order, tm)
    y, ge, bsr, bsi, wp, wglu, wout, wple, wpg = _ssm_fwd(proj, bdr, bdi, cdr, cdi, abr, abi, d_skip, P, tbs,
                                                          comm=ag_later)
    pooled, mixed = _pool_fwd(proj, wp, P, tb)
    hg = _mm(ge, wglu, dims=NN, grid=(T // tm, 2 * P // DH, 1),
             a_spec=_bs((tm, P), row_k), b_spec=_bs((P, DH), lambda i, n, k: (k, n)),
             o_spec=_bs((tm, DH), row_n), out_shape=f32(T, 2 * P), name="mm_glu")
    cat = _gate_fwd(mixed, proj, hg, pool_scale, tb)
    h1, h1b = _mm(cat, wout, dims=NN, grid=(T // tm, D // DH, 1), res=xs, bf16_copy=True,
                  a_spec=_bs((tm, D), row_k), b_spec=_bs((D, DH), lambda i, n, k: (k, n)),
                  r_spec=_bs((tm, DH), row_n), o_spec=_bs((tm, DH), row_n), out_shape=f32(T, D), name="mm_out")
    z = _mm(h1b, wpg, dims=NN, grid=(T // tm, D // DH, 1),
            a_spec=_bs((tm, D), row_k), b_spec=_bs((D, DH), lambda i, n, k: (k, n)),
            o_spec=_bs((tm, DH), row_n), out_shape=f32(T, D), name="mm_pgate")
    dh2, de, dz, dg2, lpart = _final_fb(h1, pe, wple, z, tgt, final_gain.reshape(1, D), tb)

    col_m = lambda m, n, k: (k, m)
    col_n = lambda m, n, k: (k, n)
    dh1, dh1b = _mm(dz, wpg, dims=NT, grid=(T // tm, D // DH, 1), res=dh2, bf16_copy=True,
                    a_spec=_bs((tm, D), row_k), b_spec=_bs((DH, D), lambda i, n, k: (n, k)),
                    r_spec=_bs((tm, DH), row_n), o_spec=_bs((tm, DH), row_n), out_shape=f32(T, D), name="mm_dh1")
    g_wpg = _mm(h1b, dz, dims=TN, grid=(D // DH, D // DH, T // tk),
                a_spec=_bs((tk, DH), col_m), b_spec=_bs((tk, DH), col_n),
                o_spec=_bs((DH, DH), lambda m, n, k: (m, n)), out_shape=f32(D, D), name="mm_gwpg")
    g_wple = _mm(pe, de, dims=TN, grid=(1, N_CHIP, T // tk),
                 a_spec=_bs((tk, E), col_m), b_spec=_bs((tk, Q), col_n),
                 o_spec=_bs((None, E, Q), lambda m, j, k: (j, 0, 0)), out_shape=f32(N_CHIP, E, Q), name="mm_gwple")
    dcat = _mm(dh1b, wout, dims=NT, grid=(T // tm, D // DH, 1),
               a_spec=_bs((tm, D), row_k), b_spec=_bs((DH, D), lambda i, n, k: (n, k)),
               o_spec=_bs((tm, DH), row_n), out_shape=f32(T, D), name="mm_dcat")
    g_wout = _mm(cat, dh1b, dims=TN, grid=(D // DH, D // DH, T // tk),
                 a_spec=_bs((tk, DH), col_m), b_spec=_bs((tk, DH), col_n),
                 o_spec=_bs((DH, DH), lambda m, n, k: (m, n)), out_shape=f32(D, D), name="mm_gwout")
    gbig = {"w_out": g_wout.reshape(N_CHIP, Q, D), "w_ple": g_wple, "w_ple_gate": g_wpg.reshape(N_CHIP, Q, D)}
    first = list(gbig)
    res = _gate_bwd(dcat, mixed, proj, hg, pool_scale, tb, comm=_halves_plan([gbig[n] for n in first]))
    dmixed, dpg, dsg, dhg, dps = res[:5]
    got = dict(zip(first, res[5:]))
    dge = _mm(dhg, wglu, dims=NT, grid=(T // tm, 1, 1),
              a_spec=_bs((tm, 2 * P), row_k), b_spec=_bs((P, 2 * P), lambda i, n, k: (n, k)),
              o_spec=_bs((tm, P), row_n), out_shape=f32(T, P), name="mm_dge")
    g_wglu = _mm(ge, dhg, dims=TN, grid=(1, N_CHIP, T // tk),
                 a_spec=_bs((tk, P), col_m), b_spec=_bs((tk, Q), col_n),
                 o_spec=_bs((None, P, Q), lambda m, j, k: (j, 0, 0)), out_shape=f32(N_CHIP, P, Q), name="mm_gwglu")
    g_wp = _mm(pooled, dmixed, dims=TN, grid=(NG, 1, T // tk),
               a_spec=_bs((tk, PG), col_m), b_spec=_bs((tk, PG), col_m),
               o_spec=_bs((None, PG, PG), lambda g, n, k: (g, 0, 0)), out_shape=f32(NG, PG, PG), name="mm_gwp")
    gbig["w_pool"] = g_wp.reshape(NG, N_CHIP, PG // N_CHIP, PG).transpose(1, 0, 2, 3).reshape(
        N_CHIP, NG * PG // N_CHIP, PG)
    gbig["w_glu"] = g_wglu
    res = _pool_bwd(dmixed, wp, tb, comm=_halves_plan([gbig["w_pool"], gbig["w_glu"]]))
    dpi, got["w_pool"], got["w_glu"] = res
    early = list(gbig)
    chip_sums = {n: _sum_cast(gbig[n], got[n], place, "sum_cast_" + n) for n in early}
    res = _ssm_bwd(proj, y, dge, bsr, bsi, bdr, bdi, cdr, cdi, abr, abi, d_skip, dpi, dpg, dsg, P, tbs,
                   comm=_scatter_plan([chip_sums[n] for n in early]))
    dproj, dabr, dabi, dd, dbdr, dbdi, dcdr, dcdi = res[:8]
    arrived = dict(zip(early, res[8:]))
    gbig["w_in"] = _mm(hn, dproj, dims=TN, grid=(D // DH, N_CHIP, T // tk),
                       a_spec=_bs((tk, DH), col_m), b_spec=_bs((tk, P), col_n),
                       o_spec=_bs((None, DH, P), lambda m, j, k: (j, m, 0)), out_shape=f32(N_CHIP, D, P),
                       name="mm_gwin")
    got["w_in"], = _comm_call(_halves_plan([gbig["w_in"]]), "rs_halves_late")
    chip_sums["w_in"] = _sum_cast(gbig["w_in"], got["w_in"], place, "sum_cast_w_in")
    KH = _t(4 * P, 2048)
    dhn, arrived["w_in"] = _mm(dproj, win, dims=NT, grid=(T // tm, D // DH, 4 * P // KH),
                               a_spec=_bs((tm, KH), row_k), b_spec=_bs((DH, KH), lambda i, n, k: (n, k)),
                               o_spec=_bs((tm, DH), row_n), out_shape=f32(T, D), name="mm_dhn",
                               comm=_scatter_plan([chip_sums["w_in"]]))
    grad_x, dg1 = _norm1_bwd(xs, dhn, dh1, norm_gain, tb)

    dbbt_re = _block_diag_extract(dbdr, GT, C, N).reshape(G * C, N)
    dbbt_im = _block_diag_extract(dbdi, GT, C, N).reshape(G * C, N)
    g_c_re = _block_diag_extract(dcdr, GT, N, C).transpose(0, 2, 1)
    g_c_im = _block_diag_extract(dcdi, GT, N, C).transpose(0, 2, 1)
    dab_re_r = rep(dabr.reshape(G, N)) * (1.0 / C)
    dab_im_r = rep(dabi.reshape(G, N)) * (1.0 / C)
    g_a_re, g_a_im, g_ldt, g_bt_re, g_bt_im = _ssm_prep_bwd(a_re_r, a_im_r, ldt_r, bt_re, bt_im,
                                                            dab_re_r, dab_im_r, dbbt_re, dbbt_im, G)
    g_b_re = g_bt_re.reshape(G, C, N).transpose(0, 2, 1)
    g_b_im = g_bt_im.reshape(G, C, N).transpose(0, 2, 1)

    halves = [_sum_chips(chip_sums[n], arrived[n], place, "sum_chips_" + n) for n in big_names]
    gshard = _join_halves(halves)

    small_names = ["norm_gain", "pool_scale", "a_re", "a_im", "log_dt", "b_re", "b_im", "c_re", "c_im",
                   "d_skip", "final_gain"]
    small_w = dict(norm_gain=norm_gain, pool_scale=pool_scale, a_re=a_re, a_im=a_im, log_dt=log_dt, b_re=b_re,
                   b_im=b_im, c_re=c_re, c_im=c_im, d_skip=d_skip, final_gain=final_gain)
    small_m = dict(norm_gain=m_norm_gain, pool_scale=m_pool_scale, a_re=m_a_re, a_im=m_a_im, log_dt=m_log_dt,
                   b_re=m_b_re, b_im=m_b_im, c_re=m_c_re, c_im=m_c_im, d_skip=m_d_skip, final_gain=m_final_gain)
    small_v = dict(norm_gain=v_norm_gain, pool_scale=v_pool_scale, a_re=v_a_re, a_im=v_a_im, log_dt=v_log_dt,
                   b_re=v_b_re, b_im=v_b_im, c_re=v_c_re, c_im=v_c_im, d_skip=v_d_skip, final_gain=v_final_gain)
    small_g = dict(norm_gain=dg1, pool_scale=dps, a_re=g_a_re, a_im=g_a_im, log_dt=g_ldt, b_re=g_b_re,
                   b_im=g_b_im, c_re=g_c_re, c_im=g_c_im, d_skip=dd, final_gain=dg2)
    shapes = [small_w[n].shape for n in small_names]
    total = sum(small_w[n].size for n in small_names) + 1
    unit = N_DEV * SUBLANES
    rows = -(-(-(-total // LANES)) // unit) * unit
    gbuf = _pack_small([small_g[n] for n in small_names] + [lpart[0, :1]], rows)
    gsum = _allreduce_small(gbuf)
    wbuf = _pack_small([small_w[n] for n in small_names], rows)
    mbuf = _pack_small([small_m[n] for n in small_names], rows)
    vbuf = _pack_small([small_v[n] for n in small_names], rows)
    dsm, msm, vsm = _adamw(wbuf, gsum, mbuf, vbuf, "adamw_small")
    g_small = dict(zip(small_names, _unpack_small(gsum, shapes)))
    d_small = dict(zip(small_names, _unpack_small(dsm, shapes)))
    m_small = dict(zip(small_names, _unpack_small(msm, shapes)))
    v_small = dict(zip(small_names, _unpack_small(vsm, shapes)))
    loss = gsum.reshape(-1)[total - 1]

    g_out, d_out, m_out, v_out = dict(g_small), dict(d_small), dict(m_small), dict(v_small)
    for n, gs in zip(big_names, gshard):
        w_, m_, v_ = big[n]
        r2 = shard2d[n]
        d_, mn_, vn_ = _adamw(w_.reshape(r2), gs, m_.reshape(r2), v_.reshape(r2), "adamw_" + n)
        g_out[n], d_out[n], m_out[n], v_out[n] = (a.reshape(w_.shape) for a in (gs, d_, mn_, vn_))

    order = ["norm_gain", "w_in", "w_pool", "pool_scale", "a_re", "a_im", "log_dt", "b_re", "b_im", "c_re",
             "c_im", "d_skip", "w_glu", "w_out", "w_ple", "w_ple_gate", "final_gain"]
    return (loss, grad_x[None], *[g_out[n] for n in order], *[d_out[n] for n in order],
            *[m_out[n] for n in order], *[v_out[n] for n in order])
```

```python
import functools

import jax
import jax.numpy as jnp
from jax import lax
from jax.experimental import pallas as pl
from jax.experimental.pallas import tpu as pltpu

F32, BF16 = jnp.float32, jnp.bfloat16
MESH = pl.DeviceIdType.MESH
ANY = pl.BlockSpec(memory_space=pl.ANY)
VMEM_FULL = pl.BlockSpec(memory_space=pltpu.VMEM)

EPS = 1e-6
A_RE_MAX = -1e-4
SSM_GROUP = 16
SSM_STATE = 64
POOL_WINDOWS = (2, 4, 8, 16)
POOL_HALO = 16
ADAM_LR, ADAM_B1, ADAM_B2, ADAM_EPS, ADAM_WD, ADAM_STEP = 0.001, 0.9, 0.999, 1e-08, 0.01, 10

V7X_VMEM_BYTES = 64 * 1024 * 1024
VMEM_LIMIT = V7X_VMEM_BYTES - 8 * 1024 * 1024
SUBLANES, LANES = 8, 128
SSM_TILE_GROUPS = 8
SCAN_LANES = 512
N_DEV, N_CHIP = 8, 4
DMA_CHUNK_BYTES = 256 * 1024
DMA_MAX_CHUNKS = 32
AG_CHUNKS = 8
RS_CHUNKS = 8


def _t(n, pref):
    return pref if n % pref == 0 else n


def _cp(sem=None, vmem=VMEM_LIMIT):
    return pltpu.CompilerParams(dimension_semantics=sem, vmem_limit_bytes=vmem)


def _call(body, **kw):
    return pl.pallas_call(body, **kw)


NN = ((1,), (0,))
NT = ((1,), (1,))
TN = ((0,), (0,))


def _mm(a, b, *, dims, grid, a_spec, b_spec, o_spec, out_shape, name, res=None, r_spec=None, bf16_copy=False,
        comm=None):
    nk, kax = grid[-1], len(grid) - 1
    acc_shape = tuple(d for d in o_spec.block_shape if d is not None)

    def core(*refs):
        refs = list(refs)
        a_ref, b_ref = refs[:2]
        r_ref = refs[2] if res is not None else None
        outs = refs[3 if res is not None else 2:]
        o_ref = outs[0]
        o2_ref = outs[1] if bf16_copy else None
        acc = outs[-1] if nk > 1 else None

        def finish(r):
            if r_ref is not None:
                r = r + r_ref[...]
            o_ref[...] = r.astype(o_ref.dtype)
            if o2_ref is not None:
                o2_ref[...] = r.astype(BF16)

        part = lax.dot_general(a_ref[...].astype(BF16), b_ref[...].astype(BF16),
                               (dims, ((), ())), preferred_element_type=F32)
        if nk == 1:
            finish(part)
        else:
            k = pl.program_id(kax)

            @pl.when(k == 0)
            def _():
                acc[...] = part

            @pl.when(k > 0)
            def _():
                acc[...] += part

            @pl.when(k == nk - 1)
            def _():
                finish(acc[...])

    ins, specs = [a, b], [a_spec, b_spec]
    if res is not None:
        ins.append(res)
        specs.append(r_spec)
    o_specs, o_shapes = [o_spec], [out_shape]
    if bf16_copy:
        o_specs = [o_spec, o_spec]
        o_shapes = [out_shape, jax.ShapeDtypeStruct(out_shape.shape, BF16)]
    scratch = [pltpu.VMEM(acc_shape, F32)] if nk > 1 else []
    body, extra = _hosted(core, comm, grid, len(ins), len(o_specs), len(scratch))
    sem = ("arbitrary",) * len(grid) if comm else ("parallel",) * kax + ("arbitrary",)
    outs = _call(body, grid=grid, in_specs=specs + extra["in_specs"], out_specs=o_specs + extra["out_specs"],
                 out_shape=o_shapes + extra["out_shape"], scratch_shapes=scratch + extra["scratch"],
                 compiler_params=_cp(sem), name=name)(*ins, *extra["ins"])
    return outs[0] if len(outs) == 1 else outs


def _bs(shape, fn):
    return pl.BlockSpec(shape, fn)


def _sigmoid(v):
    return 1.0 / (1.0 + jnp.exp(-v))


def _gelu(v):
    return 0.5 * v * (1.0 + jnp.tanh(0.7978845608028654 * (v + 0.044715 * v * v * v)))


def _gelu_grad(v):
    t = jnp.tanh(0.7978845608028654 * (v + 0.044715 * v * v * v))
    return 0.5 * (1.0 + t) + 0.5 * v * (1.0 - t * t) * 0.7978845608028654 * (1.0 + 3 * 0.044715 * v * v)


def _norm1(x, g1, tb):
    T, D = x.shape

    def body(x_ref, g_ref, o_ref):
        xv = x_ref[...]
        r = lax.rsqrt(jnp.mean(xv * xv, axis=-1, keepdims=True) + EPS)
        o_ref[...] = ((xv * r) * g_ref[...]).astype(BF16)

    return _call(body, grid=(T // tb,),
                 in_specs=[_bs((tb, D), lambda i: (i, 0)), _bs((1, D), lambda i: (0, 0))],
                 out_specs=_bs((tb, D), lambda i: (i, 0)), out_shape=jax.ShapeDtypeStruct((T, D), BF16),
                 compiler_params=_cp(("parallel",)), name="norm1")(x, g1)


def _norm1_bwd(x, dhn, dh1, g1, tb, comm=None):
    T, D = x.shape

    def core(x_ref, dhn_ref, dh1_ref, g_ref, dx_ref, dg_ref):
        @pl.when(pl.program_id(0) == 0)
        def _():
            dg_ref[...] = jnp.zeros_like(dg_ref)

        xv = x_ref[...]
        r = lax.rsqrt(jnp.mean(xv * xv, axis=-1, keepdims=True) + EPS)
        xh = xv * r
        dhn_v = dhn_ref[...]
        dg_ref[...] += jnp.sum(dhn_v * xh, axis=0, keepdims=True)
        dxh = dhn_v * g_ref[...]
        dx_ref[...] = dh1_ref[...] + r * (dxh - xh * jnp.mean(dxh * xh, axis=-1, keepdims=True))

    row = _bs((tb, D), lambda i: (i, 0))
    vec = _bs((1, D), lambda i: (0, 0))
    body, extra = _hosted(core, comm, (T // tb,), 4, 2, 0)
    return _call(body, grid=(T // tb,), in_specs=[row, row, row, vec] + extra["in_specs"],
                 out_specs=[row, vec] + extra["out_specs"],
                 out_shape=[jax.ShapeDtypeStruct((T, D), F32), jax.ShapeDtypeStruct((1, D), F32)] + extra["out_shape"],
                 scratch_shapes=extra["scratch"], input_output_aliases=extra["aliases"],
                 compiler_params=_cp(("arbitrary",)), name="norm1_bwd")(x, dhn, dh1, g1, *extra["ins"])


def _gate_fwd(mixed, proj, hg, ps, tb):
    T, P = mixed.shape

    def body(mx_ref, pg_ref, sg_ref, hg_ref, ps_ref, o_ref):
        pg, sg = pg_ref[...], sg_ref[...]
        ya = (mx_ref[...] * ps_ref[...]) * (pg * _sigmoid(pg))
        hgv = hg_ref[...]
        o = hgv[:, :P] * _sigmoid(hgv[:, P:])
        yb = o * (sg * _sigmoid(sg))
        o_ref[:, :P] = ya.astype(BF16)
        o_ref[:, P:] = yb.astype(BF16)

    return _call(body, grid=(T // tb,),
                 in_specs=[_bs((tb, P), lambda i: (i, 0)), _bs((tb, P), lambda i: (i, 1)),
                           _bs((tb, P), lambda i: (i, 3)), _bs((tb, 2 * P), lambda i: (i, 0)),
                           _bs((1, P), lambda i: (0, 0))],
                 out_specs=_bs((tb, 2 * P), lambda i: (i, 0)),
                 out_shape=jax.ShapeDtypeStruct((T, 2 * P), BF16),
                 compiler_params=_cp(("parallel",)), name="gate_fwd")(mixed, proj, proj, hg, ps)


def _gate_bwd(dcat, mixed, proj, hg, ps, tb, comm=None):
    T, P = mixed.shape

    def core(dc_ref, mx_ref, pg_ref, sg_ref, hg_ref, ps_ref, dmx_ref, dpg_ref, dsg_ref, dhg_ref, dps_ref):
        @pl.when(pl.program_id(0) == 0)
        def _():
            dps_ref[...] = jnp.zeros_like(dps_ref)

        dc = dc_ref[...]
        dya, dyb = dc[:, :P], dc[:, P:]
        pg, sg, mx, psv = pg_ref[...], sg_ref[...], mx_ref[...], ps_ref[...]
        s_pg = _sigmoid(pg)
        dpa = dya * (pg * s_pg)
        dpg_ref[...] = (dya * (mx * psv) * (s_pg * (1.0 + pg * (1.0 - s_pg)))).astype(BF16)
        dps_ref[...] += jnp.sum(dpa * mx, axis=0, keepdims=True)
        dmx_ref[...] = (dpa * psv).astype(BF16)
        hgv = hg_ref[...]
        h1, s_h2 = hgv[:, :P], _sigmoid(hgv[:, P:])
        s_sg = _sigmoid(sg)
        do = dyb * (sg * s_sg)
        dsg_ref[...] = (dyb * (h1 * s_h2) * (s_sg * (1.0 + sg * (1.0 - s_sg)))).astype(BF16)
        dhg_ref[:, :P] = (do * s_h2).astype(BF16)
        dhg_ref[:, P:] = (do * h1 * s_h2 * (1.0 - s_h2)).astype(BF16)

    rowp = _bs((tb, P), lambda i: (i, 0))
    row2 = _bs((tb, 2 * P), lambda i: (i, 0))
    vec = _bs((1, P), lambda i: (0, 0))
    body, extra = _hosted(core, comm, (T // tb,), 6, 5, 0)
    return _call(body, grid=(T // tb,),
                 in_specs=[row2, rowp, _bs((tb, P), lambda i: (i, 1)), _bs((tb, P), lambda i: (i, 3)), row2, vec]
                 + extra["in_specs"],
                 out_specs=[rowp, rowp, rowp, row2, vec] + extra["out_specs"],
                 out_shape=[jax.ShapeDtypeStruct((T, P), BF16), jax.ShapeDtypeStruct((T, P), BF16),
                            jax.ShapeDtypeStruct((T, P), BF16), jax.ShapeDtypeStruct((T, 2 * P), BF16),
                            jax.ShapeDtypeStruct((1, P), F32)] + extra["out_shape"],
                 scratch_shapes=extra["scratch"],
                 compiler_params=_cp(("arbitrary",)), name="gate_bwd")(dcat, mixed, proj, proj, hg, ps, *extra["ins"])


def _final_fb(h1, pe, wple, z, tgt, g2, tb):
    T, D = h1.shape
    E = pe.shape[1]

    def body(h1_ref, p_ref, w_ref, z_ref, t_ref, g_ref, dh2_ref, de_ref, dz_ref, dg_ref, l_ref):
        @pl.when(pl.program_id(0) == 0)
        def _():
            dg_ref[...] = jnp.zeros_like(dg_ref)
            l_ref[...] = jnp.zeros_like(l_ref)

        ev = jnp.dot(p_ref[...].astype(BF16), w_ref[...], preferred_element_type=F32)
        s = _sigmoid(z_ref[...])
        h2 = h1_ref[...] + ev * s
        r = lax.rsqrt(jnp.mean(h2 * h2, axis=-1, keepdims=True) + EPS)
        xh = h2 * r
        gv = g_ref[...]
        diff = xh * gv - t_ref[...]
        l_ref[...] += 0.5 * jnp.sum(jnp.mean(diff * diff, axis=-1, keepdims=True))
        dout = diff * (1.0 / D)
        dg_ref[...] += jnp.sum(dout * xh, axis=0, keepdims=True)
        dxh = dout * gv
        dh2 = r * (dxh - xh * jnp.mean(dxh * xh, axis=-1, keepdims=True))
        dh2_ref[...] = dh2
        de_ref[...] = (dh2 * s).astype(BF16)
        dz_ref[...] = (dh2 * ev * s * (1.0 - s)).astype(BF16)

    row = _bs((tb, D), lambda i: (i, 0))
    vec = _bs((1, D), lambda i: (0, 0))
    return _call(body, grid=(T // tb,),
                 in_specs=[row, _bs((tb, E), lambda i: (i, 0)), _bs((E, D), lambda i: (0, 0)), row, row, vec],
                 out_specs=[row, row, row, vec, _bs((1, LANES), lambda i: (0, 0))],
                 out_shape=[jax.ShapeDtypeStruct((T, D), F32), jax.ShapeDtypeStruct((T, D), BF16),
                            jax.ShapeDtypeStruct((T, D), BF16), jax.ShapeDtypeStruct((1, D), F32),
                            jax.ShapeDtypeStruct((1, LANES), F32)],
                 compiler_params=_cp(("arbitrary",)), name="final_fb")(h1, pe, wple, z, tgt, g2)


def _pool_inv_count(t0, rows, pg, ngroups):
    t = t0 + lax.broadcasted_iota(jnp.int32, (rows, pg), 0)
    parts = []
    for w in POOL_WINDOWS[:ngroups]:
        parts.append(jnp.where(t + 1 >= w, 1.0 / w, 1.0 / (t + 1).astype(F32)))
    return parts


def _pool_fwd(proj, wp, P, tb):
    T = proj.shape[0]
    ng = len(POOL_WINDOWS)
    pg = P // ng
    hb = tb // POOL_HALO

    def body(v_ref, tail_ref, w_ref, o_ref, mx_ref, ext):
        i = pl.program_id(0)
        ext[pl.ds(0, POOL_HALO), :] = jnp.where(i > 0, tail_ref[...], 0.0)
        ext[pl.ds(POOL_HALO, tb), :] = v_ref[...]
        inv = _pool_inv_count(i * tb, tb, pg, ng)
        for g, w in enumerate(POOL_WINDOWS):
            cols = pl.ds(g * pg, pg)
            win = ext[pl.ds(POOL_HALO, tb), cols]
            for k in range(1, w):
                win = win + ext[pl.ds(POOL_HALO - k, tb), cols]
            pooled = (win * inv[g] - ext[pl.ds(POOL_HALO, tb), cols]).astype(BF16)
            o_ref[:, cols] = pooled
            mx_ref[:, cols] = jnp.dot(pooled, w_ref[g], preferred_element_type=F32)

    row = _bs((tb, P), lambda i: (i, 0))
    return _call(body, grid=(T // tb,),
                 in_specs=[row, _bs((POOL_HALO, P), lambda i: (jnp.maximum(i * hb - 1, 0), 0)),
                           _bs(wp.shape, lambda i: (0, 0, 0))],
                 out_specs=[row, row],
                 out_shape=[jax.ShapeDtypeStruct((T, P), BF16), jax.ShapeDtypeStruct((T, P), F32)],
                 scratch_shapes=[pltpu.VMEM((tb + POOL_HALO, P), F32)],
                 compiler_params=_cp(("arbitrary",)), name="pool_fwd")(proj, proj, wp)


def _pool_bwd(dmixed, wp, tb, comm=None):
    T, P = dmixed.shape
    ng = len(POOL_WINDOWS)
    pg = P // ng
    hb = tb // POOL_HALO
    nb = T // tb

    def core(d_ref, head_ref, w_ref, o_ref, ext, dpl):
        i = pl.program_id(0)
        inv = _pool_inv_count(i * tb, tb, pg, ng)
        invh = _pool_inv_count((i + 1) * tb, POOL_HALO, pg, ng)
        for g in range(ng):
            cols = pl.ds(g * pg, pg)
            dp = lax.dot_general(d_ref[:, cols], w_ref[g], (NT, ((), ())), preferred_element_type=F32)
            dph = lax.dot_general(head_ref[:, cols], w_ref[g], (NT, ((), ())), preferred_element_type=F32)
            dpl[:, cols] = dp
            ext[pl.ds(0, tb), cols] = dp * inv[g]
            ext[pl.ds(tb, POOL_HALO), cols] = jnp.where(i < nb - 1, dph * invh[g], 0.0)
        for g, w in enumerate(POOL_WINDOWS):
            cols = pl.ds(g * pg, pg)
            acc = ext[pl.ds(0, tb), cols]
            for k in range(1, w):
                acc = acc + ext[pl.ds(k, tb), cols]
            o_ref[:, cols] = (acc - dpl[:, cols]).astype(BF16)

    body, extra = _hosted(core, comm, (nb,), 3, 1, 2)
    return _call(body, grid=(nb,),
                 in_specs=[_bs((tb, P), lambda i: (i, 0)),
                           _bs((POOL_HALO, P), lambda i: (jnp.minimum((i + 1) * hb, T // POOL_HALO - 1), 0)),
                           _bs(wp.shape, lambda i: (0, 0, 0))] + extra["in_specs"],
                 out_specs=[_bs((tb, P), lambda i: (i, 0))] + extra["out_specs"],
                 out_shape=[jax.ShapeDtypeStruct((T, P), BF16)] + extra["out_shape"],
                 scratch_shapes=[pltpu.VMEM((tb + POOL_HALO, P), F32), pltpu.VMEM((tb, P), F32)] + extra["scratch"],
                 compiler_params=_cp(("arbitrary",)), name="pool_bwd")(dmixed, dmixed, wp, *extra["ins"])


def _zoh(a_re, a_im, ldt, b_re, b_im):
    lam_re = jnp.minimum(a_re, A_RE_MAX)
    lam_im = a_im
    dt = jnp.exp(ldt)
    mag = jnp.exp(lam_re * dt)
    ang = lam_im * dt
    ab_re = mag * jnp.cos(ang)
    ab_im = mag * jnp.sin(ang)
    den = lam_re * lam_re + lam_im * lam_im
    n_re = ab_re - 1.0
    n_im = ab_im
    q_re = (n_re * lam_re + n_im * lam_im) / den
    q_im = (n_im * lam_re - n_re * lam_im) / den
    return ab_re, ab_im, q_re * b_re - q_im * b_im, q_re * b_im + q_im * b_re


def _ssm_prep(a_re, a_im, ldt, bt_re, bt_im):
    shp = jax.ShapeDtypeStruct(a_re.shape, F32)

    def body(a, b, c, d, e, o0, o1, o2, o3):
        r = _zoh(a[...], b[...], c[...], d[...], e[...])
        o0[...], o1[...], o2[...], o3[...] = r

    return _call(body, in_specs=[VMEM_FULL] * 5, out_specs=[VMEM_FULL] * 4, out_shape=[shp] * 4,
                 name="ssm_prep")(a_re, a_im, ldt, bt_re, bt_im)


def _ssm_prep_bwd(a_re, a_im, ldt, bt_re, bt_im, dab_re, dab_im, dbb_re, dbb_im, G):
    GC, N = a_re.shape
    C = GC // G

    def body(a, b, c, d, e, g0, g1, g2, g3, da_re, da_im, dldt, db_re, db_im):
        _, vjp = jax.vjp(_zoh, a[...], b[...], c[...], d[...], e[...])
        ga_re, ga_im, gl, gb_re, gb_im = vjp((g0[...], g1[...], g2[...], g3[...]))
        da_re[...] = jnp.sum(ga_re.reshape(G, C, N), axis=1)
        da_im[...] = jnp.sum(ga_im.reshape(G, C, N), axis=1)
        dldt[...] = jnp.sum(jnp.sum(gl.reshape(G, C, N), axis=1), axis=1, keepdims=True)
        db_re[...] = gb_re
        db_im[...] = gb_im

    gn = jax.ShapeDtypeStruct((G, N), F32)
    full = jax.ShapeDtypeStruct((GC, N), F32)
    return _call(body, in_specs=[VMEM_FULL] * 9, out_specs=[VMEM_FULL] * 5,
                 out_shape=[gn, gn, jax.ShapeDtypeStruct((G, 1), F32), full, full],
                 name="ssm_prep_bwd")(a_re, a_im, ldt, bt_re, bt_im, dab_re, dab_im, dbb_re, dbb_im)


def _coef_tiles(abr, abi, reverse):
    ns = abr.shape[1]
    row = lax.broadcasted_iota(jnp.int32, (SUBLANES, ns), 0)
    ar = jnp.broadcast_to(abr, (SUBLANES, ns))
    ai = jnp.broadcast_to(-abi if reverse else abi, (SUBLANES, ns))
    a2r, a2i = ar * ar - ai * ai, 2.0 * ar * ai
    a4r, a4i = a2r * a2r - a2i * a2i, 2.0 * a2r * a2i
    out = []
    for d, (vr, vi) in ((1, (ar, ai)), (2, (a2r, a2i)), (4, (a4r, a4i))):
        keep = (row < SUBLANES - d) if reverse else (row >= d)
        out += [jnp.where(keep, vr, 0.0), jnp.where(keep, vi, 0.0)]
    pr, pi = ar, ai
    for k in range(1, SUBLANES):
        sel = (row <= SUBLANES - 1 - k) if reverse else (row >= k)
        nr, ni = pr * ar - pi * ai, pr * ai + pi * ar
        pr, pi = jnp.where(sel, nr, pr), jnp.where(sel, ni, pi)
    return out + [pr, pi]


def _cpow(ar, ai, n):
    out, br, bi = None, ar, ai
    while n:
        if n & 1:
            out = (br, bi) if out is None else (out[0] * br - out[1] * bi, out[0] * bi + out[1] * br)
        br, bi = br * br - bi * bi, 2.0 * br * bi
        n >>= 1
    return out


def _seg_perm_matrix(nrows):
    r = jnp.arange(nrows)
    src = (nrows // SUBLANES) * (r % SUBLANES) + r // SUBLANES
    return (src[:, None] == jnp.arange(nrows)[None, :]).astype(BF16)


def _seg_order_rows(pm, xb):
    return jnp.dot(pm, xb, preferred_element_type=F32).astype(BF16)


def _time_order_rows(pm, x, terms):
    out, rest = None, x
    for t in range(terms):
        piece = rest.astype(BF16)
        part = lax.dot_general(pm, piece, (TN, ((), ())), preferred_element_type=F32)
        out = part if out is None else out + part
        if t + 1 < terms:
            rest = rest - piece.astype(F32)
    return out


def _seg_scan(xr_ref, xi_ref, abr_ref, abi_ref, coef_ref, car_ref, cai_ref, *, nrows, ns, reverse,
              cmat=None, dab=None):
    seg = nrows // SUBLANES
    cw = min(SCAN_LANES, ns)
    row = lax.broadcasted_iota(jnp.int32, (SUBLANES, cw), 0)
    first, last = (SUBLANES - 1, 0) if reverse else (0, SUBLANES - 1)

    def tile(i):
        return pl.ds(pl.multiple_of(((seg - 1 - i) if reverse else i) * SUBLANES, SUBLANES), SUBLANES)

    for cc in range(ns // cw):
        cols = pl.ds(cc * cw, cw)
        ar = jnp.broadcast_to(abr_ref[:, cols], (SUBLANES, cw))
        ai = jnp.broadcast_to(abi_ref[:, cols], (SUBLANES, cw))
        if reverse:
            ai = -ai

        def local(i, x, cols=cols, ar=ar, ai=ai):
            rows = tile(i)
            nr = ar * x[0] - ai * x[1] + xr_ref[rows, cols]
            ni = ar * x[1] + ai * x[0] + xi_ref[rows, cols]
            xr_ref[rows, cols] = nr
            xi_ref[rows, cols] = ni
            return nr, ni

        zero = jnp.zeros((SUBLANES, cw), F32)
        er, ei = lax.fori_loop(0, seg, local, (zero, zero))

        co = [coef_ref[k, :, cols] for k in range(8)]
        for lvl, d in enumerate((1, 2, 4)):
            kr, ki = co[2 * lvl], co[2 * lvl + 1]
            sh = SUBLANES - d if reverse else d
            sr, si = pltpu.roll(er, sh, 0), pltpu.roll(ei, sh, 0)
            er, ei = er + (kr * sr - ki * si), ei + (kr * si + ki * sr)
        c0r, c0i = car_ref[:, cols], cai_ref[:, cols]
        er, ei = er + (co[6] * c0r - co[7] * c0i), ei + (co[6] * c0i + co[7] * c0r)
        nb_shift = SUBLANES - 1 if reverse else 1
        cmr = jnp.where(row == first, c0r, pltpu.roll(er, nb_shift, 0))
        cmi = jnp.where(row == first, c0i, pltpu.roll(ei, nb_shift, 0))
        car_ref[:, cols] = jnp.broadcast_to(er[last:last + 1, :], er.shape)
        cai_ref[:, cols] = jnp.broadcast_to(ei[last:last + 1, :], ei.shape)
        if cmat is not None:
            cmat[0][:, cols] = cmr
            cmat[1][:, cols] = cmi

        w0 = (ar * cmr - ai * cmi, ar * cmi + ai * cmr)
        if dab is None:
            def fix(i, w, cols=cols, ar=ar, ai=ai):
                rows = tile(i)
                xr_ref[rows, cols] = xr_ref[rows, cols] + w[0]
                xi_ref[rows, cols] = xi_ref[rows, cols] + w[1]
                return ar * w[0] - ai * w[1], ar * w[1] + ai * w[0]

            lax.fori_loop(0, seg, fix, w0)
        else:
            s_re, s_im, e_re, e_im, o_re, o_im = dab

            def add(rows, w, pr, pi, acc):
                gr = xr_ref[rows, cols] + w[0]
                gi = xi_ref[rows, cols] + w[1]
                xr_ref[rows, cols] = gr
                xi_ref[rows, cols] = gi
                return acc[0] + (gr * pr + gi * pi), acc[1] + (gi * pr - gr * pi)

            def fix(i, st, cols=cols, ar=ar, ai=ai):
                w, acc = st[:2], st[2:]
                rows = tile(i)
                before = pl.ds(pl.multiple_of((seg - 2 - i) * SUBLANES, SUBLANES), SUBLANES)
                acc = add(rows, w, s_re[before, cols], s_im[before, cols], acc)
                return (ar * w[0] - ai * w[1], ar * w[1] + ai * w[0]) + acc

            st = lax.fori_loop(0, seg - 1, fix, w0 + (zero, zero))
            acc = add(pl.ds(0, SUBLANES), st[:2], e_re[:, cols], e_im[:, cols], st[2:])
            o_re[:, cols] += jnp.sum(acc[0], axis=0, keepdims=True)
            o_im[:, cols] += jnp.sum(acc[1], axis=0, keepdims=True)


def _hosted(core, comm, grid, n_in, n_out, n_scratch):
    ci = len(comm["ins"]) if comm else 0
    co = len(comm["out_shape"]) if comm else 0

    def body(*refs):
        ins, rest = refs[:n_in + ci], refs[n_in + ci:]
        outs, scr = rest[:n_out + co], rest[n_out + co:]
        hooks = functools.partial(_comm_hooks, comm, grid, ins[n_in:], outs[n_out:], scr[n_scratch:])
        hooks(before=True)
        core(*ins[:n_in], *outs[:n_out], *scr[:n_scratch])
        hooks(before=False)

    aliases = {n_in + i: n_out + i for i in range(co)} if comm and comm.get("alias") else {}
    extra = dict(ins=list(comm["ins"]) if comm else [], in_specs=[ANY] * ci, out_specs=[ANY] * co,
                 out_shape=list(comm["out_shape"]) if comm else [], scratch=list(comm["scratch"]) if comm else [],
                 aliases=aliases)
    return body, extra


def _ssm_fwd(proj, bdr, bdi, cdr, cdi, abr, abi, dsk, P, tb, comm=None):
    T = proj.shape[0]
    ntl, ct, st = bdr.shape
    ns = ntl * st
    nb = T // tb

    def core(u_ref, bdr_ref, bdi_ref, cdr_ref, cdi_ref, abr_ref, abi_ref, d_ref, pm_ref,
             y_ref, ge_ref, bsr_ref, bsi_ref, sr, si, coef, car, cai, up):
        @pl.when(pl.program_id(0) == 0)
        def _():
            seg_pow = _cpow(abr_ref[...], abi_ref[...], tb // SUBLANES)
            for k, tile in enumerate(_coef_tiles(seg_pow[0], seg_pow[1], False)):
                coef[k] = tile
            car[...] = jnp.zeros_like(car)
            cai[...] = jnp.zeros_like(cai)

        bsr_ref[...] = car[...]
        bsi_ref[...] = cai[...]
        u = u_ref[...]
        ub = _seg_order_rows(pm_ref[...], u.astype(BF16))
        for s in range(ntl):
            us = ub[:, s * ct:(s + 1) * ct]
            sr[:, s * st:(s + 1) * st] = jnp.dot(us, bdr_ref[s], preferred_element_type=F32)
            si[:, s * st:(s + 1) * st] = jnp.dot(us, bdi_ref[s], preferred_element_type=F32)
        _seg_scan(sr, si, abr_ref, abi_ref, coef, car, cai, nrows=tb, ns=ns, reverse=False)
        for s in range(ntl):
            s_re = sr[:, s * st:(s + 1) * st].astype(BF16)
            s_im = si[:, s * st:(s + 1) * st].astype(BF16)
            up[:, s * ct:(s + 1) * ct] = (jnp.dot(s_re, cdr_ref[s], preferred_element_type=F32)
                                          - jnp.dot(s_im, cdi_ref[s], preferred_element_type=F32))
        y = _time_order_rows(pm_ref[...], up[...], 3) + d_ref[...] * u
        y_ref[...] = y
        ge_ref[...] = _gelu(y).astype(BF16)

    full3 = lambda a: _bs(a.shape, lambda i: (0, 0, 0))
    vec = lambda n: _bs((1, n), lambda i: (0, 0))
    row = _bs((tb, P), lambda i: (i, 0))
    st_spec = _bs((None, SUBLANES, ns), lambda i: (i, 0, 0))
    body, extra = _hosted(core, comm, (nb,), 9, 4, 6)
    return _call(body, grid=(nb,),
                 in_specs=[_bs((tb, P), lambda i: (i, 2)), full3(bdr), full3(bdi), full3(cdr), full3(cdi),
                           vec(ns), vec(ns), vec(P), _bs((tb, tb), lambda i: (0, 0))] + extra["in_specs"],
                 out_specs=[row, row, st_spec, st_spec] + extra["out_specs"],
                 out_shape=[jax.ShapeDtypeStruct((T, P), F32), jax.ShapeDtypeStruct((T, P), BF16),
                            jax.ShapeDtypeStruct((nb, SUBLANES, ns), F32),
                            jax.ShapeDtypeStruct((nb, SUBLANES, ns), F32)] + extra["out_shape"],
                 scratch_shapes=[pltpu.VMEM((tb, ns), F32), pltpu.VMEM((tb, ns), F32),
                                 pltpu.VMEM((8, SUBLANES, ns), F32),
                                 pltpu.VMEM((SUBLANES, ns), F32), pltpu.VMEM((SUBLANES, ns), F32),
                                 pltpu.VMEM((tb, P), F32)] + extra["scratch"],
                 compiler_params=_cp(("arbitrary",)), name="ssm_fwd")(
                     proj, bdr, bdi, cdr, cdi, abr, abi, dsk, _seg_perm_matrix(tb), *extra["ins"])


def _ssm_bwd(proj, y, dge, bsr, bsi, bdr, bdi, cdr, cdi, abr, abi, dsk, dpi, dpg, dsg, P, tb, comm=None):
    T = proj.shape[0]
    ntl, ct, st = bdr.shape
    ns = ntl * st
    nb = T // tb

    def core(u_ref, y_ref, dge_ref, bsr_ref, bsi_ref, abr_ref, abi_ref, d_ref, pm_ref, dpi_ref, dpg_ref, dsg_ref,
             bdr_h, bdi_h, cdr_h, cdi_h,
             dproj_ref, dabr_ref, dabi_ref, dd_ref, dbdr_h, dbdi_h, dcdr_h, dcdi_h,
             wbdr, wbdi, wcdr, wcdi, abdr, abdi, acdr, acdi, spr, spi, gr, gi, coef_f, coef_r,
             car, cai, gcr, gci, ser, sei, dup):
        i = pl.program_id(0)

        @pl.when(i == 0)
        def _():
            for h, w in ((bdr_h, wbdr), (bdi_h, wbdi), (cdr_h, wcdr), (cdi_h, wcdi)):
                pltpu.sync_copy(h, w)
            for a in (abdr, abdi, acdr, acdi, gcr, gci):
                a[...] = jnp.zeros_like(a)
            for o in (dabr_ref, dabi_ref, dd_ref):
                o[...] = jnp.zeros_like(o)
            seg_pow = _cpow(abr_ref[...], abi_ref[...], tb // SUBLANES)
            for k, tile in enumerate(_coef_tiles(seg_pow[0], seg_pow[1], False)):
                coef_f[k] = tile
            for k, tile in enumerate(_coef_tiles(seg_pow[0], seg_pow[1], True)):
                coef_r[k] = tile

        car[...] = bsr_ref[...]
        cai[...] = bsi_ref[...]
        u = u_ref[...]
        dy = dge_ref[...] * _gelu_grad(y_ref[...])
        ub = _seg_order_rows(pm_ref[...], u.astype(BF16))
        dyb = _seg_order_rows(pm_ref[...], dy.astype(BF16))
        for s in range(ntl):
            us = ub[:, s * ct:(s + 1) * ct]
            spr[:, s * st:(s + 1) * st] = jnp.dot(us, wbdr[s], preferred_element_type=F32)
            spi[:, s * st:(s + 1) * st] = jnp.dot(us, wbdi[s], preferred_element_type=F32)
        _seg_scan(spr, spi, abr_ref, abi_ref, coef_f, car, cai, nrows=tb, ns=ns, reverse=False, cmat=(ser, sei))

        for s in range(ntl):
            dys = dyb[:, s * ct:(s + 1) * ct]
            gr[:, s * st:(s + 1) * st] = lax.dot_general(dys, wcdr[s], (NT, ((), ())), preferred_element_type=F32)
            gi[:, s * st:(s + 1) * st] = -lax.dot_general(dys, wcdi[s], (NT, ((), ())), preferred_element_type=F32)
        _seg_scan(gr, gi, abr_ref, abi_ref, coef_r, gcr, gci, nrows=tb, ns=ns, reverse=True,
                  dab=(spr, spi, ser, sei, dabr_ref, dabi_ref))

        for s in range(ntl):
            sl_c, sl_s = slice(s * ct, (s + 1) * ct), slice(s * st, (s + 1) * st)
            s_re = spr[:, sl_s].astype(BF16)
            s_im = spi[:, sl_s].astype(BF16)
            g_re, g_im = gr[:, sl_s].astype(BF16), gi[:, sl_s].astype(BF16)
            dys, us = dyb[:, sl_c], ub[:, sl_c]
            acdr[s] += lax.dot_general(s_re, dys, (TN, ((), ())), preferred_element_type=F32)
            acdi[s] -= lax.dot_general(s_im, dys, (TN, ((), ())), preferred_element_type=F32)
            abdr[s] += lax.dot_general(us, g_re, (TN, ((), ())), preferred_element_type=F32)
            abdi[s] += lax.dot_general(us, g_im, (TN, ((), ())), preferred_element_type=F32)
            dup[:, sl_c] = (lax.dot_general(g_re, wbdr[s], (NT, ((), ())), preferred_element_type=F32)
                            + lax.dot_general(g_im, wbdi[s], (NT, ((), ())), preferred_element_type=F32))
        dd_ref[...] += jnp.sum(dy * u, axis=0, keepdims=True)
        du = _time_order_rows(pm_ref[...], dup[...], 2) + d_ref[...] * dy
        dproj_ref[:, 0:P] = dpi_ref[...]
        dproj_ref[:, P:2 * P] = dpg_ref[...]
        dproj_ref[:, 2 * P:3 * P] = du.astype(BF16)
        dproj_ref[:, 3 * P:4 * P] = dsg_ref[...]

        @pl.when(i == nb - 1)
        def _():
            for a, h in ((abdr, dbdr_h), (abdi, dbdi_h), (acdr, dcdr_h), (acdi, dcdi_h)):
                pltpu.sync_copy(a, h)

    rev = lambda i: nb - 1 - i
    vec = lambda n: _bs((1, n), lambda i: (0, 0))
    row = _bs((tb, P), lambda i: (rev(i), 0))
    st_spec = _bs((None, SUBLANES, ns), lambda i: (rev(i), 0, 0))
    bshape = jax.ShapeDtypeStruct(bdr.shape, F32)
    cshape = jax.ShapeDtypeStruct(cdr.shape, F32)
    body, extra = _hosted(core, comm, (nb,), 16, 8, 21)
    return _call(body, grid=(nb,),
                 in_specs=[_bs((tb, P), lambda i: (rev(i), 2)), row, row, st_spec, st_spec,
                           vec(ns), vec(ns), vec(P), _bs((tb, tb), lambda i: (0, 0)), row, row, row,
                           ANY, ANY, ANY, ANY] + extra["in_specs"],
                 out_specs=[_bs((tb, 4 * P), lambda i: (rev(i), 0)), vec(ns), vec(ns), vec(P), ANY, ANY, ANY, ANY]
                 + extra["out_specs"],
                 out_shape=[jax.ShapeDtypeStruct((T, 4 * P), BF16), jax.ShapeDtypeStruct((1, ns), F32),
                            jax.ShapeDtypeStruct((1, ns), F32), jax.ShapeDtypeStruct((1, P), F32),
                            bshape, bshape, cshape, cshape] + extra["out_shape"],
                 scratch_shapes=[pltpu.VMEM(bdr.shape, BF16), pltpu.VMEM(bdr.shape, BF16),
                                 pltpu.VMEM(cdr.shape, BF16), pltpu.VMEM(cdr.shape, BF16),
                                 pltpu.VMEM(bdr.shape, F32), pltpu.VMEM(bdr.shape, F32),
                                 pltpu.VMEM(cdr.shape, F32), pltpu.VMEM(cdr.shape, F32),
                                 pltpu.VMEM((tb, ns), F32), pltpu.VMEM((tb, ns), F32),
                                 pltpu.VMEM((tb, ns), F32), pltpu.VMEM((tb, ns), F32),
                                 pltpu.VMEM((8, SUBLANES, ns), F32), pltpu.VMEM((8, SUBLANES, ns), F32)]
                 + [pltpu.VMEM((SUBLANES, ns), F32)] * 6 + [pltpu.VMEM((tb, P), F32)] + extra["scratch"],
                 compiler_params=_cp(("arbitrary",)), name="ssm_bwd")(
                     proj, y, dge, bsr, bsi, abr, abi, dsk, _seg_perm_matrix(tb), dpi, dpg, dsg,
                     bdr, bdi, cdr, cdi, *extra["ins"])


def _adamw(w, g, m, v, name, comm=None):
    R, C = w.shape
    tr = _t(R, 256)

    def core(w_ref, g_ref, m_ref, v_ref, d_ref, mo_ref, vo_ref):
        gv = g_ref[...]
        mn = ADAM_B1 * m_ref[...] + (1.0 - ADAM_B1) * gv
        vn = ADAM_B2 * v_ref[...] + (1.0 - ADAM_B2) * (gv * gv)
        m_hat = mn / (1.0 - ADAM_B1 ** ADAM_STEP)
        v_hat = vn / (1.0 - ADAM_B2 ** ADAM_STEP)
        d_ref[...] = -ADAM_LR * (m_hat / (jnp.sqrt(v_hat) + ADAM_EPS) + ADAM_WD * w_ref[...])
        mo_ref[...] = mn
        vo_ref[...] = vn

    blk = _bs((tr, C), lambda i: (i, 0))
    shp = jax.ShapeDtypeStruct((R, C), F32)
    body, extra = _hosted(core, comm, (R // tr,), 4, 3, 0)
    return _call(body, grid=(R // tr,), in_specs=[blk] * 4 + extra["in_specs"],
                 out_specs=[blk] * 3 + extra["out_specs"], out_shape=[shp] * 3 + extra["out_shape"],
                 scratch_shapes=extra["scratch"],
                 compiler_params=_cp(("arbitrary",) if comm else ("parallel",)), name=name)(w, g, m, v, *extra["ins"])


def _sum_cast(grad, got, place, name):
    J, H, C = got.shape
    tr = _t(H, 256)
    nb = H // tr

    def body(pl_ref, a_ref, b_ref, o_ref):
        o_ref[...] = (a_ref[...] + b_ref[...]).astype(BF16)

    blk = _bs((None, tr, C), lambda j, i, pc: (j, i, 0))
    mine = _bs((None, tr, C), lambda j, i, pc: (j, pc[1] * nb + i, 0))
    spec = pltpu.PrefetchScalarGridSpec(num_scalar_prefetch=1, grid=(J, nb), in_specs=[mine, blk], out_specs=blk)
    return _call(body, grid_spec=spec, out_shape=jax.ShapeDtypeStruct((J, H, C), BF16),
                 compiler_params=_cp(("parallel", "parallel")), name=name)(place, grad, got)


def _sum_chips(sent, arrived, place, name):
    J, H, C = arrived.shape
    tr = _t(H, 256)
    nb = H // tr

    def body(pl_ref, own_ref, a0_ref, a1_ref, a2_ref, o_ref):
        acc = own_ref[...].astype(F32)
        for r in (a0_ref, a1_ref, a2_ref):
            acc = acc + r[...].astype(F32)
        o_ref[...] = acc

    def other(k):
        return _bs((None, tr, C), lambda i, pc: (jnp.where(pc[0] <= k, k + 1, k), i, 0))

    spec = pltpu.PrefetchScalarGridSpec(
        num_scalar_prefetch=1, grid=(nb,),
        in_specs=[_bs((None, tr, C), lambda i, pc: (pc[0], i, 0)), other(0), other(1), other(2)],
        out_specs=_bs((tr, C), lambda i, pc: (pc[1] * nb + i, 0)))
    return _call(body, grid_spec=spec, out_shape=jax.ShapeDtypeStruct((2 * H, C), F32),
                 compiler_params=_cp(("parallel",)), name=name)(place, sent, arrived, arrived, arrived)


def _place():
    x, y, c = lax.axis_index("x"), lax.axis_index("y"), lax.axis_index("c")
    chips = [(1 - x, y), (x, 1 - y), (1 - x, 1 - y)]
    return x, y, c, chips


def _split(nrows, row_bytes, align, cap=None):
    k = max(1, min(cap or DMA_MAX_CHUNKS, (nrows * row_bytes) // DMA_CHUNK_BYTES))
    while k > 1 and nrows % (k * align):
        k -= 1
    return k


def _comm_call(plan, name):
    n_in, n_out = len(plan["ins"]), len(plan["out_shape"])

    def body(*refs):
        for phase in plan["phases"]:
            phase(refs[:n_in], refs[n_in:n_in + n_out], refs[n_in + n_out:])

    return _call(body, in_specs=[ANY] * n_in, out_specs=[ANY] * n_out, out_shape=plan["out_shape"],
                 input_output_aliases={i: i for i in range(n_out)} if plan.get("alias") else {},
                 scratch_shapes=plan["scratch"], name=name)(*plan["ins"])


def _comm_hooks(plan, grid, ins, outs, sems, *, before):
    if plan is None:
        return
    nsteps, step = 1, 0
    for d, g in enumerate(grid):
        nsteps, step = nsteps * g, step * g + pl.program_id(d)
    for p, (phase, frac) in enumerate(zip(plan["phases"], plan["at"])):
        if (p == 0) == before:
            pl.when(step == int(frac * (nsteps - 1)))(functools.partial(phase, ins, outs, sems))


def _ag_plan(shards, axes):
    n = len(shards)
    shapes = [a.shape for a in shards]

    def window(ref, i, chip, half=None):
        S, ax = shapes[i], axes[i]
        idx = []
        for d in range(len(S)):
            off, size = 0, S[d]
            if d == 0 and half is not None:
                off, size = half * (S[0] // 2), S[0] // 2
            if d == ax:
                off = off + chip * S[ax]
            idx.append(pl.ds(off, size))
        return ref.at[tuple(idx)]

    def copies(src, full, sems):
        ssem, rsem = sems
        x, y, c, chips = _place()
        me = 2 * x + y
        sib = (x, y, 1 - c)
        idx = [2 * cx + cy for cx, cy in chips]

        def rcopy(i, k, s_ref, d_ref, to):
            return pltpu.make_async_remote_copy(src_ref=s_ref, dst_ref=d_ref, send_sem=ssem.at[i, k],
                                                recv_sem=rsem.at[i, k], device_id=to, device_id_type=MESH)

        def ici(i, j, incoming):
            half_src = src[i].at[pl.ds(c * (shapes[i][0] // 2), shapes[i][0] // 2)]
            return rcopy(i, j, half_src, window(full[i], i, idx[j] if incoming else me, c), (*chips[j], c))

        def fwd(i, j, half):
            w = window(full[i], i, idx[j], half)
            return rcopy(i, 3 + j, w, w, sib)

        def own(i):
            return rcopy(i, 6, src[i], window(full[i], i, me), sib)

        return c, ici, fwd, own

    def send(src, full, sems):
        c, ici, fwd, own = copies(src, full, sems)
        for i in range(n):
            for j in range(3):
                ici(i, j, False).start()
        for i in range(n):
            own(i).start()

    def forward(i, src, full, sems):
        c, ici, fwd, own = copies(src, full, sems)
        for j in range(3):
            ici(i, j, True).wait_recv()
            fwd(i, j, c).start()

    def finish(src, full, sems):
        c, ici, fwd, own = copies(src, full, sems)
        for i in range(n):
            for j in range(3):
                fwd(i, j, 1 - c).wait_recv()
            own(i).wait()
        for i in range(n):
            for j in range(3):
                ici(i, j, False).wait_send()
                fwd(i, j, c).wait_send()

    out_shape = [jax.ShapeDtypeStruct(tuple(N_CHIP * d if k == ax else d for k, d in enumerate(S)), BF16)
                 for S, ax in zip(shapes, axes)]
    sizes = [a.size for a in shards]
    behind = [0.85 * sum(sizes[:i + 1]) / sum(sizes) + 0.05 for i in range(n)]
    return dict(ins=list(shards), out_shape=out_shape,
                phases=[send] + [functools.partial(forward, i) for i in range(n)] + [finish],
                at=[0.0] + behind + [1.0],
                scratch=[pltpu.SemaphoreType.DMA((n, 7)), pltpu.SemaphoreType.DMA((n, 7))])


def _proj_ag(hn, wsh, order, tm):
    T, D = hn.shape
    P = wsh.shape[1]
    H = D // 2
    nt = T // tm

    def body(order_ref, hn_ref, wsh_ref, proj_ref, win_ref, wbuf, lsem, ssem, rsem):
        n, i = pl.program_id(0), pl.program_id(1)
        x, y, c, chips = _place()
        me = 2 * x + y
        sib = (x, y, 1 - c)
        idx = [2 * cx + cy for cx, cy in chips]

        def rcopy(k, s_ref, d_ref, to):
            return pltpu.make_async_remote_copy(src_ref=s_ref, dst_ref=d_ref, send_sem=ssem.at[k],
                                                recv_sem=rsem.at[k], device_id=to, device_id_type=MESH)

        def cols(chip):
            return pl.ds(pl.multiple_of(chip * P, LANES), P)

        def rows(half):
            return pl.ds(pl.multiple_of(half * H, 16), H)

        def ici(j, incoming):
            return rcopy(j, wsh_ref.at[rows(c)], win_ref.at[rows(c), cols(idx[j] if incoming else me)],
                         (*chips[j], c))

        def fwd(j, half):
            w = win_ref.at[rows(half), cols(idx[j])]
            return rcopy(3 + j, w, w, sib)

        def own():
            return rcopy(6, wsh_ref, win_ref.at[:, cols(me)], sib)

        def load(src):
            cp = pltpu.make_async_copy(src, wbuf, lsem)
            cp.start()
            cp.wait()

        @pl.when((n == 0) & (i == 0))
        def _():
            ici(0, False).start()
            ici(1, False).start()
            own().start()
            load(wsh_ref)

        for j in range(3):
            @pl.when((n == j + 1) & (i == 0))
            def _(j=j):
                if j == 0:
                    ici(2, False).start()
                ici(j, True).wait_recv()
                fwd(j, c).start()
                fwd(j, 1 - c).wait_recv()
                load(win_ref.at[:, cols(idx[j])])

        proj_ref[...] = jnp.dot(hn_ref[...], wbuf[...], preferred_element_type=F32)

        @pl.when((n == 3) & (i == nt - 1))
        def _():
            own().wait()
            for j in range(3):
                ici(j, False).wait_send()
                fwd(j, c).wait_send()

    spec = pltpu.PrefetchScalarGridSpec(
        num_scalar_prefetch=1, grid=(N_CHIP, nt),
        in_specs=[_bs((tm, D), lambda n, i, o: (i, 0)), ANY],
        out_specs=[_bs((tm, P), lambda n, i, o: (i, o[n])), ANY],
        scratch_shapes=[pltpu.VMEM((D, P), BF16), pltpu.SemaphoreType.DMA,
                        pltpu.SemaphoreType.DMA((7,)), pltpu.SemaphoreType.DMA((7,))])
    return _call(body, grid_spec=spec,
                 out_shape=[jax.ShapeDtypeStruct((T, N_CHIP * P), F32), jax.ShapeDtypeStruct((D, N_CHIP * P), BF16)],
                 compiler_params=_cp(("arbitrary", "arbitrary")), name="proj_ag")(order, hn, wsh)


def _halves_plan(grads):
    n = len(grads)

    def send(g, got, sems):
        ssem, rsem = sems
        x, y, c, _ = _place()
        sib = (x, y, 1 - c)
        for i in range(n):
            J, R, C = g[i].shape
            H = R // 2
            k = _split(H, C * 4, SUBLANES, cap=DMA_MAX_CHUNKS // J)
            hr = H // k
            for j in range(J):
                for q in range(k):
                    other = pl.ds(pl.multiple_of((1 - c) * H + q * hr, SUBLANES), hr)
                    to = pl.ds(q * hr, hr)
                    pltpu.make_async_remote_copy(src_ref=g[i].at[j, other, :], dst_ref=got[i].at[j, to, :],
                                                 send_sem=ssem.at[i], recv_sem=rsem.at[i],
                                                 device_id=sib, device_id_type=MESH).start()

    def finish(g, got, sems):
        ssem, rsem = sems
        x, y, c, _ = _place()
        for i in range(n):
            pltpu.make_async_remote_copy(src_ref=got[i], dst_ref=got[i], send_sem=ssem.at[i], recv_sem=rsem.at[i],
                                         device_id=(x, y, 1 - c), device_id_type=MESH).wait()

    half = [jax.ShapeDtypeStruct((a.shape[0], a.shape[1] // 2, a.shape[2]), a.dtype) for a in grads]
    return dict(ins=list(grads), out_shape=half, phases=[send, finish], at=[0.0, 1.0],
                scratch=[pltpu.SemaphoreType.DMA((n,)), pltpu.SemaphoreType.DMA((n,))])


def _scatter_plan(parts):
    n = len(parts)

    def peers():
        x, y, c, chips = _place()
        return 2 * x + y, c, chips, [2 * cx + cy for cx, cy in chips]

    def send(s, got, sems):
        ssem, rsem = sems
        me, c, chips, idx = peers()
        for i in range(n):
            _, H, C = s[i].shape
            k = _split(H, C * 2, 16, cap=RS_CHUNKS)
            hr = H // k
            for q in range(k):
                rows = pl.ds(q * hr, hr)
                for j in range(3):
                    pltpu.make_async_remote_copy(src_ref=s[i].at[idx[j], rows, :], dst_ref=got[i].at[me, rows, :],
                                                 send_sem=ssem.at[i, j], recv_sem=rsem.at[i, j],
                                                 device_id=(*chips[j], c), device_id_type=MESH).start()

    def finish(s, got, sems):
        ssem, rsem = sems
        me, c, chips, idx = peers()
        for i in range(n):
            for j in range(3):
                pltpu.make_async_remote_copy(src_ref=s[i].at[idx[j]], dst_ref=got[i].at[idx[j]],
                                             send_sem=ssem.at[i, j], recv_sem=rsem.at[i, j],
                                             device_id=(*chips[j], c), device_id_type=MESH).wait()

    return dict(ins=list(parts), out_shape=[jax.ShapeDtypeStruct(a.shape, a.dtype) for a in parts],
                phases=[send, finish], at=[0.0, 1.0],
                scratch=[pltpu.SemaphoreType.DMA((n, 3)), pltpu.SemaphoreType.DMA((n, 3))])


def _join_plan(shards):
    n = len(shards)

    def send(_, full, sems):
        ssem, rsem = sems
        x, y, c, _ = _place()
        sib = (x, y, 1 - c)
        for i in range(n):
            H, C = full[i].shape[0] // 2, full[i].shape[1]
            k = _split(H, C * 4, SUBLANES)
            hr = H // k
            for q in range(k):
                rows = pl.ds(pl.multiple_of(c * H + q * hr, SUBLANES), hr)
                pltpu.make_async_remote_copy(src_ref=full[i].at[rows], dst_ref=full[i].at[rows],
                                             send_sem=ssem.at[i], recv_sem=rsem.at[i],
                                             device_id=sib, device_id_type=MESH).start()

    def finish(_, full, sems):
        ssem, rsem = sems
        x, y, c, _ = _place()
        for i in range(n):
            half = full[i].at[pl.ds(0, full[i].shape[0] // 2)]
            pltpu.make_async_remote_copy(src_ref=half, dst_ref=half, send_sem=ssem.at[i], recv_sem=rsem.at[i],
                                         device_id=(x, y, 1 - c), device_id_type=MESH).wait()

    return dict(ins=list(shards), out_shape=[jax.ShapeDtypeStruct(a.shape, a.dtype) for a in shards],
                phases=[send, finish], at=[0.0, 1.0], alias=True,
                scratch=[pltpu.SemaphoreType.DMA((n,)), pltpu.SemaphoreType.DMA((n,))])


def _allreduce_plan(buf):
    R, L = buf.shape
    RB = R // N_DEV

    def parts(sems):
        xv, got, ov, lsem, ssem, rsem = sems
        x, y, c, _ = _place()
        me = 4 * x + 2 * y + c

        def dev(k):
            return (k // 4, (k // 2) % 2, k % 2)

        def slab(k):
            return pl.ds(pl.multiple_of(k * RB, SUBLANES), RB)

        def first(d, to, landing):
            return pltpu.make_async_remote_copy(src_ref=xv.at[slab(to)], dst_ref=got.at[landing],
                                                send_sem=ssem.at[0, d], recv_sem=rsem.at[0, d],
                                                device_id=dev(to), device_id_type=MESH)

        def second(d, to, k):
            return pltpu.make_async_remote_copy(src_ref=ov.at[slab(k)], dst_ref=ov.at[slab(k)],
                                                send_sem=ssem.at[1, d], recv_sem=rsem.at[1, d],
                                                device_id=dev(to), device_id_type=MESH)

        return me, slab, first, second

    def scatter(ins, outs, sems):
        xv, lsem = sems[0], sems[3]
        me, slab, first, second = parts(sems)
        cp = pltpu.make_async_copy(ins[0], xv, lsem)
        cp.start()
        cp.wait()
        for d in range(1, N_DEV):
            first(d, (me + d) % N_DEV, me).start()

    def reduce(ins, outs, sems):
        xv, got, ov = sems[:3]
        me, slab, first, second = parts(sems)
        got[me] = xv[slab(me), :]
        for d in range(1, N_DEV):
            src = (me + N_DEV - d) % N_DEV
            first(d, src, src).wait_recv()
        acc = got[0]
        for k in range(1, N_DEV):
            acc = acc + got[k]
        ov[slab(me), :] = acc
        for d in range(1, N_DEV):
            second(d, (me + d) % N_DEV, me).start()

    def collect(ins, outs, sems):
        ov, lsem = sems[2], sems[3]
        me, slab, first, second = parts(sems)
        for d in range(1, N_DEV):
            src = (me + N_DEV - d) % N_DEV
            second(d, src, src).wait_recv()
        for d in range(1, N_DEV):
            peer = (me + d) % N_DEV
            first(d, peer, me).wait_send()
            second(d, peer, me).wait_send()
        cp = pltpu.make_async_copy(ov, outs[0], lsem)
        cp.start()
        cp.wait()

    return dict(ins=[buf], out_shape=[jax.ShapeDtypeStruct((R, L), F32)], phases=[scatter, reduce, collect],
                at=[0.0, 0.5, 1.0],
                scratch=[pltpu.VMEM((R, L), F32), pltpu.VMEM((N_DEV, RB, L), F32), pltpu.VMEM((R, L), F32),
                         pltpu.SemaphoreType.DMA, pltpu.SemaphoreType.DMA((2, N_DEV)),
                         pltpu.SemaphoreType.DMA((2, N_DEV))])


def _block_diag(t, gt):
    G, A, B = t.shape
    t4 = t.reshape(G // gt, gt, A, B)
    eye = jnp.eye(gt, dtype=t.dtype)
    return jnp.einsum('sgab,gh->sgahb', t4, eye).reshape(G // gt, gt * A, gt * B)


def _block_diag_extract(m, gt, A, B):
    S = m.shape[0]
    m5 = m.reshape(S, gt, A, gt, B)
    eye = jnp.eye(gt, dtype=m.dtype)
    return jnp.einsum('sgahb,gh->sgab', m5, eye).reshape(S * gt, A, B)


def _pack_small(arrs, rows):
    flat = jnp.concatenate([a.reshape(-1).astype(F32) for a in arrs])
    return jnp.pad(flat, (0, rows * LANES - flat.shape[0])).reshape(rows, LANES)


def _unpack_small(buf, shapes):
    flat = buf.reshape(-1)
    out, off = [], 0
    for s in shapes:
        n = 1
        for d in s:
            n *= d
        out.append(flat[off:off + n].reshape(s))
        off += n
    return out


def kernel(x, p, norm_gain, w_in, w_pool, pool_scale, a_re, a_im, log_dt, b_re, b_im, c_re, c_im, d_skip, w_glu, w_out, w_ple, w_ple_gate, final_gain, loss_target, m_norm_gain, m_w_in, m_w_pool, m_pool_scale, m_a_re, m_a_im, m_log_dt, m_b_re, m_b_im, m_c_re, m_c_im, m_d_skip, m_w_glu, m_w_out, m_w_ple, m_w_ple_gate, m_final_gain, v_norm_gain, v_w_in, v_w_pool, v_pool_scale, v_a_re, v_a_im, v_log_dt, v_b_re, v_b_im, v_c_re, v_c_im, v_d_skip, v_w_glu, v_w_out, v_w_ple, v_w_ple_gate, v_final_gain):
    xs, pe, tgt = x[0], p[0, 0], loss_target[0]
    T, D = xs.shape
    E = pe.shape[1]
    P = D // 2
    NG = len(POOL_WINDOWS)
    PG = P // NG
    G, N, C = P // SSM_GROUP, SSM_STATE, SSM_GROUP
    GT = min(SSM_TILE_GROUPS, G)
    Q = D // N_CHIP

    big = {"w_in": (w_in, m_w_in, v_w_in), "w_pool": (w_pool, m_w_pool, v_w_pool),
           "w_glu": (w_glu, m_w_glu, v_w_glu), "w_out": (w_out, m_w_out, v_w_out),
           "w_ple": (w_ple, m_w_ple, v_w_ple), "w_ple_gate": (w_ple_gate, m_w_ple_gate, v_w_ple_gate)}
    big_names = list(big)
    shard2d = {n: (big[n][0].size // big[n][0].shape[-1], big[n][0].shape[-1]) for n in big_names}
    shard_axis = {"w_in": 1, "w_pool": 1, "w_glu": 1, "w_out": 0, "w_ple": 1, "w_ple_gate": 0}
    shard16 = {n: big[n][0][0].astype(BF16) for n in big_names}
    place = jnp.stack([2 * lax.axis_index("x") + lax.axis_index("y"), lax.axis_index("c")]).astype(jnp.int32)
    mx, my = lax.axis_index("x"), lax.axis_index("y")
    block_order = jnp.stack([2 * mx + my, 2 * (1 - mx) + my, 2 * mx + (1 - my),
                             2 * (1 - mx) + (1 - my)]).astype(jnp.int32)
    later = [n for n in big_names if n != "w_in"]
    ag_later = _ag_plan([shard16[n] for n in later], [shard_axis[n] for n in later])

    rep = lambda a: jnp.repeat(a, C, axis=0)
    a_re_r, a_im_r = rep(a_re[0]), rep(a_im[0])
    ldt_r = rep(jnp.broadcast_to(log_dt[0][:, None], (G, N)))
    bt_re = b_re[0].transpose(0, 2, 1).reshape(G * C, N)
    bt_im = b_im[0].transpose(0, 2, 1).reshape(G * C, N)
    ab_re_r, ab_im_r, bbt_re, bbt_im = _ssm_prep(a_re_r, a_im_r, ldt_r, bt_re, bt_im)
    abr = ab_re_r[::C].reshape(1, G * N)
    abi = ab_im_r[::C].reshape(1, G * N)
    bdr = _block_diag(bbt_re.reshape(G, C, N), GT).astype(BF16)
    bdi = _block_diag(bbt_im.reshape(G, C, N), GT).astype(BF16)
    cdr = _block_diag(c_re[0].transpose(0, 2, 1), GT).astype(BF16)
    cdi = _block_diag(c_im[0].transpose(0, 2, 1), GT).astype(BF16)

    tb = _t(T, 256)
    tbs = _t(T, 256)
    tm = _t(T, 1024)
    tk = _t(T, 2048)
    DH = _t(D, 1024)
    row_k = lambda i, n, k: (i, k)
    row_n = lambda i, n, k: (i, n)
    f32 = lambda *shape: jax.ShapeDtypeStruct(shape, F32)
    hn = _norm1(xs, norm_gain, tb)
    proj, win = _proj_ag(hn, shard16["w_in"], block_order, tm)
    y, ge, bsr, bsi, wp, wglu, wout, wple, wpg = _ssm_fwd(proj, bdr, bdi, cdr, cdi, abr, abi, d_skip, P, tbs,
                                                          comm=ag_later)
    pooled, mixed = _pool_fwd(proj, wp, P, tb)
    hg = _mm(ge, wglu, dims=NN, grid=(T // tm, 2 * P // DH, 1),
             a_spec=_bs((tm, P), row_k), b_spec=_bs((P, DH), lambda i, n, k: (k, n)),
             o_spec=_bs((tm, DH), row_n), out_shape=f32(T, 2 * P), name="mm_glu")
    cat = _gate_fwd(mixed, proj, hg, pool_scale, tb)
    h1, h1b = _mm(cat, wout, dims=NN, grid=(T // tm, D // DH, 1), res=xs, bf16_copy=True,
                  a_spec=_bs((tm, D), row_k), b_spec=_bs((D, DH), lambda i, n, k: (k, n)),
                  r_spec=_bs((tm, DH), row_n), o_spec=_bs((tm, DH), row_n), out_shape=f32(T, D), name="mm_out")
    z = _mm(h1b, wpg, dims=NN, grid=(T // tm, D // DH, 1),
            a_spec=_bs((tm, D), row_k), b_spec=_bs((D, DH), lambda i, n, k: (k, n)),
            o_spec=_bs((tm, DH), row_n), out_shape=f32(T, D), name="mm_pgate")
    dh2, de, dz, dg2, lpart = _final_fb(h1, pe, wple, z, tgt, final_gain.reshape(1, D), tb)

    col_m = lambda m, n, k: (k, m)
    col_n = lambda m, n, k: (k, n)
    dh1, dh1b = _mm(dz, wpg, dims=NT, grid=(T // tm, D // DH, 1), res=dh2, bf16_copy=True,
                    a_spec=_bs((tm, D), row_k), b_spec=_bs((DH, D), lambda i, n, k: (n, k)),
                    r_spec=_bs((tm, DH), row_n), o_spec=_bs((tm, DH), row_n), out_shape=f32(T, D), name="mm_dh1")
    g_wpg = _mm(h1b, dz, dims=TN, grid=(D // DH, D // DH, T // tk),
                a_spec=_bs((tk, DH), col_m), b_spec=_bs((tk, DH), col_n),
                o_spec=_bs((DH, DH), lambda m, n, k: (m, n)), out_shape=f32(D, D), name="mm_gwpg")
    g_wple = _mm(pe, de, dims=TN, grid=(1, N_CHIP, T // tk),
                 a_spec=_bs((tk, E), col_m), b_spec=_bs((tk, Q), col_n),
                 o_spec=_bs((None, E, Q), lambda m, j, k: (j, 0, 0)), out_shape=f32(N_CHIP, E, Q), name="mm_gwple")
    dcat = _mm(dh1b, wout, dims=NT, grid=(T // tm, D // DH, 1),
               a_spec=_bs((tm, D), row_k), b_spec=_bs((DH, D), lambda i, n, k: (n, k)),
               o_spec=_bs((tm, DH), row_n), out_shape=f32(T, D), name="mm_dcat")
    g_wout = _mm(cat, dh1b, dims=TN, grid=(D // DH, D // DH, T // tk),
                 a_spec=_bs((tk, DH), col_m), b_spec=_bs((tk, DH), col_n),
                 o_spec=_bs((DH, DH), lambda m, n, k: (m, n)), out_shape=f32(D, D), name="mm_gwout")
    gbig = {"w_out": g_wout.reshape(N_CHIP, Q, D), "w_ple": g_wple, "w_ple_gate": g_wpg.reshape(N_CHIP, Q, D)}
    first = list(gbig)
    res = _gate_bwd(dcat, mixed, proj, hg, pool_scale, tb, comm=_halves_plan([gbig[n] for n in first]))
    dmixed, dpg, dsg, dhg, dps = res[:5]
    got = dict(zip(first, res[5:]))
    dge = _mm(dhg, wglu, dims=NT, grid=(T // tm, 1, 1),
              a_spec=_bs((tm, 2 * P), row_k), b_spec=_bs((P, 2 * P), lambda i, n, k: (n, k)),
              o_spec=_bs((tm, P), row_n), out_shape=f32(T, P), name="mm_dge")
    g_wglu = _mm(ge, dhg, dims=TN, grid=(1, N_CHIP, T // tk),
                 a_spec=_bs((tk, P), col_m), b_spec=_bs((tk, Q), col_n),
                 o_spec=_bs((None, P, Q), lambda m, j, k: (j, 0, 0)), out_shape=f32(N_CHIP, P, Q), name="mm_gwglu")
    g_wp = _mm(pooled, dmixed, dims=TN, grid=(NG, 1, T // tk),
               a_spec=_bs((tk, PG), col_m), b_spec=_bs((tk, PG), col_m),
               o_spec=_bs((None, PG, PG), lambda g, n, k: (g, 0, 0)), out_shape=f32(NG, PG, PG), name="mm_gwp")
    gbig["w_pool"] = g_wp.reshape(NG, N_CHIP, PG // N_CHIP, PG).transpose(1, 0, 2, 3).reshape(
        N_CHIP, NG * PG // N_CHIP, PG)
    gbig["w_glu"] = g_wglu
    res = _pool_bwd(dmixed, wp, tb, comm=_halves_plan([gbig["w_pool"], gbig["w_glu"]]))
    dpi, got["w_pool"], got["w_glu"] = res
    early = list(gbig)
    chip_sums = {n: _sum_cast(gbig[n], got[n], place, "sum_cast_" + n) for n in early}
    res = _ssm_bwd(proj, y, dge, bsr, bsi, bdr, bdi, cdr, cdi, abr, abi, d_skip, dpi, dpg, dsg, P, tbs,
                   comm=_scatter_plan([chip_sums[n] for n in early]))
    dproj, dabr, dabi, dd, dbdr, dbdi, dcdr, dcdi = res[:8]
    arrived = dict(zip(early, res[8:]))
    gbig["w_in"] = _mm(hn, dproj, dims=TN, grid=(D // DH, N_CHIP, T // tk),
                       a_spec=_bs((tk, DH), col_m), b_spec=_bs((tk, P), col_n),
                       o_spec=_bs((None, DH, P), lambda m, j, k: (j, m, 0)), out_shape=f32(N_CHIP, D, P),
                       name="mm_gwin")
    got["w_in"], = _comm_call(_halves_plan([gbig["w_in"]]), "rs_halves_late")
    chip_sums["w_in"] = _sum_cast(gbig["w_in"], got["w_in"], place, "sum_cast_w_in")
    KH = _t(4 * P, 2048)
    dhn, arrived["w_in"] = _mm(dproj, win, dims=NT, grid=(T // tm, D // DH, 4 * P // KH),
                               a_spec=_bs((tm, KH), row_k), b_spec=_bs((DH, KH), lambda i, n, k: (n, k)),
                               o_spec=_bs((tm, DH), row_n), out_shape=f32(T, D), name="mm_dhn",
                               comm=_scatter_plan([chip_sums["w_in"]]))
    halves = [_sum_chips(chip_sums[n], arrived[n], place, "sum_chips_" + n) for n in big_names]
    res = _norm1_bwd(xs, dhn, dh1, norm_gain, tb, comm=_join_plan(halves))
    grad_x, dg1, gshard = res[0], res[1], res[2:]

    dbbt_re = _block_diag_extract(dbdr, GT, C, N).reshape(G * C, N)
    dbbt_im = _block_diag_extract(dbdi, GT, C, N).reshape(G * C, N)
    g_c_re = _block_diag_extract(dcdr, GT, N, C).transpose(0, 2, 1)
    g_c_im = _block_diag_extract(dcdi, GT, N, C).transpose(0, 2, 1)
    dab_re_r = rep(dabr.reshape(G, N)) * (1.0 / C)
    dab_im_r = rep(dabi.reshape(G, N)) * (1.0 / C)
    g_a_re, g_a_im, g_ldt, g_bt_re, g_bt_im = _ssm_prep_bwd(a_re_r, a_im_r, ldt_r, bt_re, bt_im,
                                                            dab_re_r, dab_im_r, dbbt_re, dbbt_im, G)
    g_b_re = g_bt_re.reshape(G, C, N).transpose(0, 2, 1)
    g_b_im = g_bt_im.reshape(G, C, N).transpose(0, 2, 1)


    small_names = ["norm_gain", "pool_scale", "a_re", "a_im", "log_dt", "b_re", "b_im", "c_re", "c_im",
                   "d_skip", "final_gain"]
    small_w = dict(norm_gain=norm_gain, pool_scale=pool_scale, a_re=a_re, a_im=a_im, log_dt=log_dt, b_re=b_re,
                   b_im=b_im, c_re=c_re, c_im=c_im, d_skip=d_skip, final_gain=final_gain)
    small_m = dict(norm_gain=m_norm_gain, pool_scale=m_pool_scale, a_re=m_a_re, a_im=m_a_im, log_dt=m_log_dt,
                   b_re=m_b_re, b_im=m_b_im, c_re=m_c_re, c_im=m_c_im, d_skip=m_d_skip, final_gain=m_final_gain)
    small_v = dict(norm_gain=v_norm_gain, pool_scale=v_pool_scale, a_re=v_a_re, a_im=v_a_im, log_dt=v_log_dt,
                   b_re=v_b_re, b_im=v_b_im, c_re=v_c_re, c_im=v_c_im, d_skip=v_d_skip, final_gain=v_final_gain)
    small_g = dict(norm_gain=dg1, pool_scale=dps, a_re=g_a_re, a_im=g_a_im, log_dt=g_ldt, b_re=g_b_re,
                   b_im=g_b_im, c_re=g_c_re, c_im=g_c_im, d_skip=dd, final_gain=dg2)
    shapes = [small_w[n].shape for n in small_names]
    total = sum(small_w[n].size for n in small_names) + 1
    unit = N_DEV * SUBLANES
    rows = -(-(-(-total // LANES)) // unit) * unit
    gbuf = _pack_small([small_g[n] for n in small_names] + [lpart[0, :1]], rows)
    g_out, d_out, m_out, v_out = {}, {}, {}, {}
    gsum = None
    for n, gs in zip(big_names, gshard):
        w_, m_, v_ = big[n]
        r2 = shard2d[n]
        res = _adamw(w_.reshape(r2), gs, m_.reshape(r2), v_.reshape(r2), "adamw_" + n,
                     comm=_allreduce_plan(gbuf) if n == "w_in" else None)
        if n == "w_in":
            gsum = res[3]
        g_out[n], d_out[n], m_out[n], v_out[n] = (a.reshape(w_.shape) for a in (gs, *res[:3]))
    wbuf = _pack_small([small_w[n] for n in small_names], rows)
    mbuf = _pack_small([small_m[n] for n in small_names], rows)
    vbuf = _pack_small([small_v[n] for n in small_names], rows)
    dsm, msm, vsm = _adamw(wbuf, gsum, mbuf, vbuf, "adamw_small")
    g_small = dict(zip(small_names, _unpack_small(gsum, shapes)))
    d_small = dict(zip(small_names, _unpack_small(dsm, shapes)))
    m_small = dict(zip(small_names, _unpack_small(msm, shapes)))
    v_small = dict(zip(small_names, _unpack_small(vsm, shapes)))
    loss = gsum.reshape(-1)[total - 1]

    g_out.update(g_small)
    d_out.update(d_small)
    m_out.update(m_small)
    v_out.update(v_small)

    order = ["norm_gain", "w_in", "w_pool", "pool_scale", "a_re", "a_im", "log_dt", "b_re", "b_im", "c_re",
             "c_im", "d_skip", "w_glu", "w_out", "w_ple", "w_ple_gate", "final_gain"]
    return (loss, grad_x[None], *[g_out[n] for n in order], *[d_out[n] for n in order],
            *[m_out[n] for n in order], *[v_out[n] for n in order])
```

```python
import functools

import jax
import jax.numpy as jnp
from jax import lax
from jax.experimental import pallas as pl
from jax.experimental.pallas import tpu as pltpu

F32, BF16 = jnp.float32, jnp.bfloat16
MESH = pl.DeviceIdType.MESH
ANY = pl.BlockSpec(memory_space=pl.ANY)
VMEM_FULL = pl.BlockSpec(memory_space=pltpu.VMEM)

EPS = 1e-6
A_RE_MAX = -1e-4
SSM_GROUP = 16
SSM_STATE = 64
POOL_WINDOWS = (2, 4, 8, 16)
POOL_HALO = 16
ADAM_LR, ADAM_B1, ADAM_B2, ADAM_EPS, ADAM_WD, ADAM_STEP = 0.001, 0.9, 0.999, 1e-08, 0.01, 10

V7X_VMEM_BYTES = 64 * 1024 * 1024
VMEM_LIMIT = V7X_VMEM_BYTES - 8 * 1024 * 1024
SUBLANES, LANES = 8, 128
SSM_TILE_GROUPS = 8
SCAN_LANES = 512
N_DEV, N_CHIP = 8, 4
DMA_CHUNK_BYTES = 256 * 1024
DMA_MAX_CHUNKS = 32
AG_CHUNKS = 8
RS_CHUNKS = 8


def _t(n, pref):
    return pref if n % pref == 0 else n


def _cp(sem=None, vmem=VMEM_LIMIT):
    return pltpu.CompilerParams(dimension_semantics=sem, vmem_limit_bytes=vmem)


def _call(body, **kw):
    return pl.pallas_call(body, **kw)


NN = ((1,), (0,))
NT = ((1,), (1,))
TN = ((0,), (0,))


def _mm(a, b, *, dims, grid, a_spec, b_spec, o_spec, out_shape, name, res=None, r_spec=None, bf16_copy=False,
        comm=None):
    nk, kax = grid[-1], len(grid) - 1
    acc_shape = tuple(d for d in o_spec.block_shape if d is not None)

    def core(*refs):
        refs = list(refs)
        a_ref, b_ref = refs[:2]
        r_ref = refs[2] if res is not None else None
        outs = refs[3 if res is not None else 2:]
        o_ref = outs[0]
        o2_ref = outs[1] if bf16_copy else None
        acc = outs[-1] if nk > 1 else None

        def finish(r):
            if r_ref is not None:
                r = r + r_ref[...]
            o_ref[...] = r.astype(o_ref.dtype)
            if o2_ref is not None:
                o2_ref[...] = r.astype(BF16)

        part = lax.dot_general(a_ref[...].astype(BF16), b_ref[...].astype(BF16),
                               (dims, ((), ())), preferred_element_type=F32)
        if nk == 1:
            finish(part)
        else:
            k = pl.program_id(kax)

            @pl.when(k == 0)
            def _():
                acc[...] = part

            @pl.when(k > 0)
            def _():
                acc[...] += part

            @pl.when(k == nk - 1)
            def _():
                finish(acc[...])

    ins, specs = [a, b], [a_spec, b_spec]
    if res is not None:
        ins.append(res)
        specs.append(r_spec)
    o_specs, o_shapes = [o_spec], [out_shape]
    if bf16_copy:
        o_specs = [o_spec, o_spec]
        o_shapes = [out_shape, jax.ShapeDtypeStruct(out_shape.shape, BF16)]
    scratch = [pltpu.VMEM(acc_shape, F32)] if nk > 1 else []
    body, extra = _hosted(core, comm, grid, len(ins), len(o_specs), len(scratch))
    sem = ("arbitrary",) * len(grid) if comm else ("parallel",) * kax + ("arbitrary",)
    outs = _call(body, grid=grid, in_specs=specs + extra["in_specs"], out_specs=o_specs + extra["out_specs"],
                 out_shape=o_shapes + extra["out_shape"], scratch_shapes=scratch + extra["scratch"],
                 input_output_aliases=extra["aliases"],
                 compiler_params=_cp(sem), name=name)(*ins, *extra["ins"])
    return outs[0] if len(outs) == 1 else outs


def _bs(shape, fn):
    return pl.BlockSpec(shape, fn)


def _sigmoid(v):
    return 1.0 / (1.0 + jnp.exp(-v))


def _gelu(v):
    return 0.5 * v * (1.0 + jnp.tanh(0.7978845608028654 * (v + 0.044715 * v * v * v)))


def _gelu_grad(v):
    t = jnp.tanh(0.7978845608028654 * (v + 0.044715 * v * v * v))
    return 0.5 * (1.0 + t) + 0.5 * v * (1.0 - t * t) * 0.7978845608028654 * (1.0 + 3 * 0.044715 * v * v)


def _norm1_bwd(x, dhn, dh1, g1, tb, comm=None):
    T, D = x.shape

    def core(x_ref, dhn_ref, dh1_ref, g_ref, dx_ref, dg_ref):
        @pl.when(pl.program_id(0) == 0)
        def _():
            dg_ref[...] = jnp.zeros_like(dg_ref)

        xv = x_ref[...]
        r = lax.rsqrt(jnp.mean(xv * xv, axis=-1, keepdims=True) + EPS)
        xh = xv * r
        dhn_v = dhn_ref[...]
        dg_ref[...] += jnp.sum(dhn_v * xh, axis=0, keepdims=True)
        dxh = dhn_v * g_ref[...]
        dx_ref[...] = dh1_ref[...] + r * (dxh - xh * jnp.mean(dxh * xh, axis=-1, keepdims=True))

    row = _bs((tb, D), lambda i: (i, 0))
    vec = _bs((1, D), lambda i: (0, 0))
    body, extra = _hosted(core, comm, (T // tb,), 4, 2, 0)
    return _call(body, grid=(T // tb,), in_specs=[row, row, row, vec] + extra["in_specs"],
                 out_specs=[row, vec] + extra["out_specs"],
                 out_shape=[jax.ShapeDtypeStruct((T, D), F32), jax.ShapeDtypeStruct((1, D), F32)] + extra["out_shape"],
                 scratch_shapes=extra["scratch"], input_output_aliases=extra["aliases"],
                 compiler_params=_cp(("arbitrary",)), name="norm1_bwd")(x, dhn, dh1, g1, *extra["ins"])


def _gate_fwd(mixed, proj, hg, ps, tb):
    T, P = mixed.shape

    def body(mx_ref, pg_ref, sg_ref, hg_ref, ps_ref, o_ref):
        pg, sg = pg_ref[...], sg_ref[...]
        ya = (mx_ref[...] * ps_ref[...]) * (pg * _sigmoid(pg))
        hgv = hg_ref[...]
        o = hgv[:, :P] * _sigmoid(hgv[:, P:])
        yb = o * (sg * _sigmoid(sg))
        o_ref[:, :P] = ya.astype(BF16)
        o_ref[:, P:] = yb.astype(BF16)

    return _call(body, grid=(T // tb,),
                 in_specs=[_bs((tb, P), lambda i: (i, 0)), _bs((tb, P), lambda i: (i, 1)),
                           _bs((tb, P), lambda i: (i, 3)), _bs((tb, 2 * P), lambda i: (i, 0)),
                           _bs((1, P), lambda i: (0, 0))],
                 out_specs=_bs((tb, 2 * P), lambda i: (i, 0)),
                 out_shape=jax.ShapeDtypeStruct((T, 2 * P), BF16),
                 compiler_params=_cp(("parallel",)), name="gate_fwd")(mixed, proj, proj, hg, ps)


def _gate_bwd(dcat, mixed, proj, hg, ps, tb, comm=None):
    T, P = mixed.shape

    def core(dc_ref, mx_ref, pg_ref, sg_ref, hg_ref, ps_ref, dmx_ref, dpg_ref, dsg_ref, dhg_ref, dps_ref):
        @pl.when(pl.program_id(0) == 0)
        def _():
            dps_ref[...] = jnp.zeros_like(dps_ref)

        dc = dc_ref[...]
        dya, dyb = dc[:, :P], dc[:, P:]
        pg, sg, mx, psv = pg_ref[...], sg_ref[...], mx_ref[...], ps_ref[...]
        s_pg = _sigmoid(pg)
        dpa = dya * (pg * s_pg)
        dpg_ref[...] = (dya * (mx * psv) * (s_pg * (1.0 + pg * (1.0 - s_pg)))).astype(BF16)
        dps_ref[...] += jnp.sum(dpa * mx, axis=0, keepdims=True)
        dmx_ref[...] = (dpa * psv).astype(BF16)
        hgv = hg_ref[...]
        h1, s_h2 = hgv[:, :P], _sigmoid(hgv[:, P:])
        s_sg = _sigmoid(sg)
        do = dyb * (sg * s_sg)
        dsg_ref[...] = (dyb * (h1 * s_h2) * (s_sg * (1.0 + sg * (1.0 - s_sg)))).astype(BF16)
        dhg_ref[:, :P] = (do * s_h2).astype(BF16)
        dhg_ref[:, P:] = (do * h1 * s_h2 * (1.0 - s_h2)).astype(BF16)

    rowp = _bs((tb, P), lambda i: (i, 0))
    row2 = _bs((tb, 2 * P), lambda i: (i, 0))
    vec = _bs((1, P), lambda i: (0, 0))
    body, extra = _hosted(core, comm, (T // tb,), 6, 5, 0)
    return _call(body, grid=(T // tb,),
                 in_specs=[row2, rowp, _bs((tb, P), lambda i: (i, 1)), _bs((tb, P), lambda i: (i, 3)), row2, vec]
                 + extra["in_specs"],
                 out_specs=[rowp, rowp, rowp, row2, vec] + extra["out_specs"],
                 out_shape=[jax.ShapeDtypeStruct((T, P), BF16), jax.ShapeDtypeStruct((T, P), BF16),
                            jax.ShapeDtypeStruct((T, P), BF16), jax.ShapeDtypeStruct((T, 2 * P), BF16),
                            jax.ShapeDtypeStruct((1, P), F32)] + extra["out_shape"],
                 scratch_shapes=extra["scratch"],
                 compiler_params=_cp(("arbitrary",)), name="gate_bwd")(dcat, mixed, proj, proj, hg, ps, *extra["ins"])


def _final_fb(h1, pe, wple, z, tgt, g2, tb):
    T, D = h1.shape
    E = pe.shape[1]

    def body(h1_ref, p_ref, w_ref, z_ref, t_ref, g_ref, dh2_ref, de_ref, dz_ref, dg_ref, l_ref):
        @pl.when(pl.program_id(0) == 0)
        def _():
            dg_ref[...] = jnp.zeros_like(dg_ref)
            l_ref[...] = jnp.zeros_like(l_ref)

        ev = jnp.dot(p_ref[...].astype(BF16), w_ref[...], preferred_element_type=F32)
        s = _sigmoid(z_ref[...])
        h2 = h1_ref[...] + ev * s
        r = lax.rsqrt(jnp.mean(h2 * h2, axis=-1, keepdims=True) + EPS)
        xh = h2 * r
        gv = g_ref[...]
        diff = xh * gv - t_ref[...]
        l_ref[...] += 0.5 * jnp.sum(jnp.mean(diff * diff, axis=-1, keepdims=True))
        dout = diff * (1.0 / D)
        dg_ref[...] += jnp.sum(dout * xh, axis=0, keepdims=True)
        dxh = dout * gv
        dh2 = r * (dxh - xh * jnp.mean(dxh * xh, axis=-1, keepdims=True))
        dh2_ref[...] = dh2
        de_ref[...] = (dh2 * s).astype(BF16)
        dz_ref[...] = (dh2 * ev * s * (1.0 - s)).astype(BF16)

    row = _bs((tb, D), lambda i: (i, 0))
    vec = _bs((1, D), lambda i: (0, 0))
    return _call(body, grid=(T // tb,),
                 in_specs=[row, _bs((tb, E), lambda i: (i, 0)), _bs((E, D), lambda i: (0, 0)), row, row, vec],
                 out_specs=[row, row, row, vec, _bs((1, LANES), lambda i: (0, 0))],
                 out_shape=[jax.ShapeDtypeStruct((T, D), F32), jax.ShapeDtypeStruct((T, D), BF16),
                            jax.ShapeDtypeStruct((T, D), BF16), jax.ShapeDtypeStruct((1, D), F32),
                            jax.ShapeDtypeStruct((1, LANES), F32)],
                 compiler_params=_cp(("arbitrary",)), name="final_fb")(h1, pe, wple, z, tgt, g2)


def _pool_inv_count(t0, rows, pg, ngroups):
    t = t0 + lax.broadcasted_iota(jnp.int32, (rows, pg), 0)
    parts = []
    for w in POOL_WINDOWS[:ngroups]:
        parts.append(jnp.where(t + 1 >= w, 1.0 / w, 1.0 / (t + 1).astype(F32)))
    return parts


def _pool_fwd(proj, wp, P, tb):
    T = proj.shape[0]
    ng = len(POOL_WINDOWS)
    pg = P // ng
    hb = tb // POOL_HALO

    def body(v_ref, tail_ref, w_ref, o_ref, mx_ref, ext):
        i = pl.program_id(0)
        ext[pl.ds(0, POOL_HALO), :] = jnp.where(i > 0, tail_ref[...], 0.0)
        ext[pl.ds(POOL_HALO, tb), :] = v_ref[...]
        inv = _pool_inv_count(i * tb, tb, pg, ng)
        for g, w in enumerate(POOL_WINDOWS):
            cols = pl.ds(g * pg, pg)
            win = ext[pl.ds(POOL_HALO, tb), cols]
            for k in range(1, w):
                win = win + ext[pl.ds(POOL_HALO - k, tb), cols]
            pooled = (win * inv[g] - ext[pl.ds(POOL_HALO, tb), cols]).astype(BF16)
            o_ref[:, cols] = pooled
            mx_ref[:, cols] = jnp.dot(pooled, w_ref[g], preferred_element_type=F32)

    row = _bs((tb, P), lambda i: (i, 0))
    return _call(body, grid=(T // tb,),
                 in_specs=[row, _bs((POOL_HALO, P), lambda i: (jnp.maximum(i * hb - 1, 0), 0)),
                           _bs(wp.shape, lambda i: (0, 0, 0))],
                 out_specs=[row, row],
                 out_shape=[jax.ShapeDtypeStruct((T, P), BF16), jax.ShapeDtypeStruct((T, P), F32)],
                 scratch_shapes=[pltpu.VMEM((tb + POOL_HALO, P), F32)],
                 compiler_params=_cp(("arbitrary",)), name="pool_fwd")(proj, proj, wp)


def _pool_bwd(dmixed, wp, tb, comm=None):
    T, P = dmixed.shape
    ng = len(POOL_WINDOWS)
    pg = P // ng
    hb = tb // POOL_HALO
    nb = T // tb

    def core(d_ref, head_ref, w_ref, o_ref, ext, dpl):
        i = pl.program_id(0)
        inv = _pool_inv_count(i * tb, tb, pg, ng)
        invh = _pool_inv_count((i + 1) * tb, POOL_HALO, pg, ng)
        for g in range(ng):
            cols = pl.ds(g * pg, pg)
            dp = lax.dot_general(d_ref[:, cols], w_ref[g], (NT, ((), ())), preferred_element_type=F32)
            dph = lax.dot_general(head_ref[:, cols], w_ref[g], (NT, ((), ())), preferred_element_type=F32)
            dpl[:, cols] = dp
            ext[pl.ds(0, tb), cols] = dp * inv[g]
            ext[pl.ds(tb, POOL_HALO), cols] = jnp.where(i < nb - 1, dph * invh[g], 0.0)
        for g, w in enumerate(POOL_WINDOWS):
            cols = pl.ds(g * pg, pg)
            acc = ext[pl.ds(0, tb), cols]
            for k in range(1, w):
                acc = acc + ext[pl.ds(k, tb), cols]
            o_ref[:, cols] = (acc - dpl[:, cols]).astype(BF16)

    body, extra = _hosted(core, comm, (nb,), 3, 1, 2)
    return _call(body, grid=(nb,),
                 in_specs=[_bs((tb, P), lambda i: (i, 0)),
                           _bs((POOL_HALO, P), lambda i: (jnp.minimum((i + 1) * hb, T // POOL_HALO - 1), 0)),
                           _bs(wp.shape, lambda i: (0, 0, 0))] + extra["in_specs"],
                 out_specs=[_bs((tb, P), lambda i: (i, 0))] + extra["out_specs"],
                 out_shape=[jax.ShapeDtypeStruct((T, P), BF16)] + extra["out_shape"],
                 scratch_shapes=[pltpu.VMEM((tb + POOL_HALO, P), F32), pltpu.VMEM((tb, P), F32)] + extra["scratch"],
                 compiler_params=_cp(("arbitrary",)), name="pool_bwd")(dmixed, dmixed, wp, *extra["ins"])


def _zoh(a_re, a_im, ldt, b_re, b_im):
    lam_re = jnp.minimum(a_re, A_RE_MAX)
    lam_im = a_im
    dt = jnp.exp(ldt)
    mag = jnp.exp(lam_re * dt)
    ang = lam_im * dt
    ab_re = mag * jnp.cos(ang)
    ab_im = mag * jnp.sin(ang)
    den = lam_re * lam_re + lam_im * lam_im
    n_re = ab_re - 1.0
    n_im = ab_im
    q_re = (n_re * lam_re + n_im * lam_im) / den
    q_im = (n_im * lam_re - n_re * lam_im) / den
    return ab_re, ab_im, q_re * b_re - q_im * b_im, q_re * b_im + q_im * b_re


def _ssm_prep(a_re, a_im, ldt, bt_re, bt_im):
    shp = jax.ShapeDtypeStruct(a_re.shape, F32)

    def body(a, b, c, d, e, o0, o1, o2, o3):
        r = _zoh(a[...], b[...], c[...], d[...], e[...])
        o0[...], o1[...], o2[...], o3[...] = r

    return _call(body, in_specs=[VMEM_FULL] * 5, out_specs=[VMEM_FULL] * 4, out_shape=[shp] * 4,
                 name="ssm_prep")(a_re, a_im, ldt, bt_re, bt_im)


def _ssm_prep_bwd(a_re, a_im, ldt, bt_re, bt_im, dab_re, dab_im, dbb_re, dbb_im, G):
    GC, N = a_re.shape
    C = GC // G

    def body(a, b, c, d, e, g0, g1, g2, g3, da_re, da_im, dldt, db_re, db_im):
        _, vjp = jax.vjp(_zoh, a[...], b[...], c[...], d[...], e[...])
        ga_re, ga_im, gl, gb_re, gb_im = vjp((g0[...], g1[...], g2[...], g3[...]))
        da_re[...] = jnp.sum(ga_re.reshape(G, C, N), axis=1)
        da_im[...] = jnp.sum(ga_im.reshape(G, C, N), axis=1)
        dldt[...] = jnp.sum(jnp.sum(gl.reshape(G, C, N), axis=1), axis=1, keepdims=True)
        db_re[...] = gb_re
        db_im[...] = gb_im

    gn = jax.ShapeDtypeStruct((G, N), F32)
    full = jax.ShapeDtypeStruct((GC, N), F32)
    return _call(body, in_specs=[VMEM_FULL] * 9, out_specs=[VMEM_FULL] * 5,
                 out_shape=[gn, gn, jax.ShapeDtypeStruct((G, 1), F32), full, full],
                 name="ssm_prep_bwd")(a_re, a_im, ldt, bt_re, bt_im, dab_re, dab_im, dbb_re, dbb_im)


def _coef_tiles(abr, abi, reverse):
    ns = abr.shape[1]
    row = lax.broadcasted_iota(jnp.int32, (SUBLANES, ns), 0)
    ar = jnp.broadcast_to(abr, (SUBLANES, ns))
    ai = jnp.broadcast_to(-abi if reverse else abi, (SUBLANES, ns))
    a2r, a2i = ar * ar - ai * ai, 2.0 * ar * ai
    a4r, a4i = a2r * a2r - a2i * a2i, 2.0 * a2r * a2i
    out = []
    for d, (vr, vi) in ((1, (ar, ai)), (2, (a2r, a2i)), (4, (a4r, a4i))):
        keep = (row < SUBLANES - d) if reverse else (row >= d)
        out += [jnp.where(keep, vr, 0.0), jnp.where(keep, vi, 0.0)]
    pr, pi = ar, ai
    for k in range(1, SUBLANES):
        sel = (row <= SUBLANES - 1 - k) if reverse else (row >= k)
        nr, ni = pr * ar - pi * ai, pr * ai + pi * ar
        pr, pi = jnp.where(sel, nr, pr), jnp.where(sel, ni, pi)
    return out + [pr, pi]


def _cpow(ar, ai, n):
    out, br, bi = None, ar, ai
    while n:
        if n & 1:
            out = (br, bi) if out is None else (out[0] * br - out[1] * bi, out[0] * bi + out[1] * br)
        br, bi = br * br - bi * bi, 2.0 * br * bi
        n >>= 1
    return out


def _seg_perm_matrix(nrows):
    r = jnp.arange(nrows)
    src = (nrows // SUBLANES) * (r % SUBLANES) + r // SUBLANES
    return (src[:, None] == jnp.arange(nrows)[None, :]).astype(BF16)


def _seg_order_rows(pm, xb):
    return jnp.dot(pm, xb, preferred_element_type=F32).astype(BF16)


def _time_order_rows(pm, x, terms):
    out, rest = None, x
    for t in range(terms):
        piece = rest.astype(BF16)
        part = lax.dot_general(pm, piece, (TN, ((), ())), preferred_element_type=F32)
        out = part if out is None else out + part
        if t + 1 < terms:
            rest = rest - piece.astype(F32)
    return out


def _seg_scan(xr_ref, xi_ref, abr_ref, abi_ref, coef_ref, car_ref, cai_ref, *, nrows, ns, reverse,
              cmat=None, dab=None):
    seg = nrows // SUBLANES
    cw = min(SCAN_LANES, ns)
    row = lax.broadcasted_iota(jnp.int32, (SUBLANES, cw), 0)
    first, last = (SUBLANES - 1, 0) if reverse else (0, SUBLANES - 1)

    def tile(i):
        return pl.ds(pl.multiple_of(((seg - 1 - i) if reverse else i) * SUBLANES, SUBLANES), SUBLANES)

    for cc in range(ns // cw):
        cols = pl.ds(cc * cw, cw)
        ar = jnp.broadcast_to(abr_ref[:, cols], (SUBLANES, cw))
        ai = jnp.broadcast_to(abi_ref[:, cols], (SUBLANES, cw))
        if reverse:
            ai = -ai

        def local(i, x, cols=cols, ar=ar, ai=ai):
            rows = tile(i)
            nr = ar * x[0] - ai * x[1] + xr_ref[rows, cols]
            ni = ar * x[1] + ai * x[0] + xi_ref[rows, cols]
            xr_ref[rows, cols] = nr
            xi_ref[rows, cols] = ni
            return nr, ni

        zero = jnp.zeros((SUBLANES, cw), F32)
        er, ei = lax.fori_loop(0, seg, local, (zero, zero))

        co = [coef_ref[k, :, cols] for k in range(8)]
        for lvl, d in enumerate((1, 2, 4)):
            kr, ki = co[2 * lvl], co[2 * lvl + 1]
            sh = SUBLANES - d if reverse else d
            sr, si = pltpu.roll(er, sh, 0), pltpu.roll(ei, sh, 0)
            er, ei = er + (kr * sr - ki * si), ei + (kr * si + ki * sr)
        c0r, c0i = car_ref[:, cols], cai_ref[:, cols]
        er, ei = er + (co[6] * c0r - co[7] * c0i), ei + (co[6] * c0i + co[7] * c0r)
        nb_shift = SUBLANES - 1 if reverse else 1
        cmr = jnp.where(row == first, c0r, pltpu.roll(er, nb_shift, 0))
        cmi = jnp.where(row == first, c0i, pltpu.roll(ei, nb_shift, 0))
        car_ref[:, cols] = jnp.broadcast_to(er[last:last + 1, :], er.shape)
        cai_ref[:, cols] = jnp.broadcast_to(ei[last:last + 1, :], ei.shape)
        if cmat is not None:
            cmat[0][:, cols] = cmr
            cmat[1][:, cols] = cmi

        w0 = (ar * cmr - ai * cmi, ar * cmi + ai * cmr)
        if dab is None:
            def fix(i, w, cols=cols, ar=ar, ai=ai):
                rows = tile(i)
                xr_ref[rows, cols] = xr_ref[rows, cols] + w[0]
                xi_ref[rows, cols] = xi_ref[rows, cols] + w[1]
                return ar * w[0] - ai * w[1], ar * w[1] + ai * w[0]

            lax.fori_loop(0, seg, fix, w0)
        else:
            s_re, s_im, e_re, e_im, o_re, o_im = dab

            def add(rows, w, pr, pi, acc):
                gr = xr_ref[rows, cols] + w[0]
                gi = xi_ref[rows, cols] + w[1]
                xr_ref[rows, cols] = gr
                xi_ref[rows, cols] = gi
                return acc[0] + (gr * pr + gi * pi), acc[1] + (gi * pr - gr * pi)

            def fix(i, st, cols=cols, ar=ar, ai=ai):
                w, acc = st[:2], st[2:]
                rows = tile(i)
                before = pl.ds(pl.multiple_of((seg - 2 - i) * SUBLANES, SUBLANES), SUBLANES)
                acc = add(rows, w, s_re[before, cols], s_im[before, cols], acc)
                return (ar * w[0] - ai * w[1], ar * w[1] + ai * w[0]) + acc

            st = lax.fori_loop(0, seg - 1, fix, w0 + (zero, zero))
            acc = add(pl.ds(0, SUBLANES), st[:2], e_re[:, cols], e_im[:, cols], st[2:])
            o_re[:, cols] += jnp.sum(acc[0], axis=0, keepdims=True)
            o_im[:, cols] += jnp.sum(acc[1], axis=0, keepdims=True)


def _hosted(core, comm, grid, n_in, n_out, n_scratch):
    ci = len(comm["ins"]) if comm else 0
    co = len(comm["out_shape"]) if comm else 0

    def body(*refs):
        ins, rest = refs[:n_in + ci], refs[n_in + ci:]
        outs, scr = rest[:n_out + co], rest[n_out + co:]
        hooks = functools.partial(_comm_hooks, comm, grid, ins[n_in:], outs[n_out:], scr[n_scratch:])
        hooks(before=True)
        core(*ins[:n_in], *outs[:n_out], *scr[:n_scratch])
        hooks(before=False)

    aliases = {n_in + i: n_out + i for i in range(co)} if comm and comm.get("alias") else {}
    extra = dict(ins=list(comm["ins"]) if comm else [], in_specs=[ANY] * ci, out_specs=[ANY] * co,
                 out_shape=list(comm["out_shape"]) if comm else [], scratch=list(comm["scratch"]) if comm else [],
                 aliases=aliases)
    return body, extra


def _ssm_fwd(proj, bdr, bdi, cdr, cdi, abr, abi, dsk, P, tb, comm=None):
    T = proj.shape[0]
    ntl, ct, st = bdr.shape
    ns = ntl * st
    nb = T // tb

    def core(u_ref, bdr_ref, bdi_ref, cdr_ref, cdi_ref, abr_ref, abi_ref, d_ref, pm_ref,
             y_ref, ge_ref, bsr_ref, bsi_ref, sr, si, coef, car, cai, up):
        @pl.when(pl.program_id(0) == 0)
        def _():
            seg_pow = _cpow(abr_ref[...], abi_ref[...], tb // SUBLANES)
            for k, tile in enumerate(_coef_tiles(seg_pow[0], seg_pow[1], False)):
                coef[k] = tile
            car[...] = jnp.zeros_like(car)
            cai[...] = jnp.zeros_like(cai)

        bsr_ref[...] = car[...]
        bsi_ref[...] = cai[...]
        u = u_ref[...]
        ub = _seg_order_rows(pm_ref[...], u.astype(BF16))
        for s in range(ntl):
            us = ub[:, s * ct:(s + 1) * ct]
            sr[:, s * st:(s + 1) * st] = jnp.dot(us, bdr_ref[s], preferred_element_type=F32)
            si[:, s * st:(s + 1) * st] = jnp.dot(us, bdi_ref[s], preferred_element_type=F32)
        _seg_scan(sr, si, abr_ref, abi_ref, coef, car, cai, nrows=tb, ns=ns, reverse=False)
        for s in range(ntl):
            s_re = sr[:, s * st:(s + 1) * st].astype(BF16)
            s_im = si[:, s * st:(s + 1) * st].astype(BF16)
            up[:, s * ct:(s + 1) * ct] = (jnp.dot(s_re, cdr_ref[s], preferred_element_type=F32)
                                          - jnp.dot(s_im, cdi_ref[s], preferred_element_type=F32))
        y = _time_order_rows(pm_ref[...], up[...], 3) + d_ref[...] * u
        y_ref[...] = y
        ge_ref[...] = _gelu(y).astype(BF16)

    full3 = lambda a: _bs(a.shape, lambda i: (0, 0, 0))
    vec = lambda n: _bs((1, n), lambda i: (0, 0))
    row = _bs((tb, P), lambda i: (i, 0))
    st_spec = _bs((None, SUBLANES, ns), lambda i: (i, 0, 0))
    body, extra = _hosted(core, comm, (nb,), 9, 4, 6)
    return _call(body, grid=(nb,),
                 in_specs=[_bs((tb, P), lambda i: (i, 2)), full3(bdr), full3(bdi), full3(cdr), full3(cdi),
                           vec(ns), vec(ns), vec(P), _bs((tb, tb), lambda i: (0, 0))] + extra["in_specs"],
                 out_specs=[row, row, st_spec, st_spec] + extra["out_specs"],
                 out_shape=[jax.ShapeDtypeStruct((T, P), F32), jax.ShapeDtypeStruct((T, P), BF16),
                            jax.ShapeDtypeStruct((nb, SUBLANES, ns), F32),
                            jax.ShapeDtypeStruct((nb, SUBLANES, ns), F32)] + extra["out_shape"],
                 scratch_shapes=[pltpu.VMEM((tb, ns), F32), pltpu.VMEM((tb, ns), F32),
                                 pltpu.VMEM((8, SUBLANES, ns), F32),
                                 pltpu.VMEM((SUBLANES, ns), F32), pltpu.VMEM((SUBLANES, ns), F32),
                                 pltpu.VMEM((tb, P), F32)] + extra["scratch"],
                 compiler_params=_cp(("arbitrary",)), name="ssm_fwd")(
                     proj, bdr, bdi, cdr, cdi, abr, abi, dsk, _seg_perm_matrix(tb), *extra["ins"])


def _ssm_bwd(proj, y, dge, bsr, bsi, bdr, bdi, cdr, cdi, abr, abi, dsk, dpi, dpg, dsg, P, tb, comm=None):
    T = proj.shape[0]
    ntl, ct, st = bdr.shape
    ns = ntl * st
    nb = T // tb

    def core(u_ref, y_ref, dge_ref, bsr_ref, bsi_ref, abr_ref, abi_ref, d_ref, pm_ref, dpi_ref, dpg_ref, dsg_ref,
             bdr_h, bdi_h, cdr_h, cdi_h,
             dproj_ref, dabr_ref, dabi_ref, dd_ref, dbdr_h, dbdi_h, dcdr_h, dcdi_h,
             wbdr, wbdi, wcdr, wcdi, abdr, abdi, acdr, acdi, spr, spi, gr, gi, coef_f, coef_r,
             car, cai, gcr, gci, ser, sei, dup):
        i = pl.program_id(0)

        @pl.when(i == 0)
        def _():
            for h, w in ((bdr_h, wbdr), (bdi_h, wbdi), (cdr_h, wcdr), (cdi_h, wcdi)):
                pltpu.sync_copy(h, w)
            for a in (abdr, abdi, acdr, acdi, gcr, gci):
                a[...] = jnp.zeros_like(a)
            for o in (dabr_ref, dabi_ref, dd_ref):
                o[...] = jnp.zeros_like(o)
            seg_pow = _cpow(abr_ref[...], abi_ref[...], tb // SUBLANES)
            for k, tile in enumerate(_coef_tiles(seg_pow[0], seg_pow[1], False)):
                coef_f[k] = tile
            for k, tile in enumerate(_coef_tiles(seg_pow[0], seg_pow[1], True)):
                coef_r[k] = tile

        car[...] = bsr_ref[...]
        cai[...] = bsi_ref[...]
        u = u_ref[...]
        dy = dge_ref[...] * _gelu_grad(y_ref[...])
        ub = _seg_order_rows(pm_ref[...], u.astype(BF16))
        dyb = _seg_order_rows(pm_ref[...], dy.astype(BF16))
        for s in range(ntl):
            us = ub[:, s * ct:(s + 1) * ct]
            spr[:, s * st:(s + 1) * st] = jnp.dot(us, wbdr[s], preferred_element_type=F32)
            spi[:, s * st:(s + 1) * st] = jnp.dot(us, wbdi[s], preferred_element_type=F32)
        _seg_scan(spr, spi, abr_ref, abi_ref, coef_f, car, cai, nrows=tb, ns=ns, reverse=False, cmat=(ser, sei))

        for s in range(ntl):
            dys = dyb[:, s * ct:(s + 1) * ct]
            gr[:, s * st:(s + 1) * st] = lax.dot_general(dys, wcdr[s], (NT, ((), ())), preferred_element_type=F32)
            gi[:, s * st:(s + 1) * st] = -lax.dot_general(dys, wcdi[s], (NT, ((), ())), preferred_element_type=F32)
        _seg_scan(gr, gi, abr_ref, abi_ref, coef_r, gcr, gci, nrows=tb, ns=ns, reverse=True,
                  dab=(spr, spi, ser, sei, dabr_ref, dabi_ref))

        for s in range(ntl):
            sl_c, sl_s = slice(s * ct, (s + 1) * ct), slice(s * st, (s + 1) * st)
            s_re = spr[:, sl_s].astype(BF16)
            s_im = spi[:, sl_s].astype(BF16)
            g_re, g_im = gr[:, sl_s].astype(BF16), gi[:, sl_s].astype(BF16)
            dys, us = dyb[:, sl_c], ub[:, sl_c]
            acdr[s] += lax.dot_general(s_re, dys, (TN, ((), ())), preferred_element_type=F32)
            acdi[s] -= lax.dot_general(s_im, dys, (TN, ((), ())), preferred_element_type=F32)
            abdr[s] += lax.dot_general(us, g_re, (TN, ((), ())), preferred_element_type=F32)
            abdi[s] += lax.dot_general(us, g_im, (TN, ((), ())), preferred_element_type=F32)
            dup[:, sl_c] = (lax.dot_general(g_re, wbdr[s], (NT, ((), ())), preferred_element_type=F32)
                            + lax.dot_general(g_im, wbdi[s], (NT, ((), ())), preferred_element_type=F32))
        dd_ref[...] += jnp.sum(dy * u, axis=0, keepdims=True)
        du = _time_order_rows(pm_ref[...], dup[...], 2) + d_ref[...] * dy
        dproj_ref[:, 0:P] = dpi_ref[...]
        dproj_ref[:, P:2 * P] = dpg_ref[...]
        dproj_ref[:, 2 * P:3 * P] = du.astype(BF16)
        dproj_ref[:, 3 * P:4 * P] = dsg_ref[...]

        @pl.when(i == nb - 1)
        def _():
            for a, h in ((abdr, dbdr_h), (abdi, dbdi_h), (acdr, dcdr_h), (acdi, dcdi_h)):
                pltpu.sync_copy(a, h)

    rev = lambda i: nb - 1 - i
    vec = lambda n: _bs((1, n), lambda i: (0, 0))
    row = _bs((tb, P), lambda i: (rev(i), 0))
    st_spec = _bs((None, SUBLANES, ns), lambda i: (rev(i), 0, 0))
    bshape = jax.ShapeDtypeStruct(bdr.shape, F32)
    cshape = jax.ShapeDtypeStruct(cdr.shape, F32)
    body, extra = _hosted(core, comm, (nb,), 16, 8, 21)
    return _call(body, grid=(nb,),
                 in_specs=[_bs((tb, P), lambda i: (rev(i), 2)), row, row, st_spec, st_spec,
                           vec(ns), vec(ns), vec(P), _bs((tb, tb), lambda i: (0, 0)), row, row, row,
                           ANY, ANY, ANY, ANY] + extra["in_specs"],
                 out_specs=[_bs((tb, 4 * P), lambda i: (rev(i), 0)), vec(ns), vec(ns), vec(P), ANY, ANY, ANY, ANY]
                 + extra["out_specs"],
                 out_shape=[jax.ShapeDtypeStruct((T, 4 * P), BF16), jax.ShapeDtypeStruct((1, ns), F32),
                            jax.ShapeDtypeStruct((1, ns), F32), jax.ShapeDtypeStruct((1, P), F32),
                            bshape, bshape, cshape, cshape] + extra["out_shape"],
                 scratch_shapes=[pltpu.VMEM(bdr.shape, BF16), pltpu.VMEM(bdr.shape, BF16),
                                 pltpu.VMEM(cdr.shape, BF16), pltpu.VMEM(cdr.shape, BF16),
                                 pltpu.VMEM(bdr.shape, F32), pltpu.VMEM(bdr.shape, F32),
                                 pltpu.VMEM(cdr.shape, F32), pltpu.VMEM(cdr.shape, F32),
                                 pltpu.VMEM((tb, ns), F32), pltpu.VMEM((tb, ns), F32),
                                 pltpu.VMEM((tb, ns), F32), pltpu.VMEM((tb, ns), F32),
                                 pltpu.VMEM((8, SUBLANES, ns), F32), pltpu.VMEM((8, SUBLANES, ns), F32)]
                 + [pltpu.VMEM((SUBLANES, ns), F32)] * 6 + [pltpu.VMEM((tb, P), F32)] + extra["scratch"],
                 compiler_params=_cp(("arbitrary",)), name="ssm_bwd")(
                     proj, y, dge, bsr, bsi, abr, abi, dsk, _seg_perm_matrix(tb), dpi, dpg, dsg,
                     bdr, bdi, cdr, cdi, *extra["ins"])


def _adamw(w, g, m, v, name, comm=None):
    R, C = w.shape
    tr = _t(R, 256)

    def core(w_ref, g_ref, m_ref, v_ref, d_ref, mo_ref, vo_ref):
        gv = g_ref[...]
        mn = ADAM_B1 * m_ref[...] + (1.0 - ADAM_B1) * gv
        vn = ADAM_B2 * v_ref[...] + (1.0 - ADAM_B2) * (gv * gv)
        m_hat = mn / (1.0 - ADAM_B1 ** ADAM_STEP)
        v_hat = vn / (1.0 - ADAM_B2 ** ADAM_STEP)
        d_ref[...] = -ADAM_LR * (m_hat / (jnp.sqrt(v_hat) + ADAM_EPS) + ADAM_WD * w_ref[...])
        mo_ref[...] = mn
        vo_ref[...] = vn

    blk = _bs((tr, C), lambda i: (i, 0))
    shp = jax.ShapeDtypeStruct((R, C), F32)
    body, extra = _hosted(core, comm, (R // tr,), 4, 3, 0)
    return _call(body, grid=(R // tr,), in_specs=[blk] * 4 + extra["in_specs"],
                 out_specs=[blk] * 3 + extra["out_specs"], out_shape=[shp] * 3 + extra["out_shape"],
                 scratch_shapes=extra["scratch"],
                 compiler_params=_cp(("arbitrary",) if comm else ("parallel",)), name=name)(w, g, m, v, *extra["ins"])


def _sum_cast(grad, got, place, name):
    J, H, C = got.shape
    tr = _t(H, 256)
    nb = H // tr

    def body(pl_ref, a_ref, b_ref, o_ref):
        o_ref[...] = (a_ref[...] + b_ref[...]).astype(BF16)

    blk = _bs((None, tr, C), lambda j, i, pc: (j, i, 0))
    mine = _bs((None, tr, C), lambda j, i, pc: (j, pc[1] * nb + i, 0))
    spec = pltpu.PrefetchScalarGridSpec(num_scalar_prefetch=1, grid=(J, nb), in_specs=[mine, blk], out_specs=blk)
    return _call(body, grid_spec=spec, out_shape=jax.ShapeDtypeStruct((J, H, C), BF16),
                 compiler_params=_cp(("parallel", "parallel")), name=name)(place, grad, got)


def _sum_chips(sent, arrived, place, name):
    J, H, C = arrived.shape
    tr = _t(H, 256)
    nb = H // tr

    def body(pl_ref, own_ref, a0_ref, a1_ref, a2_ref, o_ref):
        acc = own_ref[...].astype(F32)
        for r in (a0_ref, a1_ref, a2_ref):
            acc = acc + r[...].astype(F32)
        o_ref[...] = acc

    def other(k):
        return _bs((None, tr, C), lambda i, pc: (jnp.where(pc[0] <= k, k + 1, k), i, 0))

    spec = pltpu.PrefetchScalarGridSpec(
        num_scalar_prefetch=1, grid=(nb,),
        in_specs=[_bs((None, tr, C), lambda i, pc: (pc[0], i, 0)), other(0), other(1), other(2)],
        out_specs=_bs((tr, C), lambda i, pc: (pc[1] * nb + i, 0)))
    return _call(body, grid_spec=spec, out_shape=jax.ShapeDtypeStruct((2 * H, C), F32),
                 compiler_params=_cp(("parallel",)), name=name)(place, sent, arrived, arrived, arrived)


def _place():
    x, y, c = lax.axis_index("x"), lax.axis_index("y"), lax.axis_index("c")
    chips = [(1 - x, y), (x, 1 - y), (1 - x, 1 - y)]
    return x, y, c, chips


def _split(nrows, row_bytes, align, cap=None):
    k = max(1, min(cap or DMA_MAX_CHUNKS, (nrows * row_bytes) // DMA_CHUNK_BYTES))
    while k > 1 and nrows % (k * align):
        k -= 1
    return k


def _comm_call(plan, name):
    n_in, n_out = len(plan["ins"]), len(plan["out_shape"])

    def body(*refs):
        for phase in plan["phases"]:
            phase(refs[:n_in], refs[n_in:n_in + n_out], refs[n_in + n_out:])

    return _call(body, in_specs=[ANY] * n_in, out_specs=[ANY] * n_out, out_shape=plan["out_shape"],
                 input_output_aliases={i: i for i in range(n_out)} if plan.get("alias") else {},
                 scratch_shapes=plan["scratch"], name=name)(*plan["ins"])


def _comm_hooks(plan, grid, ins, outs, sems, *, before):
    if plan is None:
        return
    nsteps, step = 1, 0
    for d, g in enumerate(grid):
        nsteps, step = nsteps * g, step * g + pl.program_id(d)
    for p, (phase, frac) in enumerate(zip(plan["phases"], plan["at"])):
        if (p == 0) == before:
            pl.when(step == int(frac * (nsteps - 1)))(functools.partial(phase, ins, outs, sems))


def _ag_plan(shards, axes):
    n = len(shards)
    shapes = [a.shape for a in shards]

    def window(ref, i, chip, half=None):
        S, ax = shapes[i], axes[i]
        idx = []
        for d in range(len(S)):
            off, size = 0, S[d]
            if d == 0 and half is not None:
                off, size = half * (S[0] // 2), S[0] // 2
            if d == ax:
                off = off + chip * S[ax]
            idx.append(pl.ds(off, size))
        return ref.at[tuple(idx)]

    def copies(src, full, sems):
        ssem, rsem = sems
        x, y, c, chips = _place()
        me = 2 * x + y
        sib = (x, y, 1 - c)
        idx = [2 * cx + cy for cx, cy in chips]

        def rcopy(i, k, s_ref, d_ref, to):
            return pltpu.make_async_remote_copy(src_ref=s_ref, dst_ref=d_ref, send_sem=ssem.at[i, k],
                                                recv_sem=rsem.at[i, k], device_id=to, device_id_type=MESH)

        def ici(i, j, incoming):
            half_src = src[i].at[pl.ds(c * (shapes[i][0] // 2), shapes[i][0] // 2)]
            return rcopy(i, j, half_src, window(full[i], i, idx[j] if incoming else me, c), (*chips[j], c))

        def fwd(i, j, half):
            w = window(full[i], i, idx[j], half)
            return rcopy(i, 3 + j, w, w, sib)

        def own(i):
            return rcopy(i, 6, src[i], window(full[i], i, me), sib)

        return c, ici, fwd, own

    def send(src, full, sems):
        c, ici, fwd, own = copies(src, full, sems)
        for i in range(n):
            for j in range(3):
                ici(i, j, False).start()
        for i in range(n):
            own(i).start()

    def forward(i, src, full, sems):
        c, ici, fwd, own = copies(src, full, sems)
        for j in range(3):
            ici(i, j, True).wait_recv()
            fwd(i, j, c).start()

    def finish(src, full, sems):
        c, ici, fwd, own = copies(src, full, sems)
        for i in range(n):
            for j in range(3):
                fwd(i, j, 1 - c).wait_recv()
            own(i).wait()
        for i in range(n):
            for j in range(3):
                ici(i, j, False).wait_send()
                fwd(i, j, c).wait_send()

    out_shape = [jax.ShapeDtypeStruct(tuple(N_CHIP * d if k == ax else d for k, d in enumerate(S)), BF16)
                 for S, ax in zip(shapes, axes)]
    sizes = [a.size for a in shards]
    behind = [0.85 * sum(sizes[:i + 1]) / sum(sizes) + 0.05 for i in range(n)]
    return dict(ins=list(shards), out_shape=out_shape,
                phases=[send] + [functools.partial(forward, i) for i in range(n)] + [finish],
                at=[0.0] + behind + [1.0],
                scratch=[pltpu.SemaphoreType.DMA((n, 7)), pltpu.SemaphoreType.DMA((n, 7))])


def _proj_ag(x, g1, wsh, order, tm):
    T, D = x.shape
    P = wsh.shape[1]
    H = D // 2
    nt = T // tm

    def body(order_ref, x_ref, g_ref, wsh_ref, hn_ref, proj_ref, win_ref, wbuf, lsem, ssem, rsem):
        n, i = pl.program_id(0), pl.program_id(1)
        x, y, c, chips = _place()
        me = 2 * x + y
        sib = (x, y, 1 - c)
        idx = [2 * cx + cy for cx, cy in chips]

        def rcopy(k, s_ref, d_ref, to):
            return pltpu.make_async_remote_copy(src_ref=s_ref, dst_ref=d_ref, send_sem=ssem.at[k],
                                                recv_sem=rsem.at[k], device_id=to, device_id_type=MESH)

        def cols(chip):
            return pl.ds(pl.multiple_of(chip * P, LANES), P)

        def rows(half):
            return pl.ds(pl.multiple_of(half * H, 16), H)

        def ici(j, incoming):
            return rcopy(j, wsh_ref.at[rows(c)], win_ref.at[rows(c), cols(idx[j] if incoming else me)],
                         (*chips[j], c))

        def fwd(j, half):
            w = win_ref.at[rows(half), cols(idx[j])]
            return rcopy(3 + j, w, w, sib)

        def own():
            return rcopy(6, wsh_ref, win_ref.at[:, cols(me)], sib)

        def load(src):
            cp = pltpu.make_async_copy(src, wbuf, lsem)
            cp.start()
            cp.wait()

        @pl.when((n == 0) & (i == 0))
        def _():
            ici(0, False).start()
            ici(1, False).start()
            own().start()
            load(wsh_ref)

        for j in range(3):
            @pl.when((n == j + 1) & (i == 0))
            def _(j=j):
                if j == 0:
                    ici(2, False).start()
                ici(j, True).wait_recv()
                fwd(j, c).start()
                fwd(j, 1 - c).wait_recv()
                load(win_ref.at[:, cols(idx[j])])

        xv = x_ref[...]
        r = lax.rsqrt(jnp.mean(xv * xv, axis=-1, keepdims=True) + EPS)
        hn = ((xv * r) * g_ref[...]).astype(BF16)

        @pl.when(n == 0)
        def _():
            hn_ref[...] = hn

        proj_ref[...] = jnp.dot(hn, wbuf[...], preferred_element_type=F32)

        @pl.when((n == 3) & (i == nt - 1))
        def _():
            own().wait()
            for j in range(3):
                ici(j, False).wait_send()
                fwd(j, c).wait_send()

    spec = pltpu.PrefetchScalarGridSpec(
        num_scalar_prefetch=1, grid=(N_CHIP, nt),
        in_specs=[_bs((tm, D), lambda n, i, o: (i, 0)), _bs((1, D), lambda n, i, o: (0, 0)), ANY],
        out_specs=[_bs((tm, D), lambda n, i, o: (jnp.where(n == 0, i, nt - 1), 0)),
                   _bs((tm, P), lambda n, i, o: (i, o[n])), ANY],
        scratch_shapes=[pltpu.VMEM((D, P), BF16), pltpu.SemaphoreType.DMA,
                        pltpu.SemaphoreType.DMA((7,)), pltpu.SemaphoreType.DMA((7,))])
    return _call(body, grid_spec=spec,
                 out_shape=[jax.ShapeDtypeStruct((T, D), BF16), jax.ShapeDtypeStruct((T, N_CHIP * P), F32),
                            jax.ShapeDtypeStruct((D, N_CHIP * P), BF16)],
                 compiler_params=_cp(("arbitrary", "arbitrary")), name="proj_ag")(order, x, g1, wsh)


def _halves_plan(grads):
    n = len(grads)

    def send(g, got, sems):
        ssem, rsem = sems
        x, y, c, _ = _place()
        sib = (x, y, 1 - c)
        for i in range(n):
            J, R, C = g[i].shape
            H = R // 2
            k = _split(H, C * 4, SUBLANES, cap=DMA_MAX_CHUNKS // J)
            hr = H // k
            for j in range(J):
                for q in range(k):
                    other = pl.ds(pl.multiple_of((1 - c) * H + q * hr, SUBLANES), hr)
                    to = pl.ds(q * hr, hr)
                    pltpu.make_async_remote_copy(src_ref=g[i].at[j, other, :], dst_ref=got[i].at[j, to, :],
                                                 send_sem=ssem.at[i], recv_sem=rsem.at[i],
                                                 device_id=sib, device_id_type=MESH).start()

    def finish(g, got, sems):
        ssem, rsem = sems
        x, y, c, _ = _place()
        for i in range(n):
            pltpu.make_async_remote_copy(src_ref=got[i], dst_ref=got[i], send_sem=ssem.at[i], recv_sem=rsem.at[i],
                                         device_id=(x, y, 1 - c), device_id_type=MESH).wait()

    half = [jax.ShapeDtypeStruct((a.shape[0], a.shape[1] // 2, a.shape[2]), a.dtype) for a in grads]
    return dict(ins=list(grads), out_shape=half, phases=[send, finish], at=[0.0, 1.0],
                scratch=[pltpu.SemaphoreType.DMA((n,)), pltpu.SemaphoreType.DMA((n,))])


def _scatter_plan(parts):
    n = len(parts)

    def peers():
        x, y, c, chips = _place()
        return 2 * x + y, c, chips, [2 * cx + cy for cx, cy in chips]

    def send(s, got, sems):
        ssem, rsem = sems
        me, c, chips, idx = peers()
        for i in range(n):
            _, H, C = s[i].shape
            k = _split(H, C * 2, 16, cap=RS_CHUNKS)
            hr = H // k
            for q in range(k):
                rows = pl.ds(q * hr, hr)
                for j in range(3):
                    pltpu.make_async_remote_copy(src_ref=s[i].at[idx[j], rows, :], dst_ref=got[i].at[me, rows, :],
                                                 send_sem=ssem.at[i, j], recv_sem=rsem.at[i, j],
                                                 device_id=(*chips[j], c), device_id_type=MESH).start()

    def finish(s, got, sems):
        ssem, rsem = sems
        me, c, chips, idx = peers()
        for i in range(n):
            for j in range(3):
                pltpu.make_async_remote_copy(src_ref=s[i].at[idx[j]], dst_ref=got[i].at[idx[j]],
                                             send_sem=ssem.at[i, j], recv_sem=rsem.at[i, j],
                                             device_id=(*chips[j], c), device_id_type=MESH).wait()

    return dict(ins=list(parts), out_shape=[jax.ShapeDtypeStruct(a.shape, a.dtype) for a in parts],
                phases=[send, finish], at=[0.0, 1.0],
                scratch=[pltpu.SemaphoreType.DMA((n, 3)), pltpu.SemaphoreType.DMA((n, 3))])


def _join_plan(shards):
    n = len(shards)

    def send(_, full, sems):
        ssem, rsem = sems
        x, y, c, _ = _place()
        sib = (x, y, 1 - c)
        for i in range(n):
            H, C = full[i].shape[0] // 2, full[i].shape[1]
            k = _split(H, C * 4, SUBLANES)
            hr = H // k
            for q in range(k):
                rows = pl.ds(pl.multiple_of(c * H + q * hr, SUBLANES), hr)
                pltpu.make_async_remote_copy(src_ref=full[i].at[rows], dst_ref=full[i].at[rows],
                                             send_sem=ssem.at[i], recv_sem=rsem.at[i],
                                             device_id=sib, device_id_type=MESH).start()

    def finish(_, full, sems):
        ssem, rsem = sems
        x, y, c, _ = _place()
        for i in range(n):
            half = full[i].at[pl.ds(0, full[i].shape[0] // 2)]
            pltpu.make_async_remote_copy(src_ref=half, dst_ref=half, send_sem=ssem.at[i], recv_sem=rsem.at[i],
                                         device_id=(x, y, 1 - c), device_id_type=MESH).wait()

    return dict(ins=list(shards), out_shape=[jax.ShapeDtypeStruct(a.shape, a.dtype) for a in shards],
                phases=[send, finish], at=[0.0, 1.0], alias=True,
                scratch=[pltpu.SemaphoreType.DMA((n,)), pltpu.SemaphoreType.DMA((n,))])


def _allreduce_plan(buf):
    R, L = buf.shape
    RB = R // N_DEV

    def parts(sems):
        xv, got, ov, lsem, ssem, rsem = sems
        x, y, c, _ = _place()
        me = 4 * x + 2 * y + c

        def dev(k):
            return (k // 4, (k // 2) % 2, k % 2)

        def slab(k):
            return pl.ds(pl.multiple_of(k * RB, SUBLANES), RB)

        def first(d, to, landing):
            return pltpu.make_async_remote_copy(src_ref=xv.at[slab(to)], dst_ref=got.at[landing],
                                                send_sem=ssem.at[0, d], recv_sem=rsem.at[0, d],
                                                device_id=dev(to), device_id_type=MESH)

        def second(d, to, k):
            return pltpu.make_async_remote_copy(src_ref=ov.at[slab(k)], dst_ref=ov.at[slab(k)],
                                                send_sem=ssem.at[1, d], recv_sem=rsem.at[1, d],
                                                device_id=dev(to), device_id_type=MESH)

        return me, slab, first, second

    def scatter(ins, outs, sems):
        xv, lsem = sems[0], sems[3]
        me, slab, first, second = parts(sems)
        cp = pltpu.make_async_copy(ins[0], xv, lsem)
        cp.start()
        cp.wait()
        for d in range(1, N_DEV):
            first(d, (me + d) % N_DEV, me).start()

    def reduce(ins, outs, sems):
        xv, got, ov = sems[:3]
        me, slab, first, second = parts(sems)
        got[me] = xv[slab(me), :]
        for d in range(1, N_DEV):
            src = (me + N_DEV - d) % N_DEV
            first(d, src, src).wait_recv()
        acc = got[0]
        for k in range(1, N_DEV):
            acc = acc + got[k]
        ov[slab(me), :] = acc
        for d in range(1, N_DEV):
            second(d, (me + d) % N_DEV, me).start()

    def collect(ins, outs, sems):
        ov, lsem = sems[2], sems[3]
        me, slab, first, second = parts(sems)
        for d in range(1, N_DEV):
            src = (me + N_DEV - d) % N_DEV
            second(d, src, src).wait_recv()
        for d in range(1, N_DEV):
            peer = (me + d) % N_DEV
            first(d, peer, me).wait_send()
            second(d, peer, me).wait_send()
        cp = pltpu.make_async_copy(ov, outs[0], lsem)
        cp.start()
        cp.wait()

    return dict(ins=[buf], out_shape=[jax.ShapeDtypeStruct((R, L), F32)], phases=[scatter, reduce, collect],
                at=[0.0, 0.5, 1.0],
                scratch=[pltpu.VMEM((R, L), F32), pltpu.VMEM((N_DEV, RB, L), F32), pltpu.VMEM((R, L), F32),
                         pltpu.SemaphoreType.DMA, pltpu.SemaphoreType.DMA((2, N_DEV)),
                         pltpu.SemaphoreType.DMA((2, N_DEV))])


def _block_diag(t, gt):
    G, A, B = t.shape
    t4 = t.reshape(G // gt, gt, A, B)
    eye = jnp.eye(gt, dtype=t.dtype)
    return jnp.einsum('sgab,gh->sgahb', t4, eye).reshape(G // gt, gt * A, gt * B)


def _block_diag_extract(m, gt, A, B):
    S = m.shape[0]
    m5 = m.reshape(S, gt, A, gt, B)
    eye = jnp.eye(gt, dtype=m.dtype)
    return jnp.einsum('sgahb,gh->sgab', m5, eye).reshape(S * gt, A, B)


def _pack_small(arrs, rows):
    flat = jnp.concatenate([a.reshape(-1).astype(F32) for a in arrs])
    return jnp.pad(flat, (0, rows * LANES - flat.shape[0])).reshape(rows, LANES)


def _unpack_small(buf, shapes):
    flat = buf.reshape(-1)
    out, off = [], 0
    for s in shapes:
        n = 1
        for d in s:
            n *= d
        out.append(flat[off:off + n].reshape(s))
        off += n
    return out


def kernel(x, p, norm_gain, w_in, w_pool, pool_scale, a_re, a_im, log_dt, b_re, b_im, c_re, c_im, d_skip, w_glu, w_out, w_ple, w_ple_gate, final_gain, loss_target, m_norm_gain, m_w_in, m_w_pool, m_pool_scale, m_a_re, m_a_im, m_log_dt, m_b_re, m_b_im, m_c_re, m_c_im, m_d_skip, m_w_glu, m_w_out, m_w_ple, m_w_ple_gate, m_final_gain, v_norm_gain, v_w_in, v_w_pool, v_pool_scale, v_a_re, v_a_im, v_log_dt, v_b_re, v_b_im, v_c_re, v_c_im, v_d_skip, v_w_glu, v_w_out, v_w_ple, v_w_ple_gate, v_final_gain):
    xs, pe, tgt = x[0], p[0, 0], loss_target[0]
    T, D = xs.shape
    E = pe.shape[1]
    P = D // 2
    NG = len(POOL_WINDOWS)
    PG = P // NG
    G, N, C = P // SSM_GROUP, SSM_STATE, SSM_GROUP
    GT = min(SSM_TILE_GROUPS, G)
    Q = D // N_CHIP

    big = {"w_in": (w_in, m_w_in, v_w_in), "w_pool": (w_pool, m_w_pool, v_w_pool),
           "w_glu": (w_glu, m_w_glu, v_w_glu), "w_out": (w_out, m_w_out, v_w_out),
           "w_ple": (w_ple, m_w_ple, v_w_ple), "w_ple_gate": (w_ple_gate, m_w_ple_gate, v_w_ple_gate)}
    big_names = list(big)
    shard2d = {n: (big[n][0].size // big[n][0].shape[-1], big[n][0].shape[-1]) for n in big_names}
    shard_axis = {"w_in": 1, "w_pool": 1, "w_glu": 1, "w_out": 0, "w_ple": 1, "w_ple_gate": 0}
    shard16 = {n: big[n][0][0].astype(BF16) for n in big_names}
    place = jnp.stack([2 * lax.axis_index("x") + lax.axis_index("y"), lax.axis_index("c")]).astype(jnp.int32)
    mx, my = lax.axis_index("x"), lax.axis_index("y")
    block_order = jnp.stack([2 * mx + my, 2 * (1 - mx) + my, 2 * mx + (1 - my),
                             2 * (1 - mx) + (1 - my)]).astype(jnp.int32)
    later = [n for n in big_names if n != "w_in"]
    ag_later = _ag_plan([shard16[n] for n in later], [shard_axis[n] for n in later])

    rep = lambda a: jnp.repeat(a, C, axis=0)
    a_re_r, a_im_r = rep(a_re[0]), rep(a_im[0])
    ldt_r = rep(jnp.broadcast_to(log_dt[0][:, None], (G, N)))
    bt_re = b_re[0].transpose(0, 2, 1).reshape(G * C, N)
    bt_im = b_im[0].transpose(0, 2, 1).reshape(G * C, N)
    ab_re_r, ab_im_r, bbt_re, bbt_im = _ssm_prep(a_re_r, a_im_r, ldt_r, bt_re, bt_im)
    abr = ab_re_r[::C].reshape(1, G * N)
    abi = ab_im_r[::C].reshape(1, G * N)
    bdr = _block_diag(bbt_re.reshape(G, C, N), GT).astype(BF16)
    bdi = _block_diag(bbt_im.reshape(G, C, N), GT).astype(BF16)
    cdr = _block_diag(c_re[0].transpose(0, 2, 1), GT).astype(BF16)
    cdi = _block_diag(c_im[0].transpose(0, 2, 1), GT).astype(BF16)

    tb = _t(T, 256)
    tbs = _t(T, 256)
    tm = _t(T, 1024)
    tk = _t(T, 2048)
    DH = _t(D, 1024)
    row_k = lambda i, n, k: (i, k)
    row_n = lambda i, n, k: (i, n)
    f32 = lambda *shape: jax.ShapeDtypeStruct(shape, F32)
    hn, proj, win = _proj_ag(xs, norm_gain, shard16["w_in"], block_order, tm)
    y, ge, bsr, bsi, wp, wglu, wout, wple, wpg = _ssm_fwd(proj, bdr, bdi, cdr, cdi, abr, abi, d_skip, P, tbs,
                                                          comm=ag_later)
    pooled, mixed = _pool_fwd(proj, wp, P, tb)
    hg = _mm(ge, wglu, dims=NN, grid=(T // tm, 2 * P // DH, 1),
             a_spec=_bs((tm, P), row_k), b_spec=_bs((P, DH), lambda i, n, k: (k, n)),
             o_spec=_bs((tm, DH), row_n), out_shape=f32(T, 2 * P), name="mm_glu")
    cat = _gate_fwd(mixed, proj, hg, pool_scale, tb)
    h1, h1b = _mm(cat, wout, dims=NN, grid=(T // tm, D // DH, 1), res=xs, bf16_copy=True,
                  a_spec=_bs((tm, D), row_k), b_spec=_bs((D, DH), lambda i, n, k: (k, n)),
                  r_spec=_bs((tm, DH), row_n), o_spec=_bs((tm, DH), row_n), out_shape=f32(T, D), name="mm_out")
    z = _mm(h1b, wpg, dims=NN, grid=(T // tm, D // DH, 1),
            a_spec=_bs((tm, D), row_k), b_spec=_bs((D, DH), lambda i, n, k: (k, n)),
            o_spec=_bs((tm, DH), row_n), out_shape=f32(T, D), name="mm_pgate")
    dh2, de, dz, dg2, lpart = _final_fb(h1, pe, wple, z, tgt, final_gain.reshape(1, D), tb)

    col_m = lambda m, n, k: (k, m)
    col_n = lambda m, n, k: (k, n)
    dh1, dh1b = _mm(dz, wpg, dims=NT, grid=(T // tm, D // DH, 1), res=dh2, bf16_copy=True,
                    a_spec=_bs((tm, D), row_k), b_spec=_bs((DH, D), lambda i, n, k: (n, k)),
                    r_spec=_bs((tm, DH), row_n), o_spec=_bs((tm, DH), row_n), out_shape=f32(T, D), name="mm_dh1")
    g_wpg = _mm(h1b, dz, dims=TN, grid=(D // DH, D // DH, T // tk),
                a_spec=_bs((tk, DH), col_m), b_spec=_bs((tk, DH), col_n),
                o_spec=_bs((DH, DH), lambda m, n, k: (m, n)), out_shape=f32(D, D), name="mm_gwpg")
    g_wple = _mm(pe, de, dims=TN, grid=(1, N_CHIP, T // tk),
                 a_spec=_bs((tk, E), col_m), b_spec=_bs((tk, Q), col_n),
                 o_spec=_bs((None, E, Q), lambda m, j, k: (j, 0, 0)), out_shape=f32(N_CHIP, E, Q), name="mm_gwple")
    dcat = _mm(dh1b, wout, dims=NT, grid=(T // tm, D // DH, 1),
               a_spec=_bs((tm, D), row_k), b_spec=_bs((DH, D), lambda i, n, k: (n, k)),
               o_spec=_bs((tm, DH), row_n), out_shape=f32(T, D), name="mm_dcat")
    g_wout = _mm(cat, dh1b, dims=TN, grid=(D // DH, D // DH, T // tk),
                 a_spec=_bs((tk, DH), col_m), b_spec=_bs((tk, DH), col_n),
                 o_spec=_bs((DH, DH), lambda m, n, k: (m, n)), out_shape=f32(D, D), name="mm_gwout")
    gbig = {"w_out": g_wout.reshape(N_CHIP, Q, D), "w_ple": g_wple, "w_ple_gate": g_wpg.reshape(N_CHIP, Q, D)}
    first = list(gbig)
    res = _gate_bwd(dcat, mixed, proj, hg, pool_scale, tb, comm=_halves_plan([gbig[n] for n in first]))
    dmixed, dpg, dsg, dhg, dps = res[:5]
    got = dict(zip(first, res[5:]))
    dge = _mm(dhg, wglu, dims=NT, grid=(T // tm, 1, 1),
              a_spec=_bs((tm, 2 * P), row_k), b_spec=_bs((P, 2 * P), lambda i, n, k: (n, k)),
              o_spec=_bs((tm, P), row_n), out_shape=f32(T, P), name="mm_dge")
    g_wglu = _mm(ge, dhg, dims=TN, grid=(1, N_CHIP, T // tk),
                 a_spec=_bs((tk, P), col_m), b_spec=_bs((tk, Q), col_n),
                 o_spec=_bs((None, P, Q), lambda m, j, k: (j, 0, 0)), out_shape=f32(N_CHIP, P, Q), name="mm_gwglu")
    g_wp = _mm(pooled, dmixed, dims=TN, grid=(NG, 1, T // tk),
               a_spec=_bs((tk, PG), col_m), b_spec=_bs((tk, PG), col_m),
               o_spec=_bs((None, PG, PG), lambda g, n, k: (g, 0, 0)), out_shape=f32(NG, PG, PG), name="mm_gwp")
    gbig["w_pool"] = g_wp.reshape(NG, N_CHIP, PG // N_CHIP, PG).transpose(1, 0, 2, 3).reshape(
        N_CHIP, NG * PG // N_CHIP, PG)
    gbig["w_glu"] = g_wglu
    res = _pool_bwd(dmixed, wp, tb, comm=_halves_plan([gbig["w_pool"], gbig["w_glu"]]))
    dpi, got["w_pool"], got["w_glu"] = res
    early = list(gbig)
    chip_sums = {n: _sum_cast(gbig[n], got[n], place, "sum_cast_" + n) for n in early}
    res = _ssm_bwd(proj, y, dge, bsr, bsi, bdr, bdi, cdr, cdi, abr, abi, d_skip, dpi, dpg, dsg, P, tbs,
                   comm=_scatter_plan([chip_sums[n] for n in early]))
    dproj, dabr, dabi, dd, dbdr, dbdi, dcdr, dcdi = res[:8]
    arrived = dict(zip(early, res[8:]))
    halves = [_sum_chips(chip_sums[n], arrived[n], place, "sum_chips_" + n) for n in early]
    res = _mm(hn, dproj, dims=TN, grid=(D // DH, N_CHIP, T // tk),
              a_spec=_bs((tk, DH), col_m), b_spec=_bs((tk, P), col_n),
              o_spec=_bs((None, DH, P), lambda m, j, k: (j, m, 0)), out_shape=f32(N_CHIP, D, P),
              name="mm_gwin", comm=_join_plan(halves))
    gbig["w_in"], gshard = res[0], dict(zip(early, res[1:]))
    got["w_in"], = _comm_call(_halves_plan([gbig["w_in"]]), "rs_halves_late")
    chip_sums["w_in"] = _sum_cast(gbig["w_in"], got["w_in"], place, "sum_cast_w_in")
    KH = _t(4 * P, 2048)
    dhn, arrived["w_in"] = _mm(dproj, win, dims=NT, grid=(T // tm, D // DH, 4 * P // KH),
                               a_spec=_bs((tm, KH), row_k), b_spec=_bs((DH, KH), lambda i, n, k: (n, k)),
                               o_spec=_bs((tm, DH), row_n), out_shape=f32(T, D), name="mm_dhn",
                               comm=_scatter_plan([chip_sums["w_in"]]))
    gshard["w_in"], = _comm_call(
        _join_plan([_sum_chips(chip_sums["w_in"], arrived["w_in"], place, "sum_chips_w_in")]), "rs_join_w_in")
    grad_x, dg1 = _norm1_bwd(xs, dhn, dh1, norm_gain, tb)

    dbbt_re = _block_diag_extract(dbdr, GT, C, N).reshape(G * C, N)
    dbbt_im = _block_diag_extract(dbdi, GT, C, N).reshape(G * C, N)
    g_c_re = _block_diag_extract(dcdr, GT, N, C).transpose(0, 2, 1)
    g_c_im = _block_diag_extract(dcdi, GT, N, C).transpose(0, 2, 1)
    dab_re_r = rep(dabr.reshape(G, N)) * (1.0 / C)
    dab_im_r = rep(dabi.reshape(G, N)) * (1.0 / C)
    g_a_re, g_a_im, g_ldt, g_bt_re, g_bt_im = _ssm_prep_bwd(a_re_r, a_im_r, ldt_r, bt_re, bt_im,
                                                            dab_re_r, dab_im_r, dbbt_re, dbbt_im, G)
    g_b_re = g_bt_re.reshape(G, C, N).transpose(0, 2, 1)
    g_b_im = g_bt_im.reshape(G, C, N).transpose(0, 2, 1)


    small_names = ["norm_gain", "pool_scale", "a_re", "a_im", "log_dt", "b_re", "b_im", "c_re", "c_im",
                   "d_skip", "final_gain"]
    small_w = dict(norm_gain=norm_gain, pool_scale=pool_scale, a_re=a_re, a_im=a_im, log_dt=log_dt, b_re=b_re,
                   b_im=b_im, c_re=c_re, c_im=c_im, d_skip=d_skip, final_gain=final_gain)
    small_m = dict(norm_gain=m_norm_gain, pool_scale=m_pool_scale, a_re=m_a_re, a_im=m_a_im, log_dt=m_log_dt,
                   b_re=m_b_re, b_im=m_b_im, c_re=m_c_re, c_im=m_c_im, d_skip=m_d_skip, final_gain=m_final_gain)
    small_v = dict(norm_gain=v_norm_gain, pool_scale=v_pool_scale, a_re=v_a_re, a_im=v_a_im, log_dt=v_log_dt,
                   b_re=v_b_re, b_im=v_b_im, c_re=v_c_re, c_im=v_c_im, d_skip=v_d_skip, final_gain=v_final_gain)
    small_g = dict(norm_gain=dg1, pool_scale=dps, a_re=g_a_re, a_im=g_a_im, log_dt=g_ldt, b_re=g_b_re,
                   b_im=g_b_im, c_re=g_c_re, c_im=g_c_im, d_skip=dd, final_gain=dg2)
    shapes = [small_w[n].shape for n in small_names]
    total = sum(small_w[n].size for n in small_names) + 1
    unit = N_DEV * SUBLANES
    rows = -(-(-(-total // LANES)) // unit) * unit
    gbuf = _pack_small([small_g[n] for n in small_names] + [lpart[0, :1]], rows)
    gsum, = _comm_call(_allreduce_plan(gbuf), "allreduce_small")
    g_out, d_out, m_out, v_out = {}, {}, {}, {}
    for n in big_names:
        w_, m_, v_ = big[n]
        r2 = shard2d[n]
        res = _adamw(w_.reshape(r2), gshard[n], m_.reshape(r2), v_.reshape(r2), "adamw_" + n)
        g_out[n], d_out[n], m_out[n], v_out[n] = (a.reshape(w_.shape) for a in (gshard[n], *res))
    wbuf = _pack_small([small_w[n] for n in small_names], rows)
    mbuf = _pack_small([small_m[n] for n in small_names], rows)
    vbuf = _pack_small([small_v[n] for n in small_names], rows)
    dsm, msm, vsm = _adamw(wbuf, gsum, mbuf, vbuf, "adamw_small")
    g_small = dict(zip(small_names, _unpack_small(gsum, shapes)))
    d_small = dict(zip(small_names, _unpack_small(dsm, shapes)))
    m_small = dict(zip(small_names, _unpack_small(msm, shapes)))
    v_small = dict(zip(small_names, _unpack_small(vsm, shapes)))
    loss = gsum.reshape(-1)[total - 1]

    g_out.update(g_small)
    d_out.update(d_small)
    m_out.update(m_small)
    v_out.update(v_small)

    order = ["norm_gain", "w_in", "w_pool", "pool_scale", "a_re", "a_im", "log_dt", "b_re", "b_im", "c_re",
             "c_im", "d_skip", "w_glu", "w_out", "w_ple", "w_ple_gate", "final_gain"]
    return (loss, grad_x[None], *[g_out[n] for n in order], *[d_out[n] for n in order],
            *[m_out[n] for n in order], *[v_out[n] for n in order])
```

```python
import functools

import jax
import jax.numpy as jnp
from jax import lax
from jax.experimental import pallas as pl
from jax.experimental.pallas import tpu as pltpu

F32, BF16 = jnp.float32, jnp.bfloat16
MESH = pl.DeviceIdType.MESH
ANY = pl.BlockSpec(memory_space=pl.ANY)
VMEM_FULL = pl.BlockSpec(memory_space=pltpu.VMEM)

EPS = 1e-6
A_RE_MAX = -1e-4
SSM_GROUP = 16
SSM_STATE = 64
POOL_WINDOWS = (2, 4, 8, 16)
POOL_HALO = 16
ADAM_LR, ADAM_B1, ADAM_B2, ADAM_EPS, ADAM_WD, ADAM_STEP = 0.001, 0.9, 0.999, 1e-08, 0.01, 10

V7X_VMEM_BYTES = 64 * 1024 * 1024
VMEM_LIMIT = V7X_VMEM_BYTES - 8 * 1024 * 1024
SUBLANES, LANES = 8, 128
SSM_TILE_GROUPS = 8
SCAN_LANES = 512
N_DEV, N_CHIP = 8, 4
DMA_CHUNK_BYTES = 256 * 1024
DMA_MAX_CHUNKS = 32
AG_CHUNKS = 8
RS_CHUNKS = 8


def _t(n, pref):
    return pref if n % pref == 0 else n


def _cp(sem=None, vmem=VMEM_LIMIT):
    return pltpu.CompilerParams(dimension_semantics=sem, vmem_limit_bytes=vmem)


def _call(body, **kw):
    return pl.pallas_call(body, **kw)


NN = ((1,), (0,))
NT = ((1,), (1,))
TN = ((0,), (0,))


def _mm(a, b, *, dims, grid, a_spec, b_spec, o_spec, out_shape, name, res=None, r_spec=None, bf16_copy=False,
        comm=None):
    nk, kax = grid[-1], len(grid) - 1
    acc_shape = tuple(d for d in o_spec.block_shape if d is not None)

    def core(*refs):
        refs = list(refs)
        a_ref, b_ref = refs[:2]
        r_ref = refs[2] if res is not None else None
        outs = refs[3 if res is not None else 2:]
        o_ref = outs[0]
        o2_ref = outs[1] if bf16_copy else None
        acc = outs[-1] if nk > 1 else None

        def finish(r):
            if r_ref is not None:
                r = r + r_ref[...]
            o_ref[...] = r.astype(o_ref.dtype)
            if o2_ref is not None:
                o2_ref[...] = r.astype(BF16)

        part = lax.dot_general(a_ref[...].astype(BF16), b_ref[...].astype(BF16),
                               (dims, ((), ())), preferred_element_type=F32)
        if nk == 1:
            finish(part)
        else:
            k = pl.program_id(kax)

            @pl.when(k == 0)
            def _():
                acc[...] = part

            @pl.when(k > 0)
            def _():
                acc[...] += part

            @pl.when(k == nk - 1)
            def _():
                finish(acc[...])

    ins, specs = [a, b], [a_spec, b_spec]
    if res is not None:
        ins.append(res)
        specs.append(r_spec)
    o_specs, o_shapes = [o_spec], [out_shape]
    if bf16_copy:
        o_specs = [o_spec, o_spec]
        o_shapes = [out_shape, jax.ShapeDtypeStruct(out_shape.shape, BF16)]
    scratch = [pltpu.VMEM(acc_shape, F32)] if nk > 1 else []
    body, extra = _hosted(core, comm, grid, len(ins), len(o_specs), len(scratch))
    sem = ("arbitrary",) * len(grid) if comm else ("parallel",) * kax + ("arbitrary",)
    outs = _call(body, grid=grid, in_specs=specs + extra["in_specs"], out_specs=o_specs + extra["out_specs"],
                 out_shape=o_shapes + extra["out_shape"], scratch_shapes=scratch + extra["scratch"],
                 input_output_aliases=extra["aliases"],
                 compiler_params=_cp(sem), name=name)(*ins, *extra["ins"])
    return outs[0] if len(outs) == 1 else outs


def _bs(shape, fn):
    return pl.BlockSpec(shape, fn)


def _sigmoid(v):
    return 1.0 / (1.0 + jnp.exp(-v))


def _gelu(v):
    return 0.5 * v * (1.0 + jnp.tanh(0.7978845608028654 * (v + 0.044715 * v * v * v)))


def _gelu_grad(v):
    t = jnp.tanh(0.7978845608028654 * (v + 0.044715 * v * v * v))
    return 0.5 * (1.0 + t) + 0.5 * v * (1.0 - t * t) * 0.7978845608028654 * (1.0 + 3 * 0.044715 * v * v)


def _norm1_bwd(x, dhn, dh1, g1, tb, comm=None):
    T, D = x.shape

    def core(x_ref, dhn_ref, dh1_ref, g_ref, dx_ref, dg_ref):
        @pl.when(pl.program_id(0) == 0)
        def _():
            dg_ref[...] = jnp.zeros_like(dg_ref)

        xv = x_ref[...]
        r = lax.rsqrt(jnp.mean(xv * xv, axis=-1, keepdims=True) + EPS)
        xh = xv * r
        dhn_v = dhn_ref[...]
        dg_ref[...] += jnp.sum(dhn_v * xh, axis=0, keepdims=True)
        dxh = dhn_v * g_ref[...]
        dx_ref[...] = dh1_ref[...] + r * (dxh - xh * jnp.mean(dxh * xh, axis=-1, keepdims=True))

    row = _bs((tb, D), lambda i: (i, 0))
    vec = _bs((1, D), lambda i: (0, 0))
    body, extra = _hosted(core, comm, (T // tb,), 4, 2, 0)
    return _call(body, grid=(T // tb,), in_specs=[row, row, row, vec] + extra["in_specs"],
                 out_specs=[row, vec] + extra["out_specs"],
                 out_shape=[jax.ShapeDtypeStruct((T, D), F32), jax.ShapeDtypeStruct((1, D), F32)] + extra["out_shape"],
                 scratch_shapes=extra["scratch"], input_output_aliases=extra["aliases"],
                 compiler_params=_cp(("arbitrary",)), name="norm1_bwd")(x, dhn, dh1, g1, *extra["ins"])


def _gate_fwd(mixed, proj, hg, ps, tb):
    T, P = mixed.shape

    def body(mx_ref, pg_ref, sg_ref, hg_ref, ps_ref, o_ref):
        pg, sg = pg_ref[...], sg_ref[...]
        ya = (mx_ref[...] * ps_ref[...]) * (pg * _sigmoid(pg))
        hgv = hg_ref[...]
        o = hgv[:, :P] * _sigmoid(hgv[:, P:])
        yb = o * (sg * _sigmoid(sg))
        o_ref[:, :P] = ya.astype(BF16)
        o_ref[:, P:] = yb.astype(BF16)

    return _call(body, grid=(T // tb,),
                 in_specs=[_bs((tb, P), lambda i: (i, 0)), _bs((tb, P), lambda i: (i, 1)),
                           _bs((tb, P), lambda i: (i, 3)), _bs((tb, 2 * P), lambda i: (i, 0)),
                           _bs((1, P), lambda i: (0, 0))],
                 out_specs=_bs((tb, 2 * P), lambda i: (i, 0)),
                 out_shape=jax.ShapeDtypeStruct((T, 2 * P), BF16),
                 compiler_params=_cp(("parallel",)), name="gate_fwd")(mixed, proj, proj, hg, ps)


def _gate_bwd(dcat, mixed, proj, hg, ps, tb, comm=None):
    T, P = mixed.shape

    def core(dc_ref, mx_ref, pg_ref, sg_ref, hg_ref, ps_ref, dmx_ref, dpg_ref, dsg_ref, dhg_ref, dps_ref):
        @pl.when(pl.program_id(0) == 0)
        def _():
            dps_ref[...] = jnp.zeros_like(dps_ref)

        dc = dc_ref[...]
        dya, dyb = dc[:, :P], dc[:, P:]
        pg, sg, mx, psv = pg_ref[...], sg_ref[...], mx_ref[...], ps_ref[...]
        s_pg = _sigmoid(pg)
        dpa = dya * (pg * s_pg)
        dpg_ref[...] = (dya * (mx * psv) * (s_pg * (1.0 + pg * (1.0 - s_pg)))).astype(BF16)
        dps_ref[...] += jnp.sum(dpa * mx, axis=0, keepdims=True)
        dmx_ref[...] = (dpa * psv).astype(BF16)
        hgv = hg_ref[...]
        h1, s_h2 = hgv[:, :P], _sigmoid(hgv[:, P:])
        s_sg = _sigmoid(sg)
        do = dyb * (sg * s_sg)
        dsg_ref[...] = (dyb * (h1 * s_h2) * (s_sg * (1.0 + sg * (1.0 - s_sg)))).astype(BF16)
        dhg_ref[:, :P] = (do * s_h2).astype(BF16)
        dhg_ref[:, P:] = (do * h1 * s_h2 * (1.0 - s_h2)).astype(BF16)

    rowp = _bs((tb, P), lambda i: (i, 0))
    row2 = _bs((tb, 2 * P), lambda i: (i, 0))
    vec = _bs((1, P), lambda i: (0, 0))
    body, extra = _hosted(core, comm, (T // tb,), 6, 5, 0)
    return _call(body, grid=(T // tb,),
                 in_specs=[row2, rowp, _bs((tb, P), lambda i: (i, 1)), _bs((tb, P), lambda i: (i, 3)), row2, vec]
                 + extra["in_specs"],
                 out_specs=[rowp, rowp, rowp, row2, vec] + extra["out_specs"],
                 out_shape=[jax.ShapeDtypeStruct((T, P), BF16), jax.ShapeDtypeStruct((T, P), BF16),
                            jax.ShapeDtypeStruct((T, P), BF16), jax.ShapeDtypeStruct((T, 2 * P), BF16),
                            jax.ShapeDtypeStruct((1, P), F32)] + extra["out_shape"],
                 scratch_shapes=extra["scratch"],
                 compiler_params=_cp(("arbitrary",)), name="gate_bwd")(dcat, mixed, proj, proj, hg, ps, *extra["ins"])


def _final_fb(h1, pe, wple, z, tgt, g2, tb):
    T, D = h1.shape
    E = pe.shape[1]

    def body(h1_ref, p_ref, w_ref, z_ref, t_ref, g_ref, dh2_ref, de_ref, dz_ref, dg_ref, l_ref):
        @pl.when(pl.program_id(0) == 0)
        def _():
            dg_ref[...] = jnp.zeros_like(dg_ref)
            l_ref[...] = jnp.zeros_like(l_ref)

        ev = jnp.dot(p_ref[...].astype(BF16), w_ref[...], preferred_element_type=F32)
        s = _sigmoid(z_ref[...])
        h2 = h1_ref[...] + ev * s
        r = lax.rsqrt(jnp.mean(h2 * h2, axis=-1, keepdims=True) + EPS)
        xh = h2 * r
        gv = g_ref[...]
        diff = xh * gv - t_ref[...]
        l_ref[...] += 0.5 * jnp.sum(jnp.mean(diff * diff, axis=-1, keepdims=True))
        dout = diff * (1.0 / D)
        dg_ref[...] += jnp.sum(dout * xh, axis=0, keepdims=True)
        dxh = dout * gv
        dh2 = r * (dxh - xh * jnp.mean(dxh * xh, axis=-1, keepdims=True))
        dh2_ref[...] = dh2
        de_ref[...] = (dh2 * s).astype(BF16)
        dz_ref[...] = (dh2 * ev * s * (1.0 - s)).astype(BF16)

    row = _bs((tb, D), lambda i: (i, 0))
    vec = _bs((1, D), lambda i: (0, 0))
    return _call(body, grid=(T // tb,),
                 in_specs=[row, _bs((tb, E), lambda i: (i, 0)), _bs((E, D), lambda i: (0, 0)), row, row, vec],
                 out_specs=[row, row, row, vec, _bs((1, LANES), lambda i: (0, 0))],
                 out_shape=[jax.ShapeDtypeStruct((T, D), F32), jax.ShapeDtypeStruct((T, D), BF16),
                            jax.ShapeDtypeStruct((T, D), BF16), jax.ShapeDtypeStruct((1, D), F32),
                            jax.ShapeDtypeStruct((1, LANES), F32)],
                 compiler_params=_cp(("arbitrary",)), name="final_fb")(h1, pe, wple, z, tgt, g2)


def _pool_inv_count(t0, rows, pg, ngroups):
    t = t0 + lax.broadcasted_iota(jnp.int32, (rows, pg), 0)
    parts = []
    for w in POOL_WINDOWS[:ngroups]:
        parts.append(jnp.where(t + 1 >= w, 1.0 / w, 1.0 / (t + 1).astype(F32)))
    return parts


def _pool_fwd(proj, wp, P, tb):
    T = proj.shape[0]
    ng = len(POOL_WINDOWS)
    pg = P // ng
    hb = tb // POOL_HALO

    def body(v_ref, tail_ref, w_ref, o_ref, mx_ref, ext):
        i = pl.program_id(0)
        ext[pl.ds(0, POOL_HALO), :] = jnp.where(i > 0, tail_ref[...], 0.0)
        ext[pl.ds(POOL_HALO, tb), :] = v_ref[...]
        inv = _pool_inv_count(i * tb, tb, pg, ng)
        for g, w in enumerate(POOL_WINDOWS):
            cols = pl.ds(g * pg, pg)
            win = ext[pl.ds(POOL_HALO, tb), cols]
            for k in range(1, w):
                win = win + ext[pl.ds(POOL_HALO - k, tb), cols]
            pooled = (win * inv[g] - ext[pl.ds(POOL_HALO, tb), cols]).astype(BF16)
            o_ref[:, cols] = pooled
            mx_ref[:, cols] = jnp.dot(pooled, w_ref[g], preferred_element_type=F32)

    row = _bs((tb, P), lambda i: (i, 0))
    return _call(body, grid=(T // tb,),
                 in_specs=[row, _bs((POOL_HALO, P), lambda i: (jnp.maximum(i * hb - 1, 0), 0)),
                           _bs(wp.shape, lambda i: (0, 0, 0))],
                 out_specs=[row, row],
                 out_shape=[jax.ShapeDtypeStruct((T, P), BF16), jax.ShapeDtypeStruct((T, P), F32)],
                 scratch_shapes=[pltpu.VMEM((tb + POOL_HALO, P), F32)],
                 compiler_params=_cp(("arbitrary",)), name="pool_fwd")(proj, proj, wp)


def _pool_bwd(dmixed, wp, tb, comm=None):
    T, P = dmixed.shape
    ng = len(POOL_WINDOWS)
    pg = P // ng
    hb = tb // POOL_HALO
    nb = T // tb

    def core(d_ref, head_ref, w_ref, o_ref, ext, dpl):
        i = pl.program_id(0)
        inv = _pool_inv_count(i * tb, tb, pg, ng)
        invh = _pool_inv_count((i + 1) * tb, POOL_HALO, pg, ng)
        for g in range(ng):
            cols = pl.ds(g * pg, pg)
            dp = lax.dot_general(d_ref[:, cols], w_ref[g], (NT, ((), ())), preferred_element_type=F32)
            dph = lax.dot_general(head_ref[:, cols], w_ref[g], (NT, ((), ())), preferred_element_type=F32)
            dpl[:, cols] = dp
            ext[pl.ds(0, tb), cols] = dp * inv[g]
            ext[pl.ds(tb, POOL_HALO), cols] = jnp.where(i < nb - 1, dph * invh[g], 0.0)
        for g, w in enumerate(POOL_WINDOWS):
            cols = pl.ds(g * pg, pg)
            acc = ext[pl.ds(0, tb), cols]
            for k in range(1, w):
                acc = acc + ext[pl.ds(k, tb), cols]
            o_ref[:, cols] = (acc - dpl[:, cols]).astype(BF16)

    body, extra = _hosted(core, comm, (nb,), 3, 1, 2)
    return _call(body, grid=(nb,),
                 in_specs=[_bs((tb, P), lambda i: (i, 0)),
                           _bs((POOL_HALO, P), lambda i: (jnp.minimum((i + 1) * hb, T // POOL_HALO - 1), 0)),
                           _bs(wp.shape, lambda i: (0, 0, 0))] + extra["in_specs"],
                 out_specs=[_bs((tb, P), lambda i: (i, 0))] + extra["out_specs"],
                 out_shape=[jax.ShapeDtypeStruct((T, P), BF16)] + extra["out_shape"],
                 scratch_shapes=[pltpu.VMEM((tb + POOL_HALO, P), F32), pltpu.VMEM((tb, P), F32)] + extra["scratch"],
                 compiler_params=_cp(("arbitrary",)), name="pool_bwd")(dmixed, dmixed, wp, *extra["ins"])


def _zoh(a_re, a_im, ldt, b_re, b_im):
    lam_re = jnp.minimum(a_re, A_RE_MAX)
    lam_im = a_im
    dt = jnp.exp(ldt)
    mag = jnp.exp(lam_re * dt)
    ang = lam_im * dt
    ab_re = mag * jnp.cos(ang)
    ab_im = mag * jnp.sin(ang)
    den = lam_re * lam_re + lam_im * lam_im
    n_re = ab_re - 1.0
    n_im = ab_im
    q_re = (n_re * lam_re + n_im * lam_im) / den
    q_im = (n_im * lam_re - n_re * lam_im) / den
    return ab_re, ab_im, q_re * b_re - q_im * b_im, q_re * b_im + q_im * b_re


def _ssm_prep(a_re, a_im, ldt, bt_re, bt_im):
    shp = jax.ShapeDtypeStruct(a_re.shape, F32)

    def body(a, b, c, d, e, o0, o1, o2, o3):
        r = _zoh(a[...], b[...], c[...], d[...], e[...])
        o0[...], o1[...], o2[...], o3[...] = r

    return _call(body, in_specs=[VMEM_FULL] * 5, out_specs=[VMEM_FULL] * 4, out_shape=[shp] * 4,
                 name="ssm_prep")(a_re, a_im, ldt, bt_re, bt_im)


def _ssm_prep_bwd(a_re, a_im, ldt, bt_re, bt_im, dab_re, dab_im, dbb_re, dbb_im, G):
    GC, N = a_re.shape
    C = GC // G

    def body(a, b, c, d, e, g0, g1, g2, g3, da_re, da_im, dldt, db_re, db_im):
        _, vjp = jax.vjp(_zoh, a[...], b[...], c[...], d[...], e[...])
        ga_re, ga_im, gl, gb_re, gb_im = vjp((g0[...], g1[...], g2[...], g3[...]))
        da_re[...] = jnp.sum(ga_re.reshape(G, C, N), axis=1)
        da_im[...] = jnp.sum(ga_im.reshape(G, C, N), axis=1)
        dldt[...] = jnp.sum(jnp.sum(gl.reshape(G, C, N), axis=1), axis=1, keepdims=True)
        db_re[...] = gb_re
        db_im[...] = gb_im

    gn = jax.ShapeDtypeStruct((G, N), F32)
    full = jax.ShapeDtypeStruct((GC, N), F32)
    return _call(body, in_specs=[VMEM_FULL] * 9, out_specs=[VMEM_FULL] * 5,
                 out_shape=[gn, gn, jax.ShapeDtypeStruct((G, 1), F32), full, full],
                 name="ssm_prep_bwd")(a_re, a_im, ldt, bt_re, bt_im, dab_re, dab_im, dbb_re, dbb_im)


def _coef_tiles(abr, abi, reverse):
    ns = abr.shape[1]
    row = lax.broadcasted_iota(jnp.int32, (SUBLANES, ns), 0)
    ar = jnp.broadcast_to(abr, (SUBLANES, ns))
    ai = jnp.broadcast_to(-abi if reverse else abi, (SUBLANES, ns))
    a2r, a2i = ar * ar - ai * ai, 2.0 * ar * ai
    a4r, a4i = a2r * a2r - a2i * a2i, 2.0 * a2r * a2i
    out = []
    for d, (vr, vi) in ((1, (ar, ai)), (2, (a2r, a2i)), (4, (a4r, a4i))):
        keep = (row < SUBLANES - d) if reverse else (row >= d)
        out += [jnp.where(keep, vr, 0.0), jnp.where(keep, vi, 0.0)]
    pr, pi = ar, ai
    for k in range(1, SUBLANES):
        sel = (row <= SUBLANES - 1 - k) if reverse else (row >= k)
        nr, ni = pr * ar - pi * ai, pr * ai + pi * ar
        pr, pi = jnp.where(sel, nr, pr), jnp.where(sel, ni, pi)
    return out + [pr, pi]


def _cpow(ar, ai, n):
    out, br, bi = None, ar, ai
    while n:
        if n & 1:
            out = (br, bi) if out is None else (out[0] * br - out[1] * bi, out[0] * bi + out[1] * br)
        br, bi = br * br - bi * bi, 2.0 * br * bi
        n >>= 1
    return out


def _seg_perm_matrix(nrows):
    r = jnp.arange(nrows)
    src = (nrows // SUBLANES) * (r % SUBLANES) + r // SUBLANES
    return (src[:, None] == jnp.arange(nrows)[None, :]).astype(BF16)


def _seg_order_rows(pm, xb):
    return jnp.dot(pm, xb, preferred_element_type=F32).astype(BF16)


def _time_order_rows(pm, x, terms):
    out, rest = None, x
    for t in range(terms):
        piece = rest.astype(BF16)
        part = lax.dot_general(pm, piece, (TN, ((), ())), preferred_element_type=F32)
        out = part if out is None else out + part
        if t + 1 < terms:
            rest = rest - piece.astype(F32)
    return out


def _seg_scan(xr_ref, xi_ref, abr_ref, abi_ref, coef_ref, car_ref, cai_ref, *, nrows, ns, reverse,
              cmat=None, dab=None):
    seg = nrows // SUBLANES
    cw = min(SCAN_LANES, ns)
    row = lax.broadcasted_iota(jnp.int32, (SUBLANES, cw), 0)
    first, last = (SUBLANES - 1, 0) if reverse else (0, SUBLANES - 1)

    def tile(i):
        return pl.ds(pl.multiple_of(((seg - 1 - i) if reverse else i) * SUBLANES, SUBLANES), SUBLANES)

    for cc in range(ns // cw):
        cols = pl.ds(cc * cw, cw)
        ar = jnp.broadcast_to(abr_ref[:, cols], (SUBLANES, cw))
        ai = jnp.broadcast_to(abi_ref[:, cols], (SUBLANES, cw))
        if reverse:
            ai = -ai

        def local(i, x, cols=cols, ar=ar, ai=ai):
            rows = tile(i)
            nr = ar * x[0] - ai * x[1] + xr_ref[rows, cols]
            ni = ar * x[1] + ai * x[0] + xi_ref[rows, cols]
            xr_ref[rows, cols] = nr
            xi_ref[rows, cols] = ni
            return nr, ni

        zero = jnp.zeros((SUBLANES, cw), F32)
        er, ei = lax.fori_loop(0, seg, local, (zero, zero))

        co = [coef_ref[k, :, cols] for k in range(8)]
        for lvl, d in enumerate((1, 2, 4)):
            kr, ki = co[2 * lvl], co[2 * lvl + 1]
            sh = SUBLANES - d if reverse else d
            sr, si = pltpu.roll(er, sh, 0), pltpu.roll(ei, sh, 0)
            er, ei = er + (kr * sr - ki * si), ei + (kr * si + ki * sr)
        c0r, c0i = car_ref[:, cols], cai_ref[:, cols]
        er, ei = er + (co[6] * c0r - co[7] * c0i), ei + (co[6] * c0i + co[7] * c0r)
        nb_shift = SUBLANES - 1 if reverse else 1
        cmr = jnp.where(row == first, c0r, pltpu.roll(er, nb_shift, 0))
        cmi = jnp.where(row == first, c0i, pltpu.roll(ei, nb_shift, 0))
        car_ref[:, cols] = jnp.broadcast_to(er[last:last + 1, :], er.shape)
        cai_ref[:, cols] = jnp.broadcast_to(ei[last:last + 1, :], ei.shape)
        if cmat is not None:
            cmat[0][:, cols] = cmr
            cmat[1][:, cols] = cmi

        w0 = (ar * cmr - ai * cmi, ar * cmi + ai * cmr)
        if dab is None:
            def fix(i, w, cols=cols, ar=ar, ai=ai):
                rows = tile(i)
                xr_ref[rows, cols] = xr_ref[rows, cols] + w[0]
                xi_ref[rows, cols] = xi_ref[rows, cols] + w[1]
                return ar * w[0] - ai * w[1], ar * w[1] + ai * w[0]

            lax.fori_loop(0, seg, fix, w0)
        else:
            s_re, s_im, e_re, e_im, o_re, o_im = dab

            def add(rows, w, pr, pi, acc):
                gr = xr_ref[rows, cols] + w[0]
                gi = xi_ref[rows, cols] + w[1]
                xr_ref[rows, cols] = gr
                xi_ref[rows, cols] = gi
                return acc[0] + (gr * pr + gi * pi), acc[1] + (gi * pr - gr * pi)

            def fix(i, st, cols=cols, ar=ar, ai=ai):
                w, acc = st[:2], st[2:]
                rows = tile(i)
                before = pl.ds(pl.multiple_of((seg - 2 - i) * SUBLANES, SUBLANES), SUBLANES)
                acc = add(rows, w, s_re[before, cols], s_im[before, cols], acc)
                return (ar * w[0] - ai * w[1], ar * w[1] + ai * w[0]) + acc

            st = lax.fori_loop(0, seg - 1, fix, w0 + (zero, zero))
            acc = add(pl.ds(0, SUBLANES), st[:2], e_re[:, cols], e_im[:, cols], st[2:])
            o_re[:, cols] += jnp.sum(acc[0], axis=0, keepdims=True)
            o_im[:, cols] += jnp.sum(acc[1], axis=0, keepdims=True)


def _hosted(core, comm, grid, n_in, n_out, n_scratch):
    ci = len(comm["ins"]) if comm else 0
    co = len(comm["out_shape"]) if comm else 0

    def body(*refs):
        ins, rest = refs[:n_in + ci], refs[n_in + ci:]
        outs, scr = rest[:n_out + co], rest[n_out + co:]
        hooks = functools.partial(_comm_hooks, comm, grid, ins[n_in:], outs[n_out:], scr[n_scratch:])
        hooks(before=True)
        core(*ins[:n_in], *outs[:n_out], *scr[:n_scratch])
        hooks(before=False)

    aliases = {n_in + i: n_out + i for i in range(co)} if comm and comm.get("alias") else {}
    extra = dict(ins=list(comm["ins"]) if comm else [], in_specs=[ANY] * ci, out_specs=[ANY] * co,
                 out_shape=list(comm["out_shape"]) if comm else [], scratch=list(comm["scratch"]) if comm else [],
                 aliases=aliases)
    return body, extra


def _ssm_fwd(proj, bdr, bdi, cdr, cdi, abr, abi, dsk, P, tb, comm=None):
    T = proj.shape[0]
    ntl, ct, st = bdr.shape
    ns = ntl * st
    nb = T // tb

    def core(u_ref, bdr_ref, bdi_ref, cdr_ref, cdi_ref, abr_ref, abi_ref, d_ref, pm_ref,
             y_ref, ge_ref, bsr_ref, bsi_ref, sr, si, coef, car, cai, up):
        @pl.when(pl.program_id(0) == 0)
        def _():
            seg_pow = _cpow(abr_ref[...], abi_ref[...], tb // SUBLANES)
            for k, tile in enumerate(_coef_tiles(seg_pow[0], seg_pow[1], False)):
                coef[k] = tile
            car[...] = jnp.zeros_like(car)
            cai[...] = jnp.zeros_like(cai)

        bsr_ref[...] = car[...]
        bsi_ref[...] = cai[...]
        u = u_ref[...]
        ub = _seg_order_rows(pm_ref[...], u.astype(BF16))
        for s in range(ntl):
            us = ub[:, s * ct:(s + 1) * ct]
            sr[:, s * st:(s + 1) * st] = jnp.dot(us, bdr_ref[s], preferred_element_type=F32)
            si[:, s * st:(s + 1) * st] = jnp.dot(us, bdi_ref[s], preferred_element_type=F32)
        _seg_scan(sr, si, abr_ref, abi_ref, coef, car, cai, nrows=tb, ns=ns, reverse=False)
        for s in range(ntl):
            s_re = sr[:, s * st:(s + 1) * st].astype(BF16)
            s_im = si[:, s * st:(s + 1) * st].astype(BF16)
            up[:, s * ct:(s + 1) * ct] = (jnp.dot(s_re, cdr_ref[s], preferred_element_type=F32)
                                          - jnp.dot(s_im, cdi_ref[s], preferred_element_type=F32))
        y = _time_order_rows(pm_ref[...], up[...], 3) + d_ref[...] * u
        y_ref[...] = y
        ge_ref[...] = _gelu(y).astype(BF16)

    full3 = lambda a: _bs(a.shape, lambda i: (0, 0, 0))
    vec = lambda n: _bs((1, n), lambda i: (0, 0))
    row = _bs((tb, P), lambda i: (i, 0))
    st_spec = _bs((None, SUBLANES, ns), lambda i: (i, 0, 0))
    body, extra = _hosted(core, comm, (nb,), 9, 4, 6)
    return _call(body, grid=(nb,),
                 in_specs=[_bs((tb, P), lambda i: (i, 2)), full3(bdr), full3(bdi), full3(cdr), full3(cdi),
                           vec(ns), vec(ns), vec(P), _bs((tb, tb), lambda i: (0, 0))] + extra["in_specs"],
                 out_specs=[row, row, st_spec, st_spec] + extra["out_specs"],
                 out_shape=[jax.ShapeDtypeStruct((T, P), F32), jax.ShapeDtypeStruct((T, P), BF16),
                            jax.ShapeDtypeStruct((nb, SUBLANES, ns), F32),
                            jax.ShapeDtypeStruct((nb, SUBLANES, ns), F32)] + extra["out_shape"],
                 scratch_shapes=[pltpu.VMEM((tb, ns), F32), pltpu.VMEM((tb, ns), F32),
                                 pltpu.VMEM((8, SUBLANES, ns), F32),
                                 pltpu.VMEM((SUBLANES, ns), F32), pltpu.VMEM((SUBLANES, ns), F32),
                                 pltpu.VMEM((tb, P), F32)] + extra["scratch"],
                 compiler_params=_cp(("arbitrary",)), name="ssm_fwd")(
                     proj, bdr, bdi, cdr, cdi, abr, abi, dsk, _seg_perm_matrix(tb), *extra["ins"])


def _ssm_bwd(proj, y, dge, bsr, bsi, bdr, bdi, cdr, cdi, abr, abi, dsk, dpi, dpg, dsg, P, tb, comm=None):
    T = proj.shape[0]
    ntl, ct, st = bdr.shape
    ns = ntl * st
    nb = T // tb

    def core(u_ref, y_ref, dge_ref, bsr_ref, bsi_ref, abr_ref, abi_ref, d_ref, pm_ref, dpi_ref, dpg_ref, dsg_ref,
             bdr_h, bdi_h, cdr_h, cdi_h,
             dproj_ref, dabr_ref, dabi_ref, dd_ref, dbdr_h, dbdi_h, dcdr_h, dcdi_h,
             wbdr, wbdi, wcdr, wcdi, abdr, abdi, acdr, acdi, spr, spi, gr, gi, coef_f, coef_r,
             car, cai, gcr, gci, ser, sei, dup):
        i = pl.program_id(0)

        @pl.when(i == 0)
        def _():
            for h, w in ((bdr_h, wbdr), (bdi_h, wbdi), (cdr_h, wcdr), (cdi_h, wcdi)):
                pltpu.sync_copy(h, w)
            for a in (abdr, abdi, acdr, acdi, gcr, gci):
                a[...] = jnp.zeros_like(a)
            for o in (dabr_ref, dabi_ref, dd_ref):
                o[...] = jnp.zeros_like(o)
            seg_pow = _cpow(abr_ref[...], abi_ref[...], tb // SUBLANES)
            for k, tile in enumerate(_coef_tiles(seg_pow[0], seg_pow[1], False)):
                coef_f[k] = tile
            for k, tile in enumerate(_coef_tiles(seg_pow[0], seg_pow[1], True)):
                coef_r[k] = tile

        car[...] = bsr_ref[...]
        cai[...] = bsi_ref[...]
        u = u_ref[...]
        dy = dge_ref[...] * _gelu_grad(y_ref[...])
        ub = _seg_order_rows(pm_ref[...], u.astype(BF16))
        dyb = _seg_order_rows(pm_ref[...], dy.astype(BF16))
        for s in range(ntl):
            us = ub[:, s * ct:(s + 1) * ct]
            spr[:, s * st:(s + 1) * st] = jnp.dot(us, wbdr[s], preferred_element_type=F32)
            spi[:, s * st:(s + 1) * st] = jnp.dot(us, wbdi[s], preferred_element_type=F32)
        _seg_scan(spr, spi, abr_ref, abi_ref, coef_f, car, cai, nrows=tb, ns=ns, reverse=False, cmat=(ser, sei))

        for s in range(ntl):
            dys = dyb[:, s * ct:(s + 1) * ct]
            gr[:, s * st:(s + 1) * st] = lax.dot_general(dys, wcdr[s], (NT, ((), ())), preferred_element_type=F32)
            gi[:, s * st:(s + 1) * st] = -lax.dot_general(dys, wcdi[s], (NT, ((), ())), preferred_element_type=F32)
        _seg_scan(gr, gi, abr_ref, abi_ref, coef_r, gcr, gci, nrows=tb, ns=ns, reverse=True,
                  dab=(spr, spi, ser, sei, dabr_ref, dabi_ref))

        for s in range(ntl):
            sl_c, sl_s = slice(s * ct, (s + 1) * ct), slice(s * st, (s + 1) * st)
            s_re = spr[:, sl_s].astype(BF16)
            s_im = spi[:, sl_s].astype(BF16)
            g_re, g_im = gr[:, sl_s].astype(BF16), gi[:, sl_s].astype(BF16)
            dys, us = dyb[:, sl_c], ub[:, sl_c]
            acdr[s] += lax.dot_general(s_re, dys, (TN, ((), ())), preferred_element_type=F32)
            acdi[s] -= lax.dot_general(s_im, dys, (TN, ((), ())), preferred_element_type=F32)
            abdr[s] += lax.dot_general(us, g_re, (TN, ((), ())), preferred_element_type=F32)
            abdi[s] += lax.dot_general(us, g_im, (TN, ((), ())), preferred_element_type=F32)
            dup[:, sl_c] = (lax.dot_general(g_re, wbdr[s], (NT, ((), ())), preferred_element_type=F32)
                            + lax.dot_general(g_im, wbdi[s], (NT, ((), ())), preferred_element_type=F32))
        dd_ref[...] += jnp.sum(dy * u, axis=0, keepdims=True)
        du = _time_order_rows(pm_ref[...], dup[...], 2) + d_ref[...] * dy
        dproj_ref[:, 0:P] = dpi_ref[...]
        dproj_ref[:, P:2 * P] = dpg_ref[...]
        dproj_ref[:, 2 * P:3 * P] = du.astype(BF16)
        dproj_ref[:, 3 * P:4 * P] = dsg_ref[...]

        @pl.when(i == nb - 1)
        def _():
            for a, h in ((abdr, dbdr_h), (abdi, dbdi_h), (acdr, dcdr_h), (acdi, dcdi_h)):
                pltpu.sync_copy(a, h)

    rev = lambda i: nb - 1 - i
    vec = lambda n: _bs((1, n), lambda i: (0, 0))
    row = _bs((tb, P), lambda i: (rev(i), 0))
    st_spec = _bs((None, SUBLANES, ns), lambda i: (rev(i), 0, 0))
    bshape = jax.ShapeDtypeStruct(bdr.shape, F32)
    cshape = jax.ShapeDtypeStruct(cdr.shape, F32)
    body, extra = _hosted(core, comm, (nb,), 16, 8, 21)
    return _call(body, grid=(nb,),
                 in_specs=[_bs((tb, P), lambda i: (rev(i), 2)), row, row, st_spec, st_spec,
                           vec(ns), vec(ns), vec(P), _bs((tb, tb), lambda i: (0, 0)), row, row, row,
                           ANY, ANY, ANY, ANY] + extra["in_specs"],
                 out_specs=[_bs((tb, 4 * P), lambda i: (rev(i), 0)), vec(ns), vec(ns), vec(P), ANY, ANY, ANY, ANY]
                 + extra["out_specs"],
                 out_shape=[jax.ShapeDtypeStruct((T, 4 * P), BF16), jax.ShapeDtypeStruct((1, ns), F32),
                            jax.ShapeDtypeStruct((1, ns), F32), jax.ShapeDtypeStruct((1, P), F32),
                            bshape, bshape, cshape, cshape] + extra["out_shape"],
                 scratch_shapes=[pltpu.VMEM(bdr.shape, BF16), pltpu.VMEM(bdr.shape, BF16),
                                 pltpu.VMEM(cdr.shape, BF16), pltpu.VMEM(cdr.shape, BF16),
                                 pltpu.VMEM(bdr.shape, F32), pltpu.VMEM(bdr.shape, F32),
                                 pltpu.VMEM(cdr.shape, F32), pltpu.VMEM(cdr.shape, F32),
                                 pltpu.VMEM((tb, ns), F32), pltpu.VMEM((tb, ns), F32),
                                 pltpu.VMEM((tb, ns), F32), pltpu.VMEM((tb, ns), F32),
                                 pltpu.VMEM((8, SUBLANES, ns), F32), pltpu.VMEM((8, SUBLANES, ns), F32)]
                 + [pltpu.VMEM((SUBLANES, ns), F32)] * 6 + [pltpu.VMEM((tb, P), F32)] + extra["scratch"],
                 compiler_params=_cp(("arbitrary",)), name="ssm_bwd")(
                     proj, y, dge, bsr, bsi, abr, abi, dsk, _seg_perm_matrix(tb), dpi, dpg, dsg,
                     bdr, bdi, cdr, cdi, *extra["ins"])


def _adamw(w, g, m, v, name, comm=None):
    R, C = w.shape
    tr = _t(R, 256)

    def core(w_ref, g_ref, m_ref, v_ref, d_ref, mo_ref, vo_ref):
        gv = g_ref[...]
        mn = ADAM_B1 * m_ref[...] + (1.0 - ADAM_B1) * gv
        vn = ADAM_B2 * v_ref[...] + (1.0 - ADAM_B2) * (gv * gv)
        m_hat = mn / (1.0 - ADAM_B1 ** ADAM_STEP)
        v_hat = vn / (1.0 - ADAM_B2 ** ADAM_STEP)
        d_ref[...] = -ADAM_LR * (m_hat / (jnp.sqrt(v_hat) + ADAM_EPS) + ADAM_WD * w_ref[...])
        mo_ref[...] = mn
        vo_ref[...] = vn

    blk = _bs((tr, C), lambda i: (i, 0))
    shp = jax.ShapeDtypeStruct((R, C), F32)
    body, extra = _hosted(core, comm, (R // tr,), 4, 3, 0)
    return _call(body, grid=(R // tr,), in_specs=[blk] * 4 + extra["in_specs"],
                 out_specs=[blk] * 3 + extra["out_specs"], out_shape=[shp] * 3 + extra["out_shape"],
                 scratch_shapes=extra["scratch"],
                 compiler_params=_cp(("arbitrary",) if comm else ("parallel",)), name=name)(w, g, m, v, *extra["ins"])


def _sum_cast(grad, got, place, name):
    J, H, C = got.shape
    tr = _t(H, 256)
    nb = H // tr

    def body(pl_ref, a_ref, b_ref, o_ref):
        o_ref[...] = (a_ref[...] + b_ref[...]).astype(BF16)

    blk = _bs((None, tr, C), lambda j, i, pc: (j, i, 0))
    mine = _bs((None, tr, C), lambda j, i, pc: (j, pc[1] * nb + i, 0))
    spec = pltpu.PrefetchScalarGridSpec(num_scalar_prefetch=1, grid=(J, nb), in_specs=[mine, blk], out_specs=blk)
    return _call(body, grid_spec=spec, out_shape=jax.ShapeDtypeStruct((J, H, C), BF16),
                 compiler_params=_cp(("parallel", "parallel")), name=name)(place, grad, got)


def _sum_chips(sent, arrived, place, name):
    J, H, C = arrived.shape
    tr = _t(H, 256)
    nb = H // tr

    def body(pl_ref, own_ref, a0_ref, a1_ref, a2_ref, o_ref):
        acc = own_ref[...].astype(F32)
        for r in (a0_ref, a1_ref, a2_ref):
            acc = acc + r[...].astype(F32)
        o_ref[...] = acc

    def other(k):
        return _bs((None, tr, C), lambda i, pc: (jnp.where(pc[0] <= k, k + 1, k), i, 0))

    spec = pltpu.PrefetchScalarGridSpec(
        num_scalar_prefetch=1, grid=(nb,),
        in_specs=[_bs((None, tr, C), lambda i, pc: (pc[0], i, 0)), other(0), other(1), other(2)],
        out_specs=_bs((tr, C), lambda i, pc: (pc[1] * nb + i, 0)))
    return _call(body, grid_spec=spec, out_shape=jax.ShapeDtypeStruct((2 * H, C), F32),
                 compiler_params=_cp(("parallel",)), name=name)(place, sent, arrived, arrived, arrived)


def _place():
    x, y, c = lax.axis_index("x"), lax.axis_index("y"), lax.axis_index("c")
    chips = [(1 - x, y), (x, 1 - y), (1 - x, 1 - y)]
    return x, y, c, chips


def _split(nrows, row_bytes, align, cap=None):
    k = max(1, min(cap or DMA_MAX_CHUNKS, (nrows * row_bytes) // DMA_CHUNK_BYTES))
    while k > 1 and nrows % (k * align):
        k -= 1
    return k


def _comm_call(plan, name):
    n_in, n_out = len(plan["ins"]), len(plan["out_shape"])

    def body(*refs):
        for phase in plan["phases"]:
            phase(refs[:n_in], refs[n_in:n_in + n_out], refs[n_in + n_out:])

    return _call(body, in_specs=[ANY] * n_in, out_specs=[ANY] * n_out, out_shape=plan["out_shape"],
                 input_output_aliases={i: i for i in range(n_out)} if plan.get("alias") else {},
                 scratch_shapes=plan["scratch"], name=name)(*plan["ins"])


def _comm_hooks(plan, grid, ins, outs, sems, *, before):
    if plan is None:
        return
    nsteps, step = 1, 0
    for d, g in enumerate(grid):
        nsteps, step = nsteps * g, step * g + pl.program_id(d)
    for p, (phase, frac) in enumerate(zip(plan["phases"], plan["at"])):
        if (p == 0) == before:
            pl.when(step == int(frac * (nsteps - 1)))(functools.partial(phase, ins, outs, sems))


def _ag_plan(shards, axes):
    n = len(shards)
    shapes = [a.shape for a in shards]

    def window(ref, i, chip, half=None):
        S, ax = shapes[i], axes[i]
        idx = []
        for d in range(len(S)):
            off, size = 0, S[d]
            if d == 0 and half is not None:
                off, size = half * (S[0] // 2), S[0] // 2
            if d == ax:
                off = off + chip * S[ax]
            idx.append(pl.ds(off, size))
        return ref.at[tuple(idx)]

    def copies(src, full, sems):
        ssem, rsem = sems
        x, y, c, chips = _place()
        me = 2 * x + y
        sib = (x, y, 1 - c)
        idx = [2 * cx + cy for cx, cy in chips]

        def rcopy(i, k, s_ref, d_ref, to):
            return pltpu.make_async_remote_copy(src_ref=s_ref, dst_ref=d_ref, send_sem=ssem.at[i, k],
                                                recv_sem=rsem.at[i, k], device_id=to, device_id_type=MESH)

        def ici(i, j, incoming):
            half_src = src[i].at[pl.ds(c * (shapes[i][0] // 2), shapes[i][0] // 2)]
            return rcopy(i, j, half_src, window(full[i], i, idx[j] if incoming else me, c), (*chips[j], c))

        def fwd(i, j, half):
            w = window(full[i], i, idx[j], half)
            return rcopy(i, 3 + j, w, w, sib)

        def own(i):
            return rcopy(i, 6, src[i], window(full[i], i, me), sib)

        return c, ici, fwd, own

    def send(src, full, sems):
        c, ici, fwd, own = copies(src, full, sems)
        for i in range(n):
            for j in range(3):
                ici(i, j, False).start()
        for i in range(n):
            own(i).start()

    def forward(i, src, full, sems):
        c, ici, fwd, own = copies(src, full, sems)
        for j in range(3):
            ici(i, j, True).wait_recv()
            fwd(i, j, c).start()

    def finish(src, full, sems):
        c, ici, fwd, own = copies(src, full, sems)
        for i in range(n):
            for j in range(3):
                fwd(i, j, 1 - c).wait_recv()
            own(i).wait()
        for i in range(n):
            for j in range(3):
                ici(i, j, False).wait_send()
                fwd(i, j, c).wait_send()

    out_shape = [jax.ShapeDtypeStruct(tuple(N_CHIP * d if k == ax else d for k, d in enumerate(S)), BF16)
                 for S, ax in zip(shapes, axes)]
    sizes = [a.size for a in shards]
    behind = [0.85 * sum(sizes[:i + 1]) / sum(sizes) + 0.05 for i in range(n)]
    return dict(ins=list(shards), out_shape=out_shape,
                phases=[send] + [functools.partial(forward, i) for i in range(n)] + [finish],
                at=[0.0] + behind + [1.0],
                scratch=[pltpu.SemaphoreType.DMA((n, 7)), pltpu.SemaphoreType.DMA((n, 7))])


def _proj_ag(x, g1, wsh, order, tm):
    T, D = x.shape
    P = wsh.shape[1]
    H = D // 2
    nt = T // tm

    def body(order_ref, x_ref, g_ref, wsh_ref, hn_ref, proj_ref, win_ref, wbuf, lsem, ssem, rsem):
        n, i = pl.program_id(0), pl.program_id(1)
        x, y, c, chips = _place()
        me = 2 * x + y
        sib = (x, y, 1 - c)
        idx = [2 * cx + cy for cx, cy in chips]

        def rcopy(k, s_ref, d_ref, to):
            return pltpu.make_async_remote_copy(src_ref=s_ref, dst_ref=d_ref, send_sem=ssem.at[k],
                                                recv_sem=rsem.at[k], device_id=to, device_id_type=MESH)

        def cols(chip):
            return pl.ds(pl.multiple_of(chip * P, LANES), P)

        def rows(half):
            return pl.ds(pl.multiple_of(half * H, 16), H)

        def ici(j, incoming):
            return rcopy(j, wsh_ref.at[rows(c)], win_ref.at[rows(c), cols(idx[j] if incoming else me)],
                         (*chips[j], c))

        def fwd(j, half):
            w = win_ref.at[rows(half), cols(idx[j])]
            return rcopy(3 + j, w, w, sib)

        def own():
            return rcopy(6, wsh_ref, win_ref.at[:, cols(me)], sib)

        def load(src):
            cp = pltpu.make_async_copy(src, wbuf, lsem)
            cp.start()
            cp.wait()

        @pl.when((n == 0) & (i == 0))
        def _():
            ici(0, False).start()
            ici(1, False).start()
            own().start()
            load(wsh_ref)

        for j in range(3):
            @pl.when((n == j + 1) & (i == 0))
            def _(j=j):
                if j == 0:
                    ici(2, False).start()
                ici(j, True).wait_recv()
                fwd(j, c).start()
                fwd(j, 1 - c).wait_recv()
                load(win_ref.at[:, cols(idx[j])])

        xv = x_ref[...]
        r = lax.rsqrt(jnp.mean(xv * xv, axis=-1, keepdims=True) + EPS)
        hn = ((xv * r) * g_ref[...]).astype(BF16)

        @pl.when(n == 0)
        def _():
            hn_ref[...] = hn

        proj_ref[...] = jnp.dot(hn, wbuf[...], preferred_element_type=F32)

        @pl.when((n == 3) & (i == nt - 1))
        def _():
            own().wait()
            for j in range(3):
                ici(j, False).wait_send()
                fwd(j, c).wait_send()

    spec = pltpu.PrefetchScalarGridSpec(
        num_scalar_prefetch=1, grid=(N_CHIP, nt),
        in_specs=[_bs((tm, D), lambda n, i, o: (i, 0)), _bs((1, D), lambda n, i, o: (0, 0)), ANY],
        out_specs=[_bs((tm, D), lambda n, i, o: (jnp.where(n == 0, i, nt - 1), 0)),
                   _bs((tm, P), lambda n, i, o: (i, o[n])), ANY],
        scratch_shapes=[pltpu.VMEM((D, P), BF16), pltpu.SemaphoreType.DMA,
                        pltpu.SemaphoreType.DMA((7,)), pltpu.SemaphoreType.DMA((7,))])
    return _call(body, grid_spec=spec,
                 out_shape=[jax.ShapeDtypeStruct((T, D), BF16), jax.ShapeDtypeStruct((T, N_CHIP * P), F32),
                            jax.ShapeDtypeStruct((D, N_CHIP * P), BF16)],
                 compiler_params=_cp(("arbitrary", "arbitrary")), name="proj_ag")(order, x, g1, wsh)


def _halves_plan(grads):
    n = len(grads)

    def send(g, got, sems):
        ssem, rsem = sems
        x, y, c, _ = _place()
        sib = (x, y, 1 - c)
        for i in range(n):
            J, R, C = g[i].shape
            H = R // 2
            size = g[i].dtype.itemsize
            tile_rows = SUBLANES * 4 // size
            k = _split(H, C * size, tile_rows, cap=DMA_MAX_CHUNKS // J)
            hr = H // k
            for j in range(J):
                for q in range(k):
                    other = pl.ds(pl.multiple_of((1 - c) * H + q * hr, tile_rows), hr)
                    to = pl.ds(q * hr, hr)
                    pltpu.make_async_remote_copy(src_ref=g[i].at[j, other, :], dst_ref=got[i].at[j, to, :],
                                                 send_sem=ssem.at[i], recv_sem=rsem.at[i],
                                                 device_id=sib, device_id_type=MESH).start()

    def finish(g, got, sems):
        ssem, rsem = sems
        x, y, c, _ = _place()
        for i in range(n):
            pltpu.make_async_remote_copy(src_ref=got[i], dst_ref=got[i], send_sem=ssem.at[i], recv_sem=rsem.at[i],
                                         device_id=(x, y, 1 - c), device_id_type=MESH).wait()

    half = [jax.ShapeDtypeStruct((a.shape[0], a.shape[1] // 2, a.shape[2]), a.dtype) for a in grads]
    return dict(ins=list(grads), out_shape=half, phases=[send, finish], at=[0.0, 1.0],
                scratch=[pltpu.SemaphoreType.DMA((n,)), pltpu.SemaphoreType.DMA((n,))])


def _scatter_plan(parts):
    n = len(parts)

    def peers():
        x, y, c, chips = _place()
        return 2 * x + y, c, chips, [2 * cx + cy for cx, cy in chips]

    def send(s, got, sems):
        ssem, rsem = sems
        me, c, chips, idx = peers()
        for i in range(n):
            _, H, C = s[i].shape
            k = _split(H, C * 2, 16, cap=RS_CHUNKS)
            hr = H // k
            for q in range(k):
                rows = pl.ds(q * hr, hr)
                for j in range(3):
                    pltpu.make_async_remote_copy(src_ref=s[i].at[idx[j], rows, :], dst_ref=got[i].at[me, rows, :],
                                                 send_sem=ssem.at[i, j], recv_sem=rsem.at[i, j],
                                                 device_id=(*chips[j], c), device_id_type=MESH).start()

    def finish(s, got, sems):
        ssem, rsem = sems
        me, c, chips, idx = peers()
        for i in range(n):
            for j in range(3):
                pltpu.make_async_remote_copy(src_ref=s[i].at[idx[j]], dst_ref=got[i].at[idx[j]],
                                             send_sem=ssem.at[i, j], recv_sem=rsem.at[i, j],
                                             device_id=(*chips[j], c), device_id_type=MESH).wait()

    return dict(ins=list(parts), out_shape=[jax.ShapeDtypeStruct(a.shape, a.dtype) for a in parts],
                phases=[send, finish], at=[0.0, 1.0],
                scratch=[pltpu.SemaphoreType.DMA((n, 3)), pltpu.SemaphoreType.DMA((n, 3))])


def _join_plan(shards):
    n = len(shards)

    def send(_, full, sems):
        ssem, rsem = sems
        x, y, c, _ = _place()
        sib = (x, y, 1 - c)
        for i in range(n):
            H, C = full[i].shape[0] // 2, full[i].shape[1]
            k = _split(H, C * 4, SUBLANES)
            hr = H // k
            for q in range(k):
                rows = pl.ds(pl.multiple_of(c * H + q * hr, SUBLANES), hr)
                pltpu.make_async_remote_copy(src_ref=full[i].at[rows], dst_ref=full[i].at[rows],
                                             send_sem=ssem.at[i], recv_sem=rsem.at[i],
                                             device_id=sib, device_id_type=MESH).start()

    def finish(_, full, sems):
        ssem, rsem = sems
        x, y, c, _ = _place()
        for i in range(n):
            half = full[i].at[pl.ds(0, full[i].shape[0] // 2)]
            pltpu.make_async_remote_copy(src_ref=half, dst_ref=half, send_sem=ssem.at[i], recv_sem=rsem.at[i],
                                         device_id=(x, y, 1 - c), device_id_type=MESH).wait()

    return dict(ins=list(shards), out_shape=[jax.ShapeDtypeStruct(a.shape, a.dtype) for a in shards],
                phases=[send, finish], at=[0.0, 1.0], alias=True,
                scratch=[pltpu.SemaphoreType.DMA((n,)), pltpu.SemaphoreType.DMA((n,))])


def _allreduce_plan(buf):
    R, L = buf.shape
    RB = R // N_DEV

    def parts(sems):
        xv, got, ov, lsem, ssem, rsem = sems
        x, y, c, _ = _place()
        me = 4 * x + 2 * y + c

        def dev(k):
            return (k // 4, (k // 2) % 2, k % 2)

        def slab(k):
            return pl.ds(pl.multiple_of(k * RB, SUBLANES), RB)

        def first(d, to, landing):
            return pltpu.make_async_remote_copy(src_ref=xv.at[slab(to)], dst_ref=got.at[landing],
                                                send_sem=ssem.at[0, d], recv_sem=rsem.at[0, d],
                                                device_id=dev(to), device_id_type=MESH)

        def second(d, to, k):
            return pltpu.make_async_remote_copy(src_ref=ov.at[slab(k)], dst_ref=ov.at[slab(k)],
                                                send_sem=ssem.at[1, d], recv_sem=rsem.at[1, d],
                                                device_id=dev(to), device_id_type=MESH)

        return me, slab, first, second

    def scatter(ins, outs, sems):
        xv, lsem = sems[0], sems[3]
        me, slab, first, second = parts(sems)
        cp = pltpu.make_async_copy(ins[0], xv, lsem)
        cp.start()
        cp.wait()
        for d in range(1, N_DEV):
            first(d, (me + d) % N_DEV, me).start()

    def reduce(ins, outs, sems):
        xv, got, ov = sems[:3]
        me, slab, first, second = parts(sems)
        got[me] = xv[slab(me), :]
        for d in range(1, N_DEV):
            src = (me + N_DEV - d) % N_DEV
            first(d, src, src).wait_recv()
        acc = got[0]
        for k in range(1, N_DEV):
            acc = acc + got[k]
        ov[slab(me), :] = acc
        for d in range(1, N_DEV):
            second(d, (me + d) % N_DEV, me).start()

    def collect(ins, outs, sems):
        ov, lsem = sems[2], sems[3]
        me, slab, first, second = parts(sems)
        for d in range(1, N_DEV):
            src = (me + N_DEV - d) % N_DEV
            second(d, src, src).wait_recv()
        for d in range(1, N_DEV):
            peer = (me + d) % N_DEV
            first(d, peer, me).wait_send()
            second(d, peer, me).wait_send()
        cp = pltpu.make_async_copy(ov, outs[0], lsem)
        cp.start()
        cp.wait()

    return dict(ins=[buf], out_shape=[jax.ShapeDtypeStruct((R, L), F32)], phases=[scatter, reduce, collect],
                at=[0.0, 0.5, 1.0],
                scratch=[pltpu.VMEM((R, L), F32), pltpu.VMEM((N_DEV, RB, L), F32), pltpu.VMEM((R, L), F32),
                         pltpu.SemaphoreType.DMA, pltpu.SemaphoreType.DMA((2, N_DEV)),
                         pltpu.SemaphoreType.DMA((2, N_DEV))])


def _block_diag(t, gt):
    G, A, B = t.shape
    t4 = t.reshape(G // gt, gt, A, B)
    eye = jnp.eye(gt, dtype=t.dtype)
    return jnp.einsum('sgab,gh->sgahb', t4, eye).reshape(G // gt, gt * A, gt * B)


def _block_diag_extract(m, gt, A, B):
    S = m.shape[0]
    m5 = m.reshape(S, gt, A, gt, B)
    eye = jnp.eye(gt, dtype=m.dtype)
    return jnp.einsum('sgahb,gh->sgab', m5, eye).reshape(S * gt, A, B)


def _pack_small(arrs, rows):
    flat = jnp.concatenate([a.reshape(-1).astype(F32) for a in arrs])
    return jnp.pad(flat, (0, rows * LANES - flat.shape[0])).reshape(rows, LANES)


def _unpack_small(buf, shapes):
    flat = buf.reshape(-1)
    out, off = [], 0
    for s in shapes:
        n = 1
        for d in s:
            n *= d
        out.append(flat[off:off + n].reshape(s))
        off += n
    return out


def kernel(x, p, norm_gain, w_in, w_pool, pool_scale, a_re, a_im, log_dt, b_re, b_im, c_re, c_im, d_skip, w_glu, w_out, w_ple, w_ple_gate, final_gain, loss_target, m_norm_gain, m_w_in, m_w_pool, m_pool_scale, m_a_re, m_a_im, m_log_dt, m_b_re, m_b_im, m_c_re, m_c_im, m_d_skip, m_w_glu, m_w_out, m_w_ple, m_w_ple_gate, m_final_gain, v_norm_gain, v_w_in, v_w_pool, v_pool_scale, v_a_re, v_a_im, v_log_dt, v_b_re, v_b_im, v_c_re, v_c_im, v_d_skip, v_w_glu, v_w_out, v_w_ple, v_w_ple_gate, v_final_gain):
    xs, pe, tgt = x[0], p[0, 0], loss_target[0]
    T, D = xs.shape
    E = pe.shape[1]
    P = D // 2
    NG = len(POOL_WINDOWS)
    PG = P // NG
    G, N, C = P // SSM_GROUP, SSM_STATE, SSM_GROUP
    GT = min(SSM_TILE_GROUPS, G)
    Q = D // N_CHIP

    big = {"w_in": (w_in, m_w_in, v_w_in), "w_pool": (w_pool, m_w_pool, v_w_pool),
           "w_glu": (w_glu, m_w_glu, v_w_glu), "w_out": (w_out, m_w_out, v_w_out),
           "w_ple": (w_ple, m_w_ple, v_w_ple), "w_ple_gate": (w_ple_gate, m_w_ple_gate, v_w_ple_gate)}
    big_names = list(big)
    shard2d = {n: (big[n][0].size // big[n][0].shape[-1], big[n][0].shape[-1]) for n in big_names}
    shard_axis = {"w_in": 1, "w_pool": 1, "w_glu": 1, "w_out": 0, "w_ple": 1, "w_ple_gate": 0}
    shard16 = {n: big[n][0][0].astype(BF16) for n in big_names}
    place = jnp.stack([2 * lax.axis_index("x") + lax.axis_index("y"), lax.axis_index("c")]).astype(jnp.int32)
    mx, my = lax.axis_index("x"), lax.axis_index("y")
    block_order = jnp.stack([2 * mx + my, 2 * (1 - mx) + my, 2 * mx + (1 - my),
                             2 * (1 - mx) + (1 - my)]).astype(jnp.int32)
    later = [n for n in big_names if n != "w_in"]
    ag_later = _ag_plan([shard16[n] for n in later], [shard_axis[n] for n in later])

    rep = lambda a: jnp.repeat(a, C, axis=0)
    a_re_r, a_im_r = rep(a_re[0]), rep(a_im[0])
    ldt_r = rep(jnp.broadcast_to(log_dt[0][:, None], (G, N)))
    bt_re = b_re[0].transpose(0, 2, 1).reshape(G * C, N)
    bt_im = b_im[0].transpose(0, 2, 1).reshape(G * C, N)
    ab_re_r, ab_im_r, bbt_re, bbt_im = _ssm_prep(a_re_r, a_im_r, ldt_r, bt_re, bt_im)
    abr = ab_re_r[::C].reshape(1, G * N)
    abi = ab_im_r[::C].reshape(1, G * N)
    bdr = _block_diag(bbt_re.reshape(G, C, N), GT).astype(BF16)
    bdi = _block_diag(bbt_im.reshape(G, C, N), GT).astype(BF16)
    cdr = _block_diag(c_re[0].transpose(0, 2, 1), GT).astype(BF16)
    cdi = _block_diag(c_im[0].transpose(0, 2, 1), GT).astype(BF16)

    tb = _t(T, 256)
    tbs = _t(T, 256)
    tm = _t(T, 1024)
    tk = _t(T, 2048)
    DH = _t(D, 1024)
    row_k = lambda i, n, k: (i, k)
    row_n = lambda i, n, k: (i, n)
    f32 = lambda *shape: jax.ShapeDtypeStruct(shape, F32)
    hn, proj, win = _proj_ag(xs, norm_gain, shard16["w_in"], block_order, tm)
    y, ge, bsr, bsi, wp, wglu, wout, wple, wpg = _ssm_fwd(proj, bdr, bdi, cdr, cdi, abr, abi, d_skip, P, tbs,
                                                          comm=ag_later)
    pooled, mixed = _pool_fwd(proj, wp, P, tb)
    hg = _mm(ge, wglu, dims=NN, grid=(T // tm, 2 * P // DH, 1),
             a_spec=_bs((tm, P), row_k), b_spec=_bs((P, DH), lambda i, n, k: (k, n)),
             o_spec=_bs((tm, DH), row_n), out_shape=f32(T, 2 * P), name="mm_glu")
    cat = _gate_fwd(mixed, proj, hg, pool_scale, tb)
    h1, h1b = _mm(cat, wout, dims=NN, grid=(T // tm, D // DH, 1), res=xs, bf16_copy=True,
                  a_spec=_bs((tm, D), row_k), b_spec=_bs((D, DH), lambda i, n, k: (k, n)),
                  r_spec=_bs((tm, DH), row_n), o_spec=_bs((tm, DH), row_n), out_shape=f32(T, D), name="mm_out")
    z = _mm(h1b, wpg, dims=NN, grid=(T // tm, D // DH, 1),
            a_spec=_bs((tm, D), row_k), b_spec=_bs((D, DH), lambda i, n, k: (k, n)),
            o_spec=_bs((tm, DH), row_n), out_shape=f32(T, D), name="mm_pgate")
    dh2, de, dz, dg2, lpart = _final_fb(h1, pe, wple, z, tgt, final_gain.reshape(1, D), tb)

    col_m = lambda m, n, k: (k, m)
    col_n = lambda m, n, k: (k, n)
    dh1, dh1b = _mm(dz, wpg, dims=NT, grid=(T // tm, D // DH, 1), res=dh2, bf16_copy=True,
                    a_spec=_bs((tm, D), row_k), b_spec=_bs((DH, D), lambda i, n, k: (n, k)),
                    r_spec=_bs((tm, DH), row_n), o_spec=_bs((tm, DH), row_n), out_shape=f32(T, D), name="mm_dh1")
    g_wpg, g_wpg16 = _mm(h1b, dz, dims=TN, grid=(D // DH, D // DH, T // tk), bf16_copy=True,
                         a_spec=_bs((tk, DH), col_m), b_spec=_bs((tk, DH), col_n),
                         o_spec=_bs((DH, DH), lambda m, n, k: (m, n)), out_shape=f32(D, D), name="mm_gwpg")
    g_wple, g_wple16 = _mm(pe, de, dims=TN, grid=(1, N_CHIP, T // tk), bf16_copy=True,
                           a_spec=_bs((tk, E), col_m), b_spec=_bs((tk, Q), col_n),
                           o_spec=_bs((None, E, Q), lambda m, j, k: (j, 0, 0)), out_shape=f32(N_CHIP, E, Q),
                           name="mm_gwple")
    dcat = _mm(dh1b, wout, dims=NT, grid=(T // tm, D // DH, 1),
               a_spec=_bs((tm, D), row_k), b_spec=_bs((DH, D), lambda i, n, k: (n, k)),
               o_spec=_bs((tm, DH), row_n), out_shape=f32(T, D), name="mm_dcat")
    g_wout, g_wout16 = _mm(cat, dh1b, dims=TN, grid=(D // DH, D // DH, T // tk), bf16_copy=True,
                           a_spec=_bs((tk, DH), col_m), b_spec=_bs((tk, DH), col_n),
                           o_spec=_bs((DH, DH), lambda m, n, k: (m, n)), out_shape=f32(D, D), name="mm_gwout")
    gbig = {"w_out": g_wout.reshape(N_CHIP, Q, D), "w_ple": g_wple, "w_ple_gate": g_wpg.reshape(N_CHIP, Q, D)}
    gbig16 = {"w_out": g_wout16.reshape(N_CHIP, Q, D), "w_ple": g_wple16,
              "w_ple_gate": g_wpg16.reshape(N_CHIP, Q, D)}
    first = list(gbig)
    res = _gate_bwd(dcat, mixed, proj, hg, pool_scale, tb, comm=_halves_plan([gbig16[n] for n in first]))
    dmixed, dpg, dsg, dhg, dps = res[:5]
    got = dict(zip(first, res[5:]))
    dge = _mm(dhg, wglu, dims=NT, grid=(T // tm, 1, 1),
              a_spec=_bs((tm, 2 * P), row_k), b_spec=_bs((P, 2 * P), lambda i, n, k: (n, k)),
              o_spec=_bs((tm, P), row_n), out_shape=f32(T, P), name="mm_dge")
    gbig["w_glu"], gbig16["w_glu"] = _mm(ge, dhg, dims=TN, grid=(1, N_CHIP, T // tk), bf16_copy=True,
                                         a_spec=_bs((tk, P), col_m), b_spec=_bs((tk, Q), col_n),
                                         o_spec=_bs((None, P, Q), lambda m, j, k: (j, 0, 0)),
                                         out_shape=f32(N_CHIP, P, Q), name="mm_gwglu")
    g_wp = _mm(pooled, dmixed, dims=TN, grid=(NG, 1, T // tk), bf16_copy=True,
               a_spec=_bs((tk, PG), col_m), b_spec=_bs((tk, PG), col_m),
               o_spec=_bs((None, PG, PG), lambda g, n, k: (g, 0, 0)), out_shape=f32(NG, PG, PG), name="mm_gwp")
    by_chip = lambda a: a.reshape(NG, N_CHIP, PG // N_CHIP, PG).transpose(1, 0, 2, 3).reshape(
        N_CHIP, NG * PG // N_CHIP, PG)
    gbig["w_pool"], gbig16["w_pool"] = by_chip(g_wp[0]), by_chip(g_wp[1])
    res = _pool_bwd(dmixed, wp, tb, comm=_halves_plan([gbig16["w_pool"], gbig16["w_glu"]]))
    dpi, got["w_pool"], got["w_glu"] = res
    early = list(gbig)
    chip_sums = {n: _sum_cast(gbig[n], got[n], place, "sum_cast_" + n) for n in early}
    res = _ssm_bwd(proj, y, dge, bsr, bsi, bdr, bdi, cdr, cdi, abr, abi, d_skip, dpi, dpg, dsg, P, tbs,
                   comm=_scatter_plan([chip_sums[n] for n in early]))
    dproj, dabr, dabi, dd, dbdr, dbdi, dcdr, dcdi = res[:8]
    arrived = dict(zip(early, res[8:]))
    halves = [_sum_chips(chip_sums[n], arrived[n], place, "sum_chips_" + n) for n in early]
    res = _mm(hn, dproj, dims=TN, grid=(D // DH, N_CHIP, T // tk), bf16_copy=True,
              a_spec=_bs((tk, DH), col_m), b_spec=_bs((tk, P), col_n),
              o_spec=_bs((None, DH, P), lambda m, j, k: (j, m, 0)), out_shape=f32(N_CHIP, D, P),
              name="mm_gwin", comm=_join_plan(halves))
    gbig["w_in"], gbig16["w_in"], gshard = res[0], res[1], dict(zip(early, res[2:]))
    got["w_in"], = _comm_call(_halves_plan([gbig16["w_in"]]), "rs_halves_late")
    chip_sums["w_in"] = _sum_cast(gbig["w_in"], got["w_in"], place, "sum_cast_w_in")
    KH = _t(4 * P, 2048)
    dhn, arrived["w_in"] = _mm(dproj, win, dims=NT, grid=(T // tm, D // DH, 4 * P // KH),
                               a_spec=_bs((tm, KH), row_k), b_spec=_bs((DH, KH), lambda i, n, k: (n, k)),
                               o_spec=_bs((tm, DH), row_n), out_shape=f32(T, D), name="mm_dhn",
                               comm=_scatter_plan([chip_sums["w_in"]]))
    gshard["w_in"], = _comm_call(
        _join_plan([_sum_chips(chip_sums["w_in"], arrived["w_in"], place, "sum_chips_w_in")]), "rs_join_w_in")
    grad_x, dg1 = _norm1_bwd(xs, dhn, dh1, norm_gain, tb)

    dbbt_re = _block_diag_extract(dbdr, GT, C, N).reshape(G * C, N)
    dbbt_im = _block_diag_extract(dbdi, GT, C, N).reshape(G * C, N)
    g_c_re = _block_diag_extract(dcdr, GT, N, C).transpose(0, 2, 1)
    g_c_im = _block_diag_extract(dcdi, GT, N, C).transpose(0, 2, 1)
    dab_re_r = rep(dabr.reshape(G, N)) * (1.0 / C)
    dab_im_r = rep(dabi.reshape(G, N)) * (1.0 / C)
    g_a_re, g_a_im, g_ldt, g_bt_re, g_bt_im = _ssm_prep_bwd(a_re_r, a_im_r, ldt_r, bt_re, bt_im,
                                                            dab_re_r, dab_im_r, dbbt_re, dbbt_im, G)
    g_b_re = g_bt_re.reshape(G, C, N).transpose(0, 2, 1)
    g_b_im = g_bt_im.reshape(G, C, N).transpose(0, 2, 1)


    small_names = ["norm_gain", "pool_scale", "a_re", "a_im", "log_dt", "b_re", "b_im", "c_re", "c_im",
                   "d_skip", "final_gain"]
    small_w = dict(norm_gain=norm_gain, pool_scale=pool_scale, a_re=a_re, a_im=a_im, log_dt=log_dt, b_re=b_re,
                   b_im=b_im, c_re=c_re, c_im=c_im, d_skip=d_skip, final_gain=final_gain)
    small_m = dict(norm_gain=m_norm_gain, pool_scale=m_pool_scale, a_re=m_a_re, a_im=m_a_im, log_dt=m_log_dt,
                   b_re=m_b_re, b_im=m_b_im, c_re=m_c_re, c_im=m_c_im, d_skip=m_d_skip, final_gain=m_final_gain)
    small_v = dict(norm_gain=v_norm_gain, pool_scale=v_pool_scale, a_re=v_a_re, a_im=v_a_im, log_dt=v_log_dt,
                   b_re=v_b_re, b_im=v_b_im, c_re=v_c_re, c_im=v_c_im, d_skip=v_d_skip, final_gain=v_final_gain)
    small_g = dict(norm_gain=dg1, pool_scale=dps, a_re=g_a_re, a_im=g_a_im, log_dt=g_ldt, b_re=g_b_re,
                   b_im=g_b_im, c_re=g_c_re, c_im=g_c_im, d_skip=dd, final_gain=dg2)
    shapes = [small_w[n].shape for n in small_names]
    total = sum(small_w[n].size for n in small_names) + 1
    unit = N_DEV * SUBLANES
    rows = -(-(-(-total // LANES)) // unit) * unit
    gbuf = _pack_small([small_g[n] for n in small_names] + [lpart[0, :1]], rows)
    gsum, = _comm_call(_allreduce_plan(gbuf), "allreduce_small")
    g_out, d_out, m_out, v_out = {}, {}, {}, {}
    for n in big_names:
        w_, m_, v_ = big[n]
        r2 = shard2d[n]
        res = _adamw(w_.reshape(r2), gshard[n], m_.reshape(r2), v_.reshape(r2), "adamw_" + n)
        g_out[n], d_out[n], m_out[n], v_out[n] = (a.reshape(w_.shape) for a in (gshard[n], *res))
    wbuf = _pack_small([small_w[n] for n in small_names], rows)
    mbuf = _pack_small([small_m[n] for n in small_names], rows)
    vbuf = _pack_small([small_v[n] for n in small_names], rows)
    dsm, msm, vsm = _adamw(wbuf, gsum, mbuf, vbuf, "adamw_small")
    g_small = dict(zip(small_names, _unpack_small(gsum, shapes)))
    d_small = dict(zip(small_names, _unpack_small(dsm, shapes)))
    m_small = dict(zip(small_names, _unpack_small(msm, shapes)))
    v_small = dict(zip(small_names, _unpack_small(vsm, shapes)))
    loss = gsum.reshape(-1)[total - 1]

    g_out.update(g_small)
    d_out.update(d_small)
    m_out.update(m_small)
    v_out.update(v_small)

    order = ["norm_gain", "w_in", "w_pool", "pool_scale", "a_re", "a_im", "log_dt", "b_re", "b_im", "c_re",
             "c_im", "d_skip", "w_glu", "w_out", "w_ple", "w_ple_gate", "final_gain"]
    return (loss, grad_x[None], *[g_out[n] for n in order], *[d_out[n] for n in order],
            *[m_out[n] for n in order], *[v_out[n] for n in order])
```

```python
import functools

import jax
import jax.numpy as jnp
from jax import lax
from jax.experimental import pallas as pl
from jax.experimental.pallas import tpu as pltpu

F32, BF16 = jnp.float32, jnp.bfloat16
MESH = pl.DeviceIdType.MESH
ANY = pl.BlockSpec(memory_space=pl.ANY)
VMEM_FULL = pl.BlockSpec(memory_space=pltpu.VMEM)

EPS = 1e-6
A_RE_MAX = -1e-4
SSM_GROUP = 16
SSM_STATE = 64
POOL_WINDOWS = (2, 4, 8, 16)
POOL_HALO = 16
ADAM_LR, ADAM_B1, ADAM_B2, ADAM_EPS, ADAM_WD, ADAM_STEP = 0.001, 0.9, 0.999, 1e-08, 0.01, 10

V7X_VMEM_BYTES = 64 * 1024 * 1024
VMEM_LIMIT = V7X_VMEM_BYTES - 8 * 1024 * 1024
SUBLANES, LANES = 8, 128
SSM_TILE_GROUPS = 8
SCAN_LANES = 512
N_DEV, N_CHIP = 8, 4
DMA_CHUNK_BYTES = 256 * 1024
DMA_MAX_CHUNKS = 32
AG_CHUNKS = 8
RS_CHUNKS = 8


def _t(n, pref):
    return pref if n % pref == 0 else n


def _cp(sem=None, vmem=VMEM_LIMIT):
    return pltpu.CompilerParams(dimension_semantics=sem, vmem_limit_bytes=vmem)


def _call(body, **kw):
    return pl.pallas_call(body, **kw)


NN = ((1,), (0,))
NT = ((1,), (1,))
TN = ((0,), (0,))


def _mm(a, b, *, dims, grid, a_spec, b_spec, o_spec, out_shape, name, res=None, r_spec=None, bf16_copy=False,
        comm=None):
    nk, kax = grid[-1], len(grid) - 1
    acc_shape = tuple(d for d in o_spec.block_shape if d is not None)

    def core(*refs):
        refs = list(refs)
        a_ref, b_ref = refs[:2]
        r_ref = refs[2] if res is not None else None
        outs = refs[3 if res is not None else 2:]
        o_ref = outs[0]
        o2_ref = outs[1] if bf16_copy else None
        acc = outs[-1] if nk > 1 else None

        def finish(r):
            if r_ref is not None:
                r = r + r_ref[...]
            o_ref[...] = r.astype(o_ref.dtype)
            if o2_ref is not None:
                o2_ref[...] = r.astype(BF16)

        part = lax.dot_general(a_ref[...].astype(BF16), b_ref[...].astype(BF16),
                               (dims, ((), ())), preferred_element_type=F32)
        if nk == 1:
            finish(part)
        else:
            k = pl.program_id(kax)

            @pl.when(k == 0)
            def _():
                acc[...] = part

            @pl.when(k > 0)
            def _():
                acc[...] += part

            @pl.when(k == nk - 1)
            def _():
                finish(acc[...])

    ins, specs = [a, b], [a_spec, b_spec]
    if res is not None:
        ins.append(res)
        specs.append(r_spec)
    o_specs, o_shapes = [o_spec], [out_shape]
    if bf16_copy:
        o_specs = [o_spec, o_spec]
        o_shapes = [out_shape, jax.ShapeDtypeStruct(out_shape.shape, BF16)]
    scratch = [pltpu.VMEM(acc_shape, F32)] if nk > 1 else []
    body, extra = _hosted(core, comm, grid, len(ins), len(o_specs), len(scratch))
    sem = ("arbitrary",) * len(grid) if comm else ("parallel",) * kax + ("arbitrary",)
    outs = _call(body, grid=grid, in_specs=specs + extra["in_specs"], out_specs=o_specs + extra["out_specs"],
                 out_shape=o_shapes + extra["out_shape"], scratch_shapes=scratch + extra["scratch"],
                 input_output_aliases=extra["aliases"],
                 compiler_params=_cp(sem), name=name)(*ins, *extra["ins"])
    return outs[0] if len(outs) == 1 else outs


def _bs(shape, fn):
    return pl.BlockSpec(shape, fn)


def _sigmoid(v):
    return 1.0 / (1.0 + jnp.exp(-v))


def _gelu(v):
    return 0.5 * v * (1.0 + jnp.tanh(0.7978845608028654 * (v + 0.044715 * v * v * v)))


def _gelu_grad(v):
    t = jnp.tanh(0.7978845608028654 * (v + 0.044715 * v * v * v))
    return 0.5 * (1.0 + t) + 0.5 * v * (1.0 - t * t) * 0.7978845608028654 * (1.0 + 3 * 0.044715 * v * v)


def _norm1_bwd(x, dhn, dh1, g1, tb, comm=None):
    T, D = x.shape

    def core(x_ref, dhn_ref, dh1_ref, g_ref, dx_ref, dg_ref):
        @pl.when(pl.program_id(0) == 0)
        def _():
            dg_ref[...] = jnp.zeros_like(dg_ref)

        xv = x_ref[...]
        r = lax.rsqrt(jnp.mean(xv * xv, axis=-1, keepdims=True) + EPS)
        xh = xv * r
        dhn_v = dhn_ref[...]
        dg_ref[...] += jnp.sum(dhn_v * xh, axis=0, keepdims=True)
        dxh = dhn_v * g_ref[...]
        dx_ref[...] = dh1_ref[...] + r * (dxh - xh * jnp.mean(dxh * xh, axis=-1, keepdims=True))

    row = _bs((tb, D), lambda i: (i, 0))
    vec = _bs((1, D), lambda i: (0, 0))
    body, extra = _hosted(core, comm, (T // tb,), 4, 2, 0)
    return _call(body, grid=(T // tb,), in_specs=[row, row, row, vec] + extra["in_specs"],
                 out_specs=[row, vec] + extra["out_specs"],
                 out_shape=[jax.ShapeDtypeStruct((T, D), F32), jax.ShapeDtypeStruct((1, D), F32)] + extra["out_shape"],
                 scratch_shapes=extra["scratch"], input_output_aliases=extra["aliases"],
                 compiler_params=_cp(("arbitrary",)), name="norm1_bwd")(x, dhn, dh1, g1, *extra["ins"])


def _gate_fwd(mixed, proj, hg, ps, tb):
    T, P = mixed.shape

    def body(mx_ref, pg_ref, sg_ref, hg_ref, ps_ref, o_ref):
        pg, sg = pg_ref[...], sg_ref[...]
        ya = (mx_ref[...] * ps_ref[...]) * (pg * _sigmoid(pg))
        hgv = hg_ref[...]
        o = hgv[:, :P] * _sigmoid(hgv[:, P:])
        yb = o * (sg * _sigmoid(sg))
        o_ref[:, :P] = ya.astype(BF16)
        o_ref[:, P:] = yb.astype(BF16)

    return _call(body, grid=(T // tb,),
                 in_specs=[_bs((tb, P), lambda i: (i, 0)), _bs((tb, P), lambda i: (i, 1)),
                           _bs((tb, P), lambda i: (i, 3)), _bs((tb, 2 * P), lambda i: (i, 0)),
                           _bs((1, P), lambda i: (0, 0))],
                 out_specs=_bs((tb, 2 * P), lambda i: (i, 0)),
                 out_shape=jax.ShapeDtypeStruct((T, 2 * P), BF16),
                 compiler_params=_cp(("parallel",)), name="gate_fwd")(mixed, proj, proj, hg, ps)


def _gate_bwd(dcat, mixed, proj, hg, ps, tb, comm=None):
    T, P = mixed.shape

    def core(dc_ref, mx_ref, pg_ref, sg_ref, hg_ref, ps_ref, dmx_ref, dpg_ref, dsg_ref, dhg_ref, dps_ref):
        @pl.when(pl.program_id(0) == 0)
        def _():
            dps_ref[...] = jnp.zeros_like(dps_ref)

        dc = dc_ref[...]
        dya, dyb = dc[:, :P], dc[:, P:]
        pg, sg, mx, psv = pg_ref[...], sg_ref[...], mx_ref[...], ps_ref[...]
        s_pg = _sigmoid(pg)
        dpa = dya * (pg * s_pg)
        dpg_ref[...] = (dya * (mx * psv) * (s_pg * (1.0 + pg * (1.0 - s_pg)))).astype(BF16)
        dps_ref[...] += jnp.sum(dpa * mx, axis=0, keepdims=True)
        dmx_ref[...] = (dpa * psv).astype(BF16)
        hgv = hg_ref[...]
        h1, s_h2 = hgv[:, :P], _sigmoid(hgv[:, P:])
        s_sg = _sigmoid(sg)
        do = dyb * (sg * s_sg)
        dsg_ref[...] = (dyb * (h1 * s_h2) * (s_sg * (1.0 + sg * (1.0 - s_sg)))).astype(BF16)
        dhg_ref[:, :P] = (do * s_h2).astype(BF16)
        dhg_ref[:, P:] = (do * h1 * s_h2 * (1.0 - s_h2)).astype(BF16)

    rowp = _bs((tb, P), lambda i: (i, 0))
    row2 = _bs((tb, 2 * P), lambda i: (i, 0))
    vec = _bs((1, P), lambda i: (0, 0))
    body, extra = _hosted(core, comm, (T // tb,), 6, 5, 0)
    return _call(body, grid=(T // tb,),
                 in_specs=[row2, rowp, _bs((tb, P), lambda i: (i, 1)), _bs((tb, P), lambda i: (i, 3)), row2, vec]
                 + extra["in_specs"],
                 out_specs=[rowp, rowp, rowp, row2, vec] + extra["out_specs"],
                 out_shape=[jax.ShapeDtypeStruct((T, P), BF16), jax.ShapeDtypeStruct((T, P), BF16),
                            jax.ShapeDtypeStruct((T, P), BF16), jax.ShapeDtypeStruct((T, 2 * P), BF16),
                            jax.ShapeDtypeStruct((1, P), F32)] + extra["out_shape"],
                 scratch_shapes=extra["scratch"],
                 compiler_params=_cp(("arbitrary",)), name="gate_bwd")(dcat, mixed, proj, proj, hg, ps, *extra["ins"])


def _final_fb(h1, pe, wple, z, tgt, g2, tb):
    T, D = h1.shape
    E = pe.shape[1]

    def body(h1_ref, p_ref, w_ref, z_ref, t_ref, g_ref, dh2_ref, de_ref, dz_ref, dg_ref, l_ref):
        @pl.when(pl.program_id(0) == 0)
        def _():
            dg_ref[...] = jnp.zeros_like(dg_ref)
            l_ref[...] = jnp.zeros_like(l_ref)

        ev = jnp.dot(p_ref[...].astype(BF16), w_ref[...], preferred_element_type=F32)
        s = _sigmoid(z_ref[...])
        h2 = h1_ref[...] + ev * s
        r = lax.rsqrt(jnp.mean(h2 * h2, axis=-1, keepdims=True) + EPS)
        xh = h2 * r
        gv = g_ref[...]
        diff = xh * gv - t_ref[...]
        l_ref[...] += 0.5 * jnp.sum(jnp.mean(diff * diff, axis=-1, keepdims=True))
        dout = diff * (1.0 / D)
        dg_ref[...] += jnp.sum(dout * xh, axis=0, keepdims=True)
        dxh = dout * gv
        dh2 = r * (dxh - xh * jnp.mean(dxh * xh, axis=-1, keepdims=True))
        dh2_ref[...] = dh2
        de_ref[...] = (dh2 * s).astype(BF16)
        dz_ref[...] = (dh2 * ev * s * (1.0 - s)).astype(BF16)

    row = _bs((tb, D), lambda i: (i, 0))
    vec = _bs((1, D), lambda i: (0, 0))
    return _call(body, grid=(T // tb,),
                 in_specs=[row, _bs((tb, E), lambda i: (i, 0)), _bs((E, D), lambda i: (0, 0)), row, row, vec],
                 out_specs=[row, row, row, vec, _bs((1, LANES), lambda i: (0, 0))],
                 out_shape=[jax.ShapeDtypeStruct((T, D), F32), jax.ShapeDtypeStruct((T, D), BF16),
                            jax.ShapeDtypeStruct((T, D), BF16), jax.ShapeDtypeStruct((1, D), F32),
                            jax.ShapeDtypeStruct((1, LANES), F32)],
                 compiler_params=_cp(("arbitrary",)), name="final_fb")(h1, pe, wple, z, tgt, g2)


def _pool_inv_count(t0, rows, pg, ngroups):
    t = t0 + lax.broadcasted_iota(jnp.int32, (rows, pg), 0)
    parts = []
    for w in POOL_WINDOWS[:ngroups]:
        parts.append(jnp.where(t + 1 >= w, 1.0 / w, 1.0 / (t + 1).astype(F32)))
    return parts


def _pool_fwd(proj, wp, P, tb):
    T = proj.shape[0]
    ng = len(POOL_WINDOWS)
    pg = P // ng
    hb = tb // POOL_HALO

    def body(v_ref, tail_ref, w_ref, o_ref, mx_ref, ext):
        i = pl.program_id(0)
        ext[pl.ds(0, POOL_HALO), :] = jnp.where(i > 0, tail_ref[...], 0.0)
        ext[pl.ds(POOL_HALO, tb), :] = v_ref[...]
        inv = _pool_inv_count(i * tb, tb, pg, ng)
        for g, w in enumerate(POOL_WINDOWS):
            cols = pl.ds(g * pg, pg)
            win = ext[pl.ds(POOL_HALO, tb), cols]
            for k in range(1, w):
                win = win + ext[pl.ds(POOL_HALO - k, tb), cols]
            pooled = (win * inv[g] - ext[pl.ds(POOL_HALO, tb), cols]).astype(BF16)
            o_ref[:, cols] = pooled
            mx_ref[:, cols] = jnp.dot(pooled, w_ref[g], preferred_element_type=F32)

    row = _bs((tb, P), lambda i: (i, 0))
    return _call(body, grid=(T // tb,),
                 in_specs=[row, _bs((POOL_HALO, P), lambda i: (jnp.maximum(i * hb - 1, 0), 0)),
                           _bs(wp.shape, lambda i: (0, 0, 0))],
                 out_specs=[row, row],
                 out_shape=[jax.ShapeDtypeStruct((T, P), BF16), jax.ShapeDtypeStruct((T, P), F32)],
                 scratch_shapes=[pltpu.VMEM((tb + POOL_HALO, P), F32)],
                 compiler_params=_cp(("arbitrary",)), name="pool_fwd")(proj, proj, wp)


def _pool_bwd(dmixed, wp, tb, comm=None):
    T, P = dmixed.shape
    ng = len(POOL_WINDOWS)
    pg = P // ng
    hb = tb // POOL_HALO
    nb = T // tb

    def core(d_ref, head_ref, w_ref, o_ref, ext, dpl):
        i = pl.program_id(0)
        inv = _pool_inv_count(i * tb, tb, pg, ng)
        invh = _pool_inv_count((i + 1) * tb, POOL_HALO, pg, ng)
        for g in range(ng):
            cols = pl.ds(g * pg, pg)
            dp = lax.dot_general(d_ref[:, cols], w_ref[g], (NT, ((), ())), preferred_element_type=F32)
            dph = lax.dot_general(head_ref[:, cols], w_ref[g], (NT, ((), ())), preferred_element_type=F32)
            dpl[:, cols] = dp
            ext[pl.ds(0, tb), cols] = dp * inv[g]
            ext[pl.ds(tb, POOL_HALO), cols] = jnp.where(i < nb - 1, dph * invh[g], 0.0)
        for g, w in enumerate(POOL_WINDOWS):
            cols = pl.ds(g * pg, pg)
            acc = ext[pl.ds(0, tb), cols]
            for k in range(1, w):
                acc = acc + ext[pl.ds(k, tb), cols]
            o_ref[:, cols] = (acc - dpl[:, cols]).astype(BF16)

    body, extra = _hosted(core, comm, (nb,), 3, 1, 2)
    return _call(body, grid=(nb,),
                 in_specs=[_bs((tb, P), lambda i: (i, 0)),
                           _bs((POOL_HALO, P), lambda i: (jnp.minimum((i + 1) * hb, T // POOL_HALO - 1), 0)),
                           _bs(wp.shape, lambda i: (0, 0, 0))] + extra["in_specs"],
                 out_specs=[_bs((tb, P), lambda i: (i, 0))] + extra["out_specs"],
                 out_shape=[jax.ShapeDtypeStruct((T, P), BF16)] + extra["out_shape"],
                 scratch_shapes=[pltpu.VMEM((tb + POOL_HALO, P), F32), pltpu.VMEM((tb, P), F32)] + extra["scratch"],
                 compiler_params=_cp(("arbitrary",)), name="pool_bwd")(dmixed, dmixed, wp, *extra["ins"])


def _zoh(a_re, a_im, ldt, b_re, b_im):
    lam_re = jnp.minimum(a_re, A_RE_MAX)
    lam_im = a_im
    dt = jnp.exp(ldt)
    mag = jnp.exp(lam_re * dt)
    ang = lam_im * dt
    ab_re = mag * jnp.cos(ang)
    ab_im = mag * jnp.sin(ang)
    den = lam_re * lam_re + lam_im * lam_im
    n_re = ab_re - 1.0
    n_im = ab_im
    q_re = (n_re * lam_re + n_im * lam_im) / den
    q_im = (n_im * lam_re - n_re * lam_im) / den
    return ab_re, ab_im, q_re * b_re - q_im * b_im, q_re * b_im + q_im * b_re


def _ssm_prep(a_re, a_im, ldt, bt_re, bt_im):
    shp = jax.ShapeDtypeStruct(a_re.shape, F32)

    def body(a, b, c, d, e, o0, o1, o2, o3):
        r = _zoh(a[...], b[...], c[...], d[...], e[...])
        o0[...], o1[...], o2[...], o3[...] = r

    return _call(body, in_specs=[VMEM_FULL] * 5, out_specs=[VMEM_FULL] * 4, out_shape=[shp] * 4,
                 name="ssm_prep")(a_re, a_im, ldt, bt_re, bt_im)


def _ssm_prep_bwd(a_re, a_im, ldt, bt_re, bt_im, dab_re, dab_im, dbb_re, dbb_im, G):
    GC, N = a_re.shape
    C = GC // G

    def body(a, b, c, d, e, g0, g1, g2, g3, da_re, da_im, dldt, db_re, db_im):
        _, vjp = jax.vjp(_zoh, a[...], b[...], c[...], d[...], e[...])
        ga_re, ga_im, gl, gb_re, gb_im = vjp((g0[...], g1[...], g2[...], g3[...]))
        da_re[...] = jnp.sum(ga_re.reshape(G, C, N), axis=1)
        da_im[...] = jnp.sum(ga_im.reshape(G, C, N), axis=1)
        dldt[...] = jnp.sum(jnp.sum(gl.reshape(G, C, N), axis=1), axis=1, keepdims=True)
        db_re[...] = gb_re
        db_im[...] = gb_im

    gn = jax.ShapeDtypeStruct((G, N), F32)
    full = jax.ShapeDtypeStruct((GC, N), F32)
    return _call(body, in_specs=[VMEM_FULL] * 9, out_specs=[VMEM_FULL] * 5,
                 out_shape=[gn, gn, jax.ShapeDtypeStruct((G, 1), F32), full, full],
                 name="ssm_prep_bwd")(a_re, a_im, ldt, bt_re, bt_im, dab_re, dab_im, dbb_re, dbb_im)


def _coef_tiles(abr, abi, reverse):
    ns = abr.shape[1]
    row = lax.broadcasted_iota(jnp.int32, (SUBLANES, ns), 0)
    ar = jnp.broadcast_to(abr, (SUBLANES, ns))
    ai = jnp.broadcast_to(-abi if reverse else abi, (SUBLANES, ns))
    a2r, a2i = ar * ar - ai * ai, 2.0 * ar * ai
    a4r, a4i = a2r * a2r - a2i * a2i, 2.0 * a2r * a2i
    out = []
    for d, (vr, vi) in ((1, (ar, ai)), (2, (a2r, a2i)), (4, (a4r, a4i))):
        keep = (row < SUBLANES - d) if reverse else (row >= d)
        out += [jnp.where(keep, vr, 0.0), jnp.where(keep, vi, 0.0)]
    pr, pi = ar, ai
    for k in range(1, SUBLANES):
        sel = (row <= SUBLANES - 1 - k) if reverse else (row >= k)
        nr, ni = pr * ar - pi * ai, pr * ai + pi * ar
        pr, pi = jnp.where(sel, nr, pr), jnp.where(sel, ni, pi)
    return out + [pr, pi]


def _cpow(ar, ai, n):
    out, br, bi = None, ar, ai
    while n:
        if n & 1:
            out = (br, bi) if out is None else (out[0] * br - out[1] * bi, out[0] * bi + out[1] * br)
        br, bi = br * br - bi * bi, 2.0 * br * bi
        n >>= 1
    return out


def _seg_perm_matrix(nrows):
    r = jnp.arange(nrows)
    src = (nrows // SUBLANES) * (r % SUBLANES) + r // SUBLANES
    return (src[:, None] == jnp.arange(nrows)[None, :]).astype(BF16)


def _seg_order_rows(pm, xb):
    return jnp.dot(pm, xb, preferred_element_type=F32).astype(BF16)


def _time_order_rows(pm, x, terms):
    out, rest = None, x
    for t in range(terms):
        piece = rest.astype(BF16)
        part = lax.dot_general(pm, piece, (TN, ((), ())), preferred_element_type=F32)
        out = part if out is None else out + part
        if t + 1 < terms:
            rest = rest - piece.astype(F32)
    return out


def _scan_tiles(abr, abi, seg, reverse):
    ns = abr.shape[1]
    seg_pow = _cpow(abr, abi, seg)
    step = [jnp.broadcast_to(abr, (SUBLANES, ns)), jnp.broadcast_to(-abi if reverse else abi, (SUBLANES, ns))]
    return _coef_tiles(seg_pow[0], seg_pow[1], reverse) + step


def _seg_scan(xr_ref, xi_ref, coef_ref, car_ref, cai_ref, *, nrows, ns, reverse, cmat=None, dab=None):
    seg = nrows // SUBLANES
    cw = min(SCAN_LANES, ns)
    row = lax.broadcasted_iota(jnp.int32, (SUBLANES, cw), 0)
    first, last = (SUBLANES - 1, 0) if reverse else (0, SUBLANES - 1)

    def tile(i):
        return pl.ds(pl.multiple_of(((seg - 1 - i) if reverse else i) * SUBLANES, SUBLANES), SUBLANES)

    for cc in range(ns // cw):
        cols = pl.ds(cc * cw, cw)
        ar, ai = coef_ref[8, :, cols], coef_ref[9, :, cols]

        def local(i, x, cols=cols, ar=ar, ai=ai):
            rows = tile(i)
            nr = ar * x[0] - ai * x[1] + xr_ref[rows, cols]
            ni = ar * x[1] + ai * x[0] + xi_ref[rows, cols]
            xr_ref[rows, cols] = nr
            xi_ref[rows, cols] = ni
            return nr, ni

        zero = jnp.zeros((SUBLANES, cw), F32)
        er, ei = lax.fori_loop(0, seg, local, (zero, zero))

        co = [coef_ref[k, :, cols] for k in range(8)]
        for lvl, d in enumerate((1, 2, 4)):
            kr, ki = co[2 * lvl], co[2 * lvl + 1]
            sh = SUBLANES - d if reverse else d
            sr, si = pltpu.roll(er, sh, 0), pltpu.roll(ei, sh, 0)
            er, ei = er + (kr * sr - ki * si), ei + (kr * si + ki * sr)
        c0r, c0i = car_ref[:, cols], cai_ref[:, cols]
        er, ei = er + (co[6] * c0r - co[7] * c0i), ei + (co[6] * c0i + co[7] * c0r)
        nb_shift = SUBLANES - 1 if reverse else 1
        cmr = jnp.where(row == first, c0r, pltpu.roll(er, nb_shift, 0))
        cmi = jnp.where(row == first, c0i, pltpu.roll(ei, nb_shift, 0))
        car_ref[:, cols] = jnp.broadcast_to(er[last:last + 1, :], er.shape)
        cai_ref[:, cols] = jnp.broadcast_to(ei[last:last + 1, :], ei.shape)
        if cmat is not None:
            cmat[0][:, cols] = cmr
            cmat[1][:, cols] = cmi

        w0 = (ar * cmr - ai * cmi, ar * cmi + ai * cmr)
        if dab is None:
            def fix(i, w, cols=cols, ar=ar, ai=ai):
                rows = tile(i)
                xr_ref[rows, cols] = xr_ref[rows, cols] + w[0]
                xi_ref[rows, cols] = xi_ref[rows, cols] + w[1]
                return ar * w[0] - ai * w[1], ar * w[1] + ai * w[0]

            lax.fori_loop(0, seg, fix, w0)
        else:
            s_re, s_im, e_re, e_im, o_re, o_im = dab

            def add(rows, w, pr, pi, acc):
                gr = xr_ref[rows, cols] + w[0]
                gi = xi_ref[rows, cols] + w[1]
                xr_ref[rows, cols] = gr
                xi_ref[rows, cols] = gi
                return acc[0] + (gr * pr + gi * pi), acc[1] + (gi * pr - gr * pi)

            def fix(i, st, cols=cols, ar=ar, ai=ai):
                w, acc = st[:2], st[2:]
                rows = tile(i)
                before = pl.ds(pl.multiple_of((seg - 2 - i) * SUBLANES, SUBLANES), SUBLANES)
                acc = add(rows, w, s_re[before, cols], s_im[before, cols], acc)
                return (ar * w[0] - ai * w[1], ar * w[1] + ai * w[0]) + acc

            st = lax.fori_loop(0, seg - 1, fix, w0 + (zero, zero))
            acc = add(pl.ds(0, SUBLANES), st[:2], e_re[:, cols], e_im[:, cols], st[2:])
            o_re[:, cols] += jnp.sum(acc[0], axis=0, keepdims=True)
            o_im[:, cols] += jnp.sum(acc[1], axis=0, keepdims=True)


def _hosted(core, comm, grid, n_in, n_out, n_scratch):
    ci = len(comm["ins"]) if comm else 0
    co = len(comm["out_shape"]) if comm else 0

    def body(*refs):
        ins, rest = refs[:n_in + ci], refs[n_in + ci:]
        outs, scr = rest[:n_out + co], rest[n_out + co:]
        hooks = functools.partial(_comm_hooks, comm, grid, ins[n_in:], outs[n_out:], scr[n_scratch:])
        hooks(before=True)
        core(*ins[:n_in], *outs[:n_out], *scr[:n_scratch])
        hooks(before=False)

    aliases = {n_in + i: n_out + i for i in range(co)} if comm and comm.get("alias") else {}
    extra = dict(ins=list(comm["ins"]) if comm else [], in_specs=[ANY] * ci, out_specs=[ANY] * co,
                 out_shape=list(comm["out_shape"]) if comm else [], scratch=list(comm["scratch"]) if comm else [],
                 aliases=aliases)
    return body, extra


def _ssm_fwd(proj, bdr, bdi, cdr, cdi, abr, abi, dsk, P, tb, comm=None):
    T = proj.shape[0]
    ntl, ct, st = bdr.shape
    ns = ntl * st
    nb = T // tb

    def core(u_ref, bdr_ref, bdi_ref, cdr_ref, cdi_ref, abr_ref, abi_ref, d_ref, pm_ref,
             y_ref, ge_ref, bsr_ref, bsi_ref, sr, si, coef, car, cai, up):
        @pl.when(pl.program_id(0) == 0)
        def _():
            for k, tile in enumerate(_scan_tiles(abr_ref[...], abi_ref[...], tb // SUBLANES, False)):
                coef[k] = tile
            car[...] = jnp.zeros_like(car)
            cai[...] = jnp.zeros_like(cai)

        bsr_ref[...] = car[...]
        bsi_ref[...] = cai[...]
        u = u_ref[...]
        ub = _seg_order_rows(pm_ref[...], u.astype(BF16))
        for s in range(ntl):
            us = ub[:, s * ct:(s + 1) * ct]
            sr[:, s * st:(s + 1) * st] = jnp.dot(us, bdr_ref[s], preferred_element_type=F32)
            si[:, s * st:(s + 1) * st] = jnp.dot(us, bdi_ref[s], preferred_element_type=F32)
        _seg_scan(sr, si, coef, car, cai, nrows=tb, ns=ns, reverse=False)
        for s in range(ntl):
            s_re = sr[:, s * st:(s + 1) * st].astype(BF16)
            s_im = si[:, s * st:(s + 1) * st].astype(BF16)
            up[:, s * ct:(s + 1) * ct] = (jnp.dot(s_re, cdr_ref[s], preferred_element_type=F32)
                                          - jnp.dot(s_im, cdi_ref[s], preferred_element_type=F32))
        y = _time_order_rows(pm_ref[...], up[...], 3) + d_ref[...] * u
        y_ref[...] = y
        ge_ref[...] = _gelu(y).astype(BF16)

    full3 = lambda a: _bs(a.shape, lambda i: (0, 0, 0))
    vec = lambda n: _bs((1, n), lambda i: (0, 0))
    row = _bs((tb, P), lambda i: (i, 0))
    st_spec = _bs((None, SUBLANES, ns), lambda i: (i, 0, 0))
    body, extra = _hosted(core, comm, (nb,), 9, 4, 6)
    return _call(body, grid=(nb,),
                 in_specs=[_bs((tb, P), lambda i: (i, 2)), full3(bdr), full3(bdi), full3(cdr), full3(cdi),
                           vec(ns), vec(ns), vec(P), _bs((tb, tb), lambda i: (0, 0))] + extra["in_specs"],
                 out_specs=[row, row, st_spec, st_spec] + extra["out_specs"],
                 out_shape=[jax.ShapeDtypeStruct((T, P), F32), jax.ShapeDtypeStruct((T, P), BF16),
                            jax.ShapeDtypeStruct((nb, SUBLANES, ns), F32),
                            jax.ShapeDtypeStruct((nb, SUBLANES, ns), F32)] + extra["out_shape"],
                 scratch_shapes=[pltpu.VMEM((tb, ns), F32), pltpu.VMEM((tb, ns), F32),
                                 pltpu.VMEM((10, SUBLANES, ns), F32),
                                 pltpu.VMEM((SUBLANES, ns), F32), pltpu.VMEM((SUBLANES, ns), F32),
                                 pltpu.VMEM((tb, P), F32)] + extra["scratch"],
                 compiler_params=_cp(("arbitrary",)), name="ssm_fwd")(
                     proj, bdr, bdi, cdr, cdi, abr, abi, dsk, _seg_perm_matrix(tb), *extra["ins"])


def _ssm_bwd(proj, y, dge, bsr, bsi, bdr, bdi, cdr, cdi, abr, abi, dsk, dpi, dpg, dsg, P, tb, comm=None):
    T = proj.shape[0]
    ntl, ct, st = bdr.shape
    ns = ntl * st
    nb = T // tb

    def core(u_ref, y_ref, dge_ref, bsr_ref, bsi_ref, abr_ref, abi_ref, d_ref, pm_ref, dpi_ref, dpg_ref, dsg_ref,
             bdr_h, bdi_h, cdr_h, cdi_h,
             dproj_ref, dabr_ref, dabi_ref, dd_ref, dbdr_h, dbdi_h, dcdr_h, dcdi_h,
             wbdr, wbdi, wcdr, wcdi, abdr, abdi, acdr, acdi, spr, spi, gr, gi, coef_f, coef_r,
             car, cai, gcr, gci, ser, sei, dup):
        i = pl.program_id(0)

        @pl.when(i == 0)
        def _():
            for h, w in ((bdr_h, wbdr), (bdi_h, wbdi), (cdr_h, wcdr), (cdi_h, wcdi)):
                pltpu.sync_copy(h, w)
            for a in (abdr, abdi, acdr, acdi, gcr, gci):
                a[...] = jnp.zeros_like(a)
            for o in (dabr_ref, dabi_ref, dd_ref):
                o[...] = jnp.zeros_like(o)
            for k, tile in enumerate(_scan_tiles(abr_ref[...], abi_ref[...], tb // SUBLANES, False)):
                coef_f[k] = tile
            for k, tile in enumerate(_scan_tiles(abr_ref[...], abi_ref[...], tb // SUBLANES, True)):
                coef_r[k] = tile

        car[...] = bsr_ref[...]
        cai[...] = bsi_ref[...]
        u = u_ref[...]
        dy = dge_ref[...] * _gelu_grad(y_ref[...])
        ub = _seg_order_rows(pm_ref[...], u.astype(BF16))
        dyb = _seg_order_rows(pm_ref[...], dy.astype(BF16))
        for s in range(ntl):
            us = ub[:, s * ct:(s + 1) * ct]
            spr[:, s * st:(s + 1) * st] = jnp.dot(us, wbdr[s], preferred_element_type=F32)
            spi[:, s * st:(s + 1) * st] = jnp.dot(us, wbdi[s], preferred_element_type=F32)
        _seg_scan(spr, spi, coef_f, car, cai, nrows=tb, ns=ns, reverse=False, cmat=(ser, sei))

        for s in range(ntl):
            dys = dyb[:, s * ct:(s + 1) * ct]
            gr[:, s * st:(s + 1) * st] = lax.dot_general(dys, wcdr[s], (NT, ((), ())), preferred_element_type=F32)
            gi[:, s * st:(s + 1) * st] = -lax.dot_general(dys, wcdi[s], (NT, ((), ())), preferred_element_type=F32)
        _seg_scan(gr, gi, coef_r, gcr, gci, nrows=tb, ns=ns, reverse=True,
                  dab=(spr, spi, ser, sei, dabr_ref, dabi_ref))

        for s in range(ntl):
            sl_c, sl_s = slice(s * ct, (s + 1) * ct), slice(s * st, (s + 1) * st)
            s_re = spr[:, sl_s].astype(BF16)
            s_im = spi[:, sl_s].astype(BF16)
            g_re, g_im = gr[:, sl_s].astype(BF16), gi[:, sl_s].astype(BF16)
            dys, us = dyb[:, sl_c], ub[:, sl_c]
            acdr[s] += lax.dot_general(s_re, dys, (TN, ((), ())), preferred_element_type=F32)
            acdi[s] -= lax.dot_general(s_im, dys, (TN, ((), ())), preferred_element_type=F32)
            abdr[s] += lax.dot_general(us, g_re, (TN, ((), ())), preferred_element_type=F32)
            abdi[s] += lax.dot_general(us, g_im, (TN, ((), ())), preferred_element_type=F32)
            dup[:, sl_c] = (lax.dot_general(g_re, wbdr[s], (NT, ((), ())), preferred_element_type=F32)
                            + lax.dot_general(g_im, wbdi[s], (NT, ((), ())), preferred_element_type=F32))
        dd_ref[...] += jnp.sum(dy * u, axis=0, keepdims=True)
        du = _time_order_rows(pm_ref[...], dup[...], 2) + d_ref[...] * dy
        dproj_ref[:, 0:P] = dpi_ref[...]
        dproj_ref[:, P:2 * P] = dpg_ref[...]
        dproj_ref[:, 2 * P:3 * P] = du.astype(BF16)
        dproj_ref[:, 3 * P:4 * P] = dsg_ref[...]

        @pl.when(i == nb - 1)
        def _():
            for a, h in ((abdr, dbdr_h), (abdi, dbdi_h), (acdr, dcdr_h), (acdi, dcdi_h)):
                pltpu.sync_copy(a, h)

    rev = lambda i: nb - 1 - i
    vec = lambda n: _bs((1, n), lambda i: (0, 0))
    row = _bs((tb, P), lambda i: (rev(i), 0))
    st_spec = _bs((None, SUBLANES, ns), lambda i: (rev(i), 0, 0))
    bshape = jax.ShapeDtypeStruct(bdr.shape, F32)
    cshape = jax.ShapeDtypeStruct(cdr.shape, F32)
    body, extra = _hosted(core, comm, (nb,), 16, 8, 21)
    return _call(body, grid=(nb,),
                 in_specs=[_bs((tb, P), lambda i: (rev(i), 2)), row, row, st_spec, st_spec,
                           vec(ns), vec(ns), vec(P), _bs((tb, tb), lambda i: (0, 0)), row, row, row,
                           ANY, ANY, ANY, ANY] + extra["in_specs"],
                 out_specs=[_bs((tb, 4 * P), lambda i: (rev(i), 0)), vec(ns), vec(ns), vec(P), ANY, ANY, ANY, ANY]
                 + extra["out_specs"],
                 out_shape=[jax.ShapeDtypeStruct((T, 4 * P), BF16), jax.ShapeDtypeStruct((1, ns), F32),
                            jax.ShapeDtypeStruct((1, ns), F32), jax.ShapeDtypeStruct((1, P), F32),
                            bshape, bshape, cshape, cshape] + extra["out_shape"],
                 scratch_shapes=[pltpu.VMEM(bdr.shape, BF16), pltpu.VMEM(bdr.shape, BF16),
                                 pltpu.VMEM(cdr.shape, BF16), pltpu.VMEM(cdr.shape, BF16),
                                 pltpu.VMEM(bdr.shape, F32), pltpu.VMEM(bdr.shape, F32),
                                 pltpu.VMEM(cdr.shape, F32), pltpu.VMEM(cdr.shape, F32),
                                 pltpu.VMEM((tb, ns), F32), pltpu.VMEM((tb, ns), F32),
                                 pltpu.VMEM((tb, ns), F32), pltpu.VMEM((tb, ns), F32),
                                 pltpu.VMEM((10, SUBLANES, ns), F32), pltpu.VMEM((10, SUBLANES, ns), F32)]
                 + [pltpu.VMEM((SUBLANES, ns), F32)] * 6 + [pltpu.VMEM((tb, P), F32)] + extra["scratch"],
                 compiler_params=_cp(("arbitrary",)), name="ssm_bwd")(
                     proj, y, dge, bsr, bsi, abr, abi, dsk, _seg_perm_matrix(tb), dpi, dpg, dsg,
                     bdr, bdi, cdr, cdi, *extra["ins"])


def _adamw(w, g, m, v, name, comm=None):
    R, C = w.shape
    tr = _t(R, 256)

    def core(w_ref, g_ref, m_ref, v_ref, d_ref, mo_ref, vo_ref):
        gv = g_ref[...]
        mn = ADAM_B1 * m_ref[...] + (1.0 - ADAM_B1) * gv
        vn = ADAM_B2 * v_ref[...] + (1.0 - ADAM_B2) * (gv * gv)
        m_hat = mn / (1.0 - ADAM_B1 ** ADAM_STEP)
        v_hat = vn / (1.0 - ADAM_B2 ** ADAM_STEP)
        d_ref[...] = -ADAM_LR * (m_hat / (jnp.sqrt(v_hat) + ADAM_EPS) + ADAM_WD * w_ref[...])
        mo_ref[...] = mn
        vo_ref[...] = vn

    blk = _bs((tr, C), lambda i: (i, 0))
    shp = jax.ShapeDtypeStruct((R, C), F32)
    body, extra = _hosted(core, comm, (R // tr,), 4, 3, 0)
    return _call(body, grid=(R // tr,), in_specs=[blk] * 4 + extra["in_specs"],
                 out_specs=[blk] * 3 + extra["out_specs"], out_shape=[shp] * 3 + extra["out_shape"],
                 scratch_shapes=extra["scratch"],
                 compiler_params=_cp(("arbitrary",) if comm else ("parallel",)), name=name)(w, g, m, v, *extra["ins"])


def _sum_cast(grad, got, place, name):
    J, H, C = got.shape
    tr = _t(H, 256)
    nb = H // tr

    def body(pl_ref, a_ref, b_ref, o_ref):
        o_ref[...] = (a_ref[...] + b_ref[...]).astype(BF16)

    blk = _bs((None, tr, C), lambda j, i, pc: (j, i, 0))
    mine = _bs((None, tr, C), lambda j, i, pc: (j, pc[1] * nb + i, 0))
    spec = pltpu.PrefetchScalarGridSpec(num_scalar_prefetch=1, grid=(J, nb), in_specs=[mine, blk], out_specs=blk)
    return _call(body, grid_spec=spec, out_shape=jax.ShapeDtypeStruct((J, H, C), BF16),
                 compiler_params=_cp(("parallel", "parallel")), name=name)(place, grad, got)


def _sum_chips(sent, arrived, place, name):
    J, H, C = arrived.shape
    tr = _t(H, 256)
    nb = H // tr

    def body(pl_ref, own_ref, a0_ref, a1_ref, a2_ref, o_ref):
        acc = own_ref[...].astype(F32)
        for r in (a0_ref, a1_ref, a2_ref):
            acc = acc + r[...].astype(F32)
        o_ref[...] = acc

    def other(k):
        return _bs((None, tr, C), lambda i, pc: (jnp.where(pc[0] <= k, k + 1, k), i, 0))

    spec = pltpu.PrefetchScalarGridSpec(
        num_scalar_prefetch=1, grid=(nb,),
        in_specs=[_bs((None, tr, C), lambda i, pc: (pc[0], i, 0)), other(0), other(1), other(2)],
        out_specs=_bs((tr, C), lambda i, pc: (pc[1] * nb + i, 0)))
    return _call(body, grid_spec=spec, out_shape=jax.ShapeDtypeStruct((2 * H, C), F32),
                 compiler_params=_cp(("parallel",)), name=name)(place, sent, arrived, arrived, arrived)


def _place():
    x, y, c = lax.axis_index("x"), lax.axis_index("y"), lax.axis_index("c")
    chips = [(1 - x, y), (x, 1 - y), (1 - x, 1 - y)]
    return x, y, c, chips


def _split(nrows, row_bytes, align, cap=None):
    k = max(1, min(cap or DMA_MAX_CHUNKS, (nrows * row_bytes) // DMA_CHUNK_BYTES))
    while k > 1 and nrows % (k * align):
        k -= 1
    return k


def _comm_call(plan, name):
    n_in, n_out = len(plan["ins"]), len(plan["out_shape"])

    def body(*refs):
        for phase in plan["phases"]:
            phase(refs[:n_in], refs[n_in:n_in + n_out], refs[n_in + n_out:])

    return _call(body, in_specs=[ANY] * n_in, out_specs=[ANY] * n_out, out_shape=plan["out_shape"],
                 input_output_aliases={i: i for i in range(n_out)} if plan.get("alias") else {},
                 scratch_shapes=plan["scratch"], name=name)(*plan["ins"])


def _comm_hooks(plan, grid, ins, outs, sems, *, before):
    if plan is None:
        return
    nsteps, step = 1, 0
    for d, g in enumerate(grid):
        nsteps, step = nsteps * g, step * g + pl.program_id(d)
    for p, (phase, frac) in enumerate(zip(plan["phases"], plan["at"])):
        if (p == 0) == before:
            pl.when(step == int(frac * (nsteps - 1)))(functools.partial(phase, ins, outs, sems))


def _ag_plan(shards, axes):
    n = len(shards)
    shapes = [a.shape for a in shards]

    def window(ref, i, chip, half=None):
        S, ax = shapes[i], axes[i]
        idx = []
        for d in range(len(S)):
            off, size = 0, S[d]
            if d == 0 and half is not None:
                off, size = half * (S[0] // 2), S[0] // 2
            if d == ax:
                off = off + chip * S[ax]
            idx.append(pl.ds(off, size))
        return ref.at[tuple(idx)]

    def copies(src, full, sems):
        ssem, rsem = sems
        x, y, c, chips = _place()
        me = 2 * x + y
        sib = (x, y, 1 - c)
        idx = [2 * cx + cy for cx, cy in chips]

        def rcopy(i, k, s_ref, d_ref, to):
            return pltpu.make_async_remote_copy(src_ref=s_ref, dst_ref=d_ref, send_sem=ssem.at[i, k],
                                                recv_sem=rsem.at[i, k], device_id=to, device_id_type=MESH)

        def ici(i, j, incoming):
            half_src = src[i].at[pl.ds(c * (shapes[i][0] // 2), shapes[i][0] // 2)]
            return rcopy(i, j, half_src, window(full[i], i, idx[j] if incoming else me, c), (*chips[j], c))

        def fwd(i, j, half):
            w = window(full[i], i, idx[j], half)
            return rcopy(i, 3 + j, w, w, sib)

        def own(i):
            return rcopy(i, 6, src[i], window(full[i], i, me), sib)

        return c, ici, fwd, own

    def send(src, full, sems):
        c, ici, fwd, own = copies(src, full, sems)
        for i in range(n):
            for j in range(3):
                ici(i, j, False).start()
        for i in range(n):
            own(i).start()

    def forward(i, src, full, sems):
        c, ici, fwd, own = copies(src, full, sems)
        for j in range(3):
            ici(i, j, True).wait_recv()
            fwd(i, j, c).start()

    def finish(src, full, sems):
        c, ici, fwd, own = copies(src, full, sems)
        for i in range(n):
            for j in range(3):
                fwd(i, j, 1 - c).wait_recv()
            own(i).wait()
        for i in range(n):
            for j in range(3):
                ici(i, j, False).wait_send()
                fwd(i, j, c).wait_send()

    out_shape = [jax.ShapeDtypeStruct(tuple(N_CHIP * d if k == ax else d for k, d in enumerate(S)), BF16)
                 for S, ax in zip(shapes, axes)]
    sizes = [a.size for a in shards]
    behind = [0.85 * sum(sizes[:i + 1]) / sum(sizes) + 0.05 for i in range(n)]
    return dict(ins=list(shards), out_shape=out_shape,
                phases=[send] + [functools.partial(forward, i) for i in range(n)] + [finish],
                at=[0.0] + behind + [1.0],
                scratch=[pltpu.SemaphoreType.DMA((n, 7)), pltpu.SemaphoreType.DMA((n, 7))])


def _proj_ag(x, g1, wsh, order, tm):
    T, D = x.shape
    P = wsh.shape[1]
    H = D // 2
    nt = T // tm

    def body(order_ref, x_ref, g_ref, wsh_ref, hn_ref, proj_ref, win_ref, wbuf, lsem, ssem, rsem):
        n, i = pl.program_id(0), pl.program_id(1)
        x, y, c, chips = _place()
        me = 2 * x + y
        sib = (x, y, 1 - c)
        idx = [2 * cx + cy for cx, cy in chips]

        def rcopy(k, s_ref, d_ref, to):
            return pltpu.make_async_remote_copy(src_ref=s_ref, dst_ref=d_ref, send_sem=ssem.at[k],
                                                recv_sem=rsem.at[k], device_id=to, device_id_type=MESH)

        def cols(chip):
            return pl.ds(pl.multiple_of(chip * P, LANES), P)

        def rows(half):
            return pl.ds(pl.multiple_of(half * H, 16), H)

        def ici(j, incoming):
            return rcopy(j, wsh_ref.at[rows(c)], win_ref.at[rows(c), cols(idx[j] if incoming else me)],
                         (*chips[j], c))

        def fwd(j, half):
            w = win_ref.at[rows(half), cols(idx[j])]
            return rcopy(3 + j, w, w, sib)

        def own():
            return rcopy(6, wsh_ref, win_ref.at[:, cols(me)], sib)

        def load(src):
            cp = pltpu.make_async_copy(src, wbuf, lsem)
            cp.start()
            cp.wait()

        @pl.when((n == 0) & (i == 0))
        def _():
            ici(0, False).start()
            ici(1, False).start()
            own().start()
            load(wsh_ref)

        for j in range(3):
            @pl.when((n == j + 1) & (i == 0))
            def _(j=j):
                if j == 0:
                    ici(2, False).start()
                ici(j, True).wait_recv()
                fwd(j, c).start()
                fwd(j, 1 - c).wait_recv()
                load(win_ref.at[:, cols(idx[j])])

        xv = x_ref[...]
        r = lax.rsqrt(jnp.mean(xv * xv, axis=-1, keepdims=True) + EPS)
        hn = ((xv * r) * g_ref[...]).astype(BF16)

        @pl.when(n == 0)
        def _():
            hn_ref[...] = hn

        proj_ref[...] = jnp.dot(hn, wbuf[...], preferred_element_type=F32)

        @pl.when((n == 3) & (i == nt - 1))
        def _():
            own().wait()
            for j in range(3):
                ici(j, False).wait_send()
                fwd(j, c).wait_send()

    spec = pltpu.PrefetchScalarGridSpec(
        num_scalar_prefetch=1, grid=(N_CHIP, nt),
        in_specs=[_bs((tm, D), lambda n, i, o: (i, 0)), _bs((1, D), lambda n, i, o: (0, 0)), ANY],
        out_specs=[_bs((tm, D), lambda n, i, o: (jnp.where(n == 0, i, nt - 1), 0)),
                   _bs((tm, P), lambda n, i, o: (i, o[n])), ANY],
        scratch_shapes=[pltpu.VMEM((D, P), BF16), pltpu.SemaphoreType.DMA,
                        pltpu.SemaphoreType.DMA((7,)), pltpu.SemaphoreType.DMA((7,))])
    return _call(body, grid_spec=spec,
                 out_shape=[jax.ShapeDtypeStruct((T, D), BF16), jax.ShapeDtypeStruct((T, N_CHIP * P), F32),
                            jax.ShapeDtypeStruct((D, N_CHIP * P), BF16)],
                 compiler_params=_cp(("arbitrary", "arbitrary")), name="proj_ag")(order, x, g1, wsh)


def _halves_plan(grads):
    n = len(grads)

    def send(g, got, sems):
        ssem, rsem = sems
        x, y, c, _ = _place()
        sib = (x, y, 1 - c)
        for i in range(n):
            J, R, C = g[i].shape
            H = R // 2
            size = g[i].dtype.itemsize
            tile_rows = SUBLANES * 4 // size
            k = _split(H, C * size, tile_rows, cap=DMA_MAX_CHUNKS // J)
            hr = H // k
            for j in range(J):
                for q in range(k):
                    other = pl.ds(pl.multiple_of((1 - c) * H + q * hr, tile_rows), hr)
                    to = pl.ds(q * hr, hr)
                    pltpu.make_async_remote_copy(src_ref=g[i].at[j, other, :], dst_ref=got[i].at[j, to, :],
                                                 send_sem=ssem.at[i], recv_sem=rsem.at[i],
                                                 device_id=sib, device_id_type=MESH).start()

    def finish(g, got, sems):
        ssem, rsem = sems
        x, y, c, _ = _place()
        for i in range(n):
            pltpu.make_async_remote_copy(src_ref=got[i], dst_ref=got[i], send_sem=ssem.at[i], recv_sem=rsem.at[i],
                                         device_id=(x, y, 1 - c), device_id_type=MESH).wait()

    half = [jax.ShapeDtypeStruct((a.shape[0], a.shape[1] // 2, a.shape[2]), a.dtype) for a in grads]
    return dict(ins=list(grads), out_shape=half, phases=[send, finish], at=[0.0, 1.0],
                scratch=[pltpu.SemaphoreType.DMA((n,)), pltpu.SemaphoreType.DMA((n,))])


def _scatter_plan(parts):
    n = len(parts)

    def peers():
        x, y, c, chips = _place()
        return 2 * x + y, c, chips, [2 * cx + cy for cx, cy in chips]

    def send(s, got, sems):
        ssem, rsem = sems
        me, c, chips, idx = peers()
        for i in range(n):
            _, H, C = s[i].shape
            k = _split(H, C * 2, 16, cap=RS_CHUNKS)
            hr = H // k
            for q in range(k):
                rows = pl.ds(q * hr, hr)
                for j in range(3):
                    pltpu.make_async_remote_copy(src_ref=s[i].at[idx[j], rows, :], dst_ref=got[i].at[me, rows, :],
                                                 send_sem=ssem.at[i, j], recv_sem=rsem.at[i, j],
                                                 device_id=(*chips[j], c), device_id_type=MESH).start()

    def finish(s, got, sems):
        ssem, rsem = sems
        me, c, chips, idx = peers()
        for i in range(n):
            for j in range(3):
                pltpu.make_async_remote_copy(src_ref=s[i].at[idx[j]], dst_ref=got[i].at[idx[j]],
                                             send_sem=ssem.at[i, j], recv_sem=rsem.at[i, j],
                                             device_id=(*chips[j], c), device_id_type=MESH).wait()

    return dict(ins=list(parts), out_shape=[jax.ShapeDtypeStruct(a.shape, a.dtype) for a in parts],
                phases=[send, finish], at=[0.0, 1.0],
                scratch=[pltpu.SemaphoreType.DMA((n, 3)), pltpu.SemaphoreType.DMA((n, 3))])


def _join_plan(shards):
    n = len(shards)

    def send(_, full, sems):
        ssem, rsem = sems
        x, y, c, _ = _place()
        sib = (x, y, 1 - c)
        for i in range(n):
            H, C = full[i].shape[0] // 2, full[i].shape[1]
            k = _split(H, C * 4, SUBLANES)
            hr = H // k
            for q in range(k):
                rows = pl.ds(pl.multiple_of(c * H + q * hr, SUBLANES), hr)
                pltpu.make_async_remote_copy(src_ref=full[i].at[rows], dst_ref=full[i].at[rows],
                                             send_sem=ssem.at[i], recv_sem=rsem.at[i],
                                             device_id=sib, device_id_type=MESH).start()

    def finish(_, full, sems):
        ssem, rsem = sems
        x, y, c, _ = _place()
        for i in range(n):
            half = full[i].at[pl.ds(0, full[i].shape[0] // 2)]
            pltpu.make_async_remote_copy(src_ref=half, dst_ref=half, send_sem=ssem.at[i], recv_sem=rsem.at[i],
                                         device_id=(x, y, 1 - c), device_id_type=MESH).wait()

    return dict(ins=list(shards), out_shape=[jax.ShapeDtypeStruct(a.shape, a.dtype) for a in shards],
                phases=[send, finish], at=[0.0, 1.0], alias=True,
                scratch=[pltpu.SemaphoreType.DMA((n,)), pltpu.SemaphoreType.DMA((n,))])


def _allreduce_plan(buf):
    R, L = buf.shape
    RB = R // N_DEV

    def parts(sems):
        xv, got, ov, lsem, ssem, rsem = sems
        x, y, c, _ = _place()
        me = 4 * x + 2 * y + c

        def dev(k):
            return (k // 4, (k // 2) % 2, k % 2)

        def slab(k):
            return pl.ds(pl.multiple_of(k * RB, SUBLANES), RB)

        def first(d, to, landing):
            return pltpu.make_async_remote_copy(src_ref=xv.at[slab(to)], dst_ref=got.at[landing],
                                                send_sem=ssem.at[0, d], recv_sem=rsem.at[0, d],
                                                device_id=dev(to), device_id_type=MESH)

        def second(d, to, k):
            return pltpu.make_async_remote_copy(src_ref=ov.at[slab(k)], dst_ref=ov.at[slab(k)],
                                                send_sem=ssem.at[1, d], recv_sem=rsem.at[1, d],
                                                device_id=dev(to), device_id_type=MESH)

        return me, slab, first, second

    def scatter(ins, outs, sems):
        xv, lsem = sems[0], sems[3]
        me, slab, first, second = parts(sems)
        cp = pltpu.make_async_copy(ins[0], xv, lsem)
        cp.start()
        cp.wait()
        for d in range(1, N_DEV):
            first(d, (me + d) % N_DEV, me).start()

    def reduce(ins, outs, sems):
        xv, got, ov = sems[:3]
        me, slab, first, second = parts(sems)
        got[me] = xv[slab(me), :]
        for d in range(1, N_DEV):
            src = (me + N_DEV - d) % N_DEV
            first(d, src, src).wait_recv()
        acc = got[0]
        for k in range(1, N_DEV):
            acc = acc + got[k]
        ov[slab(me), :] = acc
        for d in range(1, N_DEV):
            second(d, (me + d) % N_DEV, me).start()

    def collect(ins, outs, sems):
        ov, lsem = sems[2], sems[3]
        me, slab, first, second = parts(sems)
        for d in range(1, N_DEV):
            src = (me + N_DEV - d) % N_DEV
            second(d, src, src).wait_recv()
        for d in range(1, N_DEV):
            peer = (me + d) % N_DEV
            first(d, peer, me).wait_send()
            second(d, peer, me).wait_send()
        cp = pltpu.make_async_copy(ov, outs[0], lsem)
        cp.start()
        cp.wait()

    return dict(ins=[buf], out_shape=[jax.ShapeDtypeStruct((R, L), F32)], phases=[scatter, reduce, collect],
                at=[0.0, 0.5, 1.0],
                scratch=[pltpu.VMEM((R, L), F32), pltpu.VMEM((N_DEV, RB, L), F32), pltpu.VMEM((R, L), F32),
                         pltpu.SemaphoreType.DMA, pltpu.SemaphoreType.DMA((2, N_DEV)),
                         pltpu.SemaphoreType.DMA((2, N_DEV))])


def _block_diag(t, gt):
    G, A, B = t.shape
    t4 = t.reshape(G // gt, gt, A, B)
    eye = jnp.eye(gt, dtype=t.dtype)
    return jnp.einsum('sgab,gh->sgahb', t4, eye).reshape(G // gt, gt * A, gt * B)


def _block_diag_extract(m, gt, A, B):
    S = m.shape[0]
    m5 = m.reshape(S, gt, A, gt, B)
    eye = jnp.eye(gt, dtype=m.dtype)
    return jnp.einsum('sgahb,gh->sgab', m5, eye).reshape(S * gt, A, B)


def _tile_rows(n):
    return -(-n // (SUBLANES * LANES)) * SUBLANES


def _pack_small(arrs, rows):
    parts = []
    for a in arrs:
        flat = a.reshape(-1).astype(F32)
        r = _tile_rows(flat.shape[0])
        parts.append(jnp.pad(flat, (0, r * LANES - flat.shape[0])).reshape(r, LANES))
    used = sum(p.shape[0] for p in parts)
    if rows > used:
        parts.append(jnp.zeros((rows - used, LANES), F32))
    return jnp.concatenate(parts)


def _unpack_small(buf, shapes):
    out, off = [], 0
    for s in shapes:
        n = 1
        for d in s:
            n *= d
        r = _tile_rows(n)
        piece = buf[off:off + r]
        out.append(piece.reshape(s) if n == r * LANES else piece.reshape(-1)[:n].reshape(s))
        off += r
    return out


def kernel(x, p, norm_gain, w_in, w_pool, pool_scale, a_re, a_im, log_dt, b_re, b_im, c_re, c_im, d_skip, w_glu, w_out, w_ple, w_ple_gate, final_gain, loss_target, m_norm_gain, m_w_in, m_w_pool, m_pool_scale, m_a_re, m_a_im, m_log_dt, m_b_re, m_b_im, m_c_re, m_c_im, m_d_skip, m_w_glu, m_w_out, m_w_ple, m_w_ple_gate, m_final_gain, v_norm_gain, v_w_in, v_w_pool, v_pool_scale, v_a_re, v_a_im, v_log_dt, v_b_re, v_b_im, v_c_re, v_c_im, v_d_skip, v_w_glu, v_w_out, v_w_ple, v_w_ple_gate, v_final_gain):
    xs, pe, tgt = x[0], p[0, 0], loss_target[0]
    T, D = xs.shape
    E = pe.shape[1]
    P = D // 2
    NG = len(POOL_WINDOWS)
    PG = P // NG
    G, N, C = P // SSM_GROUP, SSM_STATE, SSM_GROUP
    GT = min(SSM_TILE_GROUPS, G)
    Q = D // N_CHIP

    big = {"w_in": (w_in, m_w_in, v_w_in), "w_pool": (w_pool, m_w_pool, v_w_pool),
           "w_glu": (w_glu, m_w_glu, v_w_glu), "w_out": (w_out, m_w_out, v_w_out),
           "w_ple": (w_ple, m_w_ple, v_w_ple), "w_ple_gate": (w_ple_gate, m_w_ple_gate, v_w_ple_gate)}
    big_names = list(big)
    shard2d = {n: (big[n][0].size // big[n][0].shape[-1], big[n][0].shape[-1]) for n in big_names}
    shard_axis = {"w_in": 1, "w_pool": 1, "w_glu": 1, "w_out": 0, "w_ple": 1, "w_ple_gate": 0}
    shard16 = {n: big[n][0][0].astype(BF16) for n in big_names}
    place = jnp.stack([2 * lax.axis_index("x") + lax.axis_index("y"), lax.axis_index("c")]).astype(jnp.int32)
    mx, my = lax.axis_index("x"), lax.axis_index("y")
    block_order = jnp.stack([2 * mx + my, 2 * (1 - mx) + my, 2 * mx + (1 - my),
                             2 * (1 - mx) + (1 - my)]).astype(jnp.int32)
    later = [n for n in big_names if n != "w_in"]
    ag_later = _ag_plan([shard16[n] for n in later], [shard_axis[n] for n in later])

    rep = lambda a: jnp.repeat(a, C, axis=0)
    a_re_r, a_im_r = rep(a_re[0]), rep(a_im[0])
    ldt_r = rep(jnp.broadcast_to(log_dt[0][:, None], (G, N)))
    bt_re = b_re[0].transpose(0, 2, 1).reshape(G * C, N)
    bt_im = b_im[0].transpose(0, 2, 1).reshape(G * C, N)
    ab_re_r, ab_im_r, bbt_re, bbt_im = _ssm_prep(a_re_r, a_im_r, ldt_r, bt_re, bt_im)
    abr = ab_re_r[::C].reshape(1, G * N)
    abi = ab_im_r[::C].reshape(1, G * N)
    bdr = _block_diag(bbt_re.reshape(G, C, N), GT).astype(BF16)
    bdi = _block_diag(bbt_im.reshape(G, C, N), GT).astype(BF16)
    cdr = _block_diag(c_re[0].transpose(0, 2, 1), GT).astype(BF16)
    cdi = _block_diag(c_im[0].transpose(0, 2, 1), GT).astype(BF16)

    tb = _t(T, 256)
    tbs = _t(T, 256)
    tm = _t(T, 1024)
    tk = _t(T, 2048)
    DH = _t(D, 1024)
    row_k = lambda i, n, k: (i, k)
    row_n = lambda i, n, k: (i, n)
    f32 = lambda *shape: jax.ShapeDtypeStruct(shape, F32)
    hn, proj, win = _proj_ag(xs, norm_gain, shard16["w_in"], block_order, tm)
    y, ge, bsr, bsi, wp, wglu, wout, wple, wpg = _ssm_fwd(proj, bdr, bdi, cdr, cdi, abr, abi, d_skip, P, tbs,
                                                          comm=ag_later)
    pooled, mixed = _pool_fwd(proj, wp, P, tb)
    hg = _mm(ge, wglu, dims=NN, grid=(T // tm, 2 * P // DH, 1),
             a_spec=_bs((tm, P), row_k), b_spec=_bs((P, DH), lambda i, n, k: (k, n)),
             o_spec=_bs((tm, DH), row_n), out_shape=f32(T, 2 * P), name="mm_glu")
    cat = _gate_fwd(mixed, proj, hg, pool_scale, tb)
    h1, h1b = _mm(cat, wout, dims=NN, grid=(T // tm, D // DH, 1), res=xs, bf16_copy=True,
                  a_spec=_bs((tm, D), row_k), b_spec=_bs((D, DH), lambda i, n, k: (k, n)),
                  r_spec=_bs((tm, DH), row_n), o_spec=_bs((tm, DH), row_n), out_shape=f32(T, D), name="mm_out")
    z = _mm(h1b, wpg, dims=NN, grid=(T // tm, D // DH, 1),
            a_spec=_bs((tm, D), row_k), b_spec=_bs((D, DH), lambda i, n, k: (k, n)),
            o_spec=_bs((tm, DH), row_n), out_shape=f32(T, D), name="mm_pgate")
    dh2, de, dz, dg2, lpart = _final_fb(h1, pe, wple, z, tgt, final_gain.reshape(1, D), tb)

    col_m = lambda m, n, k: (k, m)
    col_n = lambda m, n, k: (k, n)
    dh1, dh1b = _mm(dz, wpg, dims=NT, grid=(T // tm, D // DH, 1), res=dh2, bf16_copy=True,
                    a_spec=_bs((tm, D), row_k), b_spec=_bs((DH, D), lambda i, n, k: (n, k)),
                    r_spec=_bs((tm, DH), row_n), o_spec=_bs((tm, DH), row_n), out_shape=f32(T, D), name="mm_dh1")
    g_wpg, g_wpg16 = _mm(h1b, dz, dims=TN, grid=(D // DH, D // DH, T // tk), bf16_copy=True,
                         a_spec=_bs((tk, DH), col_m), b_spec=_bs((tk, DH), col_n),
                         o_spec=_bs((DH, DH), lambda m, n, k: (m, n)), out_shape=f32(D, D), name="mm_gwpg")
    g_wple, g_wple16 = _mm(pe, de, dims=TN, grid=(1, N_CHIP, T // tk), bf16_copy=True,
                           a_spec=_bs((tk, E), col_m), b_spec=_bs((tk, Q), col_n),
                           o_spec=_bs((None, E, Q), lambda m, j, k: (j, 0, 0)), out_shape=f32(N_CHIP, E, Q),
                           name="mm_gwple")
    dcat = _mm(dh1b, wout, dims=NT, grid=(T // tm, D // DH, 1),
               a_spec=_bs((tm, D), row_k), b_spec=_bs((DH, D), lambda i, n, k: (n, k)),
               o_spec=_bs((tm, DH), row_n), out_shape=f32(T, D), name="mm_dcat")
    g_wout, g_wout16 = _mm(cat, dh1b, dims=TN, grid=(D // DH, D // DH, T // tk), bf16_copy=True,
                           a_spec=_bs((tk, DH), col_m), b_spec=_bs((tk, DH), col_n),
                           o_spec=_bs((DH, DH), lambda m, n, k: (m, n)), out_shape=f32(D, D), name="mm_gwout")
    gbig = {"w_out": g_wout.reshape(N_CHIP, Q, D), "w_ple": g_wple, "w_ple_gate": g_wpg.reshape(N_CHIP, Q, D)}
    gbig16 = {"w_out": g_wout16.reshape(N_CHIP, Q, D), "w_ple": g_wple16,
              "w_ple_gate": g_wpg16.reshape(N_CHIP, Q, D)}
    first = list(gbig)
    res = _gate_bwd(dcat, mixed, proj, hg, pool_scale, tb, comm=_halves_plan([gbig16[n] for n in first]))
    dmixed, dpg, dsg, dhg, dps = res[:5]
    got = dict(zip(first, res[5:]))
    dge = _mm(dhg, wglu, dims=NT, grid=(T // tm, 1, 1),
              a_spec=_bs((tm, 2 * P), row_k), b_spec=_bs((P, 2 * P), lambda i, n, k: (n, k)),
              o_spec=_bs((tm, P), row_n), out_shape=f32(T, P), name="mm_dge")
    gbig["w_glu"], gbig16["w_glu"] = _mm(ge, dhg, dims=TN, grid=(1, N_CHIP, T // tk), bf16_copy=True,
                                         a_spec=_bs((tk, P), col_m), b_spec=_bs((tk, Q), col_n),
                                         o_spec=_bs((None, P, Q), lambda m, j, k: (j, 0, 0)),
                                         out_shape=f32(N_CHIP, P, Q), name="mm_gwglu")
    g_wp = _mm(pooled, dmixed, dims=TN, grid=(NG, 1, T // tk), bf16_copy=True,
               a_spec=_bs((tk, PG), col_m), b_spec=_bs((tk, PG), col_m),
               o_spec=_bs((None, PG, PG), lambda g, n, k: (g, 0, 0)), out_shape=f32(NG, PG, PG), name="mm_gwp")
    by_chip = lambda a: a.reshape(NG, N_CHIP, PG // N_CHIP, PG).transpose(1, 0, 2, 3).reshape(
        N_CHIP, NG * PG // N_CHIP, PG)
    gbig["w_pool"], gbig16["w_pool"] = by_chip(g_wp[0]), by_chip(g_wp[1])
    res = _pool_bwd(dmixed, wp, tb, comm=_halves_plan([gbig16["w_pool"], gbig16["w_glu"]]))
    dpi, got["w_pool"], got["w_glu"] = res
    early = list(gbig)
    chip_sums = {n: _sum_cast(gbig[n], got[n], place, "sum_cast_" + n) for n in early}
    res = _ssm_bwd(proj, y, dge, bsr, bsi, bdr, bdi, cdr, cdi, abr, abi, d_skip, dpi, dpg, dsg, P, tbs,
                   comm=_scatter_plan([chip_sums[n] for n in early]))
    dproj, dabr, dabi, dd, dbdr, dbdi, dcdr, dcdi = res[:8]
    arrived = dict(zip(early, res[8:]))
    halves = [_sum_chips(chip_sums[n], arrived[n], place, "sum_chips_" + n) for n in early]
    res = _mm(hn, dproj, dims=TN, grid=(D // DH, N_CHIP, T // tk), bf16_copy=True,
              a_spec=_bs((tk, DH), col_m), b_spec=_bs((tk, P), col_n),
              o_spec=_bs((None, DH, P), lambda m, j, k: (j, m, 0)), out_shape=f32(N_CHIP, D, P),
              name="mm_gwin", comm=_join_plan(halves))
    gbig["w_in"], gbig16["w_in"], gshard = res[0], res[1], dict(zip(early, res[2:]))
    got["w_in"], = _comm_call(_halves_plan([gbig16["w_in"]]), "rs_halves_late")
    chip_sums["w_in"] = _sum_cast(gbig["w_in"], got["w_in"], place, "sum_cast_w_in")
    KH = _t(4 * P, 2048)
    dhn, arrived["w_in"] = _mm(dproj, win, dims=NT, grid=(T // tm, D // DH, 4 * P // KH),
                               a_spec=_bs((tm, KH), row_k), b_spec=_bs((DH, KH), lambda i, n, k: (n, k)),
                               o_spec=_bs((tm, DH), row_n), out_shape=f32(T, D), name="mm_dhn",
                               comm=_scatter_plan([chip_sums["w_in"]]))
    gshard["w_in"], = _comm_call(
        _join_plan([_sum_chips(chip_sums["w_in"], arrived["w_in"], place, "sum_chips_w_in")]), "rs_join_w_in")
    grad_x, dg1 = _norm1_bwd(xs, dhn, dh1, norm_gain, tb)

    dbbt_re = _block_diag_extract(dbdr, GT, C, N).reshape(G * C, N)
    dbbt_im = _block_diag_extract(dbdi, GT, C, N).reshape(G * C, N)
    g_c_re = _block_diag_extract(dcdr, GT, N, C).transpose(0, 2, 1)
    g_c_im = _block_diag_extract(dcdi, GT, N, C).transpose(0, 2, 1)
    dab_re_r = rep(dabr.reshape(G, N)) * (1.0 / C)
    dab_im_r = rep(dabi.reshape(G, N)) * (1.0 / C)
    g_a_re, g_a_im, g_ldt, g_bt_re, g_bt_im = _ssm_prep_bwd(a_re_r, a_im_r, ldt_r, bt_re, bt_im,
                                                            dab_re_r, dab_im_r, dbbt_re, dbbt_im, G)
    g_b_re = g_bt_re.reshape(G, C, N).transpose(0, 2, 1)
    g_b_im = g_bt_im.reshape(G, C, N).transpose(0, 2, 1)


    small_names = ["norm_gain", "pool_scale", "a_re", "a_im", "log_dt", "b_re", "b_im", "c_re", "c_im",
                   "d_skip", "final_gain"]
    small_w = dict(norm_gain=norm_gain, pool_scale=pool_scale, a_re=a_re, a_im=a_im, log_dt=log_dt, b_re=b_re,
                   b_im=b_im, c_re=c_re, c_im=c_im, d_skip=d_skip, final_gain=final_gain)
    small_m = dict(norm_gain=m_norm_gain, pool_scale=m_pool_scale, a_re=m_a_re, a_im=m_a_im, log_dt=m_log_dt,
                   b_re=m_b_re, b_im=m_b_im, c_re=m_c_re, c_im=m_c_im, d_skip=m_d_skip, final_gain=m_final_gain)
    small_v = dict(norm_gain=v_norm_gain, pool_scale=v_pool_scale, a_re=v_a_re, a_im=v_a_im, log_dt=v_log_dt,
                   b_re=v_b_re, b_im=v_b_im, c_re=v_c_re, c_im=v_c_im, d_skip=v_d_skip, final_gain=v_final_gain)
    small_g = dict(norm_gain=dg1, pool_scale=dps, a_re=g_a_re, a_im=g_a_im, log_dt=g_ldt, b_re=g_b_re,
                   b_im=g_b_im, c_re=g_c_re, c_im=g_c_im, d_skip=dd, final_gain=dg2)
    shapes = [small_w[n].shape for n in small_names]
    loss_row = sum(_tile_rows(small_w[n].size) for n in small_names)
    unit = N_DEV * SUBLANES
    rows = -(-(loss_row + SUBLANES) // unit) * unit
    gbuf = _pack_small([small_g[n] for n in small_names] + [lpart[0, :1]], rows)
    gsum, = _comm_call(_allreduce_plan(gbuf), "allreduce_small")
    g_out, d_out, m_out, v_out = {}, {}, {}, {}
    for n in big_names:
        w_, m_, v_ = big[n]
        r2 = shard2d[n]
        res = _adamw(w_.reshape(r2), gshard[n], m_.reshape(r2), v_.reshape(r2), "adamw_" + n)
        g_out[n], d_out[n], m_out[n], v_out[n] = (a.reshape(w_.shape) for a in (gshard[n], *res))
    wbuf = _pack_small([small_w[n] for n in small_names], rows)
    mbuf = _pack_small([small_m[n] for n in small_names], rows)
    vbuf = _pack_small([small_v[n] for n in small_names], rows)
    dsm, msm, vsm = _adamw(wbuf, gsum, mbuf, vbuf, "adamw_small")
    g_small = dict(zip(small_names, _unpack_small(gsum, shapes)))
    d_small = dict(zip(small_names, _unpack_small(dsm, shapes)))
    m_small = dict(zip(small_names, _unpack_small(msm, shapes)))
    v_small = dict(zip(small_names, _unpack_small(vsm, shapes)))
    loss = gsum[loss_row, 0]

    g_out.update(g_small)
    d_out.update(d_small)
    m_out.update(m_small)
    v_out.update(v_small)

    order = ["norm_gain", "w_in", "w_pool", "pool_scale", "a_re", "a_im", "log_dt", "b_re", "b_im", "c_re",
             "c_im", "d_skip", "w_glu", "w_out", "w_ple", "w_ple_gate", "final_gain"]
    return (loss, grad_x[None], *[g_out[n] for n in order], *[d_out[n] for n in order],
            *[m_out[n] for n in order], *[v_out[n] for n in order])
```

```python
import functools

import jax
import jax.numpy as jnp
from jax import lax
from jax.experimental import pallas as pl
from jax.experimental.pallas import tpu as pltpu

F32, BF16 = jnp.float32, jnp.bfloat16
MESH = pl.DeviceIdType.MESH
ANY = pl.BlockSpec(memory_space=pl.ANY)
VMEM_FULL = pl.BlockSpec(memory_space=pltpu.VMEM)

EPS = 1e-6
A_RE_MAX = -1e-4
SSM_GROUP = 16
SSM_STATE = 64
POOL_WINDOWS = (2, 4, 8, 16)
POOL_HALO = 16
ADAM_LR, ADAM_B1, ADAM_B2, ADAM_EPS, ADAM_WD, ADAM_STEP = 0.001, 0.9, 0.999, 1e-08, 0.01, 10

V7X_VMEM_BYTES = 64 * 1024 * 1024
VMEM_LIMIT = V7X_VMEM_BYTES - 8 * 1024 * 1024
SUBLANES, LANES = 8, 128
SSM_TILE_GROUPS = 8
SCAN_LANES = 512
N_DEV, N_CHIP = 8, 4
DMA_CHUNK_BYTES = 256 * 1024
DMA_MAX_CHUNKS = 32
RS_CHUNKS = 8


def _t(n, pref):
    return pref if n % pref == 0 else n


def _cp(sem=None, vmem=VMEM_LIMIT):
    return pltpu.CompilerParams(dimension_semantics=sem, vmem_limit_bytes=vmem)


def _call(body, **kw):
    return pl.pallas_call(body, **kw)


NN = ((1,), (0,))
NT = ((1,), (1,))
TN = ((0,), (0,))


def _mm(a, b, *, dims, grid, a_spec, b_spec, o_spec, out_shape, name, res=None, r_spec=None, bf16_copy=False,
        comm=None):
    nk, kax = grid[-1], len(grid) - 1
    acc_shape = tuple(d for d in o_spec.block_shape if d is not None)

    def core(*refs):
        refs = list(refs)
        a_ref, b_ref = refs[:2]
        r_ref = refs[2] if res is not None else None
        outs = refs[3 if res is not None else 2:]
        o_ref = outs[0]
        o2_ref = outs[1] if bf16_copy else None
        acc = outs[-1] if nk > 1 else None

        def finish(r):
            if r_ref is not None:
                r = r + r_ref[...]
            o_ref[...] = r.astype(o_ref.dtype)
            if o2_ref is not None:
                o2_ref[...] = r.astype(BF16)

        part = lax.dot_general(a_ref[...].astype(BF16), b_ref[...].astype(BF16),
                               (dims, ((), ())), preferred_element_type=F32)
        if nk == 1:
            finish(part)
        else:
            k = pl.program_id(kax)

            @pl.when(k == 0)
            def _():
                acc[...] = part

            @pl.when(k > 0)
            def _():
                acc[...] += part

            @pl.when(k == nk - 1)
            def _():
                finish(acc[...])

    ins, specs = [a, b], [a_spec, b_spec]
    if res is not None:
        ins.append(res)
        specs.append(r_spec)
    o_specs, o_shapes = [o_spec], [out_shape]
    if bf16_copy:
        o_specs = [o_spec, o_spec]
        o_shapes = [out_shape, jax.ShapeDtypeStruct(out_shape.shape, BF16)]
    scratch = [pltpu.VMEM(acc_shape, F32)] if nk > 1 else []
    body, extra = _hosted(core, comm, grid, len(ins), len(o_specs), len(scratch))
    sem = ("arbitrary",) * len(grid) if comm else ("parallel",) * kax + ("arbitrary",)
    outs = _call(body, grid=grid, in_specs=specs + extra["in_specs"], out_specs=o_specs + extra["out_specs"],
                 out_shape=o_shapes + extra["out_shape"], scratch_shapes=scratch + extra["scratch"],
                 input_output_aliases=extra["aliases"],
                 compiler_params=_cp(sem), name=name)(*ins, *extra["ins"])
    return outs[0] if len(outs) == 1 else outs


def _bs(shape, fn):
    return pl.BlockSpec(shape, fn)


def _sigmoid(v):
    return 1.0 / (1.0 + jnp.exp(-v))


def _gelu(v):
    return 0.5 * v * (1.0 + jnp.tanh(0.7978845608028654 * (v + 0.044715 * v * v * v)))


def _gelu_grad(v):
    t = jnp.tanh(0.7978845608028654 * (v + 0.044715 * v * v * v))
    return 0.5 * (1.0 + t) + 0.5 * v * (1.0 - t * t) * 0.7978845608028654 * (1.0 + 3 * 0.044715 * v * v)


def _norm1_bwd(x, dhn, dh1, g1, tb, comm=None):
    T, D = x.shape

    def core(x_ref, dhn_ref, dh1_ref, g_ref, dx_ref, dg_ref):
        @pl.when(pl.program_id(0) == 0)
        def _():
            dg_ref[...] = jnp.zeros_like(dg_ref)

        xv = x_ref[...]
        r = lax.rsqrt(jnp.mean(xv * xv, axis=-1, keepdims=True) + EPS)
        xh = xv * r
        dhn_v = dhn_ref[...]
        dg_ref[...] += jnp.sum(dhn_v * xh, axis=0, keepdims=True)
        dxh = dhn_v * g_ref[...]
        dx_ref[...] = dh1_ref[...] + r * (dxh - xh * jnp.mean(dxh * xh, axis=-1, keepdims=True))

    row = _bs((tb, D), lambda i: (i, 0))
    vec = _bs((1, D), lambda i: (0, 0))
    body, extra = _hosted(core, comm, (T // tb,), 4, 2, 0)
    return _call(body, grid=(T // tb,), in_specs=[row, row, row, vec] + extra["in_specs"],
                 out_specs=[row, vec] + extra["out_specs"],
                 out_shape=[jax.ShapeDtypeStruct((T, D), F32), jax.ShapeDtypeStruct((1, D), F32)] + extra["out_shape"],
                 scratch_shapes=extra["scratch"], input_output_aliases=extra["aliases"],
                 compiler_params=_cp(("arbitrary",)), name="norm1_bwd")(x, dhn, dh1, g1, *extra["ins"])


def _gate_fwd(mixed, proj, hg, ps, tb):
    T, P = mixed.shape

    def body(mx_ref, pg_ref, sg_ref, hg_ref, ps_ref, o_ref):
        pg, sg = pg_ref[...], sg_ref[...]
        ya = (mx_ref[...] * ps_ref[...]) * (pg * _sigmoid(pg))
        hgv = hg_ref[...]
        o = hgv[:, :P] * _sigmoid(hgv[:, P:])
        yb = o * (sg * _sigmoid(sg))
        o_ref[:, :P] = ya.astype(BF16)
        o_ref[:, P:] = yb.astype(BF16)

    return _call(body, grid=(T // tb,),
                 in_specs=[_bs((tb, P), lambda i: (i, 0)), _bs((tb, P), lambda i: (i, 1)),
                           _bs((tb, P), lambda i: (i, 3)), _bs((tb, 2 * P), lambda i: (i, 0)),
                           _bs((1, P), lambda i: (0, 0))],
                 out_specs=_bs((tb, 2 * P), lambda i: (i, 0)),
                 out_shape=jax.ShapeDtypeStruct((T, 2 * P), BF16),
                 compiler_params=_cp(("parallel",)), name="gate_fwd")(mixed, proj, proj, hg, ps)


def _gate_bwd(dcat, mixed, proj, hg, ps, tb, comm=None):
    T, P = mixed.shape

    def core(dc_ref, mx_ref, pg_ref, sg_ref, hg_ref, ps_ref, dmx_ref, dpg_ref, dsg_ref, dhg_ref, dps_ref):
        @pl.when(pl.program_id(0) == 0)
        def _():
            dps_ref[...] = jnp.zeros_like(dps_ref)

        dc = dc_ref[...]
        dya, dyb = dc[:, :P], dc[:, P:]
        pg, sg, mx, psv = pg_ref[...], sg_ref[...], mx_ref[...], ps_ref[...]
        s_pg = _sigmoid(pg)
        dpa = dya * (pg * s_pg)
        dpg_ref[...] = (dya * (mx * psv) * (s_pg * (1.0 + pg * (1.0 - s_pg)))).astype(BF16)
        dps_ref[...] += jnp.sum(dpa * mx, axis=0, keepdims=True)
        dmx_ref[...] = (dpa * psv).astype(BF16)
        hgv = hg_ref[...]
        h1, s_h2 = hgv[:, :P], _sigmoid(hgv[:, P:])
        s_sg = _sigmoid(sg)
        do = dyb * (sg * s_sg)
        dsg_ref[...] = (dyb * (h1 * s_h2) * (s_sg * (1.0 + sg * (1.0 - s_sg)))).astype(BF16)
        dhg_ref[:, :P] = (do * s_h2).astype(BF16)
        dhg_ref[:, P:] = (do * h1 * s_h2 * (1.0 - s_h2)).astype(BF16)

    rowp = _bs((tb, P), lambda i: (i, 0))
    row2 = _bs((tb, 2 * P), lambda i: (i, 0))
    vec = _bs((1, P), lambda i: (0, 0))
    body, extra = _hosted(core, comm, (T // tb,), 6, 5, 0)
    return _call(body, grid=(T // tb,),
                 in_specs=[row2, rowp, _bs((tb, P), lambda i: (i, 1)), _bs((tb, P), lambda i: (i, 3)), row2, vec]
                 + extra["in_specs"],
                 out_specs=[rowp, rowp, rowp, row2, vec] + extra["out_specs"],
                 out_shape=[jax.ShapeDtypeStruct((T, P), BF16), jax.ShapeDtypeStruct((T, P), BF16),
                            jax.ShapeDtypeStruct((T, P), BF16), jax.ShapeDtypeStruct((T, 2 * P), BF16),
                            jax.ShapeDtypeStruct((1, P), F32)] + extra["out_shape"],
                 scratch_shapes=extra["scratch"],
                 compiler_params=_cp(("arbitrary",)), name="gate_bwd")(dcat, mixed, proj, proj, hg, ps, *extra["ins"])


def _final_fb(h1, pe, wple, z, tgt, g2, tb):
    T, D = h1.shape
    E = pe.shape[1]

    def body(h1_ref, p_ref, w_ref, z_ref, t_ref, g_ref, dh2_ref, de_ref, dz_ref, dg_ref, l_ref):
        @pl.when(pl.program_id(0) == 0)
        def _():
            dg_ref[...] = jnp.zeros_like(dg_ref)
            l_ref[...] = jnp.zeros_like(l_ref)

        ev = jnp.dot(p_ref[...].astype(BF16), w_ref[...], preferred_element_type=F32)
        s = _sigmoid(z_ref[...])
        h2 = h1_ref[...] + ev * s
        r = lax.rsqrt(jnp.mean(h2 * h2, axis=-1, keepdims=True) + EPS)
        xh = h2 * r
        gv = g_ref[...]
        diff = xh * gv - t_ref[...]
        l_ref[...] += 0.5 * jnp.sum(jnp.mean(diff * diff, axis=-1, keepdims=True))
        dout = diff * (1.0 / D)
        dg_ref[...] += jnp.sum(dout * xh, axis=0, keepdims=True)
        dxh = dout * gv
        dh2 = r * (dxh - xh * jnp.mean(dxh * xh, axis=-1, keepdims=True))
        dh2_ref[...] = dh2
        de_ref[...] = (dh2 * s).astype(BF16)
        dz_ref[...] = (dh2 * ev * s * (1.0 - s)).astype(BF16)

    row = _bs((tb, D), lambda i: (i, 0))
    vec = _bs((1, D), lambda i: (0, 0))
    return _call(body, grid=(T // tb,),
                 in_specs=[row, _bs((tb, E), lambda i: (i, 0)), _bs((E, D), lambda i: (0, 0)), row, row, vec],
                 out_specs=[row, row, row, vec, _bs((1, LANES), lambda i: (0, 0))],
                 out_shape=[jax.ShapeDtypeStruct((T, D), F32), jax.ShapeDtypeStruct((T, D), BF16),
                            jax.ShapeDtypeStruct((T, D), BF16), jax.ShapeDtypeStruct((1, D), F32),
                            jax.ShapeDtypeStruct((1, LANES), F32)],
                 compiler_params=_cp(("arbitrary",)), name="final_fb")(h1, pe, wple, z, tgt, g2)


def _pool_inv_count(t0, rows, pg, ngroups):
    t = t0 + lax.broadcasted_iota(jnp.int32, (rows, pg), 0)
    parts = []
    for w in POOL_WINDOWS[:ngroups]:
        parts.append(jnp.where(t + 1 >= w, 1.0 / w, 1.0 / (t + 1).astype(F32)))
    return parts


def _pool_fwd(proj, wp, P, tb):
    T = proj.shape[0]
    ng = len(POOL_WINDOWS)
    pg = P // ng
    hb = tb // POOL_HALO

    def body(v_ref, tail_ref, w_ref, o_ref, mx_ref, ext):
        i = pl.program_id(0)
        ext[pl.ds(0, POOL_HALO), :] = jnp.where(i > 0, tail_ref[...], 0.0)
        ext[pl.ds(POOL_HALO, tb), :] = v_ref[...]
        inv = _pool_inv_count(i * tb, tb, pg, ng)
        for g, w in enumerate(POOL_WINDOWS):
            cols = pl.ds(g * pg, pg)
            win = ext[pl.ds(POOL_HALO, tb), cols]
            for k in range(1, w):
                win = win + ext[pl.ds(POOL_HALO - k, tb), cols]
            pooled = (win * inv[g] - ext[pl.ds(POOL_HALO, tb), cols]).astype(BF16)
            o_ref[:, cols] = pooled
            mx_ref[:, cols] = jnp.dot(pooled, w_ref[g], preferred_element_type=F32)

    row = _bs((tb, P), lambda i: (i, 0))
    return _call(body, grid=(T // tb,),
                 in_specs=[row, _bs((POOL_HALO, P), lambda i: (jnp.maximum(i * hb - 1, 0), 0)),
                           _bs(wp.shape, lambda i: (0, 0, 0))],
                 out_specs=[row, row],
                 out_shape=[jax.ShapeDtypeStruct((T, P), BF16), jax.ShapeDtypeStruct((T, P), F32)],
                 scratch_shapes=[pltpu.VMEM((tb + POOL_HALO, P), F32)],
                 compiler_params=_cp(("arbitrary",)), name="pool_fwd")(proj, proj, wp)


def _pool_bwd(dmixed, wp, tb, comm=None):
    T, P = dmixed.shape
    ng = len(POOL_WINDOWS)
    pg = P // ng
    hb = tb // POOL_HALO
    nb = T // tb

    def core(d_ref, head_ref, w_ref, o_ref, ext, dpl):
        i = pl.program_id(0)
        inv = _pool_inv_count(i * tb, tb, pg, ng)
        invh = _pool_inv_count((i + 1) * tb, POOL_HALO, pg, ng)
        for g in range(ng):
            cols = pl.ds(g * pg, pg)
            dp = lax.dot_general(d_ref[:, cols], w_ref[g], (NT, ((), ())), preferred_element_type=F32)
            dph = lax.dot_general(head_ref[:, cols], w_ref[g], (NT, ((), ())), preferred_element_type=F32)
            dpl[:, cols] = dp
            ext[pl.ds(0, tb), cols] = dp * inv[g]
            ext[pl.ds(tb, POOL_HALO), cols] = jnp.where(i < nb - 1, dph * invh[g], 0.0)
        for g, w in enumerate(POOL_WINDOWS):
            cols = pl.ds(g * pg, pg)
            acc = ext[pl.ds(0, tb), cols]
            for k in range(1, w):
                acc = acc + ext[pl.ds(k, tb), cols]
            o_ref[:, cols] = (acc - dpl[:, cols]).astype(BF16)

    body, extra = _hosted(core, comm, (nb,), 3, 1, 2)
    return _call(body, grid=(nb,),
                 in_specs=[_bs((tb, P), lambda i: (i, 0)),
                           _bs((POOL_HALO, P), lambda i: (jnp.minimum((i + 1) * hb, T // POOL_HALO - 1), 0)),
                           _bs(wp.shape, lambda i: (0, 0, 0))] + extra["in_specs"],
                 out_specs=[_bs((tb, P), lambda i: (i, 0))] + extra["out_specs"],
                 out_shape=[jax.ShapeDtypeStruct((T, P), BF16)] + extra["out_shape"],
                 scratch_shapes=[pltpu.VMEM((tb + POOL_HALO, P), F32), pltpu.VMEM((tb, P), F32)] + extra["scratch"],
                 compiler_params=_cp(("arbitrary",)), name="pool_bwd")(dmixed, dmixed, wp, *extra["ins"])


def _zoh(a_re, a_im, ldt, b_re, b_im):
    lam_re = jnp.minimum(a_re, A_RE_MAX)
    lam_im = a_im
    dt = jnp.exp(ldt)
    mag = jnp.exp(lam_re * dt)
    ang = lam_im * dt
    ab_re = mag * jnp.cos(ang)
    ab_im = mag * jnp.sin(ang)
    den = lam_re * lam_re + lam_im * lam_im
    n_re = ab_re - 1.0
    n_im = ab_im
    q_re = (n_re * lam_re + n_im * lam_im) / den
    q_im = (n_im * lam_re - n_re * lam_im) / den
    return ab_re, ab_im, q_re * b_re - q_im * b_im, q_re * b_im + q_im * b_re


def _ssm_prep(a_re, a_im, ldt, bt_re, bt_im):
    shp = jax.ShapeDtypeStruct(a_re.shape, F32)

    def body(a, b, c, d, e, o0, o1, o2, o3):
        r = _zoh(a[...], b[...], c[...], d[...], e[...])
        o0[...], o1[...], o2[...], o3[...] = r

    return _call(body, in_specs=[VMEM_FULL] * 5, out_specs=[VMEM_FULL] * 4, out_shape=[shp] * 4,
                 name="ssm_prep")(a_re, a_im, ldt, bt_re, bt_im)


def _ssm_prep_bwd(a_re, a_im, ldt, bt_re, bt_im, dab_re, dab_im, dbb_re, dbb_im, G):
    GC, N = a_re.shape
    C = GC // G

    def body(a, b, c, d, e, g0, g1, g2, g3, da_re, da_im, dldt, db_re, db_im):
        _, vjp = jax.vjp(_zoh, a[...], b[...], c[...], d[...], e[...])
        ga_re, ga_im, gl, gb_re, gb_im = vjp((g0[...], g1[...], g2[...], g3[...]))
        da_re[...] = jnp.sum(ga_re.reshape(G, C, N), axis=1)
        da_im[...] = jnp.sum(ga_im.reshape(G, C, N), axis=1)
        dldt[...] = jnp.sum(jnp.sum(gl.reshape(G, C, N), axis=1), axis=1, keepdims=True)
        db_re[...] = gb_re
        db_im[...] = gb_im

    gn = jax.ShapeDtypeStruct((G, N), F32)
    full = jax.ShapeDtypeStruct((GC, N), F32)
    return _call(body, in_specs=[VMEM_FULL] * 9, out_specs=[VMEM_FULL] * 5,
                 out_shape=[gn, gn, jax.ShapeDtypeStruct((G, 1), F32), full, full],
                 name="ssm_prep_bwd")(a_re, a_im, ldt, bt_re, bt_im, dab_re, dab_im, dbb_re, dbb_im)


def _coef_tiles(abr, abi, reverse):
    ns = abr.shape[1]
    row = lax.broadcasted_iota(jnp.int32, (SUBLANES, ns), 0)
    ar = jnp.broadcast_to(abr, (SUBLANES, ns))
    ai = jnp.broadcast_to(-abi if reverse else abi, (SUBLANES, ns))
    a2r, a2i = ar * ar - ai * ai, 2.0 * ar * ai
    a4r, a4i = a2r * a2r - a2i * a2i, 2.0 * a2r * a2i
    out = []
    for d, (vr, vi) in ((1, (ar, ai)), (2, (a2r, a2i)), (4, (a4r, a4i))):
        keep = (row < SUBLANES - d) if reverse else (row >= d)
        out += [jnp.where(keep, vr, 0.0), jnp.where(keep, vi, 0.0)]
    pr, pi = ar, ai
    for k in range(1, SUBLANES):
        sel = (row <= SUBLANES - 1 - k) if reverse else (row >= k)
        nr, ni = pr * ar - pi * ai, pr * ai + pi * ar
        pr, pi = jnp.where(sel, nr, pr), jnp.where(sel, ni, pi)
    return out + [pr, pi]


def _cpow(ar, ai, n):
    out, br, bi = None, ar, ai
    while n:
        if n & 1:
            out = (br, bi) if out is None else (out[0] * br - out[1] * bi, out[0] * bi + out[1] * br)
        br, bi = br * br - bi * bi, 2.0 * br * bi
        n >>= 1
    return out


def _seg_perm_matrix(nrows):
    r = jnp.arange(nrows)
    src = (nrows // SUBLANES) * (r % SUBLANES) + r // SUBLANES
    return (src[:, None] == jnp.arange(nrows)[None, :]).astype(BF16)


def _seg_order_rows(pm, xb):
    return jnp.dot(pm, xb, preferred_element_type=F32).astype(BF16)


def _time_order_rows(pm, x, terms):
    out, rest = None, x
    for t in range(terms):
        piece = rest.astype(BF16)
        part = lax.dot_general(pm, piece, (TN, ((), ())), preferred_element_type=F32)
        out = part if out is None else out + part
        if t + 1 < terms:
            rest = rest - piece.astype(F32)
    return out


def _scan_tiles(abr, abi, seg, reverse):
    ns = abr.shape[1]
    seg_pow = _cpow(abr, abi, seg)
    step = [jnp.broadcast_to(abr, (SUBLANES, ns)), jnp.broadcast_to(-abi if reverse else abi, (SUBLANES, ns))]
    return _coef_tiles(seg_pow[0], seg_pow[1], reverse) + step


def _seg_scan(xr_ref, xi_ref, coef_ref, car_ref, cai_ref, *, nrows, ns, reverse, cmat=None, dab=None):
    seg = nrows // SUBLANES
    cw = min(SCAN_LANES, ns)
    row = lax.broadcasted_iota(jnp.int32, (SUBLANES, cw), 0)
    first, last = (SUBLANES - 1, 0) if reverse else (0, SUBLANES - 1)

    def tile(i):
        return pl.ds(pl.multiple_of(((seg - 1 - i) if reverse else i) * SUBLANES, SUBLANES), SUBLANES)

    for cc in range(ns // cw):
        cols = pl.ds(cc * cw, cw)
        ar, ai = coef_ref[8, :, cols], coef_ref[9, :, cols]

        def local(i, x, cols=cols, ar=ar, ai=ai):
            rows = tile(i)
            nr = ar * x[0] - ai * x[1] + xr_ref[rows, cols]
            ni = ar * x[1] + ai * x[0] + xi_ref[rows, cols]
            xr_ref[rows, cols] = nr
            xi_ref[rows, cols] = ni
            return nr, ni

        zero = jnp.zeros((SUBLANES, cw), F32)
        er, ei = lax.fori_loop(0, seg, local, (zero, zero))

        co = [coef_ref[k, :, cols] for k in range(8)]
        for lvl, d in enumerate((1, 2, 4)):
            kr, ki = co[2 * lvl], co[2 * lvl + 1]
            sh = SUBLANES - d if reverse else d
            sr, si = pltpu.roll(er, sh, 0), pltpu.roll(ei, sh, 0)
            er, ei = er + (kr * sr - ki * si), ei + (kr * si + ki * sr)
        c0r, c0i = car_ref[:, cols], cai_ref[:, cols]
        er, ei = er + (co[6] * c0r - co[7] * c0i), ei + (co[6] * c0i + co[7] * c0r)
        nb_shift = SUBLANES - 1 if reverse else 1
        cmr = jnp.where(row == first, c0r, pltpu.roll(er, nb_shift, 0))
        cmi = jnp.where(row == first, c0i, pltpu.roll(ei, nb_shift, 0))
        car_ref[:, cols] = jnp.broadcast_to(er[last:last + 1, :], er.shape)
        cai_ref[:, cols] = jnp.broadcast_to(ei[last:last + 1, :], ei.shape)
        if cmat is not None:
            cmat[0][:, cols] = cmr
            cmat[1][:, cols] = cmi

        w0 = (ar * cmr - ai * cmi, ar * cmi + ai * cmr)
        if dab is None:
            def fix(i, w, cols=cols, ar=ar, ai=ai):
                rows = tile(i)
                xr_ref[rows, cols] = xr_ref[rows, cols] + w[0]
                xi_ref[rows, cols] = xi_ref[rows, cols] + w[1]
                return ar * w[0] - ai * w[1], ar * w[1] + ai * w[0]

            lax.fori_loop(0, seg, fix, w0)
        else:
            s_re, s_im, e_re, e_im, o_re, o_im = dab

            def add(rows, w, pr, pi, acc):
                gr = xr_ref[rows, cols] + w[0]
                gi = xi_ref[rows, cols] + w[1]
                xr_ref[rows, cols] = gr
                xi_ref[rows, cols] = gi
                return acc[0] + (gr * pr + gi * pi), acc[1] + (gi * pr - gr * pi)

            def fix(i, st, cols=cols, ar=ar, ai=ai):
                w, acc = st[:2], st[2:]
                rows = tile(i)
                before = pl.ds(pl.multiple_of((seg - 2 - i) * SUBLANES, SUBLANES), SUBLANES)
                acc = add(rows, w, s_re[before, cols], s_im[before, cols], acc)
                return (ar * w[0] - ai * w[1], ar * w[1] + ai * w[0]) + acc

            st = lax.fori_loop(0, seg - 1, fix, w0 + (zero, zero))
            acc = add(pl.ds(0, SUBLANES), st[:2], e_re[:, cols], e_im[:, cols], st[2:])
            o_re[:, cols] += jnp.sum(acc[0], axis=0, keepdims=True)
            o_im[:, cols] += jnp.sum(acc[1], axis=0, keepdims=True)


def _hosted(core, comm, grid, n_in, n_out, n_scratch):
    ci = len(comm["ins"]) if comm else 0
    co = len(comm["out_shape"]) if comm else 0

    def body(*refs):
        ins, rest = refs[:n_in + ci], refs[n_in + ci:]
        outs, scr = rest[:n_out + co], rest[n_out + co:]
        hooks = functools.partial(_comm_hooks, comm, grid, ins[n_in:], outs[n_out:], scr[n_scratch:])
        hooks(before=True)
        core(*ins[:n_in], *outs[:n_out], *scr[:n_scratch])
        hooks(before=False)

    aliases = {n_in + i: n_out + i for i in range(co)} if comm and comm.get("alias") else {}
    extra = dict(ins=list(comm["ins"]) if comm else [], in_specs=[ANY] * ci, out_specs=[ANY] * co,
                 out_shape=list(comm["out_shape"]) if comm else [], scratch=list(comm["scratch"]) if comm else [],
                 aliases=aliases)
    return body, extra


def _ssm_fwd(proj, bdr, bdi, cdr, cdi, abr, abi, dsk, P, tb, comm=None):
    T = proj.shape[0]
    ntl, ct, st = bdr.shape
    ns = ntl * st
    nb = T // tb

    def core(u_ref, bdr_ref, bdi_ref, cdr_ref, cdi_ref, abr_ref, abi_ref, d_ref, pm_ref,
             y_ref, ge_ref, bsr_ref, bsi_ref, sr, si, coef, car, cai, up):
        @pl.when(pl.program_id(0) == 0)
        def _():
            for k, tile in enumerate(_scan_tiles(abr_ref[...], abi_ref[...], tb // SUBLANES, False)):
                coef[k] = tile
            car[...] = jnp.zeros_like(car)
            cai[...] = jnp.zeros_like(cai)

        bsr_ref[...] = car[...]
        bsi_ref[...] = cai[...]
        u = u_ref[...]
        ub = _seg_order_rows(pm_ref[...], u.astype(BF16))
        for s in range(ntl):
            us = ub[:, s * ct:(s + 1) * ct]
            sr[:, s * st:(s + 1) * st] = jnp.dot(us, bdr_ref[s], preferred_element_type=F32)
            si[:, s * st:(s + 1) * st] = jnp.dot(us, bdi_ref[s], preferred_element_type=F32)
        _seg_scan(sr, si, coef, car, cai, nrows=tb, ns=ns, reverse=False)
        for s in range(ntl):
            s_re = sr[:, s * st:(s + 1) * st].astype(BF16)
            s_im = si[:, s * st:(s + 1) * st].astype(BF16)
            up[:, s * ct:(s + 1) * ct] = (jnp.dot(s_re, cdr_ref[s], preferred_element_type=F32)
                                          - jnp.dot(s_im, cdi_ref[s], preferred_element_type=F32))
        y = _time_order_rows(pm_ref[...], up[...], 3) + d_ref[...] * u
        y_ref[...] = y
        ge_ref[...] = _gelu(y).astype(BF16)

    full3 = lambda a: _bs(a.shape, lambda i: (0, 0, 0))
    vec = lambda n: _bs((1, n), lambda i: (0, 0))
    row = _bs((tb, P), lambda i: (i, 0))
    st_spec = _bs((None, SUBLANES, ns), lambda i: (i, 0, 0))
    body, extra = _hosted(core, comm, (nb,), 9, 4, 6)
    return _call(body, grid=(nb,),
                 in_specs=[_bs((tb, P), lambda i: (i, 2)), full3(bdr), full3(bdi), full3(cdr), full3(cdi),
                           vec(ns), vec(ns), vec(P), _bs((tb, tb), lambda i: (0, 0))] + extra["in_specs"],
                 out_specs=[row, row, st_spec, st_spec] + extra["out_specs"],
                 out_shape=[jax.ShapeDtypeStruct((T, P), F32), jax.ShapeDtypeStruct((T, P), BF16),
                            jax.ShapeDtypeStruct((nb, SUBLANES, ns), F32),
                            jax.ShapeDtypeStruct((nb, SUBLANES, ns), F32)] + extra["out_shape"],
                 scratch_shapes=[pltpu.VMEM((tb, ns), F32), pltpu.VMEM((tb, ns), F32),
                                 pltpu.VMEM((10, SUBLANES, ns), F32),
                                 pltpu.VMEM((SUBLANES, ns), F32), pltpu.VMEM((SUBLANES, ns), F32),
                                 pltpu.VMEM((tb, P), F32)] + extra["scratch"],
                 compiler_params=_cp(("arbitrary",)), name="ssm_fwd")(
                     proj, bdr, bdi, cdr, cdi, abr, abi, dsk, _seg_perm_matrix(tb), *extra["ins"])


def _ssm_bwd(proj, y, dge, bsr, bsi, bdr, bdi, cdr, cdi, abr, abi, dsk, dpi, dpg, dsg, P, tb, comm=None):
    T = proj.shape[0]
    ntl, ct, st = bdr.shape
    ns = ntl * st
    nb = T // tb

    def core(u_ref, y_ref, dge_ref, bsr_ref, bsi_ref, abr_ref, abi_ref, d_ref, pm_ref, dpi_ref, dpg_ref, dsg_ref,
             bdr_h, bdi_h, cdr_h, cdi_h,
             dproj_ref, dabr_ref, dabi_ref, dd_ref, dbdr_h, dbdi_h, dcdr_h, dcdi_h,
             wbdr, wbdi, wcdr, wcdi, abdr, abdi, acdr, acdi, spr, spi, gr, gi, coef_f, coef_r,
             car, cai, gcr, gci, ser, sei, dup):
        i = pl.program_id(0)

        @pl.when(i == 0)
        def _():
            for h, w in ((bdr_h, wbdr), (bdi_h, wbdi), (cdr_h, wcdr), (cdi_h, wcdi)):
                pltpu.sync_copy(h, w)
            for a in (abdr, abdi, acdr, acdi, gcr, gci):
                a[...] = jnp.zeros_like(a)
            for o in (dabr_ref, dabi_ref, dd_ref):
                o[...] = jnp.zeros_like(o)
            for k, tile in enumerate(_scan_tiles(abr_ref[...], abi_ref[...], tb // SUBLANES, False)):
                coef_f[k] = tile
            for k, tile in enumerate(_scan_tiles(abr_ref[...], abi_ref[...], tb // SUBLANES, True)):
                coef_r[k] = tile

        car[...] = bsr_ref[...]
        cai[...] = bsi_ref[...]
        u = u_ref[...]
        dy = dge_ref[...] * _gelu_grad(y_ref[...])
        ub = _seg_order_rows(pm_ref[...], u.astype(BF16))
        dyb = _seg_order_rows(pm_ref[...], dy.astype(BF16))
        for s in range(ntl):
            us = ub[:, s * ct:(s + 1) * ct]
            spr[:, s * st:(s + 1) * st] = jnp.dot(us, wbdr[s], preferred_element_type=F32)
            spi[:, s * st:(s + 1) * st] = jnp.dot(us, wbdi[s], preferred_element_type=F32)
        _seg_scan(spr, spi, coef_f, car, cai, nrows=tb, ns=ns, reverse=False, cmat=(ser, sei))

        for s in range(ntl):
            dys = dyb[:, s * ct:(s + 1) * ct]
            gr[:, s * st:(s + 1) * st] = lax.dot_general(dys, wcdr[s], (NT, ((), ())), preferred_element_type=F32)
            gi[:, s * st:(s + 1) * st] = -lax.dot_general(dys, wcdi[s], (NT, ((), ())), preferred_element_type=F32)
        _seg_scan(gr, gi, coef_r, gcr, gci, nrows=tb, ns=ns, reverse=True,
                  dab=(spr, spi, ser, sei, dabr_ref, dabi_ref))

        for s in range(ntl):
            sl_c, sl_s = slice(s * ct, (s + 1) * ct), slice(s * st, (s + 1) * st)
            s_re = spr[:, sl_s].astype(BF16)
            s_im = spi[:, sl_s].astype(BF16)
            g_re, g_im = gr[:, sl_s].astype(BF16), gi[:, sl_s].astype(BF16)
            dys, us = dyb[:, sl_c], ub[:, sl_c]
            acdr[s] += lax.dot_general(s_re, dys, (TN, ((), ())), preferred_element_type=F32)
            acdi[s] -= lax.dot_general(s_im, dys, (TN, ((), ())), preferred_element_type=F32)
            abdr[s] += lax.dot_general(us, g_re, (TN, ((), ())), preferred_element_type=F32)
            abdi[s] += lax.dot_general(us, g_im, (TN, ((), ())), preferred_element_type=F32)
            dup[:, sl_c] = (lax.dot_general(g_re, wbdr[s], (NT, ((), ())), preferred_element_type=F32)
                            + lax.dot_general(g_im, wbdi[s], (NT, ((), ())), preferred_element_type=F32))
        dd_ref[...] += jnp.sum(dy * u, axis=0, keepdims=True)
        du = _time_order_rows(pm_ref[...], dup[...], 2) + d_ref[...] * dy
        dproj_ref[:, 0:P] = dpi_ref[...]
        dproj_ref[:, P:2 * P] = dpg_ref[...]
        dproj_ref[:, 2 * P:3 * P] = du.astype(BF16)
        dproj_ref[:, 3 * P:4 * P] = dsg_ref[...]

        @pl.when(i == nb - 1)
        def _():
            for a, h in ((abdr, dbdr_h), (abdi, dbdi_h), (acdr, dcdr_h), (acdi, dcdi_h)):
                pltpu.sync_copy(a, h)

    rev = lambda i: nb - 1 - i
    vec = lambda n: _bs((1, n), lambda i: (0, 0))
    row = _bs((tb, P), lambda i: (rev(i), 0))
    st_spec = _bs((None, SUBLANES, ns), lambda i: (rev(i), 0, 0))
    bshape = jax.ShapeDtypeStruct(bdr.shape, F32)
    cshape = jax.ShapeDtypeStruct(cdr.shape, F32)
    body, extra = _hosted(core, comm, (nb,), 16, 8, 21)
    return _call(body, grid=(nb,),
                 in_specs=[_bs((tb, P), lambda i: (rev(i), 2)), row, row, st_spec, st_spec,
                           vec(ns), vec(ns), vec(P), _bs((tb, tb), lambda i: (0, 0)), row, row, row,
                           ANY, ANY, ANY, ANY] + extra["in_specs"],
                 out_specs=[_bs((tb, 4 * P), lambda i: (rev(i), 0)), vec(ns), vec(ns), vec(P), ANY, ANY, ANY, ANY]
                 + extra["out_specs"],
                 out_shape=[jax.ShapeDtypeStruct((T, 4 * P), BF16), jax.ShapeDtypeStruct((1, ns), F32),
                            jax.ShapeDtypeStruct((1, ns), F32), jax.ShapeDtypeStruct((1, P), F32),
                            bshape, bshape, cshape, cshape] + extra["out_shape"],
                 scratch_shapes=[pltpu.VMEM(bdr.shape, BF16), pltpu.VMEM(bdr.shape, BF16),
                                 pltpu.VMEM(cdr.shape, BF16), pltpu.VMEM(cdr.shape, BF16),
                                 pltpu.VMEM(bdr.shape, F32), pltpu.VMEM(bdr.shape, F32),
                                 pltpu.VMEM(cdr.shape, F32), pltpu.VMEM(cdr.shape, F32),
                                 pltpu.VMEM((tb, ns), F32), pltpu.VMEM((tb, ns), F32),
                                 pltpu.VMEM((tb, ns), F32), pltpu.VMEM((tb, ns), F32),
                                 pltpu.VMEM((10, SUBLANES, ns), F32), pltpu.VMEM((10, SUBLANES, ns), F32)]
                 + [pltpu.VMEM((SUBLANES, ns), F32)] * 6 + [pltpu.VMEM((tb, P), F32)] + extra["scratch"],
                 compiler_params=_cp(("arbitrary",)), name="ssm_bwd")(
                     proj, y, dge, bsr, bsi, abr, abi, dsk, _seg_perm_matrix(tb), dpi, dpg, dsg,
                     bdr, bdi, cdr, cdi, *extra["ins"])


def _adamw(w, g, m, v, name, comm=None):
    R, C = w.shape
    tr = _t(R, 256)

    def core(w_ref, g_ref, m_ref, v_ref, d_ref, mo_ref, vo_ref):
        gv = g_ref[...]
        mn = ADAM_B1 * m_ref[...] + (1.0 - ADAM_B1) * gv
        vn = ADAM_B2 * v_ref[...] + (1.0 - ADAM_B2) * (gv * gv)
        m_hat = mn / (1.0 - ADAM_B1 ** ADAM_STEP)
        v_hat = vn / (1.0 - ADAM_B2 ** ADAM_STEP)
        d_ref[...] = -ADAM_LR * (m_hat / (jnp.sqrt(v_hat) + ADAM_EPS) + ADAM_WD * w_ref[...])
        mo_ref[...] = mn
        vo_ref[...] = vn

    blk = _bs((tr, C), lambda i: (i, 0))
    shp = jax.ShapeDtypeStruct((R, C), F32)
    body, extra = _hosted(core, comm, (R // tr,), 4, 3, 0)
    return _call(body, grid=(R // tr,), in_specs=[blk] * 4 + extra["in_specs"],
                 out_specs=[blk] * 3 + extra["out_specs"], out_shape=[shp] * 3 + extra["out_shape"],
                 scratch_shapes=extra["scratch"],
                 compiler_params=_cp(("arbitrary",) if comm else ("parallel",)), name=name)(w, g, m, v, *extra["ins"])


def _sum_cast(grad, got, place, name):
    J, H, C = got.shape
    tr = _t(H, 256)
    nb = H // tr

    def body(pl_ref, a_ref, b_ref, o_ref):
        o_ref[...] = (a_ref[...] + b_ref[...]).astype(BF16)

    blk = _bs((None, tr, C), lambda j, i, pc: (j, i, 0))
    mine = _bs((None, tr, C), lambda j, i, pc: (j, pc[1] * nb + i, 0))
    spec = pltpu.PrefetchScalarGridSpec(num_scalar_prefetch=1, grid=(J, nb), in_specs=[mine, blk], out_specs=blk)
    return _call(body, grid_spec=spec, out_shape=jax.ShapeDtypeStruct((J, H, C), BF16),
                 compiler_params=_cp(("parallel", "parallel")), name=name)(place, grad, got)


def _sum_chips(sent, arrived, place, name):
    J, H, C = arrived.shape
    tr = _t(H, 256)
    nb = H // tr

    def body(pl_ref, own_ref, a0_ref, a1_ref, a2_ref, o_ref):
        acc = own_ref[...].astype(F32)
        for r in (a0_ref, a1_ref, a2_ref):
            acc = acc + r[...].astype(F32)
        o_ref[...] = acc

    def other(k):
        return _bs((None, tr, C), lambda i, pc: (jnp.where(pc[0] <= k, k + 1, k), i, 0))

    spec = pltpu.PrefetchScalarGridSpec(
        num_scalar_prefetch=1, grid=(nb,),
        in_specs=[_bs((None, tr, C), lambda i, pc: (pc[0], i, 0)), other(0), other(1), other(2)],
        out_specs=_bs((tr, C), lambda i, pc: (pc[1] * nb + i, 0)))
    return _call(body, grid_spec=spec, out_shape=jax.ShapeDtypeStruct((2 * H, C), F32),
                 compiler_params=_cp(("parallel",)), name=name)(place, sent, arrived, arrived, arrived)


def _place():
    x, y, c = lax.axis_index("x"), lax.axis_index("y"), lax.axis_index("c")
    chips = [(1 - x, y), (x, 1 - y), (1 - x, 1 - y)]
    return x, y, c, chips


def _split(nrows, row_bytes, align, cap=None):
    k = max(1, min(cap or DMA_MAX_CHUNKS, (nrows * row_bytes) // DMA_CHUNK_BYTES))
    while k > 1 and nrows % (k * align):
        k -= 1
    return k


def _comm_call(plan, name):
    n_in, n_out = len(plan["ins"]), len(plan["out_shape"])

    def body(*refs):
        for phase in plan["phases"]:
            phase(refs[:n_in], refs[n_in:n_in + n_out], refs[n_in + n_out:])

    return _call(body, in_specs=[ANY] * n_in, out_specs=[ANY] * n_out, out_shape=plan["out_shape"],
                 input_output_aliases={i: i for i in range(n_out)} if plan.get("alias") else {},
                 scratch_shapes=plan["scratch"], name=name)(*plan["ins"])


def _comm_hooks(plan, grid, ins, outs, sems, *, before):
    if plan is None:
        return
    nsteps, step = 1, 0
    for d, g in enumerate(grid):
        nsteps, step = nsteps * g, step * g + pl.program_id(d)
    for p, (phase, frac) in enumerate(zip(plan["phases"], plan["at"])):
        if (p == 0) == before:
            pl.when(step == int(frac * (nsteps - 1)))(functools.partial(phase, ins, outs, sems))


def _ag_plan(shards, axes):
    n = len(shards)
    shapes = [a.shape for a in shards]

    def window(ref, i, chip, half=None):
        S, ax = shapes[i], axes[i]
        idx = []
        for d in range(len(S)):
            off, size = 0, S[d]
            if d == 0 and half is not None:
                off, size = half * (S[0] // 2), S[0] // 2
            if d == ax:
                off = off + chip * S[ax]
            idx.append(pl.ds(off, size))
        return ref.at[tuple(idx)]

    def copies(src, full, sems):
        ssem, rsem = sems
        x, y, c, chips = _place()
        me = 2 * x + y
        sib = (x, y, 1 - c)
        idx = [2 * cx + cy for cx, cy in chips]

        def rcopy(i, k, s_ref, d_ref, to):
            return pltpu.make_async_remote_copy(src_ref=s_ref, dst_ref=d_ref, send_sem=ssem.at[i, k],
                                                recv_sem=rsem.at[i, k], device_id=to, device_id_type=MESH)

        def ici(i, j, incoming):
            half_src = src[i].at[pl.ds(c * (shapes[i][0] // 2), shapes[i][0] // 2)]
            return rcopy(i, j, half_src, window(full[i], i, idx[j] if incoming else me, c), (*chips[j], c))

        def fwd(i, j, half):
            w = window(full[i], i, idx[j], half)
            return rcopy(i, 3 + j, w, w, sib)

        def own(i):
            return rcopy(i, 6, src[i], window(full[i], i, me), sib)

        return c, ici, fwd, own

    def send(src, full, sems):
        c, ici, fwd, own = copies(src, full, sems)
        for i in range(n):
            for j in range(3):
                ici(i, j, False).start()
        for i in range(n):
            own(i).start()

    def forward(i, src, full, sems):
        c, ici, fwd, own = copies(src, full, sems)
        for j in range(3):
            ici(i, j, True).wait_recv()
            fwd(i, j, c).start()

    def finish(src, full, sems):
        c, ici, fwd, own = copies(src, full, sems)
        for i in range(n):
            for j in range(3):
                fwd(i, j, 1 - c).wait_recv()
            own(i).wait()
        for i in range(n):
            for j in range(3):
                ici(i, j, False).wait_send()
                fwd(i, j, c).wait_send()

    out_shape = [jax.ShapeDtypeStruct(tuple(N_CHIP * d if k == ax else d for k, d in enumerate(S)), BF16)
                 for S, ax in zip(shapes, axes)]
    sizes = [a.size for a in shards]
    behind = [0.85 * sum(sizes[:i + 1]) / sum(sizes) + 0.05 for i in range(n)]
    return dict(ins=list(shards), out_shape=out_shape,
                phases=[send] + [functools.partial(forward, i) for i in range(n)] + [finish],
                at=[0.0] + behind + [1.0],
                scratch=[pltpu.SemaphoreType.DMA((n, 7)), pltpu.SemaphoreType.DMA((n, 7))])


def _proj_ag(x, g1, wsh, order, tm):
    T, D = x.shape
    P = wsh.shape[1]
    H = D // 2
    nt = T // tm

    def body(order_ref, x_ref, g_ref, wsh_ref, hn_ref, proj_ref, win_ref, wbuf, lsem, ssem, rsem):
        n, i = pl.program_id(0), pl.program_id(1)
        x, y, c, chips = _place()
        me = 2 * x + y
        sib = (x, y, 1 - c)
        idx = [2 * cx + cy for cx, cy in chips]

        def rcopy(k, s_ref, d_ref, to):
            return pltpu.make_async_remote_copy(src_ref=s_ref, dst_ref=d_ref, send_sem=ssem.at[k],
                                                recv_sem=rsem.at[k], device_id=to, device_id_type=MESH)

        def cols(chip):
            return pl.ds(pl.multiple_of(chip * P, LANES), P)

        def rows(half):
            return pl.ds(pl.multiple_of(half * H, 16), H)

        def ici(j, incoming):
            return rcopy(j, wsh_ref.at[rows(c)], win_ref.at[rows(c), cols(idx[j] if incoming else me)],
                         (*chips[j], c))

        def fwd(j, half):
            w = win_ref.at[rows(half), cols(idx[j])]
            return rcopy(3 + j, w, w, sib)

        def own():
            return rcopy(6, wsh_ref, win_ref.at[:, cols(me)], sib)

        def load(src):
            cp = pltpu.make_async_copy(src, wbuf, lsem)
            cp.start()
            cp.wait()

        @pl.when((n == 0) & (i == 0))
        def _():
            ici(0, False).start()
            ici(1, False).start()
            own().start()
            load(wsh_ref)

        for j in range(3):
            @pl.when((n == j + 1) & (i == 0))
            def _(j=j):
                if j == 0:
                    ici(2, False).start()
                ici(j, True).wait_recv()
                fwd(j, c).start()
                fwd(j, 1 - c).wait_recv()
                load(win_ref.at[:, cols(idx[j])])

        xv = x_ref[...]
        r = lax.rsqrt(jnp.mean(xv * xv, axis=-1, keepdims=True) + EPS)
        hn = ((xv * r) * g_ref[...]).astype(BF16)

        @pl.when(n == 0)
        def _():
            hn_ref[...] = hn

        proj_ref[...] = jnp.dot(hn, wbuf[...], preferred_element_type=F32)

        @pl.when((n == 3) & (i == nt - 1))
        def _():
            own().wait()
            for j in range(3):
                ici(j, False).wait_send()
                fwd(j, c).wait_send()

    spec = pltpu.PrefetchScalarGridSpec(
        num_scalar_prefetch=1, grid=(N_CHIP, nt),
        in_specs=[_bs((tm, D), lambda n, i, o: (i, 0)), _bs((1, D), lambda n, i, o: (0, 0)), ANY],
        out_specs=[_bs((tm, D), lambda n, i, o: (jnp.where(n == 0, i, nt - 1), 0)),
                   _bs((tm, P), lambda n, i, o: (i, o[n])), ANY],
        scratch_shapes=[pltpu.VMEM((D, P), BF16), pltpu.SemaphoreType.DMA,
                        pltpu.SemaphoreType.DMA((7,)), pltpu.SemaphoreType.DMA((7,))])
    return _call(body, grid_spec=spec,
                 out_shape=[jax.ShapeDtypeStruct((T, D), BF16), jax.ShapeDtypeStruct((T, N_CHIP * P), F32),
                            jax.ShapeDtypeStruct((D, N_CHIP * P), BF16)],
                 compiler_params=_cp(("arbitrary", "arbitrary")), name="proj_ag")(order, x, g1, wsh)


def _halves_plan(grads):
    n = len(grads)

    def send(g, got, sems):
        ssem, rsem = sems
        x, y, c, _ = _place()
        sib = (x, y, 1 - c)
        for i in range(n):
            J, R, C = g[i].shape
            H = R // 2
            size = g[i].dtype.itemsize
            tile_rows = SUBLANES * 4 // size
            k = _split(H, C * size, tile_rows, cap=DMA_MAX_CHUNKS // J)
            hr = H // k
            for j in range(J):
                for q in range(k):
                    other = pl.ds(pl.multiple_of((1 - c) * H + q * hr, tile_rows), hr)
                    to = pl.ds(q * hr, hr)
                    pltpu.make_async_remote_copy(src_ref=g[i].at[j, other, :], dst_ref=got[i].at[j, to, :],
                                                 send_sem=ssem.at[i], recv_sem=rsem.at[i],
                                                 device_id=sib, device_id_type=MESH).start()

    def finish(g, got, sems):
        ssem, rsem = sems
        x, y, c, _ = _place()
        for i in range(n):
            pltpu.make_async_remote_copy(src_ref=got[i], dst_ref=got[i], send_sem=ssem.at[i], recv_sem=rsem.at[i],
                                         device_id=(x, y, 1 - c), device_id_type=MESH).wait()

    half = [jax.ShapeDtypeStruct((a.shape[0], a.shape[1] // 2, a.shape[2]), a.dtype) for a in grads]
    return dict(ins=list(grads), out_shape=half, phases=[send, finish], at=[0.0, 1.0],
                scratch=[pltpu.SemaphoreType.DMA((n,)), pltpu.SemaphoreType.DMA((n,))])


def _scatter_plan(parts):
    n = len(parts)

    def peers():
        x, y, c, chips = _place()
        return 2 * x + y, c, chips, [2 * cx + cy for cx, cy in chips]

    def send(s, got, sems):
        ssem, rsem = sems
        me, c, chips, idx = peers()
        for i in range(n):
            _, H, C = s[i].shape
            k = _split(H, C * 2, 16, cap=RS_CHUNKS)
            hr = H // k
            for q in range(k):
                rows = pl.ds(q * hr, hr)
                for j in range(3):
                    pltpu.make_async_remote_copy(src_ref=s[i].at[idx[j], rows, :], dst_ref=got[i].at[me, rows, :],
                                                 send_sem=ssem.at[i, j], recv_sem=rsem.at[i, j],
                                                 device_id=(*chips[j], c), device_id_type=MESH).start()

    def finish(s, got, sems):
        ssem, rsem = sems
        me, c, chips, idx = peers()
        for i in range(n):
            for j in range(3):
                pltpu.make_async_remote_copy(src_ref=s[i].at[idx[j]], dst_ref=got[i].at[idx[j]],
                                             send_sem=ssem.at[i, j], recv_sem=rsem.at[i, j],
                                             device_id=(*chips[j], c), device_id_type=MESH).wait()

    return dict(ins=list(parts), out_shape=[jax.ShapeDtypeStruct(a.shape, a.dtype) for a in parts],
                phases=[send, finish], at=[0.0, 1.0],
                scratch=[pltpu.SemaphoreType.DMA((n, 3)), pltpu.SemaphoreType.DMA((n, 3))])


def _join_plan(shards):
    n = len(shards)

    def send(_, full, sems):
        ssem, rsem = sems
        x, y, c, _ = _place()
        sib = (x, y, 1 - c)
        for i in range(n):
            H, C = full[i].shape[0] // 2, full[i].shape[1]
            k = _split(H, C * 4, SUBLANES)
            hr = H // k
            for q in range(k):
                rows = pl.ds(pl.multiple_of(c * H + q * hr, SUBLANES), hr)
                pltpu.make_async_remote_copy(src_ref=full[i].at[rows], dst_ref=full[i].at[rows],
                                             send_sem=ssem.at[i], recv_sem=rsem.at[i],
                                             device_id=sib, device_id_type=MESH).start()

    def finish(_, full, sems):
        ssem, rsem = sems
        x, y, c, _ = _place()
        for i in range(n):
            half = full[i].at[pl.ds(0, full[i].shape[0] // 2)]
            pltpu.make_async_remote_copy(src_ref=half, dst_ref=half, send_sem=ssem.at[i], recv_sem=rsem.at[i],
                                         device_id=(x, y, 1 - c), device_id_type=MESH).wait()

    return dict(ins=list(shards), out_shape=[jax.ShapeDtypeStruct(a.shape, a.dtype) for a in shards],
                phases=[send, finish], at=[0.0, 1.0], alias=True,
                scratch=[pltpu.SemaphoreType.DMA((n,)), pltpu.SemaphoreType.DMA((n,))])


def _allreduce_plan(buf):
    R, L = buf.shape
    RB = R // N_DEV

    def parts(sems):
        xv, got, ov, lsem, ssem, rsem = sems
        x, y, c, _ = _place()
        me = 4 * x + 2 * y + c

        def dev(k):
            return (k // 4, (k // 2) % 2, k % 2)

        def slab(k):
            return pl.ds(pl.multiple_of(k * RB, SUBLANES), RB)

        def first(d, to, landing):
            return pltpu.make_async_remote_copy(src_ref=xv.at[slab(to)], dst_ref=got.at[landing],
                                                send_sem=ssem.at[0, d], recv_sem=rsem.at[0, d],
                                                device_id=dev(to), device_id_type=MESH)

        def second(d, to, k):
            return pltpu.make_async_remote_copy(src_ref=ov.at[slab(k)], dst_ref=ov.at[slab(k)],
                                                send_sem=ssem.at[1, d], recv_sem=rsem.at[1, d],
                                                device_id=dev(to), device_id_type=MESH)

        return me, slab, first, second

    def scatter(ins, outs, sems):
        xv, lsem = sems[0], sems[3]
        me, slab, first, second = parts(sems)
        cp = pltpu.make_async_copy(ins[0], xv, lsem)
        cp.start()
        cp.wait()
        for d in range(1, N_DEV):
            first(d, (me + d) % N_DEV, me).start()

    def reduce(ins, outs, sems):
        xv, got, ov = sems[:3]
        me, slab, first, second = parts(sems)
        got[me] = xv[slab(me), :]
        for d in range(1, N_DEV):
            src = (me + N_DEV - d) % N_DEV
            first(d, src, src).wait_recv()
        acc = got[0]
        for k in range(1, N_DEV):
            acc = acc + got[k]
        ov[slab(me), :] = acc
        for d in range(1, N_DEV):
            second(d, (me + d) % N_DEV, me).start()

    def collect(ins, outs, sems):
        ov, lsem = sems[2], sems[3]
        me, slab, first, second = parts(sems)
        for d in range(1, N_DEV):
            src = (me + N_DEV - d) % N_DEV
            second(d, src, src).wait_recv()
        for d in range(1, N_DEV):
            peer = (me + d) % N_DEV
            first(d, peer, me).wait_send()
            second(d, peer, me).wait_send()
        cp = pltpu.make_async_copy(ov, outs[0], lsem)
        cp.start()
        cp.wait()

    return dict(ins=[buf], out_shape=[jax.ShapeDtypeStruct((R, L), F32)], phases=[scatter, reduce, collect],
                at=[0.0, 0.5, 1.0],
                scratch=[pltpu.VMEM((R, L), F32), pltpu.VMEM((N_DEV, RB, L), F32), pltpu.VMEM((R, L), F32),
                         pltpu.SemaphoreType.DMA, pltpu.SemaphoreType.DMA((2, N_DEV)),
                         pltpu.SemaphoreType.DMA((2, N_DEV))])


def _block_diag(t, gt):
    G, A, B = t.shape
    t4 = t.reshape(G // gt, gt, A, B)
    eye = jnp.eye(gt, dtype=t.dtype)
    return jnp.einsum('sgab,gh->sgahb', t4, eye).reshape(G // gt, gt * A, gt * B)


def _block_diag_extract(m, gt, A, B):
    S = m.shape[0]
    m5 = m.reshape(S, gt, A, gt, B)
    eye = jnp.eye(gt, dtype=m.dtype)
    return jnp.einsum('sgahb,gh->sgab', m5, eye).reshape(S * gt, A, B)


def _tile_rows(n):
    return -(-n // (SUBLANES * LANES)) * SUBLANES


def _pack_small(arrs, rows):
    parts = []
    for a in arrs:
        flat = a.reshape(-1).astype(F32)
        r = _tile_rows(flat.shape[0])
        parts.append(jnp.pad(flat, (0, r * LANES - flat.shape[0])).reshape(r, LANES))
    used = sum(p.shape[0] for p in parts)
    if rows > used:
        parts.append(jnp.zeros((rows - used, LANES), F32))
    return jnp.concatenate(parts)


def _unpack_small(buf, shapes):
    out, off = [], 0
    for s in shapes:
        n = 1
        for d in s:
            n *= d
        r = _tile_rows(n)
        piece = buf[off:off + r]
        out.append(piece.reshape(s) if n == r * LANES else piece.reshape(-1)[:n].reshape(s))
        off += r
    return out


def kernel(x, p, norm_gain, w_in, w_pool, pool_scale, a_re, a_im, log_dt, b_re, b_im, c_re, c_im, d_skip, w_glu, w_out, w_ple, w_ple_gate, final_gain, loss_target, m_norm_gain, m_w_in, m_w_pool, m_pool_scale, m_a_re, m_a_im, m_log_dt, m_b_re, m_b_im, m_c_re, m_c_im, m_d_skip, m_w_glu, m_w_out, m_w_ple, m_w_ple_gate, m_final_gain, v_norm_gain, v_w_in, v_w_pool, v_pool_scale, v_a_re, v_a_im, v_log_dt, v_b_re, v_b_im, v_c_re, v_c_im, v_d_skip, v_w_glu, v_w_out, v_w_ple, v_w_ple_gate, v_final_gain):
    xs, pe, tgt = x[0], p[0, 0], loss_target[0]
    T, D = xs.shape
    E = pe.shape[1]
    P = D // 2
    NG = len(POOL_WINDOWS)
    PG = P // NG
    G, N, C = P // SSM_GROUP, SSM_STATE, SSM_GROUP
    GT = min(SSM_TILE_GROUPS, G)
    Q = D // N_CHIP

    big = {"w_in": (w_in, m_w_in, v_w_in), "w_pool": (w_pool, m_w_pool, v_w_pool),
           "w_glu": (w_glu, m_w_glu, v_w_glu), "w_out": (w_out, m_w_out, v_w_out),
           "w_ple": (w_ple, m_w_ple, v_w_ple), "w_ple_gate": (w_ple_gate, m_w_ple_gate, v_w_ple_gate)}
    big_names = list(big)
    shard2d = {n: (big[n][0].size // big[n][0].shape[-1], big[n][0].shape[-1]) for n in big_names}
    shard_axis = {"w_in": 1, "w_pool": 1, "w_glu": 1, "w_out": 0, "w_ple": 1, "w_ple_gate": 0}
    shard16 = {n: big[n][0][0].astype(BF16) for n in big_names}
    place = jnp.stack([2 * lax.axis_index("x") + lax.axis_index("y"), lax.axis_index("c")]).astype(jnp.int32)
    mx, my = lax.axis_index("x"), lax.axis_index("y")
    block_order = jnp.stack([2 * mx + my, 2 * (1 - mx) + my, 2 * mx + (1 - my),
                             2 * (1 - mx) + (1 - my)]).astype(jnp.int32)
    later = [n for n in big_names if n != "w_in"]
    ag_later = _ag_plan([shard16[n] for n in later], [shard_axis[n] for n in later])

    rep = lambda a: jnp.repeat(a, C, axis=0)
    a_re_r, a_im_r = rep(a_re[0]), rep(a_im[0])
    ldt_r = rep(jnp.broadcast_to(log_dt[0][:, None], (G, N)))
    bt_re = b_re[0].transpose(0, 2, 1).reshape(G * C, N)
    bt_im = b_im[0].transpose(0, 2, 1).reshape(G * C, N)
    ab_re_r, ab_im_r, bbt_re, bbt_im = _ssm_prep(a_re_r, a_im_r, ldt_r, bt_re, bt_im)
    abr = ab_re_r[::C].reshape(1, G * N)
    abi = ab_im_r[::C].reshape(1, G * N)
    bdr = _block_diag(bbt_re.reshape(G, C, N), GT).astype(BF16)
    bdi = _block_diag(bbt_im.reshape(G, C, N), GT).astype(BF16)
    cdr = _block_diag(c_re[0].transpose(0, 2, 1), GT).astype(BF16)
    cdi = _block_diag(c_im[0].transpose(0, 2, 1), GT).astype(BF16)

    tb = _t(T, 256)
    tbs = _t(T, 256)
    tm = _t(T, 1024)
    tk = _t(T, 2048)
    DH = _t(D, 1024)
    row_k = lambda i, n, k: (i, k)
    row_n = lambda i, n, k: (i, n)
    f32 = lambda *shape: jax.ShapeDtypeStruct(shape, F32)
    hn, proj, win = _proj_ag(xs, norm_gain, shard16["w_in"], block_order, tm)
    y, ge, bsr, bsi, wp, wglu, wout, wple, wpg = _ssm_fwd(proj, bdr, bdi, cdr, cdi, abr, abi, d_skip, P, tbs,
                                                          comm=ag_later)
    pooled, mixed = _pool_fwd(proj, wp, P, tb)
    hg = _mm(ge, wglu, dims=NN, grid=(T // tm, 1, 1),
             a_spec=_bs((tm, P), row_k), b_spec=_bs((P, 2 * P), lambda i, n, k: (k, n)),
             o_spec=_bs((tm, 2 * P), row_n), out_shape=f32(T, 2 * P), name="mm_glu")
    cat = _gate_fwd(mixed, proj, hg, pool_scale, tb)
    h1, h1b = _mm(cat, wout, dims=NN, grid=(T // tm, D // DH, 1), res=xs, bf16_copy=True,
                  a_spec=_bs((tm, D), row_k), b_spec=_bs((D, DH), lambda i, n, k: (k, n)),
                  r_spec=_bs((tm, DH), row_n), o_spec=_bs((tm, DH), row_n), out_shape=f32(T, D), name="mm_out")
    z = _mm(h1b, wpg, dims=NN, grid=(T // tm, 1, 1),
            a_spec=_bs((tm, D), row_k), b_spec=_bs((D, D), lambda i, n, k: (k, n)),
            o_spec=_bs((tm, D), row_n), out_shape=f32(T, D), name="mm_pgate")
    dh2, de, dz, dg2, lpart = _final_fb(h1, pe, wple, z, tgt, final_gain.reshape(1, D), tb)

    col_m = lambda m, n, k: (k, m)
    col_n = lambda m, n, k: (k, n)
    dh1, dh1b = _mm(dz, wpg, dims=NT, grid=(T // tm, D // DH, 1), res=dh2, bf16_copy=True,
                    a_spec=_bs((tm, D), row_k), b_spec=_bs((DH, D), lambda i, n, k: (n, k)),
                    r_spec=_bs((tm, DH), row_n), o_spec=_bs((tm, DH), row_n), out_shape=f32(T, D), name="mm_dh1")
    g_wpg, g_wpg16 = _mm(h1b, dz, dims=TN, grid=(D // DH, D // DH, T // tk), bf16_copy=True,
                         a_spec=_bs((tk, DH), col_m), b_spec=_bs((tk, DH), col_n),
                         o_spec=_bs((DH, DH), lambda m, n, k: (m, n)), out_shape=f32(D, D), name="mm_gwpg")
    g_wple, g_wple16 = _mm(pe, de, dims=TN, grid=(1, N_CHIP, T // tk), bf16_copy=True,
                           a_spec=_bs((tk, E), col_m), b_spec=_bs((tk, Q), col_n),
                           o_spec=_bs((None, E, Q), lambda m, j, k: (j, 0, 0)), out_shape=f32(N_CHIP, E, Q),
                           name="mm_gwple")
    dcat = _mm(dh1b, wout, dims=NT, grid=(T // tm, 1, 1),
               a_spec=_bs((tm, D), row_k), b_spec=_bs((D, D), lambda i, n, k: (n, k)),
               o_spec=_bs((tm, D), row_n), out_shape=f32(T, D), name="mm_dcat")
    g_wout, g_wout16 = _mm(cat, dh1b, dims=TN, grid=(D // DH, D // DH, T // tk), bf16_copy=True,
                           a_spec=_bs((tk, DH), col_m), b_spec=_bs((tk, DH), col_n),
                           o_spec=_bs((DH, DH), lambda m, n, k: (m, n)), out_shape=f32(D, D), name="mm_gwout")
    gbig = {"w_out": g_wout.reshape(N_CHIP, Q, D), "w_ple": g_wple, "w_ple_gate": g_wpg.reshape(N_CHIP, Q, D)}
    gbig16 = {"w_out": g_wout16.reshape(N_CHIP, Q, D), "w_ple": g_wple16,
              "w_ple_gate": g_wpg16.reshape(N_CHIP, Q, D)}
    first = list(gbig)
    res = _gate_bwd(dcat, mixed, proj, hg, pool_scale, tb, comm=_halves_plan([gbig16[n] for n in first]))
    dmixed, dpg, dsg, dhg, dps = res[:5]
    got = dict(zip(first, res[5:]))
    dge = _mm(dhg, wglu, dims=NT, grid=(T // tm, 1, 1),
              a_spec=_bs((tm, 2 * P), row_k), b_spec=_bs((P, 2 * P), lambda i, n, k: (n, k)),
              o_spec=_bs((tm, P), row_n), out_shape=f32(T, P), name="mm_dge")
    gbig["w_glu"], gbig16["w_glu"] = _mm(ge, dhg, dims=TN, grid=(1, N_CHIP, T // tk), bf16_copy=True,
                                         a_spec=_bs((tk, P), col_m), b_spec=_bs((tk, Q), col_n),
                                         o_spec=_bs((None, P, Q), lambda m, j, k: (j, 0, 0)),
                                         out_shape=f32(N_CHIP, P, Q), name="mm_gwglu")
    g_wp = _mm(pooled, dmixed, dims=TN, grid=(NG, 1, T // tk), bf16_copy=True,
               a_spec=_bs((tk, PG), col_m), b_spec=_bs((tk, PG), col_m),
               o_spec=_bs((None, PG, PG), lambda g, n, k: (g, 0, 0)), out_shape=f32(NG, PG, PG), name="mm_gwp")
    by_chip = lambda a: a.reshape(NG, N_CHIP, PG // N_CHIP, PG).transpose(1, 0, 2, 3).reshape(
        N_CHIP, NG * PG // N_CHIP, PG)
    gbig["w_pool"], gbig16["w_pool"] = by_chip(g_wp[0]), by_chip(g_wp[1])
    res = _pool_bwd(dmixed, wp, tb, comm=_halves_plan([gbig16["w_pool"], gbig16["w_glu"]]))
    dpi, got["w_pool"], got["w_glu"] = res
    early = list(gbig)
    chip_sums = {n: _sum_cast(gbig[n], got[n], place, "sum_cast_" + n) for n in early}
    res = _ssm_bwd(proj, y, dge, bsr, bsi, bdr, bdi, cdr, cdi, abr, abi, d_skip, dpi, dpg, dsg, P, tbs,
                   comm=_scatter_plan([chip_sums[n] for n in early]))
    dproj, dabr, dabi, dd, dbdr, dbdi, dcdr, dcdi = res[:8]
    arrived = dict(zip(early, res[8:]))
    halves = [_sum_chips(chip_sums[n], arrived[n], place, "sum_chips_" + n) for n in early]
    res = _mm(hn, dproj, dims=TN, grid=(D // DH, N_CHIP, T // tk), bf16_copy=True,
              a_spec=_bs((tk, DH), col_m), b_spec=_bs((tk, P), col_n),
              o_spec=_bs((None, DH, P), lambda m, j, k: (j, m, 0)), out_shape=f32(N_CHIP, D, P),
              name="mm_gwin", comm=_join_plan(halves))
    gbig["w_in"], gbig16["w_in"], gshard = res[0], res[1], dict(zip(early, res[2:]))
    got["w_in"], = _comm_call(_halves_plan([gbig16["w_in"]]), "rs_halves_late")
    chip_sums["w_in"] = _sum_cast(gbig["w_in"], got["w_in"], place, "sum_cast_w_in")
    KH = _t(4 * P, 2048)
    dhn, arrived["w_in"] = _mm(dproj, win, dims=NT, grid=(T // tm, D // DH, 4 * P // KH),
                               a_spec=_bs((tm, KH), row_k), b_spec=_bs((DH, KH), lambda i, n, k: (n, k)),
                               o_spec=_bs((tm, DH), row_n), out_shape=f32(T, D), name="mm_dhn",
                               comm=_scatter_plan([chip_sums["w_in"]]))
    gshard["w_in"], = _comm_call(
        _join_plan([_sum_chips(chip_sums["w_in"], arrived["w_in"], place, "sum_chips_w_in")]), "rs_join_w_in")
    grad_x, dg1 = _norm1_bwd(xs, dhn, dh1, norm_gain, tb)

    dbbt_re = _block_diag_extract(dbdr, GT, C, N).reshape(G * C, N)
    dbbt_im = _block_diag_extract(dbdi, GT, C, N).reshape(G * C, N)
    g_c_re = _block_diag_extract(dcdr, GT, N, C).transpose(0, 2, 1)
    g_c_im = _block_diag_extract(dcdi, GT, N, C).transpose(0, 2, 1)
    dab_re_r = rep(dabr.reshape(G, N)) * (1.0 / C)
    dab_im_r = rep(dabi.reshape(G, N)) * (1.0 / C)
    g_a_re, g_a_im, g_ldt, g_bt_re, g_bt_im = _ssm_prep_bwd(a_re_r, a_im_r, ldt_r, bt_re, bt_im,
                                                            dab_re_r, dab_im_r, dbbt_re, dbbt_im, G)
    g_b_re = g_bt_re.reshape(G, C, N).transpose(0, 2, 1)
    g_b_im = g_bt_im.reshape(G, C, N).transpose(0, 2, 1)


    small_names = ["norm_gain", "pool_scale", "a_re", "a_im", "log_dt", "b_re", "b_im", "c_re", "c_im",
                   "d_skip", "final_gain"]
    small_w = dict(norm_gain=norm_gain, pool_scale=pool_scale, a_re=a_re, a_im=a_im, log_dt=log_dt, b_re=b_re,
                   b_im=b_im, c_re=c_re, c_im=c_im, d_skip=d_skip, final_gain=final_gain)
    small_m = dict(norm_gain=m_norm_gain, pool_scale=m_pool_scale, a_re=m_a_re, a_im=m_a_im, log_dt=m_log_dt,
                   b_re=m_b_re, b_im=m_b_im, c_re=m_c_re, c_im=m_c_im, d_skip=m_d_skip, final_gain=m_final_gain)
    small_v = dict(norm_gain=v_norm_gain, pool_scale=v_pool_scale, a_re=v_a_re, a_im=v_a_im, log_dt=v_log_dt,
                   b_re=v_b_re, b_im=v_b_im, c_re=v_c_re, c_im=v_c_im, d_skip=v_d_skip, final_gain=v_final_gain)
    small_g = dict(norm_gain=dg1, pool_scale=dps, a_re=g_a_re, a_im=g_a_im, log_dt=g_ldt, b_re=g_b_re,
                   b_im=g_b_im, c_re=g_c_re, c_im=g_c_im, d_skip=dd, final_gain=dg2)
    shapes = [small_w[n].shape for n in small_names]
    loss_row = sum(_tile_rows(small_w[n].size) for n in small_names)
    unit = N_DEV * SUBLANES
    rows = -(-(loss_row + SUBLANES) // unit) * unit
    gbuf = _pack_small([small_g[n] for n in small_names] + [lpart[0, :1]], rows)
    gsum, = _comm_call(_allreduce_plan(gbuf), "allreduce_small")
    g_out, d_out, m_out, v_out = {}, {}, {}, {}
    for n in big_names:
        w_, m_, v_ = big[n]
        r2 = shard2d[n]
        res = _adamw(w_.reshape(r2), gshard[n], m_.reshape(r2), v_.reshape(r2), "adamw_" + n)
        g_out[n], d_out[n], m_out[n], v_out[n] = (a.reshape(w_.shape) for a in (gshard[n], *res))
    wbuf = _pack_small([small_w[n] for n in small_names], rows)
    mbuf = _pack_small([small_m[n] for n in small_names], rows)
    vbuf = _pack_small([small_v[n] for n in small_names], rows)
    dsm, msm, vsm = _adamw(wbuf, gsum, mbuf, vbuf, "adamw_small")
    g_small = dict(zip(small_names, _unpack_small(gsum, shapes)))
    d_small = dict(zip(small_names, _unpack_small(dsm, shapes)))
    m_small = dict(zip(small_names, _unpack_small(msm, shapes)))
    v_small = dict(zip(small_names, _unpack_small(vsm, shapes)))
    loss = gsum[loss_row, 0]

    g_out.update(g_small)
    d_out.update(d_small)
    m_out.update(m_small)
    v_out.update(v_small)

    order = ["norm_gain", "w_in", "w_pool", "pool_scale", "a_re", "a_im", "log_dt", "b_re", "b_im", "c_re",
             "c_im", "d_skip", "w_glu", "w_out", "w_ple", "w_ple_gate", "final_gain"]
    return (loss, grad_x[None], *[g_out[n] for n in order], *[d_out[n] for n in order],
            *[m_out[n] for n in order], *[v_out[n] for n in order])
```

```python
import functools

import jax
import jax.numpy as jnp
from jax import lax
from jax.experimental import pallas as pl
from jax.experimental.pallas import tpu as pltpu

F32, BF16 = jnp.float32, jnp.bfloat16
MESH = pl.DeviceIdType.MESH
ANY = pl.BlockSpec(memory_space=pl.ANY)
VMEM_FULL = pl.BlockSpec(memory_space=pltpu.VMEM)

EPS = 1e-6
A_RE_MAX = -1e-4
SSM_GROUP = 16
SSM_STATE = 64
POOL_WINDOWS = (2, 4, 8, 16)
POOL_HALO = 16
ADAM_LR, ADAM_B1, ADAM_B2, ADAM_EPS, ADAM_WD, ADAM_STEP = 0.001, 0.9, 0.999, 1e-08, 0.01, 10

V7X_VMEM_BYTES = 64 * 1024 * 1024
VMEM_LIMIT = V7X_VMEM_BYTES - 8 * 1024 * 1024
SUBLANES, LANES = 8, 128
BF16_TILE_ROWS = 16
SSM_TILE_GROUPS = 8
SCAN_LANES = 1024
N_DEV, N_CHIP = 8, 4
DMA_CHUNK_BYTES = 256 * 1024
DMA_MAX_CHUNKS = 32
RS_CHUNKS = 8
ROWS_ELEMENTWISE = 256
ROWS_SSM = 256
ROWS_MATMUL = 1024
DEPTH_MATMUL = 2048
AG_FORWARD_SCALE, AG_FORWARD_LAG = 0.85, 0.05


def _t(n, pref):
    return pref if n % pref == 0 else n


def _cp(sem=None, vmem=VMEM_LIMIT):
    return pltpu.CompilerParams(dimension_semantics=sem, vmem_limit_bytes=vmem)


def _call(body, **kw):
    return pl.pallas_call(body, **kw)


NN = ((1,), (0,))
NT = ((1,), (1,))
TN = ((0,), (0,))


def _mm(a, b, *, dims, grid, a_spec, b_spec, o_spec, out_shape, name, res=None, r_spec=None, bf16_copy=False,
        comm=None):
    nk, kax = grid[-1], len(grid) - 1
    acc_shape = tuple(d for d in o_spec.block_shape if d is not None)

    def core(*refs):
        refs = list(refs)
        a_ref, b_ref = refs[:2]
        r_ref = refs[2] if res is not None else None
        outs = refs[3 if res is not None else 2:]
        o_ref = outs[0]
        o2_ref = outs[1] if bf16_copy else None
        acc = outs[-1] if nk > 1 else None

        def finish(r):
            if r_ref is not None:
                r = r + r_ref[...]
            o_ref[...] = r.astype(o_ref.dtype)
            if o2_ref is not None:
                o2_ref[...] = r.astype(BF16)

        part = lax.dot_general(a_ref[...].astype(BF16), b_ref[...].astype(BF16),
                               (dims, ((), ())), preferred_element_type=F32)
        if nk == 1:
            finish(part)
        else:
            k = pl.program_id(kax)

            @pl.when(k == 0)
            def _():
                acc[...] = part

            @pl.when(k > 0)
            def _():
                acc[...] += part

            @pl.when(k == nk - 1)
            def _():
                finish(acc[...])

    ins, specs = [a, b], [a_spec, b_spec]
    if res is not None:
        ins.append(res)
        specs.append(r_spec)
    o_specs, o_shapes = [o_spec], [out_shape]
    if bf16_copy:
        o_specs = [o_spec, o_spec]
        o_shapes = [out_shape, jax.ShapeDtypeStruct(out_shape.shape, BF16)]
    scratch = [pltpu.VMEM(acc_shape, F32)] if nk > 1 else []
    body, extra = _hosted(core, comm, grid, len(ins), len(o_specs), len(scratch))
    sem = ("arbitrary",) * len(grid) if comm else ("parallel",) * kax + ("arbitrary",)
    outs = _call(body, grid=grid, in_specs=specs + extra["in_specs"], out_specs=o_specs + extra["out_specs"],
                 out_shape=o_shapes + extra["out_shape"], scratch_shapes=scratch + extra["scratch"],
                 input_output_aliases=extra["aliases"],
                 compiler_params=_cp(sem), name=name)(*ins, *extra["ins"])
    return outs[0] if len(outs) == 1 else outs


def _bs(shape, fn):
    return pl.BlockSpec(shape, fn)


def _sigmoid(v):
    return 1.0 / (1.0 + jnp.exp(-v))


def _gelu(v):
    return 0.5 * v * (1.0 + jnp.tanh(0.7978845608028654 * (v + 0.044715 * v * v * v)))


def _gelu_grad(v):
    t = jnp.tanh(0.7978845608028654 * (v + 0.044715 * v * v * v))
    return 0.5 * (1.0 + t) + 0.5 * v * (1.0 - t * t) * 0.7978845608028654 * (1.0 + 3 * 0.044715 * v * v)


def _norm1_bwd(x, dhn, dh1, g1, tb, comm=None):
    T, D = x.shape

    def core(x_ref, dhn_ref, dh1_ref, g_ref, dx_ref, dg_ref):
        @pl.when(pl.program_id(0) == 0)
        def _():
            dg_ref[...] = jnp.zeros_like(dg_ref)

        xv = x_ref[...]
        r = lax.rsqrt(jnp.mean(xv * xv, axis=-1, keepdims=True) + EPS)
        xh = xv * r
        dhn_v = dhn_ref[...]
        dg_ref[...] += jnp.sum(dhn_v * xh, axis=0, keepdims=True)
        dxh = dhn_v * g_ref[...]
        dx_ref[...] = dh1_ref[...] + r * (dxh - xh * jnp.mean(dxh * xh, axis=-1, keepdims=True))

    row = _bs((tb, D), lambda i: (i, 0))
    vec = _bs((1, D), lambda i: (0, 0))
    body, extra = _hosted(core, comm, (T // tb,), 4, 2, 0)
    return _call(body, grid=(T // tb,), in_specs=[row, row, row, vec] + extra["in_specs"],
                 out_specs=[row, vec] + extra["out_specs"],
                 out_shape=[jax.ShapeDtypeStruct((T, D), F32), jax.ShapeDtypeStruct((1, D), F32)] + extra["out_shape"],
                 scratch_shapes=extra["scratch"], input_output_aliases=extra["aliases"],
                 compiler_params=_cp(("arbitrary",)), name="norm1_bwd")(x, dhn, dh1, g1, *extra["ins"])


def _gate_fwd(mixed, proj, hg, ps, tb):
    T, P = mixed.shape

    def body(mx_ref, pg_ref, sg_ref, hg_ref, ps_ref, o_ref):
        pg, sg = pg_ref[...], sg_ref[...]
        ya = (mx_ref[...] * ps_ref[...]) * (pg * _sigmoid(pg))
        hgv = hg_ref[...]
        o = hgv[:, :P] * _sigmoid(hgv[:, P:])
        yb = o * (sg * _sigmoid(sg))
        o_ref[:, :P] = ya.astype(BF16)
        o_ref[:, P:] = yb.astype(BF16)

    return _call(body, grid=(T // tb,),
                 in_specs=[_bs((tb, P), lambda i: (i, 0)), _bs((tb, P), lambda i: (i, 1)),
                           _bs((tb, P), lambda i: (i, 3)), _bs((tb, 2 * P), lambda i: (i, 0)),
                           _bs((1, P), lambda i: (0, 0))],
                 out_specs=_bs((tb, 2 * P), lambda i: (i, 0)),
                 out_shape=jax.ShapeDtypeStruct((T, 2 * P), BF16),
                 compiler_params=_cp(("parallel",)), name="gate_fwd")(mixed, proj, proj, hg, ps)


def _gate_bwd(dcat, mixed, proj, hg, ps, tb, comm=None):
    T, P = mixed.shape

    def core(dc_ref, mx_ref, pg_ref, sg_ref, hg_ref, ps_ref, dmx_ref, dpg_ref, dsg_ref, dhg_ref, dps_ref):
        @pl.when(pl.program_id(0) == 0)
        def _():
            dps_ref[...] = jnp.zeros_like(dps_ref)

        dc = dc_ref[...]
        dya, dyb = dc[:, :P], dc[:, P:]
        pg, sg, mx, psv = pg_ref[...], sg_ref[...], mx_ref[...], ps_ref[...]
        s_pg = _sigmoid(pg)
        dpa = dya * (pg * s_pg)
        dpg_ref[...] = (dya * (mx * psv) * (s_pg * (1.0 + pg * (1.0 - s_pg)))).astype(BF16)
        dps_ref[...] += jnp.sum(dpa * mx, axis=0, keepdims=True)
        dmx_ref[...] = (dpa * psv).astype(BF16)
        hgv = hg_ref[...]
        h1, s_h2 = hgv[:, :P], _sigmoid(hgv[:, P:])
        s_sg = _sigmoid(sg)
        do = dyb * (sg * s_sg)
        dsg_ref[...] = (dyb * (h1 * s_h2) * (s_sg * (1.0 + sg * (1.0 - s_sg)))).astype(BF16)
        dhg_ref[:, :P] = (do * s_h2).astype(BF16)
        dhg_ref[:, P:] = (do * h1 * s_h2 * (1.0 - s_h2)).astype(BF16)

    rowp = _bs((tb, P), lambda i: (i, 0))
    row2 = _bs((tb, 2 * P), lambda i: (i, 0))
    vec = _bs((1, P), lambda i: (0, 0))
    body, extra = _hosted(core, comm, (T // tb,), 6, 5, 0)
    return _call(body, grid=(T // tb,),
                 in_specs=[row2, rowp, _bs((tb, P), lambda i: (i, 1)), _bs((tb, P), lambda i: (i, 3)), row2, vec]
                 + extra["in_specs"],
                 out_specs=[rowp, rowp, rowp, row2, vec] + extra["out_specs"],
                 out_shape=[jax.ShapeDtypeStruct((T, P), BF16), jax.ShapeDtypeStruct((T, P), BF16),
                            jax.ShapeDtypeStruct((T, P), BF16), jax.ShapeDtypeStruct((T, 2 * P), BF16),
                            jax.ShapeDtypeStruct((1, P), F32)] + extra["out_shape"],
                 scratch_shapes=extra["scratch"],
                 compiler_params=_cp(("arbitrary",)), name="gate_bwd")(dcat, mixed, proj, proj, hg, ps, *extra["ins"])


def _final_fb(h1, pe, wple, z, tgt, g2, tb):
    T, D = h1.shape
    E = pe.shape[1]
    Q = D // N_CHIP
    nb = T // tb

    def body(h1_ref, p_ref, w_ref, z_ref, t_ref, g_ref, dh2_ref, dz_ref, dg_ref, l_ref, gw_ref, gw16_ref, acc):
        @pl.when(pl.program_id(0) == 0)
        def _():
            dg_ref[...] = jnp.zeros_like(dg_ref)
            l_ref[...] = jnp.zeros_like(l_ref)
            acc[...] = jnp.zeros_like(acc)

        pb = p_ref[...].astype(BF16)
        ev = jnp.dot(pb, w_ref[...], preferred_element_type=F32)
        s = _sigmoid(z_ref[...])
        h2 = h1_ref[...] + ev * s
        r = lax.rsqrt(jnp.mean(h2 * h2, axis=-1, keepdims=True) + EPS)
        xh = h2 * r
        gv = g_ref[...]
        diff = xh * gv - t_ref[...]
        l_ref[...] += 0.5 * jnp.sum(jnp.mean(diff * diff, axis=-1, keepdims=True))
        dout = diff * (1.0 / D)
        dg_ref[...] += jnp.sum(dout * xh, axis=0, keepdims=True)
        dxh = dout * gv
        dh2 = r * (dxh - xh * jnp.mean(dxh * xh, axis=-1, keepdims=True))
        dh2_ref[...] = dh2
        dz_ref[...] = (dh2 * ev * s * (1.0 - s)).astype(BF16)
        acc[...] += lax.dot_general(pb, (dh2 * s).astype(BF16), (TN, ((), ())), preferred_element_type=F32)

        @pl.when(pl.program_id(0) == nb - 1)
        def _():
            for j in range(N_CHIP):
                slab = acc[:, j * Q:(j + 1) * Q]
                gw_ref[j] = slab
                gw16_ref[j] = slab.astype(BF16)

    row = _bs((tb, D), lambda i: (i, 0))
    vec = _bs((1, D), lambda i: (0, 0))
    slabs = _bs((N_CHIP, E, Q), lambda i: (0, 0, 0))
    return _call(body, grid=(nb,),
                 in_specs=[row, _bs((tb, E), lambda i: (i, 0)), _bs((E, D), lambda i: (0, 0)), row, row, vec],
                 out_specs=[row, row, vec, _bs((1, LANES), lambda i: (0, 0)), slabs, slabs],
                 out_shape=[jax.ShapeDtypeStruct((T, D), F32), jax.ShapeDtypeStruct((T, D), BF16),
                            jax.ShapeDtypeStruct((1, D), F32), jax.ShapeDtypeStruct((1, LANES), F32),
                            jax.ShapeDtypeStruct((N_CHIP, E, Q), F32), jax.ShapeDtypeStruct((N_CHIP, E, Q), BF16)],
                 scratch_shapes=[pltpu.VMEM((E, D), F32)],
                 compiler_params=_cp(("arbitrary",)), name="final_fb")(h1, pe, wple, z, tgt, g2)


def _pool_inv_count(t0, rows, pg, ngroups):
    t = t0 + lax.broadcasted_iota(jnp.int32, (rows, pg), 0)
    parts = []
    for w in POOL_WINDOWS[:ngroups]:
        parts.append(jnp.where(t + 1 >= w, 1.0 / w, 1.0 / (t + 1).astype(F32)))
    return parts


def _pool_fwd(proj, wp, P, tb):
    T = proj.shape[0]
    ng = len(POOL_WINDOWS)
    pg = P // ng
    hb = tb // POOL_HALO

    def body(v_ref, tail_ref, w_ref, o_ref, mx_ref, ext):
        i = pl.program_id(0)
        ext[pl.ds(0, POOL_HALO), :] = jnp.where(i > 0, tail_ref[...], 0.0)
        ext[pl.ds(POOL_HALO, tb), :] = v_ref[...]
        inv = _pool_inv_count(i * tb, tb, pg, ng)
        for g, w in enumerate(POOL_WINDOWS):
            cols = pl.ds(g * pg, pg)
            win = ext[pl.ds(POOL_HALO, tb), cols]
            for k in range(1, w):
                win = win + ext[pl.ds(POOL_HALO - k, tb), cols]
            pooled = (win * inv[g] - ext[pl.ds(POOL_HALO, tb), cols]).astype(BF16)
            o_ref[:, cols] = pooled
            mx_ref[:, cols] = jnp.dot(pooled, w_ref[g], preferred_element_type=F32)

    row = _bs((tb, P), lambda i: (i, 0))
    return _call(body, grid=(T // tb,),
                 in_specs=[row, _bs((POOL_HALO, P), lambda i: (jnp.maximum(i * hb - 1, 0), 0)),
                           _bs(wp.shape, lambda i: (0, 0, 0))],
                 out_specs=[row, row],
                 out_shape=[jax.ShapeDtypeStruct((T, P), BF16), jax.ShapeDtypeStruct((T, P), F32)],
                 scratch_shapes=[pltpu.VMEM((tb + POOL_HALO, P), F32)],
                 compiler_params=_cp(("arbitrary",)), name="pool_fwd")(proj, proj, wp)


def _pool_bwd(dmixed, wp, tb, comm=None):
    T, P = dmixed.shape
    ng = len(POOL_WINDOWS)
    pg = P // ng
    hb = tb // POOL_HALO
    nb = T // tb

    def core(d_ref, head_ref, w_ref, o_ref, ext, dpl):
        i = pl.program_id(0)
        inv = _pool_inv_count(i * tb, tb, pg, ng)
        invh = _pool_inv_count((i + 1) * tb, POOL_HALO, pg, ng)
        for g in range(ng):
            cols = pl.ds(g * pg, pg)
            dp = lax.dot_general(d_ref[:, cols], w_ref[g], (NT, ((), ())), preferred_element_type=F32)
            dph = lax.dot_general(head_ref[:, cols], w_ref[g], (NT, ((), ())), preferred_element_type=F32)
            dpl[:, cols] = dp
            ext[pl.ds(0, tb), cols] = dp * inv[g]
            ext[pl.ds(tb, POOL_HALO), cols] = jnp.where(i < nb - 1, dph * invh[g], 0.0)
        for g, w in enumerate(POOL_WINDOWS):
            cols = pl.ds(g * pg, pg)
            acc = ext[pl.ds(0, tb), cols]
            for k in range(1, w):
                acc = acc + ext[pl.ds(k, tb), cols]
            o_ref[:, cols] = (acc - dpl[:, cols]).astype(BF16)

    body, extra = _hosted(core, comm, (nb,), 3, 1, 2)
    return _call(body, grid=(nb,),
                 in_specs=[_bs((tb, P), lambda i: (i, 0)),
                           _bs((POOL_HALO, P), lambda i: (jnp.minimum((i + 1) * hb, T // POOL_HALO - 1), 0)),
                           _bs(wp.shape, lambda i: (0, 0, 0))] + extra["in_specs"],
                 out_specs=[_bs((tb, P), lambda i: (i, 0))] + extra["out_specs"],
                 out_shape=[jax.ShapeDtypeStruct((T, P), BF16)] + extra["out_shape"],
                 scratch_shapes=[pltpu.VMEM((tb + POOL_HALO, P), F32), pltpu.VMEM((tb, P), F32)] + extra["scratch"],
                 compiler_params=_cp(("arbitrary",)), name="pool_bwd")(dmixed, dmixed, wp, *extra["ins"])


def _zoh(a_re, a_im, ldt, b_re, b_im):
    lam_re = jnp.minimum(a_re, A_RE_MAX)
    lam_im = a_im
    dt = jnp.exp(ldt)
    mag = jnp.exp(lam_re * dt)
    ang = lam_im * dt
    ab_re = mag * jnp.cos(ang)
    ab_im = mag * jnp.sin(ang)
    den = lam_re * lam_re + lam_im * lam_im
    n_re = ab_re - 1.0
    n_im = ab_im
    q_re = (n_re * lam_re + n_im * lam_im) / den
    q_im = (n_im * lam_re - n_re * lam_im) / den
    return ab_re, ab_im, q_re * b_re - q_im * b_im, q_re * b_im + q_im * b_re


def _ssm_prep(a_re, a_im, ldt, bt_re, bt_im):
    shp = jax.ShapeDtypeStruct(a_re.shape, F32)

    def body(a, b, c, d, e, o0, o1, o2, o3):
        r = _zoh(a[...], b[...], c[...], d[...], e[...])
        o0[...], o1[...], o2[...], o3[...] = r

    return _call(body, in_specs=[VMEM_FULL] * 5, out_specs=[VMEM_FULL] * 4, out_shape=[shp] * 4,
                 name="ssm_prep")(a_re, a_im, ldt, bt_re, bt_im)


def _ssm_prep_bwd(a_re, a_im, ldt, bt_re, bt_im, dab_re, dab_im, dbb_re, dbb_im, G):
    GC, N = a_re.shape
    C = GC // G

    def body(a, b, c, d, e, g0, g1, g2, g3, da_re, da_im, dldt, db_re, db_im):
        _, vjp = jax.vjp(_zoh, a[...], b[...], c[...], d[...], e[...])
        ga_re, ga_im, gl, gb_re, gb_im = vjp((g0[...], g1[...], g2[...], g3[...]))
        da_re[...] = jnp.sum(ga_re.reshape(G, C, N), axis=1)
        da_im[...] = jnp.sum(ga_im.reshape(G, C, N), axis=1)
        dldt[...] = jnp.sum(jnp.sum(gl.reshape(G, C, N), axis=1), axis=1, keepdims=True)
        db_re[...] = gb_re
        db_im[...] = gb_im

    gn = jax.ShapeDtypeStruct((G, N), F32)
    full = jax.ShapeDtypeStruct((GC, N), F32)
    return _call(body, in_specs=[VMEM_FULL] * 9, out_specs=[VMEM_FULL] * 5,
                 out_shape=[gn, gn, jax.ShapeDtypeStruct((G, 1), F32), full, full],
                 name="ssm_prep_bwd")(a_re, a_im, ldt, bt_re, bt_im, dab_re, dab_im, dbb_re, dbb_im)


def _coef_tiles(abr, abi, reverse):
    ns = abr.shape[1]
    row = lax.broadcasted_iota(jnp.int32, (SUBLANES, ns), 0)
    ar = jnp.broadcast_to(abr, (SUBLANES, ns))
    ai = jnp.broadcast_to(-abi if reverse else abi, (SUBLANES, ns))
    a2r, a2i = ar * ar - ai * ai, 2.0 * ar * ai
    a4r, a4i = a2r * a2r - a2i * a2i, 2.0 * a2r * a2i
    out = []
    for d, (vr, vi) in ((1, (ar, ai)), (2, (a2r, a2i)), (4, (a4r, a4i))):
        keep = (row < SUBLANES - d) if reverse else (row >= d)
        out += [jnp.where(keep, vr, 0.0), jnp.where(keep, vi, 0.0)]
    pr, pi = ar, ai
    for k in range(1, SUBLANES):
        sel = (row <= SUBLANES - 1 - k) if reverse else (row >= k)
        nr, ni = pr * ar - pi * ai, pr * ai + pi * ar
        pr, pi = jnp.where(sel, nr, pr), jnp.where(sel, ni, pi)
    return out + [pr, pi]


def _cpow(ar, ai, n):
    out, br, bi = None, ar, ai
    while n:
        if n & 1:
            out = (br, bi) if out is None else (out[0] * br - out[1] * bi, out[0] * bi + out[1] * br)
        br, bi = br * br - bi * bi, 2.0 * br * bi
        n >>= 1
    return out


def _seg_perm_matrix(nrows):
    r = jnp.arange(nrows)
    src = (nrows // SUBLANES) * (r % SUBLANES) + r // SUBLANES
    return (src[:, None] == jnp.arange(nrows)[None, :]).astype(BF16)


def _seg_order_rows(pm, xb):
    return jnp.dot(pm, xb, preferred_element_type=F32).astype(BF16)


def _time_order_rows(pm, x, terms):
    out, rest = None, x
    for t in range(terms):
        piece = rest.astype(BF16)
        part = lax.dot_general(pm, piece, (TN, ((), ())), preferred_element_type=F32)
        out = part if out is None else out + part
        if t + 1 < terms:
            rest = rest - piece.astype(F32)
    return out


def _scan_tiles(abr, abi, seg, reverse):
    ns = abr.shape[1]
    seg_pow = _cpow(abr, abi, seg)
    step = [jnp.broadcast_to(abr, (SUBLANES, ns)), jnp.broadcast_to(-abi if reverse else abi, (SUBLANES, ns))]
    return _coef_tiles(seg_pow[0], seg_pow[1], reverse) + step


def _seg_scan(xr_ref, xi_ref, coef_ref, car_ref, cai_ref, *, nrows, ns, reverse, cmat=None, dab=None):
    seg = nrows // SUBLANES
    cw = min(SCAN_LANES, ns)
    row = lax.broadcasted_iota(jnp.int32, (SUBLANES, cw), 0)
    first, last = (SUBLANES - 1, 0) if reverse else (0, SUBLANES - 1)

    def tile(i):
        return pl.ds(pl.multiple_of(((seg - 1 - i) if reverse else i) * SUBLANES, SUBLANES), SUBLANES)

    for cc in range(ns // cw):
        cols = pl.ds(cc * cw, cw)
        ar, ai = coef_ref[8, :, cols], coef_ref[9, :, cols]

        def local(i, x, cols=cols, ar=ar, ai=ai):
            rows = tile(i)
            nr = ar * x[0] - ai * x[1] + xr_ref[rows, cols]
            ni = ar * x[1] + ai * x[0] + xi_ref[rows, cols]
            xr_ref[rows, cols] = nr
            xi_ref[rows, cols] = ni
            return nr, ni

        zero = jnp.zeros((SUBLANES, cw), F32)
        er, ei = lax.fori_loop(0, seg, local, (zero, zero))

        co = [coef_ref[k, :, cols] for k in range(8)]
        for lvl, d in enumerate((1, 2, 4)):
            kr, ki = co[2 * lvl], co[2 * lvl + 1]
            sh = SUBLANES - d if reverse else d
            sr, si = pltpu.roll(er, sh, 0), pltpu.roll(ei, sh, 0)
            er, ei = er + (kr * sr - ki * si), ei + (kr * si + ki * sr)
        c0r, c0i = car_ref[:, cols], cai_ref[:, cols]
        er, ei = er + (co[6] * c0r - co[7] * c0i), ei + (co[6] * c0i + co[7] * c0r)
        nb_shift = SUBLANES - 1 if reverse else 1
        cmr = jnp.where(row == first, c0r, pltpu.roll(er, nb_shift, 0))
        cmi = jnp.where(row == first, c0i, pltpu.roll(ei, nb_shift, 0))
        car_ref[:, cols] = jnp.broadcast_to(er[last:last + 1, :], er.shape)
        cai_ref[:, cols] = jnp.broadcast_to(ei[last:last + 1, :], ei.shape)
        if cmat is not None:
            cmat[0][:, cols] = cmr
            cmat[1][:, cols] = cmi

        w0 = (ar * cmr - ai * cmi, ar * cmi + ai * cmr)
        if dab is None:
            def fix(i, w, cols=cols, ar=ar, ai=ai):
                rows = tile(i)
                xr_ref[rows, cols] = xr_ref[rows, cols] + w[0]
                xi_ref[rows, cols] = xi_ref[rows, cols] + w[1]
                return ar * w[0] - ai * w[1], ar * w[1] + ai * w[0]

            lax.fori_loop(0, seg, fix, w0)
        else:
            s_re, s_im, e_re, e_im, o_re, o_im = dab

            def add(rows, w, pr, pi, acc):
                gr = xr_ref[rows, cols] + w[0]
                gi = xi_ref[rows, cols] + w[1]
                xr_ref[rows, cols] = gr
                xi_ref[rows, cols] = gi
                return acc[0] + (gr * pr + gi * pi), acc[1] + (gi * pr - gr * pi)

            def fix(i, st, cols=cols, ar=ar, ai=ai):
                w, acc = st[:2], st[2:]
                rows = tile(i)
                before = pl.ds(pl.multiple_of((seg - 2 - i) * SUBLANES, SUBLANES), SUBLANES)
                acc = add(rows, w, s_re[before, cols], s_im[before, cols], acc)
                return (ar * w[0] - ai * w[1], ar * w[1] + ai * w[0]) + acc

            st = lax.fori_loop(0, seg - 1, fix, w0 + (zero, zero))
            acc = add(pl.ds(0, SUBLANES), st[:2], e_re[:, cols], e_im[:, cols], st[2:])
            o_re[:, cols] += jnp.sum(acc[0], axis=0, keepdims=True)
            o_im[:, cols] += jnp.sum(acc[1], axis=0, keepdims=True)


def _hosted(core, comm, grid, n_in, n_out, n_scratch):
    ci = len(comm["ins"]) if comm else 0
    co = len(comm["out_shape"]) if comm else 0

    def body(*refs):
        ins, rest = refs[:n_in + ci], refs[n_in + ci:]
        outs, scr = rest[:n_out + co], rest[n_out + co:]
        hooks = functools.partial(_comm_hooks, comm, grid, ins[n_in:], outs[n_out:], scr[n_scratch:])
        hooks(before=True)
        core(*ins[:n_in], *outs[:n_out], *scr[:n_scratch])
        hooks(before=False)

    aliases = {n_in + i: n_out + i for i in range(co)} if comm and comm.get("alias") else {}
    extra = dict(ins=list(comm["ins"]) if comm else [], in_specs=[ANY] * ci, out_specs=[ANY] * co,
                 out_shape=list(comm["out_shape"]) if comm else [], scratch=list(comm["scratch"]) if comm else [],
                 aliases=aliases)
    return body, extra


def _ssm_fwd(proj, bdr, bdi, cdr, cdi, abr, abi, dsk, P, tb, comm=None):
    T = proj.shape[0]
    ntl, ct, st = bdr.shape
    ns = ntl * st
    nb = T // tb

    def core(u_ref, bdr_ref, bdi_ref, cdr_ref, cdi_ref, abr_ref, abi_ref, d_ref, pm_ref,
             y_ref, ge_ref, bsr_ref, bsi_ref, sr, si, coef, car, cai, up):
        @pl.when(pl.program_id(0) == 0)
        def _():
            for k, tile in enumerate(_scan_tiles(abr_ref[...], abi_ref[...], tb // SUBLANES, False)):
                coef[k] = tile
            car[...] = jnp.zeros_like(car)
            cai[...] = jnp.zeros_like(cai)

        bsr_ref[...] = car[...]
        bsi_ref[...] = cai[...]
        u = u_ref[...]
        ub = _seg_order_rows(pm_ref[...], u.astype(BF16))
        for s in range(ntl):
            us = ub[:, s * ct:(s + 1) * ct]
            sr[:, s * st:(s + 1) * st] = jnp.dot(us, bdr_ref[s], preferred_element_type=F32)
            si[:, s * st:(s + 1) * st] = jnp.dot(us, bdi_ref[s], preferred_element_type=F32)
        _seg_scan(sr, si, coef, car, cai, nrows=tb, ns=ns, reverse=False)
        for s in range(ntl):
            s_re = sr[:, s * st:(s + 1) * st].astype(BF16)
            s_im = si[:, s * st:(s + 1) * st].astype(BF16)
            up[:, s * ct:(s + 1) * ct] = (jnp.dot(s_re, cdr_ref[s], preferred_element_type=F32)
                                          - jnp.dot(s_im, cdi_ref[s], preferred_element_type=F32))
        y = _time_order_rows(pm_ref[...], up[...], 3) + d_ref[...] * u
        y_ref[...] = y
        ge_ref[...] = _gelu(y).astype(BF16)

    full3 = lambda a: _bs(a.shape, lambda i: (0, 0, 0))
    vec = lambda n: _bs((1, n), lambda i: (0, 0))
    row = _bs((tb, P), lambda i: (i, 0))
    st_spec = _bs((None, SUBLANES, ns), lambda i: (i, 0, 0))
    body, extra = _hosted(core, comm, (nb,), 9, 4, 6)
    return _call(body, grid=(nb,),
                 in_specs=[_bs((tb, P), lambda i: (i, 2)), full3(bdr), full3(bdi), full3(cdr), full3(cdi),
                           vec(ns), vec(ns), vec(P), _bs((tb, tb), lambda i: (0, 0))] + extra["in_specs"],
                 out_specs=[row, row, st_spec, st_spec] + extra["out_specs"],
                 out_shape=[jax.ShapeDtypeStruct((T, P), F32), jax.ShapeDtypeStruct((T, P), BF16),
                            jax.ShapeDtypeStruct((nb, SUBLANES, ns), F32),
                            jax.ShapeDtypeStruct((nb, SUBLANES, ns), F32)] + extra["out_shape"],
                 scratch_shapes=[pltpu.VMEM((tb, ns), F32), pltpu.VMEM((tb, ns), F32),
                                 pltpu.VMEM((10, SUBLANES, ns), F32),
                                 pltpu.VMEM((SUBLANES, ns), F32), pltpu.VMEM((SUBLANES, ns), F32),
                                 pltpu.VMEM((tb, P), F32)] + extra["scratch"],
                 compiler_params=_cp(("arbitrary",)), name="ssm_fwd")(
                     proj, bdr, bdi, cdr, cdi, abr, abi, dsk, _seg_perm_matrix(tb), *extra["ins"])


def _ssm_bwd(proj, y, dge, bsr, bsi, bdr, bdi, cdr, cdi, abr, abi, dsk, dpi, dpg, dsg, P, tb, comm=None):
    T = proj.shape[0]
    ntl, ct, st = bdr.shape
    ns = ntl * st
    nb = T // tb

    def core(u_ref, y_ref, dge_ref, bsr_ref, bsi_ref, abr_ref, abi_ref, d_ref, pm_ref, dpi_ref, dpg_ref, dsg_ref,
             bdr_h, bdi_h, cdr_h, cdi_h,
             dproj_ref, dabr_ref, dabi_ref, dd_ref, dbdr_h, dbdi_h, dcdr_h, dcdi_h,
             wbdr, wbdi, wcdr, wcdi, abdr, abdi, acdr, acdi, spr, spi, gr, gi, coef_f, coef_r,
             car, cai, gcr, gci, ser, sei, dup):
        i = pl.program_id(0)

        @pl.when(i == 0)
        def _():
            for h, w in ((bdr_h, wbdr), (bdi_h, wbdi), (cdr_h, wcdr), (cdi_h, wcdi)):
                pltpu.sync_copy(h, w)
            for a in (abdr, abdi, acdr, acdi, gcr, gci):
                a[...] = jnp.zeros_like(a)
            for o in (dabr_ref, dabi_ref, dd_ref):
                o[...] = jnp.zeros_like(o)
            for k, tile in enumerate(_scan_tiles(abr_ref[...], abi_ref[...], tb // SUBLANES, False)):
                coef_f[k] = tile
            for k, tile in enumerate(_scan_tiles(abr_ref[...], abi_ref[...], tb // SUBLANES, True)):
                coef_r[k] = tile

        car[...] = bsr_ref[...]
        cai[...] = bsi_ref[...]
        u = u_ref[...]
        dy = dge_ref[...] * _gelu_grad(y_ref[...])
        ub = _seg_order_rows(pm_ref[...], u.astype(BF16))
        dyb = _seg_order_rows(pm_ref[...], dy.astype(BF16))
        for s in range(ntl):
            us = ub[:, s * ct:(s + 1) * ct]
            spr[:, s * st:(s + 1) * st] = jnp.dot(us, wbdr[s], preferred_element_type=F32)
            spi[:, s * st:(s + 1) * st] = jnp.dot(us, wbdi[s], preferred_element_type=F32)
        _seg_scan(spr, spi, coef_f, car, cai, nrows=tb, ns=ns, reverse=False, cmat=(ser, sei))

        for s in range(ntl):
            dys = dyb[:, s * ct:(s + 1) * ct]
            gr[:, s * st:(s + 1) * st] = lax.dot_general(dys, wcdr[s], (NT, ((), ())), preferred_element_type=F32)
            gi[:, s * st:(s + 1) * st] = -lax.dot_general(dys, wcdi[s], (NT, ((), ())), preferred_element_type=F32)
        _seg_scan(gr, gi, coef_r, gcr, gci, nrows=tb, ns=ns, reverse=True,
                  dab=(spr, spi, ser, sei, dabr_ref, dabi_ref))

        for s in range(ntl):
            sl_c, sl_s = slice(s * ct, (s + 1) * ct), slice(s * st, (s + 1) * st)
            s_re = spr[:, sl_s].astype(BF16)
            s_im = spi[:, sl_s].astype(BF16)
            g_re, g_im = gr[:, sl_s].astype(BF16), gi[:, sl_s].astype(BF16)
            dys, us = dyb[:, sl_c], ub[:, sl_c]
            acdr[s] += lax.dot_general(s_re, dys, (TN, ((), ())), preferred_element_type=F32)
            acdi[s] -= lax.dot_general(s_im, dys, (TN, ((), ())), preferred_element_type=F32)
            abdr[s] += lax.dot_general(us, g_re, (TN, ((), ())), preferred_element_type=F32)
            abdi[s] += lax.dot_general(us, g_im, (TN, ((), ())), preferred_element_type=F32)
            dup[:, sl_c] = (lax.dot_general(g_re, wbdr[s], (NT, ((), ())), preferred_element_type=F32)
                            + lax.dot_general(g_im, wbdi[s], (NT, ((), ())), preferred_element_type=F32))
        dd_ref[...] += jnp.sum(dy * u, axis=0, keepdims=True)
        du = _time_order_rows(pm_ref[...], dup[...], 2) + d_ref[...] * dy
        dproj_ref[:, 0:P] = dpi_ref[...]
        dproj_ref[:, P:2 * P] = dpg_ref[...]
        dproj_ref[:, 2 * P:3 * P] = du.astype(BF16)
        dproj_ref[:, 3 * P:4 * P] = dsg_ref[...]

        @pl.when(i == nb - 1)
        def _():
            for a, h in ((abdr, dbdr_h), (abdi, dbdi_h), (acdr, dcdr_h), (acdi, dcdi_h)):
                pltpu.sync_copy(a, h)

    rev = lambda i: nb - 1 - i
    vec = lambda n: _bs((1, n), lambda i: (0, 0))
    row = _bs((tb, P), lambda i: (rev(i), 0))
    st_spec = _bs((None, SUBLANES, ns), lambda i: (rev(i), 0, 0))
    bshape = jax.ShapeDtypeStruct(bdr.shape, F32)
    cshape = jax.ShapeDtypeStruct(cdr.shape, F32)
    body, extra = _hosted(core, comm, (nb,), 16, 8, 21)
    return _call(body, grid=(nb,),
                 in_specs=[_bs((tb, P), lambda i: (rev(i), 2)), row, row, st_spec, st_spec,
                           vec(ns), vec(ns), vec(P), _bs((tb, tb), lambda i: (0, 0)), row, row, row,
                           ANY, ANY, ANY, ANY] + extra["in_specs"],
                 out_specs=[_bs((tb, 4 * P), lambda i: (rev(i), 0)), vec(ns), vec(ns), vec(P), ANY, ANY, ANY, ANY]
                 + extra["out_specs"],
                 out_shape=[jax.ShapeDtypeStruct((T, 4 * P), BF16), jax.ShapeDtypeStruct((1, ns), F32),
                            jax.ShapeDtypeStruct((1, ns), F32), jax.ShapeDtypeStruct((1, P), F32),
                            bshape, bshape, cshape, cshape] + extra["out_shape"],
                 scratch_shapes=[pltpu.VMEM(bdr.shape, BF16), pltpu.VMEM(bdr.shape, BF16),
                                 pltpu.VMEM(cdr.shape, BF16), pltpu.VMEM(cdr.shape, BF16),
                                 pltpu.VMEM(bdr.shape, F32), pltpu.VMEM(bdr.shape, F32),
                                 pltpu.VMEM(cdr.shape, F32), pltpu.VMEM(cdr.shape, F32),
                                 pltpu.VMEM((tb, ns), F32), pltpu.VMEM((tb, ns), F32),
                                 pltpu.VMEM((tb, ns), F32), pltpu.VMEM((tb, ns), F32),
                                 pltpu.VMEM((10, SUBLANES, ns), F32), pltpu.VMEM((10, SUBLANES, ns), F32)]
                 + [pltpu.VMEM((SUBLANES, ns), F32)] * 6 + [pltpu.VMEM((tb, P), F32)] + extra["scratch"],
                 compiler_params=_cp(("arbitrary",)), name="ssm_bwd")(
                     proj, y, dge, bsr, bsi, abr, abi, dsk, _seg_perm_matrix(tb), dpi, dpg, dsg,
                     bdr, bdi, cdr, cdi, *extra["ins"])


def _adamw(w, g, m, v, name, comm=None):
    R, C = w.shape
    tr = _t(R, ROWS_ELEMENTWISE)

    def core(w_ref, g_ref, m_ref, v_ref, d_ref, mo_ref, vo_ref):
        gv = g_ref[...]
        mn = ADAM_B1 * m_ref[...] + (1.0 - ADAM_B1) * gv
        vn = ADAM_B2 * v_ref[...] + (1.0 - ADAM_B2) * (gv * gv)
        m_hat = mn / (1.0 - ADAM_B1 ** ADAM_STEP)
        v_hat = vn / (1.0 - ADAM_B2 ** ADAM_STEP)
        d_ref[...] = -ADAM_LR * (m_hat / (jnp.sqrt(v_hat) + ADAM_EPS) + ADAM_WD * w_ref[...])
        mo_ref[...] = mn
        vo_ref[...] = vn

    blk = _bs((tr, C), lambda i: (i, 0))
    shp = jax.ShapeDtypeStruct((R, C), F32)
    body, extra = _hosted(core, comm, (R // tr,), 4, 3, 0)
    return _call(body, grid=(R // tr,), in_specs=[blk] * 4 + extra["in_specs"],
                 out_specs=[blk] * 3 + extra["out_specs"], out_shape=[shp] * 3 + extra["out_shape"],
                 scratch_shapes=extra["scratch"],
                 compiler_params=_cp(("arbitrary",) if comm else ("parallel",)), name=name)(w, g, m, v, *extra["ins"])


def _sum_cast(grad, got, place, name):
    J, H, C = got.shape
    tr = _t(H, ROWS_ELEMENTWISE)
    nb = H // tr

    def body(pl_ref, a_ref, b_ref, o_ref):
        o_ref[...] = (a_ref[...] + b_ref[...]).astype(BF16)

    blk = _bs((None, tr, C), lambda j, i, pc: (j, i, 0))
    mine = _bs((None, tr, C), lambda j, i, pc: (j, pc[1] * nb + i, 0))
    spec = pltpu.PrefetchScalarGridSpec(num_scalar_prefetch=1, grid=(J, nb), in_specs=[mine, blk], out_specs=blk)
    return _call(body, grid_spec=spec, out_shape=jax.ShapeDtypeStruct((J, H, C), BF16),
                 compiler_params=_cp(("parallel", "parallel")), name=name)(place, grad, got)


def _sum_chips(sent, arrived, place, name):
    J, H, C = arrived.shape
    tr = _t(H, ROWS_ELEMENTWISE)
    nb = H // tr

    def body(pl_ref, own_ref, a0_ref, a1_ref, a2_ref, o_ref):
        acc = own_ref[...].astype(F32)
        for r in (a0_ref, a1_ref, a2_ref):
            acc = acc + r[...].astype(F32)
        o_ref[...] = acc

    def other(k):
        return _bs((None, tr, C), lambda i, pc: (jnp.where(pc[0] <= k, k + 1, k), i, 0))

    spec = pltpu.PrefetchScalarGridSpec(
        num_scalar_prefetch=1, grid=(nb,),
        in_specs=[_bs((None, tr, C), lambda i, pc: (pc[0], i, 0)), other(0), other(1), other(2)],
        out_specs=_bs((tr, C), lambda i, pc: (pc[1] * nb + i, 0)))
    return _call(body, grid_spec=spec, out_shape=jax.ShapeDtypeStruct((2 * H, C), F32),
                 compiler_params=_cp(("parallel",)), name=name)(place, sent, arrived, arrived, arrived)


def _place():
    x, y, c = lax.axis_index("x"), lax.axis_index("y"), lax.axis_index("c")
    chips = [(1 - x, y), (x, 1 - y), (1 - x, 1 - y)]
    return x, y, c, chips


def _split(nrows, row_bytes, align, cap=None):
    k = max(1, min(cap or DMA_MAX_CHUNKS, (nrows * row_bytes) // DMA_CHUNK_BYTES))
    while k > 1 and nrows % (k * align):
        k -= 1
    return k


def _comm_call(plan, name):
    n_in, n_out = len(plan["ins"]), len(plan["out_shape"])

    def body(*refs):
        for phase in plan["phases"]:
            phase(refs[:n_in], refs[n_in:n_in + n_out], refs[n_in + n_out:])

    return _call(body, in_specs=[ANY] * n_in, out_specs=[ANY] * n_out, out_shape=plan["out_shape"],
                 input_output_aliases={i: i for i in range(n_out)} if plan.get("alias") else {},
                 scratch_shapes=plan["scratch"], name=name)(*plan["ins"])


def _comm_hooks(plan, grid, ins, outs, sems, *, before):
    if plan is None:
        return
    nsteps, step = 1, 0
    for d, g in enumerate(grid):
        nsteps, step = nsteps * g, step * g + pl.program_id(d)
    for p, (phase, frac) in enumerate(zip(plan["phases"], plan["at"])):
        if (p == 0) == before:
            pl.when(step == int(frac * (nsteps - 1)))(functools.partial(phase, ins, outs, sems))


def _ag_plan(shards, axes):
    n = len(shards)
    shapes = [a.shape for a in shards]

    def window(ref, i, chip, half=None):
        S, ax = shapes[i], axes[i]
        idx = []
        for d in range(len(S)):
            off, size = 0, S[d]
            if d == 0 and half is not None:
                off, size = half * (S[0] // 2), S[0] // 2
            if d == ax:
                off = off + chip * S[ax]
            idx.append(pl.ds(off, size))
        return ref.at[tuple(idx)]

    def copies(src, full, sems):
        ssem, rsem = sems
        x, y, c, chips = _place()
        me = 2 * x + y
        sib = (x, y, 1 - c)
        idx = [2 * cx + cy for cx, cy in chips]

        def rcopy(i, k, s_ref, d_ref, to):
            return pltpu.make_async_remote_copy(src_ref=s_ref, dst_ref=d_ref, send_sem=ssem.at[i, k],
                                                recv_sem=rsem.at[i, k], device_id=to, device_id_type=MESH)

        def ici(i, j, incoming):
            half_src = src[i].at[pl.ds(c * (shapes[i][0] // 2), shapes[i][0] // 2)]
            return rcopy(i, j, half_src, window(full[i], i, idx[j] if incoming else me, c), (*chips[j], c))

        def fwd(i, j, half):
            w = window(full[i], i, idx[j], half)
            return rcopy(i, 3 + j, w, w, sib)

        def own(i):
            return rcopy(i, 6, src[i], window(full[i], i, me), sib)

        return c, ici, fwd, own

    def send(src, full, sems):
        c, ici, fwd, own = copies(src, full, sems)
        for i in range(n):
            for j in range(3):
                ici(i, j, False).start()
        for i in range(n):
            own(i).start()

    def forward(i, src, full, sems):
        c, ici, fwd, own = copies(src, full, sems)
        for j in range(3):
            ici(i, j, True).wait_recv()
            fwd(i, j, c).start()

    def finish(src, full, sems):
        c, ici, fwd, own = copies(src, full, sems)
        for i in range(n):
            for j in range(3):
                fwd(i, j, 1 - c).wait_recv()
            own(i).wait()
        for i in range(n):
            for j in range(3):
                ici(i, j, False).wait_send()
                fwd(i, j, c).wait_send()

    out_shape = [jax.ShapeDtypeStruct(tuple(N_CHIP * d if k == ax else d for k, d in enumerate(S)), BF16)
                 for S, ax in zip(shapes, axes)]
    sizes = [a.size for a in shards]
    behind = [AG_FORWARD_SCALE * sum(sizes[:i + 1]) / sum(sizes) + AG_FORWARD_LAG for i in range(n)]
    return dict(ins=list(shards), out_shape=out_shape,
                phases=[send] + [functools.partial(forward, i) for i in range(n)] + [finish],
                at=[0.0] + behind + [1.0],
                scratch=[pltpu.SemaphoreType.DMA((n, 7)), pltpu.SemaphoreType.DMA((n, 7))])


def _proj_ag(x, g1, wsh, order, tm):
    T, D = x.shape
    P = wsh.shape[1]
    H = D // 2
    nt = T // tm

    def body(order_ref, x_ref, g_ref, wsh_ref, hn_ref, proj_ref, win_ref, wbuf, lsem, ssem, rsem):
        n, i = pl.program_id(0), pl.program_id(1)
        x, y, c, chips = _place()
        me = 2 * x + y
        sib = (x, y, 1 - c)
        idx = [2 * cx + cy for cx, cy in chips]

        def rcopy(k, s_ref, d_ref, to):
            return pltpu.make_async_remote_copy(src_ref=s_ref, dst_ref=d_ref, send_sem=ssem.at[k],
                                                recv_sem=rsem.at[k], device_id=to, device_id_type=MESH)

        def cols(chip):
            return pl.ds(pl.multiple_of(chip * P, LANES), P)

        def rows(half):
            return pl.ds(pl.multiple_of(half * H, BF16_TILE_ROWS), H)

        def ici(j, incoming):
            return rcopy(j, wsh_ref.at[rows(c)], win_ref.at[rows(c), cols(idx[j] if incoming else me)],
                         (*chips[j], c))

        def fwd(j, half):
            w = win_ref.at[rows(half), cols(idx[j])]
            return rcopy(3 + j, w, w, sib)

        def own():
            return rcopy(6, wsh_ref, win_ref.at[:, cols(me)], sib)

        def load(src):
            cp = pltpu.make_async_copy(src, wbuf, lsem)
            cp.start()
            cp.wait()

        @pl.when((n == 0) & (i == 0))
        def _():
            ici(0, False).start()
            ici(1, False).start()
            own().start()
            load(wsh_ref)

        for j in range(3):
            @pl.when((n == j + 1) & (i == 0))
            def _(j=j):
                if j == 0:
                    ici(2, False).start()
                ici(j, True).wait_recv()
                fwd(j, c).start()
                fwd(j, 1 - c).wait_recv()
                load(win_ref.at[:, cols(idx[j])])

        xv = x_ref[...]
        r = lax.rsqrt(jnp.mean(xv * xv, axis=-1, keepdims=True) + EPS)
        hn = ((xv * r) * g_ref[...]).astype(BF16)

        @pl.when(n == 0)
        def _():
            hn_ref[...] = hn

        proj_ref[...] = jnp.dot(hn, wbuf[...], preferred_element_type=F32)

        @pl.when((n == 3) & (i == nt - 1))
        def _():
            own().wait()
            for j in range(3):
                ici(j, False).wait_send()
                fwd(j, c).wait_send()

    spec = pltpu.PrefetchScalarGridSpec(
        num_scalar_prefetch=1, grid=(N_CHIP, nt),
        in_specs=[_bs((tm, D), lambda n, i, o: (i, 0)), _bs((1, D), lambda n, i, o: (0, 0)), ANY],
        out_specs=[_bs((tm, D), lambda n, i, o: (jnp.where(n == 0, i, nt - 1), 0)),
                   _bs((tm, P), lambda n, i, o: (i, o[n])), ANY],
        scratch_shapes=[pltpu.VMEM((D, P), BF16), pltpu.SemaphoreType.DMA,
                        pltpu.SemaphoreType.DMA((7,)), pltpu.SemaphoreType.DMA((7,))])
    return _call(body, grid_spec=spec,
                 out_shape=[jax.ShapeDtypeStruct((T, D), BF16), jax.ShapeDtypeStruct((T, N_CHIP * P), F32),
                            jax.ShapeDtypeStruct((D, N_CHIP * P), BF16)],
                 compiler_params=_cp(("arbitrary", "arbitrary")), name="proj_ag")(order, x, g1, wsh)


def _halves_plan(grads):
    n = len(grads)

    def send(g, got, sems):
        ssem, rsem = sems
        x, y, c, _ = _place()
        sib = (x, y, 1 - c)
        for i in range(n):
            J, R, C = g[i].shape
            H = R // 2
            size = g[i].dtype.itemsize
            tile_rows = SUBLANES * 4 // size
            k = _split(H, C * size, tile_rows, cap=DMA_MAX_CHUNKS // J)
            hr = H // k
            for j in range(J):
                for q in range(k):
                    other = pl.ds(pl.multiple_of((1 - c) * H + q * hr, tile_rows), hr)
                    to = pl.ds(q * hr, hr)
                    pltpu.make_async_remote_copy(src_ref=g[i].at[j, other, :], dst_ref=got[i].at[j, to, :],
                                                 send_sem=ssem.at[i], recv_sem=rsem.at[i],
                                                 device_id=sib, device_id_type=MESH).start()

    def finish(g, got, sems):
        ssem, rsem = sems
        x, y, c, _ = _place()
        for i in range(n):
            pltpu.make_async_remote_copy(src_ref=got[i], dst_ref=got[i], send_sem=ssem.at[i], recv_sem=rsem.at[i],
                                         device_id=(x, y, 1 - c), device_id_type=MESH).wait()

    half = [jax.ShapeDtypeStruct((a.shape[0], a.shape[1] // 2, a.shape[2]), a.dtype) for a in grads]
    return dict(ins=list(grads), out_shape=half, phases=[send, finish], at=[0.0, 1.0],
                scratch=[pltpu.SemaphoreType.DMA((n,)), pltpu.SemaphoreType.DMA((n,))])


def _scatter_plan(parts):
    n = len(parts)

    def peers():
        x, y, c, chips = _place()
        return 2 * x + y, c, chips, [2 * cx + cy for cx, cy in chips]

    def send(s, got, sems):
        ssem, rsem = sems
        me, c, chips, idx = peers()
        for i in range(n):
            _, H, C = s[i].shape
            k = _split(H, C * 2, BF16_TILE_ROWS, cap=RS_CHUNKS)
            hr = H // k
            for q in range(k):
                rows = pl.ds(q * hr, hr)
                for j in range(3):
                    pltpu.make_async_remote_copy(src_ref=s[i].at[idx[j], rows, :], dst_ref=got[i].at[me, rows, :],
                                                 send_sem=ssem.at[i, j], recv_sem=rsem.at[i, j],
                                                 device_id=(*chips[j], c), device_id_type=MESH).start()

    def finish(s, got, sems):
        ssem, rsem = sems
        me, c, chips, idx = peers()
        for i in range(n):
            for j in range(3):
                pltpu.make_async_remote_copy(src_ref=s[i].at[idx[j]], dst_ref=got[i].at[idx[j]],
                                             send_sem=ssem.at[i, j], recv_sem=rsem.at[i, j],
                                             device_id=(*chips[j], c), device_id_type=MESH).wait()

    return dict(ins=list(parts), out_shape=[jax.ShapeDtypeStruct(a.shape, a.dtype) for a in parts],
                phases=[send, finish], at=[0.0, 1.0],
                scratch=[pltpu.SemaphoreType.DMA((n, 3)), pltpu.SemaphoreType.DMA((n, 3))])


def _join_plan(shards):
    n = len(shards)

    def send(_, full, sems):
        ssem, rsem = sems
        x, y, c, _ = _place()
        sib = (x, y, 1 - c)
        for i in range(n):
            H, C = full[i].shape[0] // 2, full[i].shape[1]
            k = _split(H, C * 4, SUBLANES)
            hr = H // k
            for q in range(k):
                rows = pl.ds(pl.multiple_of(c * H + q * hr, SUBLANES), hr)
                pltpu.make_async_remote_copy(src_ref=full[i].at[rows], dst_ref=full[i].at[rows],
                                             send_sem=ssem.at[i], recv_sem=rsem.at[i],
                                             device_id=sib, device_id_type=MESH).start()

    def finish(_, full, sems):
        ssem, rsem = sems
        x, y, c, _ = _place()
        for i in range(n):
            half = full[i].at[pl.ds(0, full[i].shape[0] // 2)]
            pltpu.make_async_remote_copy(src_ref=half, dst_ref=half, send_sem=ssem.at[i], recv_sem=rsem.at[i],
                                         device_id=(x, y, 1 - c), device_id_type=MESH).wait()

    return dict(ins=list(shards), out_shape=[jax.ShapeDtypeStruct(a.shape, a.dtype) for a in shards],
                phases=[send, finish], at=[0.0, 1.0], alias=True,
                scratch=[pltpu.SemaphoreType.DMA((n,)), pltpu.SemaphoreType.DMA((n,))])


def _allreduce_plan(buf):
    R, L = buf.shape
    RB = R // N_DEV

    def parts(sems):
        xv, got, ov, lsem, ssem, rsem = sems
        x, y, c, _ = _place()
        me = 4 * x + 2 * y + c

        def dev(k):
            return (k // 4, (k // 2) % 2, k % 2)

        def slab(k):
            return pl.ds(pl.multiple_of(k * RB, SUBLANES), RB)

        def first(d, to, landing):
            return pltpu.make_async_remote_copy(src_ref=xv.at[slab(to)], dst_ref=got.at[landing],
                                                send_sem=ssem.at[0, d], recv_sem=rsem.at[0, d],
                                                device_id=dev(to), device_id_type=MESH)

        def second(d, to, k):
            return pltpu.make_async_remote_copy(src_ref=ov.at[slab(k)], dst_ref=ov.at[slab(k)],
                                                send_sem=ssem.at[1, d], recv_sem=rsem.at[1, d],
                                                device_id=dev(to), device_id_type=MESH)

        return me, slab, first, second

    def scatter(ins, outs, sems):
        xv, lsem = sems[0], sems[3]
        me, slab, first, second = parts(sems)
        cp = pltpu.make_async_copy(ins[0], xv, lsem)
        cp.start()
        cp.wait()
        for d in range(1, N_DEV):
            first(d, (me + d) % N_DEV, me).start()

    def reduce(ins, outs, sems):
        xv, got, ov = sems[:3]
        me, slab, first, second = parts(sems)
        got[me] = xv[slab(me), :]
        for d in range(1, N_DEV):
            src = (me + N_DEV - d) % N_DEV
            first(d, src, src).wait_recv()
        acc = got[0]
        for k in range(1, N_DEV):
            acc = acc + got[k]
        ov[slab(me), :] = acc
        for d in range(1, N_DEV):
            second(d, (me + d) % N_DEV, me).start()

    def collect(ins, outs, sems):
        ov, lsem = sems[2], sems[3]
        me, slab, first, second = parts(sems)
        for d in range(1, N_DEV):
            src = (me + N_DEV - d) % N_DEV
            second(d, src, src).wait_recv()
        for d in range(1, N_DEV):
            peer = (me + d) % N_DEV
            first(d, peer, me).wait_send()
            second(d, peer, me).wait_send()
        cp = pltpu.make_async_copy(ov, outs[0], lsem)
        cp.start()
        cp.wait()

    return dict(ins=[buf], out_shape=[jax.ShapeDtypeStruct((R, L), F32)], phases=[scatter, reduce, collect],
                at=[0.0, 0.5, 1.0],
                scratch=[pltpu.VMEM((R, L), F32), pltpu.VMEM((N_DEV, RB, L), F32), pltpu.VMEM((R, L), F32),
                         pltpu.SemaphoreType.DMA, pltpu.SemaphoreType.DMA((2, N_DEV)),
                         pltpu.SemaphoreType.DMA((2, N_DEV))])


def _block_diag(t, gt):
    G, A, B = t.shape
    t4 = t.reshape(G // gt, gt, A, B)
    eye = jnp.eye(gt, dtype=t.dtype)
    return jnp.einsum('sgab,gh->sgahb', t4, eye).reshape(G // gt, gt * A, gt * B)


def _block_diag_extract(m, gt, A, B):
    S = m.shape[0]
    m5 = m.reshape(S, gt, A, gt, B)
    eye = jnp.eye(gt, dtype=m.dtype)
    return jnp.einsum('sgahb,gh->sgab', m5, eye).reshape(S * gt, A, B)


def _tile_rows(n):
    return -(-n // (SUBLANES * LANES)) * SUBLANES


def _pack_small(arrs, rows):
    parts = []
    for a in arrs:
        flat = a.reshape(-1).astype(F32)
        r = _tile_rows(flat.shape[0])
        parts.append(jnp.pad(flat, (0, r * LANES - flat.shape[0])).reshape(r, LANES))
    used = sum(p.shape[0] for p in parts)
    if rows > used:
        parts.append(jnp.zeros((rows - used, LANES), F32))
    return jnp.concatenate(parts)


def _unpack_small(buf, shapes):
    out, off = [], 0
    for s in shapes:
        n = 1
        for d in s:
            n *= d
        r = _tile_rows(n)
        piece = buf[off:off + r]
        out.append(piece.reshape(s) if n == r * LANES else piece.reshape(-1)[:n].reshape(s))
        off += r
    return out


def kernel(x, p, norm_gain, w_in, w_pool, pool_scale, a_re, a_im, log_dt, b_re, b_im, c_re, c_im, d_skip, w_glu, w_out, w_ple, w_ple_gate, final_gain, loss_target, m_norm_gain, m_w_in, m_w_pool, m_pool_scale, m_a_re, m_a_im, m_log_dt, m_b_re, m_b_im, m_c_re, m_c_im, m_d_skip, m_w_glu, m_w_out, m_w_ple, m_w_ple_gate, m_final_gain, v_norm_gain, v_w_in, v_w_pool, v_pool_scale, v_a_re, v_a_im, v_log_dt, v_b_re, v_b_im, v_c_re, v_c_im, v_d_skip, v_w_glu, v_w_out, v_w_ple, v_w_ple_gate, v_final_gain):
    xs, pe, tgt = x[0], p[0, 0], loss_target[0]
    T, D = xs.shape
    E = pe.shape[1]
    P = D // 2
    NG = len(POOL_WINDOWS)
    PG = P // NG
    G, N, C = P // SSM_GROUP, SSM_STATE, SSM_GROUP
    GT = min(SSM_TILE_GROUPS, G)
    Q = D // N_CHIP

    big = {"w_in": (w_in, m_w_in, v_w_in), "w_pool": (w_pool, m_w_pool, v_w_pool),
           "w_glu": (w_glu, m_w_glu, v_w_glu), "w_out": (w_out, m_w_out, v_w_out),
           "w_ple": (w_ple, m_w_ple, v_w_ple), "w_ple_gate": (w_ple_gate, m_w_ple_gate, v_w_ple_gate)}
    big_names = list(big)
    shard2d = {n: (big[n][0].size // big[n][0].shape[-1], big[n][0].shape[-1]) for n in big_names}
    shard_axis = {"w_in": 1, "w_pool": 1, "w_glu": 1, "w_out": 0, "w_ple": 1, "w_ple_gate": 0}
    shard16 = {n: big[n][0][0].astype(BF16) for n in big_names}
    place = jnp.stack([2 * lax.axis_index("x") + lax.axis_index("y"), lax.axis_index("c")]).astype(jnp.int32)
    mx, my = lax.axis_index("x"), lax.axis_index("y")
    block_order = jnp.stack([2 * mx + my, 2 * (1 - mx) + my, 2 * mx + (1 - my),
                             2 * (1 - mx) + (1 - my)]).astype(jnp.int32)
    later = [n for n in big_names if n != "w_in"]
    ag_later = _ag_plan([shard16[n] for n in later], [shard_axis[n] for n in later])

    rep = lambda a: jnp.repeat(a, C, axis=0)
    a_re_r, a_im_r = rep(a_re[0]), rep(a_im[0])
    ldt_r = rep(jnp.broadcast_to(log_dt[0][:, None], (G, N)))
    bt_re = b_re[0].transpose(0, 2, 1).reshape(G * C, N)
    bt_im = b_im[0].transpose(0, 2, 1).reshape(G * C, N)
    ab_re_r, ab_im_r, bbt_re, bbt_im = _ssm_prep(a_re_r, a_im_r, ldt_r, bt_re, bt_im)
    abr = ab_re_r[::C].reshape(1, G * N)
    abi = ab_im_r[::C].reshape(1, G * N)
    bdr = _block_diag(bbt_re.reshape(G, C, N), GT).astype(BF16)
    bdi = _block_diag(bbt_im.reshape(G, C, N), GT).astype(BF16)
    cdr = _block_diag(c_re[0].transpose(0, 2, 1), GT).astype(BF16)
    cdi = _block_diag(c_im[0].transpose(0, 2, 1), GT).astype(BF16)

    tb = _t(T, ROWS_ELEMENTWISE)
    tbs = _t(T, ROWS_SSM)
    tm = _t(T, ROWS_MATMUL)
    tk = _t(T, DEPTH_MATMUL)
    DH = _t(D, ROWS_MATMUL)
    row_k = lambda i, n, k: (i, k)
    row_n = lambda i, n, k: (i, n)
    f32 = lambda *shape: jax.ShapeDtypeStruct(shape, F32)
    hn, proj, win = _proj_ag(xs, norm_gain, shard16["w_in"], block_order, tm)
    y, ge, bsr, bsi, wp, wglu, wout, wple, wpg = _ssm_fwd(proj, bdr, bdi, cdr, cdi, abr, abi, d_skip, P, tbs,
                                                          comm=ag_later)
    pooled, mixed = _pool_fwd(proj, wp, P, tb)
    hg = _mm(ge, wglu, dims=NN, grid=(T // tm, 1, 1),
             a_spec=_bs((tm, P), row_k), b_spec=_bs((P, 2 * P), lambda i, n, k: (k, n)),
             o_spec=_bs((tm, 2 * P), row_n), out_shape=f32(T, 2 * P), name="mm_glu")
    cat = _gate_fwd(mixed, proj, hg, pool_scale, tb)
    h1, h1b = _mm(cat, wout, dims=NN, grid=(T // tm, D // DH, 1), res=xs, bf16_copy=True,
                  a_spec=_bs((tm, D), row_k), b_spec=_bs((D, DH), lambda i, n, k: (k, n)),
                  r_spec=_bs((tm, DH), row_n), o_spec=_bs((tm, DH), row_n), out_shape=f32(T, D), name="mm_out")
    z = _mm(h1b, wpg, dims=NN, grid=(T // tm, 1, 1),
            a_spec=_bs((tm, D), row_k), b_spec=_bs((D, D), lambda i, n, k: (k, n)),
            o_spec=_bs((tm, D), row_n), out_shape=f32(T, D), name="mm_pgate")
    dh2, dz, dg2, lpart, g_wple, g_wple16 = _final_fb(h1, pe, wple, z, tgt, final_gain.reshape(1, D), tb)

    col_m = lambda m, n, k: (k, m)
    col_n = lambda m, n, k: (k, n)
    dh1, dh1b = _mm(dz, wpg, dims=NT, grid=(T // tm, D // DH, 1), res=dh2, bf16_copy=True,
                    a_spec=_bs((tm, D), row_k), b_spec=_bs((DH, D), lambda i, n, k: (n, k)),
                    r_spec=_bs((tm, DH), row_n), o_spec=_bs((tm, DH), row_n), out_shape=f32(T, D), name="mm_dh1")
    g_wpg, g_wpg16 = _mm(h1b, dz, dims=TN, grid=(D // DH, D // DH, T // tk), bf16_copy=True,
                         a_spec=_bs((tk, DH), col_m), b_spec=_bs((tk, DH), col_n),
                         o_spec=_bs((DH, DH), lambda m, n, k: (m, n)), out_shape=f32(D, D), name="mm_gwpg")
    dcat = _mm(dh1b, wout, dims=NT, grid=(T // tm, 1, 1),
               a_spec=_bs((tm, D), row_k), b_spec=_bs((D, D), lambda i, n, k: (n, k)),
               o_spec=_bs((tm, D), row_n), out_shape=f32(T, D), name="mm_dcat")
    g_wout, g_wout16 = _mm(cat, dh1b, dims=TN, grid=(D // DH, D // DH, T // tk), bf16_copy=True,
                           a_spec=_bs((tk, DH), col_m), b_spec=_bs((tk, DH), col_n),
                           o_spec=_bs((DH, DH), lambda m, n, k: (m, n)), out_shape=f32(D, D), name="mm_gwout")
    gbig = {"w_out": g_wout.reshape(N_CHIP, Q, D), "w_ple": g_wple, "w_ple_gate": g_wpg.reshape(N_CHIP, Q, D)}
    gbig16 = {"w_out": g_wout16.reshape(N_CHIP, Q, D), "w_ple": g_wple16,
              "w_ple_gate": g_wpg16.reshape(N_CHIP, Q, D)}
    first = list(gbig)
    res = _gate_bwd(dcat, mixed, proj, hg, pool_scale, tb, comm=_halves_plan([gbig16[n] for n in first]))
    dmixed, dpg, dsg, dhg, dps = res[:5]
    got = dict(zip(first, res[5:]))
    dge = _mm(dhg, wglu, dims=NT, grid=(T // tm, 1, 1),
              a_spec=_bs((tm, 2 * P), row_k), b_spec=_bs((P, 2 * P), lambda i, n, k: (n, k)),
              o_spec=_bs((tm, P), row_n), out_shape=f32(T, P), name="mm_dge")
    gbig["w_glu"], gbig16["w_glu"] = _mm(ge, dhg, dims=TN, grid=(1, N_CHIP, T // tk), bf16_copy=True,
                                         a_spec=_bs((tk, P), col_m), b_spec=_bs((tk, Q), col_n),
                                         o_spec=_bs((None, P, Q), lambda m, j, k: (j, 0, 0)),
                                         out_shape=f32(N_CHIP, P, Q), name="mm_gwglu")
    g_wp = _mm(pooled, dmixed, dims=TN, grid=(NG, 1, T // tk), bf16_copy=True,
               a_spec=_bs((tk, PG), col_m), b_spec=_bs((tk, PG), col_m),
               o_spec=_bs((None, PG, PG), lambda g, n, k: (g, 0, 0)), out_shape=f32(NG, PG, PG), name="mm_gwp")
    by_chip = lambda a: a.reshape(NG, N_CHIP, PG // N_CHIP, PG).transpose(1, 0, 2, 3).reshape(
        N_CHIP, NG * PG // N_CHIP, PG)
    gbig["w_pool"], gbig16["w_pool"] = by_chip(g_wp[0]), by_chip(g_wp[1])
    res = _pool_bwd(dmixed, wp, tb, comm=_halves_plan([gbig16["w_pool"], gbig16["w_glu"]]))
    dpi, got["w_pool"], got["w_glu"] = res
    early = list(gbig)
    chip_sums = {n: _sum_cast(gbig[n], got[n], place, "sum_cast_" + n) for n in early}
    res = _ssm_bwd(proj, y, dge, bsr, bsi, bdr, bdi, cdr, cdi, abr, abi, d_skip, dpi, dpg, dsg, P, tbs,
                   comm=_scatter_plan([chip_sums[n] for n in early]))
    dproj, dabr, dabi, dd, dbdr, dbdi, dcdr, dcdi = res[:8]
    arrived = dict(zip(early, res[8:]))
    halves = [_sum_chips(chip_sums[n], arrived[n], place, "sum_chips_" + n) for n in early]
    res = _mm(hn, dproj, dims=TN, grid=(D // DH, N_CHIP, T // tk), bf16_copy=True,
              a_spec=_bs((tk, DH), col_m), b_spec=_bs((tk, P), col_n),
              o_spec=_bs((None, DH, P), lambda m, j, k: (j, m, 0)), out_shape=f32(N_CHIP, D, P),
              name="mm_gwin", comm=_join_plan(halves))
    gbig["w_in"], gbig16["w_in"], gshard = res[0], res[1], dict(zip(early, res[2:]))
    got["w_in"], = _comm_call(_halves_plan([gbig16["w_in"]]), "rs_halves_late")
    chip_sums["w_in"] = _sum_cast(gbig["w_in"], got["w_in"], place, "sum_cast_w_in")
    KH = _t(4 * P, DEPTH_MATMUL)
    dhn, arrived["w_in"] = _mm(dproj, win, dims=NT, grid=(T // tm, D // DH, 4 * P // KH),
                               a_spec=_bs((tm, KH), row_k), b_spec=_bs((DH, KH), lambda i, n, k: (n, k)),
                               o_spec=_bs((tm, DH), row_n), out_shape=f32(T, D), name="mm_dhn",
                               comm=_scatter_plan([chip_sums["w_in"]]))
    gshard["w_in"], = _comm_call(
        _join_plan([_sum_chips(chip_sums["w_in"], arrived["w_in"], place, "sum_chips_w_in")]), "rs_join_w_in")
    grad_x, dg1 = _norm1_bwd(xs, dhn, dh1, norm_gain, tb)

    dbbt_re = _block_diag_extract(dbdr, GT, C, N).reshape(G * C, N)
    dbbt_im = _block_diag_extract(dbdi, GT, C, N).reshape(G * C, N)
    g_c_re = _block_diag_extract(dcdr, GT, N, C).transpose(0, 2, 1)
    g_c_im = _block_diag_extract(dcdi, GT, N, C).transpose(0, 2, 1)
    dab_re_r = rep(dabr.reshape(G, N)) * (1.0 / C)
    dab_im_r = rep(dabi.reshape(G, N)) * (1.0 / C)
    g_a_re, g_a_im, g_ldt, g_bt_re, g_bt_im = _ssm_prep_bwd(a_re_r, a_im_r, ldt_r, bt_re, bt_im,
                                                            dab_re_r, dab_im_r, dbbt_re, dbbt_im, G)
    g_b_re = g_bt_re.reshape(G, C, N).transpose(0, 2, 1)
    g_b_im = g_bt_im.reshape(G, C, N).transpose(0, 2, 1)


    small_names = ["norm_gain", "pool_scale", "a_re", "a_im", "log_dt", "b_re", "b_im", "c_re", "c_im",
                   "d_skip", "final_gain"]
    small_w = dict(norm_gain=norm_gain, pool_scale=pool_scale, a_re=a_re, a_im=a_im, log_dt=log_dt, b_re=b_re,
                   b_im=b_im, c_re=c_re, c_im=c_im, d_skip=d_skip, final_gain=final_gain)
    small_m = dict(norm_gain=m_norm_gain, pool_scale=m_pool_scale, a_re=m_a_re, a_im=m_a_im, log_dt=m_log_dt,
                   b_re=m_b_re, b_im=m_b_im, c_re=m_c_re, c_im=m_c_im, d_skip=m_d_skip, final_gain=m_final_gain)
    small_v = dict(norm_gain=v_norm_gain, pool_scale=v_pool_scale, a_re=v_a_re, a_im=v_a_im, log_dt=v_log_dt,
                   b_re=v_b_re, b_im=v_b_im, c_re=v_c_re, c_im=v_c_im, d_skip=v_d_skip, final_gain=v_final_gain)
    small_g = dict(norm_gain=dg1, pool_scale=dps, a_re=g_a_re, a_im=g_a_im, log_dt=g_ldt, b_re=g_b_re,
                   b_im=g_b_im, c_re=g_c_re, c_im=g_c_im, d_skip=dd, final_gain=dg2)
    shapes = [small_w[n].shape for n in small_names]
    loss_row = sum(_tile_rows(small_w[n].size) for n in small_names)
    unit = N_DEV * SUBLANES
    rows = -(-(loss_row + SUBLANES) // unit) * unit
    gbuf = _pack_small([small_g[n] for n in small_names] + [lpart[0, :1]], rows)
    gsum, = _comm_call(_allreduce_plan(gbuf), "allreduce_small")
    g_out, d_out, m_out, v_out = {}, {}, {}, {}
    for n in big_names:
        w_, m_, v_ = big[n]
        r2 = shard2d[n]
        res = _adamw(w_.reshape(r2), gshard[n], m_.reshape(r2), v_.reshape(r2), "adamw_" + n)
        g_out[n], d_out[n], m_out[n], v_out[n] = (a.reshape(w_.shape) for a in (gshard[n], *res))
    wbuf = _pack_small([small_w[n] for n in small_names], rows)
    mbuf = _pack_small([small_m[n] for n in small_names], rows)
    vbuf = _pack_small([small_v[n] for n in small_names], rows)
    dsm, msm, vsm = _adamw(wbuf, gsum, mbuf, vbuf, "adamw_small")
    g_small = dict(zip(small_names, _unpack_small(gsum, shapes)))
    d_small = dict(zip(small_names, _unpack_small(dsm, shapes)))
    m_small = dict(zip(small_names, _unpack_small(msm, shapes)))
    v_small = dict(zip(small_names, _unpack_small(vsm, shapes)))
    loss = gsum[loss_row, 0]

    g_out.update(g_small)
    d_out.update(d_small)
    m_out.update(m_small)
    v_out.update(v_small)

    order = ["norm_gain", "w_in", "w_pool", "pool_scale", "a_re", "a_im", "log_dt", "b_re", "b_im", "c_re",
             "c_im", "d_skip", "w_glu", "w_out", "w_ple", "w_ple_gate", "final_gain"]
    return (loss, grad_x[None], *[g_out[n] for n in order], *[d_out[n] for n in order],
            *[m_out[n] for n in order], *[v_out[n] for n in order])
```

```python
import functools

import jax
import jax.numpy as jnp
from jax import lax
from jax.experimental import pallas as pl
from jax.experimental.pallas import tpu as pltpu

F32, BF16 = jnp.float32, jnp.bfloat16
MESH = pl.DeviceIdType.MESH
ANY = pl.BlockSpec(memory_space=pl.ANY)
VMEM_FULL = pl.BlockSpec(memory_space=pltpu.VMEM)

EPS = 1e-6
A_RE_MAX = -1e-4
SSM_GROUP = 16
SSM_STATE = 64
POOL_WINDOWS = (2, 4, 8, 16)
POOL_HALO = 16
ADAM_LR, ADAM_B1, ADAM_B2, ADAM_EPS, ADAM_WD, ADAM_STEP = 0.001, 0.9, 0.999, 1e-08, 0.01, 10

V7X_VMEM_BYTES = 64 * 1024 * 1024
VMEM_LIMIT = V7X_VMEM_BYTES - 8 * 1024 * 1024
SUBLANES, LANES = 8, 128
BF16_TILE_ROWS = 16
SSM_TILE_GROUPS = 8
SCAN_LANES = 1024
N_DEV, N_CHIP = 8, 4
DMA_CHUNK_BYTES = 256 * 1024
DMA_MAX_CHUNKS = 32
RS_CHUNKS = 8
ROWS_ELEMENTWISE = 256
ROWS_SSM = 256
ROWS_MATMUL = 1024
DEPTH_MATMUL = 4096
AG_FORWARD_SCALE, AG_FORWARD_LAG = 0.85, 0.05


def _t(n, pref):
    return pref if n % pref == 0 else n


def _cp(sem=None, vmem=VMEM_LIMIT):
    return pltpu.CompilerParams(dimension_semantics=sem, vmem_limit_bytes=vmem)


def _call(body, **kw):
    return pl.pallas_call(body, **kw)


NN = ((1,), (0,))
NT = ((1,), (1,))
TN = ((0,), (0,))


def _mm(a, b, *, dims, grid, a_spec, b_spec, o_spec, out_shape, name, res=None, r_spec=None, bf16_copy=False,
        comm=None):
    nk, kax = grid[-1], len(grid) - 1
    acc_shape = tuple(d for d in o_spec.block_shape if d is not None)

    def core(*refs):
        refs = list(refs)
        a_ref, b_ref = refs[:2]
        r_ref = refs[2] if res is not None else None
        outs = refs[3 if res is not None else 2:]
        o_ref = outs[0]
        o2_ref = outs[1] if bf16_copy else None
        acc = outs[-1] if nk > 1 else None

        def finish(r):
            if r_ref is not None:
                r = r + r_ref[...]
            o_ref[...] = r.astype(o_ref.dtype)
            if o2_ref is not None:
                o2_ref[...] = r.astype(BF16)

        part = lax.dot_general(a_ref[...].astype(BF16), b_ref[...].astype(BF16),
                               (dims, ((), ())), preferred_element_type=F32)
        if nk == 1:
            finish(part)
        else:
            k = pl.program_id(kax)

            @pl.when(k == 0)
            def _():
                acc[...] = part

            @pl.when(k > 0)
            def _():
                acc[...] += part

            @pl.when(k == nk - 1)
            def _():
                finish(acc[...])

    ins, specs = [a, b], [a_spec, b_spec]
    if res is not None:
        ins.append(res)
        specs.append(r_spec)
    o_specs, o_shapes = [o_spec], [out_shape]
    if bf16_copy:
        o_specs = [o_spec, o_spec]
        o_shapes = [out_shape, jax.ShapeDtypeStruct(out_shape.shape, BF16)]
    scratch = [pltpu.VMEM(acc_shape, F32)] if nk > 1 else []
    body, extra = _hosted(core, comm, grid, len(ins), len(o_specs), len(scratch))
    sem = ("arbitrary",) * len(grid) if comm else ("parallel",) * kax + ("arbitrary",)
    outs = _call(body, grid=grid, in_specs=specs + extra["in_specs"], out_specs=o_specs + extra["out_specs"],
                 out_shape=o_shapes + extra["out_shape"], scratch_shapes=scratch + extra["scratch"],
                 input_output_aliases=extra["aliases"],
                 compiler_params=_cp(sem), name=name)(*ins, *extra["ins"])
    return outs[0] if len(outs) == 1 else outs


def _bs(shape, fn):
    return pl.BlockSpec(shape, fn)


def _sigmoid(v):
    return 1.0 / (1.0 + jnp.exp(-v))


def _gelu(v):
    return 0.5 * v * (1.0 + jnp.tanh(0.7978845608028654 * (v + 0.044715 * v * v * v)))


def _gelu_grad(v):
    t = jnp.tanh(0.7978845608028654 * (v + 0.044715 * v * v * v))
    return 0.5 * (1.0 + t) + 0.5 * v * (1.0 - t * t) * 0.7978845608028654 * (1.0 + 3 * 0.044715 * v * v)


def _norm1_bwd(x, dhn, dh1, g1, tb, comm=None):
    T, D = x.shape

    def core(x_ref, dhn_ref, dh1_ref, g_ref, dx_ref, dg_ref):
        @pl.when(pl.program_id(0) == 0)
        def _():
            dg_ref[...] = jnp.zeros_like(dg_ref)

        xv = x_ref[...]
        r = lax.rsqrt(jnp.mean(xv * xv, axis=-1, keepdims=True) + EPS)
        xh = xv * r
        dhn_v = dhn_ref[...]
        dg_ref[...] += jnp.sum(dhn_v * xh, axis=0, keepdims=True)
        dxh = dhn_v * g_ref[...]
        dx_ref[...] = dh1_ref[...] + r * (dxh - xh * jnp.mean(dxh * xh, axis=-1, keepdims=True))

    row = _bs((tb, D), lambda i: (i, 0))
    vec = _bs((1, D), lambda i: (0, 0))
    body, extra = _hosted(core, comm, (T // tb,), 4, 2, 0)
    return _call(body, grid=(T // tb,), in_specs=[row, row, row, vec] + extra["in_specs"],
                 out_specs=[row, vec] + extra["out_specs"],
                 out_shape=[jax.ShapeDtypeStruct((T, D), F32), jax.ShapeDtypeStruct((1, D), F32)] + extra["out_shape"],
                 scratch_shapes=extra["scratch"], input_output_aliases=extra["aliases"],
                 compiler_params=_cp(("arbitrary",)), name="norm1_bwd")(x, dhn, dh1, g1, *extra["ins"])


def _gate_fwd(mixed, proj, hg, ps, tb):
    T, P = mixed.shape

    def body(mx_ref, pg_ref, sg_ref, hg_ref, ps_ref, o_ref):
        pg, sg = pg_ref[...], sg_ref[...]
        ya = (mx_ref[...] * ps_ref[...]) * (pg * _sigmoid(pg))
        hgv = hg_ref[...]
        o = hgv[:, :P] * _sigmoid(hgv[:, P:])
        yb = o * (sg * _sigmoid(sg))
        o_ref[:, :P] = ya.astype(BF16)
        o_ref[:, P:] = yb.astype(BF16)

    return _call(body, grid=(T // tb,),
                 in_specs=[_bs((tb, P), lambda i: (i, 0)), _bs((tb, P), lambda i: (i, 1)),
                           _bs((tb, P), lambda i: (i, 3)), _bs((tb, 2 * P), lambda i: (i, 0)),
                           _bs((1, P), lambda i: (0, 0))],
                 out_specs=_bs((tb, 2 * P), lambda i: (i, 0)),
                 out_shape=jax.ShapeDtypeStruct((T, 2 * P), BF16),
                 compiler_params=_cp(("parallel",)), name="gate_fwd")(mixed, proj, proj, hg, ps)


def _gate_bwd(dcat, mixed, proj, hg, ps, tb, comm=None):
    T, P = mixed.shape

    def core(dc_ref, mx_ref, pg_ref, sg_ref, hg_ref, ps_ref, dmx_ref, dpg_ref, dsg_ref, dhg_ref, dps_ref):
        @pl.when(pl.program_id(0) == 0)
        def _():
            dps_ref[...] = jnp.zeros_like(dps_ref)

        dc = dc_ref[...]
        dya, dyb = dc[:, :P], dc[:, P:]
        pg, sg, mx, psv = pg_ref[...], sg_ref[...], mx_ref[...], ps_ref[...]
        s_pg = _sigmoid(pg)
        dpa = dya * (pg * s_pg)
        dpg_ref[...] = (dya * (mx * psv) * (s_pg * (1.0 + pg * (1.0 - s_pg)))).astype(BF16)
        dps_ref[...] += jnp.sum(dpa * mx, axis=0, keepdims=True)
        dmx_ref[...] = (dpa * psv).astype(BF16)
        hgv = hg_ref[...]
        h1, s_h2 = hgv[:, :P], _sigmoid(hgv[:, P:])
        s_sg = _sigmoid(sg)
        do = dyb * (sg * s_sg)
        dsg_ref[...] = (dyb * (h1 * s_h2) * (s_sg * (1.0 + sg * (1.0 - s_sg)))).astype(BF16)
        dhg_ref[:, :P] = (do * s_h2).astype(BF16)
        dhg_ref[:, P:] = (do * h1 * s_h2 * (1.0 - s_h2)).astype(BF16)

    rowp = _bs((tb, P), lambda i: (i, 0))
    row2 = _bs((tb, 2 * P), lambda i: (i, 0))
    vec = _bs((1, P), lambda i: (0, 0))
    body, extra = _hosted(core, comm, (T // tb,), 6, 5, 0)
    return _call(body, grid=(T // tb,),
                 in_specs=[row2, rowp, _bs((tb, P), lambda i: (i, 1)), _bs((tb, P), lambda i: (i, 3)), row2, vec]
                 + extra["in_specs"],
                 out_specs=[rowp, rowp, rowp, row2, vec] + extra["out_specs"],
                 out_shape=[jax.ShapeDtypeStruct((T, P), BF16), jax.ShapeDtypeStruct((T, P), BF16),
                            jax.ShapeDtypeStruct((T, P), BF16), jax.ShapeDtypeStruct((T, 2 * P), BF16),
                            jax.ShapeDtypeStruct((1, P), F32)] + extra["out_shape"],
                 scratch_shapes=extra["scratch"],
                 compiler_params=_cp(("arbitrary",)), name="gate_bwd")(dcat, mixed, proj, proj, hg, ps, *extra["ins"])


def _final_fb(h1, pe, wple, z, tgt, g2, tb):
    T, D = h1.shape
    E = pe.shape[1]
    Q = D // N_CHIP
    nb = T // tb

    def body(h1_ref, p_ref, w_ref, z_ref, t_ref, g_ref, dh2_ref, dz_ref, dg_ref, l_ref, gw_ref, gw16_ref, acc):
        @pl.when(pl.program_id(0) == 0)
        def _():
            dg_ref[...] = jnp.zeros_like(dg_ref)
            l_ref[...] = jnp.zeros_like(l_ref)
            acc[...] = jnp.zeros_like(acc)

        pb = p_ref[...].astype(BF16)
        ev = jnp.dot(pb, w_ref[...], preferred_element_type=F32)
        s = _sigmoid(z_ref[...])
        h2 = h1_ref[...] + ev * s
        r = lax.rsqrt(jnp.mean(h2 * h2, axis=-1, keepdims=True) + EPS)
        xh = h2 * r
        gv = g_ref[...]
        diff = xh * gv - t_ref[...]
        l_ref[...] += 0.5 * jnp.sum(jnp.mean(diff * diff, axis=-1, keepdims=True))
        dout = diff * (1.0 / D)
        dg_ref[...] += jnp.sum(dout * xh, axis=0, keepdims=True)
        dxh = dout * gv
        dh2 = r * (dxh - xh * jnp.mean(dxh * xh, axis=-1, keepdims=True))
        dh2_ref[...] = dh2
        dz_ref[...] = (dh2 * ev * s * (1.0 - s)).astype(BF16)
        acc[...] += lax.dot_general(pb, (dh2 * s).astype(BF16), (TN, ((), ())), preferred_element_type=F32)

        @pl.when(pl.program_id(0) == nb - 1)
        def _():
            for j in range(N_CHIP):
                slab = acc[:, j * Q:(j + 1) * Q]
                gw_ref[j] = slab
                gw16_ref[j] = slab.astype(BF16)

    row = _bs((tb, D), lambda i: (i, 0))
    vec = _bs((1, D), lambda i: (0, 0))
    slabs = _bs((N_CHIP, E, Q), lambda i: (0, 0, 0))
    return _call(body, grid=(nb,),
                 in_specs=[row, _bs((tb, E), lambda i: (i, 0)), _bs((E, D), lambda i: (0, 0)), row, row, vec],
                 out_specs=[row, row, vec, _bs((1, LANES), lambda i: (0, 0)), slabs, slabs],
                 out_shape=[jax.ShapeDtypeStruct((T, D), F32), jax.ShapeDtypeStruct((T, D), BF16),
                            jax.ShapeDtypeStruct((1, D), F32), jax.ShapeDtypeStruct((1, LANES), F32),
                            jax.ShapeDtypeStruct((N_CHIP, E, Q), F32), jax.ShapeDtypeStruct((N_CHIP, E, Q), BF16)],
                 scratch_shapes=[pltpu.VMEM((E, D), F32)],
                 compiler_params=_cp(("arbitrary",)), name="final_fb")(h1, pe, wple, z, tgt, g2)


def _pool_inv_count(t0, rows, pg, ngroups):
    t = t0 + lax.broadcasted_iota(jnp.int32, (rows, pg), 0)
    parts = []
    for w in POOL_WINDOWS[:ngroups]:
        parts.append(jnp.where(t + 1 >= w, 1.0 / w, 1.0 / (t + 1).astype(F32)))
    return parts


def _pool_fwd(proj, wp, P, tb):
    T = proj.shape[0]
    ng = len(POOL_WINDOWS)
    pg = P // ng
    hb = tb // POOL_HALO

    def body(v_ref, tail_ref, w_ref, o_ref, mx_ref, ext):
        i = pl.program_id(0)
        ext[pl.ds(0, POOL_HALO), :] = jnp.where(i > 0, tail_ref[...], 0.0)
        ext[pl.ds(POOL_HALO, tb), :] = v_ref[...]
        inv = _pool_inv_count(i * tb, tb, pg, ng)
        for g, w in enumerate(POOL_WINDOWS):
            cols = pl.ds(g * pg, pg)
            win = ext[pl.ds(POOL_HALO, tb), cols]
            for k in range(1, w):
                win = win + ext[pl.ds(POOL_HALO - k, tb), cols]
            pooled = (win * inv[g] - ext[pl.ds(POOL_HALO, tb), cols]).astype(BF16)
            o_ref[:, cols] = pooled
            mx_ref[:, cols] = jnp.dot(pooled, w_ref[g], preferred_element_type=F32)

    row = _bs((tb, P), lambda i: (i, 0))
    return _call(body, grid=(T // tb,),
                 in_specs=[row, _bs((POOL_HALO, P), lambda i: (jnp.maximum(i * hb - 1, 0), 0)),
                           _bs(wp.shape, lambda i: (0, 0, 0))],
                 out_specs=[row, row],
                 out_shape=[jax.ShapeDtypeStruct((T, P), BF16), jax.ShapeDtypeStruct((T, P), F32)],
                 scratch_shapes=[pltpu.VMEM((tb + POOL_HALO, P), F32)],
                 compiler_params=_cp(("arbitrary",)), name="pool_fwd")(proj, proj, wp)


def _pool_bwd(dmixed, wp, tb, comm=None):
    T, P = dmixed.shape
    ng = len(POOL_WINDOWS)
    pg = P // ng
    hb = tb // POOL_HALO
    nb = T // tb

    def core(d_ref, head_ref, w_ref, o_ref, ext, dpl):
        i = pl.program_id(0)
        inv = _pool_inv_count(i * tb, tb, pg, ng)
        invh = _pool_inv_count((i + 1) * tb, POOL_HALO, pg, ng)
        for g in range(ng):
            cols = pl.ds(g * pg, pg)
            dp = lax.dot_general(d_ref[:, cols], w_ref[g], (NT, ((), ())), preferred_element_type=F32)
            dph = lax.dot_general(head_ref[:, cols], w_ref[g], (NT, ((), ())), preferred_element_type=F32)
            dpl[:, cols] = dp
            ext[pl.ds(0, tb), cols] = dp * inv[g]
            ext[pl.ds(tb, POOL_HALO), cols] = jnp.where(i < nb - 1, dph * invh[g], 0.0)
        for g, w in enumerate(POOL_WINDOWS):
            cols = pl.ds(g * pg, pg)
            acc = ext[pl.ds(0, tb), cols]
            for k in range(1, w):
                acc = acc + ext[pl.ds(k, tb), cols]
            o_ref[:, cols] = (acc - dpl[:, cols]).astype(BF16)

    body, extra = _hosted(core, comm, (nb,), 3, 1, 2)
    return _call(body, grid=(nb,),
                 in_specs=[_bs((tb, P), lambda i: (i, 0)),
                           _bs((POOL_HALO, P), lambda i: (jnp.minimum((i + 1) * hb, T // POOL_HALO - 1), 0)),
                           _bs(wp.shape, lambda i: (0, 0, 0))] + extra["in_specs"],
                 out_specs=[_bs((tb, P), lambda i: (i, 0))] + extra["out_specs"],
                 out_shape=[jax.ShapeDtypeStruct((T, P), BF16)] + extra["out_shape"],
                 scratch_shapes=[pltpu.VMEM((tb + POOL_HALO, P), F32), pltpu.VMEM((tb, P), F32)] + extra["scratch"],
                 compiler_params=_cp(("arbitrary",)), name="pool_bwd")(dmixed, dmixed, wp, *extra["ins"])


def _zoh(a_re, a_im, ldt, b_re, b_im):
    lam_re = jnp.minimum(a_re, A_RE_MAX)
    lam_im = a_im
    dt = jnp.exp(ldt)
    mag = jnp.exp(lam_re * dt)
    ang = lam_im * dt
    ab_re = mag * jnp.cos(ang)
    ab_im = mag * jnp.sin(ang)
    den = lam_re * lam_re + lam_im * lam_im
    n_re = ab_re - 1.0
    n_im = ab_im
    q_re = (n_re * lam_re + n_im * lam_im) / den
    q_im = (n_im * lam_re - n_re * lam_im) / den
    return ab_re, ab_im, q_re * b_re - q_im * b_im, q_re * b_im + q_im * b_re


def _ssm_prep(a_re, a_im, ldt, bt_re, bt_im):
    shp = jax.ShapeDtypeStruct(a_re.shape, F32)

    def body(a, b, c, d, e, o0, o1, o2, o3):
        r = _zoh(a[...], b[...], c[...], d[...], e[...])
        o0[...], o1[...], o2[...], o3[...] = r

    return _call(body, in_specs=[VMEM_FULL] * 5, out_specs=[VMEM_FULL] * 4, out_shape=[shp] * 4,
                 name="ssm_prep")(a_re, a_im, ldt, bt_re, bt_im)


def _ssm_prep_bwd(a_re, a_im, ldt, bt_re, bt_im, dab_re, dab_im, dbb_re, dbb_im, G):
    GC, N = a_re.shape
    C = GC // G

    def body(a, b, c, d, e, g0, g1, g2, g3, da_re, da_im, dldt, db_re, db_im):
        _, vjp = jax.vjp(_zoh, a[...], b[...], c[...], d[...], e[...])
        ga_re, ga_im, gl, gb_re, gb_im = vjp((g0[...], g1[...], g2[...], g3[...]))
        da_re[...] = jnp.sum(ga_re.reshape(G, C, N), axis=1)
        da_im[...] = jnp.sum(ga_im.reshape(G, C, N), axis=1)
        dldt[...] = jnp.sum(jnp.sum(gl.reshape(G, C, N), axis=1), axis=1, keepdims=True)
        db_re[...] = gb_re
        db_im[...] = gb_im

    gn = jax.ShapeDtypeStruct((G, N), F32)
    full = jax.ShapeDtypeStruct((GC, N), F32)
    return _call(body, in_specs=[VMEM_FULL] * 9, out_specs=[VMEM_FULL] * 5,
                 out_shape=[gn, gn, jax.ShapeDtypeStruct((G, 1), F32), full, full],
                 name="ssm_prep_bwd")(a_re, a_im, ldt, bt_re, bt_im, dab_re, dab_im, dbb_re, dbb_im)


def _coef_tiles(abr, abi, reverse):
    ns = abr.shape[1]
    row = lax.broadcasted_iota(jnp.int32, (SUBLANES, ns), 0)
    ar = jnp.broadcast_to(abr, (SUBLANES, ns))
    ai = jnp.broadcast_to(-abi if reverse else abi, (SUBLANES, ns))
    a2r, a2i = ar * ar - ai * ai, 2.0 * ar * ai
    a4r, a4i = a2r * a2r - a2i * a2i, 2.0 * a2r * a2i
    out = []
    for d, (vr, vi) in ((1, (ar, ai)), (2, (a2r, a2i)), (4, (a4r, a4i))):
        keep = (row < SUBLANES - d) if reverse else (row >= d)
        out += [jnp.where(keep, vr, 0.0), jnp.where(keep, vi, 0.0)]
    pr, pi = ar, ai
    for k in range(1, SUBLANES):
        sel = (row <= SUBLANES - 1 - k) if reverse else (row >= k)
        nr, ni = pr * ar - pi * ai, pr * ai + pi * ar
        pr, pi = jnp.where(sel, nr, pr), jnp.where(sel, ni, pi)
    return out + [pr, pi]


def _cpow(ar, ai, n):
    out, br, bi = None, ar, ai
    while n:
        if n & 1:
            out = (br, bi) if out is None else (out[0] * br - out[1] * bi, out[0] * bi + out[1] * br)
        br, bi = br * br - bi * bi, 2.0 * br * bi
        n >>= 1
    return out


def _seg_perm_matrix(nrows):
    r = jnp.arange(nrows)
    src = (nrows // SUBLANES) * (r % SUBLANES) + r // SUBLANES
    return (src[:, None] == jnp.arange(nrows)[None, :]).astype(BF16)


def _seg_order_rows(pm, xb):
    return jnp.dot(pm, xb, preferred_element_type=F32).astype(BF16)


def _time_order_rows(pm, x, terms):
    out, rest = None, x
    for t in range(terms):
        piece = rest.astype(BF16)
        part = lax.dot_general(pm, piece, (TN, ((), ())), preferred_element_type=F32)
        out = part if out is None else out + part
        if t + 1 < terms:
            rest = rest - piece.astype(F32)
    return out


def _scan_tiles(abr, abi, seg, reverse):
    ns = abr.shape[1]
    seg_pow = _cpow(abr, abi, seg)
    step = [jnp.broadcast_to(abr, (SUBLANES, ns)), jnp.broadcast_to(-abi if reverse else abi, (SUBLANES, ns))]
    return _coef_tiles(seg_pow[0], seg_pow[1], reverse) + step


def _seg_scan(xr_ref, xi_ref, coef_ref, car_ref, cai_ref, *, nrows, ns, reverse, cmat=None, dab=None):
    seg = nrows // SUBLANES
    cw = min(SCAN_LANES, ns)
    row = lax.broadcasted_iota(jnp.int32, (SUBLANES, cw), 0)
    first, last = (SUBLANES - 1, 0) if reverse else (0, SUBLANES - 1)

    def tile(i):
        return pl.ds(pl.multiple_of(((seg - 1 - i) if reverse else i) * SUBLANES, SUBLANES), SUBLANES)

    for cc in range(ns // cw):
        cols = pl.ds(cc * cw, cw)
        ar, ai = coef_ref[8, :, cols], coef_ref[9, :, cols]

        def local(i, x, cols=cols, ar=ar, ai=ai):
            rows = tile(i)
            nr = ar * x[0] - ai * x[1] + xr_ref[rows, cols]
            ni = ar * x[1] + ai * x[0] + xi_ref[rows, cols]
            xr_ref[rows, cols] = nr
            xi_ref[rows, cols] = ni
            return nr, ni

        zero = jnp.zeros((SUBLANES, cw), F32)
        er, ei = lax.fori_loop(0, seg, local, (zero, zero))

        co = [coef_ref[k, :, cols] for k in range(8)]
        for lvl, d in enumerate((1, 2, 4)):
            kr, ki = co[2 * lvl], co[2 * lvl + 1]
            sh = SUBLANES - d if reverse else d
            sr, si = pltpu.roll(er, sh, 0), pltpu.roll(ei, sh, 0)
            er, ei = er + (kr * sr - ki * si), ei + (kr * si + ki * sr)
        c0r, c0i = car_ref[:, cols], cai_ref[:, cols]
        er, ei = er + (co[6] * c0r - co[7] * c0i), ei + (co[6] * c0i + co[7] * c0r)
        nb_shift = SUBLANES - 1 if reverse else 1
        cmr = jnp.where(row == first, c0r, pltpu.roll(er, nb_shift, 0))
        cmi = jnp.where(row == first, c0i, pltpu.roll(ei, nb_shift, 0))
        car_ref[:, cols] = jnp.broadcast_to(er[last:last + 1, :], er.shape)
        cai_ref[:, cols] = jnp.broadcast_to(ei[last:last + 1, :], ei.shape)
        if cmat is not None:
            cmat[0][:, cols] = cmr
            cmat[1][:, cols] = cmi

        w0 = (ar * cmr - ai * cmi, ar * cmi + ai * cmr)
        if dab is None:
            def fix(i, w, cols=cols, ar=ar, ai=ai):
                rows = tile(i)
                xr_ref[rows, cols] = xr_ref[rows, cols] + w[0]
                xi_ref[rows, cols] = xi_ref[rows, cols] + w[1]
                return ar * w[0] - ai * w[1], ar * w[1] + ai * w[0]

            lax.fori_loop(0, seg, fix, w0)
        else:
            s_re, s_im, e_re, e_im, o_re, o_im = dab

            def add(rows, w, pr, pi, acc):
                gr = xr_ref[rows, cols] + w[0]
                gi = xi_ref[rows, cols] + w[1]
                xr_ref[rows, cols] = gr
                xi_ref[rows, cols] = gi
                return acc[0] + (gr * pr + gi * pi), acc[1] + (gi * pr - gr * pi)

            def fix(i, st, cols=cols, ar=ar, ai=ai):
                w, acc = st[:2], st[2:]
                rows = tile(i)
                before = pl.ds(pl.multiple_of((seg - 2 - i) * SUBLANES, SUBLANES), SUBLANES)
                acc = add(rows, w, s_re[before, cols], s_im[before, cols], acc)
                return (ar * w[0] - ai * w[1], ar * w[1] + ai * w[0]) + acc

            st = lax.fori_loop(0, seg - 1, fix, w0 + (zero, zero))
            acc = add(pl.ds(0, SUBLANES), st[:2], e_re[:, cols], e_im[:, cols], st[2:])
            o_re[:, cols] += jnp.sum(acc[0], axis=0, keepdims=True)
            o_im[:, cols] += jnp.sum(acc[1], axis=0, keepdims=True)


def _hosted(core, comm, grid, n_in, n_out, n_scratch):
    ci = len(comm["ins"]) if comm else 0
    co = len(comm["out_shape"]) if comm else 0

    def body(*refs):
        ins, rest = refs[:n_in + ci], refs[n_in + ci:]
        outs, scr = rest[:n_out + co], rest[n_out + co:]
        hooks = functools.partial(_comm_hooks, comm, grid, ins[n_in:], outs[n_out:], scr[n_scratch:])
        hooks(before=True)
        core(*ins[:n_in], *outs[:n_out], *scr[:n_scratch])
        hooks(before=False)

    aliases = {n_in + i: n_out + i for i in range(co)} if comm and comm.get("alias") else {}
    extra = dict(ins=list(comm["ins"]) if comm else [], in_specs=[ANY] * ci, out_specs=[ANY] * co,
                 out_shape=list(comm["out_shape"]) if comm else [], scratch=list(comm["scratch"]) if comm else [],
                 aliases=aliases)
    return body, extra


def _ssm_fwd(proj, bdr, bdi, cdr, cdi, abr, abi, dsk, P, tb, comm=None):
    T = proj.shape[0]
    ntl, ct, st = bdr.shape
    ns = ntl * st
    nb = T // tb

    def core(u_ref, bdr_ref, bdi_ref, cdr_ref, cdi_ref, abr_ref, abi_ref, d_ref, pm_ref,
             y_ref, ge_ref, bsr_ref, bsi_ref, sr, si, coef, car, cai, up):
        @pl.when(pl.program_id(0) == 0)
        def _():
            for k, tile in enumerate(_scan_tiles(abr_ref[...], abi_ref[...], tb // SUBLANES, False)):
                coef[k] = tile
            car[...] = jnp.zeros_like(car)
            cai[...] = jnp.zeros_like(cai)

        bsr_ref[...] = car[...]
        bsi_ref[...] = cai[...]
        u = u_ref[...]
        ub = _seg_order_rows(pm_ref[...], u.astype(BF16))
        for s in range(ntl):
            us = ub[:, s * ct:(s + 1) * ct]
            sr[:, s * st:(s + 1) * st] = jnp.dot(us, bdr_ref[s], preferred_element_type=F32)
            si[:, s * st:(s + 1) * st] = jnp.dot(us, bdi_ref[s], preferred_element_type=F32)
        _seg_scan(sr, si, coef, car, cai, nrows=tb, ns=ns, reverse=False)
        for s in range(ntl):
            s_re = sr[:, s * st:(s + 1) * st].astype(BF16)
            s_im = si[:, s * st:(s + 1) * st].astype(BF16)
            up[:, s * ct:(s + 1) * ct] = (jnp.dot(s_re, cdr_ref[s], preferred_element_type=F32)
                                          - jnp.dot(s_im, cdi_ref[s], preferred_element_type=F32))
        y = _time_order_rows(pm_ref[...], up[...], 3) + d_ref[...] * u
        y_ref[...] = y
        ge_ref[...] = _gelu(y).astype(BF16)

    full3 = lambda a: _bs(a.shape, lambda i: (0, 0, 0))
    vec = lambda n: _bs((1, n), lambda i: (0, 0))
    row = _bs((tb, P), lambda i: (i, 0))
    st_spec = _bs((None, SUBLANES, ns), lambda i: (i, 0, 0))
    body, extra = _hosted(core, comm, (nb,), 9, 4, 6)
    return _call(body, grid=(nb,),
                 in_specs=[_bs((tb, P), lambda i: (i, 2)), full3(bdr), full3(bdi), full3(cdr), full3(cdi),
                           vec(ns), vec(ns), vec(P), _bs((tb, tb), lambda i: (0, 0))] + extra["in_specs"],
                 out_specs=[row, row, st_spec, st_spec] + extra["out_specs"],
                 out_shape=[jax.ShapeDtypeStruct((T, P), F32), jax.ShapeDtypeStruct((T, P), BF16),
                            jax.ShapeDtypeStruct((nb, SUBLANES, ns), F32),
                            jax.ShapeDtypeStruct((nb, SUBLANES, ns), F32)] + extra["out_shape"],
                 scratch_shapes=[pltpu.VMEM((tb, ns), F32), pltpu.VMEM((tb, ns), F32),
                                 pltpu.VMEM((10, SUBLANES, ns), F32),
                                 pltpu.VMEM((SUBLANES, ns), F32), pltpu.VMEM((SUBLANES, ns), F32),
                                 pltpu.VMEM((tb, P), F32)] + extra["scratch"],
                 compiler_params=_cp(("arbitrary",)), name="ssm_fwd")(
                     proj, bdr, bdi, cdr, cdi, abr, abi, dsk, _seg_perm_matrix(tb), *extra["ins"])


def _ssm_bwd(proj, y, dge, bsr, bsi, bdr, bdi, cdr, cdi, abr, abi, dsk, dpi, dpg, dsg, P, tb, comm=None):
    T = proj.shape[0]
    ntl, ct, st = bdr.shape
    ns = ntl * st
    nb = T // tb

    def core(u_ref, y_ref, dge_ref, bsr_ref, bsi_ref, abr_ref, abi_ref, d_ref, pm_ref, dpi_ref, dpg_ref, dsg_ref,
             bdr_h, bdi_h, cdr_h, cdi_h,
             dproj_ref, dabr_ref, dabi_ref, dd_ref, dbdr_h, dbdi_h, dcdr_h, dcdi_h,
             wbdr, wbdi, wcdr, wcdi, abdr, abdi, acdr, acdi, spr, spi, gr, gi, coef_f, coef_r,
             car, cai, gcr, gci, ser, sei, dup):
        i = pl.program_id(0)

        @pl.when(i == 0)
        def _():
            for h, w in ((bdr_h, wbdr), (bdi_h, wbdi), (cdr_h, wcdr), (cdi_h, wcdi)):
                pltpu.sync_copy(h, w)
            for a in (abdr, abdi, acdr, acdi, gcr, gci):
                a[...] = jnp.zeros_like(a)
            for o in (dabr_ref, dabi_ref, dd_ref):
                o[...] = jnp.zeros_like(o)
            for k, tile in enumerate(_scan_tiles(abr_ref[...], abi_ref[...], tb // SUBLANES, False)):
                coef_f[k] = tile
            for k, tile in enumerate(_scan_tiles(abr_ref[...], abi_ref[...], tb // SUBLANES, True)):
                coef_r[k] = tile

        car[...] = bsr_ref[...]
        cai[...] = bsi_ref[...]
        u = u_ref[...]
        dy = dge_ref[...] * _gelu_grad(y_ref[...])
        ub = _seg_order_rows(pm_ref[...], u.astype(BF16))
        dyb = _seg_order_rows(pm_ref[...], dy.astype(BF16))
        for s in range(ntl):
            us = ub[:, s * ct:(s + 1) * ct]
            spr[:, s * st:(s + 1) * st] = jnp.dot(us, wbdr[s], preferred_element_type=F32)
            spi[:, s * st:(s + 1) * st] = jnp.dot(us, wbdi[s], preferred_element_type=F32)
        _seg_scan(spr, spi, coef_f, car, cai, nrows=tb, ns=ns, reverse=False, cmat=(ser, sei))

        for s in range(ntl):
            dys = dyb[:, s * ct:(s + 1) * ct]
            gr[:, s * st:(s + 1) * st] = lax.dot_general(dys, wcdr[s], (NT, ((), ())), preferred_element_type=F32)
            gi[:, s * st:(s + 1) * st] = -lax.dot_general(dys, wcdi[s], (NT, ((), ())), preferred_element_type=F32)
        _seg_scan(gr, gi, coef_r, gcr, gci, nrows=tb, ns=ns, reverse=True,
                  dab=(spr, spi, ser, sei, dabr_ref, dabi_ref))

        for s in range(ntl):
            sl_c, sl_s = slice(s * ct, (s + 1) * ct), slice(s * st, (s + 1) * st)
            s_re = spr[:, sl_s].astype(BF16)
            s_im = spi[:, sl_s].astype(BF16)
            g_re, g_im = gr[:, sl_s].astype(BF16), gi[:, sl_s].astype(BF16)
            dys, us = dyb[:, sl_c], ub[:, sl_c]
            acdr[s] += lax.dot_general(s_re, dys, (TN, ((), ())), preferred_element_type=F32)
            acdi[s] -= lax.dot_general(s_im, dys, (TN, ((), ())), preferred_element_type=F32)
            abdr[s] += lax.dot_general(us, g_re, (TN, ((), ())), preferred_element_type=F32)
            abdi[s] += lax.dot_general(us, g_im, (TN, ((), ())), preferred_element_type=F32)
            dup[:, sl_c] = (lax.dot_general(g_re, wbdr[s], (NT, ((), ())), preferred_element_type=F32)
                            + lax.dot_general(g_im, wbdi[s], (NT, ((), ())), preferred_element_type=F32))
        dd_ref[...] += jnp.sum(dy * u, axis=0, keepdims=True)
        du = _time_order_rows(pm_ref[...], dup[...], 2) + d_ref[...] * dy
        dproj_ref[:, 0:P] = dpi_ref[...]
        dproj_ref[:, P:2 * P] = dpg_ref[...]
        dproj_ref[:, 2 * P:3 * P] = du.astype(BF16)
        dproj_ref[:, 3 * P:4 * P] = dsg_ref[...]

        @pl.when(i == nb - 1)
        def _():
            for a, h in ((abdr, dbdr_h), (abdi, dbdi_h), (acdr, dcdr_h), (acdi, dcdi_h)):
                pltpu.sync_copy(a, h)

    rev = lambda i: nb - 1 - i
    vec = lambda n: _bs((1, n), lambda i: (0, 0))
    row = _bs((tb, P), lambda i: (rev(i), 0))
    st_spec = _bs((None, SUBLANES, ns), lambda i: (rev(i), 0, 0))
    bshape = jax.ShapeDtypeStruct(bdr.shape, F32)
    cshape = jax.ShapeDtypeStruct(cdr.shape, F32)
    body, extra = _hosted(core, comm, (nb,), 16, 8, 21)
    return _call(body, grid=(nb,),
                 in_specs=[_bs((tb, P), lambda i: (rev(i), 2)), row, row, st_spec, st_spec,
                           vec(ns), vec(ns), vec(P), _bs((tb, tb), lambda i: (0, 0)), row, row, row,
                           ANY, ANY, ANY, ANY] + extra["in_specs"],
                 out_specs=[_bs((tb, 4 * P), lambda i: (rev(i), 0)), vec(ns), vec(ns), vec(P), ANY, ANY, ANY, ANY]
                 + extra["out_specs"],
                 out_shape=[jax.ShapeDtypeStruct((T, 4 * P), BF16), jax.ShapeDtypeStruct((1, ns), F32),
                            jax.ShapeDtypeStruct((1, ns), F32), jax.ShapeDtypeStruct((1, P), F32),
                            bshape, bshape, cshape, cshape] + extra["out_shape"],
                 scratch_shapes=[pltpu.VMEM(bdr.shape, BF16), pltpu.VMEM(bdr.shape, BF16),
                                 pltpu.VMEM(cdr.shape, BF16), pltpu.VMEM(cdr.shape, BF16),
                                 pltpu.VMEM(bdr.shape, F32), pltpu.VMEM(bdr.shape, F32),
                                 pltpu.VMEM(cdr.shape, F32), pltpu.VMEM(cdr.shape, F32),
                                 pltpu.VMEM((tb, ns), F32), pltpu.VMEM((tb, ns), F32),
                                 pltpu.VMEM((tb, ns), F32), pltpu.VMEM((tb, ns), F32),
                                 pltpu.VMEM((10, SUBLANES, ns), F32), pltpu.VMEM((10, SUBLANES, ns), F32)]
                 + [pltpu.VMEM((SUBLANES, ns), F32)] * 6 + [pltpu.VMEM((tb, P), F32)] + extra["scratch"],
                 compiler_params=_cp(("arbitrary",)), name="ssm_bwd")(
                     proj, y, dge, bsr, bsi, abr, abi, dsk, _seg_perm_matrix(tb), dpi, dpg, dsg,
                     bdr, bdi, cdr, cdi, *extra["ins"])


def _adamw(w, g, m, v, name, comm=None):
    R, C = w.shape
    tr = _t(R, ROWS_ELEMENTWISE)

    def core(w_ref, g_ref, m_ref, v_ref, d_ref, mo_ref, vo_ref):
        gv = g_ref[...]
        mn = ADAM_B1 * m_ref[...] + (1.0 - ADAM_B1) * gv
        vn = ADAM_B2 * v_ref[...] + (1.0 - ADAM_B2) * (gv * gv)
        m_hat = mn / (1.0 - ADAM_B1 ** ADAM_STEP)
        v_hat = vn / (1.0 - ADAM_B2 ** ADAM_STEP)
        d_ref[...] = -ADAM_LR * (m_hat / (jnp.sqrt(v_hat) + ADAM_EPS) + ADAM_WD * w_ref[...])
        mo_ref[...] = mn
        vo_ref[...] = vn

    blk = _bs((tr, C), lambda i: (i, 0))
    shp = jax.ShapeDtypeStruct((R, C), F32)
    body, extra = _hosted(core, comm, (R // tr,), 4, 3, 0)
    return _call(body, grid=(R // tr,), in_specs=[blk] * 4 + extra["in_specs"],
                 out_specs=[blk] * 3 + extra["out_specs"], out_shape=[shp] * 3 + extra["out_shape"],
                 scratch_shapes=extra["scratch"],
                 compiler_params=_cp(("arbitrary",) if comm else ("parallel",)), name=name)(w, g, m, v, *extra["ins"])


def _sum_cast(grad, got, place, name):
    J, H, C = got.shape
    tr = _t(H, ROWS_ELEMENTWISE)
    nb = H // tr

    def body(pl_ref, a_ref, b_ref, o_ref):
        o_ref[...] = (a_ref[...] + b_ref[...]).astype(BF16)

    blk = _bs((None, tr, C), lambda j, i, pc: (j, i, 0))
    mine = _bs((None, tr, C), lambda j, i, pc: (j, pc[1] * nb + i, 0))
    spec = pltpu.PrefetchScalarGridSpec(num_scalar_prefetch=1, grid=(J, nb), in_specs=[mine, blk], out_specs=blk)
    return _call(body, grid_spec=spec, out_shape=jax.ShapeDtypeStruct((J, H, C), BF16),
                 compiler_params=_cp(("parallel", "parallel")), name=name)(place, grad, got)


def _sum_chips(sent, arrived, place, name):
    J, H, C = arrived.shape
    tr = _t(H, ROWS_ELEMENTWISE)
    nb = H // tr

    def body(pl_ref, own_ref, a0_ref, a1_ref, a2_ref, o_ref):
        acc = own_ref[...].astype(F32)
        for r in (a0_ref, a1_ref, a2_ref):
            acc = acc + r[...].astype(F32)
        o_ref[...] = acc

    def other(k):
        return _bs((None, tr, C), lambda i, pc: (jnp.where(pc[0] <= k, k + 1, k), i, 0))

    spec = pltpu.PrefetchScalarGridSpec(
        num_scalar_prefetch=1, grid=(nb,),
        in_specs=[_bs((None, tr, C), lambda i, pc: (pc[0], i, 0)), other(0), other(1), other(2)],
        out_specs=_bs((tr, C), lambda i, pc: (pc[1] * nb + i, 0)))
    return _call(body, grid_spec=spec, out_shape=jax.ShapeDtypeStruct((2 * H, C), F32),
                 compiler_params=_cp(("parallel",)), name=name)(place, sent, arrived, arrived, arrived)


def _place():
    x, y, c = lax.axis_index("x"), lax.axis_index("y"), lax.axis_index("c")
    chips = [(1 - x, y), (x, 1 - y), (1 - x, 1 - y)]
    return x, y, c, chips


def _split(nrows, row_bytes, align, cap=None):
    k = max(1, min(cap or DMA_MAX_CHUNKS, (nrows * row_bytes) // DMA_CHUNK_BYTES))
    while k > 1 and nrows % (k * align):
        k -= 1
    return k


def _comm_call(plan, name):
    n_in, n_out = len(plan["ins"]), len(plan["out_shape"])

    def body(*refs):
        for phase in plan["phases"]:
            phase(refs[:n_in], refs[n_in:n_in + n_out], refs[n_in + n_out:])

    return _call(body, in_specs=[ANY] * n_in, out_specs=[ANY] * n_out, out_shape=plan["out_shape"],
                 input_output_aliases={i: i for i in range(n_out)} if plan.get("alias") else {},
                 scratch_shapes=plan["scratch"], name=name)(*plan["ins"])


def _comm_hooks(plan, grid, ins, outs, sems, *, before):
    if plan is None:
        return
    nsteps, step = 1, 0
    for d, g in enumerate(grid):
        nsteps, step = nsteps * g, step * g + pl.program_id(d)
    for p, (phase, frac) in enumerate(zip(plan["phases"], plan["at"])):
        if (p == 0) == before:
            pl.when(step == int(frac * (nsteps - 1)))(functools.partial(phase, ins, outs, sems))


def _ag_plan(shards, axes):
    n = len(shards)
    shapes = [a.shape for a in shards]

    def window(ref, i, chip, half=None):
        S, ax = shapes[i], axes[i]
        idx = []
        for d in range(len(S)):
            off, size = 0, S[d]
            if d == 0 and half is not None:
                off, size = half * (S[0] // 2), S[0] // 2
            if d == ax:
                off = off + chip * S[ax]
            idx.append(pl.ds(off, size))
        return ref.at[tuple(idx)]

    def copies(src, full, sems):
        ssem, rsem = sems
        x, y, c, chips = _place()
        me = 2 * x + y
        sib = (x, y, 1 - c)
        idx = [2 * cx + cy for cx, cy in chips]

        def rcopy(i, k, s_ref, d_ref, to):
            return pltpu.make_async_remote_copy(src_ref=s_ref, dst_ref=d_ref, send_sem=ssem.at[i, k],
                                                recv_sem=rsem.at[i, k], device_id=to, device_id_type=MESH)

        def ici(i, j, incoming):
            half_src = src[i].at[pl.ds(c * (shapes[i][0] // 2), shapes[i][0] // 2)]
            return rcopy(i, j, half_src, window(full[i], i, idx[j] if incoming else me, c), (*chips[j], c))

        def fwd(i, j, half):
            w = window(full[i], i, idx[j], half)
            return rcopy(i, 3 + j, w, w, sib)

        def own(i):
            return rcopy(i, 6, src[i], window(full[i], i, me), sib)

        return c, ici, fwd, own

    def send(src, full, sems):
        c, ici, fwd, own = copies(src, full, sems)
        for i in range(n):
            for j in range(3):
                ici(i, j, False).start()
        for i in range(n):
            own(i).start()

    def forward(i, src, full, sems):
        c, ici, fwd, own = copies(src, full, sems)
        for j in range(3):
            ici(i, j, True).wait_recv()
            fwd(i, j, c).start()

    def finish(src, full, sems):
        c, ici, fwd, own = copies(src, full, sems)
        for i in range(n):
            for j in range(3):
                fwd(i, j, 1 - c).wait_recv()
            own(i).wait()
        for i in range(n):
            for j in range(3):
                ici(i, j, False).wait_send()
                fwd(i, j, c).wait_send()

    out_shape = [jax.ShapeDtypeStruct(tuple(N_CHIP * d if k == ax else d for k, d in enumerate(S)), BF16)
                 for S, ax in zip(shapes, axes)]
    sizes = [a.size for a in shards]
    behind = [AG_FORWARD_SCALE * sum(sizes[:i + 1]) / sum(sizes) + AG_FORWARD_LAG for i in range(n)]
    return dict(ins=list(shards), out_shape=out_shape,
                phases=[send] + [functools.partial(forward, i) for i in range(n)] + [finish],
                at=[0.0] + behind + [1.0],
                scratch=[pltpu.SemaphoreType.DMA((n, 7)), pltpu.SemaphoreType.DMA((n, 7))])


def _proj_ag(x, g1, wsh, order, tm):
    T, D = x.shape
    P = wsh.shape[1]
    H = D // 2
    nt = T // tm

    def body(order_ref, x_ref, g_ref, wsh_ref, hn_ref, proj_ref, win_ref, wbuf, lsem, ssem, rsem):
        n, i = pl.program_id(0), pl.program_id(1)
        x, y, c, chips = _place()
        me = 2 * x + y
        sib = (x, y, 1 - c)
        idx = [2 * cx + cy for cx, cy in chips]

        def rcopy(k, s_ref, d_ref, to):
            return pltpu.make_async_remote_copy(src_ref=s_ref, dst_ref=d_ref, send_sem=ssem.at[k],
                                                recv_sem=rsem.at[k], device_id=to, device_id_type=MESH)

        def cols(chip):
            return pl.ds(pl.multiple_of(chip * P, LANES), P)

        def rows(half):
            return pl.ds(pl.multiple_of(half * H, BF16_TILE_ROWS), H)

        def ici(j, incoming):
            return rcopy(j, wsh_ref.at[rows(c)], win_ref.at[rows(c), cols(idx[j] if incoming else me)],
                         (*chips[j], c))

        def fwd(j, half):
            w = win_ref.at[rows(half), cols(idx[j])]
            return rcopy(3 + j, w, w, sib)

        def own():
            return rcopy(6, wsh_ref, win_ref.at[:, cols(me)], sib)

        def load(src):
            cp = pltpu.make_async_copy(src, wbuf, lsem)
            cp.start()
            cp.wait()

        @pl.when((n == 0) & (i == 0))
        def _():
            ici(0, False).start()
            ici(1, False).start()
            own().start()
            load(wsh_ref)

        for j in range(3):
            @pl.when((n == j + 1) & (i == 0))
            def _(j=j):
                if j == 0:
                    ici(2, False).start()
                ici(j, True).wait_recv()
                fwd(j, c).start()
                fwd(j, 1 - c).wait_recv()
                load(win_ref.at[:, cols(idx[j])])

        xv = x_ref[...]
        r = lax.rsqrt(jnp.mean(xv * xv, axis=-1, keepdims=True) + EPS)
        hn = ((xv * r) * g_ref[...]).astype(BF16)

        @pl.when(n == 0)
        def _():
            hn_ref[...] = hn

        proj_ref[...] = jnp.dot(hn, wbuf[...], preferred_element_type=F32)

        @pl.when((n == 3) & (i == nt - 1))
        def _():
            own().wait()
            for j in range(3):
                ici(j, False).wait_send()
                fwd(j, c).wait_send()

    spec = pltpu.PrefetchScalarGridSpec(
        num_scalar_prefetch=1, grid=(N_CHIP, nt),
        in_specs=[_bs((tm, D), lambda n, i, o: (i, 0)), _bs((1, D), lambda n, i, o: (0, 0)), ANY],
        out_specs=[_bs((tm, D), lambda n, i, o: (jnp.where(n == 0, i, nt - 1), 0)),
                   _bs((tm, P), lambda n, i, o: (i, o[n])), ANY],
        scratch_shapes=[pltpu.VMEM((D, P), BF16), pltpu.SemaphoreType.DMA,
                        pltpu.SemaphoreType.DMA((7,)), pltpu.SemaphoreType.DMA((7,))])
    return _call(body, grid_spec=spec,
                 out_shape=[jax.ShapeDtypeStruct((T, D), BF16), jax.ShapeDtypeStruct((T, N_CHIP * P), F32),
                            jax.ShapeDtypeStruct((D, N_CHIP * P), BF16)],
                 compiler_params=_cp(("arbitrary", "arbitrary")), name="proj_ag")(order, x, g1, wsh)


def _halves_plan(grads):
    n = len(grads)

    def send(g, got, sems):
        ssem, rsem = sems
        x, y, c, _ = _place()
        sib = (x, y, 1 - c)
        for i in range(n):
            J, R, C = g[i].shape
            H = R // 2
            size = g[i].dtype.itemsize
            tile_rows = SUBLANES * 4 // size
            k = _split(H, C * size, tile_rows, cap=DMA_MAX_CHUNKS // J)
            hr = H // k
            for j in range(J):
                for q in range(k):
                    other = pl.ds(pl.multiple_of((1 - c) * H + q * hr, tile_rows), hr)
                    to = pl.ds(q * hr, hr)
                    pltpu.make_async_remote_copy(src_ref=g[i].at[j, other, :], dst_ref=got[i].at[j, to, :],
                                                 send_sem=ssem.at[i], recv_sem=rsem.at[i],
                                                 device_id=sib, device_id_type=MESH).start()

    def finish(g, got, sems):
        ssem, rsem = sems
        x, y, c, _ = _place()
        for i in range(n):
            pltpu.make_async_remote_copy(src_ref=got[i], dst_ref=got[i], send_sem=ssem.at[i], recv_sem=rsem.at[i],
                                         device_id=(x, y, 1 - c), device_id_type=MESH).wait()

    half = [jax.ShapeDtypeStruct((a.shape[0], a.shape[1] // 2, a.shape[2]), a.dtype) for a in grads]
    return dict(ins=list(grads), out_shape=half, phases=[send, finish], at=[0.0, 1.0],
                scratch=[pltpu.SemaphoreType.DMA((n,)), pltpu.SemaphoreType.DMA((n,))])


def _scatter_plan(parts):
    n = len(parts)

    def peers():
        x, y, c, chips = _place()
        return 2 * x + y, c, chips, [2 * cx + cy for cx, cy in chips]

    def send(s, got, sems):
        ssem, rsem = sems
        me, c, chips, idx = peers()
        for i in range(n):
            _, H, C = s[i].shape
            k = _split(H, C * 2, BF16_TILE_ROWS, cap=RS_CHUNKS)
            hr = H // k
            for q in range(k):
                rows = pl.ds(q * hr, hr)
                for j in range(3):
                    pltpu.make_async_remote_copy(src_ref=s[i].at[idx[j], rows, :], dst_ref=got[i].at[me, rows, :],
                                                 send_sem=ssem.at[i, j], recv_sem=rsem.at[i, j],
                                                 device_id=(*chips[j], c), device_id_type=MESH).start()

    def finish(s, got, sems):
        ssem, rsem = sems
        me, c, chips, idx = peers()
        for i in range(n):
            for j in range(3):
                pltpu.make_async_remote_copy(src_ref=s[i].at[idx[j]], dst_ref=got[i].at[idx[j]],
                                             send_sem=ssem.at[i, j], recv_sem=rsem.at[i, j],
                                             device_id=(*chips[j], c), device_id_type=MESH).wait()

    return dict(ins=list(parts), out_shape=[jax.ShapeDtypeStruct(a.shape, a.dtype) for a in parts],
                phases=[send, finish], at=[0.0, 1.0],
                scratch=[pltpu.SemaphoreType.DMA((n, 3)), pltpu.SemaphoreType.DMA((n, 3))])


def _join_plan(shards):
    n = len(shards)

    def send(_, full, sems):
        ssem, rsem = sems
        x, y, c, _ = _place()
        sib = (x, y, 1 - c)
        for i in range(n):
            H, C = full[i].shape[0] // 2, full[i].shape[1]
            k = _split(H, C * 4, SUBLANES)
            hr = H // k
            for q in range(k):
                rows = pl.ds(pl.multiple_of(c * H + q * hr, SUBLANES), hr)
                pltpu.make_async_remote_copy(src_ref=full[i].at[rows], dst_ref=full[i].at[rows],
                                             send_sem=ssem.at[i], recv_sem=rsem.at[i],
                                             device_id=sib, device_id_type=MESH).start()

    def finish(_, full, sems):
        ssem, rsem = sems
        x, y, c, _ = _place()
        for i in range(n):
            half = full[i].at[pl.ds(0, full[i].shape[0] // 2)]
            pltpu.make_async_remote_copy(src_ref=half, dst_ref=half, send_sem=ssem.at[i], recv_sem=rsem.at[i],
                                         device_id=(x, y, 1 - c), device_id_type=MESH).wait()

    return dict(ins=list(shards), out_shape=[jax.ShapeDtypeStruct(a.shape, a.dtype) for a in shards],
                phases=[send, finish], at=[0.0, 1.0], alias=True,
                scratch=[pltpu.SemaphoreType.DMA((n,)), pltpu.SemaphoreType.DMA((n,))])


def _allreduce_plan(buf):
    R, L = buf.shape
    RB = R // N_DEV

    def parts(sems):
        xv, got, ov, lsem, ssem, rsem = sems
        x, y, c, _ = _place()
        me = 4 * x + 2 * y + c

        def dev(k):
            return (k // 4, (k // 2) % 2, k % 2)

        def slab(k):
            return pl.ds(pl.multiple_of(k * RB, SUBLANES), RB)

        def first(d, to, landing):
            return pltpu.make_async_remote_copy(src_ref=xv.at[slab(to)], dst_ref=got.at[landing],
                                                send_sem=ssem.at[0, d], recv_sem=rsem.at[0, d],
                                                device_id=dev(to), device_id_type=MESH)

        def second(d, to, k):
            return pltpu.make_async_remote_copy(src_ref=ov.at[slab(k)], dst_ref=ov.at[slab(k)],
                                                send_sem=ssem.at[1, d], recv_sem=rsem.at[1, d],
                                                device_id=dev(to), device_id_type=MESH)

        return me, slab, first, second

    def scatter(ins, outs, sems):
        xv, lsem = sems[0], sems[3]
        me, slab, first, second = parts(sems)
        cp = pltpu.make_async_copy(ins[0], xv, lsem)
        cp.start()
        cp.wait()
        for d in range(1, N_DEV):
            first(d, (me + d) % N_DEV, me).start()

    def reduce(ins, outs, sems):
        xv, got, ov = sems[:3]
        me, slab, first, second = parts(sems)
        got[me] = xv[slab(me), :]
        for d in range(1, N_DEV):
            src = (me + N_DEV - d) % N_DEV
            first(d, src, src).wait_recv()
        acc = got[0]
        for k in range(1, N_DEV):
            acc = acc + got[k]
        ov[slab(me), :] = acc
        for d in range(1, N_DEV):
            second(d, (me + d) % N_DEV, me).start()

    def collect(ins, outs, sems):
        ov, lsem = sems[2], sems[3]
        me, slab, first, second = parts(sems)
        for d in range(1, N_DEV):
            src = (me + N_DEV - d) % N_DEV
            second(d, src, src).wait_recv()
        for d in range(1, N_DEV):
            peer = (me + d) % N_DEV
            first(d, peer, me).wait_send()
            second(d, peer, me).wait_send()
        cp = pltpu.make_async_copy(ov, outs[0], lsem)
        cp.start()
        cp.wait()

    return dict(ins=[buf], out_shape=[jax.ShapeDtypeStruct((R, L), F32)], phases=[scatter, reduce, collect],
                at=[0.0, 0.5, 1.0],
                scratch=[pltpu.VMEM((R, L), F32), pltpu.VMEM((N_DEV, RB, L), F32), pltpu.VMEM((R, L), F32),
                         pltpu.SemaphoreType.DMA, pltpu.SemaphoreType.DMA((2, N_DEV)),
                         pltpu.SemaphoreType.DMA((2, N_DEV))])


def _block_diag(t, gt):
    G, A, B = t.shape
    t4 = t.reshape(G // gt, gt, A, B)
    eye = jnp.eye(gt, dtype=t.dtype)
    return jnp.einsum('sgab,gh->sgahb', t4, eye).reshape(G // gt, gt * A, gt * B)


def _block_diag_extract(m, gt, A, B):
    S = m.shape[0]
    m5 = m.reshape(S, gt, A, gt, B)
    eye = jnp.eye(gt, dtype=m.dtype)
    return jnp.einsum('sgahb,gh->sgab', m5, eye).reshape(S * gt, A, B)


def _tile_rows(n):
    return -(-n // (SUBLANES * LANES)) * SUBLANES


def _pack_small(arrs, rows):
    parts = []
    for a in arrs:
        flat = a.reshape(-1).astype(F32)
        r = _tile_rows(flat.shape[0])
        parts.append(jnp.pad(flat, (0, r * LANES - flat.shape[0])).reshape(r, LANES))
    used = sum(p.shape[0] for p in parts)
    if rows > used:
        parts.append(jnp.zeros((rows - used, LANES), F32))
    return jnp.concatenate(parts)


def _unpack_small(buf, shapes):
    out, off = [], 0
    for s in shapes:
        n = 1
        for d in s:
            n *= d
        r = _tile_rows(n)
        piece = buf[off:off + r]
        out.append(piece.reshape(s) if n == r * LANES else piece.reshape(-1)[:n].reshape(s))
        off += r
    return out


def kernel(x, p, norm_gain, w_in, w_pool, pool_scale, a_re, a_im, log_dt, b_re, b_im, c_re, c_im, d_skip, w_glu, w_out, w_ple, w_ple_gate, final_gain, loss_target, m_norm_gain, m_w_in, m_w_pool, m_pool_scale, m_a_re, m_a_im, m_log_dt, m_b_re, m_b_im, m_c_re, m_c_im, m_d_skip, m_w_glu, m_w_out, m_w_ple, m_w_ple_gate, m_final_gain, v_norm_gain, v_w_in, v_w_pool, v_pool_scale, v_a_re, v_a_im, v_log_dt, v_b_re, v_b_im, v_c_re, v_c_im, v_d_skip, v_w_glu, v_w_out, v_w_ple, v_w_ple_gate, v_final_gain):
    xs, pe, tgt = x[0], p[0, 0], loss_target[0]
    T, D = xs.shape
    E = pe.shape[1]
    P = D // 2
    NG = len(POOL_WINDOWS)
    PG = P // NG
    G, N, C = P // SSM_GROUP, SSM_STATE, SSM_GROUP
    GT = min(SSM_TILE_GROUPS, G)
    Q = D // N_CHIP

    big = {"w_in": (w_in, m_w_in, v_w_in), "w_pool": (w_pool, m_w_pool, v_w_pool),
           "w_glu": (w_glu, m_w_glu, v_w_glu), "w_out": (w_out, m_w_out, v_w_out),
           "w_ple": (w_ple, m_w_ple, v_w_ple), "w_ple_gate": (w_ple_gate, m_w_ple_gate, v_w_ple_gate)}
    big_names = list(big)
    shard2d = {n: (big[n][0].size // big[n][0].shape[-1], big[n][0].shape[-1]) for n in big_names}
    shard_axis = {"w_in": 1, "w_pool": 1, "w_glu": 1, "w_out": 0, "w_ple": 1, "w_ple_gate": 0}
    shard16 = {n: big[n][0][0].astype(BF16) for n in big_names}
    place = jnp.stack([2 * lax.axis_index("x") + lax.axis_index("y"), lax.axis_index("c")]).astype(jnp.int32)
    mx, my = lax.axis_index("x"), lax.axis_index("y")
    block_order = jnp.stack([2 * mx + my, 2 * (1 - mx) + my, 2 * mx + (1 - my),
                             2 * (1 - mx) + (1 - my)]).astype(jnp.int32)
    later = [n for n in big_names if n != "w_in"]
    ag_later = _ag_plan([shard16[n] for n in later], [shard_axis[n] for n in later])

    rep = lambda a: jnp.repeat(a, C, axis=0)
    a_re_r, a_im_r = rep(a_re[0]), rep(a_im[0])
    ldt_r = rep(jnp.broadcast_to(log_dt[0][:, None], (G, N)))
    bt_re = b_re[0].transpose(0, 2, 1).reshape(G * C, N)
    bt_im = b_im[0].transpose(0, 2, 1).reshape(G * C, N)
    ab_re_r, ab_im_r, bbt_re, bbt_im = _ssm_prep(a_re_r, a_im_r, ldt_r, bt_re, bt_im)
    abr = ab_re_r[::C].reshape(1, G * N)
    abi = ab_im_r[::C].reshape(1, G * N)
    bdr = _block_diag(bbt_re.reshape(G, C, N), GT).astype(BF16)
    bdi = _block_diag(bbt_im.reshape(G, C, N), GT).astype(BF16)
    cdr = _block_diag(c_re[0].transpose(0, 2, 1), GT).astype(BF16)
    cdi = _block_diag(c_im[0].transpose(0, 2, 1), GT).astype(BF16)

    tb = _t(T, ROWS_ELEMENTWISE)
    tbs = _t(T, ROWS_SSM)
    tm = _t(T, ROWS_MATMUL)
    tk = _t(T, DEPTH_MATMUL)
    DH = _t(D, ROWS_MATMUL)
    row_k = lambda i, n, k: (i, k)
    row_n = lambda i, n, k: (i, n)
    f32 = lambda *shape: jax.ShapeDtypeStruct(shape, F32)
    hn, proj, win = _proj_ag(xs, norm_gain, shard16["w_in"], block_order, tm)
    y, ge, bsr, bsi, wp, wglu, wout, wple, wpg = _ssm_fwd(proj, bdr, bdi, cdr, cdi, abr, abi, d_skip, P, tbs,
                                                          comm=ag_later)
    pooled, mixed = _pool_fwd(proj, wp, P, tb)
    hg = _mm(ge, wglu, dims=NN, grid=(T // tm, 1, 1),
             a_spec=_bs((tm, P), row_k), b_spec=_bs((P, 2 * P), lambda i, n, k: (k, n)),
             o_spec=_bs((tm, 2 * P), row_n), out_shape=f32(T, 2 * P), name="mm_glu")
    cat = _gate_fwd(mixed, proj, hg, pool_scale, tb)
    h1, h1b = _mm(cat, wout, dims=NN, grid=(T // tm, D // DH, 1), res=xs, bf16_copy=True,
                  a_spec=_bs((tm, D), row_k), b_spec=_bs((D, DH), lambda i, n, k: (k, n)),
                  r_spec=_bs((tm, DH), row_n), o_spec=_bs((tm, DH), row_n), out_shape=f32(T, D), name="mm_out")
    z = _mm(h1b, wpg, dims=NN, grid=(T // tm, 1, 1),
            a_spec=_bs((tm, D), row_k), b_spec=_bs((D, D), lambda i, n, k: (k, n)),
            o_spec=_bs((tm, D), row_n), out_shape=f32(T, D), name="mm_pgate")
    dh2, dz, dg2, lpart, g_wple, g_wple16 = _final_fb(h1, pe, wple, z, tgt, final_gain.reshape(1, D), tb)

    col_m = lambda m, n, k: (k, m)
    col_n = lambda m, n, k: (k, n)
    dh1, dh1b = _mm(dz, wpg, dims=NT, grid=(T // tm, D // DH, 1), res=dh2, bf16_copy=True,
                    a_spec=_bs((tm, D), row_k), b_spec=_bs((DH, D), lambda i, n, k: (n, k)),
                    r_spec=_bs((tm, DH), row_n), o_spec=_bs((tm, DH), row_n), out_shape=f32(T, D), name="mm_dh1")
    g_wpg, g_wpg16 = _mm(h1b, dz, dims=TN, grid=(D // DH, D // DH, T // tk), bf16_copy=True,
                         a_spec=_bs((tk, DH), col_m), b_spec=_bs((tk, DH), col_n),
                         o_spec=_bs((DH, DH), lambda m, n, k: (m, n)), out_shape=f32(D, D), name="mm_gwpg")
    dcat = _mm(dh1b, wout, dims=NT, grid=(T // tm, 1, 1),
               a_spec=_bs((tm, D), row_k), b_spec=_bs((D, D), lambda i, n, k: (n, k)),
               o_spec=_bs((tm, D), row_n), out_shape=f32(T, D), name="mm_dcat")
    g_wout, g_wout16 = _mm(cat, dh1b, dims=TN, grid=(D // DH, D // DH, T // tk), bf16_copy=True,
                           a_spec=_bs((tk, DH), col_m), b_spec=_bs((tk, DH), col_n),
                           o_spec=_bs((DH, DH), lambda m, n, k: (m, n)), out_shape=f32(D, D), name="mm_gwout")
    gbig = {"w_out": g_wout.reshape(N_CHIP, Q, D), "w_ple": g_wple, "w_ple_gate": g_wpg.reshape(N_CHIP, Q, D)}
    gbig16 = {"w_out": g_wout16.reshape(N_CHIP, Q, D), "w_ple": g_wple16,
              "w_ple_gate": g_wpg16.reshape(N_CHIP, Q, D)}
    first = list(gbig)
    res = _gate_bwd(dcat, mixed, proj, hg, pool_scale, tb, comm=_halves_plan([gbig16[n] for n in first]))
    dmixed, dpg, dsg, dhg, dps = res[:5]
    got = dict(zip(first, res[5:]))
    dge = _mm(dhg, wglu, dims=NT, grid=(T // tm, 1, 1),
              a_spec=_bs((tm, 2 * P), row_k), b_spec=_bs((P, 2 * P), lambda i, n, k: (n, k)),
              o_spec=_bs((tm, P), row_n), out_shape=f32(T, P), name="mm_dge")
    gbig["w_glu"], gbig16["w_glu"] = _mm(ge, dhg, dims=TN, grid=(1, N_CHIP, T // tk), bf16_copy=True,
                                         a_spec=_bs((tk, P), col_m), b_spec=_bs((tk, Q), col_n),
                                         o_spec=_bs((None, P, Q), lambda m, j, k: (j, 0, 0)),
                                         out_shape=f32(N_CHIP, P, Q), name="mm_gwglu")
    g_wp = _mm(pooled, dmixed, dims=TN, grid=(NG, 1, T // tk), bf16_copy=True,
               a_spec=_bs((tk, PG), col_m), b_spec=_bs((tk, PG), col_m),
               o_spec=_bs((None, PG, PG), lambda g, n, k: (g, 0, 0)), out_shape=f32(NG, PG, PG), name="mm_gwp")
    by_chip = lambda a: a.reshape(NG, N_CHIP, PG // N_CHIP, PG).transpose(1, 0, 2, 3).reshape(
        N_CHIP, NG * PG // N_CHIP, PG)
    gbig["w_pool"], gbig16["w_pool"] = by_chip(g_wp[0]), by_chip(g_wp[1])
    res = _pool_bwd(dmixed, wp, tb, comm=_halves_plan([gbig16["w_pool"], gbig16["w_glu"]]))
    dpi, got["w_pool"], got["w_glu"] = res
    early = list(gbig)
    chip_sums = {n: _sum_cast(gbig[n], got[n], place, "sum_cast_" + n) for n in early}
    res = _ssm_bwd(proj, y, dge, bsr, bsi, bdr, bdi, cdr, cdi, abr, abi, d_skip, dpi, dpg, dsg, P, tbs,
                   comm=_scatter_plan([chip_sums[n] for n in early]))
    dproj, dabr, dabi, dd, dbdr, dbdi, dcdr, dcdi = res[:8]
    arrived = dict(zip(early, res[8:]))
    halves = [_sum_chips(chip_sums[n], arrived[n], place, "sum_chips_" + n) for n in early]
    res = _mm(hn, dproj, dims=TN, grid=(D // DH, N_CHIP, T // tk), bf16_copy=True,
              a_spec=_bs((tk, DH), col_m), b_spec=_bs((tk, P), col_n),
              o_spec=_bs((None, DH, P), lambda m, j, k: (j, m, 0)), out_shape=f32(N_CHIP, D, P),
              name="mm_gwin", comm=_join_plan(halves))
    gbig["w_in"], gbig16["w_in"], gshard = res[0], res[1], dict(zip(early, res[2:]))
    got["w_in"], = _comm_call(_halves_plan([gbig16["w_in"]]), "rs_halves_late")
    chip_sums["w_in"] = _sum_cast(gbig["w_in"], got["w_in"], place, "sum_cast_w_in")
    KH = _t(4 * P, DEPTH_MATMUL)
    dhn, arrived["w_in"] = _mm(dproj, win, dims=NT, grid=(T // tm, D // DH, 4 * P // KH),
                               a_spec=_bs((tm, KH), row_k), b_spec=_bs((DH, KH), lambda i, n, k: (n, k)),
                               o_spec=_bs((tm, DH), row_n), out_shape=f32(T, D), name="mm_dhn",
                               comm=_scatter_plan([chip_sums["w_in"]]))
    gshard["w_in"], = _comm_call(
        _join_plan([_sum_chips(chip_sums["w_in"], arrived["w_in"], place, "sum_chips_w_in")]), "rs_join_w_in")
    grad_x, dg1 = _norm1_bwd(xs, dhn, dh1, norm_gain, tb)

    dbbt_re = _block_diag_extract(dbdr, GT, C, N).reshape(G * C, N)
    dbbt_im = _block_diag_extract(dbdi, GT, C, N).reshape(G * C, N)
    g_c_re = _block_diag_extract(dcdr, GT, N, C).transpose(0, 2, 1)
    g_c_im = _block_diag_extract(dcdi, GT, N, C).transpose(0, 2, 1)
    dab_re_r = rep(dabr.reshape(G, N)) * (1.0 / C)
    dab_im_r = rep(dabi.reshape(G, N)) * (1.0 / C)
    g_a_re, g_a_im, g_ldt, g_bt_re, g_bt_im = _ssm_prep_bwd(a_re_r, a_im_r, ldt_r, bt_re, bt_im,
                                                            dab_re_r, dab_im_r, dbbt_re, dbbt_im, G)
    g_b_re = g_bt_re.reshape(G, C, N).transpose(0, 2, 1)
    g_b_im = g_bt_im.reshape(G, C, N).transpose(0, 2, 1)


    small_names = ["norm_gain", "pool_scale", "a_re", "a_im", "log_dt", "b_re", "b_im", "c_re", "c_im",
                   "d_skip", "final_gain"]
    small_w = dict(norm_gain=norm_gain, pool_scale=pool_scale, a_re=a_re, a_im=a_im, log_dt=log_dt, b_re=b_re,
                   b_im=b_im, c_re=c_re, c_im=c_im, d_skip=d_skip, final_gain=final_gain)
    small_m = dict(norm_gain=m_norm_gain, pool_scale=m_pool_scale, a_re=m_a_re, a_im=m_a_im, log_dt=m_log_dt,
                   b_re=m_b_re, b_im=m_b_im, c_re=m_c_re, c_im=m_c_im, d_skip=m_d_skip, final_gain=m_final_gain)
    small_v = dict(norm_gain=v_norm_gain, pool_scale=v_pool_scale, a_re=v_a_re, a_im=v_a_im, log_dt=v_log_dt,
                   b_re=v_b_re, b_im=v_b_im, c_re=v_c_re, c_im=v_c_im, d_skip=v_d_skip, final_gain=v_final_gain)
    small_g = dict(norm_gain=dg1, pool_scale=dps, a_re=g_a_re, a_im=g_a_im, log_dt=g_ldt, b_re=g_b_re,
                   b_im=g_b_im, c_re=g_c_re, c_im=g_c_im, d_skip=dd, final_gain=dg2)
    shapes = [small_w[n].shape for n in small_names]
    loss_row = sum(_tile_rows(small_w[n].size) for n in small_names)
    unit = N_DEV * SUBLANES
    rows = -(-(loss_row + SUBLANES) // unit) * unit
    gbuf = _pack_small([small_g[n] for n in small_names] + [lpart[0, :1]], rows)
    gsum, = _comm_call(_allreduce_plan(gbuf), "allreduce_small")
    g_out, d_out, m_out, v_out = {}, {}, {}, {}
    for n in big_names:
        w_, m_, v_ = big[n]
        r2 = shard2d[n]
        res = _adamw(w_.reshape(r2), gshard[n], m_.reshape(r2), v_.reshape(r2), "adamw_" + n)
        g_out[n], d_out[n], m_out[n], v_out[n] = (a.reshape(w_.shape) for a in (gshard[n], *res))
    wbuf = _pack_small([small_w[n] for n in small_names], rows)
    mbuf = _pack_small([small_m[n] for n in small_names], rows)
    vbuf = _pack_small([small_v[n] for n in small_names], rows)
    dsm, msm, vsm = _adamw(wbuf, gsum, mbuf, vbuf, "adamw_small")
    g_small = dict(zip(small_names, _unpack_small(gsum, shapes)))
    d_small = dict(zip(small_names, _unpack_small(dsm, shapes)))
    m_small = dict(zip(small_names, _unpack_small(msm, shapes)))
    v_small = dict(zip(small_names, _unpack_small(vsm, shapes)))
    loss = gsum[loss_row, 0]

    g_out.update(g_small)
    d_out.update(d_small)
    m_out.update(m_small)
    v_out.update(v_small)

    order = ["norm_gain", "w_in", "w_pool", "pool_scale", "a_re", "a_im", "log_dt", "b_re", "b_im", "c_re",
             "c_im", "d_skip", "w_glu", "w_out", "w_ple", "w_ple_gate", "final_gain"]
    return (loss, grad_x[None], *[g_out[n] for n in order], *[d_out[n] for n in order],
            *[m_out[n] for n in order], *[v_out[n] for n in order])
```

```python
import functools

import jax
import jax.numpy as jnp
from jax import lax
from jax.experimental import pallas as pl
from jax.experimental.pallas import tpu as pltpu

F32, BF16 = jnp.float32, jnp.bfloat16
MESH = pl.DeviceIdType.MESH
ANY = pl.BlockSpec(memory_space=pl.ANY)
VMEM_FULL = pl.BlockSpec(memory_space=pltpu.VMEM)

EPS = 1e-6
A_RE_MAX = -1e-4
SSM_GROUP = 16
SSM_STATE = 64
POOL_WINDOWS = (2, 4, 8, 16)
POOL_HALO = 16
ADAM_LR, ADAM_B1, ADAM_B2, ADAM_EPS, ADAM_WD, ADAM_STEP = 0.001, 0.9, 0.999, 1e-08, 0.01, 10

V7X_VMEM_BYTES = 64 * 1024 * 1024
VMEM_LIMIT = V7X_VMEM_BYTES - 8 * 1024 * 1024
SUBLANES, LANES = 8, 128
BF16_TILE_ROWS = 16
SSM_TILE_GROUPS = 8
SCAN_LANES = 1024
N_DEV, N_CHIP = 8, 4
DMA_CHUNK_BYTES = 256 * 1024
DMA_MAX_CHUNKS = 32
RS_CHUNKS = 8
ROWS_ELEMENTWISE = 256
ROWS_SSM = 256
ROWS_MATMUL = 1024
DEPTH_MATMUL = 4096
AG_FORWARD_SCALE, AG_FORWARD_LAG = 0.85, 0.05


def _t(n, pref):
    return pref if n % pref == 0 else n


def _cp(sem=None, vmem=VMEM_LIMIT):
    return pltpu.CompilerParams(dimension_semantics=sem, vmem_limit_bytes=vmem)


def _call(body, **kw):
    return pl.pallas_call(body, **kw)


NN = ((1,), (0,))
NT = ((1,), (1,))
TN = ((0,), (0,))


def _mm(a, b, *, dims, grid, a_spec, b_spec, o_spec, out_shape, name, res=None, r_spec=None, bf16_copy=False,
        comm=None):
    nk, kax = grid[-1], len(grid) - 1
    acc_shape = tuple(d for d in o_spec.block_shape if d is not None)

    def core(*refs):
        refs = list(refs)
        a_ref, b_ref = refs[:2]
        r_ref = refs[2] if res is not None else None
        outs = refs[3 if res is not None else 2:]
        o_ref = outs[0]
        o2_ref = outs[1] if bf16_copy else None
        acc = outs[-1] if nk > 1 else None

        def finish(r):
            if r_ref is not None:
                r = r + r_ref[...]
            o_ref[...] = r.astype(o_ref.dtype)
            if o2_ref is not None:
                o2_ref[...] = r.astype(BF16)

        part = lax.dot_general(a_ref[...].astype(BF16), b_ref[...].astype(BF16),
                               (dims, ((), ())), preferred_element_type=F32)
        if nk == 1:
            finish(part)
        else:
            k = pl.program_id(kax)

            @pl.when(k == 0)
            def _():
                acc[...] = part

            @pl.when(k > 0)
            def _():
                acc[...] += part

            @pl.when(k == nk - 1)
            def _():
                finish(acc[...])

    ins, specs = [a, b], [a_spec, b_spec]
    if res is not None:
        ins.append(res)
        specs.append(r_spec)
    o_specs, o_shapes = [o_spec], [out_shape]
    if bf16_copy:
        o_specs = [o_spec, o_spec]
        o_shapes = [out_shape, jax.ShapeDtypeStruct(out_shape.shape, BF16)]
    scratch = [pltpu.VMEM(acc_shape, F32)] if nk > 1 else []
    body, extra = _hosted(core, comm, grid, len(ins), len(o_specs), len(scratch))
    sem = ("arbitrary",) * len(grid) if comm else ("parallel",) * kax + ("arbitrary",)
    outs = _call(body, grid=grid, in_specs=specs + extra["in_specs"], out_specs=o_specs + extra["out_specs"],
                 out_shape=o_shapes + extra["out_shape"], scratch_shapes=scratch + extra["scratch"],
                 input_output_aliases=extra["aliases"],
                 compiler_params=_cp(sem), name=name)(*ins, *extra["ins"])
    return outs[0] if len(outs) == 1 else outs


def _bs(shape, fn):
    return pl.BlockSpec(shape, fn)


def _sigmoid(v):
    return 0.5 * jnp.tanh(0.5 * v) + 0.5


def _gelu(v):
    return 0.5 * v * (1.0 + jnp.tanh(0.7978845608028654 * (v + 0.044715 * v * v * v)))


def _gelu_grad(v):
    t = jnp.tanh(0.7978845608028654 * (v + 0.044715 * v * v * v))
    return 0.5 * (1.0 + t) + 0.5 * v * (1.0 - t * t) * 0.7978845608028654 * (1.0 + 3 * 0.044715 * v * v)


def _norm1_bwd(x, dhn, dh1, g1, tb, comm=None):
    T, D = x.shape

    def core(x_ref, dhn_ref, dh1_ref, g_ref, dx_ref, dg_ref):
        @pl.when(pl.program_id(0) == 0)
        def _():
            dg_ref[...] = jnp.zeros_like(dg_ref)

        xv = x_ref[...]
        r = lax.rsqrt(jnp.mean(xv * xv, axis=-1, keepdims=True) + EPS)
        xh = xv * r
        dhn_v = dhn_ref[...]
        dg_ref[...] += jnp.sum(dhn_v * xh, axis=0, keepdims=True)
        dxh = dhn_v * g_ref[...]
        dx_ref[...] = dh1_ref[...] + r * (dxh - xh * jnp.mean(dxh * xh, axis=-1, keepdims=True))

    row = _bs((tb, D), lambda i: (i, 0))
    vec = _bs((1, D), lambda i: (0, 0))
    body, extra = _hosted(core, comm, (T // tb,), 4, 2, 0)
    return _call(body, grid=(T // tb,), in_specs=[row, row, row, vec] + extra["in_specs"],
                 out_specs=[row, vec] + extra["out_specs"],
                 out_shape=[jax.ShapeDtypeStruct((T, D), F32), jax.ShapeDtypeStruct((1, D), F32)] + extra["out_shape"],
                 scratch_shapes=extra["scratch"], input_output_aliases=extra["aliases"],
                 compiler_params=_cp(("arbitrary",)), name="norm1_bwd")(x, dhn, dh1, g1, *extra["ins"])


def _gate_fwd(mixed, proj, hg, ps, tb):
    T, P = mixed.shape

    def body(mx_ref, pg_ref, sg_ref, hg_ref, ps_ref, o_ref):
        pg, sg = pg_ref[...], sg_ref[...]
        ya = (mx_ref[...] * ps_ref[...]) * (pg * _sigmoid(pg))
        hgv = hg_ref[...]
        o = hgv[:, :P] * _sigmoid(hgv[:, P:])
        yb = o * (sg * _sigmoid(sg))
        o_ref[:, :P] = ya.astype(BF16)
        o_ref[:, P:] = yb.astype(BF16)

    return _call(body, grid=(T // tb,),
                 in_specs=[_bs((tb, P), lambda i: (i, 0)), _bs((tb, P), lambda i: (i, 1)),
                           _bs((tb, P), lambda i: (i, 3)), _bs((tb, 2 * P), lambda i: (i, 0)),
                           _bs((1, P), lambda i: (0, 0))],
                 out_specs=_bs((tb, 2 * P), lambda i: (i, 0)),
                 out_shape=jax.ShapeDtypeStruct((T, 2 * P), BF16),
                 compiler_params=_cp(("parallel",)), name="gate_fwd")(mixed, proj, proj, hg, ps)


def _gate_bwd(dcat, mixed, proj, hg, ps, tb, comm=None):
    T, P = mixed.shape

    def core(dc_ref, mx_ref, pg_ref, sg_ref, hg_ref, ps_ref, dmx_ref, dpg_ref, dsg_ref, dhg_ref, dps_ref):
        @pl.when(pl.program_id(0) == 0)
        def _():
            dps_ref[...] = jnp.zeros_like(dps_ref)

        dc = dc_ref[...]
        dya, dyb = dc[:, :P], dc[:, P:]
        pg, sg, mx, psv = pg_ref[...], sg_ref[...], mx_ref[...], ps_ref[...]
        s_pg = _sigmoid(pg)
        dpa = dya * (pg * s_pg)
        dpg_ref[...] = (dya * (mx * psv) * (s_pg * (1.0 + pg * (1.0 - s_pg)))).astype(BF16)
        dps_ref[...] += jnp.sum(dpa * mx, axis=0, keepdims=True)
        dmx_ref[...] = (dpa * psv).astype(BF16)
        hgv = hg_ref[...]
        h1, s_h2 = hgv[:, :P], _sigmoid(hgv[:, P:])
        s_sg = _sigmoid(sg)
        do = dyb * (sg * s_sg)
        dsg_ref[...] = (dyb * (h1 * s_h2) * (s_sg * (1.0 + sg * (1.0 - s_sg)))).astype(BF16)
        dhg_ref[:, :P] = (do * s_h2).astype(BF16)
        dhg_ref[:, P:] = (do * h1 * s_h2 * (1.0 - s_h2)).astype(BF16)

    rowp = _bs((tb, P), lambda i: (i, 0))
    row2 = _bs((tb, 2 * P), lambda i: (i, 0))
    vec = _bs((1, P), lambda i: (0, 0))
    body, extra = _hosted(core, comm, (T // tb,), 6, 5, 0)
    return _call(body, grid=(T // tb,),
                 in_specs=[row2, rowp, _bs((tb, P), lambda i: (i, 1)), _bs((tb, P), lambda i: (i, 3)), row2, vec]
                 + extra["in_specs"],
                 out_specs=[rowp, rowp, rowp, row2, vec] + extra["out_specs"],
                 out_shape=[jax.ShapeDtypeStruct((T, P), BF16), jax.ShapeDtypeStruct((T, P), BF16),
                            jax.ShapeDtypeStruct((T, P), BF16), jax.ShapeDtypeStruct((T, 2 * P), BF16),
                            jax.ShapeDtypeStruct((1, P), F32)] + extra["out_shape"],
                 scratch_shapes=extra["scratch"],
                 compiler_params=_cp(("arbitrary",)), name="gate_bwd")(dcat, mixed, proj, proj, hg, ps, *extra["ins"])


def _final_fb(h1, pe, wple, z, tgt, g2, tb):
    T, D = h1.shape
    E = pe.shape[1]
    Q = D // N_CHIP
    nb = T // tb

    def body(h1_ref, p_ref, w_ref, z_ref, t_ref, g_ref, dh2_ref, dz_ref, dg_ref, l_ref, gw_ref, gw16_ref, acc):
        @pl.when(pl.program_id(0) == 0)
        def _():
            dg_ref[...] = jnp.zeros_like(dg_ref)
            l_ref[...] = jnp.zeros_like(l_ref)
            acc[...] = jnp.zeros_like(acc)

        pb = p_ref[...].astype(BF16)
        ev = jnp.dot(pb, w_ref[...], preferred_element_type=F32)
        s = _sigmoid(z_ref[...])
        h2 = h1_ref[...] + ev * s
        r = lax.rsqrt(jnp.mean(h2 * h2, axis=-1, keepdims=True) + EPS)
        xh = h2 * r
        gv = g_ref[...]
        diff = xh * gv - t_ref[...]
        l_ref[...] += 0.5 * jnp.sum(jnp.mean(diff * diff, axis=-1, keepdims=True))
        dout = diff * (1.0 / D)
        dg_ref[...] += jnp.sum(dout * xh, axis=0, keepdims=True)
        dxh = dout * gv
        dh2 = r * (dxh - xh * jnp.mean(dxh * xh, axis=-1, keepdims=True))
        dh2_ref[...] = dh2
        dz_ref[...] = (dh2 * ev * s * (1.0 - s)).astype(BF16)
        acc[...] += lax.dot_general(pb, (dh2 * s).astype(BF16), (TN, ((), ())), preferred_element_type=F32)

        @pl.when(pl.program_id(0) == nb - 1)
        def _():
            for j in range(N_CHIP):
                slab = acc[:, j * Q:(j + 1) * Q]
                gw_ref[j] = slab
                gw16_ref[j] = slab.astype(BF16)

    row = _bs((tb, D), lambda i: (i, 0))
    vec = _bs((1, D), lambda i: (0, 0))
    slabs = _bs((N_CHIP, E, Q), lambda i: (0, 0, 0))
    return _call(body, grid=(nb,),
                 in_specs=[row, _bs((tb, E), lambda i: (i, 0)), _bs((E, D), lambda i: (0, 0)), row, row, vec],
                 out_specs=[row, row, vec, _bs((1, LANES), lambda i: (0, 0)), slabs, slabs],
                 out_shape=[jax.ShapeDtypeStruct((T, D), F32), jax.ShapeDtypeStruct((T, D), BF16),
                            jax.ShapeDtypeStruct((1, D), F32), jax.ShapeDtypeStruct((1, LANES), F32),
                            jax.ShapeDtypeStruct((N_CHIP, E, Q), F32), jax.ShapeDtypeStruct((N_CHIP, E, Q), BF16)],
                 scratch_shapes=[pltpu.VMEM((E, D), F32)],
                 compiler_params=_cp(("arbitrary",)), name="final_fb")(h1, pe, wple, z, tgt, g2)


def _pool_inv_count(t0, rows, pg, ngroups):
    t = t0 + lax.broadcasted_iota(jnp.int32, (rows, pg), 0)
    parts = []
    for w in POOL_WINDOWS[:ngroups]:
        parts.append(jnp.where(t + 1 >= w, 1.0 / w, 1.0 / (t + 1).astype(F32)))
    return parts


def _pool_fwd(proj, wp, P, tb):
    T = proj.shape[0]
    ng = len(POOL_WINDOWS)
    pg = P // ng
    hb = tb // POOL_HALO

    def body(v_ref, tail_ref, w_ref, o_ref, mx_ref, ext):
        i = pl.program_id(0)
        ext[pl.ds(0, POOL_HALO), :] = jnp.where(i > 0, tail_ref[...], 0.0)
        ext[pl.ds(POOL_HALO, tb), :] = v_ref[...]
        inv = _pool_inv_count(i * tb, tb, pg, ng)
        for g, w in enumerate(POOL_WINDOWS):
            cols = pl.ds(g * pg, pg)
            win = ext[pl.ds(POOL_HALO, tb), cols]
            for k in range(1, w):
                win = win + ext[pl.ds(POOL_HALO - k, tb), cols]
            pooled = (win * inv[g] - ext[pl.ds(POOL_HALO, tb), cols]).astype(BF16)
            o_ref[:, cols] = pooled
            mx_ref[:, cols] = jnp.dot(pooled, w_ref[g], preferred_element_type=F32)

    row = _bs((tb, P), lambda i: (i, 0))
    return _call(body, grid=(T // tb,),
                 in_specs=[row, _bs((POOL_HALO, P), lambda i: (jnp.maximum(i * hb - 1, 0), 0)),
                           _bs(wp.shape, lambda i: (0, 0, 0))],
                 out_specs=[row, row],
                 out_shape=[jax.ShapeDtypeStruct((T, P), BF16), jax.ShapeDtypeStruct((T, P), F32)],
                 scratch_shapes=[pltpu.VMEM((tb + POOL_HALO, P), F32)],
                 compiler_params=_cp(("arbitrary",)), name="pool_fwd")(proj, proj, wp)


def _pool_bwd(dmixed, wp, tb, comm=None):
    T, P = dmixed.shape
    ng = len(POOL_WINDOWS)
    pg = P // ng
    hb = tb // POOL_HALO
    nb = T // tb

    def core(d_ref, head_ref, w_ref, o_ref, ext, dpl):
        i = pl.program_id(0)
        inv = _pool_inv_count(i * tb, tb, pg, ng)
        invh = _pool_inv_count((i + 1) * tb, POOL_HALO, pg, ng)
        for g in range(ng):
            cols = pl.ds(g * pg, pg)
            dp = lax.dot_general(d_ref[:, cols], w_ref[g], (NT, ((), ())), preferred_element_type=F32)
            dph = lax.dot_general(head_ref[:, cols], w_ref[g], (NT, ((), ())), preferred_element_type=F32)
            dpl[:, cols] = dp
            ext[pl.ds(0, tb), cols] = dp * inv[g]
            ext[pl.ds(tb, POOL_HALO), cols] = jnp.where(i < nb - 1, dph * invh[g], 0.0)
        for g, w in enumerate(POOL_WINDOWS):
            cols = pl.ds(g * pg, pg)
            acc = ext[pl.ds(0, tb), cols]
            for k in range(1, w):
                acc = acc + ext[pl.ds(k, tb), cols]
            o_ref[:, cols] = (acc - dpl[:, cols]).astype(BF16)

    body, extra = _hosted(core, comm, (nb,), 3, 1, 2)
    return _call(body, grid=(nb,),
                 in_specs=[_bs((tb, P), lambda i: (i, 0)),
                           _bs((POOL_HALO, P), lambda i: (jnp.minimum((i + 1) * hb, T // POOL_HALO - 1), 0)),
                           _bs(wp.shape, lambda i: (0, 0, 0))] + extra["in_specs"],
                 out_specs=[_bs((tb, P), lambda i: (i, 0))] + extra["out_specs"],
                 out_shape=[jax.ShapeDtypeStruct((T, P), BF16)] + extra["out_shape"],
                 scratch_shapes=[pltpu.VMEM((tb + POOL_HALO, P), F32), pltpu.VMEM((tb, P), F32)] + extra["scratch"],
                 compiler_params=_cp(("arbitrary",)), name="pool_bwd")(dmixed, dmixed, wp, *extra["ins"])


def _zoh(a_re, a_im, ldt, b_re, b_im):
    lam_re = jnp.minimum(a_re, A_RE_MAX)
    lam_im = a_im
    dt = jnp.exp(ldt)
    mag = jnp.exp(lam_re * dt)
    ang = lam_im * dt
    ab_re = mag * jnp.cos(ang)
    ab_im = mag * jnp.sin(ang)
    den = lam_re * lam_re + lam_im * lam_im
    n_re = ab_re - 1.0
    n_im = ab_im
    q_re = (n_re * lam_re + n_im * lam_im) / den
    q_im = (n_im * lam_re - n_re * lam_im) / den
    return ab_re, ab_im, q_re * b_re - q_im * b_im, q_re * b_im + q_im * b_re


def _ssm_prep(a_re, a_im, ldt, bt_re, bt_im):
    shp = jax.ShapeDtypeStruct(a_re.shape, F32)

    def body(a, b, c, d, e, o0, o1, o2, o3):
        r = _zoh(a[...], b[...], c[...], d[...], e[...])
        o0[...], o1[...], o2[...], o3[...] = r

    return _call(body, in_specs=[VMEM_FULL] * 5, out_specs=[VMEM_FULL] * 4, out_shape=[shp] * 4,
                 name="ssm_prep")(a_re, a_im, ldt, bt_re, bt_im)


def _ssm_prep_bwd(a_re, a_im, ldt, bt_re, bt_im, dab_re, dab_im, dbb_re, dbb_im, G):
    GC, N = a_re.shape
    C = GC // G

    def body(a, b, c, d, e, g0, g1, g2, g3, da_re, da_im, dldt, db_re, db_im):
        _, vjp = jax.vjp(_zoh, a[...], b[...], c[...], d[...], e[...])
        ga_re, ga_im, gl, gb_re, gb_im = vjp((g0[...], g1[...], g2[...], g3[...]))
        da_re[...] = jnp.sum(ga_re.reshape(G, C, N), axis=1)
        da_im[...] = jnp.sum(ga_im.reshape(G, C, N), axis=1)
        dldt[...] = jnp.sum(jnp.sum(gl.reshape(G, C, N), axis=1), axis=1, keepdims=True)
        db_re[...] = gb_re
        db_im[...] = gb_im

    gn = jax.ShapeDtypeStruct((G, N), F32)
    full = jax.ShapeDtypeStruct((GC, N), F32)
    return _call(body, in_specs=[VMEM_FULL] * 9, out_specs=[VMEM_FULL] * 5,
                 out_shape=[gn, gn, jax.ShapeDtypeStruct((G, 1), F32), full, full],
                 name="ssm_prep_bwd")(a_re, a_im, ldt, bt_re, bt_im, dab_re, dab_im, dbb_re, dbb_im)


def _coef_tiles(abr, abi, reverse):
    ns = abr.shape[1]
    row = lax.broadcasted_iota(jnp.int32, (SUBLANES, ns), 0)
    ar = jnp.broadcast_to(abr, (SUBLANES, ns))
    ai = jnp.broadcast_to(-abi if reverse else abi, (SUBLANES, ns))
    a2r, a2i = ar * ar - ai * ai, 2.0 * ar * ai
    a4r, a4i = a2r * a2r - a2i * a2i, 2.0 * a2r * a2i
    out = []
    for d, (vr, vi) in ((1, (ar, ai)), (2, (a2r, a2i)), (4, (a4r, a4i))):
        keep = (row < SUBLANES - d) if reverse else (row >= d)
        out += [jnp.where(keep, vr, 0.0), jnp.where(keep, vi, 0.0)]
    pr, pi = ar, ai
    for k in range(1, SUBLANES):
        sel = (row <= SUBLANES - 1 - k) if reverse else (row >= k)
        nr, ni = pr * ar - pi * ai, pr * ai + pi * ar
        pr, pi = jnp.where(sel, nr, pr), jnp.where(sel, ni, pi)
    return out + [pr, pi]


def _cpow(ar, ai, n):
    out, br, bi = None, ar, ai
    while n:
        if n & 1:
            out = (br, bi) if out is None else (out[0] * br - out[1] * bi, out[0] * bi + out[1] * br)
        br, bi = br * br - bi * bi, 2.0 * br * bi
        n >>= 1
    return out


def _seg_perm_matrix(nrows):
    r = jnp.arange(nrows)
    src = (nrows // SUBLANES) * (r % SUBLANES) + r // SUBLANES
    return (src[:, None] == jnp.arange(nrows)[None, :]).astype(BF16)


def _seg_order_rows(pm, xb):
    return jnp.dot(pm, xb, preferred_element_type=F32).astype(BF16)


def _time_order_rows(pm, x, terms):
    out, rest = None, x
    for t in range(terms):
        piece = rest.astype(BF16)
        part = lax.dot_general(pm, piece, (TN, ((), ())), preferred_element_type=F32)
        out = part if out is None else out + part
        if t + 1 < terms:
            rest = rest - piece.astype(F32)
    return out


def _scan_tiles(abr, abi, seg, reverse):
    ns = abr.shape[1]
    seg_pow = _cpow(abr, abi, seg)
    step = [jnp.broadcast_to(abr, (SUBLANES, ns)), jnp.broadcast_to(-abi if reverse else abi, (SUBLANES, ns))]
    return _coef_tiles(seg_pow[0], seg_pow[1], reverse) + step


def _seg_scan(xr_ref, xi_ref, coef_ref, car_ref, cai_ref, *, nrows, ns, reverse, cmat=None, dab=None):
    seg = nrows // SUBLANES
    cw = min(SCAN_LANES, ns)
    row = lax.broadcasted_iota(jnp.int32, (SUBLANES, cw), 0)
    first, last = (SUBLANES - 1, 0) if reverse else (0, SUBLANES - 1)

    def tile(i):
        return pl.ds(pl.multiple_of(((seg - 1 - i) if reverse else i) * SUBLANES, SUBLANES), SUBLANES)

    for cc in range(ns // cw):
        cols = pl.ds(cc * cw, cw)
        ar, ai = coef_ref[8, :, cols], coef_ref[9, :, cols]

        def local(i, x, cols=cols, ar=ar, ai=ai):
            rows = tile(i)
            nr = ar * x[0] - ai * x[1] + xr_ref[rows, cols]
            ni = ar * x[1] + ai * x[0] + xi_ref[rows, cols]
            xr_ref[rows, cols] = nr
            xi_ref[rows, cols] = ni
            return nr, ni

        zero = jnp.zeros((SUBLANES, cw), F32)
        er, ei = lax.fori_loop(0, seg, local, (zero, zero))

        co = [coef_ref[k, :, cols] for k in range(8)]
        for lvl, d in enumerate((1, 2, 4)):
            kr, ki = co[2 * lvl], co[2 * lvl + 1]
            sh = SUBLANES - d if reverse else d
            sr, si = pltpu.roll(er, sh, 0), pltpu.roll(ei, sh, 0)
            er, ei = er + (kr * sr - ki * si), ei + (kr * si + ki * sr)
        c0r, c0i = car_ref[:, cols], cai_ref[:, cols]
        er, ei = er + (co[6] * c0r - co[7] * c0i), ei + (co[6] * c0i + co[7] * c0r)
        nb_shift = SUBLANES - 1 if reverse else 1
        cmr = jnp.where(row == first, c0r, pltpu.roll(er, nb_shift, 0))
        cmi = jnp.where(row == first, c0i, pltpu.roll(ei, nb_shift, 0))
        car_ref[:, cols] = jnp.broadcast_to(er[last:last + 1, :], er.shape)
        cai_ref[:, cols] = jnp.broadcast_to(ei[last:last + 1, :], ei.shape)
        if cmat is not None:
            cmat[0][:, cols] = cmr
            cmat[1][:, cols] = cmi

        w0 = (ar * cmr - ai * cmi, ar * cmi + ai * cmr)
        if dab is None:
            def fix(i, w, cols=cols, ar=ar, ai=ai):
                rows = tile(i)
                xr_ref[rows, cols] = xr_ref[rows, cols] + w[0]
                xi_ref[rows, cols] = xi_ref[rows, cols] + w[1]
                return ar * w[0] - ai * w[1], ar * w[1] + ai * w[0]

            lax.fori_loop(0, seg, fix, w0)
        else:
            s_re, s_im, e_re, e_im, o_re, o_im = dab

            def add(rows, w, pr, pi, acc):
                gr = xr_ref[rows, cols] + w[0]
                gi = xi_ref[rows, cols] + w[1]
                xr_ref[rows, cols] = gr
                xi_ref[rows, cols] = gi
                return acc[0] + (gr * pr + gi * pi), acc[1] + (gi * pr - gr * pi)

            def fix(i, st, cols=cols, ar=ar, ai=ai):
                w, acc = st[:2], st[2:]
                rows = tile(i)
                before = pl.ds(pl.multiple_of((seg - 2 - i) * SUBLANES, SUBLANES), SUBLANES)
                acc = add(rows, w, s_re[before, cols], s_im[before, cols], acc)
                return (ar * w[0] - ai * w[1], ar * w[1] + ai * w[0]) + acc

            st = lax.fori_loop(0, seg - 1, fix, w0 + (zero, zero))
            acc = add(pl.ds(0, SUBLANES), st[:2], e_re[:, cols], e_im[:, cols], st[2:])
            o_re[:, cols] += jnp.sum(acc[0], axis=0, keepdims=True)
            o_im[:, cols] += jnp.sum(acc[1], axis=0, keepdims=True)


def _hosted(core, comm, grid, n_in, n_out, n_scratch):
    ci = len(comm["ins"]) if comm else 0
    co = len(comm["out_shape"]) if comm else 0

    def body(*refs):
        ins, rest = refs[:n_in + ci], refs[n_in + ci:]
        outs, scr = rest[:n_out + co], rest[n_out + co:]
        hooks = functools.partial(_comm_hooks, comm, grid, ins[n_in:], outs[n_out:], scr[n_scratch:])
        hooks(before=True)
        core(*ins[:n_in], *outs[:n_out], *scr[:n_scratch])
        hooks(before=False)

    aliases = {n_in + i: n_out + i for i in range(co)} if comm and comm.get("alias") else {}
    extra = dict(ins=list(comm["ins"]) if comm else [], in_specs=[ANY] * ci, out_specs=[ANY] * co,
                 out_shape=list(comm["out_shape"]) if comm else [], scratch=list(comm["scratch"]) if comm else [],
                 aliases=aliases)
    return body, extra


def _ssm_fwd(proj, bdr, bdi, cdr, cdi, abr, abi, dsk, P, tb, comm=None):
    T = proj.shape[0]
    ntl, ct, st = bdr.shape
    ns = ntl * st
    nb = T // tb

    def core(u_ref, bdr_ref, bdi_ref, cdr_ref, cdi_ref, abr_ref, abi_ref, d_ref, pm_ref,
             y_ref, ge_ref, bsr_ref, bsi_ref, sr, si, coef, car, cai, up):
        @pl.when(pl.program_id(0) == 0)
        def _():
            for k, tile in enumerate(_scan_tiles(abr_ref[...], abi_ref[...], tb // SUBLANES, False)):
                coef[k] = tile
            car[...] = jnp.zeros_like(car)
            cai[...] = jnp.zeros_like(cai)

        bsr_ref[...] = car[...]
        bsi_ref[...] = cai[...]
        u = u_ref[...]
        ub = _seg_order_rows(pm_ref[...], u.astype(BF16))
        for s in range(ntl):
            us = ub[:, s * ct:(s + 1) * ct]
            sr[:, s * st:(s + 1) * st] = jnp.dot(us, bdr_ref[s], preferred_element_type=F32)
            si[:, s * st:(s + 1) * st] = jnp.dot(us, bdi_ref[s], preferred_element_type=F32)
        _seg_scan(sr, si, coef, car, cai, nrows=tb, ns=ns, reverse=False)
        for s in range(ntl):
            s_re = sr[:, s * st:(s + 1) * st].astype(BF16)
            s_im = si[:, s * st:(s + 1) * st].astype(BF16)
            up[:, s * ct:(s + 1) * ct] = (jnp.dot(s_re, cdr_ref[s], preferred_element_type=F32)
                                          - jnp.dot(s_im, cdi_ref[s], preferred_element_type=F32))
        y = _time_order_rows(pm_ref[...], up[...], 3) + d_ref[...] * u
        y_ref[...] = y
        ge_ref[...] = _gelu(y).astype(BF16)

    full3 = lambda a: _bs(a.shape, lambda i: (0, 0, 0))
    vec = lambda n: _bs((1, n), lambda i: (0, 0))
    row = _bs((tb, P), lambda i: (i, 0))
    st_spec = _bs((None, SUBLANES, ns), lambda i: (i, 0, 0))
    body, extra = _hosted(core, comm, (nb,), 9, 4, 6)
    return _call(body, grid=(nb,),
                 in_specs=[_bs((tb, P), lambda i: (i, 2)), full3(bdr), full3(bdi), full3(cdr), full3(cdi),
                           vec(ns), vec(ns), vec(P), _bs((tb, tb), lambda i: (0, 0))] + extra["in_specs"],
                 out_specs=[row, row, st_spec, st_spec] + extra["out_specs"],
                 out_shape=[jax.ShapeDtypeStruct((T, P), F32), jax.ShapeDtypeStruct((T, P), BF16),
                            jax.ShapeDtypeStruct((nb, SUBLANES, ns), F32),
                            jax.ShapeDtypeStruct((nb, SUBLANES, ns), F32)] + extra["out_shape"],
                 scratch_shapes=[pltpu.VMEM((tb, ns), F32), pltpu.VMEM((tb, ns), F32),
                                 pltpu.VMEM((10, SUBLANES, ns), F32),
                                 pltpu.VMEM((SUBLANES, ns), F32), pltpu.VMEM((SUBLANES, ns), F32),
                                 pltpu.VMEM((tb, P), F32)] + extra["scratch"],
                 compiler_params=_cp(("arbitrary",)), name="ssm_fwd")(
                     proj, bdr, bdi, cdr, cdi, abr, abi, dsk, _seg_perm_matrix(tb), *extra["ins"])


def _ssm_bwd(proj, y, dge, bsr, bsi, bdr, bdi, cdr, cdi, abr, abi, dsk, dpi, dpg, dsg, P, tb, comm=None):
    T = proj.shape[0]
    ntl, ct, st = bdr.shape
    ns = ntl * st
    nb = T // tb

    def core(u_ref, y_ref, dge_ref, bsr_ref, bsi_ref, abr_ref, abi_ref, d_ref, pm_ref, dpi_ref, dpg_ref, dsg_ref,
             bdr_h, bdi_h, cdr_h, cdi_h,
             dproj_ref, dabr_ref, dabi_ref, dd_ref, dbdr_h, dbdi_h, dcdr_h, dcdi_h,
             wbdr, wbdi, wcdr, wcdi, abdr, abdi, acdr, acdi, spr, spi, gr, gi, coef_f, coef_r,
             car, cai, gcr, gci, ser, sei, dup):
        i = pl.program_id(0)

        @pl.when(i == 0)
        def _():
            for h, w in ((bdr_h, wbdr), (bdi_h, wbdi), (cdr_h, wcdr), (cdi_h, wcdi)):
                pltpu.sync_copy(h, w)
            for a in (abdr, abdi, acdr, acdi, gcr, gci):
                a[...] = jnp.zeros_like(a)
            for o in (dabr_ref, dabi_ref, dd_ref):
                o[...] = jnp.zeros_like(o)
            for k, tile in enumerate(_scan_tiles(abr_ref[...], abi_ref[...], tb // SUBLANES, False)):
                coef_f[k] = tile
            for k, tile in enumerate(_scan_tiles(abr_ref[...], abi_ref[...], tb // SUBLANES, True)):
                coef_r[k] = tile

        car[...] = bsr_ref[...]
        cai[...] = bsi_ref[...]
        u = u_ref[...]
        dy = dge_ref[...] * _gelu_grad(y_ref[...])
        ub = _seg_order_rows(pm_ref[...], u.astype(BF16))
        dyb = _seg_order_rows(pm_ref[...], dy.astype(BF16))
        for s in range(ntl):
            us = ub[:, s * ct:(s + 1) * ct]
            spr[:, s * st:(s + 1) * st] = jnp.dot(us, wbdr[s], preferred_element_type=F32)
            spi[:, s * st:(s + 1) * st] = jnp.dot(us, wbdi[s], preferred_element_type=F32)
        _seg_scan(spr, spi, coef_f, car, cai, nrows=tb, ns=ns, reverse=False, cmat=(ser, sei))

        for s in range(ntl):
            dys = dyb[:, s * ct:(s + 1) * ct]
            gr[:, s * st:(s + 1) * st] = lax.dot_general(dys, wcdr[s], (NT, ((), ())), preferred_element_type=F32)
            gi[:, s * st:(s + 1) * st] = -lax.dot_general(dys, wcdi[s], (NT, ((), ())), preferred_element_type=F32)
        _seg_scan(gr, gi, coef_r, gcr, gci, nrows=tb, ns=ns, reverse=True,
                  dab=(spr, spi, ser, sei, dabr_ref, dabi_ref))

        for s in range(ntl):
            sl_c, sl_s = slice(s * ct, (s + 1) * ct), slice(s * st, (s + 1) * st)
            s_re = spr[:, sl_s].astype(BF16)
            s_im = spi[:, sl_s].astype(BF16)
            g_re, g_im = gr[:, sl_s].astype(BF16), gi[:, sl_s].astype(BF16)
            dys, us = dyb[:, sl_c], ub[:, sl_c]
            acdr[s] += lax.dot_general(s_re, dys, (TN, ((), ())), preferred_element_type=F32)
            acdi[s] -= lax.dot_general(s_im, dys, (TN, ((), ())), preferred_element_type=F32)
            abdr[s] += lax.dot_general(us, g_re, (TN, ((), ())), preferred_element_type=F32)
            abdi[s] += lax.dot_general(us, g_im, (TN, ((), ())), preferred_element_type=F32)
            dup[:, sl_c] = (lax.dot_general(g_re, wbdr[s], (NT, ((), ())), preferred_element_type=F32)
                            + lax.dot_general(g_im, wbdi[s], (NT, ((), ())), preferred_element_type=F32))
        dd_ref[...] += jnp.sum(dy * u, axis=0, keepdims=True)
        du = _time_order_rows(pm_ref[...], dup[...], 2) + d_ref[...] * dy
        dproj_ref[:, 0:P] = dpi_ref[...]
        dproj_ref[:, P:2 * P] = dpg_ref[...]
        dproj_ref[:, 2 * P:3 * P] = du.astype(BF16)
        dproj_ref[:, 3 * P:4 * P] = dsg_ref[...]

        @pl.when(i == nb - 1)
        def _():
            for a, h in ((abdr, dbdr_h), (abdi, dbdi_h), (acdr, dcdr_h), (acdi, dcdi_h)):
                pltpu.sync_copy(a, h)

    rev = lambda i: nb - 1 - i
    vec = lambda n: _bs((1, n), lambda i: (0, 0))
    row = _bs((tb, P), lambda i: (rev(i), 0))
    st_spec = _bs((None, SUBLANES, ns), lambda i: (rev(i), 0, 0))
    bshape = jax.ShapeDtypeStruct(bdr.shape, F32)
    cshape = jax.ShapeDtypeStruct(cdr.shape, F32)
    body, extra = _hosted(core, comm, (nb,), 16, 8, 21)
    return _call(body, grid=(nb,),
                 in_specs=[_bs((tb, P), lambda i: (rev(i), 2)), row, row, st_spec, st_spec,
                           vec(ns), vec(ns), vec(P), _bs((tb, tb), lambda i: (0, 0)), row, row, row,
                           ANY, ANY, ANY, ANY] + extra["in_specs"],
                 out_specs=[_bs((tb, 4 * P), lambda i: (rev(i), 0)), vec(ns), vec(ns), vec(P), ANY, ANY, ANY, ANY]
                 + extra["out_specs"],
                 out_shape=[jax.ShapeDtypeStruct((T, 4 * P), BF16), jax.ShapeDtypeStruct((1, ns), F32),
                            jax.ShapeDtypeStruct((1, ns), F32), jax.ShapeDtypeStruct((1, P), F32),
                            bshape, bshape, cshape, cshape] + extra["out_shape"],
                 scratch_shapes=[pltpu.VMEM(bdr.shape, BF16), pltpu.VMEM(bdr.shape, BF16),
                                 pltpu.VMEM(cdr.shape, BF16), pltpu.VMEM(cdr.shape, BF16),
                                 pltpu.VMEM(bdr.shape, F32), pltpu.VMEM(bdr.shape, F32),
                                 pltpu.VMEM(cdr.shape, F32), pltpu.VMEM(cdr.shape, F32),
                                 pltpu.VMEM((tb, ns), F32), pltpu.VMEM((tb, ns), F32),
                                 pltpu.VMEM((tb, ns), F32), pltpu.VMEM((tb, ns), F32),
                                 pltpu.VMEM((10, SUBLANES, ns), F32), pltpu.VMEM((10, SUBLANES, ns), F32)]
                 + [pltpu.VMEM((SUBLANES, ns), F32)] * 6 + [pltpu.VMEM((tb, P), F32)] + extra["scratch"],
                 compiler_params=_cp(("arbitrary",)), name="ssm_bwd")(
                     proj, y, dge, bsr, bsi, abr, abi, dsk, _seg_perm_matrix(tb), dpi, dpg, dsg,
                     bdr, bdi, cdr, cdi, *extra["ins"])


def _adamw(w, g, m, v, name, comm=None):
    R, C = w.shape
    tr = _t(R, ROWS_ELEMENTWISE)

    def core(w_ref, g_ref, m_ref, v_ref, d_ref, mo_ref, vo_ref):
        gv = g_ref[...]
        mn = ADAM_B1 * m_ref[...] + (1.0 - ADAM_B1) * gv
        vn = ADAM_B2 * v_ref[...] + (1.0 - ADAM_B2) * (gv * gv)
        m_hat = mn / (1.0 - ADAM_B1 ** ADAM_STEP)
        v_hat = vn / (1.0 - ADAM_B2 ** ADAM_STEP)
        d_ref[...] = -ADAM_LR * (m_hat / (jnp.sqrt(v_hat) + ADAM_EPS) + ADAM_WD * w_ref[...])
        mo_ref[...] = mn
        vo_ref[...] = vn

    blk = _bs((tr, C), lambda i: (i, 0))
    shp = jax.ShapeDtypeStruct((R, C), F32)
    body, extra = _hosted(core, comm, (R // tr,), 4, 3, 0)
    return _call(body, grid=(R // tr,), in_specs=[blk] * 4 + extra["in_specs"],
                 out_specs=[blk] * 3 + extra["out_specs"], out_shape=[shp] * 3 + extra["out_shape"],
                 scratch_shapes=extra["scratch"],
                 compiler_params=_cp(("arbitrary",) if comm else ("parallel",)), name=name)(w, g, m, v, *extra["ins"])


def _sum_cast(grad, got, place, name):
    J, H, C = got.shape
    tr = _t(H, ROWS_ELEMENTWISE)
    nb = H // tr

    def body(pl_ref, a_ref, b_ref, o_ref):
        o_ref[...] = (a_ref[...] + b_ref[...]).astype(BF16)

    blk = _bs((None, tr, C), lambda j, i, pc: (j, i, 0))
    mine = _bs((None, tr, C), lambda j, i, pc: (j, pc[1] * nb + i, 0))
    spec = pltpu.PrefetchScalarGridSpec(num_scalar_prefetch=1, grid=(J, nb), in_specs=[mine, blk], out_specs=blk)
    return _call(body, grid_spec=spec, out_shape=jax.ShapeDtypeStruct((J, H, C), BF16),
                 compiler_params=_cp(("parallel", "parallel")), name=name)(place, grad, got)


def _sum_chips(sent, arrived, place, name):
    J, H, C = arrived.shape
    tr = _t(H, ROWS_ELEMENTWISE)
    nb = H // tr

    def body(pl_ref, own_ref, a0_ref, a1_ref, a2_ref, o_ref):
        acc = own_ref[...].astype(F32)
        for r in (a0_ref, a1_ref, a2_ref):
            acc = acc + r[...].astype(F32)
        o_ref[...] = acc

    def other(k):
        return _bs((None, tr, C), lambda i, pc: (jnp.where(pc[0] <= k, k + 1, k), i, 0))

    spec = pltpu.PrefetchScalarGridSpec(
        num_scalar_prefetch=1, grid=(nb,),
        in_specs=[_bs((None, tr, C), lambda i, pc: (pc[0], i, 0)), other(0), other(1), other(2)],
        out_specs=_bs((tr, C), lambda i, pc: (pc[1] * nb + i, 0)))
    return _call(body, grid_spec=spec, out_shape=jax.ShapeDtypeStruct((2 * H, C), F32),
                 compiler_params=_cp(("parallel",)), name=name)(place, sent, arrived, arrived, arrived)


def _place():
    x, y, c = lax.axis_index("x"), lax.axis_index("y"), lax.axis_index("c")
    chips = [(1 - x, y), (x, 1 - y), (1 - x, 1 - y)]
    return x, y, c, chips


def _split(nrows, row_bytes, align, cap=None):
    k = max(1, min(cap or DMA_MAX_CHUNKS, (nrows * row_bytes) // DMA_CHUNK_BYTES))
    while k > 1 and nrows % (k * align):
        k -= 1
    return k


def _comm_call(plan, name):
    n_in, n_out = len(plan["ins"]), len(plan["out_shape"])

    def body(*refs):
        for phase in plan["phases"]:
            phase(refs[:n_in], refs[n_in:n_in + n_out], refs[n_in + n_out:])

    return _call(body, in_specs=[ANY] * n_in, out_specs=[ANY] * n_out, out_shape=plan["out_shape"],
                 input_output_aliases={i: i for i in range(n_out)} if plan.get("alias") else {},
                 scratch_shapes=plan["scratch"], name=name)(*plan["ins"])


def _comm_hooks(plan, grid, ins, outs, sems, *, before):
    if plan is None:
        return
    nsteps, step = 1, 0
    for d, g in enumerate(grid):
        nsteps, step = nsteps * g, step * g + pl.program_id(d)
    for p, (phase, frac) in enumerate(zip(plan["phases"], plan["at"])):
        if (p == 0) == before:
            pl.when(step == int(frac * (nsteps - 1)))(functools.partial(phase, ins, outs, sems))


def _ag_plan(shards, axes):
    n = len(shards)
    shapes = [a.shape for a in shards]

    def window(ref, i, chip, half=None):
        S, ax = shapes[i], axes[i]
        idx = []
        for d in range(len(S)):
            off, size = 0, S[d]
            if d == 0 and half is not None:
                off, size = half * (S[0] // 2), S[0] // 2
            if d == ax:
                off = off + chip * S[ax]
            idx.append(pl.ds(off, size))
        return ref.at[tuple(idx)]

    def copies(src, full, sems):
        ssem, rsem = sems
        x, y, c, chips = _place()
        me = 2 * x + y
        sib = (x, y, 1 - c)
        idx = [2 * cx + cy for cx, cy in chips]

        def rcopy(i, k, s_ref, d_ref, to):
            return pltpu.make_async_remote_copy(src_ref=s_ref, dst_ref=d_ref, send_sem=ssem.at[i, k],
                                                recv_sem=rsem.at[i, k], device_id=to, device_id_type=MESH)

        def ici(i, j, incoming):
            half_src = src[i].at[pl.ds(c * (shapes[i][0] // 2), shapes[i][0] // 2)]
            return rcopy(i, j, half_src, window(full[i], i, idx[j] if incoming else me, c), (*chips[j], c))

        def fwd(i, j, half):
            w = window(full[i], i, idx[j], half)
            return rcopy(i, 3 + j, w, w, sib)

        def own(i):
            return rcopy(i, 6, src[i], window(full[i], i, me), sib)

        return c, ici, fwd, own

    def send(src, full, sems):
        c, ici, fwd, own = copies(src, full, sems)
        for i in range(n):
            for j in range(3):
                ici(i, j, False).start()
        for i in range(n):
            own(i).start()

    def forward(i, src, full, sems):
        c, ici, fwd, own = copies(src, full, sems)
        for j in range(3):
            ici(i, j, True).wait_recv()
            fwd(i, j, c).start()

    def finish(src, full, sems):
        c, ici, fwd, own = copies(src, full, sems)
        for i in range(n):
            for j in range(3):
                fwd(i, j, 1 - c).wait_recv()
            own(i).wait()
        for i in range(n):
            for j in range(3):
                ici(i, j, False).wait_send()
                fwd(i, j, c).wait_send()

    out_shape = [jax.ShapeDtypeStruct(tuple(N_CHIP * d if k == ax else d for k, d in enumerate(S)), BF16)
                 for S, ax in zip(shapes, axes)]
    sizes = [a.size for a in shards]
    behind = [AG_FORWARD_SCALE * sum(sizes[:i + 1]) / sum(sizes) + AG_FORWARD_LAG for i in range(n)]
    return dict(ins=list(shards), out_shape=out_shape,
                phases=[send] + [functools.partial(forward, i) for i in range(n)] + [finish],
                at=[0.0] + behind + [1.0],
                scratch=[pltpu.SemaphoreType.DMA((n, 7)), pltpu.SemaphoreType.DMA((n, 7))])


def _proj_ag(x, g1, wsh, order, tm):
    T, D = x.shape
    P = wsh.shape[1]
    H = D // 2
    nt = T // tm

    def body(order_ref, x_ref, g_ref, wsh_ref, hn_ref, proj_ref, win_ref, wbuf, lsem, ssem, rsem):
        n, i = pl.program_id(0), pl.program_id(1)
        x, y, c, chips = _place()
        me = 2 * x + y
        sib = (x, y, 1 - c)
        idx = [2 * cx + cy for cx, cy in chips]

        def rcopy(k, s_ref, d_ref, to):
            return pltpu.make_async_remote_copy(src_ref=s_ref, dst_ref=d_ref, send_sem=ssem.at[k],
                                                recv_sem=rsem.at[k], device_id=to, device_id_type=MESH)

        def cols(chip):
            return pl.ds(pl.multiple_of(chip * P, LANES), P)

        def rows(half):
            return pl.ds(pl.multiple_of(half * H, BF16_TILE_ROWS), H)

        def ici(j, incoming):
            return rcopy(j, wsh_ref.at[rows(c)], win_ref.at[rows(c), cols(idx[j] if incoming else me)],
                         (*chips[j], c))

        def fwd(j, half):
            w = win_ref.at[rows(half), cols(idx[j])]
            return rcopy(3 + j, w, w, sib)

        def own():
            return rcopy(6, wsh_ref, win_ref.at[:, cols(me)], sib)

        def load(src):
            cp = pltpu.make_async_copy(src, wbuf, lsem)
            cp.start()
            cp.wait()

        @pl.when((n == 0) & (i == 0))
        def _():
            ici(0, False).start()
            ici(1, False).start()
            own().start()
            load(wsh_ref)

        for j in range(3):
            @pl.when((n == j + 1) & (i == 0))
            def _(j=j):
                if j == 0:
                    ici(2, False).start()
                ici(j, True).wait_recv()
                fwd(j, c).start()
                fwd(j, 1 - c).wait_recv()
                load(win_ref.at[:, cols(idx[j])])

        xv = x_ref[...]
        r = lax.rsqrt(jnp.mean(xv * xv, axis=-1, keepdims=True) + EPS)
        hn = ((xv * r) * g_ref[...]).astype(BF16)

        @pl.when(n == 0)
        def _():
            hn_ref[...] = hn

        proj_ref[...] = jnp.dot(hn, wbuf[...], preferred_element_type=F32)

        @pl.when((n == 3) & (i == nt - 1))
        def _():
            own().wait()
            for j in range(3):
                ici(j, False).wait_send()
                fwd(j, c).wait_send()

    spec = pltpu.PrefetchScalarGridSpec(
        num_scalar_prefetch=1, grid=(N_CHIP, nt),
        in_specs=[_bs((tm, D), lambda n, i, o: (i, 0)), _bs((1, D), lambda n, i, o: (0, 0)), ANY],
        out_specs=[_bs((tm, D), lambda n, i, o: (jnp.where(n == 0, i, nt - 1), 0)),
                   _bs((tm, P), lambda n, i, o: (i, o[n])), ANY],
        scratch_shapes=[pltpu.VMEM((D, P), BF16), pltpu.SemaphoreType.DMA,
                        pltpu.SemaphoreType.DMA((7,)), pltpu.SemaphoreType.DMA((7,))])
    return _call(body, grid_spec=spec,
                 out_shape=[jax.ShapeDtypeStruct((T, D), BF16), jax.ShapeDtypeStruct((T, N_CHIP * P), F32),
                            jax.ShapeDtypeStruct((D, N_CHIP * P), BF16)],
                 compiler_params=_cp(("arbitrary", "arbitrary")), name="proj_ag")(order, x, g1, wsh)


def _halves_plan(grads):
    n = len(grads)

    def send(g, got, sems):
        ssem, rsem = sems
        x, y, c, _ = _place()
        sib = (x, y, 1 - c)
        for i in range(n):
            J, R, C = g[i].shape
            H = R // 2
            size = g[i].dtype.itemsize
            tile_rows = SUBLANES * 4 // size
            k = _split(H, C * size, tile_rows, cap=DMA_MAX_CHUNKS // J)
            hr = H // k
            for j in range(J):
                for q in range(k):
                    other = pl.ds(pl.multiple_of((1 - c) * H + q * hr, tile_rows), hr)
                    to = pl.ds(q * hr, hr)
                    pltpu.make_async_remote_copy(src_ref=g[i].at[j, other, :], dst_ref=got[i].at[j, to, :],
                                                 send_sem=ssem.at[i], recv_sem=rsem.at[i],
                                                 device_id=sib, device_id_type=MESH).start()

    def finish(g, got, sems):
        ssem, rsem = sems
        x, y, c, _ = _place()
        for i in range(n):
            pltpu.make_async_remote_copy(src_ref=got[i], dst_ref=got[i], send_sem=ssem.at[i], recv_sem=rsem.at[i],
                                         device_id=(x, y, 1 - c), device_id_type=MESH).wait()

    half = [jax.ShapeDtypeStruct((a.shape[0], a.shape[1] // 2, a.shape[2]), a.dtype) for a in grads]
    return dict(ins=list(grads), out_shape=half, phases=[send, finish], at=[0.0, 1.0],
                scratch=[pltpu.SemaphoreType.DMA((n,)), pltpu.SemaphoreType.DMA((n,))])


def _scatter_plan(parts):
    n = len(parts)

    def peers():
        x, y, c, chips = _place()
        return 2 * x + y, c, chips, [2 * cx + cy for cx, cy in chips]

    def send(s, got, sems):
        ssem, rsem = sems
        me, c, chips, idx = peers()
        for i in range(n):
            _, H, C = s[i].shape
            k = _split(H, C * 2, BF16_TILE_ROWS, cap=RS_CHUNKS)
            hr = H // k
            for q in range(k):
                rows = pl.ds(q * hr, hr)
                for j in range(3):
                    pltpu.make_async_remote_copy(src_ref=s[i].at[idx[j], rows, :], dst_ref=got[i].at[me, rows, :],
                                                 send_sem=ssem.at[i, j], recv_sem=rsem.at[i, j],
                                                 device_id=(*chips[j], c), device_id_type=MESH).start()

    def finish(s, got, sems):
        ssem, rsem = sems
        me, c, chips, idx = peers()
        for i in range(n):
            for j in range(3):
                pltpu.make_async_remote_copy(src_ref=s[i].at[idx[j]], dst_ref=got[i].at[idx[j]],
                                             send_sem=ssem.at[i, j], recv_sem=rsem.at[i, j],
                                             device_id=(*chips[j], c), device_id_type=MESH).wait()

    return dict(ins=list(parts), out_shape=[jax.ShapeDtypeStruct(a.shape, a.dtype) for a in parts],
                phases=[send, finish], at=[0.0, 1.0],
                scratch=[pltpu.SemaphoreType.DMA((n, 3)), pltpu.SemaphoreType.DMA((n, 3))])


def _join_plan(shards):
    n = len(shards)

    def send(_, full, sems):
        ssem, rsem = sems
        x, y, c, _ = _place()
        sib = (x, y, 1 - c)
        for i in range(n):
            H, C = full[i].shape[0] // 2, full[i].shape[1]
            k = _split(H, C * 4, SUBLANES)
            hr = H // k
            for q in range(k):
                rows = pl.ds(pl.multiple_of(c * H + q * hr, SUBLANES), hr)
                pltpu.make_async_remote_copy(src_ref=full[i].at[rows], dst_ref=full[i].at[rows],
                                             send_sem=ssem.at[i], recv_sem=rsem.at[i],
                                             device_id=sib, device_id_type=MESH).start()

    def finish(_, full, sems):
        ssem, rsem = sems
        x, y, c, _ = _place()
        for i in range(n):
            half = full[i].at[pl.ds(0, full[i].shape[0] // 2)]
            pltpu.make_async_remote_copy(src_ref=half, dst_ref=half, send_sem=ssem.at[i], recv_sem=rsem.at[i],
                                         device_id=(x, y, 1 - c), device_id_type=MESH).wait()

    return dict(ins=list(shards), out_shape=[jax.ShapeDtypeStruct(a.shape, a.dtype) for a in shards],
                phases=[send, finish], at=[0.0, 1.0], alias=True,
                scratch=[pltpu.SemaphoreType.DMA((n,)), pltpu.SemaphoreType.DMA((n,))])


def _allreduce_plan(buf):
    R, L = buf.shape
    RB = R // N_DEV

    def parts(sems):
        xv, got, ov, lsem, ssem, rsem = sems
        x, y, c, _ = _place()
        me = 4 * x + 2 * y + c

        def dev(k):
            return (k // 4, (k // 2) % 2, k % 2)

        def slab(k):
            return pl.ds(pl.multiple_of(k * RB, SUBLANES), RB)

        def first(d, to, landing):
            return pltpu.make_async_remote_copy(src_ref=xv.at[slab(to)], dst_ref=got.at[landing],
                                                send_sem=ssem.at[0, d], recv_sem=rsem.at[0, d],
                                                device_id=dev(to), device_id_type=MESH)

        def second(d, to, k):
            return pltpu.make_async_remote_copy(src_ref=ov.at[slab(k)], dst_ref=ov.at[slab(k)],
                                                send_sem=ssem.at[1, d], recv_sem=rsem.at[1, d],
                                                device_id=dev(to), device_id_type=MESH)

        return me, slab, first, second

    def scatter(ins, outs, sems):
        xv, lsem = sems[0], sems[3]
        me, slab, first, second = parts(sems)
        cp = pltpu.make_async_copy(ins[0], xv, lsem)
        cp.start()
        cp.wait()
        for d in range(1, N_DEV):
            first(d, (me + d) % N_DEV, me).start()

    def reduce(ins, outs, sems):
        xv, got, ov = sems[:3]
        me, slab, first, second = parts(sems)
        got[me] = xv[slab(me), :]
        for d in range(1, N_DEV):
            src = (me + N_DEV - d) % N_DEV
            first(d, src, src).wait_recv()
        acc = got[0]
        for k in range(1, N_DEV):
            acc = acc + got[k]
        ov[slab(me), :] = acc
        for d in range(1, N_DEV):
            second(d, (me + d) % N_DEV, me).start()

    def collect(ins, outs, sems):
        ov, lsem = sems[2], sems[3]
        me, slab, first, second = parts(sems)
        for d in range(1, N_DEV):
            src = (me + N_DEV - d) % N_DEV
            second(d, src, src).wait_recv()
        for d in range(1, N_DEV):
            peer = (me + d) % N_DEV
            first(d, peer, me).wait_send()
            second(d, peer, me).wait_send()
        cp = pltpu.make_async_copy(ov, outs[0], lsem)
        cp.start()
        cp.wait()

    return dict(ins=[buf], out_shape=[jax.ShapeDtypeStruct((R, L), F32)], phases=[scatter, reduce, collect],
                at=[0.0, 0.5, 1.0],
                scratch=[pltpu.VMEM((R, L), F32), pltpu.VMEM((N_DEV, RB, L), F32), pltpu.VMEM((R, L), F32),
                         pltpu.SemaphoreType.DMA, pltpu.SemaphoreType.DMA((2, N_DEV)),
                         pltpu.SemaphoreType.DMA((2, N_DEV))])


def _block_diag(t, gt):
    G, A, B = t.shape
    t4 = t.reshape(G // gt, gt, A, B)
    eye = jnp.eye(gt, dtype=t.dtype)
    return jnp.einsum('sgab,gh->sgahb', t4, eye).reshape(G // gt, gt * A, gt * B)


def _block_diag_extract(m, gt, A, B):
    S = m.shape[0]
    m5 = m.reshape(S, gt, A, gt, B)
    eye = jnp.eye(gt, dtype=m.dtype)
    return jnp.einsum('sgahb,gh->sgab', m5, eye).reshape(S * gt, A, B)


def _tile_rows(n):
    return -(-n // (SUBLANES * LANES)) * SUBLANES


def _pack_small(arrs, rows):
    parts = []
    for a in arrs:
        flat = a.reshape(-1).astype(F32)
        r = _tile_rows(flat.shape[0])
        parts.append(jnp.pad(flat, (0, r * LANES - flat.shape[0])).reshape(r, LANES))
    used = sum(p.shape[0] for p in parts)
    if rows > used:
        parts.append(jnp.zeros((rows - used, LANES), F32))
    return jnp.concatenate(parts)


def _unpack_small(buf, shapes):
    out, off = [], 0
    for s in shapes:
        n = 1
        for d in s:
            n *= d
        r = _tile_rows(n)
        piece = buf[off:off + r]
        out.append(piece.reshape(s) if n == r * LANES else piece.reshape(-1)[:n].reshape(s))
        off += r
    return out


def kernel(x, p, norm_gain, w_in, w_pool, pool_scale, a_re, a_im, log_dt, b_re, b_im, c_re, c_im, d_skip, w_glu, w_out, w_ple, w_ple_gate, final_gain, loss_target, m_norm_gain, m_w_in, m_w_pool, m_pool_scale, m_a_re, m_a_im, m_log_dt, m_b_re, m_b_im, m_c_re, m_c_im, m_d_skip, m_w_glu, m_w_out, m_w_ple, m_w_ple_gate, m_final_gain, v_norm_gain, v_w_in, v_w_pool, v_pool_scale, v_a_re, v_a_im, v_log_dt, v_b_re, v_b_im, v_c_re, v_c_im, v_d_skip, v_w_glu, v_w_out, v_w_ple, v_w_ple_gate, v_final_gain):
    xs, pe, tgt = x[0], p[0, 0], loss_target[0]
    T, D = xs.shape
    E = pe.shape[1]
    P = D // 2
    NG = len(POOL_WINDOWS)
    PG = P // NG
    G, N, C = P // SSM_GROUP, SSM_STATE, SSM_GROUP
    GT = min(SSM_TILE_GROUPS, G)
    Q = D // N_CHIP

    big = {"w_in": (w_in, m_w_in, v_w_in), "w_pool": (w_pool, m_w_pool, v_w_pool),
           "w_glu": (w_glu, m_w_glu, v_w_glu), "w_out": (w_out, m_w_out, v_w_out),
           "w_ple": (w_ple, m_w_ple, v_w_ple), "w_ple_gate": (w_ple_gate, m_w_ple_gate, v_w_ple_gate)}
    big_names = list(big)
    shard2d = {n: (big[n][0].size // big[n][0].shape[-1], big[n][0].shape[-1]) for n in big_names}
    shard_axis = {"w_in": 1, "w_pool": 1, "w_glu": 1, "w_out": 0, "w_ple": 1, "w_ple_gate": 0}
    shard16 = {n: big[n][0][0].astype(BF16) for n in big_names}
    place = jnp.stack([2 * lax.axis_index("x") + lax.axis_index("y"), lax.axis_index("c")]).astype(jnp.int32)
    mx, my = lax.axis_index("x"), lax.axis_index("y")
    block_order = jnp.stack([2 * mx + my, 2 * (1 - mx) + my, 2 * mx + (1 - my),
                             2 * (1 - mx) + (1 - my)]).astype(jnp.int32)
    later = [n for n in big_names if n != "w_in"]
    ag_later = _ag_plan([shard16[n] for n in later], [shard_axis[n] for n in later])

    rep = lambda a: jnp.repeat(a, C, axis=0)
    a_re_r, a_im_r = rep(a_re[0]), rep(a_im[0])
    ldt_r = rep(jnp.broadcast_to(log_dt[0][:, None], (G, N)))
    bt_re = b_re[0].transpose(0, 2, 1).reshape(G * C, N)
    bt_im = b_im[0].transpose(0, 2, 1).reshape(G * C, N)
    ab_re_r, ab_im_r, bbt_re, bbt_im = _ssm_prep(a_re_r, a_im_r, ldt_r, bt_re, bt_im)
    abr = ab_re_r[::C].reshape(1, G * N)
    abi = ab_im_r[::C].reshape(1, G * N)
    bdr = _block_diag(bbt_re.reshape(G, C, N), GT).astype(BF16)
    bdi = _block_diag(bbt_im.reshape(G, C, N), GT).astype(BF16)
    cdr = _block_diag(c_re[0].transpose(0, 2, 1), GT).astype(BF16)
    cdi = _block_diag(c_im[0].transpose(0, 2, 1), GT).astype(BF16)

    tb = _t(T, ROWS_ELEMENTWISE)
    tbs = _t(T, ROWS_SSM)
    tm = _t(T, ROWS_MATMUL)
    tk = _t(T, DEPTH_MATMUL)
    DH = _t(D, ROWS_MATMUL)
    row_k = lambda i, n, k: (i, k)
    row_n = lambda i, n, k: (i, n)
    f32 = lambda *shape: jax.ShapeDtypeStruct(shape, F32)
    hn, proj, win = _proj_ag(xs, norm_gain, shard16["w_in"], block_order, tm)
    y, ge, bsr, bsi, wp, wglu, wout, wple, wpg = _ssm_fwd(proj, bdr, bdi, cdr, cdi, abr, abi, d_skip, P, tbs,
                                                          comm=ag_later)
    pooled, mixed = _pool_fwd(proj, wp, P, tb)
    hg = _mm(ge, wglu, dims=NN, grid=(T // tm, 1, 1),
             a_spec=_bs((tm, P), row_k), b_spec=_bs((P, 2 * P), lambda i, n, k: (k, n)),
             o_spec=_bs((tm, 2 * P), row_n), out_shape=f32(T, 2 * P), name="mm_glu")
    cat = _gate_fwd(mixed, proj, hg, pool_scale, tb)
    h1, h1b = _mm(cat, wout, dims=NN, grid=(T // tm, D // DH, 1), res=xs, bf16_copy=True,
                  a_spec=_bs((tm, D), row_k), b_spec=_bs((D, DH), lambda i, n, k: (k, n)),
                  r_spec=_bs((tm, DH), row_n), o_spec=_bs((tm, DH), row_n), out_shape=f32(T, D), name="mm_out")
    z = _mm(h1b, wpg, dims=NN, grid=(T // tm, 1, 1),
            a_spec=_bs((tm, D), row_k), b_spec=_bs((D, D), lambda i, n, k: (k, n)),
            o_spec=_bs((tm, D), row_n), out_shape=f32(T, D), name="mm_pgate")
    dh2, dz, dg2, lpart, g_wple, g_wple16 = _final_fb(h1, pe, wple, z, tgt, final_gain.reshape(1, D), tb)

    col_m = lambda m, n, k: (k, m)
    col_n = lambda m, n, k: (k, n)
    dh1, dh1b = _mm(dz, wpg, dims=NT, grid=(T // tm, D // DH, 1), res=dh2, bf16_copy=True,
                    a_spec=_bs((tm, D), row_k), b_spec=_bs((DH, D), lambda i, n, k: (n, k)),
                    r_spec=_bs((tm, DH), row_n), o_spec=_bs((tm, DH), row_n), out_shape=f32(T, D), name="mm_dh1")
    g_wpg, g_wpg16 = _mm(h1b, dz, dims=TN, grid=(D // DH, D // DH, T // tk), bf16_copy=True,
                         a_spec=_bs((tk, DH), col_m), b_spec=_bs((tk, DH), col_n),
                         o_spec=_bs((DH, DH), lambda m, n, k: (m, n)), out_shape=f32(D, D), name="mm_gwpg")
    dcat = _mm(dh1b, wout, dims=NT, grid=(T // tm, 1, 1),
               a_spec=_bs((tm, D), row_k), b_spec=_bs((D, D), lambda i, n, k: (n, k)),
               o_spec=_bs((tm, D), row_n), out_shape=f32(T, D), name="mm_dcat")
    g_wout, g_wout16 = _mm(cat, dh1b, dims=TN, grid=(D // DH, D // DH, T // tk), bf16_copy=True,
                           a_spec=_bs((tk, DH), col_m), b_spec=_bs((tk, DH), col_n),
                           o_spec=_bs((DH, DH), lambda m, n, k: (m, n)), out_shape=f32(D, D), name="mm_gwout")
    gbig = {"w_out": g_wout.reshape(N_CHIP, Q, D), "w_ple": g_wple, "w_ple_gate": g_wpg.reshape(N_CHIP, Q, D)}
    gbig16 = {"w_out": g_wout16.reshape(N_CHIP, Q, D), "w_ple": g_wple16,
              "w_ple_gate": g_wpg16.reshape(N_CHIP, Q, D)}
    first = list(gbig)
    res = _gate_bwd(dcat, mixed, proj, hg, pool_scale, tb, comm=_halves_plan([gbig16[n] for n in first]))
    dmixed, dpg, dsg, dhg, dps = res[:5]
    got = dict(zip(first, res[5:]))
    dge = _mm(dhg, wglu, dims=NT, grid=(T // tm, 1, 1),
              a_spec=_bs((tm, 2 * P), row_k), b_spec=_bs((P, 2 * P), lambda i, n, k: (n, k)),
              o_spec=_bs((tm, P), row_n), out_shape=f32(T, P), name="mm_dge")
    gbig["w_glu"], gbig16["w_glu"] = _mm(ge, dhg, dims=TN, grid=(1, N_CHIP, T // tk), bf16_copy=True,
                                         a_spec=_bs((tk, P), col_m), b_spec=_bs((tk, Q), col_n),
                                         o_spec=_bs((None, P, Q), lambda m, j, k: (j, 0, 0)),
                                         out_shape=f32(N_CHIP, P, Q), name="mm_gwglu")
    g_wp = _mm(pooled, dmixed, dims=TN, grid=(NG, 1, T // tk), bf16_copy=True,
               a_spec=_bs((tk, PG), col_m), b_spec=_bs((tk, PG), col_m),
               o_spec=_bs((None, PG, PG), lambda g, n, k: (g, 0, 0)), out_shape=f32(NG, PG, PG), name="mm_gwp")
    by_chip = lambda a: a.reshape(NG, N_CHIP, PG // N_CHIP, PG).transpose(1, 0, 2, 3).reshape(
        N_CHIP, NG * PG // N_CHIP, PG)
    gbig["w_pool"], gbig16["w_pool"] = by_chip(g_wp[0]), by_chip(g_wp[1])
    res = _pool_bwd(dmixed, wp, tb, comm=_halves_plan([gbig16["w_pool"], gbig16["w_glu"]]))
    dpi, got["w_pool"], got["w_glu"] = res
    early = list(gbig)
    chip_sums = {n: _sum_cast(gbig[n], got[n], place, "sum_cast_" + n) for n in early}
    res = _ssm_bwd(proj, y, dge, bsr, bsi, bdr, bdi, cdr, cdi, abr, abi, d_skip, dpi, dpg, dsg, P, tbs,
                   comm=_scatter_plan([chip_sums[n] for n in early]))
    dproj, dabr, dabi, dd, dbdr, dbdi, dcdr, dcdi = res[:8]
    arrived = dict(zip(early, res[8:]))
    halves = [_sum_chips(chip_sums[n], arrived[n], place, "sum_chips_" + n) for n in early]
    res = _mm(hn, dproj, dims=TN, grid=(D // DH, N_CHIP, T // tk), bf16_copy=True,
              a_spec=_bs((tk, DH), col_m), b_spec=_bs((tk, P), col_n),
              o_spec=_bs((None, DH, P), lambda m, j, k: (j, m, 0)), out_shape=f32(N_CHIP, D, P),
              name="mm_gwin", comm=_join_plan(halves))
    gbig["w_in"], gbig16["w_in"], gshard = res[0], res[1], dict(zip(early, res[2:]))
    got["w_in"], = _comm_call(_halves_plan([gbig16["w_in"]]), "rs_halves_late")
    chip_sums["w_in"] = _sum_cast(gbig["w_in"], got["w_in"], place, "sum_cast_w_in")
    KH = _t(4 * P, DEPTH_MATMUL)
    dhn, arrived["w_in"] = _mm(dproj, win, dims=NT, grid=(T // tm, D // DH, 4 * P // KH),
                               a_spec=_bs((tm, KH), row_k), b_spec=_bs((DH, KH), lambda i, n, k: (n, k)),
                               o_spec=_bs((tm, DH), row_n), out_shape=f32(T, D), name="mm_dhn",
                               comm=_scatter_plan([chip_sums["w_in"]]))
    gshard["w_in"], = _comm_call(
        _join_plan([_sum_chips(chip_sums["w_in"], arrived["w_in"], place, "sum_chips_w_in")]), "rs_join_w_in")
    grad_x, dg1 = _norm1_bwd(xs, dhn, dh1, norm_gain, tb)

    dbbt_re = _block_diag_extract(dbdr, GT, C, N).reshape(G * C, N)
    dbbt_im = _block_diag_extract(dbdi, GT, C, N).reshape(G * C, N)
    g_c_re = _block_diag_extract(dcdr, GT, N, C).transpose(0, 2, 1)
    g_c_im = _block_diag_extract(dcdi, GT, N, C).transpose(0, 2, 1)
    dab_re_r = rep(dabr.reshape(G, N)) * (1.0 / C)
    dab_im_r = rep(dabi.reshape(G, N)) * (1.0 / C)
    g_a_re, g_a_im, g_ldt, g_bt_re, g_bt_im = _ssm_prep_bwd(a_re_r, a_im_r, ldt_r, bt_re, bt_im,
                                                            dab_re_r, dab_im_r, dbbt_re, dbbt_im, G)
    g_b_re = g_bt_re.reshape(G, C, N).transpose(0, 2, 1)
    g_b_im = g_bt_im.reshape(G, C, N).transpose(0, 2, 1)


    small_names = ["norm_gain", "pool_scale", "a_re", "a_im", "log_dt", "b_re", "b_im", "c_re", "c_im",
                   "d_skip", "final_gain"]
    small_w = dict(norm_gain=norm_gain, pool_scale=pool_scale, a_re=a_re, a_im=a_im, log_dt=log_dt, b_re=b_re,
                   b_im=b_im, c_re=c_re, c_im=c_im, d_skip=d_skip, final_gain=final_gain)
    small_m = dict(norm_gain=m_norm_gain, pool_scale=m_pool_scale, a_re=m_a_re, a_im=m_a_im, log_dt=m_log_dt,
                   b_re=m_b_re, b_im=m_b_im, c_re=m_c_re, c_im=m_c_im, d_skip=m_d_skip, final_gain=m_final_gain)
    small_v = dict(norm_gain=v_norm_gain, pool_scale=v_pool_scale, a_re=v_a_re, a_im=v_a_im, log_dt=v_log_dt,
                   b_re=v_b_re, b_im=v_b_im, c_re=v_c_re, c_im=v_c_im, d_skip=v_d_skip, final_gain=v_final_gain)
    small_g = dict(norm_gain=dg1, pool_scale=dps, a_re=g_a_re, a_im=g_a_im, log_dt=g_ldt, b_re=g_b_re,
                   b_im=g_b_im, c_re=g_c_re, c_im=g_c_im, d_skip=dd, final_gain=dg2)
    shapes = [small_w[n].shape for n in small_names]
    loss_row = sum(_tile_rows(small_w[n].size) for n in small_names)
    unit = N_DEV * SUBLANES
    rows = -(-(loss_row + SUBLANES) // unit) * unit
    gbuf = _pack_small([small_g[n] for n in small_names] + [lpart[0, :1]], rows)
    gsum, = _comm_call(_allreduce_plan(gbuf), "allreduce_small")
    g_out, d_out, m_out, v_out = {}, {}, {}, {}
    for n in big_names:
        w_, m_, v_ = big[n]
        r2 = shard2d[n]
        res = _adamw(w_.reshape(r2), gshard[n], m_.reshape(r2), v_.reshape(r2), "adamw_" + n)
        g_out[n], d_out[n], m_out[n], v_out[n] = (a.reshape(w_.shape) for a in (gshard[n], *res))
    wbuf = _pack_small([small_w[n] for n in small_names], rows)
    mbuf = _pack_small([small_m[n] for n in small_names], rows)
    vbuf = _pack_small([small_v[n] for n in small_names], rows)
    dsm, msm, vsm = _adamw(wbuf, gsum, mbuf, vbuf, "adamw_small")
    g_small = dict(zip(small_names, _unpack_small(gsum, shapes)))
    d_small = dict(zip(small_names, _unpack_small(dsm, shapes)))
    m_small = dict(zip(small_names, _unpack_small(msm, shapes)))
    v_small = dict(zip(small_names, _unpack_small(vsm, shapes)))
    loss = gsum[loss_row, 0]

    g_out.update(g_small)
    d_out.update(d_small)
    m_out.update(m_small)
    v_out.update(v_small)

    order = ["norm_gain", "w_in", "w_pool", "pool_scale", "a_re", "a_im", "log_dt", "b_re", "b_im", "c_re",
             "c_im", "d_skip", "w_glu", "w_out", "w_ple", "w_ple_gate", "final_gain"]
    return (loss, grad_x[None], *[g_out[n] for n in order], *[d_out[n] for n in order],
            *[m_out[n] for n in order], *[v_out[n] for n in order])
```

```python
import functools

import jax
import jax.numpy as jnp
from jax import lax
from jax.experimental import pallas as pl
from jax.experimental.pallas import tpu as pltpu

F32, BF16 = jnp.float32, jnp.bfloat16
MESH = pl.DeviceIdType.MESH
ANY = pl.BlockSpec(memory_space=pl.ANY)
VMEM_FULL = pl.BlockSpec(memory_space=pltpu.VMEM)

EPS = 1e-6
A_RE_MAX = -1e-4
SSM_GROUP = 16
SSM_STATE = 64
POOL_WINDOWS = (2, 4, 8, 16)
POOL_HALO = 16
ADAM_LR, ADAM_B1, ADAM_B2, ADAM_EPS, ADAM_WD, ADAM_STEP = 0.001, 0.9, 0.999, 1e-08, 0.01, 10

V7X_VMEM_BYTES = 64 * 1024 * 1024
VMEM_LIMIT = V7X_VMEM_BYTES - 8 * 1024 * 1024
SUBLANES, LANES = 8, 128
BF16_TILE_ROWS = 16
SSM_TILE_GROUPS = 8
SCAN_LANES = 1024
N_DEV, N_CHIP = 8, 4
DMA_CHUNK_BYTES = 256 * 1024
DMA_MAX_CHUNKS = 32
RS_CHUNKS = 8
STREAM_BUFFERS = 3
ROWS_ELEMENTWISE = 256
ROWS_SSM = 256
ROWS_MATMUL = 1024
DEPTH_MATMUL = 4096
AG_FORWARD_SCALE, AG_FORWARD_LAG = 0.85, 0.05


def _t(n, pref):
    return pref if n % pref == 0 else n


def _cp(sem=None, vmem=VMEM_LIMIT):
    return pltpu.CompilerParams(dimension_semantics=sem, vmem_limit_bytes=vmem)


def _call(body, **kw):
    return pl.pallas_call(body, **kw)


NN = ((1,), (0,))
NT = ((1,), (1,))
TN = ((0,), (0,))


def _mm(a, b, *, dims, grid, a_spec, b_spec, o_spec, out_shape, name, res=None, r_spec=None, bf16_copy=False,
        comm=None):
    nk, kax = grid[-1], len(grid) - 1
    acc_shape = tuple(d for d in o_spec.block_shape if d is not None)

    def core(*refs):
        refs = list(refs)
        a_ref, b_ref = refs[:2]
        r_ref = refs[2] if res is not None else None
        outs = refs[3 if res is not None else 2:]
        o_ref = outs[0]
        o2_ref = outs[1] if bf16_copy else None
        acc = outs[-1] if nk > 1 else None

        def finish(r):
            if r_ref is not None:
                r = r + r_ref[...]
            o_ref[...] = r.astype(o_ref.dtype)
            if o2_ref is not None:
                o2_ref[...] = r.astype(BF16)

        part = lax.dot_general(a_ref[...].astype(BF16), b_ref[...].astype(BF16),
                               (dims, ((), ())), preferred_element_type=F32)
        if nk == 1:
            finish(part)
        else:
            k = pl.program_id(kax)

            @pl.when(k == 0)
            def _():
                acc[...] = part

            @pl.when(k > 0)
            def _():
                acc[...] += part

            @pl.when(k == nk - 1)
            def _():
                finish(acc[...])

    ins, specs = [a, b], [a_spec, b_spec]
    if res is not None:
        ins.append(res)
        specs.append(r_spec)
    o_specs, o_shapes = [o_spec], [out_shape]
    if bf16_copy:
        o_specs = [o_spec, o_spec]
        o_shapes = [out_shape, jax.ShapeDtypeStruct(out_shape.shape, BF16)]
    scratch = [pltpu.VMEM(acc_shape, F32)] if nk > 1 else []
    body, extra = _hosted(core, comm, grid, len(ins), len(o_specs), len(scratch))
    sem = ("arbitrary",) * len(grid) if comm else ("parallel",) * kax + ("arbitrary",)
    outs = _call(body, grid=grid, in_specs=specs + extra["in_specs"], out_specs=o_specs + extra["out_specs"],
                 out_shape=o_shapes + extra["out_shape"], scratch_shapes=scratch + extra["scratch"],
                 input_output_aliases=extra["aliases"],
                 compiler_params=_cp(sem), name=name)(*ins, *extra["ins"])
    return outs[0] if len(outs) == 1 else outs


def _bs(shape, fn):
    return pl.BlockSpec(shape, fn)


def _ring_fetch(hbm, ring, sems, step, nsteps, rows):
    n = len(hbm)

    def copy(k, s):
        slot = s % STREAM_BUFFERS
        return pltpu.make_async_copy(hbm[k].at[pl.ds(pl.multiple_of(s * rows, SUBLANES), rows)],
                                     ring[k].at[slot], sems.at[k, slot])

    @pl.when(step == 0)
    def _():
        for s in range(min(STREAM_BUFFERS - 1, nsteps)):
            for k in range(n):
                copy(k, s).start()

    @pl.when(step + STREAM_BUFFERS - 1 < nsteps)
    def _():
        for k in range(n):
            copy(k, step + STREAM_BUFFERS - 1).start()

    for k in range(n):
        copy(k, step).wait()
    return [ring[k].at[step % STREAM_BUFFERS] for k in range(n)]


def _sigmoid(v):
    return 1.0 / (1.0 + jnp.exp(-v))


def _gelu(v):
    return 0.5 * v * (1.0 + jnp.tanh(0.7978845608028654 * (v + 0.044715 * v * v * v)))


def _gelu_grad(v):
    t = jnp.tanh(0.7978845608028654 * (v + 0.044715 * v * v * v))
    return 0.5 * (1.0 + t) + 0.5 * v * (1.0 - t * t) * 0.7978845608028654 * (1.0 + 3 * 0.044715 * v * v)


def _norm1_bwd(x, dhn, dh1, g1, tb, comm=None):
    T, D = x.shape
    nb = T // tb

    def core(x_hbm, dhn_hbm, dh1_hbm, g_ref, dx_ref, dg_ref, rx, rdhn, rdh1, sems):
        @pl.when(pl.program_id(0) == 0)
        def _():
            dg_ref[...] = jnp.zeros_like(dg_ref)

        x_ref, dhn_ref, dh1_ref = _ring_fetch((x_hbm, dhn_hbm, dh1_hbm), (rx, rdhn, rdh1), sems,
                                              pl.program_id(0), nb, tb)
        xv = x_ref[...]
        r = lax.rsqrt(jnp.mean(xv * xv, axis=-1, keepdims=True) + EPS)
        xh = xv * r
        dhn_v = dhn_ref[...]
        dg_ref[...] += jnp.sum(dhn_v * xh, axis=0, keepdims=True)
        dxh = dhn_v * g_ref[...]
        dx_ref[...] = dh1_ref[...] + r * (dxh - xh * jnp.mean(dxh * xh, axis=-1, keepdims=True))

    row = _bs((tb, D), lambda i: (i, 0))
    vec = _bs((1, D), lambda i: (0, 0))
    body, extra = _hosted(core, comm, (nb,), 4, 2, 4)
    ring = pltpu.VMEM((STREAM_BUFFERS, tb, D), F32)
    return _call(body, grid=(nb,), in_specs=[ANY, ANY, ANY, vec] + extra["in_specs"],
                 out_specs=[row, vec] + extra["out_specs"],
                 out_shape=[jax.ShapeDtypeStruct((T, D), F32), jax.ShapeDtypeStruct((1, D), F32)] + extra["out_shape"],
                 scratch_shapes=[ring, ring, ring, pltpu.SemaphoreType.DMA((3, STREAM_BUFFERS))] + extra["scratch"],
                 input_output_aliases=extra["aliases"],
                 compiler_params=_cp(("arbitrary",)), name="norm1_bwd")(x, dhn, dh1, g1, *extra["ins"])


def _gate_fwd(mixed, proj, hg, ps, tb):
    T, P = mixed.shape

    def body(mx_ref, pg_ref, sg_ref, hg_ref, ps_ref, o_ref):
        pg, sg = pg_ref[...], sg_ref[...]
        ya = (mx_ref[...] * ps_ref[...]) * (pg * _sigmoid(pg))
        hgv = hg_ref[...]
        o = hgv[:, :P] * _sigmoid(hgv[:, P:])
        yb = o * (sg * _sigmoid(sg))
        o_ref[:, :P] = ya.astype(BF16)
        o_ref[:, P:] = yb.astype(BF16)

    return _call(body, grid=(T // tb,),
                 in_specs=[_bs((tb, P), lambda i: (i, 0)), _bs((tb, P), lambda i: (i, 1)),
                           _bs((tb, P), lambda i: (i, 3)), _bs((tb, 2 * P), lambda i: (i, 0)),
                           _bs((1, P), lambda i: (0, 0))],
                 out_specs=_bs((tb, 2 * P), lambda i: (i, 0)),
                 out_shape=jax.ShapeDtypeStruct((T, 2 * P), BF16),
                 compiler_params=_cp(("parallel",)), name="gate_fwd")(mixed, proj, proj, hg, ps)


def _gate_bwd(dcat, mixed, proj, hg, ps, tb, comm=None):
    T, P = mixed.shape

    def core(dc_ref, mx_ref, pg_ref, sg_ref, hg_ref, ps_ref, dmx_ref, dpg_ref, dsg_ref, dhg_ref, dps_ref):
        @pl.when(pl.program_id(0) == 0)
        def _():
            dps_ref[...] = jnp.zeros_like(dps_ref)

        dc = dc_ref[...]
        dya, dyb = dc[:, :P], dc[:, P:]
        pg, sg, mx, psv = pg_ref[...], sg_ref[...], mx_ref[...], ps_ref[...]
        s_pg = _sigmoid(pg)
        dpa = dya * (pg * s_pg)
        dpg_ref[...] = (dya * (mx * psv) * (s_pg * (1.0 + pg * (1.0 - s_pg)))).astype(BF16)
        dps_ref[...] += jnp.sum(dpa * mx, axis=0, keepdims=True)
        dmx_ref[...] = (dpa * psv).astype(BF16)
        hgv = hg_ref[...]
        h1, s_h2 = hgv[:, :P], _sigmoid(hgv[:, P:])
        s_sg = _sigmoid(sg)
        do = dyb * (sg * s_sg)
        dsg_ref[...] = (dyb * (h1 * s_h2) * (s_sg * (1.0 + sg * (1.0 - s_sg)))).astype(BF16)
        dhg_ref[:, :P] = (do * s_h2).astype(BF16)
        dhg_ref[:, P:] = (do * h1 * s_h2 * (1.0 - s_h2)).astype(BF16)

    rowp = _bs((tb, P), lambda i: (i, 0))
    row2 = _bs((tb, 2 * P), lambda i: (i, 0))
    vec = _bs((1, P), lambda i: (0, 0))
    body, extra = _hosted(core, comm, (T // tb,), 6, 5, 0)
    return _call(body, grid=(T // tb,),
                 in_specs=[_bs((tb, 2 * P), lambda i: (i, 0)), _bs((tb, P), lambda i: (i, 0)),
                           _bs((tb, P), lambda i: (i, 1)), _bs((tb, P), lambda i: (i, 3)),
                           _bs((tb, 2 * P), lambda i: (i, 0)), vec] + extra["in_specs"],
                 out_specs=[rowp, rowp, rowp, row2, vec] + extra["out_specs"],
                 out_shape=[jax.ShapeDtypeStruct((T, P), BF16), jax.ShapeDtypeStruct((T, P), BF16),
                            jax.ShapeDtypeStruct((T, P), BF16), jax.ShapeDtypeStruct((T, 2 * P), BF16),
                            jax.ShapeDtypeStruct((1, P), F32)] + extra["out_shape"],
                 scratch_shapes=extra["scratch"],
                 compiler_params=_cp(("arbitrary",)), name="gate_bwd")(dcat, mixed, proj, proj, hg, ps, *extra["ins"])


def _final_fb(h1, pe, wple, z, tgt, g2, tb):
    T, D = h1.shape
    E = pe.shape[1]
    Q = D // N_CHIP
    nb = T // tb

    def body(h1_ref, p_ref, w_ref, z_ref, t_ref, g_ref, dh2_ref, dz_ref, dg_ref, l_ref, gw_ref, gw16_ref, acc):
        @pl.when(pl.program_id(0) == 0)
        def _():
            dg_ref[...] = jnp.zeros_like(dg_ref)
            l_ref[...] = jnp.zeros_like(l_ref)
            acc[...] = jnp.zeros_like(acc)

        pb = p_ref[...].astype(BF16)
        ev = jnp.dot(pb, w_ref[...], preferred_element_type=F32)
        s = _sigmoid(z_ref[...])
        h2 = h1_ref[...] + ev * s
        r = lax.rsqrt(jnp.mean(h2 * h2, axis=-1, keepdims=True) + EPS)
        xh = h2 * r
        gv = g_ref[...]
        diff = xh * gv - t_ref[...]
        l_ref[...] += 0.5 * jnp.sum(jnp.mean(diff * diff, axis=-1, keepdims=True))
        dout = diff * (1.0 / D)
        dg_ref[...] += jnp.sum(dout * xh, axis=0, keepdims=True)
        dxh = dout * gv
        dh2 = r * (dxh - xh * jnp.mean(dxh * xh, axis=-1, keepdims=True))
        dh2_ref[...] = dh2
        dz_ref[...] = (dh2 * ev * s * (1.0 - s)).astype(BF16)
        acc[...] += lax.dot_general(pb, (dh2 * s).astype(BF16), (TN, ((), ())), preferred_element_type=F32)

        @pl.when(pl.program_id(0) == nb - 1)
        def _():
            for j in range(N_CHIP):
                slab = acc[:, j * Q:(j + 1) * Q]
                gw_ref[j] = slab
                gw16_ref[j] = slab.astype(BF16)

    row = _bs((tb, D), lambda i: (i, 0))
    vec = _bs((1, D), lambda i: (0, 0))
    slabs = _bs((N_CHIP, E, Q), lambda i: (0, 0, 0))
    rin = _bs((tb, D), lambda i: (i, 0))
    return _call(body, grid=(nb,),
                 in_specs=[rin, _bs((tb, E), lambda i: (i, 0)), _bs((E, D), lambda i: (0, 0)), rin, rin, vec],
                 out_specs=[row, row, vec, _bs((1, LANES), lambda i: (0, 0)), slabs, slabs],
                 out_shape=[jax.ShapeDtypeStruct((T, D), F32), jax.ShapeDtypeStruct((T, D), BF16),
                            jax.ShapeDtypeStruct((1, D), F32), jax.ShapeDtypeStruct((1, LANES), F32),
                            jax.ShapeDtypeStruct((N_CHIP, E, Q), F32), jax.ShapeDtypeStruct((N_CHIP, E, Q), BF16)],
                 scratch_shapes=[pltpu.VMEM((E, D), F32)],
                 compiler_params=_cp(("arbitrary",)), name="final_fb")(h1, pe, wple, z, tgt, g2)


def _pool_inv_count(t0, rows, pg, ngroups):
    t = t0 + lax.broadcasted_iota(jnp.int32, (rows, pg), 0)
    parts = []
    for w in POOL_WINDOWS[:ngroups]:
        parts.append(jnp.where(t + 1 >= w, 1.0 / w, 1.0 / (t + 1).astype(F32)))
    return parts


def _pool_fwd(proj, wp, P, tb):
    T = proj.shape[0]
    ng = len(POOL_WINDOWS)
    pg = P // ng
    hb = tb // POOL_HALO

    def body(v_ref, tail_ref, w_ref, o_ref, mx_ref, ext):
        i = pl.program_id(0)
        ext[pl.ds(0, POOL_HALO), :] = jnp.where(i > 0, tail_ref[...], 0.0)
        ext[pl.ds(POOL_HALO, tb), :] = v_ref[...]
        inv = _pool_inv_count(i * tb, tb, pg, ng)
        for g, w in enumerate(POOL_WINDOWS):
            cols = pl.ds(g * pg, pg)
            win = ext[pl.ds(POOL_HALO, tb), cols]
            for k in range(1, w):
                win = win + ext[pl.ds(POOL_HALO - k, tb), cols]
            pooled = (win * inv[g] - ext[pl.ds(POOL_HALO, tb), cols]).astype(BF16)
            o_ref[:, cols] = pooled
            mx_ref[:, cols] = jnp.dot(pooled, w_ref[g], preferred_element_type=F32)

    row = _bs((tb, P), lambda i: (i, 0))
    return _call(body, grid=(T // tb,),
                 in_specs=[row, _bs((POOL_HALO, P), lambda i: (jnp.maximum(i * hb - 1, 0), 0)),
                           _bs(wp.shape, lambda i: (0, 0, 0))],
                 out_specs=[row, row],
                 out_shape=[jax.ShapeDtypeStruct((T, P), BF16), jax.ShapeDtypeStruct((T, P), F32)],
                 scratch_shapes=[pltpu.VMEM((tb + POOL_HALO, P), F32)],
                 compiler_params=_cp(("arbitrary",)), name="pool_fwd")(proj, proj, wp)


def _pool_bwd(dmixed, wp, tb, comm=None):
    T, P = dmixed.shape
    ng = len(POOL_WINDOWS)
    pg = P // ng
    hb = tb // POOL_HALO
    nb = T // tb

    def core(d_ref, head_ref, w_ref, o_ref, ext, dpl):
        i = pl.program_id(0)
        inv = _pool_inv_count(i * tb, tb, pg, ng)
        invh = _pool_inv_count((i + 1) * tb, POOL_HALO, pg, ng)
        for g in range(ng):
            cols = pl.ds(g * pg, pg)
            dp = lax.dot_general(d_ref[:, cols], w_ref[g], (NT, ((), ())), preferred_element_type=F32)
            dph = lax.dot_general(head_ref[:, cols], w_ref[g], (NT, ((), ())), preferred_element_type=F32)
            dpl[:, cols] = dp
            ext[pl.ds(0, tb), cols] = dp * inv[g]
            ext[pl.ds(tb, POOL_HALO), cols] = jnp.where(i < nb - 1, dph * invh[g], 0.0)
        for g, w in enumerate(POOL_WINDOWS):
            cols = pl.ds(g * pg, pg)
            acc = ext[pl.ds(0, tb), cols]
            for k in range(1, w):
                acc = acc + ext[pl.ds(k, tb), cols]
            o_ref[:, cols] = (acc - dpl[:, cols]).astype(BF16)

    body, extra = _hosted(core, comm, (nb,), 3, 1, 2)
    return _call(body, grid=(nb,),
                 in_specs=[_bs((tb, P), lambda i: (i, 0)),
                           _bs((POOL_HALO, P), lambda i: (jnp.minimum((i + 1) * hb, T // POOL_HALO - 1), 0)),
                           _bs(wp.shape, lambda i: (0, 0, 0))] + extra["in_specs"],
                 out_specs=[_bs((tb, P), lambda i: (i, 0))] + extra["out_specs"],
                 out_shape=[jax.ShapeDtypeStruct((T, P), BF16)] + extra["out_shape"],
                 scratch_shapes=[pltpu.VMEM((tb + POOL_HALO, P), F32), pltpu.VMEM((tb, P), F32)] + extra["scratch"],
                 compiler_params=_cp(("arbitrary",)), name="pool_bwd")(dmixed, dmixed, wp, *extra["ins"])


def _zoh(a_re, a_im, ldt, b_re, b_im):
    lam_re = jnp.minimum(a_re, A_RE_MAX)
    lam_im = a_im
    dt = jnp.exp(ldt)
    mag = jnp.exp(lam_re * dt)
    ang = lam_im * dt
    ab_re = mag * jnp.cos(ang)
    ab_im = mag * jnp.sin(ang)
    den = lam_re * lam_re + lam_im * lam_im
    n_re = ab_re - 1.0
    n_im = ab_im
    q_re = (n_re * lam_re + n_im * lam_im) / den
    q_im = (n_im * lam_re - n_re * lam_im) / den
    return ab_re, ab_im, q_re * b_re - q_im * b_im, q_re * b_im + q_im * b_re


def _ssm_prep(a_re, a_im, ldt, bt_re, bt_im):
    shp = jax.ShapeDtypeStruct(a_re.shape, F32)

    def body(a, b, c, d, e, o0, o1, o2, o3):
        r = _zoh(a[...], b[...], c[...], d[...], e[...])
        o0[...], o1[...], o2[...], o3[...] = r

    return _call(body, in_specs=[VMEM_FULL] * 5, out_specs=[VMEM_FULL] * 4, out_shape=[shp] * 4,
                 name="ssm_prep")(a_re, a_im, ldt, bt_re, bt_im)


def _ssm_prep_bwd(a_re, a_im, ldt, bt_re, bt_im, dab_re, dab_im, dbb_re, dbb_im, G):
    GC, N = a_re.shape
    C = GC // G

    def body(a, b, c, d, e, g0, g1, g2, g3, da_re, da_im, dldt, db_re, db_im):
        _, vjp = jax.vjp(_zoh, a[...], b[...], c[...], d[...], e[...])
        ga_re, ga_im, gl, gb_re, gb_im = vjp((g0[...], g1[...], g2[...], g3[...]))
        da_re[...] = jnp.sum(ga_re.reshape(G, C, N), axis=1)
        da_im[...] = jnp.sum(ga_im.reshape(G, C, N), axis=1)
        dldt[...] = jnp.sum(jnp.sum(gl.reshape(G, C, N), axis=1), axis=1, keepdims=True)
        db_re[...] = gb_re
        db_im[...] = gb_im

    gn = jax.ShapeDtypeStruct((G, N), F32)
    full = jax.ShapeDtypeStruct((GC, N), F32)
    return _call(body, in_specs=[VMEM_FULL] * 9, out_specs=[VMEM_FULL] * 5,
                 out_shape=[gn, gn, jax.ShapeDtypeStruct((G, 1), F32), full, full],
                 name="ssm_prep_bwd")(a_re, a_im, ldt, bt_re, bt_im, dab_re, dab_im, dbb_re, dbb_im)


def _coef_tiles(abr, abi, reverse):
    ns = abr.shape[1]
    row = lax.broadcasted_iota(jnp.int32, (SUBLANES, ns), 0)
    ar = jnp.broadcast_to(abr, (SUBLANES, ns))
    ai = jnp.broadcast_to(-abi if reverse else abi, (SUBLANES, ns))
    a2r, a2i = ar * ar - ai * ai, 2.0 * ar * ai
    a4r, a4i = a2r * a2r - a2i * a2i, 2.0 * a2r * a2i
    out = []
    for d, (vr, vi) in ((1, (ar, ai)), (2, (a2r, a2i)), (4, (a4r, a4i))):
        keep = (row < SUBLANES - d) if reverse else (row >= d)
        out += [jnp.where(keep, vr, 0.0), jnp.where(keep, vi, 0.0)]
    pr, pi = ar, ai
    for k in range(1, SUBLANES):
        sel = (row <= SUBLANES - 1 - k) if reverse else (row >= k)
        nr, ni = pr * ar - pi * ai, pr * ai + pi * ar
        pr, pi = jnp.where(sel, nr, pr), jnp.where(sel, ni, pi)
    return out + [pr, pi]


def _cpow(ar, ai, n):
    out, br, bi = None, ar, ai
    while n:
        if n & 1:
            out = (br, bi) if out is None else (out[0] * br - out[1] * bi, out[0] * bi + out[1] * br)
        br, bi = br * br - bi * bi, 2.0 * br * bi
        n >>= 1
    return out


def _seg_perm_matrix(nrows):
    r = jnp.arange(nrows)
    src = (nrows // SUBLANES) * (r % SUBLANES) + r // SUBLANES
    return (src[:, None] == jnp.arange(nrows)[None, :]).astype(BF16)


def _seg_order_rows(pm, xb):
    return jnp.dot(pm, xb, preferred_element_type=F32).astype(BF16)


def _time_order_rows(pm, x, terms):
    out, rest = None, x
    for t in range(terms):
        piece = rest.astype(BF16)
        part = lax.dot_general(pm, piece, (TN, ((), ())), preferred_element_type=F32)
        out = part if out is None else out + part
        if t + 1 < terms:
            rest = rest - piece.astype(F32)
    return out


def _scan_tiles(abr, abi, seg, reverse):
    ns = abr.shape[1]
    seg_pow = _cpow(abr, abi, seg)
    step = [jnp.broadcast_to(abr, (SUBLANES, ns)), jnp.broadcast_to(-abi if reverse else abi, (SUBLANES, ns))]
    return _coef_tiles(seg_pow[0], seg_pow[1], reverse) + step


def _seg_scan(xr_ref, xi_ref, coef_ref, car_ref, cai_ref, *, nrows, ns, reverse, cmat=None, dab=None):
    seg = nrows // SUBLANES
    cw = min(SCAN_LANES, ns)
    row = lax.broadcasted_iota(jnp.int32, (SUBLANES, cw), 0)
    first, last = (SUBLANES - 1, 0) if reverse else (0, SUBLANES - 1)

    def tile(i):
        return pl.ds(pl.multiple_of(((seg - 1 - i) if reverse else i) * SUBLANES, SUBLANES), SUBLANES)

    for cc in range(ns // cw):
        cols = pl.ds(cc * cw, cw)
        ar, ai = coef_ref[8, :, cols], coef_ref[9, :, cols]

        def local(i, x, cols=cols, ar=ar, ai=ai):
            rows = tile(i)
            nr = ar * x[0] - ai * x[1] + xr_ref[rows, cols]
            ni = ar * x[1] + ai * x[0] + xi_ref[rows, cols]
            xr_ref[rows, cols] = nr
            xi_ref[rows, cols] = ni
            return nr, ni

        zero = jnp.zeros((SUBLANES, cw), F32)
        er, ei = lax.fori_loop(0, seg, local, (zero, zero))

        co = [coef_ref[k, :, cols] for k in range(8)]
        for lvl, d in enumerate((1, 2, 4)):
            kr, ki = co[2 * lvl], co[2 * lvl + 1]
            sh = SUBLANES - d if reverse else d
            sr, si = pltpu.roll(er, sh, 0), pltpu.roll(ei, sh, 0)
            er, ei = er + (kr * sr - ki * si), ei + (kr * si + ki * sr)
        c0r, c0i = car_ref[:, cols], cai_ref[:, cols]
        er, ei = er + (co[6] * c0r - co[7] * c0i), ei + (co[6] * c0i + co[7] * c0r)
        nb_shift = SUBLANES - 1 if reverse else 1
        cmr = jnp.where(row == first, c0r, pltpu.roll(er, nb_shift, 0))
        cmi = jnp.where(row == first, c0i, pltpu.roll(ei, nb_shift, 0))
        car_ref[:, cols] = jnp.broadcast_to(er[last:last + 1, :], er.shape)
        cai_ref[:, cols] = jnp.broadcast_to(ei[last:last + 1, :], ei.shape)
        if cmat is not None:
            cmat[0][:, cols] = cmr
            cmat[1][:, cols] = cmi

        w0 = (ar * cmr - ai * cmi, ar * cmi + ai * cmr)
        if dab is None:
            def fix(i, w, cols=cols, ar=ar, ai=ai):
                rows = tile(i)
                xr_ref[rows, cols] = xr_ref[rows, cols] + w[0]
                xi_ref[rows, cols] = xi_ref[rows, cols] + w[1]
                return ar * w[0] - ai * w[1], ar * w[1] + ai * w[0]

            lax.fori_loop(0, seg, fix, w0)
        else:
            s_re, s_im, e_re, e_im, o_re, o_im = dab

            def add(rows, w, pr, pi, acc):
                gr = xr_ref[rows, cols] + w[0]
                gi = xi_ref[rows, cols] + w[1]
                xr_ref[rows, cols] = gr
                xi_ref[rows, cols] = gi
                return acc[0] + (gr * pr + gi * pi), acc[1] + (gi * pr - gr * pi)

            def fix(i, st, cols=cols, ar=ar, ai=ai):
                w, acc = st[:2], st[2:]
                rows = tile(i)
                before = pl.ds(pl.multiple_of((seg - 2 - i) * SUBLANES, SUBLANES), SUBLANES)
                acc = add(rows, w, s_re[before, cols], s_im[before, cols], acc)
                return (ar * w[0] - ai * w[1], ar * w[1] + ai * w[0]) + acc

            st = lax.fori_loop(0, seg - 1, fix, w0 + (zero, zero))
            acc = add(pl.ds(0, SUBLANES), st[:2], e_re[:, cols], e_im[:, cols], st[2:])
            o_re[:, cols] += jnp.sum(acc[0], axis=0, keepdims=True)
            o_im[:, cols] += jnp.sum(acc[1], axis=0, keepdims=True)


def _hosted(core, comm, grid, n_in, n_out, n_scratch):
    ci = len(comm["ins"]) if comm else 0
    co = len(comm["out_shape"]) if comm else 0

    def body(*refs):
        ins, rest = refs[:n_in + ci], refs[n_in + ci:]
        outs, scr = rest[:n_out + co], rest[n_out + co:]
        hooks = functools.partial(_comm_hooks, comm, grid, ins[n_in:], outs[n_out:], scr[n_scratch:])
        hooks(before=True)
        core(*ins[:n_in], *outs[:n_out], *scr[:n_scratch])
        hooks(before=False)

    aliases = {n_in + i: n_out + i for i in range(co)} if comm and comm.get("alias") else {}
    extra = dict(ins=list(comm["ins"]) if comm else [], in_specs=[ANY] * ci, out_specs=[ANY] * co,
                 out_shape=list(comm["out_shape"]) if comm else [], scratch=list(comm["scratch"]) if comm else [],
                 aliases=aliases)
    return body, extra


def _ssm_fwd(proj, bdr, bdi, cdr, cdi, abr, abi, dsk, P, tb, comm=None):
    T = proj.shape[0]
    ntl, ct, st = bdr.shape
    ns = ntl * st
    nb = T // tb

    def core(u_ref, bdr_ref, bdi_ref, cdr_ref, cdi_ref, abr_ref, abi_ref, d_ref, pm_ref,
             y_ref, ge_ref, bsr_ref, bsi_ref, sr, si, coef, car, cai, up):
        @pl.when(pl.program_id(0) == 0)
        def _():
            for k, tile in enumerate(_scan_tiles(abr_ref[...], abi_ref[...], tb // SUBLANES, False)):
                coef[k] = tile
            car[...] = jnp.zeros_like(car)
            cai[...] = jnp.zeros_like(cai)

        bsr_ref[...] = car[...]
        bsi_ref[...] = cai[...]
        u = u_ref[...]
        ub = _seg_order_rows(pm_ref[...], u.astype(BF16))
        for s in range(ntl):
            us = ub[:, s * ct:(s + 1) * ct]
            sr[:, s * st:(s + 1) * st] = jnp.dot(us, bdr_ref[s], preferred_element_type=F32)
            si[:, s * st:(s + 1) * st] = jnp.dot(us, bdi_ref[s], preferred_element_type=F32)
        _seg_scan(sr, si, coef, car, cai, nrows=tb, ns=ns, reverse=False)
        for s in range(ntl):
            s_re = sr[:, s * st:(s + 1) * st].astype(BF16)
            s_im = si[:, s * st:(s + 1) * st].astype(BF16)
            up[:, s * ct:(s + 1) * ct] = (jnp.dot(s_re, cdr_ref[s], preferred_element_type=F32)
                                          - jnp.dot(s_im, cdi_ref[s], preferred_element_type=F32))
        y = _time_order_rows(pm_ref[...], up[...], 3) + d_ref[...] * u
        y_ref[...] = y
        ge_ref[...] = _gelu(y).astype(BF16)

    full3 = lambda a: _bs(a.shape, lambda i: (0, 0, 0))
    vec = lambda n: _bs((1, n), lambda i: (0, 0))
    row = _bs((tb, P), lambda i: (i, 0))
    st_spec = _bs((None, SUBLANES, ns), lambda i: (i, 0, 0))
    body, extra = _hosted(core, comm, (nb,), 9, 4, 6)
    return _call(body, grid=(nb,),
                 in_specs=[_bs((tb, P), lambda i: (i, 2)), full3(bdr), full3(bdi), full3(cdr), full3(cdi),
                           vec(ns), vec(ns), vec(P), _bs((tb, tb), lambda i: (0, 0))] + extra["in_specs"],
                 out_specs=[row, row, st_spec, st_spec] + extra["out_specs"],
                 out_shape=[jax.ShapeDtypeStruct((T, P), F32), jax.ShapeDtypeStruct((T, P), BF16),
                            jax.ShapeDtypeStruct((nb, SUBLANES, ns), F32),
                            jax.ShapeDtypeStruct((nb, SUBLANES, ns), F32)] + extra["out_shape"],
                 scratch_shapes=[pltpu.VMEM((tb, ns), F32), pltpu.VMEM((tb, ns), F32),
                                 pltpu.VMEM((10, SUBLANES, ns), F32),
                                 pltpu.VMEM((SUBLANES, ns), F32), pltpu.VMEM((SUBLANES, ns), F32),
                                 pltpu.VMEM((tb, P), F32)] + extra["scratch"],
                 compiler_params=_cp(("arbitrary",)), name="ssm_fwd")(
                     proj, bdr, bdi, cdr, cdi, abr, abi, dsk, _seg_perm_matrix(tb), *extra["ins"])


def _ssm_bwd(proj, y, dge, bsr, bsi, bdr, bdi, cdr, cdi, abr, abi, dsk, dpi, dpg, dsg, P, tb, comm=None):
    T = proj.shape[0]
    ntl, ct, st = bdr.shape
    ns = ntl * st
    nb = T // tb

    def core(u_ref, y_ref, dge_ref, bsr_ref, bsi_ref, abr_ref, abi_ref, d_ref, pm_ref, dpi_ref, dpg_ref, dsg_ref,
             bdr_h, bdi_h, cdr_h, cdi_h,
             dproj_ref, dabr_ref, dabi_ref, dd_ref, dbdr_h, dbdi_h, dcdr_h, dcdi_h,
             wbdr, wbdi, wcdr, wcdi, abdr, abdi, acdr, acdi, spr, spi, gr, gi, coef_f, coef_r,
             car, cai, gcr, gci, ser, sei, dup):
        i = pl.program_id(0)

        @pl.when(i == 0)
        def _():
            for h, w in ((bdr_h, wbdr), (bdi_h, wbdi), (cdr_h, wcdr), (cdi_h, wcdi)):
                pltpu.sync_copy(h, w)
            for a in (abdr, abdi, acdr, acdi, gcr, gci):
                a[...] = jnp.zeros_like(a)
            for o in (dabr_ref, dabi_ref, dd_ref):
                o[...] = jnp.zeros_like(o)
            for k, tile in enumerate(_scan_tiles(abr_ref[...], abi_ref[...], tb // SUBLANES, False)):
                coef_f[k] = tile
            for k, tile in enumerate(_scan_tiles(abr_ref[...], abi_ref[...], tb // SUBLANES, True)):
                coef_r[k] = tile

        car[...] = bsr_ref[...]
        cai[...] = bsi_ref[...]
        u = u_ref[...]
        dy = dge_ref[...] * _gelu_grad(y_ref[...])
        ub = _seg_order_rows(pm_ref[...], u.astype(BF16))
        dyb = _seg_order_rows(pm_ref[...], dy.astype(BF16))
        for s in range(ntl):
            us = ub[:, s * ct:(s + 1) * ct]
            spr[:, s * st:(s + 1) * st] = jnp.dot(us, wbdr[s], preferred_element_type=F32)
            spi[:, s * st:(s + 1) * st] = jnp.dot(us, wbdi[s], preferred_element_type=F32)
        _seg_scan(spr, spi, coef_f, car, cai, nrows=tb, ns=ns, reverse=False, cmat=(ser, sei))

        for s in range(ntl):
            dys = dyb[:, s * ct:(s + 1) * ct]
            gr[:, s * st:(s + 1) * st] = lax.dot_general(dys, wcdr[s], (NT, ((), ())), preferred_element_type=F32)
            gi[:, s * st:(s + 1) * st] = -lax.dot_general(dys, wcdi[s], (NT, ((), ())), preferred_element_type=F32)
        _seg_scan(gr, gi, coef_r, gcr, gci, nrows=tb, ns=ns, reverse=True,
                  dab=(spr, spi, ser, sei, dabr_ref, dabi_ref))

        for s in range(ntl):
            sl_c, sl_s = slice(s * ct, (s + 1) * ct), slice(s * st, (s + 1) * st)
            s_re = spr[:, sl_s].astype(BF16)
            s_im = spi[:, sl_s].astype(BF16)
            g_re, g_im = gr[:, sl_s].astype(BF16), gi[:, sl_s].astype(BF16)
            dys, us = dyb[:, sl_c], ub[:, sl_c]
            acdr[s] += lax.dot_general(s_re, dys, (TN, ((), ())), preferred_element_type=F32)
            acdi[s] -= lax.dot_general(s_im, dys, (TN, ((), ())), preferred_element_type=F32)
            abdr[s] += lax.dot_general(us, g_re, (TN, ((), ())), preferred_element_type=F32)
            abdi[s] += lax.dot_general(us, g_im, (TN, ((), ())), preferred_element_type=F32)
            dup[:, sl_c] = (lax.dot_general(g_re, wbdr[s], (NT, ((), ())), preferred_element_type=F32)
                            + lax.dot_general(g_im, wbdi[s], (NT, ((), ())), preferred_element_type=F32))
        dd_ref[...] += jnp.sum(dy * u, axis=0, keepdims=True)
        du = _time_order_rows(pm_ref[...], dup[...], 2) + d_ref[...] * dy
        dproj_ref[:, 0:P] = dpi_ref[...]
        dproj_ref[:, P:2 * P] = dpg_ref[...]
        dproj_ref[:, 2 * P:3 * P] = du.astype(BF16)
        dproj_ref[:, 3 * P:4 * P] = dsg_ref[...]

        @pl.when(i == nb - 1)
        def _():
            for a, h in ((abdr, dbdr_h), (abdi, dbdi_h), (acdr, dcdr_h), (acdi, dcdi_h)):
                pltpu.sync_copy(a, h)

    rev = lambda i: nb - 1 - i
    vec = lambda n: _bs((1, n), lambda i: (0, 0))
    row = _bs((tb, P), lambda i: (rev(i), 0))
    st_spec = _bs((None, SUBLANES, ns), lambda i: (rev(i), 0, 0))
    bshape = jax.ShapeDtypeStruct(bdr.shape, F32)
    cshape = jax.ShapeDtypeStruct(cdr.shape, F32)
    body, extra = _hosted(core, comm, (nb,), 16, 8, 21)
    return _call(body, grid=(nb,),
                 in_specs=[_bs((tb, P), lambda i: (rev(i), 2)), row, row, st_spec, st_spec,
                           vec(ns), vec(ns), vec(P), _bs((tb, tb), lambda i: (0, 0)), row, row, row,
                           ANY, ANY, ANY, ANY] + extra["in_specs"],
                 out_specs=[_bs((tb, 4 * P), lambda i: (rev(i), 0)), vec(ns), vec(ns), vec(P), ANY, ANY, ANY, ANY]
                 + extra["out_specs"],
                 out_shape=[jax.ShapeDtypeStruct((T, 4 * P), BF16), jax.ShapeDtypeStruct((1, ns), F32),
                            jax.ShapeDtypeStruct((1, ns), F32), jax.ShapeDtypeStruct((1, P), F32),
                            bshape, bshape, cshape, cshape] + extra["out_shape"],
                 scratch_shapes=[pltpu.VMEM(bdr.shape, BF16), pltpu.VMEM(bdr.shape, BF16),
                                 pltpu.VMEM(cdr.shape, BF16), pltpu.VMEM(cdr.shape, BF16),
                                 pltpu.VMEM(bdr.shape, F32), pltpu.VMEM(bdr.shape, F32),
                                 pltpu.VMEM(cdr.shape, F32), pltpu.VMEM(cdr.shape, F32),
                                 pltpu.VMEM((tb, ns), F32), pltpu.VMEM((tb, ns), F32),
                                 pltpu.VMEM((tb, ns), F32), pltpu.VMEM((tb, ns), F32),
                                 pltpu.VMEM((10, SUBLANES, ns), F32), pltpu.VMEM((10, SUBLANES, ns), F32)]
                 + [pltpu.VMEM((SUBLANES, ns), F32)] * 6 + [pltpu.VMEM((tb, P), F32)] + extra["scratch"],
                 compiler_params=_cp(("arbitrary",)), name="ssm_bwd")(
                     proj, y, dge, bsr, bsi, abr, abi, dsk, _seg_perm_matrix(tb), dpi, dpg, dsg,
                     bdr, bdi, cdr, cdi, *extra["ins"])


def _adamw(w, g, m, v, name, comm=None):
    R, C = w.shape
    tr = _t(R, ROWS_ELEMENTWISE)

    def core(w_ref, g_ref, m_ref, v_ref, d_ref, mo_ref, vo_ref):
        gv = g_ref[...]
        mn = ADAM_B1 * m_ref[...] + (1.0 - ADAM_B1) * gv
        vn = ADAM_B2 * v_ref[...] + (1.0 - ADAM_B2) * (gv * gv)
        m_hat = mn / (1.0 - ADAM_B1 ** ADAM_STEP)
        v_hat = vn / (1.0 - ADAM_B2 ** ADAM_STEP)
        d_ref[...] = -ADAM_LR * (m_hat / (jnp.sqrt(v_hat) + ADAM_EPS) + ADAM_WD * w_ref[...])
        mo_ref[...] = mn
        vo_ref[...] = vn

    blk = _bs((tr, C), lambda i: (i, 0))
    shp = jax.ShapeDtypeStruct((R, C), F32)
    body, extra = _hosted(core, comm, (R // tr,), 4, 3, 0)
    return _call(body, grid=(R // tr,), in_specs=[blk] * 4 + extra["in_specs"],
                 out_specs=[blk] * 3 + extra["out_specs"], out_shape=[shp] * 3 + extra["out_shape"],
                 scratch_shapes=extra["scratch"],
                 compiler_params=_cp(("arbitrary",) if comm else ("parallel",)), name=name)(w, g, m, v, *extra["ins"])


def _sum_cast(grad, got, place, name):
    J, H, C = got.shape
    tr = _t(H, ROWS_ELEMENTWISE)
    nb = H // tr

    def body(pl_ref, a_ref, b_ref, o_ref):
        o_ref[...] = (a_ref[...] + b_ref[...]).astype(BF16)

    blk = _bs((None, tr, C), lambda j, i, pc: (j, i, 0))
    mine = _bs((None, tr, C), lambda j, i, pc: (j, pc[1] * nb + i, 0))
    spec = pltpu.PrefetchScalarGridSpec(num_scalar_prefetch=1, grid=(J, nb), in_specs=[mine, blk], out_specs=blk)
    return _call(body, grid_spec=spec, out_shape=jax.ShapeDtypeStruct((J, H, C), BF16),
                 compiler_params=_cp(("parallel", "parallel")), name=name)(place, grad, got)


def _sum_chips(sent, arrived, place, name):
    J, H, C = arrived.shape
    tr = _t(H, ROWS_ELEMENTWISE)
    nb = H // tr

    def body(pl_ref, own_ref, a0_ref, a1_ref, a2_ref, o_ref):
        acc = own_ref[...].astype(F32)
        for r in (a0_ref, a1_ref, a2_ref):
            acc = acc + r[...].astype(F32)
        o_ref[...] = acc

    def other(k):
        return _bs((None, tr, C), lambda i, pc: (jnp.where(pc[0] <= k, k + 1, k), i, 0))

    spec = pltpu.PrefetchScalarGridSpec(
        num_scalar_prefetch=1, grid=(nb,),
        in_specs=[_bs((None, tr, C), lambda i, pc: (pc[0], i, 0)), other(0), other(1), other(2)],
        out_specs=_bs((tr, C), lambda i, pc: (pc[1] * nb + i, 0)))
    return _call(body, grid_spec=spec, out_shape=jax.ShapeDtypeStruct((2 * H, C), F32),
                 compiler_params=_cp(("parallel",)), name=name)(place, sent, arrived, arrived, arrived)


def _place():
    x, y, c = lax.axis_index("x"), lax.axis_index("y"), lax.axis_index("c")
    chips = [(1 - x, y), (x, 1 - y), (1 - x, 1 - y)]
    return x, y, c, chips


def _split(nrows, row_bytes, align, cap=None):
    k = max(1, min(cap or DMA_MAX_CHUNKS, (nrows * row_bytes) // DMA_CHUNK_BYTES))
    while k > 1 and nrows % (k * align):
        k -= 1
    return k


def _comm_call(plan, name):
    n_in, n_out = len(plan["ins"]), len(plan["out_shape"])

    def body(*refs):
        for phase in plan["phases"]:
            phase(refs[:n_in], refs[n_in:n_in + n_out], refs[n_in + n_out:])

    return _call(body, in_specs=[ANY] * n_in, out_specs=[ANY] * n_out, out_shape=plan["out_shape"],
                 input_output_aliases={i: i for i in range(n_out)} if plan.get("alias") else {},
                 scratch_shapes=plan["scratch"], name=name)(*plan["ins"])


def _comm_hooks(plan, grid, ins, outs, sems, *, before):
    if plan is None:
        return
    nsteps, step = 1, 0
    for d, g in enumerate(grid):
        nsteps, step = nsteps * g, step * g + pl.program_id(d)
    for p, (phase, frac) in enumerate(zip(plan["phases"], plan["at"])):
        if (p == 0) == before:
            pl.when(step == int(frac * (nsteps - 1)))(functools.partial(phase, ins, outs, sems))


def _ag_plan(shards, axes):
    n = len(shards)
    shapes = [a.shape for a in shards]

    def window(ref, i, chip, half=None):
        S, ax = shapes[i], axes[i]
        idx = []
        for d in range(len(S)):
            off, size = 0, S[d]
            if d == 0 and half is not None:
                off, size = half * (S[0] // 2), S[0] // 2
            if d == ax:
                off = off + chip * S[ax]
            idx.append(pl.ds(off, size))
        return ref.at[tuple(idx)]

    def copies(src, full, sems):
        ssem, rsem = sems
        x, y, c, chips = _place()
        me = 2 * x + y
        sib = (x, y, 1 - c)
        idx = [2 * cx + cy for cx, cy in chips]

        def rcopy(i, k, s_ref, d_ref, to):
            return pltpu.make_async_remote_copy(src_ref=s_ref, dst_ref=d_ref, send_sem=ssem.at[i, k],
                                                recv_sem=rsem.at[i, k], device_id=to, device_id_type=MESH)

        def ici(i, j, incoming):
            half_src = src[i].at[pl.ds(c * (shapes[i][0] // 2), shapes[i][0] // 2)]
            return rcopy(i, j, half_src, window(full[i], i, idx[j] if incoming else me, c), (*chips[j], c))

        def fwd(i, j, half):
            w = window(full[i], i, idx[j], half)
            return rcopy(i, 3 + j, w, w, sib)

        def own(i):
            return rcopy(i, 6, src[i], window(full[i], i, me), sib)

        return c, ici, fwd, own

    def send(src, full, sems):
        c, ici, fwd, own = copies(src, full, sems)
        for i in range(n):
            for j in range(3):
                ici(i, j, False).start()
        for i in range(n):
            own(i).start()

    def forward(i, src, full, sems):
        c, ici, fwd, own = copies(src, full, sems)
        for j in range(3):
            ici(i, j, True).wait_recv()
            fwd(i, j, c).start()

    def finish(src, full, sems):
        c, ici, fwd, own = copies(src, full, sems)
        for i in range(n):
            for j in range(3):
                fwd(i, j, 1 - c).wait_recv()
            own(i).wait()
        for i in range(n):
            for j in range(3):
                ici(i, j, False).wait_send()
                fwd(i, j, c).wait_send()

    out_shape = [jax.ShapeDtypeStruct(tuple(N_CHIP * d if k == ax else d for k, d in enumerate(S)), BF16)
                 for S, ax in zip(shapes, axes)]
    sizes = [a.size for a in shards]
    behind = [AG_FORWARD_SCALE * sum(sizes[:i + 1]) / sum(sizes) + AG_FORWARD_LAG for i in range(n)]
    return dict(ins=list(shards), out_shape=out_shape,
                phases=[send] + [functools.partial(forward, i) for i in range(n)] + [finish],
                at=[0.0] + behind + [1.0],
                scratch=[pltpu.SemaphoreType.DMA((n, 7)), pltpu.SemaphoreType.DMA((n, 7))])


def _proj_ag(x, g1, wsh, order, tm):
    T, D = x.shape
    P = wsh.shape[1]
    H = D // 2
    nt = T // tm

    def body(order_ref, x_ref, g_ref, wsh_ref, hn_ref, proj_ref, win_ref, wbuf, lsem, ssem, rsem):
        n, i = pl.program_id(0), pl.program_id(1)
        x, y, c, chips = _place()
        me = 2 * x + y
        sib = (x, y, 1 - c)
        idx = [2 * cx + cy for cx, cy in chips]

        def rcopy(k, s_ref, d_ref, to):
            return pltpu.make_async_remote_copy(src_ref=s_ref, dst_ref=d_ref, send_sem=ssem.at[k],
                                                recv_sem=rsem.at[k], device_id=to, device_id_type=MESH)

        def cols(chip):
            return pl.ds(pl.multiple_of(chip * P, LANES), P)

        def rows(half):
            return pl.ds(pl.multiple_of(half * H, BF16_TILE_ROWS), H)

        def ici(j, incoming):
            return rcopy(j, wsh_ref.at[rows(c)], win_ref.at[rows(c), cols(idx[j] if incoming else me)],
                         (*chips[j], c))

        def fwd(j, half):
            w = win_ref.at[rows(half), cols(idx[j])]
            return rcopy(3 + j, w, w, sib)

        def own():
            return rcopy(6, wsh_ref, win_ref.at[:, cols(me)], sib)

        def load(src):
            cp = pltpu.make_async_copy(src, wbuf, lsem)
            cp.start()
            cp.wait()

        @pl.when((n == 0) & (i == 0))
        def _():
            ici(0, False).start()
            ici(1, False).start()
            own().start()
            load(wsh_ref)

        for j in range(3):
            @pl.when((n == j + 1) & (i == 0))
            def _(j=j):
                if j == 0:
                    ici(2, False).start()
                ici(j, True).wait_recv()
                fwd(j, c).start()
                fwd(j, 1 - c).wait_recv()
                load(win_ref.at[:, cols(idx[j])])

        xv = x_ref[...]
        r = lax.rsqrt(jnp.mean(xv * xv, axis=-1, keepdims=True) + EPS)
        hn = ((xv * r) * g_ref[...]).astype(BF16)

        @pl.when(n == 0)
        def _():
            hn_ref[...] = hn

        proj_ref[...] = jnp.dot(hn, wbuf[...], preferred_element_type=F32)

        @pl.when((n == 3) & (i == nt - 1))
        def _():
            own().wait()
            for j in range(3):
                ici(j, False).wait_send()
                fwd(j, c).wait_send()

    spec = pltpu.PrefetchScalarGridSpec(
        num_scalar_prefetch=1, grid=(N_CHIP, nt),
        in_specs=[_bs((tm, D), lambda n, i, o: (i, 0)), _bs((1, D), lambda n, i, o: (0, 0)), ANY],
        out_specs=[_bs((tm, D), lambda n, i, o: (jnp.where(n == 0, i, nt - 1), 0)),
                   _bs((tm, P), lambda n, i, o: (i, o[n])), ANY],
        scratch_shapes=[pltpu.VMEM((D, P), BF16), pltpu.SemaphoreType.DMA,
                        pltpu.SemaphoreType.DMA((7,)), pltpu.SemaphoreType.DMA((7,))])
    return _call(body, grid_spec=spec,
                 out_shape=[jax.ShapeDtypeStruct((T, D), BF16), jax.ShapeDtypeStruct((T, N_CHIP * P), F32),
                            jax.ShapeDtypeStruct((D, N_CHIP * P), BF16)],
                 compiler_params=_cp(("arbitrary", "arbitrary")), name="proj_ag")(order, x, g1, wsh)


def _halves_plan(grads):
    n = len(grads)

    def send(g, got, sems):
        ssem, rsem = sems
        x, y, c, _ = _place()
        sib = (x, y, 1 - c)
        for i in range(n):
            J, R, C = g[i].shape
            H = R // 2
            size = g[i].dtype.itemsize
            tile_rows = SUBLANES * 4 // size
            k = _split(H, C * size, tile_rows, cap=DMA_MAX_CHUNKS // J)
            hr = H // k
            for j in range(J):
                for q in range(k):
                    other = pl.ds(pl.multiple_of((1 - c) * H + q * hr, tile_rows), hr)
                    to = pl.ds(q * hr, hr)
                    pltpu.make_async_remote_copy(src_ref=g[i].at[j, other, :], dst_ref=got[i].at[j, to, :],
                                                 send_sem=ssem.at[i], recv_sem=rsem.at[i],
                                                 device_id=sib, device_id_type=MESH).start()

    def finish(g, got, sems):
        ssem, rsem = sems
        x, y, c, _ = _place()
        for i in range(n):
            pltpu.make_async_remote_copy(src_ref=got[i], dst_ref=got[i], send_sem=ssem.at[i], recv_sem=rsem.at[i],
                                         device_id=(x, y, 1 - c), device_id_type=MESH).wait()

    half = [jax.ShapeDtypeStruct((a.shape[0], a.shape[1] // 2, a.shape[2]), a.dtype) for a in grads]
    return dict(ins=list(grads), out_shape=half, phases=[send, finish], at=[0.0, 1.0],
                scratch=[pltpu.SemaphoreType.DMA((n,)), pltpu.SemaphoreType.DMA((n,))])


def _scatter_plan(parts):
    n = len(parts)

    def peers():
        x, y, c, chips = _place()
        return 2 * x + y, c, chips, [2 * cx + cy for cx, cy in chips]

    def send(s, got, sems):
        ssem, rsem = sems
        me, c, chips, idx = peers()
        for i in range(n):
            _, H, C = s[i].shape
            k = _split(H, C * 2, BF16_TILE_ROWS, cap=RS_CHUNKS)
            hr = H // k
            for q in range(k):
                rows = pl.ds(q * hr, hr)
                for j in range(3):
                    pltpu.make_async_remote_copy(src_ref=s[i].at[idx[j], rows, :], dst_ref=got[i].at[me, rows, :],
                                                 send_sem=ssem.at[i, j], recv_sem=rsem.at[i, j],
                                                 device_id=(*chips[j], c), device_id_type=MESH).start()

    def finish(s, got, sems):
        ssem, rsem = sems
        me, c, chips, idx = peers()
        for i in range(n):
            for j in range(3):
                pltpu.make_async_remote_copy(src_ref=s[i].at[idx[j]], dst_ref=got[i].at[idx[j]],
                                             send_sem=ssem.at[i, j], recv_sem=rsem.at[i, j],
                                             device_id=(*chips[j], c), device_id_type=MESH).wait()

    return dict(ins=list(parts), out_shape=[jax.ShapeDtypeStruct(a.shape, a.dtype) for a in parts],
                phases=[send, finish], at=[0.0, 1.0],
                scratch=[pltpu.SemaphoreType.DMA((n, 3)), pltpu.SemaphoreType.DMA((n, 3))])


def _join_plan(shards):
    n = len(shards)

    def send(_, full, sems):
        ssem, rsem = sems
        x, y, c, _ = _place()
        sib = (x, y, 1 - c)
        for i in range(n):
            H, C = full[i].shape[0] // 2, full[i].shape[1]
            k = _split(H, C * 4, SUBLANES)
            hr = H // k
            for q in range(k):
                rows = pl.ds(pl.multiple_of(c * H + q * hr, SUBLANES), hr)
                pltpu.make_async_remote_copy(src_ref=full[i].at[rows], dst_ref=full[i].at[rows],
                                             send_sem=ssem.at[i], recv_sem=rsem.at[i],
                                             device_id=sib, device_id_type=MESH).start()

    def finish(_, full, sems):
        ssem, rsem = sems
        x, y, c, _ = _place()
        for i in range(n):
            half = full[i].at[pl.ds(0, full[i].shape[0] // 2)]
            pltpu.make_async_remote_copy(src_ref=half, dst_ref=half, send_sem=ssem.at[i], recv_sem=rsem.at[i],
                                         device_id=(x, y, 1 - c), device_id_type=MESH).wait()

    return dict(ins=list(shards), out_shape=[jax.ShapeDtypeStruct(a.shape, a.dtype) for a in shards],
                phases=[send, finish], at=[0.0, 1.0], alias=True,
                scratch=[pltpu.SemaphoreType.DMA((n,)), pltpu.SemaphoreType.DMA((n,))])


def _allreduce_plan(buf):
    R, L = buf.shape
    RB = R // N_DEV

    def parts(sems):
        xv, got, ov, lsem, ssem, rsem = sems
        x, y, c, _ = _place()
        me = 4 * x + 2 * y + c

        def dev(k):
            return (k // 4, (k // 2) % 2, k % 2)

        def slab(k):
            return pl.ds(pl.multiple_of(k * RB, SUBLANES), RB)

        def first(d, to, landing):
            return pltpu.make_async_remote_copy(src_ref=xv.at[slab(to)], dst_ref=got.at[landing],
                                                send_sem=ssem.at[0, d], recv_sem=rsem.at[0, d],
                                                device_id=dev(to), device_id_type=MESH)

        def second(d, to, k):
            return pltpu.make_async_remote_copy(src_ref=ov.at[slab(k)], dst_ref=ov.at[slab(k)],
                                                send_sem=ssem.at[1, d], recv_sem=rsem.at[1, d],
                                                device_id=dev(to), device_id_type=MESH)

        return me, slab, first, second

    def scatter(ins, outs, sems):
        xv, lsem = sems[0], sems[3]
        me, slab, first, second = parts(sems)
        cp = pltpu.make_async_copy(ins[0], xv, lsem)
        cp.start()
        cp.wait()
        for d in range(1, N_DEV):
            first(d, (me + d) % N_DEV, me).start()

    def reduce(ins, outs, sems):
        xv, got, ov = sems[:3]
        me, slab, first, second = parts(sems)
        got[me] = xv[slab(me), :]
        for d in range(1, N_DEV):
            src = (me + N_DEV - d) % N_DEV
            first(d, src, src).wait_recv()
        acc = got[0]
        for k in range(1, N_DEV):
            acc = acc + got[k]
        ov[slab(me), :] = acc
        for d in range(1, N_DEV):
            second(d, (me + d) % N_DEV, me).start()

    def collect(ins, outs, sems):
        ov, lsem = sems[2], sems[3]
        me, slab, first, second = parts(sems)
        for d in range(1, N_DEV):
            src = (me + N_DEV - d) % N_DEV
            second(d, src, src).wait_recv()
        for d in range(1, N_DEV):
            peer = (me + d) % N_DEV
            first(d, peer, me).wait_send()
            second(d, peer, me).wait_send()
        cp = pltpu.make_async_copy(ov, outs[0], lsem)
        cp.start()
        cp.wait()

    return dict(ins=[buf], out_shape=[jax.ShapeDtypeStruct((R, L), F32)], phases=[scatter, reduce, collect],
                at=[0.0, 0.5, 1.0],
                scratch=[pltpu.VMEM((R, L), F32), pltpu.VMEM((N_DEV, RB, L), F32), pltpu.VMEM((R, L), F32),
                         pltpu.SemaphoreType.DMA, pltpu.SemaphoreType.DMA((2, N_DEV)),
                         pltpu.SemaphoreType.DMA((2, N_DEV))])


def _block_diag(t, gt):
    G, A, B = t.shape
    t4 = t.reshape(G // gt, gt, A, B)
    eye = jnp.eye(gt, dtype=t.dtype)
    return jnp.einsum('sgab,gh->sgahb', t4, eye).reshape(G // gt, gt * A, gt * B)


def _block_diag_extract(m, gt, A, B):
    S = m.shape[0]
    m5 = m.reshape(S, gt, A, gt, B)
    eye = jnp.eye(gt, dtype=m.dtype)
    return jnp.einsum('sgahb,gh->sgab', m5, eye).reshape(S * gt, A, B)


def _tile_rows(n):
    return -(-n // (SUBLANES * LANES)) * SUBLANES


def _pack_small(arrs, rows):
    parts = []
    for a in arrs:
        flat = a.reshape(-1).astype(F32)
        r = _tile_rows(flat.shape[0])
        parts.append(jnp.pad(flat, (0, r * LANES - flat.shape[0])).reshape(r, LANES))
    used = sum(p.shape[0] for p in parts)
    if rows > used:
        parts.append(jnp.zeros((rows - used, LANES), F32))
    return jnp.concatenate(parts)


def _unpack_small(buf, shapes):
    out, off = [], 0
    for s in shapes:
        n = 1
        for d in s:
            n *= d
        r = _tile_rows(n)
        piece = buf[off:off + r]
        out.append(piece.reshape(s) if n == r * LANES else piece.reshape(-1)[:n].reshape(s))
        off += r
    return out


def kernel(x, p, norm_gain, w_in, w_pool, pool_scale, a_re, a_im, log_dt, b_re, b_im, c_re, c_im, d_skip, w_glu, w_out, w_ple, w_ple_gate, final_gain, loss_target, m_norm_gain, m_w_in, m_w_pool, m_pool_scale, m_a_re, m_a_im, m_log_dt, m_b_re, m_b_im, m_c_re, m_c_im, m_d_skip, m_w_glu, m_w_out, m_w_ple, m_w_ple_gate, m_final_gain, v_norm_gain, v_w_in, v_w_pool, v_pool_scale, v_a_re, v_a_im, v_log_dt, v_b_re, v_b_im, v_c_re, v_c_im, v_d_skip, v_w_glu, v_w_out, v_w_ple, v_w_ple_gate, v_final_gain):
    xs, pe, tgt = x[0], p[0, 0], loss_target[0]
    T, D = xs.shape
    E = pe.shape[1]
    P = D // 2
    NG = len(POOL_WINDOWS)
    PG = P // NG
    G, N, C = P // SSM_GROUP, SSM_STATE, SSM_GROUP
    GT = min(SSM_TILE_GROUPS, G)
    Q = D // N_CHIP

    big = {"w_in": (w_in, m_w_in, v_w_in), "w_pool": (w_pool, m_w_pool, v_w_pool),
           "w_glu": (w_glu, m_w_glu, v_w_glu), "w_out": (w_out, m_w_out, v_w_out),
           "w_ple": (w_ple, m_w_ple, v_w_ple), "w_ple_gate": (w_ple_gate, m_w_ple_gate, v_w_ple_gate)}
    big_names = list(big)
    shard2d = {n: (big[n][0].size // big[n][0].shape[-1], big[n][0].shape[-1]) for n in big_names}
    shard_axis = {"w_in": 1, "w_pool": 1, "w_glu": 1, "w_out": 0, "w_ple": 1, "w_ple_gate": 0}
    shard16 = {n: big[n][0][0].astype(BF16) for n in big_names}
    place = jnp.stack([2 * lax.axis_index("x") + lax.axis_index("y"), lax.axis_index("c")]).astype(jnp.int32)
    mx, my = lax.axis_index("x"), lax.axis_index("y")
    block_order = jnp.stack([2 * mx + my, 2 * (1 - mx) + my, 2 * mx + (1 - my),
                             2 * (1 - mx) + (1 - my)]).astype(jnp.int32)
    later = [n for n in big_names if n != "w_in"]
    ag_later = _ag_plan([shard16[n] for n in later], [shard_axis[n] for n in later])

    rep = lambda a: jnp.repeat(a, C, axis=0)
    a_re_r, a_im_r = rep(a_re[0]), rep(a_im[0])
    ldt_r = rep(jnp.broadcast_to(log_dt[0][:, None], (G, N)))
    bt_re = b_re[0].transpose(0, 2, 1).reshape(G * C, N)
    bt_im = b_im[0].transpose(0, 2, 1).reshape(G * C, N)
    ab_re_r, ab_im_r, bbt_re, bbt_im = _ssm_prep(a_re_r, a_im_r, ldt_r, bt_re, bt_im)
    abr = ab_re_r[::C].reshape(1, G * N)
    abi = ab_im_r[::C].reshape(1, G * N)
    bdr = _block_diag(bbt_re.reshape(G, C, N), GT).astype(BF16)
    bdi = _block_diag(bbt_im.reshape(G, C, N), GT).astype(BF16)
    cdr = _block_diag(c_re[0].transpose(0, 2, 1), GT).astype(BF16)
    cdi = _block_diag(c_im[0].transpose(0, 2, 1), GT).astype(BF16)

    tb = _t(T, ROWS_ELEMENTWISE)
    tbs = _t(T, ROWS_SSM)
    tm = _t(T, ROWS_MATMUL)
    tk = _t(T, DEPTH_MATMUL)
    DH = _t(D, ROWS_MATMUL)
    row_k = lambda i, n, k: (i, k)
    row_n = lambda i, n, k: (i, n)
    f32 = lambda *shape: jax.ShapeDtypeStruct(shape, F32)
    hn, proj, win = _proj_ag(xs, norm_gain, shard16["w_in"], block_order, tm)
    y, ge, bsr, bsi, wp, wglu, wout, wple, wpg = _ssm_fwd(proj, bdr, bdi, cdr, cdi, abr, abi, d_skip, P, tbs,
                                                          comm=ag_later)
    pooled, mixed = _pool_fwd(proj, wp, P, tb)
    hg = _mm(ge, wglu, dims=NN, grid=(T // tm, 1, 1),
             a_spec=_bs((tm, P), row_k), b_spec=_bs((P, 2 * P), lambda i, n, k: (k, n)),
             o_spec=_bs((tm, 2 * P), row_n), out_shape=f32(T, 2 * P), name="mm_glu")
    cat = _gate_fwd(mixed, proj, hg, pool_scale, tb)
    h1, h1b = _mm(cat, wout, dims=NN, grid=(T // tm, D // DH, 1), res=xs, bf16_copy=True,
                  a_spec=_bs((tm, D), row_k), b_spec=_bs((D, DH), lambda i, n, k: (k, n)),
                  r_spec=_bs((tm, DH), row_n), o_spec=_bs((tm, DH), row_n), out_shape=f32(T, D), name="mm_out")
    z = _mm(h1b, wpg, dims=NN, grid=(T // tm, 1, 1),
            a_spec=_bs((tm, D), row_k), b_spec=_bs((D, D), lambda i, n, k: (k, n)),
            o_spec=_bs((tm, D), row_n), out_shape=f32(T, D), name="mm_pgate")
    dh2, dz, dg2, lpart, g_wple, g_wple16 = _final_fb(h1, pe, wple, z, tgt, final_gain.reshape(1, D), tb)

    col_m = lambda m, n, k: (k, m)
    col_n = lambda m, n, k: (k, n)
    dh1, dh1b = _mm(dz, wpg, dims=NT, grid=(T // tm, D // DH, 1), res=dh2, bf16_copy=True,
                    a_spec=_bs((tm, D), row_k), b_spec=_bs((DH, D), lambda i, n, k: (n, k)),
                    r_spec=_bs((tm, DH), row_n), o_spec=_bs((tm, DH), row_n), out_shape=f32(T, D), name="mm_dh1")
    g_wpg, g_wpg16 = _mm(h1b, dz, dims=TN, grid=(D // DH, D // DH, T // tk), bf16_copy=True,
                         a_spec=_bs((tk, DH), col_m), b_spec=_bs((tk, DH), col_n),
                         o_spec=_bs((DH, DH), lambda m, n, k: (m, n)), out_shape=f32(D, D), name="mm_gwpg")
    dcat = _mm(dh1b, wout, dims=NT, grid=(T // tm, 1, 1),
               a_spec=_bs((tm, D), row_k), b_spec=_bs((D, D), lambda i, n, k: (n, k)),
               o_spec=_bs((tm, D), row_n), out_shape=f32(T, D), name="mm_dcat")
    g_wout, g_wout16 = _mm(cat, dh1b, dims=TN, grid=(D // DH, D // DH, T // tk), bf16_copy=True,
                           a_spec=_bs((tk, DH), col_m), b_spec=_bs((tk, DH), col_n),
                           o_spec=_bs((DH, DH), lambda m, n, k: (m, n)), out_shape=f32(D, D), name="mm_gwout")
    gbig = {"w_out": g_wout.reshape(N_CHIP, Q, D), "w_ple": g_wple, "w_ple_gate": g_wpg.reshape(N_CHIP, Q, D)}
    gbig16 = {"w_out": g_wout16.reshape(N_CHIP, Q, D), "w_ple": g_wple16,
              "w_ple_gate": g_wpg16.reshape(N_CHIP, Q, D)}
    first = list(gbig)
    res = _gate_bwd(dcat, mixed, proj, hg, pool_scale, tb, comm=_halves_plan([gbig16[n] for n in first]))
    dmixed, dpg, dsg, dhg, dps = res[:5]
    got = dict(zip(first, res[5:]))
    dge = _mm(dhg, wglu, dims=NT, grid=(T // tm, 1, 1),
              a_spec=_bs((tm, 2 * P), row_k), b_spec=_bs((P, 2 * P), lambda i, n, k: (n, k)),
              o_spec=_bs((tm, P), row_n), out_shape=f32(T, P), name="mm_dge")
    gbig["w_glu"], gbig16["w_glu"] = _mm(ge, dhg, dims=TN, grid=(1, N_CHIP, T // tk), bf16_copy=True,
                                         a_spec=_bs((tk, P), col_m), b_spec=_bs((tk, Q), col_n),
                                         o_spec=_bs((None, P, Q), lambda m, j, k: (j, 0, 0)),
                                         out_shape=f32(N_CHIP, P, Q), name="mm_gwglu")
    g_wp = _mm(pooled, dmixed, dims=TN, grid=(NG, 1, T // tk), bf16_copy=True,
               a_spec=_bs((tk, PG), col_m), b_spec=_bs((tk, PG), col_m),
               o_spec=_bs((None, PG, PG), lambda g, n, k: (g, 0, 0)), out_shape=f32(NG, PG, PG), name="mm_gwp")
    by_chip = lambda a: a.reshape(NG, N_CHIP, PG // N_CHIP, PG).transpose(1, 0, 2, 3).reshape(
        N_CHIP, NG * PG // N_CHIP, PG)
    gbig["w_pool"], gbig16["w_pool"] = by_chip(g_wp[0]), by_chip(g_wp[1])
    res = _pool_bwd(dmixed, wp, tb, comm=_halves_plan([gbig16["w_pool"], gbig16["w_glu"]]))
    dpi, got["w_pool"], got["w_glu"] = res
    early = list(gbig)
    chip_sums = {n: _sum_cast(gbig[n], got[n], place, "sum_cast_" + n) for n in early}
    res = _ssm_bwd(proj, y, dge, bsr, bsi, bdr, bdi, cdr, cdi, abr, abi, d_skip, dpi, dpg, dsg, P, tbs,
                   comm=_scatter_plan([chip_sums[n] for n in early]))
    dproj, dabr, dabi, dd, dbdr, dbdi, dcdr, dcdi = res[:8]
    arrived = dict(zip(early, res[8:]))
    halves = [_sum_chips(chip_sums[n], arrived[n], place, "sum_chips_" + n) for n in early]
    res = _mm(hn, dproj, dims=TN, grid=(D // DH, N_CHIP, T // tk), bf16_copy=True,
              a_spec=_bs((tk, DH), col_m), b_spec=_bs((tk, P), col_n),
              o_spec=_bs((None, DH, P), lambda m, j, k: (j, m, 0)), out_shape=f32(N_CHIP, D, P),
              name="mm_gwin", comm=_join_plan(halves))
    gbig["w_in"], gbig16["w_in"], gshard = res[0], res[1], dict(zip(early, res[2:]))
    got["w_in"], = _comm_call(_halves_plan([gbig16["w_in"]]), "rs_halves_late")
    chip_sums["w_in"] = _sum_cast(gbig["w_in"], got["w_in"], place, "sum_cast_w_in")
    KH = _t(4 * P, DEPTH_MATMUL)
    dhn, arrived["w_in"] = _mm(dproj, win, dims=NT, grid=(T // tm, D // DH, 4 * P // KH),
                               a_spec=_bs((tm, KH), row_k), b_spec=_bs((DH, KH), lambda i, n, k: (n, k)),
                               o_spec=_bs((tm, DH), row_n), out_shape=f32(T, D), name="mm_dhn",
                               comm=_scatter_plan([chip_sums["w_in"]]))
    gshard["w_in"], = _comm_call(
        _join_plan([_sum_chips(chip_sums["w_in"], arrived["w_in"], place, "sum_chips_w_in")]), "rs_join_w_in")
    grad_x, dg1 = _norm1_bwd(xs, dhn, dh1, norm_gain, tb)

    dbbt_re = _block_diag_extract(dbdr, GT, C, N).reshape(G * C, N)
    dbbt_im = _block_diag_extract(dbdi, GT, C, N).reshape(G * C, N)
    g_c_re = _block_diag_extract(dcdr, GT, N, C).transpose(0, 2, 1)
    g_c_im = _block_diag_extract(dcdi, GT, N, C).transpose(0, 2, 1)
    dab_re_r = rep(dabr.reshape(G, N)) * (1.0 / C)
    dab_im_r = rep(dabi.reshape(G, N)) * (1.0 / C)
    g_a_re, g_a_im, g_ldt, g_bt_re, g_bt_im = _ssm_prep_bwd(a_re_r, a_im_r, ldt_r, bt_re, bt_im,
                                                            dab_re_r, dab_im_r, dbbt_re, dbbt_im, G)
    g_b_re = g_bt_re.reshape(G, C, N).transpose(0, 2, 1)
    g_b_im = g_bt_im.reshape(G, C, N).transpose(0, 2, 1)


    small_names = ["norm_gain", "pool_scale", "a_re", "a_im", "log_dt", "b_re", "b_im", "c_re", "c_im",
                   "d_skip", "final_gain"]
    small_w = dict(norm_gain=norm_gain, pool_scale=pool_scale, a_re=a_re, a_im=a_im, log_dt=log_dt, b_re=b_re,
                   b_im=b_im, c_re=c_re, c_im=c_im, d_skip=d_skip, final_gain=final_gain)
    small_m = dict(norm_gain=m_norm_gain, pool_scale=m_pool_scale, a_re=m_a_re, a_im=m_a_im, log_dt=m_log_dt,
                   b_re=m_b_re, b_im=m_b_im, c_re=m_c_re, c_im=m_c_im, d_skip=m_d_skip, final_gain=m_final_gain)
    small_v = dict(norm_gain=v_norm_gain, pool_scale=v_pool_scale, a_re=v_a_re, a_im=v_a_im, log_dt=v_log_dt,
                   b_re=v_b_re, b_im=v_b_im, c_re=v_c_re, c_im=v_c_im, d_skip=v_d_skip, final_gain=v_final_gain)
    small_g = dict(norm_gain=dg1, pool_scale=dps, a_re=g_a_re, a_im=g_a_im, log_dt=g_ldt, b_re=g_b_re,
                   b_im=g_b_im, c_re=g_c_re, c_im=g_c_im, d_skip=dd, final_gain=dg2)
    shapes = [small_w[n].shape for n in small_names]
    loss_row = sum(_tile_rows(small_w[n].size) for n in small_names)
    unit = N_DEV * SUBLANES
    rows = -(-(loss_row + SUBLANES) // unit) * unit
    gbuf = _pack_small([small_g[n] for n in small_names] + [lpart[0, :1]], rows)
    gsum, = _comm_call(_allreduce_plan(gbuf), "allreduce_small")
    g_out, d_out, m_out, v_out = {}, {}, {}, {}
    for n in big_names:
        w_, m_, v_ = big[n]
        r2 = shard2d[n]
        res = _adamw(w_.reshape(r2), gshard[n], m_.reshape(r2), v_.reshape(r2), "adamw_" + n)
        g_out[n], d_out[n], m_out[n], v_out[n] = (a.reshape(w_.shape) for a in (gshard[n], *res))
    wbuf = _pack_small([small_w[n] for n in small_names], rows)
    mbuf = _pack_small([small_m[n] for n in small_names], rows)
    vbuf = _pack_small([small_v[n] for n in small_names], rows)
    dsm, msm, vsm = _adamw(wbuf, gsum, mbuf, vbuf, "adamw_small")
    g_small = dict(zip(small_names, _unpack_small(gsum, shapes)))
    d_small = dict(zip(small_names, _unpack_small(dsm, shapes)))
    m_small = dict(zip(small_names, _unpack_small(msm, shapes)))
    v_small = dict(zip(small_names, _unpack_small(vsm, shapes)))
    loss = gsum[loss_row, 0]

    g_out.update(g_small)
    d_out.update(d_small)
    m_out.update(m_small)
    v_out.update(v_small)

    order = ["norm_gain", "w_in", "w_pool", "pool_scale", "a_re", "a_im", "log_dt", "b_re", "b_im", "c_re",
             "c_im", "d_skip", "w_glu", "w_out", "w_ple", "w_ple_gate", "final_gain"]
    return (loss, grad_x[None], *[g_out[n] for n in order], *[d_out[n] for n in order],
            *[m_out[n] for n in order], *[v_out[n] for n in order])
```

```python
import functools

import jax
import jax.numpy as jnp
from jax import lax
from jax.experimental import pallas as pl
from jax.experimental.pallas import tpu as pltpu

F32, BF16 = jnp.float32, jnp.bfloat16
MESH = pl.DeviceIdType.MESH
ANY = pl.BlockSpec(memory_space=pl.ANY)
VMEM_FULL = pl.BlockSpec(memory_space=pltpu.VMEM)

EPS = 1e-6
A_RE_MAX = -1e-4
SSM_GROUP = 16
SSM_STATE = 64
POOL_WINDOWS = (2, 4, 8, 16)
POOL_HALO = 16
ADAM_LR, ADAM_B1, ADAM_B2, ADAM_EPS, ADAM_WD, ADAM_STEP = 0.001, 0.9, 0.999, 1e-08, 0.01, 10

V7X_VMEM_BYTES = 64 * 1024 * 1024
VMEM_LIMIT = V7X_VMEM_BYTES - 8 * 1024 * 1024
SUBLANES, LANES = 8, 128
BF16_TILE_ROWS = 16
SSM_TILE_GROUPS = 8
SCAN_LANES = 1024
N_DEV, N_CHIP = 8, 4
DMA_CHUNK_BYTES = 256 * 1024
DMA_MAX_CHUNKS = 32
RS_CHUNKS = 8
STREAM_BUFFERS = 3
ROWS_ELEMENTWISE = 256
ROWS_SSM = 256
ROWS_MATMUL = 1024
DEPTH_MATMUL = 4096
AG_FORWARD_SCALE, AG_FORWARD_LAG = 0.85, 0.05


def _t(n, pref):
    return pref if n % pref == 0 else n


def _cp(sem=None, vmem=VMEM_LIMIT):
    return pltpu.CompilerParams(dimension_semantics=sem, vmem_limit_bytes=vmem)


def _call(body, **kw):
    return pl.pallas_call(body, **kw)


NN = ((1,), (0,))
NT = ((1,), (1,))
TN = ((0,), (0,))


def _mm(a, b, *, dims, grid, a_spec, b_spec, o_spec, out_shape, name, res=None, r_spec=None, bf16_copy=False,
        comm=None):
    nk, kax = grid[-1], len(grid) - 1
    acc_shape = tuple(d for d in o_spec.block_shape if d is not None)

    def core(*refs):
        refs = list(refs)
        a_ref, b_ref = refs[:2]
        r_ref = refs[2] if res is not None else None
        outs = refs[3 if res is not None else 2:]
        o_ref = outs[0]
        o2_ref = outs[1] if bf16_copy else None
        acc = outs[-1] if nk > 1 else None

        def finish(r):
            if r_ref is not None:
                r = r + r_ref[...]
            o_ref[...] = r.astype(o_ref.dtype)
            if o2_ref is not None:
                o2_ref[...] = r.astype(BF16)

        part = lax.dot_general(a_ref[...].astype(BF16), b_ref[...].astype(BF16),
                               (dims, ((), ())), preferred_element_type=F32)
        if nk == 1:
            finish(part)
        else:
            k = pl.program_id(kax)

            @pl.when(k == 0)
            def _():
                acc[...] = part

            @pl.when(k > 0)
            def _():
                acc[...] += part

            @pl.when(k == nk - 1)
            def _():
                finish(acc[...])

    ins, specs = [a, b], [a_spec, b_spec]
    if res is not None:
        ins.append(res)
        specs.append(r_spec)
    o_specs, o_shapes = [o_spec], [out_shape]
    if bf16_copy:
        o_specs = [o_spec, o_spec]
        o_shapes = [out_shape, jax.ShapeDtypeStruct(out_shape.shape, BF16)]
    scratch = [pltpu.VMEM(acc_shape, F32)] if nk > 1 else []
    body, extra = _hosted(core, comm, grid, len(ins), len(o_specs), len(scratch))
    sem = ("arbitrary",) * len(grid) if comm else ("parallel",) * kax + ("arbitrary",)
    outs = _call(body, grid=grid, in_specs=specs + extra["in_specs"], out_specs=o_specs + extra["out_specs"],
                 out_shape=o_shapes + extra["out_shape"], scratch_shapes=scratch + extra["scratch"],
                 input_output_aliases=extra["aliases"],
                 compiler_params=_cp(sem), name=name)(*ins, *extra["ins"])
    return outs[0] if len(outs) == 1 else outs


def _bs(shape, fn):
    return pl.BlockSpec(shape, fn)


def _ring_fetch(hbm, ring, sems, step, nsteps, rows):
    n = len(hbm)

    def copy(k, s):
        slot = s % STREAM_BUFFERS
        src, c0 = hbm[k] if isinstance(hbm[k], tuple) else (hbm[k], 0)
        window = (pl.ds(pl.multiple_of(s * rows, SUBLANES), rows), pl.ds(c0, ring[k].shape[2]))
        return pltpu.make_async_copy(src.at[window], ring[k].at[slot], sems.at[k, slot])

    @pl.when(step == 0)
    def _():
        for s in range(min(STREAM_BUFFERS - 1, nsteps)):
            for k in range(n):
                copy(k, s).start()

    @pl.when(step + STREAM_BUFFERS - 1 < nsteps)
    def _():
        for k in range(n):
            copy(k, step + STREAM_BUFFERS - 1).start()

    for k in range(n):
        copy(k, step).wait()
    return [ring[k].at[step % STREAM_BUFFERS] for k in range(n)]


def _sigmoid(v):
    return 1.0 / (1.0 + jnp.exp(-v))


def _gelu(v):
    return 0.5 * v * (1.0 + jnp.tanh(0.7978845608028654 * (v + 0.044715 * v * v * v)))


def _gelu_grad(v):
    t = jnp.tanh(0.7978845608028654 * (v + 0.044715 * v * v * v))
    return 0.5 * (1.0 + t) + 0.5 * v * (1.0 - t * t) * 0.7978845608028654 * (1.0 + 3 * 0.044715 * v * v)


def _norm1_bwd(x, dhn, dh1, g1, tb, comm=None):
    T, D = x.shape
    nb = T // tb

    def core(x_hbm, dhn_hbm, dh1_hbm, g_ref, dx_ref, dg_ref, rx, rdhn, rdh1, sems):
        @pl.when(pl.program_id(0) == 0)
        def _():
            dg_ref[...] = jnp.zeros_like(dg_ref)

        x_ref, dhn_ref, dh1_ref = _ring_fetch((x_hbm, dhn_hbm, dh1_hbm), (rx, rdhn, rdh1), sems,
                                              pl.program_id(0), nb, tb)
        xv = x_ref[...]
        r = lax.rsqrt(jnp.mean(xv * xv, axis=-1, keepdims=True) + EPS)
        xh = xv * r
        dhn_v = dhn_ref[...]
        dg_ref[...] += jnp.sum(dhn_v * xh, axis=0, keepdims=True)
        dxh = dhn_v * g_ref[...]
        dx_ref[...] = dh1_ref[...] + r * (dxh - xh * jnp.mean(dxh * xh, axis=-1, keepdims=True))

    row = _bs((tb, D), lambda i: (i, 0))
    vec = _bs((1, D), lambda i: (0, 0))
    body, extra = _hosted(core, comm, (nb,), 4, 2, 4)
    ring = pltpu.VMEM((STREAM_BUFFERS, tb, D), F32)
    return _call(body, grid=(nb,), in_specs=[ANY, ANY, ANY, vec] + extra["in_specs"],
                 out_specs=[row, vec] + extra["out_specs"],
                 out_shape=[jax.ShapeDtypeStruct((T, D), F32), jax.ShapeDtypeStruct((1, D), F32)] + extra["out_shape"],
                 scratch_shapes=[ring, ring, ring, pltpu.SemaphoreType.DMA((3, STREAM_BUFFERS))] + extra["scratch"],
                 input_output_aliases=extra["aliases"],
                 compiler_params=_cp(("arbitrary",)), name="norm1_bwd")(x, dhn, dh1, g1, *extra["ins"])


def _gate_fwd(mixed, proj, hg, ps, tb):
    T, P = mixed.shape
    nb = T // tb

    def body(mx_hbm, proj_hbm, hg_hbm, ps_ref, o_ref, rmx, rpg, rsg, rhg, sems):
        mx_ref, pg_ref, sg_ref, hg_ref = _ring_fetch((mx_hbm, (proj_hbm, P), (proj_hbm, 3 * P), hg_hbm),
                                                     (rmx, rpg, rsg, rhg), sems, pl.program_id(0), nb, tb)
        pg, sg = pg_ref[...], sg_ref[...]
        ya = (mx_ref[...] * ps_ref[...]) * (pg * _sigmoid(pg))
        hgv = hg_ref[...]
        o = hgv[:, :P] * _sigmoid(hgv[:, P:])
        yb = o * (sg * _sigmoid(sg))
        o_ref[:, :P] = ya.astype(BF16)
        o_ref[:, P:] = yb.astype(BF16)

    ring = lambda w: pltpu.VMEM((STREAM_BUFFERS, tb, w), F32)
    return _call(body, grid=(nb,),
                 in_specs=[ANY, ANY, ANY, _bs((1, P), lambda i: (0, 0))],
                 out_specs=_bs((tb, 2 * P), lambda i: (i, 0)),
                 out_shape=jax.ShapeDtypeStruct((T, 2 * P), BF16),
                 scratch_shapes=[ring(P), ring(P), ring(P), ring(2 * P), pltpu.SemaphoreType.DMA((4, STREAM_BUFFERS))],
                 compiler_params=_cp(("arbitrary",)), name="gate_fwd")(mixed, proj, hg, ps)


def _gate_bwd(dcat, mixed, proj, hg, ps, tb, comm=None):
    T, P = mixed.shape

    def core(dc_ref, mx_ref, pg_ref, sg_ref, hg_ref, ps_ref, dmx_ref, dpg_ref, dsg_ref, dhg_ref, dps_ref):
        @pl.when(pl.program_id(0) == 0)
        def _():
            dps_ref[...] = jnp.zeros_like(dps_ref)

        dc = dc_ref[...]
        dya, dyb = dc[:, :P], dc[:, P:]
        pg, sg, mx, psv = pg_ref[...], sg_ref[...], mx_ref[...], ps_ref[...]
        s_pg = _sigmoid(pg)
        dpa = dya * (pg * s_pg)
        dpg_ref[...] = (dya * (mx * psv) * (s_pg * (1.0 + pg * (1.0 - s_pg)))).astype(BF16)
        dps_ref[...] += jnp.sum(dpa * mx, axis=0, keepdims=True)
        dmx_ref[...] = (dpa * psv).astype(BF16)
        hgv = hg_ref[...]
        h1, s_h2 = hgv[:, :P], _sigmoid(hgv[:, P:])
        s_sg = _sigmoid(sg)
        do = dyb * (sg * s_sg)
        dsg_ref[...] = (dyb * (h1 * s_h2) * (s_sg * (1.0 + sg * (1.0 - s_sg)))).astype(BF16)
        dhg_ref[:, :P] = (do * s_h2).astype(BF16)
        dhg_ref[:, P:] = (do * h1 * s_h2 * (1.0 - s_h2)).astype(BF16)

    rowp = _bs((tb, P), lambda i: (i, 0))
    row2 = _bs((tb, 2 * P), lambda i: (i, 0))
    vec = _bs((1, P), lambda i: (0, 0))
    body, extra = _hosted(core, comm, (T // tb,), 6, 5, 0)
    return _call(body, grid=(T // tb,),
                 in_specs=[_bs((tb, 2 * P), lambda i: (i, 0)), _bs((tb, P), lambda i: (i, 0)),
                           _bs((tb, P), lambda i: (i, 1)), _bs((tb, P), lambda i: (i, 3)),
                           _bs((tb, 2 * P), lambda i: (i, 0)), vec] + extra["in_specs"],
                 out_specs=[rowp, rowp, rowp, row2, vec] + extra["out_specs"],
                 out_shape=[jax.ShapeDtypeStruct((T, P), BF16), jax.ShapeDtypeStruct((T, P), BF16),
                            jax.ShapeDtypeStruct((T, P), BF16), jax.ShapeDtypeStruct((T, 2 * P), BF16),
                            jax.ShapeDtypeStruct((1, P), F32)] + extra["out_shape"],
                 scratch_shapes=extra["scratch"],
                 compiler_params=_cp(("arbitrary",)), name="gate_bwd")(dcat, mixed, proj, proj, hg, ps, *extra["ins"])


def _final_fb(h1, pe, wple, z, tgt, g2, tb):
    T, D = h1.shape
    E = pe.shape[1]
    Q = D // N_CHIP
    nb = T // tb

    def body(h1_hbm, p_ref, w_ref, z_hbm, t_hbm, g_ref, dh2_ref, dz_ref, dg_ref, l_ref, gw_ref, gw16_ref, acc,
             rh1, rz, rt, sems):
        @pl.when(pl.program_id(0) == 0)
        def _():
            dg_ref[...] = jnp.zeros_like(dg_ref)
            l_ref[...] = jnp.zeros_like(l_ref)
            acc[...] = jnp.zeros_like(acc)

        h1_ref, z_ref, t_ref = _ring_fetch((h1_hbm, z_hbm, t_hbm), (rh1, rz, rt), sems, pl.program_id(0), nb, tb)
        pb = p_ref[...].astype(BF16)
        ev = jnp.dot(pb, w_ref[...], preferred_element_type=F32)
        s = _sigmoid(z_ref[...])
        h2 = h1_ref[...] + ev * s
        r = lax.rsqrt(jnp.mean(h2 * h2, axis=-1, keepdims=True) + EPS)
        xh = h2 * r
        gv = g_ref[...]
        diff = xh * gv - t_ref[...]
        l_ref[...] += 0.5 * jnp.sum(jnp.mean(diff * diff, axis=-1, keepdims=True))
        dout = diff * (1.0 / D)
        dg_ref[...] += jnp.sum(dout * xh, axis=0, keepdims=True)
        dxh = dout * gv
        dh2 = r * (dxh - xh * jnp.mean(dxh * xh, axis=-1, keepdims=True))
        dh2_ref[...] = dh2
        dz_ref[...] = (dh2 * ev * s * (1.0 - s)).astype(BF16)
        acc[...] += lax.dot_general(pb, (dh2 * s).astype(BF16), (TN, ((), ())), preferred_element_type=F32)

        @pl.when(pl.program_id(0) == nb - 1)
        def _():
            for j in range(N_CHIP):
                slab = acc[:, j * Q:(j + 1) * Q]
                gw_ref[j] = slab
                gw16_ref[j] = slab.astype(BF16)

    row = _bs((tb, D), lambda i: (i, 0))
    vec = _bs((1, D), lambda i: (0, 0))
    slabs = _bs((N_CHIP, E, Q), lambda i: (0, 0, 0))
    ring = pltpu.VMEM((STREAM_BUFFERS, tb, D), F32)
    return _call(body, grid=(nb,),
                 in_specs=[ANY, _bs((tb, E), lambda i: (i, 0)), _bs((E, D), lambda i: (0, 0)), ANY, ANY, vec],
                 out_specs=[row, row, vec, _bs((1, LANES), lambda i: (0, 0)), slabs, slabs],
                 out_shape=[jax.ShapeDtypeStruct((T, D), F32), jax.ShapeDtypeStruct((T, D), BF16),
                            jax.ShapeDtypeStruct((1, D), F32), jax.ShapeDtypeStruct((1, LANES), F32),
                            jax.ShapeDtypeStruct((N_CHIP, E, Q), F32), jax.ShapeDtypeStruct((N_CHIP, E, Q), BF16)],
                 scratch_shapes=[pltpu.VMEM((E, D), F32), ring, ring, ring,
                                 pltpu.SemaphoreType.DMA((3, STREAM_BUFFERS))],
                 compiler_params=_cp(("arbitrary",)), name="final_fb")(h1, pe, wple, z, tgt, g2)


def _pool_inv_count(t0, rows, pg, ngroups):
    t = t0 + lax.broadcasted_iota(jnp.int32, (rows, pg), 0)
    parts = []
    for w in POOL_WINDOWS[:ngroups]:
        parts.append(jnp.where(t + 1 >= w, 1.0 / w, 1.0 / (t + 1).astype(F32)))
    return parts


def _pool_fwd(proj, wp, P, tb):
    T = proj.shape[0]
    ng = len(POOL_WINDOWS)
    pg = P // ng
    hb = tb // POOL_HALO

    def body(v_ref, tail_ref, w_ref, o_ref, mx_ref, ext):
        i = pl.program_id(0)
        ext[pl.ds(0, POOL_HALO), :] = jnp.where(i > 0, tail_ref[...], 0.0)
        ext[pl.ds(POOL_HALO, tb), :] = v_ref[...]
        inv = _pool_inv_count(i * tb, tb, pg, ng)
        for g, w in enumerate(POOL_WINDOWS):
            cols = pl.ds(g * pg, pg)
            win = ext[pl.ds(POOL_HALO, tb), cols]
            for k in range(1, w):
                win = win + ext[pl.ds(POOL_HALO - k, tb), cols]
            pooled = (win * inv[g] - ext[pl.ds(POOL_HALO, tb), cols]).astype(BF16)
            o_ref[:, cols] = pooled
            mx_ref[:, cols] = jnp.dot(pooled, w_ref[g], preferred_element_type=F32)

    row = _bs((tb, P), lambda i: (i, 0))
    return _call(body, grid=(T // tb,),
                 in_specs=[row, _bs((POOL_HALO, P), lambda i: (jnp.maximum(i * hb - 1, 0), 0)),
                           _bs(wp.shape, lambda i: (0, 0, 0))],
                 out_specs=[row, row],
                 out_shape=[jax.ShapeDtypeStruct((T, P), BF16), jax.ShapeDtypeStruct((T, P), F32)],
                 scratch_shapes=[pltpu.VMEM((tb + POOL_HALO, P), F32)],
                 compiler_params=_cp(("arbitrary",)), name="pool_fwd")(proj, proj, wp)


def _pool_bwd(dmixed, wp, tb, comm=None):
    T, P = dmixed.shape
    ng = len(POOL_WINDOWS)
    pg = P // ng
    hb = tb // POOL_HALO
    nb = T // tb

    def core(d_ref, head_ref, w_ref, o_ref, ext, dpl):
        i = pl.program_id(0)
        inv = _pool_inv_count(i * tb, tb, pg, ng)
        invh = _pool_inv_count((i + 1) * tb, POOL_HALO, pg, ng)
        for g in range(ng):
            cols = pl.ds(g * pg, pg)
            dp = lax.dot_general(d_ref[:, cols], w_ref[g], (NT, ((), ())), preferred_element_type=F32)
            dph = lax.dot_general(head_ref[:, cols], w_ref[g], (NT, ((), ())), preferred_element_type=F32)
            dpl[:, cols] = dp
            ext[pl.ds(0, tb), cols] = dp * inv[g]
            ext[pl.ds(tb, POOL_HALO), cols] = jnp.where(i < nb - 1, dph * invh[g], 0.0)
        for g, w in enumerate(POOL_WINDOWS):
            cols = pl.ds(g * pg, pg)
            acc = ext[pl.ds(0, tb), cols]
            for k in range(1, w):
                acc = acc + ext[pl.ds(k, tb), cols]
            o_ref[:, cols] = (acc - dpl[:, cols]).astype(BF16)

    body, extra = _hosted(core, comm, (nb,), 3, 1, 2)
    return _call(body, grid=(nb,),
                 in_specs=[_bs((tb, P), lambda i: (i, 0)),
                           _bs((POOL_HALO, P), lambda i: (jnp.minimum((i + 1) * hb, T // POOL_HALO - 1), 0)),
                           _bs(wp.shape, lambda i: (0, 0, 0))] + extra["in_specs"],
                 out_specs=[_bs((tb, P), lambda i: (i, 0))] + extra["out_specs"],
                 out_shape=[jax.ShapeDtypeStruct((T, P), BF16)] + extra["out_shape"],
                 scratch_shapes=[pltpu.VMEM((tb + POOL_HALO, P), F32), pltpu.VMEM((tb, P), F32)] + extra["scratch"],
                 compiler_params=_cp(("arbitrary",)), name="pool_bwd")(dmixed, dmixed, wp, *extra["ins"])


def _zoh(a_re, a_im, ldt, b_re, b_im):
    lam_re = jnp.minimum(a_re, A_RE_MAX)
    lam_im = a_im
    dt = jnp.exp(ldt)
    mag = jnp.exp(lam_re * dt)
    ang = lam_im * dt
    ab_re = mag * jnp.cos(ang)
    ab_im = mag * jnp.sin(ang)
    den = lam_re * lam_re + lam_im * lam_im
    n_re = ab_re - 1.0
    n_im = ab_im
    q_re = (n_re * lam_re + n_im * lam_im) / den
    q_im = (n_im * lam_re - n_re * lam_im) / den
    return ab_re, ab_im, q_re * b_re - q_im * b_im, q_re * b_im + q_im * b_re


def _ssm_prep(a_re, a_im, ldt, bt_re, bt_im):
    shp = jax.ShapeDtypeStruct(a_re.shape, F32)

    def body(a, b, c, d, e, o0, o1, o2, o3):
        r = _zoh(a[...], b[...], c[...], d[...], e[...])
        o0[...], o1[...], o2[...], o3[...] = r

    return _call(body, in_specs=[VMEM_FULL] * 5, out_specs=[VMEM_FULL] * 4, out_shape=[shp] * 4,
                 name="ssm_prep")(a_re, a_im, ldt, bt_re, bt_im)


def _ssm_prep_bwd(a_re, a_im, ldt, bt_re, bt_im, dab_re, dab_im, dbb_re, dbb_im, G):
    GC, N = a_re.shape
    C = GC // G

    def body(a, b, c, d, e, g0, g1, g2, g3, da_re, da_im, dldt, db_re, db_im):
        _, vjp = jax.vjp(_zoh, a[...], b[...], c[...], d[...], e[...])
        ga_re, ga_im, gl, gb_re, gb_im = vjp((g0[...], g1[...], g2[...], g3[...]))
        da_re[...] = jnp.sum(ga_re.reshape(G, C, N), axis=1)
        da_im[...] = jnp.sum(ga_im.reshape(G, C, N), axis=1)
        dldt[...] = jnp.sum(jnp.sum(gl.reshape(G, C, N), axis=1), axis=1, keepdims=True)
        db_re[...] = gb_re
        db_im[...] = gb_im

    gn = jax.ShapeDtypeStruct((G, N), F32)
    full = jax.ShapeDtypeStruct((GC, N), F32)
    return _call(body, in_specs=[VMEM_FULL] * 9, out_specs=[VMEM_FULL] * 5,
                 out_shape=[gn, gn, jax.ShapeDtypeStruct((G, 1), F32), full, full],
                 name="ssm_prep_bwd")(a_re, a_im, ldt, bt_re, bt_im, dab_re, dab_im, dbb_re, dbb_im)


def _coef_tiles(abr, abi, reverse):
    ns = abr.shape[1]
    row = lax.broadcasted_iota(jnp.int32, (SUBLANES, ns), 0)
    ar = jnp.broadcast_to(abr, (SUBLANES, ns))
    ai = jnp.broadcast_to(-abi if reverse else abi, (SUBLANES, ns))
    a2r, a2i = ar * ar - ai * ai, 2.0 * ar * ai
    a4r, a4i = a2r * a2r - a2i * a2i, 2.0 * a2r * a2i
    out = []
    for d, (vr, vi) in ((1, (ar, ai)), (2, (a2r, a2i)), (4, (a4r, a4i))):
        keep = (row < SUBLANES - d) if reverse else (row >= d)
        out += [jnp.where(keep, vr, 0.0), jnp.where(keep, vi, 0.0)]
    pr, pi = ar, ai
    for k in range(1, SUBLANES):
        sel = (row <= SUBLANES - 1 - k) if reverse else (row >= k)
        nr, ni = pr * ar - pi * ai, pr * ai + pi * ar
        pr, pi = jnp.where(sel, nr, pr), jnp.where(sel, ni, pi)
    return out + [pr, pi]


def _cpow(ar, ai, n):
    out, br, bi = None, ar, ai
    while n:
        if n & 1:
            out = (br, bi) if out is None else (out[0] * br - out[1] * bi, out[0] * bi + out[1] * br)
        br, bi = br * br - bi * bi, 2.0 * br * bi
        n >>= 1
    return out


def _seg_perm_matrix(nrows):
    r = jnp.arange(nrows)
    src = (nrows // SUBLANES) * (r % SUBLANES) + r // SUBLANES
    return (src[:, None] == jnp.arange(nrows)[None, :]).astype(BF16)


def _seg_order_rows(pm, xb):
    return jnp.dot(pm, xb, preferred_element_type=F32).astype(BF16)


def _time_order_rows(pm, x, terms):
    out, rest = None, x
    for t in range(terms):
        piece = rest.astype(BF16)
        part = lax.dot_general(pm, piece, (TN, ((), ())), preferred_element_type=F32)
        out = part if out is None else out + part
        if t + 1 < terms:
            rest = rest - piece.astype(F32)
    return out


def _scan_tiles(abr, abi, seg, reverse):
    ns = abr.shape[1]
    seg_pow = _cpow(abr, abi, seg)
    step = [jnp.broadcast_to(abr, (SUBLANES, ns)), jnp.broadcast_to(-abi if reverse else abi, (SUBLANES, ns))]
    return _coef_tiles(seg_pow[0], seg_pow[1], reverse) + step


def _seg_scan(xr_ref, xi_ref, coef_ref, car_ref, cai_ref, *, nrows, ns, reverse, cmat=None, dab=None):
    seg = nrows // SUBLANES
    cw = min(SCAN_LANES, ns)
    row = lax.broadcasted_iota(jnp.int32, (SUBLANES, cw), 0)
    first, last = (SUBLANES - 1, 0) if reverse else (0, SUBLANES - 1)

    def tile(i):
        return pl.ds(pl.multiple_of(((seg - 1 - i) if reverse else i) * SUBLANES, SUBLANES), SUBLANES)

    for cc in range(ns // cw):
        cols = pl.ds(cc * cw, cw)
        ar, ai = coef_ref[8, :, cols], coef_ref[9, :, cols]

        def local(i, x, cols=cols, ar=ar, ai=ai):
            rows = tile(i)
            nr = ar * x[0] - ai * x[1] + xr_ref[rows, cols]
            ni = ar * x[1] + ai * x[0] + xi_ref[rows, cols]
            xr_ref[rows, cols] = nr
            xi_ref[rows, cols] = ni
            return nr, ni

        zero = jnp.zeros((SUBLANES, cw), F32)
        er, ei = lax.fori_loop(0, seg, local, (zero, zero))

        co = [coef_ref[k, :, cols] for k in range(8)]
        for lvl, d in enumerate((1, 2, 4)):
            kr, ki = co[2 * lvl], co[2 * lvl + 1]
            sh = SUBLANES - d if reverse else d
            sr, si = pltpu.roll(er, sh, 0), pltpu.roll(ei, sh, 0)
            er, ei = er + (kr * sr - ki * si), ei + (kr * si + ki * sr)
        c0r, c0i = car_ref[:, cols], cai_ref[:, cols]
        er, ei = er + (co[6] * c0r - co[7] * c0i), ei + (co[6] * c0i + co[7] * c0r)
        nb_shift = SUBLANES - 1 if reverse else 1
        cmr = jnp.where(row == first, c0r, pltpu.roll(er, nb_shift, 0))
        cmi = jnp.where(row == first, c0i, pltpu.roll(ei, nb_shift, 0))
        car_ref[:, cols] = jnp.broadcast_to(er[last:last + 1, :], er.shape)
        cai_ref[:, cols] = jnp.broadcast_to(ei[last:last + 1, :], ei.shape)
        if cmat is not None:
            cmat[0][:, cols] = cmr
            cmat[1][:, cols] = cmi

        w0 = (ar * cmr - ai * cmi, ar * cmi + ai * cmr)
        if dab is None:
            def fix(i, w, cols=cols, ar=ar, ai=ai):
                rows = tile(i)
                xr_ref[rows, cols] = xr_ref[rows, cols] + w[0]
                xi_ref[rows, cols] = xi_ref[rows, cols] + w[1]
                return ar * w[0] - ai * w[1], ar * w[1] + ai * w[0]

            lax.fori_loop(0, seg, fix, w0)
        else:
            s_re, s_im, e_re, e_im, o_re, o_im = dab

            def add(rows, w, pr, pi, acc):
                gr = xr_ref[rows, cols] + w[0]
                gi = xi_ref[rows, cols] + w[1]
                xr_ref[rows, cols] = gr
                xi_ref[rows, cols] = gi
                return acc[0] + (gr * pr + gi * pi), acc[1] + (gi * pr - gr * pi)

            def fix(i, st, cols=cols, ar=ar, ai=ai):
                w, acc = st[:2], st[2:]
                rows = tile(i)
                before = pl.ds(pl.multiple_of((seg - 2 - i) * SUBLANES, SUBLANES), SUBLANES)
                acc = add(rows, w, s_re[before, cols], s_im[before, cols], acc)
                return (ar * w[0] - ai * w[1], ar * w[1] + ai * w[0]) + acc

            st = lax.fori_loop(0, seg - 1, fix, w0 + (zero, zero))
            acc = add(pl.ds(0, SUBLANES), st[:2], e_re[:, cols], e_im[:, cols], st[2:])
            o_re[:, cols] += jnp.sum(acc[0], axis=0, keepdims=True)
            o_im[:, cols] += jnp.sum(acc[1], axis=0, keepdims=True)


def _hosted(core, comm, grid, n_in, n_out, n_scratch):
    ci = len(comm["ins"]) if comm else 0
    co = len(comm["out_shape"]) if comm else 0

    def body(*refs):
        ins, rest = refs[:n_in + ci], refs[n_in + ci:]
        outs, scr = rest[:n_out + co], rest[n_out + co:]
        hooks = functools.partial(_comm_hooks, comm, grid, ins[n_in:], outs[n_out:], scr[n_scratch:])
        hooks(before=True)
        core(*ins[:n_in], *outs[:n_out], *scr[:n_scratch])
        hooks(before=False)

    aliases = {n_in + i: n_out + i for i in range(co)} if comm and comm.get("alias") else {}
    extra = dict(ins=list(comm["ins"]) if comm else [], in_specs=[ANY] * ci, out_specs=[ANY] * co,
                 out_shape=list(comm["out_shape"]) if comm else [], scratch=list(comm["scratch"]) if comm else [],
                 aliases=aliases)
    return body, extra


def _ssm_fwd(proj, bdr, bdi, cdr, cdi, abr, abi, dsk, P, tb, comm=None):
    T = proj.shape[0]
    ntl, ct, st = bdr.shape
    ns = ntl * st
    nb = T // tb

    def core(u_ref, bdr_ref, bdi_ref, cdr_ref, cdi_ref, abr_ref, abi_ref, d_ref, pm_ref,
             y_ref, ge_ref, bsr_ref, bsi_ref, sr, si, coef, car, cai, up):
        @pl.when(pl.program_id(0) == 0)
        def _():
            for k, tile in enumerate(_scan_tiles(abr_ref[...], abi_ref[...], tb // SUBLANES, False)):
                coef[k] = tile
            car[...] = jnp.zeros_like(car)
            cai[...] = jnp.zeros_like(cai)

        bsr_ref[...] = car[...]
        bsi_ref[...] = cai[...]
        u = u_ref[...]
        ub = _seg_order_rows(pm_ref[...], u.astype(BF16))
        for s in range(ntl):
            us = ub[:, s * ct:(s + 1) * ct]
            sr[:, s * st:(s + 1) * st] = jnp.dot(us, bdr_ref[s], preferred_element_type=F32)
            si[:, s * st:(s + 1) * st] = jnp.dot(us, bdi_ref[s], preferred_element_type=F32)
        _seg_scan(sr, si, coef, car, cai, nrows=tb, ns=ns, reverse=False)
        for s in range(ntl):
            s_re = sr[:, s * st:(s + 1) * st].astype(BF16)
            s_im = si[:, s * st:(s + 1) * st].astype(BF16)
            up[:, s * ct:(s + 1) * ct] = (jnp.dot(s_re, cdr_ref[s], preferred_element_type=F32)
                                          - jnp.dot(s_im, cdi_ref[s], preferred_element_type=F32))
        y = _time_order_rows(pm_ref[...], up[...], 3) + d_ref[...] * u
        y_ref[...] = y
        ge_ref[...] = _gelu(y).astype(BF16)

    full3 = lambda a: _bs(a.shape, lambda i: (0, 0, 0))
    vec = lambda n: _bs((1, n), lambda i: (0, 0))
    row = _bs((tb, P), lambda i: (i, 0))
    st_spec = _bs((None, SUBLANES, ns), lambda i: (i, 0, 0))
    body, extra = _hosted(core, comm, (nb,), 9, 4, 6)
    return _call(body, grid=(nb,),
                 in_specs=[_bs((tb, P), lambda i: (i, 2)), full3(bdr), full3(bdi), full3(cdr), full3(cdi),
                           vec(ns), vec(ns), vec(P), _bs((tb, tb), lambda i: (0, 0))] + extra["in_specs"],
                 out_specs=[row, row, st_spec, st_spec] + extra["out_specs"],
                 out_shape=[jax.ShapeDtypeStruct((T, P), F32), jax.ShapeDtypeStruct((T, P), BF16),
                            jax.ShapeDtypeStruct((nb, SUBLANES, ns), F32),
                            jax.ShapeDtypeStruct((nb, SUBLANES, ns), F32)] + extra["out_shape"],
                 scratch_shapes=[pltpu.VMEM((tb, ns), F32), pltpu.VMEM((tb, ns), F32),
                                 pltpu.VMEM((10, SUBLANES, ns), F32),
                                 pltpu.VMEM((SUBLANES, ns), F32), pltpu.VMEM((SUBLANES, ns), F32),
                                 pltpu.VMEM((tb, P), F32)] + extra["scratch"],
                 compiler_params=_cp(("arbitrary",)), name="ssm_fwd")(
                     proj, bdr, bdi, cdr, cdi, abr, abi, dsk, _seg_perm_matrix(tb), *extra["ins"])


def _ssm_bwd(proj, y, dge, bsr, bsi, bdr, bdi, cdr, cdi, abr, abi, dsk, dpi, dpg, dsg, P, tb, comm=None):
    T = proj.shape[0]
    ntl, ct, st = bdr.shape
    ns = ntl * st
    nb = T // tb

    def core(u_ref, y_ref, dge_ref, bsr_ref, bsi_ref, abr_ref, abi_ref, d_ref, pm_ref, dpi_ref, dpg_ref, dsg_ref,
             bdr_h, bdi_h, cdr_h, cdi_h,
             dproj_ref, dabr_ref, dabi_ref, dd_ref, dbdr_h, dbdi_h, dcdr_h, dcdi_h,
             wbdr, wbdi, wcdr, wcdi, abdr, abdi, acdr, acdi, spr, spi, gr, gi, coef_f, coef_r,
             car, cai, gcr, gci, ser, sei, dup):
        i = pl.program_id(0)

        @pl.when(i == 0)
        def _():
            for h, w in ((bdr_h, wbdr), (bdi_h, wbdi), (cdr_h, wcdr), (cdi_h, wcdi)):
                pltpu.sync_copy(h, w)
            for a in (abdr, abdi, acdr, acdi, gcr, gci):
                a[...] = jnp.zeros_like(a)
            for o in (dabr_ref, dabi_ref, dd_ref):
                o[...] = jnp.zeros_like(o)
            for k, tile in enumerate(_scan_tiles(abr_ref[...], abi_ref[...], tb // SUBLANES, False)):
                coef_f[k] = tile
            for k, tile in enumerate(_scan_tiles(abr_ref[...], abi_ref[...], tb // SUBLANES, True)):
                coef_r[k] = tile

        car[...] = bsr_ref[...]
        cai[...] = bsi_ref[...]
        u = u_ref[...]
        dy = dge_ref[...] * _gelu_grad(y_ref[...])
        ub = _seg_order_rows(pm_ref[...], u.astype(BF16))
        dyb = _seg_order_rows(pm_ref[...], dy.astype(BF16))
        for s in range(ntl):
            us = ub[:, s * ct:(s + 1) * ct]
            spr[:, s * st:(s + 1) * st] = jnp.dot(us, wbdr[s], preferred_element_type=F32)
            spi[:, s * st:(s + 1) * st] = jnp.dot(us, wbdi[s], preferred_element_type=F32)
        _seg_scan(spr, spi, coef_f, car, cai, nrows=tb, ns=ns, reverse=False, cmat=(ser, sei))

        for s in range(ntl):
            dys = dyb[:, s * ct:(s + 1) * ct]
            gr[:, s * st:(s + 1) * st] = lax.dot_general(dys, wcdr[s], (NT, ((), ())), preferred_element_type=F32)
            gi[:, s * st:(s + 1) * st] = -lax.dot_general(dys, wcdi[s], (NT, ((), ())), preferred_element_type=F32)
        _seg_scan(gr, gi, coef_r, gcr, gci, nrows=tb, ns=ns, reverse=True,
                  dab=(spr, spi, ser, sei, dabr_ref, dabi_ref))

        for s in range(ntl):
            sl_c, sl_s = slice(s * ct, (s + 1) * ct), slice(s * st, (s + 1) * st)
            s_re = spr[:, sl_s].astype(BF16)
            s_im = spi[:, sl_s].astype(BF16)
            g_re, g_im = gr[:, sl_s].astype(BF16), gi[:, sl_s].astype(BF16)
            dys, us = dyb[:, sl_c], ub[:, sl_c]
            acdr[s] += lax.dot_general(s_re, dys, (TN, ((), ())), preferred_element_type=F32)
            acdi[s] -= lax.dot_general(s_im, dys, (TN, ((), ())), preferred_element_type=F32)
            abdr[s] += lax.dot_general(us, g_re, (TN, ((), ())), preferred_element_type=F32)
            abdi[s] += lax.dot_general(us, g_im, (TN, ((), ())), preferred_element_type=F32)
            dup[:, sl_c] = (lax.dot_general(g_re, wbdr[s], (NT, ((), ())), preferred_element_type=F32)
                            + lax.dot_general(g_im, wbdi[s], (NT, ((), ())), preferred_element_type=F32))
        dd_ref[...] += jnp.sum(dy * u, axis=0, keepdims=True)
        du = _time_order_rows(pm_ref[...], dup[...], 2) + d_ref[...] * dy
        dproj_ref[:, 0:P] = dpi_ref[...]
        dproj_ref[:, P:2 * P] = dpg_ref[...]
        dproj_ref[:, 2 * P:3 * P] = du.astype(BF16)
        dproj_ref[:, 3 * P:4 * P] = dsg_ref[...]

        @pl.when(i == nb - 1)
        def _():
            for a, h in ((abdr, dbdr_h), (abdi, dbdi_h), (acdr, dcdr_h), (acdi, dcdi_h)):
                pltpu.sync_copy(a, h)

    rev = lambda i: nb - 1 - i
    vec = lambda n: _bs((1, n), lambda i: (0, 0))
    row = _bs((tb, P), lambda i: (rev(i), 0))
    st_spec = _bs((None, SUBLANES, ns), lambda i: (rev(i), 0, 0))
    bshape = jax.ShapeDtypeStruct(bdr.shape, F32)
    cshape = jax.ShapeDtypeStruct(cdr.shape, F32)
    body, extra = _hosted(core, comm, (nb,), 16, 8, 21)
    return _call(body, grid=(nb,),
                 in_specs=[_bs((tb, P), lambda i: (rev(i), 2)), row, row, st_spec, st_spec,
                           vec(ns), vec(ns), vec(P), _bs((tb, tb), lambda i: (0, 0)), row, row, row,
                           ANY, ANY, ANY, ANY] + extra["in_specs"],
                 out_specs=[_bs((tb, 4 * P), lambda i: (rev(i), 0)), vec(ns), vec(ns), vec(P), ANY, ANY, ANY, ANY]
                 + extra["out_specs"],
                 out_shape=[jax.ShapeDtypeStruct((T, 4 * P), BF16), jax.ShapeDtypeStruct((1, ns), F32),
                            jax.ShapeDtypeStruct((1, ns), F32), jax.ShapeDtypeStruct((1, P), F32),
                            bshape, bshape, cshape, cshape] + extra["out_shape"],
                 scratch_shapes=[pltpu.VMEM(bdr.shape, BF16), pltpu.VMEM(bdr.shape, BF16),
                                 pltpu.VMEM(cdr.shape, BF16), pltpu.VMEM(cdr.shape, BF16),
                                 pltpu.VMEM(bdr.shape, F32), pltpu.VMEM(bdr.shape, F32),
                                 pltpu.VMEM(cdr.shape, F32), pltpu.VMEM(cdr.shape, F32),
                                 pltpu.VMEM((tb, ns), F32), pltpu.VMEM((tb, ns), F32),
                                 pltpu.VMEM((tb, ns), F32), pltpu.VMEM((tb, ns), F32),
                                 pltpu.VMEM((10, SUBLANES, ns), F32), pltpu.VMEM((10, SUBLANES, ns), F32)]
                 + [pltpu.VMEM((SUBLANES, ns), F32)] * 6 + [pltpu.VMEM((tb, P), F32)] + extra["scratch"],
                 compiler_params=_cp(("arbitrary",)), name="ssm_bwd")(
                     proj, y, dge, bsr, bsi, abr, abi, dsk, _seg_perm_matrix(tb), dpi, dpg, dsg,
                     bdr, bdi, cdr, cdi, *extra["ins"])


def _adamw(w, g, m, v, name, comm=None):
    R, C = w.shape
    tr = _t(R, ROWS_ELEMENTWISE)

    def core(w_ref, g_ref, m_ref, v_ref, d_ref, mo_ref, vo_ref):
        gv = g_ref[...]
        mn = ADAM_B1 * m_ref[...] + (1.0 - ADAM_B1) * gv
        vn = ADAM_B2 * v_ref[...] + (1.0 - ADAM_B2) * (gv * gv)
        m_hat = mn / (1.0 - ADAM_B1 ** ADAM_STEP)
        v_hat = vn / (1.0 - ADAM_B2 ** ADAM_STEP)
        d_ref[...] = -ADAM_LR * (m_hat / (jnp.sqrt(v_hat) + ADAM_EPS) + ADAM_WD * w_ref[...])
        mo_ref[...] = mn
        vo_ref[...] = vn

    blk = _bs((tr, C), lambda i: (i, 0))
    shp = jax.ShapeDtypeStruct((R, C), F32)
    body, extra = _hosted(core, comm, (R // tr,), 4, 3, 0)
    return _call(body, grid=(R // tr,), in_specs=[blk] * 4 + extra["in_specs"],
                 out_specs=[blk] * 3 + extra["out_specs"], out_shape=[shp] * 3 + extra["out_shape"],
                 scratch_shapes=extra["scratch"],
                 compiler_params=_cp(("arbitrary",) if comm else ("parallel",)), name=name)(w, g, m, v, *extra["ins"])


def _sum_cast(grad, got, place, name):
    J, H, C = got.shape
    tr = _t(H, ROWS_ELEMENTWISE)
    nb = H // tr

    def body(pl_ref, a_ref, b_ref, o_ref):
        o_ref[...] = (a_ref[...] + b_ref[...]).astype(BF16)

    blk = _bs((None, tr, C), lambda j, i, pc: (j, i, 0))
    mine = _bs((None, tr, C), lambda j, i, pc: (j, pc[1] * nb + i, 0))
    spec = pltpu.PrefetchScalarGridSpec(num_scalar_prefetch=1, grid=(J, nb), in_specs=[mine, blk], out_specs=blk)
    return _call(body, grid_spec=spec, out_shape=jax.ShapeDtypeStruct((J, H, C), BF16),
                 compiler_params=_cp(("parallel", "parallel")), name=name)(place, grad, got)


def _sum_chips(sent, arrived, place, name):
    J, H, C = arrived.shape
    tr = _t(H, ROWS_ELEMENTWISE)
    nb = H // tr

    def body(pl_ref, own_ref, a0_ref, a1_ref, a2_ref, o_ref):
        acc = own_ref[...].astype(F32)
        for r in (a0_ref, a1_ref, a2_ref):
            acc = acc + r[...].astype(F32)
        o_ref[...] = acc

    def other(k):
        return _bs((None, tr, C), lambda i, pc: (jnp.where(pc[0] <= k, k + 1, k), i, 0))

    spec = pltpu.PrefetchScalarGridSpec(
        num_scalar_prefetch=1, grid=(nb,),
        in_specs=[_bs((None, tr, C), lambda i, pc: (pc[0], i, 0)), other(0), other(1), other(2)],
        out_specs=_bs((tr, C), lambda i, pc: (pc[1] * nb + i, 0)))
    return _call(body, grid_spec=spec, out_shape=jax.ShapeDtypeStruct((2 * H, C), F32),
                 compiler_params=_cp(("parallel",)), name=name)(place, sent, arrived, arrived, arrived)


def _place():
    x, y, c = lax.axis_index("x"), lax.axis_index("y"), lax.axis_index("c")
    chips = [(1 - x, y), (x, 1 - y), (1 - x, 1 - y)]
    return x, y, c, chips


def _split(nrows, row_bytes, align, cap=None):
    k = max(1, min(cap or DMA_MAX_CHUNKS, (nrows * row_bytes) // DMA_CHUNK_BYTES))
    while k > 1 and nrows % (k * align):
        k -= 1
    return k


def _comm_call(plan, name):
    n_in, n_out = len(plan["ins"]), len(plan["out_shape"])

    def body(*refs):
        for phase in plan["phases"]:
            phase(refs[:n_in], refs[n_in:n_in + n_out], refs[n_in + n_out:])

    return _call(body, in_specs=[ANY] * n_in, out_specs=[ANY] * n_out, out_shape=plan["out_shape"],
                 input_output_aliases={i: i for i in range(n_out)} if plan.get("alias") else {},
                 scratch_shapes=plan["scratch"], name=name)(*plan["ins"])


def _comm_hooks(plan, grid, ins, outs, sems, *, before):
    if plan is None:
        return
    nsteps, step = 1, 0
    for d, g in enumerate(grid):
        nsteps, step = nsteps * g, step * g + pl.program_id(d)
    for p, (phase, frac) in enumerate(zip(plan["phases"], plan["at"])):
        if (p == 0) == before:
            pl.when(step == int(frac * (nsteps - 1)))(functools.partial(phase, ins, outs, sems))


def _ag_plan(shards, axes):
    n = len(shards)
    shapes = [a.shape for a in shards]

    def window(ref, i, chip, half=None):
        S, ax = shapes[i], axes[i]
        idx = []
        for d in range(len(S)):
            off, size = 0, S[d]
            if d == 0 and half is not None:
                off, size = half * (S[0] // 2), S[0] // 2
            if d == ax:
                off = off + chip * S[ax]
            idx.append(pl.ds(off, size))
        return ref.at[tuple(idx)]

    def copies(src, full, sems):
        ssem, rsem = sems
        x, y, c, chips = _place()
        me = 2 * x + y
        sib = (x, y, 1 - c)
        idx = [2 * cx + cy for cx, cy in chips]

        def rcopy(i, k, s_ref, d_ref, to):
            return pltpu.make_async_remote_copy(src_ref=s_ref, dst_ref=d_ref, send_sem=ssem.at[i, k],
                                                recv_sem=rsem.at[i, k], device_id=to, device_id_type=MESH)

        def ici(i, j, incoming):
            half_src = src[i].at[pl.ds(c * (shapes[i][0] // 2), shapes[i][0] // 2)]
            return rcopy(i, j, half_src, window(full[i], i, idx[j] if incoming else me, c), (*chips[j], c))

        def fwd(i, j, half):
            w = window(full[i], i, idx[j], half)
            return rcopy(i, 3 + j, w, w, sib)

        def own(i):
            return rcopy(i, 6, src[i], window(full[i], i, me), sib)

        return c, ici, fwd, own

    def send(src, full, sems):
        c, ici, fwd, own = copies(src, full, sems)
        for i in range(n):
            for j in range(3):
                ici(i, j, False).start()
        for i in range(n):
            own(i).start()

    def forward(i, src, full, sems):
        c, ici, fwd, own = copies(src, full, sems)
        for j in range(3):
            ici(i, j, True).wait_recv()
            fwd(i, j, c).start()

    def finish(src, full, sems):
        c, ici, fwd, own = copies(src, full, sems)
        for i in range(n):
            for j in range(3):
                fwd(i, j, 1 - c).wait_recv()
            own(i).wait()
        for i in range(n):
            for j in range(3):
                ici(i, j, False).wait_send()
                fwd(i, j, c).wait_send()

    out_shape = [jax.ShapeDtypeStruct(tuple(N_CHIP * d if k == ax else d for k, d in enumerate(S)), BF16)
                 for S, ax in zip(shapes, axes)]
    sizes = [a.size for a in shards]
    behind = [AG_FORWARD_SCALE * sum(sizes[:i + 1]) / sum(sizes) + AG_FORWARD_LAG for i in range(n)]
    return dict(ins=list(shards), out_shape=out_shape,
                phases=[send] + [functools.partial(forward, i) for i in range(n)] + [finish],
                at=[0.0] + behind + [1.0],
                scratch=[pltpu.SemaphoreType.DMA((n, 7)), pltpu.SemaphoreType.DMA((n, 7))])


def _proj_ag(x, g1, wsh, order, tm):
    T, D = x.shape
    P = wsh.shape[1]
    H = D // 2
    nt = T // tm

    def body(order_ref, x_ref, g_ref, wsh_ref, hn_ref, proj_ref, win_ref, wbuf, lsem, ssem, rsem):
        n, i = pl.program_id(0), pl.program_id(1)
        x, y, c, chips = _place()
        me = 2 * x + y
        sib = (x, y, 1 - c)
        idx = [2 * cx + cy for cx, cy in chips]

        def rcopy(k, s_ref, d_ref, to):
            return pltpu.make_async_remote_copy(src_ref=s_ref, dst_ref=d_ref, send_sem=ssem.at[k],
                                                recv_sem=rsem.at[k], device_id=to, device_id_type=MESH)

        def cols(chip):
            return pl.ds(pl.multiple_of(chip * P, LANES), P)

        def rows(half):
            return pl.ds(pl.multiple_of(half * H, BF16_TILE_ROWS), H)

        def ici(j, incoming):
            return rcopy(j, wsh_ref.at[rows(c)], win_ref.at[rows(c), cols(idx[j] if incoming else me)],
                         (*chips[j], c))

        def fwd(j, half):
            w = win_ref.at[rows(half), cols(idx[j])]
            return rcopy(3 + j, w, w, sib)

        def own():
            return rcopy(6, wsh_ref, win_ref.at[:, cols(me)], sib)

        def load(src):
            cp = pltpu.make_async_copy(src, wbuf, lsem)
            cp.start()
            cp.wait()

        @pl.when((n == 0) & (i == 0))
        def _():
            ici(0, False).start()
            ici(1, False).start()
            own().start()
            load(wsh_ref)

        for j in range(3):
            @pl.when((n == j + 1) & (i == 0))
            def _(j=j):
                if j == 0:
                    ici(2, False).start()
                ici(j, True).wait_recv()
                fwd(j, c).start()
                fwd(j, 1 - c).wait_recv()
                load(win_ref.at[:, cols(idx[j])])

        xv = x_ref[...]
        r = lax.rsqrt(jnp.mean(xv * xv, axis=-1, keepdims=True) + EPS)
        hn = ((xv * r) * g_ref[...]).astype(BF16)

        @pl.when(n == 0)
        def _():
            hn_ref[...] = hn

        proj_ref[...] = jnp.dot(hn, wbuf[...], preferred_element_type=F32)

        @pl.when((n == 3) & (i == nt - 1))
        def _():
            own().wait()
            for j in range(3):
                ici(j, False).wait_send()
                fwd(j, c).wait_send()

    spec = pltpu.PrefetchScalarGridSpec(
        num_scalar_prefetch=1, grid=(N_CHIP, nt),
        in_specs=[_bs((tm, D), lambda n, i, o: (i, 0)), _bs((1, D), lambda n, i, o: (0, 0)), ANY],
        out_specs=[_bs((tm, D), lambda n, i, o: (jnp.where(n == 0, i, nt - 1), 0)),
                   _bs((tm, P), lambda n, i, o: (i, o[n])), ANY],
        scratch_shapes=[pltpu.VMEM((D, P), BF16), pltpu.SemaphoreType.DMA,
                        pltpu.SemaphoreType.DMA((7,)), pltpu.SemaphoreType.DMA((7,))])
    return _call(body, grid_spec=spec,
                 out_shape=[jax.ShapeDtypeStruct((T, D), BF16), jax.ShapeDtypeStruct((T, N_CHIP * P), F32),
                            jax.ShapeDtypeStruct((D, N_CHIP * P), BF16)],
                 compiler_params=_cp(("arbitrary", "arbitrary")), name="proj_ag")(order, x, g1, wsh)


def _halves_plan(grads):
    n = len(grads)

    def send(g, got, sems):
        ssem, rsem = sems
        x, y, c, _ = _place()
        sib = (x, y, 1 - c)
        for i in range(n):
            J, R, C = g[i].shape
            H = R // 2
            size = g[i].dtype.itemsize
            tile_rows = SUBLANES * 4 // size
            k = _split(H, C * size, tile_rows, cap=DMA_MAX_CHUNKS // J)
            hr = H // k
            for j in range(J):
                for q in range(k):
                    other = pl.ds(pl.multiple_of((1 - c) * H + q * hr, tile_rows), hr)
                    to = pl.ds(q * hr, hr)
                    pltpu.make_async_remote_copy(src_ref=g[i].at[j, other, :], dst_ref=got[i].at[j, to, :],
                                                 send_sem=ssem.at[i], recv_sem=rsem.at[i],
                                                 device_id=sib, device_id_type=MESH).start()

    def finish(g, got, sems):
        ssem, rsem = sems
        x, y, c, _ = _place()
        for i in range(n):
            pltpu.make_async_remote_copy(src_ref=got[i], dst_ref=got[i], send_sem=ssem.at[i], recv_sem=rsem.at[i],
                                         device_id=(x, y, 1 - c), device_id_type=MESH).wait()

    half = [jax.ShapeDtypeStruct((a.shape[0], a.shape[1] // 2, a.shape[2]), a.dtype) for a in grads]
    return dict(ins=list(grads), out_shape=half, phases=[send, finish], at=[0.0, 1.0],
                scratch=[pltpu.SemaphoreType.DMA((n,)), pltpu.SemaphoreType.DMA((n,))])


def _scatter_plan(parts):
    n = len(parts)

    def peers():
        x, y, c, chips = _place()
        return 2 * x + y, c, chips, [2 * cx + cy for cx, cy in chips]

    def send(s, got, sems):
        ssem, rsem = sems
        me, c, chips, idx = peers()
        for i in range(n):
            _, H, C = s[i].shape
            k = _split(H, C * 2, BF16_TILE_ROWS, cap=RS_CHUNKS)
            hr = H // k
            for q in range(k):
                rows = pl.ds(q * hr, hr)
                for j in range(3):
                    pltpu.make_async_remote_copy(src_ref=s[i].at[idx[j], rows, :], dst_ref=got[i].at[me, rows, :],
                                                 send_sem=ssem.at[i, j], recv_sem=rsem.at[i, j],
                                                 device_id=(*chips[j], c), device_id_type=MESH).start()

    def finish(s, got, sems):
        ssem, rsem = sems
        me, c, chips, idx = peers()
        for i in range(n):
            for j in range(3):
                pltpu.make_async_remote_copy(src_ref=s[i].at[idx[j]], dst_ref=got[i].at[idx[j]],
                                             send_sem=ssem.at[i, j], recv_sem=rsem.at[i, j],
                                             device_id=(*chips[j], c), device_id_type=MESH).wait()

    return dict(ins=list(parts), out_shape=[jax.ShapeDtypeStruct(a.shape, a.dtype) for a in parts],
                phases=[send, finish], at=[0.0, 1.0],
                scratch=[pltpu.SemaphoreType.DMA((n, 3)), pltpu.SemaphoreType.DMA((n, 3))])


def _join_plan(shards):
    n = len(shards)

    def send(_, full, sems):
        ssem, rsem = sems
        x, y, c, _ = _place()
        sib = (x, y, 1 - c)
        for i in range(n):
            H, C = full[i].shape[0] // 2, full[i].shape[1]
            k = _split(H, C * 4, SUBLANES)
            hr = H // k
            for q in range(k):
                rows = pl.ds(pl.multiple_of(c * H + q * hr, SUBLANES), hr)
                pltpu.make_async_remote_copy(src_ref=full[i].at[rows], dst_ref=full[i].at[rows],
                                             send_sem=ssem.at[i], recv_sem=rsem.at[i],
                                             device_id=sib, device_id_type=MESH).start()

    def finish(_, full, sems):
        ssem, rsem = sems
        x, y, c, _ = _place()
        for i in range(n):
            half = full[i].at[pl.ds(0, full[i].shape[0] // 2)]
            pltpu.make_async_remote_copy(src_ref=half, dst_ref=half, send_sem=ssem.at[i], recv_sem=rsem.at[i],
                                         device_id=(x, y, 1 - c), device_id_type=MESH).wait()

    return dict(ins=list(shards), out_shape=[jax.ShapeDtypeStruct(a.shape, a.dtype) for a in shards],
                phases=[send, finish], at=[0.0, 1.0], alias=True,
                scratch=[pltpu.SemaphoreType.DMA((n,)), pltpu.SemaphoreType.DMA((n,))])


def _allreduce_plan(buf):
    R, L = buf.shape
    RB = R // N_DEV

    def parts(sems):
        xv, got, ov, lsem, ssem, rsem = sems
        x, y, c, _ = _place()
        me = 4 * x + 2 * y + c

        def dev(k):
            return (k // 4, (k // 2) % 2, k % 2)

        def slab(k):
            return pl.ds(pl.multiple_of(k * RB, SUBLANES), RB)

        def first(d, to, landing):
            return pltpu.make_async_remote_copy(src_ref=xv.at[slab(to)], dst_ref=got.at[landing],
                                                send_sem=ssem.at[0, d], recv_sem=rsem.at[0, d],
                                                device_id=dev(to), device_id_type=MESH)

        def second(d, to, k):
            return pltpu.make_async_remote_copy(src_ref=ov.at[slab(k)], dst_ref=ov.at[slab(k)],
                                                send_sem=ssem.at[1, d], recv_sem=rsem.at[1, d],
                                                device_id=dev(to), device_id_type=MESH)

        return me, slab, first, second

    def scatter(ins, outs, sems):
        xv, lsem = sems[0], sems[3]
        me, slab, first, second = parts(sems)
        cp = pltpu.make_async_copy(ins[0], xv, lsem)
        cp.start()
        cp.wait()
        for d in range(1, N_DEV):
            first(d, (me + d) % N_DEV, me).start()

    def reduce(ins, outs, sems):
        xv, got, ov = sems[:3]
        me, slab, first, second = parts(sems)
        got[me] = xv[slab(me), :]
        for d in range(1, N_DEV):
            src = (me + N_DEV - d) % N_DEV
            first(d, src, src).wait_recv()
        acc = got[0]
        for k in range(1, N_DEV):
            acc = acc + got[k]
        ov[slab(me), :] = acc
        for d in range(1, N_DEV):
            second(d, (me + d) % N_DEV, me).start()

    def collect(ins, outs, sems):
        ov, lsem = sems[2], sems[3]
        me, slab, first, second = parts(sems)
        for d in range(1, N_DEV):
            src = (me + N_DEV - d) % N_DEV
            second(d, src, src).wait_recv()
        for d in range(1, N_DEV):
            peer = (me + d) % N_DEV
            first(d, peer, me).wait_send()
            second(d, peer, me).wait_send()
        cp = pltpu.make_async_copy(ov, outs[0], lsem)
        cp.start()
        cp.wait()

    return dict(ins=[buf], out_shape=[jax.ShapeDtypeStruct((R, L), F32)], phases=[scatter, reduce, collect],
                at=[0.0, 0.5, 1.0],
                scratch=[pltpu.VMEM((R, L), F32), pltpu.VMEM((N_DEV, RB, L), F32), pltpu.VMEM((R, L), F32),
                         pltpu.SemaphoreType.DMA, pltpu.SemaphoreType.DMA((2, N_DEV)),
                         pltpu.SemaphoreType.DMA((2, N_DEV))])


def _block_diag(t, gt):
    G, A, B = t.shape
    t4 = t.reshape(G // gt, gt, A, B)
    eye = jnp.eye(gt, dtype=t.dtype)
    return jnp.einsum('sgab,gh->sgahb', t4, eye).reshape(G // gt, gt * A, gt * B)


def _block_diag_extract(m, gt, A, B):
    S = m.shape[0]
    m5 = m.reshape(S, gt, A, gt, B)
    eye = jnp.eye(gt, dtype=m.dtype)
    return jnp.einsum('sgahb,gh->sgab', m5, eye).reshape(S * gt, A, B)


def _tile_rows(n):
    return -(-n // (SUBLANES * LANES)) * SUBLANES


def _pack_small(arrs, rows):
    parts = []
    for a in arrs:
        flat = a.reshape(-1).astype(F32)
        r = _tile_rows(flat.shape[0])
        parts.append(jnp.pad(flat, (0, r * LANES - flat.shape[0])).reshape(r, LANES))
    used = sum(p.shape[0] for p in parts)
    if rows > used:
        parts.append(jnp.zeros((rows - used, LANES), F32))
    return jnp.concatenate(parts)


def _unpack_small(buf, shapes):
    out, off = [], 0
    for s in shapes:
        n = 1
        for d in s:
            n *= d
        r = _tile_rows(n)
        piece = buf[off:off + r]
        out.append(piece.reshape(s) if n == r * LANES else piece.reshape(-1)[:n].reshape(s))
        off += r
    return out


def kernel(x, p, norm_gain, w_in, w_pool, pool_scale, a_re, a_im, log_dt, b_re, b_im, c_re, c_im, d_skip, w_glu, w_out, w_ple, w_ple_gate, final_gain, loss_target, m_norm_gain, m_w_in, m_w_pool, m_pool_scale, m_a_re, m_a_im, m_log_dt, m_b_re, m_b_im, m_c_re, m_c_im, m_d_skip, m_w_glu, m_w_out, m_w_ple, m_w_ple_gate, m_final_gain, v_norm_gain, v_w_in, v_w_pool, v_pool_scale, v_a_re, v_a_im, v_log_dt, v_b_re, v_b_im, v_c_re, v_c_im, v_d_skip, v_w_glu, v_w_out, v_w_ple, v_w_ple_gate, v_final_gain):
    xs, pe, tgt = x[0], p[0, 0], loss_target[0]
    T, D = xs.shape
    E = pe.shape[1]
    P = D // 2
    NG = len(POOL_WINDOWS)
    PG = P // NG
    G, N, C = P // SSM_GROUP, SSM_STATE, SSM_GROUP
    GT = min(SSM_TILE_GROUPS, G)
    Q = D // N_CHIP

    big = {"w_in": (w_in, m_w_in, v_w_in), "w_pool": (w_pool, m_w_pool, v_w_pool),
           "w_glu": (w_glu, m_w_glu, v_w_glu), "w_out": (w_out, m_w_out, v_w_out),
           "w_ple": (w_ple, m_w_ple, v_w_ple), "w_ple_gate": (w_ple_gate, m_w_ple_gate, v_w_ple_gate)}
    big_names = list(big)
    shard2d = {n: (big[n][0].size // big[n][0].shape[-1], big[n][0].shape[-1]) for n in big_names}
    shard_axis = {"w_in": 1, "w_pool": 1, "w_glu": 1, "w_out": 0, "w_ple": 1, "w_ple_gate": 0}
    shard16 = {n: big[n][0][0].astype(BF16) for n in big_names}
    place = jnp.stack([2 * lax.axis_index("x") + lax.axis_index("y"), lax.axis_index("c")]).astype(jnp.int32)
    mx, my = lax.axis_index("x"), lax.axis_index("y")
    block_order = jnp.stack([2 * mx + my, 2 * (1 - mx) + my, 2 * mx + (1 - my),
                             2 * (1 - mx) + (1 - my)]).astype(jnp.int32)
    later = [n for n in big_names if n != "w_in"]
    ag_later = _ag_plan([shard16[n] for n in later], [shard_axis[n] for n in later])

    rep = lambda a: jnp.repeat(a, C, axis=0)
    a_re_r, a_im_r = rep(a_re[0]), rep(a_im[0])
    ldt_r = rep(jnp.broadcast_to(log_dt[0][:, None], (G, N)))
    bt_re = b_re[0].transpose(0, 2, 1).reshape(G * C, N)
    bt_im = b_im[0].transpose(0, 2, 1).reshape(G * C, N)
    ab_re_r, ab_im_r, bbt_re, bbt_im = _ssm_prep(a_re_r, a_im_r, ldt_r, bt_re, bt_im)
    abr = ab_re_r[::C].reshape(1, G * N)
    abi = ab_im_r[::C].reshape(1, G * N)
    bdr = _block_diag(bbt_re.reshape(G, C, N), GT).astype(BF16)
    bdi = _block_diag(bbt_im.reshape(G, C, N), GT).astype(BF16)
    cdr = _block_diag(c_re[0].transpose(0, 2, 1), GT).astype(BF16)
    cdi = _block_diag(c_im[0].transpose(0, 2, 1), GT).astype(BF16)

    tb = _t(T, ROWS_ELEMENTWISE)
    tbs = _t(T, ROWS_SSM)
    tm = _t(T, ROWS_MATMUL)
    tk = _t(T, DEPTH_MATMUL)
    DH = _t(D, ROWS_MATMUL)
    row_k = lambda i, n, k: (i, k)
    row_n = lambda i, n, k: (i, n)
    f32 = lambda *shape: jax.ShapeDtypeStruct(shape, F32)
    hn, proj, win = _proj_ag(xs, norm_gain, shard16["w_in"], block_order, tm)
    y, ge, bsr, bsi, wp, wglu, wout, wple, wpg = _ssm_fwd(proj, bdr, bdi, cdr, cdi, abr, abi, d_skip, P, tbs,
                                                          comm=ag_later)
    pooled, mixed = _pool_fwd(proj, wp, P, tb)
    hg = _mm(ge, wglu, dims=NN, grid=(T // tm, 1, 1),
             a_spec=_bs((tm, P), row_k), b_spec=_bs((P, 2 * P), lambda i, n, k: (k, n)),
             o_spec=_bs((tm, 2 * P), row_n), out_shape=f32(T, 2 * P), name="mm_glu")
    cat = _gate_fwd(mixed, proj, hg, pool_scale, tb)
    h1, h1b = _mm(cat, wout, dims=NN, grid=(T // tm, D // DH, 1), res=xs, bf16_copy=True,
                  a_spec=_bs((tm, D), row_k), b_spec=_bs((D, DH), lambda i, n, k: (k, n)),
                  r_spec=_bs((tm, DH), row_n), o_spec=_bs((tm, DH), row_n), out_shape=f32(T, D), name="mm_out")
    z = _mm(h1b, wpg, dims=NN, grid=(T // tm, 1, 1),
            a_spec=_bs((tm, D), row_k), b_spec=_bs((D, D), lambda i, n, k: (k, n)),
            o_spec=_bs((tm, D), row_n), out_shape=f32(T, D), name="mm_pgate")
    dh2, dz, dg2, lpart, g_wple, g_wple16 = _final_fb(h1, pe, wple, z, tgt, final_gain.reshape(1, D), tb)

    col_m = lambda m, n, k: (k, m)
    col_n = lambda m, n, k: (k, n)
    dh1, dh1b = _mm(dz, wpg, dims=NT, grid=(T // tm, D // DH, 1), res=dh2, bf16_copy=True,
                    a_spec=_bs((tm, D), row_k), b_spec=_bs((DH, D), lambda i, n, k: (n, k)),
                    r_spec=_bs((tm, DH), row_n), o_spec=_bs((tm, DH), row_n), out_shape=f32(T, D), name="mm_dh1")
    g_wpg, g_wpg16 = _mm(h1b, dz, dims=TN, grid=(D // DH, D // DH, T // tk), bf16_copy=True,
                         a_spec=_bs((tk, DH), col_m), b_spec=_bs((tk, DH), col_n),
                         o_spec=_bs((DH, DH), lambda m, n, k: (m, n)), out_shape=f32(D, D), name="mm_gwpg")
    dcat = _mm(dh1b, wout, dims=NT, grid=(T // tm, 1, 1),
               a_spec=_bs((tm, D), row_k), b_spec=_bs((D, D), lambda i, n, k: (n, k)),
               o_spec=_bs((tm, D), row_n), out_shape=f32(T, D), name="mm_dcat")
    g_wout, g_wout16 = _mm(cat, dh1b, dims=TN, grid=(D // DH, D // DH, T // tk), bf16_copy=True,
                           a_spec=_bs((tk, DH), col_m), b_spec=_bs((tk, DH), col_n),
                           o_spec=_bs((DH, DH), lambda m, n, k: (m, n)), out_shape=f32(D, D), name="mm_gwout")
    gbig = {"w_out": g_wout.reshape(N_CHIP, Q, D), "w_ple": g_wple, "w_ple_gate": g_wpg.reshape(N_CHIP, Q, D)}
    gbig16 = {"w_out": g_wout16.reshape(N_CHIP, Q, D), "w_ple": g_wple16,
              "w_ple_gate": g_wpg16.reshape(N_CHIP, Q, D)}
    first = list(gbig)
    res = _gate_bwd(dcat, mixed, proj, hg, pool_scale, tb, comm=_halves_plan([gbig16[n] for n in first]))
    dmixed, dpg, dsg, dhg, dps = res[:5]
    got = dict(zip(first, res[5:]))
    dge = _mm(dhg, wglu, dims=NT, grid=(T // tm, 1, 1),
              a_spec=_bs((tm, 2 * P), row_k), b_spec=_bs((P, 2 * P), lambda i, n, k: (n, k)),
              o_spec=_bs((tm, P), row_n), out_shape=f32(T, P), name="mm_dge")
    gbig["w_glu"], gbig16["w_glu"] = _mm(ge, dhg, dims=TN, grid=(1, N_CHIP, T // tk), bf16_copy=True,
                                         a_spec=_bs((tk, P), col_m), b_spec=_bs((tk, Q), col_n),
                                         o_spec=_bs((None, P, Q), lambda m, j, k: (j, 0, 0)),
                                         out_shape=f32(N_CHIP, P, Q), name="mm_gwglu")
    g_wp = _mm(pooled, dmixed, dims=TN, grid=(NG, 1, T // tk), bf16_copy=True,
               a_spec=_bs((tk, PG), col_m), b_spec=_bs((tk, PG), col_m),
               o_spec=_bs((None, PG, PG), lambda g, n, k: (g, 0, 0)), out_shape=f32(NG, PG, PG), name="mm_gwp")
    by_chip = lambda a: a.reshape(NG, N_CHIP, PG // N_CHIP, PG).transpose(1, 0, 2, 3).reshape(
        N_CHIP, NG * PG // N_CHIP, PG)
    gbig["w_pool"], gbig16["w_pool"] = by_chip(g_wp[0]), by_chip(g_wp[1])
    res = _pool_bwd(dmixed, wp, tb, comm=_halves_plan([gbig16["w_pool"], gbig16["w_glu"]]))
    dpi, got["w_pool"], got["w_glu"] = res
    early = list(gbig)
    chip_sums = {n: _sum_cast(gbig[n], got[n], place, "sum_cast_" + n) for n in early}
    res = _ssm_bwd(proj, y, dge, bsr, bsi, bdr, bdi, cdr, cdi, abr, abi, d_skip, dpi, dpg, dsg, P, tbs,
                   comm=_scatter_plan([chip_sums[n] for n in early]))
    dproj, dabr, dabi, dd, dbdr, dbdi, dcdr, dcdi = res[:8]
    arrived = dict(zip(early, res[8:]))
    halves = [_sum_chips(chip_sums[n], arrived[n], place, "sum_chips_" + n) for n in early]
    res = _mm(hn, dproj, dims=TN, grid=(D // DH, N_CHIP, T // tk), bf16_copy=True,
              a_spec=_bs((tk, DH), col_m), b_spec=_bs((tk, P), col_n),
              o_spec=_bs((None, DH, P), lambda m, j, k: (j, m, 0)), out_shape=f32(N_CHIP, D, P),
              name="mm_gwin", comm=_join_plan(halves))
    gbig["w_in"], gbig16["w_in"], gshard = res[0], res[1], dict(zip(early, res[2:]))
    got["w_in"], = _comm_call(_halves_plan([gbig16["w_in"]]), "rs_halves_late")
    chip_sums["w_in"] = _sum_cast(gbig["w_in"], got["w_in"], place, "sum_cast_w_in")
    KH = _t(4 * P, DEPTH_MATMUL)
    dhn, arrived["w_in"] = _mm(dproj, win, dims=NT, grid=(T // tm, D // DH, 4 * P // KH),
                               a_spec=_bs((tm, KH), row_k), b_spec=_bs((DH, KH), lambda i, n, k: (n, k)),
                               o_spec=_bs((tm, DH), row_n), out_shape=f32(T, D), name="mm_dhn",
                               comm=_scatter_plan([chip_sums["w_in"]]))
    gshard["w_in"], = _comm_call(
        _join_plan([_sum_chips(chip_sums["w_in"], arrived["w_in"], place, "sum_chips_w_in")]), "rs_join_w_in")
    grad_x, dg1 = _norm1_bwd(xs, dhn, dh1, norm_gain, tb)

    dbbt_re = _block_diag_extract(dbdr, GT, C, N).reshape(G * C, N)
    dbbt_im = _block_diag_extract(dbdi, GT, C, N).reshape(G * C, N)
    g_c_re = _block_diag_extract(dcdr, GT, N, C).transpose(0, 2, 1)
    g_c_im = _block_diag_extract(dcdi, GT, N, C).transpose(0, 2, 1)
    dab_re_r = rep(dabr.reshape(G, N)) * (1.0 / C)
    dab_im_r = rep(dabi.reshape(G, N)) * (1.0 / C)
    g_a_re, g_a_im, g_ldt, g_bt_re, g_bt_im = _ssm_prep_bwd(a_re_r, a_im_r, ldt_r, bt_re, bt_im,
                                                            dab_re_r, dab_im_r, dbbt_re, dbbt_im, G)
    g_b_re = g_bt_re.reshape(G, C, N).transpose(0, 2, 1)
    g_b_im = g_bt_im.reshape(G, C, N).transpose(0, 2, 1)


    small_names = ["norm_gain", "pool_scale", "a_re", "a_im", "log_dt", "b_re", "b_im", "c_re", "c_im",
                   "d_skip", "final_gain"]
    small_w = dict(norm_gain=norm_gain, pool_scale=pool_scale, a_re=a_re, a_im=a_im, log_dt=log_dt, b_re=b_re,
                   b_im=b_im, c_re=c_re, c_im=c_im, d_skip=d_skip, final_gain=final_gain)
    small_m = dict(norm_gain=m_norm_gain, pool_scale=m_pool_scale, a_re=m_a_re, a_im=m_a_im, log_dt=m_log_dt,
                   b_re=m_b_re, b_im=m_b_im, c_re=m_c_re, c_im=m_c_im, d_skip=m_d_skip, final_gain=m_final_gain)
    small_v = dict(norm_gain=v_norm_gain, pool_scale=v_pool_scale, a_re=v_a_re, a_im=v_a_im, log_dt=v_log_dt,
                   b_re=v_b_re, b_im=v_b_im, c_re=v_c_re, c_im=v_c_im, d_skip=v_d_skip, final_gain=v_final_gain)
    small_g = dict(norm_gain=dg1, pool_scale=dps, a_re=g_a_re, a_im=g_a_im, log_dt=g_ldt, b_re=g_b_re,
                   b_im=g_b_im, c_re=g_c_re, c_im=g_c_im, d_skip=dd, final_gain=dg2)
    shapes = [small_w[n].shape for n in small_names]
    loss_row = sum(_tile_rows(small_w[n].size) for n in small_names)
    unit = N_DEV * SUBLANES
    rows = -(-(loss_row + SUBLANES) // unit) * unit
    gbuf = _pack_small([small_g[n] for n in small_names] + [lpart[0, :1]], rows)
    gsum, = _comm_call(_allreduce_plan(gbuf), "allreduce_small")
    g_out, d_out, m_out, v_out = {}, {}, {}, {}
    for n in big_names:
        w_, m_, v_ = big[n]
        r2 = shard2d[n]
        res = _adamw(w_.reshape(r2), gshard[n], m_.reshape(r2), v_.reshape(r2), "adamw_" + n)
        g_out[n], d_out[n], m_out[n], v_out[n] = (a.reshape(w_.shape) for a in (gshard[n], *res))
    wbuf = _pack_small([small_w[n] for n in small_names], rows)
    mbuf = _pack_small([small_m[n] for n in small_names], rows)
    vbuf = _pack_small([small_v[n] for n in small_names], rows)
    dsm, msm, vsm = _adamw(wbuf, gsum, mbuf, vbuf, "adamw_small")
    g_small = dict(zip(small_names, _unpack_small(gsum, shapes)))
    d_small = dict(zip(small_names, _unpack_small(dsm, shapes)))
    m_small = dict(zip(small_names, _unpack_small(msm, shapes)))
    v_small = dict(zip(small_names, _unpack_small(vsm, shapes)))
    loss = gsum[loss_row, 0]

    g_out.update(g_small)
    d_out.update(d_small)
    m_out.update(m_small)
    v_out.update(v_small)

    order = ["norm_gain", "w_in", "w_pool", "pool_scale", "a_re", "a_im", "log_dt", "b_re", "b_im", "c_re",
             "c_im", "d_skip", "w_glu", "w_out", "w_ple", "w_ple_gate", "final_gain"]
    return (loss, grad_x[None], *[g_out[n] for n in order], *[d_out[n] for n in order],
            *[m_out[n] for n in order], *[v_out[n] for n in order])
```

```python
import functools

import jax
import jax.numpy as jnp
from jax import lax
from jax.experimental import pallas as pl
from jax.experimental.pallas import tpu as pltpu

F32, BF16 = jnp.float32, jnp.bfloat16
MESH = pl.DeviceIdType.MESH
ANY = pl.BlockSpec(memory_space=pl.ANY)
VMEM_FULL = pl.BlockSpec(memory_space=pltpu.VMEM)

EPS = 1e-6
A_RE_MAX = -1e-4
SSM_GROUP = 16
SSM_STATE = 64
POOL_WINDOWS = (2, 4, 8, 16)
POOL_HALO = 16
ADAM_LR, ADAM_B1, ADAM_B2, ADAM_EPS, ADAM_WD, ADAM_STEP = 0.001, 0.9, 0.999, 1e-08, 0.01, 10

V7X_VMEM_BYTES = 64 * 1024 * 1024
VMEM_LIMIT = V7X_VMEM_BYTES - 8 * 1024 * 1024
SUBLANES, LANES = 8, 128
BF16_TILE_ROWS = 16
SSM_TILE_GROUPS = 8
SCAN_LANES = 1024
N_DEV, N_CHIP = 8, 4
DMA_CHUNK_BYTES = 256 * 1024
DMA_MAX_CHUNKS = 32
RS_CHUNKS = 8
STREAM_BUFFERS = 3
ROWS_ELEMENTWISE = 256
ROWS_SSM = 256
ROWS_MATMUL = 1024
DEPTH_MATMUL = 4096
AG_FORWARD_SCALE, AG_FORWARD_LAG = 0.85, 0.05


def _t(n, pref):
    return pref if n % pref == 0 else n


def _cp(sem=None, vmem=VMEM_LIMIT):
    return pltpu.CompilerParams(dimension_semantics=sem, vmem_limit_bytes=vmem)


def _call(body, **kw):
    return pl.pallas_call(body, **kw)


NN = ((1,), (0,))
NT = ((1,), (1,))
TN = ((0,), (0,))


def _mm(a, b, *, dims, grid, a_spec, b_spec, o_spec, out_shape, name, res=None, r_spec=None, bf16_copy=False,
        comm=None):
    nk, kax = grid[-1], len(grid) - 1
    acc_shape = tuple(d for d in o_spec.block_shape if d is not None)

    def core(*refs):
        refs = list(refs)
        a_ref, b_ref = refs[:2]
        r_ref = refs[2] if res is not None else None
        outs = refs[3 if res is not None else 2:]
        o_ref = outs[0]
        o2_ref = outs[1] if bf16_copy else None
        acc = outs[-1] if nk > 1 else None

        def finish(r):
            if r_ref is not None:
                r = r + r_ref[...]
            o_ref[...] = r.astype(o_ref.dtype)
            if o2_ref is not None:
                o2_ref[...] = r.astype(BF16)

        part = lax.dot_general(a_ref[...].astype(BF16), b_ref[...].astype(BF16),
                               (dims, ((), ())), preferred_element_type=F32)
        if nk == 1:
            finish(part)
        else:
            k = pl.program_id(kax)

            @pl.when(k == 0)
            def _():
                acc[...] = part

            @pl.when(k > 0)
            def _():
                acc[...] += part

            @pl.when(k == nk - 1)
            def _():
                finish(acc[...])

    ins, specs = [a, b], [a_spec, b_spec]
    if res is not None:
        ins.append(res)
        specs.append(r_spec)
    o_specs, o_shapes = [o_spec], [out_shape]
    if bf16_copy:
        o_specs = [o_spec, o_spec]
        o_shapes = [out_shape, jax.ShapeDtypeStruct(out_shape.shape, BF16)]
    scratch = [pltpu.VMEM(acc_shape, F32)] if nk > 1 else []
    body, extra = _hosted(core, comm, grid, len(ins), len(o_specs), len(scratch))
    sem = ("arbitrary",) * len(grid) if comm else ("parallel",) * kax + ("arbitrary",)
    outs = _call(body, grid=grid, in_specs=specs + extra["in_specs"], out_specs=o_specs + extra["out_specs"],
                 out_shape=o_shapes + extra["out_shape"], scratch_shapes=scratch + extra["scratch"],
                 input_output_aliases=extra["aliases"],
                 compiler_params=_cp(sem), name=name)(*ins, *extra["ins"])
    return outs[0] if len(outs) == 1 else outs


def _bs(shape, fn):
    return pl.BlockSpec(shape, fn)


def _ring_fetch(hbm, ring, sems, step, nsteps, rows):
    n = len(hbm)

    def copy(k, s):
        slot = s % STREAM_BUFFERS
        src, c0 = hbm[k] if isinstance(hbm[k], tuple) else (hbm[k], 0)
        window = (pl.ds(pl.multiple_of(s * rows, SUBLANES), rows), pl.ds(c0, ring[k].shape[2]))
        return pltpu.make_async_copy(src.at[window], ring[k].at[slot], sems.at[k, slot])

    @pl.when(step == 0)
    def _():
        for s in range(min(STREAM_BUFFERS - 1, nsteps)):
            for k in range(n):
                copy(k, s).start()

    @pl.when(step + STREAM_BUFFERS - 1 < nsteps)
    def _():
        for k in range(n):
            copy(k, step + STREAM_BUFFERS - 1).start()

    for k in range(n):
        copy(k, step).wait()
    return [ring[k].at[step % STREAM_BUFFERS] for k in range(n)]


def _sigmoid(v):
    return 1.0 / (1.0 + jnp.exp(-v))


def _gelu(v):
    return 0.5 * v * (1.0 + jnp.tanh(0.7978845608028654 * (v + 0.044715 * v * v * v)))


def _gelu_grad(v):
    t = jnp.tanh(0.7978845608028654 * (v + 0.044715 * v * v * v))
    return 0.5 * (1.0 + t) + 0.5 * v * (1.0 - t * t) * 0.7978845608028654 * (1.0 + 3 * 0.044715 * v * v)


def _norm1_bwd(x, dhn, dh1, g1, tb, comm=None):
    T, D = x.shape
    nb = T // tb

    def core(x_hbm, dhn_hbm, dh1_hbm, g_ref, dx_ref, dg_ref, rx, rdhn, rdh1, sems):
        @pl.when(pl.program_id(0) == 0)
        def _():
            dg_ref[...] = jnp.zeros_like(dg_ref)

        x_ref, dhn_ref, dh1_ref = _ring_fetch((x_hbm, dhn_hbm, dh1_hbm), (rx, rdhn, rdh1), sems,
                                              pl.program_id(0), nb, tb)
        xv = x_ref[...]
        r = lax.rsqrt(jnp.mean(xv * xv, axis=-1, keepdims=True) + EPS)
        xh = xv * r
        dhn_v = dhn_ref[...]
        dg_ref[...] += jnp.sum(dhn_v * xh, axis=0, keepdims=True)
        dxh = dhn_v * g_ref[...]
        dx_ref[...] = dh1_ref[...] + r * (dxh - xh * jnp.mean(dxh * xh, axis=-1, keepdims=True))

    row = _bs((tb, D), lambda i: (i, 0))
    vec = _bs((1, D), lambda i: (0, 0))
    body, extra = _hosted(core, comm, (nb,), 4, 2, 4)
    ring = pltpu.VMEM((STREAM_BUFFERS, tb, D), F32)
    return _call(body, grid=(nb,), in_specs=[ANY, ANY, ANY, vec] + extra["in_specs"],
                 out_specs=[row, vec] + extra["out_specs"],
                 out_shape=[jax.ShapeDtypeStruct((T, D), F32), jax.ShapeDtypeStruct((1, D), F32)] + extra["out_shape"],
                 scratch_shapes=[ring, ring, ring, pltpu.SemaphoreType.DMA((3, STREAM_BUFFERS))] + extra["scratch"],
                 input_output_aliases=extra["aliases"],
                 compiler_params=_cp(("arbitrary",)), name="norm1_bwd")(x, dhn, dh1, g1, *extra["ins"])


def _gate_fwd(mixed, proj, hg, ps, tb):
    T, P = mixed.shape
    nb = T // tb

    def body(mx_hbm, proj_hbm, hg_hbm, ps_ref, o_ref, rmx, rpg, rsg, rhg, sems):
        mx_ref, pg_ref, sg_ref, hg_ref = _ring_fetch((mx_hbm, (proj_hbm, P), (proj_hbm, 3 * P), hg_hbm),
                                                     (rmx, rpg, rsg, rhg), sems, pl.program_id(0), nb, tb)
        pg, sg = pg_ref[...], sg_ref[...]
        ya = (mx_ref[...] * ps_ref[...]) * (pg * _sigmoid(pg))
        hgv = hg_ref[...]
        o = hgv[:, :P] * _sigmoid(hgv[:, P:])
        yb = o * (sg * _sigmoid(sg))
        o_ref[:, :P] = ya.astype(BF16)
        o_ref[:, P:] = yb.astype(BF16)

    ring = lambda w: pltpu.VMEM((STREAM_BUFFERS, tb, w), F32)
    return _call(body, grid=(nb,),
                 in_specs=[ANY, ANY, ANY, _bs((1, P), lambda i: (0, 0))],
                 out_specs=_bs((tb, 2 * P), lambda i: (i, 0)),
                 out_shape=jax.ShapeDtypeStruct((T, 2 * P), BF16),
                 scratch_shapes=[ring(P), ring(P), ring(P), ring(2 * P), pltpu.SemaphoreType.DMA((4, STREAM_BUFFERS))],
                 compiler_params=_cp(("arbitrary",)), name="gate_fwd")(mixed, proj, hg, ps)


def _gate_bwd(dcat, mixed, proj, hg, ps, tb, comm=None):
    T, P = mixed.shape

    def core(dc_hbm, mx_hbm, proj_hbm, hg_hbm, ps_ref, dmx_ref, dpg_ref, dsg_ref, dhg_ref, dps_ref,
             rdc, rmx, rpg, rsg, rhg, sems):
        @pl.when(pl.program_id(0) == 0)
        def _():
            dps_ref[...] = jnp.zeros_like(dps_ref)

        dc_ref, mx_ref, pg_ref, sg_ref, hg_ref = _ring_fetch(
            (dc_hbm, mx_hbm, (proj_hbm, P), (proj_hbm, 3 * P), hg_hbm), (rdc, rmx, rpg, rsg, rhg), sems,
            pl.program_id(0), T // tb, tb)
        dc = dc_ref[...]
        dya, dyb = dc[:, :P], dc[:, P:]
        pg, sg, mx, psv = pg_ref[...], sg_ref[...], mx_ref[...], ps_ref[...]
        s_pg = _sigmoid(pg)
        dpa = dya * (pg * s_pg)
        dpg_ref[...] = (dya * (mx * psv) * (s_pg * (1.0 + pg * (1.0 - s_pg)))).astype(BF16)
        dps_ref[...] += jnp.sum(dpa * mx, axis=0, keepdims=True)
        dmx_ref[...] = (dpa * psv).astype(BF16)
        hgv = hg_ref[...]
        h1, s_h2 = hgv[:, :P], _sigmoid(hgv[:, P:])
        s_sg = _sigmoid(sg)
        do = dyb * (sg * s_sg)
        dsg_ref[...] = (dyb * (h1 * s_h2) * (s_sg * (1.0 + sg * (1.0 - s_sg)))).astype(BF16)
        dhg_ref[:, :P] = (do * s_h2).astype(BF16)
        dhg_ref[:, P:] = (do * h1 * s_h2 * (1.0 - s_h2)).astype(BF16)

    rowp = _bs((tb, P), lambda i: (i, 0))
    row2 = _bs((tb, 2 * P), lambda i: (i, 0))
    vec = _bs((1, P), lambda i: (0, 0))
    body, extra = _hosted(core, comm, (T // tb,), 5, 5, 6)
    ring = lambda w: pltpu.VMEM((STREAM_BUFFERS, tb, w), F32)
    return _call(body, grid=(T // tb,),
                 in_specs=[ANY, ANY, ANY, ANY, vec] + extra["in_specs"],
                 out_specs=[rowp, rowp, rowp, row2, vec] + extra["out_specs"],
                 out_shape=[jax.ShapeDtypeStruct((T, P), BF16), jax.ShapeDtypeStruct((T, P), BF16),
                            jax.ShapeDtypeStruct((T, P), BF16), jax.ShapeDtypeStruct((T, 2 * P), BF16),
                            jax.ShapeDtypeStruct((1, P), F32)] + extra["out_shape"],
                 scratch_shapes=[ring(2 * P), ring(P), ring(P), ring(P), ring(2 * P),
                                 pltpu.SemaphoreType.DMA((5, STREAM_BUFFERS))] + extra["scratch"],
                 compiler_params=_cp(("arbitrary",)), name="gate_bwd")(dcat, mixed, proj, hg, ps, *extra["ins"])


def _final_fb(h1, pe, wple, z, tgt, g2, tb):
    T, D = h1.shape
    E = pe.shape[1]
    Q = D // N_CHIP
    nb = T // tb

    def body(h1_hbm, p_ref, w_ref, z_hbm, t_hbm, g_ref, dh2_ref, dz_ref, dg_ref, l_ref, gw_ref, gw16_ref, acc,
             rh1, rz, rt, sems):
        @pl.when(pl.program_id(0) == 0)
        def _():
            dg_ref[...] = jnp.zeros_like(dg_ref)
            l_ref[...] = jnp.zeros_like(l_ref)
            acc[...] = jnp.zeros_like(acc)

        h1_ref, z_ref, t_ref = _ring_fetch((h1_hbm, z_hbm, t_hbm), (rh1, rz, rt), sems, pl.program_id(0), nb, tb)
        pb = p_ref[...].astype(BF16)
        ev = jnp.dot(pb, w_ref[...], preferred_element_type=F32)
        s = _sigmoid(z_ref[...])
        h2 = h1_ref[...] + ev * s
        r = lax.rsqrt(jnp.mean(h2 * h2, axis=-1, keepdims=True) + EPS)
        xh = h2 * r
        gv = g_ref[...]
        diff = xh * gv - t_ref[...]
        l_ref[...] += 0.5 * jnp.sum(jnp.mean(diff * diff, axis=-1, keepdims=True))
        dout = diff * (1.0 / D)
        dg_ref[...] += jnp.sum(dout * xh, axis=0, keepdims=True)
        dxh = dout * gv
        dh2 = r * (dxh - xh * jnp.mean(dxh * xh, axis=-1, keepdims=True))
        dh2_ref[...] = dh2
        dz_ref[...] = (dh2 * ev * s * (1.0 - s)).astype(BF16)
        acc[...] += lax.dot_general(pb, (dh2 * s).astype(BF16), (TN, ((), ())), preferred_element_type=F32)

        @pl.when(pl.program_id(0) == nb - 1)
        def _():
            for j in range(N_CHIP):
                slab = acc[:, j * Q:(j + 1) * Q]
                gw_ref[j] = slab
                gw16_ref[j] = slab.astype(BF16)

    row = _bs((tb, D), lambda i: (i, 0))
    vec = _bs((1, D), lambda i: (0, 0))
    slabs = _bs((N_CHIP, E, Q), lambda i: (0, 0, 0))
    ring = pltpu.VMEM((STREAM_BUFFERS, tb, D), F32)
    return _call(body, grid=(nb,),
                 in_specs=[ANY, _bs((tb, E), lambda i: (i, 0)), _bs((E, D), lambda i: (0, 0)), ANY, ANY, vec],
                 out_specs=[row, row, vec, _bs((1, LANES), lambda i: (0, 0)), slabs, slabs],
                 out_shape=[jax.ShapeDtypeStruct((T, D), F32), jax.ShapeDtypeStruct((T, D), BF16),
                            jax.ShapeDtypeStruct((1, D), F32), jax.ShapeDtypeStruct((1, LANES), F32),
                            jax.ShapeDtypeStruct((N_CHIP, E, Q), F32), jax.ShapeDtypeStruct((N_CHIP, E, Q), BF16)],
                 scratch_shapes=[pltpu.VMEM((E, D), F32), ring, ring, ring,
                                 pltpu.SemaphoreType.DMA((3, STREAM_BUFFERS))],
                 compiler_params=_cp(("arbitrary",)), name="final_fb")(h1, pe, wple, z, tgt, g2)


def _pool_inv_count(t0, rows, pg, ngroups):
    t = t0 + lax.broadcasted_iota(jnp.int32, (rows, pg), 0)
    parts = []
    for w in POOL_WINDOWS[:ngroups]:
        parts.append(jnp.where(t + 1 >= w, 1.0 / w, 1.0 / (t + 1).astype(F32)))
    return parts


def _pool_fwd(proj, wp, P, tb):
    T = proj.shape[0]
    ng = len(POOL_WINDOWS)
    pg = P // ng
    hb = tb // POOL_HALO

    def body(v_ref, tail_ref, w_ref, o_ref, mx_ref, ext):
        i = pl.program_id(0)
        ext[pl.ds(0, POOL_HALO), :] = jnp.where(i > 0, tail_ref[...], 0.0)
        ext[pl.ds(POOL_HALO, tb), :] = v_ref[...]
        inv = _pool_inv_count(i * tb, tb, pg, ng)
        for g, w in enumerate(POOL_WINDOWS):
            cols = pl.ds(g * pg, pg)
            win = ext[pl.ds(POOL_HALO, tb), cols]
            for k in range(1, w):
                win = win + ext[pl.ds(POOL_HALO - k, tb), cols]
            pooled = (win * inv[g] - ext[pl.ds(POOL_HALO, tb), cols]).astype(BF16)
            o_ref[:, cols] = pooled
            mx_ref[:, cols] = jnp.dot(pooled, w_ref[g], preferred_element_type=F32)

    row = _bs((tb, P), lambda i: (i, 0))
    return _call(body, grid=(T // tb,),
                 in_specs=[row, _bs((POOL_HALO, P), lambda i: (jnp.maximum(i * hb - 1, 0), 0)),
                           _bs(wp.shape, lambda i: (0, 0, 0))],
                 out_specs=[row, row],
                 out_shape=[jax.ShapeDtypeStruct((T, P), BF16), jax.ShapeDtypeStruct((T, P), F32)],
                 scratch_shapes=[pltpu.VMEM((tb + POOL_HALO, P), F32)],
                 compiler_params=_cp(("arbitrary",)), name="pool_fwd")(proj, proj, wp)


def _pool_bwd(dmixed, wp, tb, comm=None):
    T, P = dmixed.shape
    ng = len(POOL_WINDOWS)
    pg = P // ng
    hb = tb // POOL_HALO
    nb = T // tb

    def core(d_ref, head_ref, w_ref, o_ref, ext, dpl):
        i = pl.program_id(0)
        inv = _pool_inv_count(i * tb, tb, pg, ng)
        invh = _pool_inv_count((i + 1) * tb, POOL_HALO, pg, ng)
        for g in range(ng):
            cols = pl.ds(g * pg, pg)
            dp = lax.dot_general(d_ref[:, cols], w_ref[g], (NT, ((), ())), preferred_element_type=F32)
            dph = lax.dot_general(head_ref[:, cols], w_ref[g], (NT, ((), ())), preferred_element_type=F32)
            dpl[:, cols] = dp
            ext[pl.ds(0, tb), cols] = dp * inv[g]
            ext[pl.ds(tb, POOL_HALO), cols] = jnp.where(i < nb - 1, dph * invh[g], 0.0)
        for g, w in enumerate(POOL_WINDOWS):
            cols = pl.ds(g * pg, pg)
            acc = ext[pl.ds(0, tb), cols]
            for k in range(1, w):
                acc = acc + ext[pl.ds(k, tb), cols]
            o_ref[:, cols] = (acc - dpl[:, cols]).astype(BF16)

    body, extra = _hosted(core, comm, (nb,), 3, 1, 2)
    return _call(body, grid=(nb,),
                 in_specs=[_bs((tb, P), lambda i: (i, 0)),
                           _bs((POOL_HALO, P), lambda i: (jnp.minimum((i + 1) * hb, T // POOL_HALO - 1), 0)),
                           _bs(wp.shape, lambda i: (0, 0, 0))] + extra["in_specs"],
                 out_specs=[_bs((tb, P), lambda i: (i, 0))] + extra["out_specs"],
                 out_shape=[jax.ShapeDtypeStruct((T, P), BF16)] + extra["out_shape"],
                 scratch_shapes=[pltpu.VMEM((tb + POOL_HALO, P), F32), pltpu.VMEM((tb, P), F32)] + extra["scratch"],
                 compiler_params=_cp(("arbitrary",)), name="pool_bwd")(dmixed, dmixed, wp, *extra["ins"])


def _zoh(a_re, a_im, ldt, b_re, b_im):
    lam_re = jnp.minimum(a_re, A_RE_MAX)
    lam_im = a_im
    dt = jnp.exp(ldt)
    mag = jnp.exp(lam_re * dt)
    ang = lam_im * dt
    ab_re = mag * jnp.cos(ang)
    ab_im = mag * jnp.sin(ang)
    den = lam_re * lam_re + lam_im * lam_im
    n_re = ab_re - 1.0
    n_im = ab_im
    q_re = (n_re * lam_re + n_im * lam_im) / den
    q_im = (n_im * lam_re - n_re * lam_im) / den
    return ab_re, ab_im, q_re * b_re - q_im * b_im, q_re * b_im + q_im * b_re


def _ssm_prep(a_re, a_im, ldt, bt_re, bt_im):
    shp = jax.ShapeDtypeStruct(a_re.shape, F32)

    def body(a, b, c, d, e, o0, o1, o2, o3):
        r = _zoh(a[...], b[...], c[...], d[...], e[...])
        o0[...], o1[...], o2[...], o3[...] = r

    return _call(body, in_specs=[VMEM_FULL] * 5, out_specs=[VMEM_FULL] * 4, out_shape=[shp] * 4,
                 name="ssm_prep")(a_re, a_im, ldt, bt_re, bt_im)


def _ssm_prep_bwd(a_re, a_im, ldt, bt_re, bt_im, dab_re, dab_im, dbb_re, dbb_im, G):
    GC, N = a_re.shape
    C = GC // G

    def body(a, b, c, d, e, g0, g1, g2, g3, da_re, da_im, dldt, db_re, db_im):
        _, vjp = jax.vjp(_zoh, a[...], b[...], c[...], d[...], e[...])
        ga_re, ga_im, gl, gb_re, gb_im = vjp((g0[...], g1[...], g2[...], g3[...]))
        da_re[...] = jnp.sum(ga_re.reshape(G, C, N), axis=1)
        da_im[...] = jnp.sum(ga_im.reshape(G, C, N), axis=1)
        dldt[...] = jnp.sum(jnp.sum(gl.reshape(G, C, N), axis=1), axis=1, keepdims=True)
        db_re[...] = gb_re
        db_im[...] = gb_im

    gn = jax.ShapeDtypeStruct((G, N), F32)
    full = jax.ShapeDtypeStruct((GC, N), F32)
    return _call(body, in_specs=[VMEM_FULL] * 9, out_specs=[VMEM_FULL] * 5,
                 out_shape=[gn, gn, jax.ShapeDtypeStruct((G, 1), F32), full, full],
                 name="ssm_prep_bwd")(a_re, a_im, ldt, bt_re, bt_im, dab_re, dab_im, dbb_re, dbb_im)


def _coef_tiles(abr, abi, reverse):
    ns = abr.shape[1]
    row = lax.broadcasted_iota(jnp.int32, (SUBLANES, ns), 0)
    ar = jnp.broadcast_to(abr, (SUBLANES, ns))
    ai = jnp.broadcast_to(-abi if reverse else abi, (SUBLANES, ns))
    a2r, a2i = ar * ar - ai * ai, 2.0 * ar * ai
    a4r, a4i = a2r * a2r - a2i * a2i, 2.0 * a2r * a2i
    out = []
    for d, (vr, vi) in ((1, (ar, ai)), (2, (a2r, a2i)), (4, (a4r, a4i))):
        keep = (row < SUBLANES - d) if reverse else (row >= d)
        out += [jnp.where(keep, vr, 0.0), jnp.where(keep, vi, 0.0)]
    pr, pi = ar, ai
    for k in range(1, SUBLANES):
        sel = (row <= SUBLANES - 1 - k) if reverse else (row >= k)
        nr, ni = pr * ar - pi * ai, pr * ai + pi * ar
        pr, pi = jnp.where(sel, nr, pr), jnp.where(sel, ni, pi)
    return out + [pr, pi]


def _cpow(ar, ai, n):
    out, br, bi = None, ar, ai
    while n:
        if n & 1:
            out = (br, bi) if out is None else (out[0] * br - out[1] * bi, out[0] * bi + out[1] * br)
        br, bi = br * br - bi * bi, 2.0 * br * bi
        n >>= 1
    return out


def _seg_perm_matrix(nrows):
    r = jnp.arange(nrows)
    src = (nrows // SUBLANES) * (r % SUBLANES) + r // SUBLANES
    return (src[:, None] == jnp.arange(nrows)[None, :]).astype(BF16)


def _seg_order_rows(pm, xb):
    return jnp.dot(pm, xb, preferred_element_type=F32).astype(BF16)


def _time_order_rows(pm, x, terms):
    out, rest = None, x
    for t in range(terms):
        piece = rest.astype(BF16)
        part = lax.dot_general(pm, piece, (TN, ((), ())), preferred_element_type=F32)
        out = part if out is None else out + part
        if t + 1 < terms:
            rest = rest - piece.astype(F32)
    return out


def _scan_tiles(abr, abi, seg, reverse):
    ns = abr.shape[1]
    seg_pow = _cpow(abr, abi, seg)
    step = [jnp.broadcast_to(abr, (SUBLANES, ns)), jnp.broadcast_to(-abi if reverse else abi, (SUBLANES, ns))]
    return _coef_tiles(seg_pow[0], seg_pow[1], reverse) + step


def _seg_scan(xr_ref, xi_ref, coef_ref, car_ref, cai_ref, *, nrows, ns, reverse, cmat=None, dab=None):
    seg = nrows // SUBLANES
    cw = min(SCAN_LANES, ns)
    row = lax.broadcasted_iota(jnp.int32, (SUBLANES, cw), 0)
    first, last = (SUBLANES - 1, 0) if reverse else (0, SUBLANES - 1)

    def tile(i):
        return pl.ds(pl.multiple_of(((seg - 1 - i) if reverse else i) * SUBLANES, SUBLANES), SUBLANES)

    for cc in range(ns // cw):
        cols = pl.ds(cc * cw, cw)
        ar, ai = coef_ref[8, :, cols], coef_ref[9, :, cols]

        def local(i, x, cols=cols, ar=ar, ai=ai):
            rows = tile(i)
            nr = ar * x[0] - ai * x[1] + xr_ref[rows, cols]
            ni = ar * x[1] + ai * x[0] + xi_ref[rows, cols]
            xr_ref[rows, cols] = nr
            xi_ref[rows, cols] = ni
            return nr, ni

        zero = jnp.zeros((SUBLANES, cw), F32)
        er, ei = lax.fori_loop(0, seg, local, (zero, zero))

        co = [coef_ref[k, :, cols] for k in range(8)]
        for lvl, d in enumerate((1, 2, 4)):
            kr, ki = co[2 * lvl], co[2 * lvl + 1]
            sh = SUBLANES - d if reverse else d
            sr, si = pltpu.roll(er, sh, 0), pltpu.roll(ei, sh, 0)
            er, ei = er + (kr * sr - ki * si), ei + (kr * si + ki * sr)
        c0r, c0i = car_ref[:, cols], cai_ref[:, cols]
        er, ei = er + (co[6] * c0r - co[7] * c0i), ei + (co[6] * c0i + co[7] * c0r)
        nb_shift = SUBLANES - 1 if reverse else 1
        cmr = jnp.where(row == first, c0r, pltpu.roll(er, nb_shift, 0))
        cmi = jnp.where(row == first, c0i, pltpu.roll(ei, nb_shift, 0))
        car_ref[:, cols] = jnp.broadcast_to(er[last:last + 1, :], er.shape)
        cai_ref[:, cols] = jnp.broadcast_to(ei[last:last + 1, :], ei.shape)
        if cmat is not None:
            cmat[0][:, cols] = cmr
            cmat[1][:, cols] = cmi

        w0 = (ar * cmr - ai * cmi, ar * cmi + ai * cmr)
        if dab is None:
            def fix(i, w, cols=cols, ar=ar, ai=ai):
                rows = tile(i)
                xr_ref[rows, cols] = xr_ref[rows, cols] + w[0]
                xi_ref[rows, cols] = xi_ref[rows, cols] + w[1]
                return ar * w[0] - ai * w[1], ar * w[1] + ai * w[0]

            lax.fori_loop(0, seg, fix, w0)
        else:
            s_re, s_im, e_re, e_im, o_re, o_im = dab

            def add(rows, w, pr, pi, acc):
                gr = xr_ref[rows, cols] + w[0]
                gi = xi_ref[rows, cols] + w[1]
                xr_ref[rows, cols] = gr
                xi_ref[rows, cols] = gi
                return acc[0] + (gr * pr + gi * pi), acc[1] + (gi * pr - gr * pi)

            def fix(i, st, cols=cols, ar=ar, ai=ai):
                w, acc = st[:2], st[2:]
                rows = tile(i)
                before = pl.ds(pl.multiple_of((seg - 2 - i) * SUBLANES, SUBLANES), SUBLANES)
                acc = add(rows, w, s_re[before, cols], s_im[before, cols], acc)
                return (ar * w[0] - ai * w[1], ar * w[1] + ai * w[0]) + acc

            st = lax.fori_loop(0, seg - 1, fix, w0 + (zero, zero))
            acc = add(pl.ds(0, SUBLANES), st[:2], e_re[:, cols], e_im[:, cols], st[2:])
            o_re[:, cols] += jnp.sum(acc[0], axis=0, keepdims=True)
            o_im[:, cols] += jnp.sum(acc[1], axis=0, keepdims=True)


def _hosted(core, comm, grid, n_in, n_out, n_scratch):
    ci = len(comm["ins"]) if comm else 0
    co = len(comm["out_shape"]) if comm else 0

    def body(*refs):
        ins, rest = refs[:n_in + ci], refs[n_in + ci:]
        outs, scr = rest[:n_out + co], rest[n_out + co:]
        hooks = functools.partial(_comm_hooks, comm, grid, ins[n_in:], outs[n_out:], scr[n_scratch:])
        hooks(before=True)
        core(*ins[:n_in], *outs[:n_out], *scr[:n_scratch])
        hooks(before=False)

    aliases = {n_in + i: n_out + i for i in range(co)} if comm and comm.get("alias") else {}
    extra = dict(ins=list(comm["ins"]) if comm else [], in_specs=[ANY] * ci, out_specs=[ANY] * co,
                 out_shape=list(comm["out_shape"]) if comm else [], scratch=list(comm["scratch"]) if comm else [],
                 aliases=aliases)
    return body, extra


def _ssm_fwd(proj, bdr, bdi, cdr, cdi, abr, abi, dsk, P, tb, comm=None):
    T = proj.shape[0]
    ntl, ct, st = bdr.shape
    ns = ntl * st
    nb = T // tb

    def core(u_ref, bdr_ref, bdi_ref, cdr_ref, cdi_ref, abr_ref, abi_ref, d_ref, pm_ref,
             y_ref, ge_ref, bsr_ref, bsi_ref, sr, si, coef, car, cai, up):
        @pl.when(pl.program_id(0) == 0)
        def _():
            for k, tile in enumerate(_scan_tiles(abr_ref[...], abi_ref[...], tb // SUBLANES, False)):
                coef[k] = tile
            car[...] = jnp.zeros_like(car)
            cai[...] = jnp.zeros_like(cai)

        bsr_ref[...] = car[...]
        bsi_ref[...] = cai[...]
        u = u_ref[...]
        ub = _seg_order_rows(pm_ref[...], u.astype(BF16))
        for s in range(ntl):
            us = ub[:, s * ct:(s + 1) * ct]
            sr[:, s * st:(s + 1) * st] = jnp.dot(us, bdr_ref[s], preferred_element_type=F32)
            si[:, s * st:(s + 1) * st] = jnp.dot(us, bdi_ref[s], preferred_element_type=F32)
        _seg_scan(sr, si, coef, car, cai, nrows=tb, ns=ns, reverse=False)
        for s in range(ntl):
            s_re = sr[:, s * st:(s + 1) * st].astype(BF16)
            s_im = si[:, s * st:(s + 1) * st].astype(BF16)
            up[:, s * ct:(s + 1) * ct] = (jnp.dot(s_re, cdr_ref[s], preferred_element_type=F32)
                                          - jnp.dot(s_im, cdi_ref[s], preferred_element_type=F32))
        y = _time_order_rows(pm_ref[...], up[...], 3) + d_ref[...] * u
        y_ref[...] = y
        ge_ref[...] = _gelu(y).astype(BF16)

    full3 = lambda a: _bs(a.shape, lambda i: (0, 0, 0))
    vec = lambda n: _bs((1, n), lambda i: (0, 0))
    row = _bs((tb, P), lambda i: (i, 0))
    st_spec = _bs((None, SUBLANES, ns), lambda i: (i, 0, 0))
    body, extra = _hosted(core, comm, (nb,), 9, 4, 6)
    return _call(body, grid=(nb,),
                 in_specs=[_bs((tb, P), lambda i: (i, 2)), full3(bdr), full3(bdi), full3(cdr), full3(cdi),
                           vec(ns), vec(ns), vec(P), _bs((tb, tb), lambda i: (0, 0))] + extra["in_specs"],
                 out_specs=[row, row, st_spec, st_spec] + extra["out_specs"],
                 out_shape=[jax.ShapeDtypeStruct((T, P), F32), jax.ShapeDtypeStruct((T, P), BF16),
                            jax.ShapeDtypeStruct((nb, SUBLANES, ns), F32),
                            jax.ShapeDtypeStruct((nb, SUBLANES, ns), F32)] + extra["out_shape"],
                 scratch_shapes=[pltpu.VMEM((tb, ns), F32), pltpu.VMEM((tb, ns), F32),
                                 pltpu.VMEM((10, SUBLANES, ns), F32),
                                 pltpu.VMEM((SUBLANES, ns), F32), pltpu.VMEM((SUBLANES, ns), F32),
                                 pltpu.VMEM((tb, P), F32)] + extra["scratch"],
                 compiler_params=_cp(("arbitrary",)), name="ssm_fwd")(
                     proj, bdr, bdi, cdr, cdi, abr, abi, dsk, _seg_perm_matrix(tb), *extra["ins"])


def _ssm_bwd(proj, y, dge, bsr, bsi, bdr, bdi, cdr, cdi, abr, abi, dsk, dpi, dpg, dsg, P, tb, comm=None):
    T = proj.shape[0]
    ntl, ct, st = bdr.shape
    ns = ntl * st
    nb = T // tb

    def core(u_ref, y_ref, dge_ref, bsr_ref, bsi_ref, abr_ref, abi_ref, d_ref, pm_ref, dpi_ref, dpg_ref, dsg_ref,
             bdr_h, bdi_h, cdr_h, cdi_h,
             dproj_ref, dabr_ref, dabi_ref, dd_ref, dbdr_h, dbdi_h, dcdr_h, dcdi_h,
             wbdr, wbdi, wcdr, wcdi, abdr, abdi, acdr, acdi, spr, spi, gr, gi, coef_f, coef_r,
             car, cai, gcr, gci, ser, sei, dup):
        i = pl.program_id(0)

        @pl.when(i == 0)
        def _():
            for h, w in ((bdr_h, wbdr), (bdi_h, wbdi), (cdr_h, wcdr), (cdi_h, wcdi)):
                pltpu.sync_copy(h, w)
            for a in (abdr, abdi, acdr, acdi, gcr, gci):
                a[...] = jnp.zeros_like(a)
            for o in (dabr_ref, dabi_ref, dd_ref):
                o[...] = jnp.zeros_like(o)
            for k, tile in enumerate(_scan_tiles(abr_ref[...], abi_ref[...], tb // SUBLANES, False)):
                coef_f[k] = tile
            for k, tile in enumerate(_scan_tiles(abr_ref[...], abi_ref[...], tb // SUBLANES, True)):
                coef_r[k] = tile

        car[...] = bsr_ref[...]
        cai[...] = bsi_ref[...]
        u = u_ref[...]
        dy = dge_ref[...] * _gelu_grad(y_ref[...])
        ub = _seg_order_rows(pm_ref[...], u.astype(BF16))
        dyb = _seg_order_rows(pm_ref[...], dy.astype(BF16))
        for s in range(ntl):
            us = ub[:, s * ct:(s + 1) * ct]
            spr[:, s * st:(s + 1) * st] = jnp.dot(us, wbdr[s], preferred_element_type=F32)
            spi[:, s * st:(s + 1) * st] = jnp.dot(us, wbdi[s], preferred_element_type=F32)
        _seg_scan(spr, spi, coef_f, car, cai, nrows=tb, ns=ns, reverse=False, cmat=(ser, sei))

        for s in range(ntl):
            dys = dyb[:, s * ct:(s + 1) * ct]
            gr[:, s * st:(s + 1) * st] = lax.dot_general(dys, wcdr[s], (NT, ((), ())), preferred_element_type=F32)
            gi[:, s * st:(s + 1) * st] = -lax.dot_general(dys, wcdi[s], (NT, ((), ())), preferred_element_type=F32)
        _seg_scan(gr, gi, coef_r, gcr, gci, nrows=tb, ns=ns, reverse=True,
                  dab=(spr, spi, ser, sei, dabr_ref, dabi_ref))

        for s in range(ntl):
            sl_c, sl_s = slice(s * ct, (s + 1) * ct), slice(s * st, (s + 1) * st)
            s_re = spr[:, sl_s].astype(BF16)
            s_im = spi[:, sl_s].astype(BF16)
            g_re, g_im = gr[:, sl_s].astype(BF16), gi[:, sl_s].astype(BF16)
            dys, us = dyb[:, sl_c], ub[:, sl_c]
            acdr[s] += lax.dot_general(s_re, dys, (TN, ((), ())), preferred_element_type=F32)
            acdi[s] -= lax.dot_general(s_im, dys, (TN, ((), ())), preferred_element_type=F32)
            abdr[s] += lax.dot_general(us, g_re, (TN, ((), ())), preferred_element_type=F32)
            abdi[s] += lax.dot_general(us, g_im, (TN, ((), ())), preferred_element_type=F32)
            dup[:, sl_c] = (lax.dot_general(g_re, wbdr[s], (NT, ((), ())), preferred_element_type=F32)
                            + lax.dot_general(g_im, wbdi[s], (NT, ((), ())), preferred_element_type=F32))
        dd_ref[...] += jnp.sum(dy * u, axis=0, keepdims=True)
        du = _time_order_rows(pm_ref[...], dup[...], 2) + d_ref[...] * dy
        dproj_ref[:, 0:P] = dpi_ref[...]
        dproj_ref[:, P:2 * P] = dpg_ref[...]
        dproj_ref[:, 2 * P:3 * P] = du.astype(BF16)
        dproj_ref[:, 3 * P:4 * P] = dsg_ref[...]

        @pl.when(i == nb - 1)
        def _():
            for a, h in ((abdr, dbdr_h), (abdi, dbdi_h), (acdr, dcdr_h), (acdi, dcdi_h)):
                pltpu.sync_copy(a, h)

    rev = lambda i: nb - 1 - i
    vec = lambda n: _bs((1, n), lambda i: (0, 0))
    row = _bs((tb, P), lambda i: (rev(i), 0))
    st_spec = _bs((None, SUBLANES, ns), lambda i: (rev(i), 0, 0))
    bshape = jax.ShapeDtypeStruct(bdr.shape, F32)
    cshape = jax.ShapeDtypeStruct(cdr.shape, F32)
    body, extra = _hosted(core, comm, (nb,), 16, 8, 21)
    return _call(body, grid=(nb,),
                 in_specs=[_bs((tb, P), lambda i: (rev(i), 2)), row, row, st_spec, st_spec,
                           vec(ns), vec(ns), vec(P), _bs((tb, tb), lambda i: (0, 0)), row, row, row,
                           ANY, ANY, ANY, ANY] + extra["in_specs"],
                 out_specs=[_bs((tb, 4 * P), lambda i: (rev(i), 0)), vec(ns), vec(ns), vec(P), ANY, ANY, ANY, ANY]
                 + extra["out_specs"],
                 out_shape=[jax.ShapeDtypeStruct((T, 4 * P), BF16), jax.ShapeDtypeStruct((1, ns), F32),
                            jax.ShapeDtypeStruct((1, ns), F32), jax.ShapeDtypeStruct((1, P), F32),
                            bshape, bshape, cshape, cshape] + extra["out_shape"],
                 scratch_shapes=[pltpu.VMEM(bdr.shape, BF16), pltpu.VMEM(bdr.shape, BF16),
                                 pltpu.VMEM(cdr.shape, BF16), pltpu.VMEM(cdr.shape, BF16),
                                 pltpu.VMEM(bdr.shape, F32), pltpu.VMEM(bdr.shape, F32),
                                 pltpu.VMEM(cdr.shape, F32), pltpu.VMEM(cdr.shape, F32),
                                 pltpu.VMEM((tb, ns), F32), pltpu.VMEM((tb, ns), F32),
                                 pltpu.VMEM((tb, ns), F32), pltpu.VMEM((tb, ns), F32),
                                 pltpu.VMEM((10, SUBLANES, ns), F32), pltpu.VMEM((10, SUBLANES, ns), F32)]
                 + [pltpu.VMEM((SUBLANES, ns), F32)] * 6 + [pltpu.VMEM((tb, P), F32)] + extra["scratch"],
                 compiler_params=_cp(("arbitrary",)), name="ssm_bwd")(
                     proj, y, dge, bsr, bsi, abr, abi, dsk, _seg_perm_matrix(tb), dpi, dpg, dsg,
                     bdr, bdi, cdr, cdi, *extra["ins"])


def _adamw(w, g, m, v, name, comm=None):
    R, C = w.shape
    tr = _t(R, ROWS_ELEMENTWISE)

    def core(w_ref, g_ref, m_ref, v_ref, d_ref, mo_ref, vo_ref):
        gv = g_ref[...]
        mn = ADAM_B1 * m_ref[...] + (1.0 - ADAM_B1) * gv
        vn = ADAM_B2 * v_ref[...] + (1.0 - ADAM_B2) * (gv * gv)
        m_hat = mn / (1.0 - ADAM_B1 ** ADAM_STEP)
        v_hat = vn / (1.0 - ADAM_B2 ** ADAM_STEP)
        d_ref[...] = -ADAM_LR * (m_hat / (jnp.sqrt(v_hat) + ADAM_EPS) + ADAM_WD * w_ref[...])
        mo_ref[...] = mn
        vo_ref[...] = vn

    blk = _bs((tr, C), lambda i: (i, 0))
    shp = jax.ShapeDtypeStruct((R, C), F32)
    body, extra = _hosted(core, comm, (R // tr,), 4, 3, 0)
    return _call(body, grid=(R // tr,), in_specs=[blk] * 4 + extra["in_specs"],
                 out_specs=[blk] * 3 + extra["out_specs"], out_shape=[shp] * 3 + extra["out_shape"],
                 scratch_shapes=extra["scratch"],
                 compiler_params=_cp(("arbitrary",) if comm else ("parallel",)), name=name)(w, g, m, v, *extra["ins"])


def _sum_cast(grad, got, place, name):
    J, H, C = got.shape
    tr = _t(H, ROWS_ELEMENTWISE)
    nb = H // tr

    def body(pl_ref, a_ref, b_ref, o_ref):
        o_ref[...] = (a_ref[...] + b_ref[...]).astype(BF16)

    blk = _bs((None, tr, C), lambda j, i, pc: (j, i, 0))
    mine = _bs((None, tr, C), lambda j, i, pc: (j, pc[1] * nb + i, 0))
    spec = pltpu.PrefetchScalarGridSpec(num_scalar_prefetch=1, grid=(J, nb), in_specs=[mine, blk], out_specs=blk)
    return _call(body, grid_spec=spec, out_shape=jax.ShapeDtypeStruct((J, H, C), BF16),
                 compiler_params=_cp(("parallel", "parallel")), name=name)(place, grad, got)


def _sum_chips(sent, arrived, place, name):
    J, H, C = arrived.shape
    tr = _t(H, ROWS_ELEMENTWISE)
    nb = H // tr

    def body(pl_ref, own_ref, a0_ref, a1_ref, a2_ref, o_ref):
        acc = own_ref[...].astype(F32)
        for r in (a0_ref, a1_ref, a2_ref):
            acc = acc + r[...].astype(F32)
        o_ref[...] = acc

    def other(k):
        return _bs((None, tr, C), lambda i, pc: (jnp.where(pc[0] <= k, k + 1, k), i, 0))

    spec = pltpu.PrefetchScalarGridSpec(
        num_scalar_prefetch=1, grid=(nb,),
        in_specs=[_bs((None, tr, C), lambda i, pc: (pc[0], i, 0)), other(0), other(1), other(2)],
        out_specs=_bs((tr, C), lambda i, pc: (pc[1] * nb + i, 0)))
    return _call(body, grid_spec=spec, out_shape=jax.ShapeDtypeStruct((2 * H, C), F32),
                 compiler_params=_cp(("parallel",)), name=name)(place, sent, arrived, arrived, arrived)


def _place():
    x, y, c = lax.axis_index("x"), lax.axis_index("y"), lax.axis_index("c")
    chips = [(1 - x, y), (x, 1 - y), (1 - x, 1 - y)]
    return x, y, c, chips


def _split(nrows, row_bytes, align, cap=None):
    k = max(1, min(cap or DMA_MAX_CHUNKS, (nrows * row_bytes) // DMA_CHUNK_BYTES))
    while k > 1 and nrows % (k * align):
        k -= 1
    return k


def _comm_call(plan, name):
    n_in, n_out = len(plan["ins"]), len(plan["out_shape"])

    def body(*refs):
        for phase in plan["phases"]:
            phase(refs[:n_in], refs[n_in:n_in + n_out], refs[n_in + n_out:])

    return _call(body, in_specs=[ANY] * n_in, out_specs=[ANY] * n_out, out_shape=plan["out_shape"],
                 input_output_aliases={i: i for i in range(n_out)} if plan.get("alias") else {},
                 scratch_shapes=plan["scratch"], name=name)(*plan["ins"])


def _comm_hooks(plan, grid, ins, outs, sems, *, before):
    if plan is None:
        return
    nsteps, step = 1, 0
    for d, g in enumerate(grid):
        nsteps, step = nsteps * g, step * g + pl.program_id(d)
    for p, (phase, frac) in enumerate(zip(plan["phases"], plan["at"])):
        if (p == 0) == before:
            pl.when(step == int(frac * (nsteps - 1)))(functools.partial(phase, ins, outs, sems))


def _ag_plan(shards, axes):
    n = len(shards)
    shapes = [a.shape for a in shards]

    def window(ref, i, chip, half=None):
        S, ax = shapes[i], axes[i]
        idx = []
        for d in range(len(S)):
            off, size = 0, S[d]
            if d == 0 and half is not None:
                off, size = half * (S[0] // 2), S[0] // 2
            if d == ax:
                off = off + chip * S[ax]
            idx.append(pl.ds(off, size))
        return ref.at[tuple(idx)]

    def copies(src, full, sems):
        ssem, rsem = sems
        x, y, c, chips = _place()
        me = 2 * x + y
        sib = (x, y, 1 - c)
        idx = [2 * cx + cy for cx, cy in chips]

        def rcopy(i, k, s_ref, d_ref, to):
            return pltpu.make_async_remote_copy(src_ref=s_ref, dst_ref=d_ref, send_sem=ssem.at[i, k],
                                                recv_sem=rsem.at[i, k], device_id=to, device_id_type=MESH)

        def ici(i, j, incoming):
            half_src = src[i].at[pl.ds(c * (shapes[i][0] // 2), shapes[i][0] // 2)]
            return rcopy(i, j, half_src, window(full[i], i, idx[j] if incoming else me, c), (*chips[j], c))

        def fwd(i, j, half):
            w = window(full[i], i, idx[j], half)
            return rcopy(i, 3 + j, w, w, sib)

        def own(i):
            return rcopy(i, 6, src[i], window(full[i], i, me), sib)

        return c, ici, fwd, own

    def send(src, full, sems):
        c, ici, fwd, own = copies(src, full, sems)
        for i in range(n):
            for j in range(3):
                ici(i, j, False).start()
        for i in range(n):
            own(i).start()

    def forward(i, src, full, sems):
        c, ici, fwd, own = copies(src, full, sems)
        for j in range(3):
            ici(i, j, True).wait_recv()
            fwd(i, j, c).start()

    def finish(src, full, sems):
        c, ici, fwd, own = copies(src, full, sems)
        for i in range(n):
            for j in range(3):
                fwd(i, j, 1 - c).wait_recv()
            own(i).wait()
        for i in range(n):
            for j in range(3):
                ici(i, j, False).wait_send()
                fwd(i, j, c).wait_send()

    out_shape = [jax.ShapeDtypeStruct(tuple(N_CHIP * d if k == ax else d for k, d in enumerate(S)), BF16)
                 for S, ax in zip(shapes, axes)]
    sizes = [a.size for a in shards]
    behind = [AG_FORWARD_SCALE * sum(sizes[:i + 1]) / sum(sizes) + AG_FORWARD_LAG for i in range(n)]
    return dict(ins=list(shards), out_shape=out_shape,
                phases=[send] + [functools.partial(forward, i) for i in range(n)] + [finish],
                at=[0.0] + behind + [1.0],
                scratch=[pltpu.SemaphoreType.DMA((n, 7)), pltpu.SemaphoreType.DMA((n, 7))])


def _proj_ag(x, g1, wsh, order, tm):
    T, D = x.shape
    P = wsh.shape[1]
    H = D // 2
    nt = T // tm

    def body(order_ref, x_ref, g_ref, wsh_ref, hn_ref, proj_ref, win_ref, wbuf, lsem, ssem, rsem):
        n, i = pl.program_id(0), pl.program_id(1)
        x, y, c, chips = _place()
        me = 2 * x + y
        sib = (x, y, 1 - c)
        idx = [2 * cx + cy for cx, cy in chips]

        def rcopy(k, s_ref, d_ref, to):
            return pltpu.make_async_remote_copy(src_ref=s_ref, dst_ref=d_ref, send_sem=ssem.at[k],
                                                recv_sem=rsem.at[k], device_id=to, device_id_type=MESH)

        def cols(chip):
            return pl.ds(pl.multiple_of(chip * P, LANES), P)

        def rows(half):
            return pl.ds(pl.multiple_of(half * H, BF16_TILE_ROWS), H)

        def ici(j, incoming):
            return rcopy(j, wsh_ref.at[rows(c)], win_ref.at[rows(c), cols(idx[j] if incoming else me)],
                         (*chips[j], c))

        def fwd(j, half):
            w = win_ref.at[rows(half), cols(idx[j])]
            return rcopy(3 + j, w, w, sib)

        def own():
            return rcopy(6, wsh_ref, win_ref.at[:, cols(me)], sib)

        def load(src):
            cp = pltpu.make_async_copy(src, wbuf, lsem)
            cp.start()
            cp.wait()

        @pl.when((n == 0) & (i == 0))
        def _():
            ici(0, False).start()
            ici(1, False).start()
            own().start()
            load(wsh_ref)

        for j in range(3):
            @pl.when((n == j + 1) & (i == 0))
            def _(j=j):
                if j == 0:
                    ici(2, False).start()
                ici(j, True).wait_recv()
                fwd(j, c).start()
                fwd(j, 1 - c).wait_recv()
                load(win_ref.at[:, cols(idx[j])])

        xv = x_ref[...]
        r = lax.rsqrt(jnp.mean(xv * xv, axis=-1, keepdims=True) + EPS)
        hn = ((xv * r) * g_ref[...]).astype(BF16)

        @pl.when(n == 0)
        def _():
            hn_ref[...] = hn

        proj_ref[...] = jnp.dot(hn, wbuf[...], preferred_element_type=F32)

        @pl.when((n == 3) & (i == nt - 1))
        def _():
            own().wait()
            for j in range(3):
                ici(j, False).wait_send()
                fwd(j, c).wait_send()

    spec = pltpu.PrefetchScalarGridSpec(
        num_scalar_prefetch=1, grid=(N_CHIP, nt),
        in_specs=[_bs((tm, D), lambda n, i, o: (i, 0)), _bs((1, D), lambda n, i, o: (0, 0)), ANY],
        out_specs=[_bs((tm, D), lambda n, i, o: (jnp.where(n == 0, i, nt - 1), 0)),
                   _bs((tm, P), lambda n, i, o: (i, o[n])), ANY],
        scratch_shapes=[pltpu.VMEM((D, P), BF16), pltpu.SemaphoreType.DMA,
                        pltpu.SemaphoreType.DMA((7,)), pltpu.SemaphoreType.DMA((7,))])
    return _call(body, grid_spec=spec,
                 out_shape=[jax.ShapeDtypeStruct((T, D), BF16), jax.ShapeDtypeStruct((T, N_CHIP * P), F32),
                            jax.ShapeDtypeStruct((D, N_CHIP * P), BF16)],
                 compiler_params=_cp(("arbitrary", "arbitrary")), name="proj_ag")(order, x, g1, wsh)


def _halves_plan(grads):
    n = len(grads)

    def send(g, got, sems):
        ssem, rsem = sems
        x, y, c, _ = _place()
        sib = (x, y, 1 - c)
        for i in range(n):
            J, R, C = g[i].shape
            H = R // 2
            size = g[i].dtype.itemsize
            tile_rows = SUBLANES * 4 // size
            k = _split(H, C * size, tile_rows, cap=DMA_MAX_CHUNKS // J)
            hr = H // k
            for j in range(J):
                for q in range(k):
                    other = pl.ds(pl.multiple_of((1 - c) * H + q * hr, tile_rows), hr)
                    to = pl.ds(q * hr, hr)
                    pltpu.make_async_remote_copy(src_ref=g[i].at[j, other, :], dst_ref=got[i].at[j, to, :],
                                                 send_sem=ssem.at[i], recv_sem=rsem.at[i],
                                                 device_id=sib, device_id_type=MESH).start()

    def finish(g, got, sems):
        ssem, rsem = sems
        x, y, c, _ = _place()
        for i in range(n):
            pltpu.make_async_remote_copy(src_ref=got[i], dst_ref=got[i], send_sem=ssem.at[i], recv_sem=rsem.at[i],
                                         device_id=(x, y, 1 - c), device_id_type=MESH).wait()

    half = [jax.ShapeDtypeStruct((a.shape[0], a.shape[1] // 2, a.shape[2]), a.dtype) for a in grads]
    return dict(ins=list(grads), out_shape=half, phases=[send, finish], at=[0.0, 1.0],
                scratch=[pltpu.SemaphoreType.DMA((n,)), pltpu.SemaphoreType.DMA((n,))])


def _scatter_plan(parts):
    n = len(parts)

    def peers():
        x, y, c, chips = _place()
        return 2 * x + y, c, chips, [2 * cx + cy for cx, cy in chips]

    def send(s, got, sems):
        ssem, rsem = sems
        me, c, chips, idx = peers()
        for i in range(n):
            _, H, C = s[i].shape
            k = _split(H, C * 2, BF16_TILE_ROWS, cap=RS_CHUNKS)
            hr = H // k
            for q in range(k):
                rows = pl.ds(q * hr, hr)
                for j in range(3):
                    pltpu.make_async_remote_copy(src_ref=s[i].at[idx[j], rows, :], dst_ref=got[i].at[me, rows, :],
                                                 send_sem=ssem.at[i, j], recv_sem=rsem.at[i, j],
                                                 device_id=(*chips[j], c), device_id_type=MESH).start()

    def finish(s, got, sems):
        ssem, rsem = sems
        me, c, chips, idx = peers()
        for i in range(n):
            for j in range(3):
                pltpu.make_async_remote_copy(src_ref=s[i].at[idx[j]], dst_ref=got[i].at[idx[j]],
                                             send_sem=ssem.at[i, j], recv_sem=rsem.at[i, j],
                                             device_id=(*chips[j], c), device_id_type=MESH).wait()

    return dict(ins=list(parts), out_shape=[jax.ShapeDtypeStruct(a.shape, a.dtype) for a in parts],
                phases=[send, finish], at=[0.0, 1.0],
                scratch=[pltpu.SemaphoreType.DMA((n, 3)), pltpu.SemaphoreType.DMA((n, 3))])


def _join_plan(shards):
    n = len(shards)

    def send(_, full, sems):
        ssem, rsem = sems
        x, y, c, _ = _place()
        sib = (x, y, 1 - c)
        for i in range(n):
            H, C = full[i].shape[0] // 2, full[i].shape[1]
            k = _split(H, C * 4, SUBLANES)
            hr = H // k
            for q in range(k):
                rows = pl.ds(pl.multiple_of(c * H + q * hr, SUBLANES), hr)
                pltpu.make_async_remote_copy(src_ref=full[i].at[rows], dst_ref=full[i].at[rows],
                                             send_sem=ssem.at[i], recv_sem=rsem.at[i],
                                             device_id=sib, device_id_type=MESH).start()

    def finish(_, full, sems):
        ssem, rsem = sems
        x, y, c, _ = _place()
        for i in range(n):
            half = full[i].at[pl.ds(0, full[i].shape[0] // 2)]
            pltpu.make_async_remote_copy(src_ref=half, dst_ref=half, send_sem=ssem.at[i], recv_sem=rsem.at[i],
                                         device_id=(x, y, 1 - c), device_id_type=MESH).wait()

    return dict(ins=list(shards), out_shape=[jax.ShapeDtypeStruct(a.shape, a.dtype) for a in shards],
                phases=[send, finish], at=[0.0, 1.0], alias=True,
                scratch=[pltpu.SemaphoreType.DMA((n,)), pltpu.SemaphoreType.DMA((n,))])


def _allreduce_plan(buf):
    R, L = buf.shape
    RB = R // N_DEV

    def parts(sems):
        xv, got, ov, lsem, ssem, rsem = sems
        x, y, c, _ = _place()
        me = 4 * x + 2 * y + c

        def dev(k):
            return (k // 4, (k // 2) % 2, k % 2)

        def slab(k):
            return pl.ds(pl.multiple_of(k * RB, SUBLANES), RB)

        def first(d, to, landing):
            return pltpu.make_async_remote_copy(src_ref=xv.at[slab(to)], dst_ref=got.at[landing],
                                                send_sem=ssem.at[0, d], recv_sem=rsem.at[0, d],
                                                device_id=dev(to), device_id_type=MESH)

        def second(d, to, k):
            return pltpu.make_async_remote_copy(src_ref=ov.at[slab(k)], dst_ref=ov.at[slab(k)],
                                                send_sem=ssem.at[1, d], recv_sem=rsem.at[1, d],
                                                device_id=dev(to), device_id_type=MESH)

        return me, slab, first, second

    def scatter(ins, outs, sems):
        xv, lsem = sems[0], sems[3]
        me, slab, first, second = parts(sems)
        cp = pltpu.make_async_copy(ins[0], xv, lsem)
        cp.start()
        cp.wait()
        for d in range(1, N_DEV):
            first(d, (me + d) % N_DEV, me).start()

    def reduce(ins, outs, sems):
        xv, got, ov = sems[:3]
        me, slab, first, second = parts(sems)
        got[me] = xv[slab(me), :]
        for d in range(1, N_DEV):
            src = (me + N_DEV - d) % N_DEV
            first(d, src, src).wait_recv()
        acc = got[0]
        for k in range(1, N_DEV):
            acc = acc + got[k]
        ov[slab(me), :] = acc
        for d in range(1, N_DEV):
            second(d, (me + d) % N_DEV, me).start()

    def collect(ins, outs, sems):
        ov, lsem = sems[2], sems[3]
        me, slab, first, second = parts(sems)
        for d in range(1, N_DEV):
            src = (me + N_DEV - d) % N_DEV
            second(d, src, src).wait_recv()
        for d in range(1, N_DEV):
            peer = (me + d) % N_DEV
            first(d, peer, me).wait_send()
            second(d, peer, me).wait_send()
        cp = pltpu.make_async_copy(ov, outs[0], lsem)
        cp.start()
        cp.wait()

    return dict(ins=[buf], out_shape=[jax.ShapeDtypeStruct((R, L), F32)], phases=[scatter, reduce, collect],
                at=[0.0, 0.5, 1.0],
                scratch=[pltpu.VMEM((R, L), F32), pltpu.VMEM((N_DEV, RB, L), F32), pltpu.VMEM((R, L), F32),
                         pltpu.SemaphoreType.DMA, pltpu.SemaphoreType.DMA((2, N_DEV)),
                         pltpu.SemaphoreType.DMA((2, N_DEV))])


def _block_diag(t, gt):
    G, A, B = t.shape
    t4 = t.reshape(G // gt, gt, A, B)
    eye = jnp.eye(gt, dtype=t.dtype)
    return jnp.einsum('sgab,gh->sgahb', t4, eye).reshape(G // gt, gt * A, gt * B)


def _block_diag_extract(m, gt, A, B):
    S = m.shape[0]
    m5 = m.reshape(S, gt, A, gt, B)
    eye = jnp.eye(gt, dtype=m.dtype)
    return jnp.einsum('sgahb,gh->sgab', m5, eye).reshape(S * gt, A, B)


def _tile_rows(n):
    return -(-n // (SUBLANES * LANES)) * SUBLANES


def _pack_small(arrs, rows):
    parts = []
    for a in arrs:
        flat = a.reshape(-1).astype(F32)
        r = _tile_rows(flat.shape[0])
        parts.append(jnp.pad(flat, (0, r * LANES - flat.shape[0])).reshape(r, LANES))
    used = sum(p.shape[0] for p in parts)
    if rows > used:
        parts.append(jnp.zeros((rows - used, LANES), F32))
    return jnp.concatenate(parts)


def _unpack_small(buf, shapes):
    out, off = [], 0
    for s in shapes:
        n = 1
        for d in s:
            n *= d
        r = _tile_rows(n)
        piece = buf[off:off + r]
        out.append(piece.reshape(s) if n == r * LANES else piece.reshape(-1)[:n].reshape(s))
        off += r
    return out


def kernel(x, p, norm_gain, w_in, w_pool, pool_scale, a_re, a_im, log_dt, b_re, b_im, c_re, c_im, d_skip, w_glu, w_out, w_ple, w_ple_gate, final_gain, loss_target, m_norm_gain, m_w_in, m_w_pool, m_pool_scale, m_a_re, m_a_im, m_log_dt, m_b_re, m_b_im, m_c_re, m_c_im, m_d_skip, m_w_glu, m_w_out, m_w_ple, m_w_ple_gate, m_final_gain, v_norm_gain, v_w_in, v_w_pool, v_pool_scale, v_a_re, v_a_im, v_log_dt, v_b_re, v_b_im, v_c_re, v_c_im, v_d_skip, v_w_glu, v_w_out, v_w_ple, v_w_ple_gate, v_final_gain):
    xs, pe, tgt = x[0], p[0, 0], loss_target[0]
    T, D = xs.shape
    E = pe.shape[1]
    P = D // 2
    NG = len(POOL_WINDOWS)
    PG = P // NG
    G, N, C = P // SSM_GROUP, SSM_STATE, SSM_GROUP
    GT = min(SSM_TILE_GROUPS, G)
    Q = D // N_CHIP

    big = {"w_in": (w_in, m_w_in, v_w_in), "w_pool": (w_pool, m_w_pool, v_w_pool),
           "w_glu": (w_glu, m_w_glu, v_w_glu), "w_out": (w_out, m_w_out, v_w_out),
           "w_ple": (w_ple, m_w_ple, v_w_ple), "w_ple_gate": (w_ple_gate, m_w_ple_gate, v_w_ple_gate)}
    big_names = list(big)
    shard2d = {n: (big[n][0].size // big[n][0].shape[-1], big[n][0].shape[-1]) for n in big_names}
    shard_axis = {"w_in": 1, "w_pool": 1, "w_glu": 1, "w_out": 0, "w_ple": 1, "w_ple_gate": 0}
    shard16 = {n: big[n][0][0].astype(BF16) for n in big_names}
    place = jnp.stack([2 * lax.axis_index("x") + lax.axis_index("y"), lax.axis_index("c")]).astype(jnp.int32)
    mx, my = lax.axis_index("x"), lax.axis_index("y")
    block_order = jnp.stack([2 * mx + my, 2 * (1 - mx) + my, 2 * mx + (1 - my),
                             2 * (1 - mx) + (1 - my)]).astype(jnp.int32)
    later = [n for n in big_names if n != "w_in"]
    ag_later = _ag_plan([shard16[n] for n in later], [shard_axis[n] for n in later])

    rep = lambda a: jnp.repeat(a, C, axis=0)
    a_re_r, a_im_r = rep(a_re[0]), rep(a_im[0])
    ldt_r = rep(jnp.broadcast_to(log_dt[0][:, None], (G, N)))
    bt_re = b_re[0].transpose(0, 2, 1).reshape(G * C, N)
    bt_im = b_im[0].transpose(0, 2, 1).reshape(G * C, N)
    ab_re_r, ab_im_r, bbt_re, bbt_im = _ssm_prep(a_re_r, a_im_r, ldt_r, bt_re, bt_im)
    abr = ab_re_r[::C].reshape(1, G * N)
    abi = ab_im_r[::C].reshape(1, G * N)
    bdr = _block_diag(bbt_re.reshape(G, C, N), GT).astype(BF16)
    bdi = _block_diag(bbt_im.reshape(G, C, N), GT).astype(BF16)
    cdr = _block_diag(c_re[0].transpose(0, 2, 1), GT).astype(BF16)
    cdi = _block_diag(c_im[0].transpose(0, 2, 1), GT).astype(BF16)

    tb = _t(T, ROWS_ELEMENTWISE)
    tbs = _t(T, ROWS_SSM)
    tm = _t(T, ROWS_MATMUL)
    tk = _t(T, DEPTH_MATMUL)
    DH = _t(D, ROWS_MATMUL)
    row_k = lambda i, n, k: (i, k)
    row_n = lambda i, n, k: (i, n)
    f32 = lambda *shape: jax.ShapeDtypeStruct(shape, F32)
    hn, proj, win = _proj_ag(xs, norm_gain, shard16["w_in"], block_order, tm)
    y, ge, bsr, bsi, wp, wglu, wout, wple, wpg = _ssm_fwd(proj, bdr, bdi, cdr, cdi, abr, abi, d_skip, P, tbs,
                                                          comm=ag_later)
    pooled, mixed = _pool_fwd(proj, wp, P, tb)
    hg = _mm(ge, wglu, dims=NN, grid=(T // tm, 1, 1),
             a_spec=_bs((tm, P), row_k), b_spec=_bs((P, 2 * P), lambda i, n, k: (k, n)),
             o_spec=_bs((tm, 2 * P), row_n), out_shape=f32(T, 2 * P), name="mm_glu")
    cat = _gate_fwd(mixed, proj, hg, pool_scale, tb)
    h1, h1b = _mm(cat, wout, dims=NN, grid=(T // tm, D // DH, 1), res=xs, bf16_copy=True,
                  a_spec=_bs((tm, D), row_k), b_spec=_bs((D, DH), lambda i, n, k: (k, n)),
                  r_spec=_bs((tm, DH), row_n), o_spec=_bs((tm, DH), row_n), out_shape=f32(T, D), name="mm_out")
    z = _mm(h1b, wpg, dims=NN, grid=(T // tm, 1, 1),
            a_spec=_bs((tm, D), row_k), b_spec=_bs((D, D), lambda i, n, k: (k, n)),
            o_spec=_bs((tm, D), row_n), out_shape=f32(T, D), name="mm_pgate")
    dh2, dz, dg2, lpart, g_wple, g_wple16 = _final_fb(h1, pe, wple, z, tgt, final_gain.reshape(1, D), tb)

    col_m = lambda m, n, k: (k, m)
    col_n = lambda m, n, k: (k, n)
    dh1, dh1b = _mm(dz, wpg, dims=NT, grid=(T // tm, D // DH, 1), res=dh2, bf16_copy=True,
                    a_spec=_bs((tm, D), row_k), b_spec=_bs((DH, D), lambda i, n, k: (n, k)),
                    r_spec=_bs((tm, DH), row_n), o_spec=_bs((tm, DH), row_n), out_shape=f32(T, D), name="mm_dh1")
    g_wpg, g_wpg16 = _mm(h1b, dz, dims=TN, grid=(D // DH, D // DH, T // tk), bf16_copy=True,
                         a_spec=_bs((tk, DH), col_m), b_spec=_bs((tk, DH), col_n),
                         o_spec=_bs((DH, DH), lambda m, n, k: (m, n)), out_shape=f32(D, D), name="mm_gwpg")
    dcat = _mm(dh1b, wout, dims=NT, grid=(T // tm, 1, 1),
               a_spec=_bs((tm, D), row_k), b_spec=_bs((D, D), lambda i, n, k: (n, k)),
               o_spec=_bs((tm, D), row_n), out_shape=f32(T, D), name="mm_dcat")
    g_wout, g_wout16 = _mm(cat, dh1b, dims=TN, grid=(D // DH, D // DH, T // tk), bf16_copy=True,
                           a_spec=_bs((tk, DH), col_m), b_spec=_bs((tk, DH), col_n),
                           o_spec=_bs((DH, DH), lambda m, n, k: (m, n)), out_shape=f32(D, D), name="mm_gwout")
    gbig = {"w_out": g_wout.reshape(N_CHIP, Q, D), "w_ple": g_wple, "w_ple_gate": g_wpg.reshape(N_CHIP, Q, D)}
    gbig16 = {"w_out": g_wout16.reshape(N_CHIP, Q, D), "w_ple": g_wple16,
              "w_ple_gate": g_wpg16.reshape(N_CHIP, Q, D)}
    first = list(gbig)
    res = _gate_bwd(dcat, mixed, proj, hg, pool_scale, tb, comm=_halves_plan([gbig16[n] for n in first]))
    dmixed, dpg, dsg, dhg, dps = res[:5]
    got = dict(zip(first, res[5:]))
    dge = _mm(dhg, wglu, dims=NT, grid=(T // tm, 1, 1),
              a_spec=_bs((tm, 2 * P), row_k), b_spec=_bs((P, 2 * P), lambda i, n, k: (n, k)),
              o_spec=_bs((tm, P), row_n), out_shape=f32(T, P), name="mm_dge")
    gbig["w_glu"], gbig16["w_glu"] = _mm(ge, dhg, dims=TN, grid=(1, N_CHIP, T // tk), bf16_copy=True,
                                         a_spec=_bs((tk, P), col_m), b_spec=_bs((tk, Q), col_n),
                                         o_spec=_bs((None, P, Q), lambda m, j, k: (j, 0, 0)),
                                         out_shape=f32(N_CHIP, P, Q), name="mm_gwglu")
    g_wp = _mm(pooled, dmixed, dims=TN, grid=(NG, 1, T // tk), bf16_copy=True,
               a_spec=_bs((tk, PG), col_m), b_spec=_bs((tk, PG), col_m),
               o_spec=_bs((None, PG, PG), lambda g, n, k: (g, 0, 0)), out_shape=f32(NG, PG, PG), name="mm_gwp")
    by_chip = lambda a: a.reshape(NG, N_CHIP, PG // N_CHIP, PG).transpose(1, 0, 2, 3).reshape(
        N_CHIP, NG * PG // N_CHIP, PG)
    gbig["w_pool"], gbig16["w_pool"] = by_chip(g_wp[0]), by_chip(g_wp[1])
    res = _pool_bwd(dmixed, wp, tb, comm=_halves_plan([gbig16["w_pool"], gbig16["w_glu"]]))
    dpi, got["w_pool"], got["w_glu"] = res
    early = list(gbig)
    chip_sums = {n: _sum_cast(gbig[n], got[n], place, "sum_cast_" + n) for n in early}
    res = _ssm_bwd(proj, y, dge, bsr, bsi, bdr, bdi, cdr, cdi, abr, abi, d_skip, dpi, dpg, dsg, P, tbs,
                   comm=_scatter_plan([chip_sums[n] for n in early]))
    dproj, dabr, dabi, dd, dbdr, dbdi, dcdr, dcdi = res[:8]
    arrived = dict(zip(early, res[8:]))
    halves = [_sum_chips(chip_sums[n], arrived[n], place, "sum_chips_" + n) for n in early]
    res = _mm(hn, dproj, dims=TN, grid=(D // DH, N_CHIP, T // tk), bf16_copy=True,
              a_spec=_bs((tk, DH), col_m), b_spec=_bs((tk, P), col_n),
              o_spec=_bs((None, DH, P), lambda m, j, k: (j, m, 0)), out_shape=f32(N_CHIP, D, P),
              name="mm_gwin", comm=_join_plan(halves))
    gbig["w_in"], gbig16["w_in"], gshard = res[0], res[1], dict(zip(early, res[2:]))
    got["w_in"], = _comm_call(_halves_plan([gbig16["w_in"]]), "rs_halves_late")
    chip_sums["w_in"] = _sum_cast(gbig["w_in"], got["w_in"], place, "sum_cast_w_in")
    KH = _t(4 * P, DEPTH_MATMUL)
    dhn, arrived["w_in"] = _mm(dproj, win, dims=NT, grid=(T // tm, D // DH, 4 * P // KH),
                               a_spec=_bs((tm, KH), row_k), b_spec=_bs((DH, KH), lambda i, n, k: (n, k)),
                               o_spec=_bs((tm, DH), row_n), out_shape=f32(T, D), name="mm_dhn",
                               comm=_scatter_plan([chip_sums["w_in"]]))
    gshard["w_in"], = _comm_call(
        _join_plan([_sum_chips(chip_sums["w_in"], arrived["w_in"], place, "sum_chips_w_in")]), "rs_join_w_in")
    grad_x, dg1 = _norm1_bwd(xs, dhn, dh1, norm_gain, tb)

    dbbt_re = _block_diag_extract(dbdr, GT, C, N).reshape(G * C, N)
    dbbt_im = _block_diag_extract(dbdi, GT, C, N).reshape(G * C, N)
    g_c_re = _block_diag_extract(dcdr, GT, N, C).transpose(0, 2, 1)
    g_c_im = _block_diag_extract(dcdi, GT, N, C).transpose(0, 2, 1)
    dab_re_r = rep(dabr.reshape(G, N)) * (1.0 / C)
    dab_im_r = rep(dabi.reshape(G, N)) * (1.0 / C)
    g_a_re, g_a_im, g_ldt, g_bt_re, g_bt_im = _ssm_prep_bwd(a_re_r, a_im_r, ldt_r, bt_re, bt_im,
                                                            dab_re_r, dab_im_r, dbbt_re, dbbt_im, G)
    g_b_re = g_bt_re.reshape(G, C, N).transpose(0, 2, 1)
    g_b_im = g_bt_im.reshape(G, C, N).transpose(0, 2, 1)


    small_names = ["norm_gain", "pool_scale", "a_re", "a_im", "log_dt", "b_re", "b_im", "c_re", "c_im",
                   "d_skip", "final_gain"]
    small_w = dict(norm_gain=norm_gain, pool_scale=pool_scale, a_re=a_re, a_im=a_im, log_dt=log_dt, b_re=b_re,
                   b_im=b_im, c_re=c_re, c_im=c_im, d_skip=d_skip, final_gain=final_gain)
    small_m = dict(norm_gain=m_norm_gain, pool_scale=m_pool_scale, a_re=m_a_re, a_im=m_a_im, log_dt=m_log_dt,
                   b_re=m_b_re, b_im=m_b_im, c_re=m_c_re, c_im=m_c_im, d_skip=m_d_skip, final_gain=m_final_gain)
    small_v = dict(norm_gain=v_norm_gain, pool_scale=v_pool_scale, a_re=v_a_re, a_im=v_a_im, log_dt=v_log_dt,
                   b_re=v_b_re, b_im=v_b_im, c_re=v_c_re, c_im=v_c_im, d_skip=v_d_skip, final_gain=v_final_gain)
    small_g = dict(norm_gain=dg1, pool_scale=dps, a_re=g_a_re, a_im=g_a_im, log_dt=g_ldt, b_re=g_b_re,
                   b_im=g_b_im, c_re=g_c_re, c_im=g_c_im, d_skip=dd, final_gain=dg2)
    shapes = [small_w[n].shape for n in small_names]
    loss_row = sum(_tile_rows(small_w[n].size) for n in small_names)
    unit = N_DEV * SUBLANES
    rows = -(-(loss_row + SUBLANES) // unit) * unit
    gbuf = _pack_small([small_g[n] for n in small_names] + [lpart[0, :1]], rows)
    gsum, = _comm_call(_allreduce_plan(gbuf), "allreduce_small")
    g_out, d_out, m_out, v_out = {}, {}, {}, {}
    for n in big_names:
        w_, m_, v_ = big[n]
        r2 = shard2d[n]
        res = _adamw(w_.reshape(r2), gshard[n], m_.reshape(r2), v_.reshape(r2), "adamw_" + n)
        g_out[n], d_out[n], m_out[n], v_out[n] = (a.reshape(w_.shape) for a in (gshard[n], *res))
    wbuf = _pack_small([small_w[n] for n in small_names], rows)
    mbuf = _pack_small([small_m[n] for n in small_names], rows)
    vbuf = _pack_small([small_v[n] for n in small_names], rows)
    dsm, msm, vsm = _adamw(wbuf, gsum, mbuf, vbuf, "adamw_small")
    g_small = dict(zip(small_names, _unpack_small(gsum, shapes)))
    d_small = dict(zip(small_names, _unpack_small(dsm, shapes)))
    m_small = dict(zip(small_names, _unpack_small(msm, shapes)))
    v_small = dict(zip(small_names, _unpack_small(vsm, shapes)))
    loss = gsum[loss_row, 0]

    g_out.update(g_small)
    d_out.update(d_small)
    m_out.update(m_small)
    v_out.update(v_small)

    order = ["norm_gain", "w_in", "w_pool", "pool_scale", "a_re", "a_im", "log_dt", "b_re", "b_im", "c_re",
             "c_im", "d_skip", "w_glu", "w_out", "w_ple", "w_ple_gate", "final_gain"]
    return (loss, grad_x[None], *[g_out[n] for n in order], *[d_out[n] for n in order],
            *[m_out[n] for n in order], *[v_out[n] for n in order])
```
